```python
import math
import jax, jax.numpy as jnp
from jax import lax
import numpy as np

D_MODEL = 2048
BATCH = 8
SEQ = 2048
DEPTH = 1

SSM_WIDTH = D_MODEL // 2
SSM_GROUP = 16
SSM_GROUPS = SSM_WIDTH // SSM_GROUP
SSM_STATE = 64
DT_MIN = 0.001
DT_MAX = 0.1
POOL_WIDTH = D_MODEL // 2
POOL_WINDOWS = (2, 4, 8, 16)
POOL_GROUP = POOL_WIDTH // len(POOL_WINDOWS)
N_BRANCH = 2
IN_WIDTH = SSM_WIDTH + POOL_WIDTH + N_BRANCH * D_MODEL
D_FF = 4 * D_MODEL
N_MOD = 6
LN_EPS = 1e-5
ALPHA = (2.0 * DEPTH) ** 0.25
BETA = (8.0 * DEPTH) ** -0.25

kernel_name = 'hybrid_s5_pool_gated_postnorm_block'


def _layernorm(x, gain=None, bias=None):
    xf = x.astype(jnp.float32)
    mu = jnp.mean(xf, axis=-1, keepdims=True)
    var = jnp.mean(jnp.square(xf - mu), axis=-1, keepdims=True)
    y = (xf - mu) * lax.rsqrt(var + LN_EPS)
    if gain is not None:
        y = y * gain.astype(jnp.float32) + bias.astype(jnp.float32)
    return y.astype(x.dtype)


def _modulate(x, shift, scale):
    return _layernorm(x) * (1.0 + scale) + shift


def _s5_branch(u, lam_re, lam_im, log_dt, b_re, b_im, c_re, c_im, d_skip, w_val, w_gate):
    f32 = jnp.float32
    bsz, seq, _ = u.shape
    uf = u.astype(f32).reshape(bsz, seq, SSM_GROUPS, SSM_GROUP)
    lr = lam_re.astype(f32)
    li = lam_im.astype(f32)
    dt = jnp.exp(log_dt.astype(f32))[:, None]
    mag = jnp.exp(lr * dt)
    ang = li * dt
    ab_re = mag * jnp.cos(ang)
    ab_im = mag * jnp.sin(ang)
    num_re = ab_re - 1.0
    num_im = ab_im
    den = lr * lr + li * li
    f_re = (num_re * lr + num_im * li) / den
    f_im = (num_im * lr - num_re * li) / den
    br = b_re.astype(f32)
    bi = b_im.astype(f32)
    bb_re = f_re[..., None] * br - f_im[..., None] * bi
    bb_im = f_re[..., None] * bi + f_im[..., None] * br
    bu_re = jnp.einsum('bsgh,gph->bsgp', uf, bb_re)
    bu_im = jnp.einsum('bsgh,gph->bsgp', uf, bb_im)
    a_re = jnp.broadcast_to(ab_re, bu_re.shape)
    a_im = jnp.broadcast_to(ab_im, bu_im.shape)

    def combine(e1, e2):
        a1r, a1i, b1r, b1i = e1
        a2r, a2i, b2r, b2i = e2
        return (a2r * a1r - a2i * a1i,
                a2r * a1i + a2i * a1r,
                a2r * b1r - a2i * b1i + b2r,
                a2r * b1i + a2i * b1r + b2i)

    _, _, xs_re, xs_im = lax.associative_scan(combine, (a_re, a_im, bu_re, bu_im), axis=1)
    y = (jnp.einsum('bsgp,ghp->bsgh', xs_re, c_re.astype(f32))
         - jnp.einsum('bsgp,ghp->bsgh', xs_im, c_im.astype(f32))
         + d_skip.astype(f32).reshape(SSM_GROUPS, SSM_GROUP) * uf)
    y = y.reshape(bsz, seq, SSM_WIDTH).astype(u.dtype)
    z = jax.nn.gelu(y)
    return (z @ w_val) * jax.nn.sigmoid(z @ w_gate)


def _pool_branch(u, w_pool, pool_scale, w_pool_out):
    f32 = jnp.float32
    seq = u.shape[1]
    uf = u.astype(f32)
    cs = jnp.pad(jnp.cumsum(uf, axis=1), ((0, 0), (1, 0), (0, 0)))
    pos = jnp.arange(seq, dtype=f32)
    outs = []
    for gi, w in enumerate(POOL_WINDOWS):
        lo_c, hi_c = gi * POOL_GROUP, (gi + 1) * POOL_GROUP
        csg = cs[..., lo_c:hi_c]
        hi = csg[:, 1:]
        lo = jnp.pad(csg, ((0, 0), (w - 1, 0), (0, 0)))[:, :seq]
        count = jnp.minimum(pos + 1.0, float(w))[None, :, None]
        pooled = (hi - lo) / count - uf[..., lo_c:hi_c]
        outs.append(jnp.einsum('bsc,cd->bsd', pooled, w_pool[gi].astype(f32)))
    y = jnp.concatenate(outs, axis=-1) * pool_scale.astype(f32)
    return y.astype(u.dtype) @ w_pool_out


def _hybrid_mixer(h, w_in, lam_re, lam_im, log_dt, b_re, b_im, c_re, c_im, d_skip,
                  w_glu_val, w_glu_gate, w_pool, pool_scale, w_pool_out, w_out):
    proj = h @ w_in
    u_ssm = proj[..., :SSM_WIDTH]
    u_pool = proj[..., SSM_WIDTH:SSM_WIDTH + POOL_WIDTH]
    g_a = proj[..., SSM_WIDTH + POOL_WIDTH:SSM_WIDTH + POOL_WIDTH + D_MODEL]
    g_b = proj[..., SSM_WIDTH + POOL_WIDTH + D_MODEL:]
    y_a = _s5_branch(u_ssm, lam_re, lam_im, log_dt, b_re, b_im, c_re, c_im, d_skip,
                     w_glu_val, w_glu_gate)
    y_b = _pool_branch(u_pool, w_pool, pool_scale, w_pool_out)
    merged = jax.nn.sigmoid(g_a) * y_a + jax.nn.sigmoid(g_b) * y_b
    return merged @ w_out


def _sq_relu_mlp(h, w_ff1, w_ff2):
    return jnp.square(jax.nn.relu(h @ w_ff1)) @ w_ff2


def _fwd_setup_inputs(seed: int = 0) -> dict:
    key = jax.random.key(seed)
    ks = jax.random.split(key, 26)
    f32 = jnp.float32
    L = DEPTH

    def nrm(k, shape, scale):
        return jax.random.normal(k, shape, f32) * scale

    n_idx = jnp.arange(SSM_STATE, dtype=f32)
    return {
        'x': nrm(ks[0], (BATCH, SEQ, D_MODEL), 1.0),
        'c': nrm(ks[1], (BATCH, D_MODEL), 1.0),
        'w_ada': nrm(ks[2], (L, D_MODEL, N_MOD * D_MODEL), 0.5 * D_MODEL ** -0.5),
        'b_ada': nrm(ks[3], (L, N_MOD * D_MODEL), 0.02),
        'w_in': nrm(ks[4], (L, D_MODEL, IN_WIDTH), D_MODEL ** -0.5),
        'lam_re': -0.5 + nrm(ks[5], (L, SSM_GROUPS, SSM_STATE), 0.01),
        'lam_im': math.pi * n_idx + nrm(ks[6], (L, SSM_GROUPS, SSM_STATE), 0.01),
        'log_dt': jax.random.uniform(ks[7], (L, SSM_GROUPS), f32,
                                     math.log(DT_MIN), math.log(DT_MAX)),
        'ssm_b_re': nrm(ks[8], (L, SSM_GROUPS, SSM_STATE, SSM_GROUP), (2.0 * SSM_GROUP) ** -0.5),
        'ssm_b_im': nrm(ks[9], (L, SSM_GROUPS, SSM_STATE, SSM_GROUP), (2.0 * SSM_GROUP) ** -0.5),
        'ssm_c_re': nrm(ks[10], (L, SSM_GROUPS, SSM_GROUP, SSM_STATE), (2.0 * SSM_STATE) ** -0.5),
        'ssm_c_im': nrm(ks[11], (L, SSM_GROUPS, SSM_GROUP, SSM_STATE), (2.0 * SSM_STATE) ** -0.5),
        'ssm_d': nrm(ks[12], (L, SSM_WIDTH), 1.0),
        'w_glu_val': nrm(ks[13], (L, SSM_WIDTH, D_MODEL), BETA * SSM_WIDTH ** -0.5),
        'w_glu_gate': nrm(ks[14], (L, SSM_WIDTH, D_MODEL), SSM_WIDTH ** -0.5),
        'w_pool': nrm(ks[15], (L, len(POOL_WINDOWS), POOL_GROUP, POOL_GROUP), POOL_GROUP ** -0.5),
        'pool_scale': 1.0 + nrm(ks[16], (L, POOL_WIDTH), 0.1),
        'w_pool_out': nrm(ks[17], (L, POOL_WIDTH, D_MODEL), BETA * POOL_WIDTH ** -0.5),
        'w_out': nrm(ks[18], (L, D_MODEL, D_MODEL), BETA * D_MODEL ** -0.5),
        'ln1_g': 1.0 + nrm(ks[19], (L, D_MODEL), 0.02),
        'ln1_b': nrm(ks[20], (L, D_MODEL), 0.02),
        'w_ff1': nrm(ks[21], (L, D_MODEL, D_FF), D_MODEL ** -0.5),
        'w_ff2': nrm(ks[22], (L, D_FF, D_MODEL), BETA * D_FF ** -0.5),
        'ln2_g': 1.0 + nrm(ks[23], (L, D_MODEL), 0.02),
        'ln2_b': nrm(ks[24], (L, D_MODEL), 0.02),
    }


def _fwd_reference(x, c, w_ada, b_ada, w_in, lam_re, lam_im, log_dt, ssm_b_re, ssm_b_im,
              ssm_c_re, ssm_c_im, ssm_d, w_glu_val, w_glu_gate, w_pool, pool_scale,
              w_pool_out, w_out, ln1_g, ln1_b, w_ff1, w_ff2, ln2_g, ln2_b):
    c_act = jax.nn.silu(c)
    for l in range(DEPTH):
        mod = (c_act @ w_ada[l] + b_ada[l])[:, None, :]
        sh1, sc1, g1, sh2, sc2, g2 = jnp.split(mod, N_MOD, axis=-1)
        h = _modulate(x, sh1, sc1)
        y = _hybrid_mixer(h, w_in[l], lam_re[l], lam_im[l], log_dt[l], ssm_b_re[l], ssm_b_im[l],
                          ssm_c_re[l], ssm_c_im[l], ssm_d[l], w_glu_val[l], w_glu_gate[l],
                          w_pool[l], pool_scale[l], w_pool_out[l], w_out[l])
        x = _layernorm(ALPHA * x + g1 * y, ln1_g[l], ln1_b[l])
        h = _modulate(x, sh2, sc2)
        y = _sq_relu_mlp(h, w_ff1[l], w_ff2[l])
        x = _layernorm(ALPHA * x + g2 * y, ln2_g[l], ln2_b[l])
    return x


import jax as _jax
import jax.numpy as _jnp

TWIN_FORMAT = 'train_step'
FWD_PARAMS = ['x', 'c', 'w_ada', 'b_ada', 'w_in', 'lam_re', 'lam_im', 'log_dt', 'ssm_b_re', 'ssm_b_im', 'ssm_c_re', 'ssm_c_im', 'ssm_d', 'w_glu_val', 'w_glu_gate', 'w_pool', 'pool_scale', 'w_pool_out', 'w_out', 'ln1_g', 'ln1_b', 'w_ff1', 'w_ff2', 'ln2_g', 'ln2_b']
TWIN_WEIGHTS = ['w_ada', 'b_ada', 'w_in', 'lam_re', 'lam_im', 'log_dt', 'ssm_b_re', 'ssm_b_im', 'ssm_c_re', 'ssm_c_im', 'ssm_d', 'w_glu_val', 'w_glu_gate', 'w_pool', 'pool_scale', 'w_pool_out', 'w_out', 'ln1_g', 'ln1_b', 'w_ff1', 'w_ff2', 'ln2_g', 'ln2_b']
TWIN_DIFF_INPUT = 'x'
TWIN_INPUTS = ['x', 'c', 'w_ada', 'b_ada', 'w_in', 'lam_re', 'lam_im', 'log_dt', 'ssm_b_re', 'ssm_b_im', 'ssm_c_re', 'ssm_c_im', 'ssm_d', 'w_glu_val', 'w_glu_gate', 'w_pool', 'pool_scale', 'w_pool_out', 'w_out', 'ln1_g', 'ln1_b', 'w_ff1', 'w_ff2', 'ln2_g', 'ln2_b', 'loss_target', 'm_w_ada', 'm_b_ada', 'm_w_in', 'm_lam_re', 'm_lam_im', 'm_log_dt', 'm_ssm_b_re', 'm_ssm_b_im', 'm_ssm_c_re', 'm_ssm_c_im', 'm_ssm_d', 'm_w_glu_val', 'm_w_glu_gate', 'm_w_pool', 'm_pool_scale', 'm_w_pool_out', 'm_w_out', 'm_ln1_g', 'm_ln1_b', 'm_w_ff1', 'm_w_ff2', 'm_ln2_g', 'm_ln2_b', 'v_w_ada', 'v_b_ada', 'v_w_in', 'v_lam_re', 'v_lam_im', 'v_log_dt', 'v_ssm_b_re', 'v_ssm_b_im', 'v_ssm_c_re', 'v_ssm_c_im', 'v_ssm_d', 'v_w_glu_val', 'v_w_glu_gate', 'v_w_pool', 'v_pool_scale', 'v_w_pool_out', 'v_w_out', 'v_ln1_g', 'v_ln1_b', 'v_w_ff1', 'v_w_ff2', 'v_ln2_g', 'v_ln2_b']
TWIN_OUTPUTS = ['loss', 'grad_x', 'grad_w_ada', 'grad_b_ada', 'grad_w_in', 'grad_lam_re', 'grad_lam_im', 'grad_log_dt', 'grad_ssm_b_re', 'grad_ssm_b_im', 'grad_ssm_c_re', 'grad_ssm_c_im', 'grad_ssm_d', 'grad_w_glu_val', 'grad_w_glu_gate', 'grad_w_pool', 'grad_pool_scale', 'grad_w_pool_out', 'grad_w_out', 'grad_ln1_g', 'grad_ln1_b', 'grad_w_ff1', 'grad_w_ff2', 'grad_ln2_g', 'grad_ln2_b', 'delta_w_ada', 'delta_b_ada', 'delta_w_in', 'delta_lam_re', 'delta_lam_im', 'delta_log_dt', 'delta_ssm_b_re', 'delta_ssm_b_im', 'delta_ssm_c_re', 'delta_ssm_c_im', 'delta_ssm_d', 'delta_w_glu_val', 'delta_w_glu_gate', 'delta_w_pool', 'delta_pool_scale', 'delta_w_pool_out', 'delta_w_out', 'delta_ln1_g', 'delta_ln1_b', 'delta_w_ff1', 'delta_w_ff2', 'delta_ln2_g', 'delta_ln2_b', 'new_m_w_ada', 'new_m_b_ada', 'new_m_w_in', 'new_m_lam_re', 'new_m_lam_im', 'new_m_log_dt', 'new_m_ssm_b_re', 'new_m_ssm_b_im', 'new_m_ssm_c_re', 'new_m_ssm_c_im', 'new_m_ssm_d', 'new_m_w_glu_val', 'new_m_w_glu_gate', 'new_m_w_pool', 'new_m_pool_scale', 'new_m_w_pool_out', 'new_m_w_out', 'new_m_ln1_g', 'new_m_ln1_b', 'new_m_w_ff1', 'new_m_w_ff2', 'new_m_ln2_g', 'new_m_ln2_b', 'new_v_w_ada', 'new_v_b_ada', 'new_v_w_in', 'new_v_lam_re', 'new_v_lam_im', 'new_v_log_dt', 'new_v_ssm_b_re', 'new_v_ssm_b_im', 'new_v_ssm_c_re', 'new_v_ssm_c_im', 'new_v_ssm_d', 'new_v_w_glu_val', 'new_v_w_glu_gate', 'new_v_w_pool', 'new_v_pool_scale', 'new_v_w_pool_out', 'new_v_w_out', 'new_v_ln1_g', 'new_v_ln1_b', 'new_v_w_ff1', 'new_v_w_ff2', 'new_v_ln2_g', 'new_v_ln2_b']
TWIN_LEAF_KINDS = {'loss': 'loss', 'grad_x': 'grad_x', 'grad_w_ada': 'grad_w', 'grad_b_ada': 'grad_w', 'grad_w_in': 'grad_w', 'grad_lam_re': 'grad_w', 'grad_lam_im': 'grad_w', 'grad_log_dt': 'grad_w', 'grad_ssm_b_re': 'grad_w', 'grad_ssm_b_im': 'grad_w', 'grad_ssm_c_re': 'grad_w', 'grad_ssm_c_im': 'grad_w', 'grad_ssm_d': 'grad_w', 'grad_w_glu_val': 'grad_w', 'grad_w_glu_gate': 'grad_w', 'grad_w_pool': 'grad_w', 'grad_pool_scale': 'grad_w', 'grad_w_pool_out': 'grad_w', 'grad_w_out': 'grad_w', 'grad_ln1_g': 'grad_w', 'grad_ln1_b': 'grad_w', 'grad_w_ff1': 'grad_w', 'grad_w_ff2': 'grad_w', 'grad_ln2_g': 'grad_w', 'grad_ln2_b': 'grad_w', 'delta_w_ada': 'delta_w', 'delta_b_ada': 'delta_w', 'delta_w_in': 'delta_w', 'delta_lam_re': 'delta_w', 'delta_lam_im': 'delta_w', 'delta_log_dt': 'delta_w', 'delta_ssm_b_re': 'delta_w', 'delta_ssm_b_im': 'delta_w', 'delta_ssm_c_re': 'delta_w', 'delta_ssm_c_im': 'delta_w', 'delta_ssm_d': 'delta_w', 'delta_w_glu_val': 'delta_w', 'delta_w_glu_gate': 'delta_w', 'delta_w_pool': 'delta_w', 'delta_pool_scale': 'delta_w', 'delta_w_pool_out': 'delta_w', 'delta_w_out': 'delta_w', 'delta_ln1_g': 'delta_w', 'delta_ln1_b': 'delta_w', 'delta_w_ff1': 'delta_w', 'delta_w_ff2': 'delta_w', 'delta_ln2_g': 'delta_w', 'delta_ln2_b': 'delta_w', 'new_m_w_ada': 'new_m', 'new_m_b_ada': 'new_m', 'new_m_w_in': 'new_m', 'new_m_lam_re': 'new_m', 'new_m_lam_im': 'new_m', 'new_m_log_dt': 'new_m', 'new_m_ssm_b_re': 'new_m', 'new_m_ssm_b_im': 'new_m', 'new_m_ssm_c_re': 'new_m', 'new_m_ssm_c_im': 'new_m', 'new_m_ssm_d': 'new_m', 'new_m_w_glu_val': 'new_m', 'new_m_w_glu_gate': 'new_m', 'new_m_w_pool': 'new_m', 'new_m_pool_scale': 'new_m', 'new_m_w_pool_out': 'new_m', 'new_m_w_out': 'new_m', 'new_m_ln1_g': 'new_m', 'new_m_ln1_b': 'new_m', 'new_m_w_ff1': 'new_m', 'new_m_w_ff2': 'new_m', 'new_m_ln2_g': 'new_m', 'new_m_ln2_b': 'new_m', 'new_v_w_ada': 'new_v', 'new_v_b_ada': 'new_v', 'new_v_w_in': 'new_v', 'new_v_lam_re': 'new_v', 'new_v_lam_im': 'new_v', 'new_v_log_dt': 'new_v', 'new_v_ssm_b_re': 'new_v', 'new_v_ssm_b_im': 'new_v', 'new_v_ssm_c_re': 'new_v', 'new_v_ssm_c_im': 'new_v', 'new_v_ssm_d': 'new_v', 'new_v_w_glu_val': 'new_v', 'new_v_w_glu_gate': 'new_v', 'new_v_w_pool': 'new_v', 'new_v_pool_scale': 'new_v', 'new_v_w_pool_out': 'new_v', 'new_v_w_out': 'new_v', 'new_v_ln1_g': 'new_v', 'new_v_ln1_b': 'new_v', 'new_v_w_ff1': 'new_v', 'new_v_w_ff2': 'new_v', 'new_v_ln2_g': 'new_v', 'new_v_ln2_b': 'new_v'}


def _forward(args):
    return _fwd_reference(*[args[k] for k in FWD_PARAMS])


def _output_shape():
    out = _jax.eval_shape(lambda: _forward(_fwd_setup_inputs(0)))
    return out.shape, out.dtype

N_MICROBATCH = 1
ADAM_LR = 0.001
ADAM_B1 = 0.9
ADAM_B2 = 0.999
ADAM_EPS = 1e-08
ADAM_WD = 0.01
ADAM_STEP = 10
PER_EXAMPLE_BATCH_AXIS = {'x': 0, 'c': 0, 'loss_target': 0}
SHARED_INPUTS = []
_WEIGHT_DTYPES = {'w_ada': _jnp.float32, 'b_ada': _jnp.float32, 'w_in': _jnp.float32, 'lam_re': _jnp.float32, 'lam_im': _jnp.float32, 'log_dt': _jnp.float32, 'ssm_b_re': _jnp.float32, 'ssm_b_im': _jnp.float32, 'ssm_c_re': _jnp.float32, 'ssm_c_im': _jnp.float32, 'ssm_d': _jnp.float32, 'w_glu_val': _jnp.float32, 'w_glu_gate': _jnp.float32, 'w_pool': _jnp.float32, 'pool_scale': _jnp.float32, 'w_pool_out': _jnp.float32, 'w_out': _jnp.float32, 'ln1_g': _jnp.float32, 'ln1_b': _jnp.float32, 'w_ff1': _jnp.float32, 'w_ff2': _jnp.float32, 'ln2_g': _jnp.float32, 'ln2_b': _jnp.float32}
MOMENT_SCALE = {'w_ada': 1.394720e-02, 'b_ada': 2.664372e-02, 'w_in': 1.949519e-03, 'lam_re': 1.315780e-04, 'lam_im': 1.742593e-04, 'log_dt': 7.203074e-02, 'ssm_b_re': 7.225269e-05, 'ssm_b_im': 7.494072e-05, 'ssm_c_re': 1.504071e-04, 'ssm_c_im': 1.473002e-04, 'ssm_d': 1.857854e-03, 'w_glu_val': 2.138923e-03, 'w_glu_gate': 3.748449e-04, 'w_pool': 4.021338e-03, 'pool_scale': 4.145289e-03, 'w_pool_out': 4.803096e-03, 'w_out': 5.268888e-03, 'ln1_g': 2.716899e-01, 'ln1_b': 1.422526e-01, 'w_ff1': 7.891129e-03, 'w_ff2': 2.551164e-02, 'ln2_g': 8.002650e+00, 'ln2_b': 4.532812e-01}


def _to_microbatches(a, axis):
    t = _jnp.moveaxis(a, axis, 0)
    t = t.reshape((N_MICROBATCH, t.shape[0] // N_MICROBATCH) + t.shape[1:])
    return _jnp.moveaxis(t, 1, axis + 1)


def setup_inputs(seed: int = 0) -> dict:
    inp = _fwd_setup_inputs(seed)
    key = _jax.random.fold_in(_jax.random.key(seed), 7919)
    shape, _ = _output_shape()
    out = dict(inp)
    out["loss_target"] = _jax.random.normal(_jax.random.fold_in(key, 0), shape, _jnp.float32)
    for i, name in enumerate(TWIN_WEIGHTS):
        w = inp[name].astype(_jnp.float32)
        if MOMENT_SCALE is None:
            s = _jnp.sqrt(_jnp.mean(_jnp.square(w)) + 1e-30)
        else:
            s = MOMENT_SCALE[name]
        km, kv = _jax.random.split(_jax.random.fold_in(key, i + 1))
        out[name] = w
        out["m_" + name] = s * _jax.random.normal(km, w.shape, _jnp.float32)
        out["v_" + name] = (s * s) * _jax.random.uniform(kv, w.shape, _jnp.float32, 0.5, 1.5)
    if N_MICROBATCH > 1:
        for name, axis in PER_EXAMPLE_BATCH_AXIS.items():
            out[name] = _to_microbatches(out[name], axis)
    return {'x': out['x'], 'c': out['c'], 'w_ada': out['w_ada'], 'b_ada': out['b_ada'], 'w_in': out['w_in'], 'lam_re': out['lam_re'], 'lam_im': out['lam_im'], 'log_dt': out['log_dt'], 'ssm_b_re': out['ssm_b_re'], 'ssm_b_im': out['ssm_b_im'], 'ssm_c_re': out['ssm_c_re'], 'ssm_c_im': out['ssm_c_im'], 'ssm_d': out['ssm_d'], 'w_glu_val': out['w_glu_val'], 'w_glu_gate': out['w_glu_gate'], 'w_pool': out['w_pool'], 'pool_scale': out['pool_scale'], 'w_pool_out': out['w_pool_out'], 'w_out': out['w_out'], 'ln1_g': out['ln1_g'], 'ln1_b': out['ln1_b'], 'w_ff1': out['w_ff1'], 'w_ff2': out['w_ff2'], 'ln2_g': out['ln2_g'], 'ln2_b': out['ln2_b'], 'loss_target': out['loss_target'], 'm_w_ada': out['m_w_ada'], 'm_b_ada': out['m_b_ada'], 'm_w_in': out['m_w_in'], 'm_lam_re': out['m_lam_re'], 'm_lam_im': out['m_lam_im'], 'm_log_dt': out['m_log_dt'], 'm_ssm_b_re': out['m_ssm_b_re'], 'm_ssm_b_im': out['m_ssm_b_im'], 'm_ssm_c_re': out['m_ssm_c_re'], 'm_ssm_c_im': out['m_ssm_c_im'], 'm_ssm_d': out['m_ssm_d'], 'm_w_glu_val': out['m_w_glu_val'], 'm_w_glu_gate': out['m_w_glu_gate'], 'm_w_pool': out['m_w_pool'], 'm_pool_scale': out['m_pool_scale'], 'm_w_pool_out': out['m_w_pool_out'], 'm_w_out': out['m_w_out'], 'm_ln1_g': out['m_ln1_g'], 'm_ln1_b': out['m_ln1_b'], 'm_w_ff1': out['m_w_ff1'], 'm_w_ff2': out['m_w_ff2'], 'm_ln2_g': out['m_ln2_g'], 'm_ln2_b': out['m_ln2_b'], 'v_w_ada': out['v_w_ada'], 'v_b_ada': out['v_b_ada'], 'v_w_in': out['v_w_in'], 'v_lam_re': out['v_lam_re'], 'v_lam_im': out['v_lam_im'], 'v_log_dt': out['v_log_dt'], 'v_ssm_b_re': out['v_ssm_b_re'], 'v_ssm_b_im': out['v_ssm_b_im'], 'v_ssm_c_re': out['v_ssm_c_re'], 'v_ssm_c_im': out['v_ssm_c_im'], 'v_ssm_d': out['v_ssm_d'], 'v_w_glu_val': out['v_w_glu_val'], 'v_w_glu_gate': out['v_w_glu_gate'], 'v_w_pool': out['v_w_pool'], 'v_pool_scale': out['v_pool_scale'], 'v_w_pool_out': out['v_w_pool_out'], 'v_w_out': out['v_w_out'], 'v_ln1_g': out['v_ln1_g'], 'v_ln1_b': out['v_ln1_b'], 'v_w_ff1': out['v_w_ff1'], 'v_w_ff2': out['v_w_ff2'], 'v_ln2_g': out['v_ln2_g'], 'v_ln2_b': out['v_ln2_b']}


def _loss(weights, diff, rest, loss_target):
    with _jax.named_scope("forward"):
        args = {**rest, TWIN_DIFF_INPUT: diff, **{k: w.astype(_WEIGHT_DTYPES[k]) for k, w in weights.items()}}
        y = _forward(args)
    with _jax.named_scope("loss_head"):
        err = _jnp.square(y.astype(_jnp.float32) - loss_target)
        return 0.5 * _jnp.sum(_jnp.mean(err, axis=-1)) if err.ndim else 0.5 * err


def _adamw(w, g, m, v):
    m = ADAM_B1 * m + (1.0 - ADAM_B1) * g
    v = ADAM_B2 * v + (1.0 - ADAM_B2) * _jnp.square(g)
    m_hat = m / (1.0 - ADAM_B1 ** ADAM_STEP)
    v_hat = v / (1.0 - ADAM_B2 ** ADAM_STEP)
    delta = -ADAM_LR * (m_hat / (_jnp.sqrt(v_hat) + ADAM_EPS) + ADAM_WD * w)
    return delta, m, v


def reference(x, c, w_ada, b_ada, w_in, lam_re, lam_im, log_dt, ssm_b_re, ssm_b_im, ssm_c_re, ssm_c_im, ssm_d, w_glu_val, w_glu_gate, w_pool, pool_scale, w_pool_out, w_out, ln1_g, ln1_b, w_ff1, w_ff2, ln2_g, ln2_b, loss_target, m_w_ada, m_b_ada, m_w_in, m_lam_re, m_lam_im, m_log_dt, m_ssm_b_re, m_ssm_b_im, m_ssm_c_re, m_ssm_c_im, m_ssm_d, m_w_glu_val, m_w_glu_gate, m_w_pool, m_pool_scale, m_w_pool_out, m_w_out, m_ln1_g, m_ln1_b, m_w_ff1, m_w_ff2, m_ln2_g, m_ln2_b, v_w_ada, v_b_ada, v_w_in, v_lam_re, v_lam_im, v_log_dt, v_ssm_b_re, v_ssm_b_im, v_ssm_c_re, v_ssm_c_im, v_ssm_d, v_w_glu_val, v_w_glu_gate, v_w_pool, v_pool_scale, v_w_pool_out, v_w_out, v_ln1_g, v_ln1_b, v_w_ff1, v_w_ff2, v_ln2_g, v_ln2_b):
    given = dict(x=x, c=c, w_ada=w_ada, b_ada=b_ada, w_in=w_in, lam_re=lam_re, lam_im=lam_im, log_dt=log_dt, ssm_b_re=ssm_b_re, ssm_b_im=ssm_b_im, ssm_c_re=ssm_c_re, ssm_c_im=ssm_c_im, ssm_d=ssm_d, w_glu_val=w_glu_val, w_glu_gate=w_glu_gate, w_pool=w_pool, pool_scale=pool_scale, w_pool_out=w_pool_out, w_out=w_out, ln1_g=ln1_g, ln1_b=ln1_b, w_ff1=w_ff1, w_ff2=w_ff2, ln2_g=ln2_g, ln2_b=ln2_b, loss_target=loss_target, m_w_ada=m_w_ada, m_b_ada=m_b_ada, m_w_in=m_w_in, m_lam_re=m_lam_re, m_lam_im=m_lam_im, m_log_dt=m_log_dt, m_ssm_b_re=m_ssm_b_re, m_ssm_b_im=m_ssm_b_im, m_ssm_c_re=m_ssm_c_re, m_ssm_c_im=m_ssm_c_im, m_ssm_d=m_ssm_d, m_w_glu_val=m_w_glu_val, m_w_glu_gate=m_w_glu_gate, m_w_pool=m_w_pool, m_pool_scale=m_pool_scale, m_w_pool_out=m_w_pool_out, m_w_out=m_w_out, m_ln1_g=m_ln1_g, m_ln1_b=m_ln1_b, m_w_ff1=m_w_ff1, m_w_ff2=m_w_ff2, m_ln2_g=m_ln2_g, m_ln2_b=m_ln2_b, v_w_ada=v_w_ada, v_b_ada=v_b_ada, v_w_in=v_w_in, v_lam_re=v_lam_re, v_lam_im=v_lam_im, v_log_dt=v_log_dt, v_ssm_b_re=v_ssm_b_re, v_ssm_b_im=v_ssm_b_im, v_ssm_c_re=v_ssm_c_re, v_ssm_c_im=v_ssm_c_im, v_ssm_d=v_ssm_d, v_w_glu_val=v_w_glu_val, v_w_glu_gate=v_w_glu_gate, v_w_pool=v_w_pool, v_pool_scale=v_pool_scale, v_w_pool_out=v_w_pool_out, v_w_out=v_w_out, v_ln1_g=v_ln1_g, v_ln1_b=v_ln1_b, v_w_ff1=v_w_ff1, v_w_ff2=v_w_ff2, v_ln2_g=v_ln2_g, v_ln2_b=v_ln2_b)
    weights = {n: given[n] for n in TWIN_WEIGHTS}
    shared = {n: given[n] for n in SHARED_INPUTS}
    per_example = {n: given[n] for n in ['x', 'c']}
    grad_fn = _jax.value_and_grad(_loss, argnums=(0, 1))

    def one_microbatch(ex, loss_target):
        ex = dict(ex)
        diff = ex.pop(TWIN_DIFF_INPUT)
        return grad_fn(weights, diff, {**shared, **ex}, loss_target)

    if N_MICROBATCH == 1:
        loss, (grad_w, grad_x) = one_microbatch(per_example, given["loss_target"])
    else:
        def body(carry, xs):
            loss_sum, grad_sum = carry
            l_k, (gw_k, gx_k) = one_microbatch(xs[0], xs[1])
            with _jax.named_scope("update"):
                return (loss_sum + l_k, _jax.tree.map(_jnp.add, grad_sum, gw_k)), gx_k

        init = (_jnp.zeros((), _jnp.float32), _jax.tree.map(_jnp.zeros_like, weights))
        (loss, grad_w), grad_x = _jax.lax.scan(body, init, (per_example, given["loss_target"]))
    with _jax.named_scope("update"):
        delta_w, new_m, new_v = {}, {}, {}
        for n in TWIN_WEIGHTS:
            delta_w[n], new_m[n], new_v[n] = _adamw(weights[n], grad_w[n], given["m_" + n], given["v_" + n])
    return (loss, grad_x, *[grad_w[n] for n in TWIN_WEIGHTS], *[delta_w[n] for n in TWIN_WEIGHTS],
            *[new_m[n] for n in TWIN_WEIGHTS], *[new_v[n] for n in TWIN_WEIGHTS])
```

```python
import functools
import math

import jax
import jax.numpy as jnp
from jax import lax
from jax.experimental import pallas as pl
from jax.experimental.pallas import tpu as pltpu

F32 = jnp.float32
BF16 = jnp.bfloat16
MESH = pl.DeviceIdType.MESH
NDEV = 8
NCHIP = 4

SSM_GROUP = 16
SSM_STATE = 64
GROUPS_PER_BLOCK = 8
POOL_WINDOWS = (2, 4, 8, 16)
LN_EPS = 1e-5
ALPHA = 2.0 ** 0.25
ADAM_LR, ADAM_B1, ADAM_B2, ADAM_EPS, ADAM_WD, ADAM_STEP = 0.001, 0.9, 0.999, 1e-08, 0.01, 10
SUBLANES = 8
LANES = 128
VMEM_LIMIT = 56 * 1024 * 1024


def _params(sem=None, vmem=VMEM_LIMIT):
    return pltpu.CompilerParams(dimension_semantics=sem, vmem_limit_bytes=vmem)


def _tile(n, pref):
    if n <= pref:
        return n
    t = 1 << (pref.bit_length() - 1)
    while n % t:
        t //= 2
    return t


def _cast_epi(acc, ex, outs):
    outs[0][...] = acc[...].astype(outs[0].dtype)


def _mm(name, kind, a, b, grid, a_spec, b_spec, outs, acc_shape, nsub=1, c=None,
        pro=None, epi=None, extras=(), stacked_out=False):
    nk = grid[2]
    n_ex, n_out = len(extras), len(outs)
    epi_fn = epi

    def body(*refs):
        a_ref, b_ref = refs[0], refs[1]
        ex = refs[2:2 + n_ex]
        out_refs = refs[2 + n_ex:2 + n_ex + n_out]
        acc = refs[-1]
        k = pl.program_id(2)

        @pl.when(k == 0)
        def _():
            acc[...] = jnp.zeros_like(acc)

        av = a_ref[...]
        if pro is not None:
            av = pro(av)
        if kind == "nn":
            for s in range(nsub):
                acc[:, s * c:(s + 1) * c] += jnp.dot(av, b_ref[s], preferred_element_type=F32)
        elif kind == "nt":
            t = acc[...]
            for s in range(nsub):
                t = t + lax.dot_general(av[:, s * c:(s + 1) * c], b_ref[s],
                                        (((1,), (1,)), ((), ())), preferred_element_type=F32)
            acc[...] = t
        else:
            acc[...] += lax.dot_general(av, b_ref[...], (((0,), (0,)), ((), ())),
                                        preferred_element_type=F32)

        @pl.when(k == nk - 1)
        def _():
            if epi_fn is not None:
                epi_fn(acc, ex, out_refs)
            elif stacked_out:
                for s in range(nsub):
                    out_refs[0][s] = acc[:, s * c:(s + 1) * c].astype(out_refs[0].dtype)
            else:
                _cast_epi(acc, ex, out_refs)

    res = pl.pallas_call(
        body, name=name, grid=grid,
        in_specs=[a_spec, b_spec] + [e[1] for e in extras],
        out_specs=[o[1] for o in outs],
        out_shape=[o[0] for o in outs],
        scratch_shapes=[pltpu.VMEM(acc_shape, F32)],
        compiler_params=_params(("parallel", "parallel", "arbitrary")),
    )(a, b, *[e[0] for e in extras])
    return res


def _sds(shape, dtype):
    return jax.ShapeDtypeStruct(shape, dtype)


def mm_nn(name, a, b3, out_dtype, nsub, tm=1024, tk=2048, tn=None, pro=None, epi=None,
          extras=(), extra_outs=(), a_col0=0):
    M = a.shape[0]
    nb, K, cdim = b3.shape
    tm, tk = _tile(M, tm), _tile(K, tk)
    if nb == 1:
        tn = _tile(cdim, tn or 1024)
        nsub, c, nj = 1, tn, cdim // tn
        b_spec = pl.BlockSpec((1, tk, tn), lambda i, j, k: (0, k, j))
        N = cdim
    else:
        c, nj, tn = cdim, nb // nsub, nsub * cdim
        b_spec = pl.BlockSpec((nsub, tk, cdim), lambda i, j, k: (j, k, 0))
        N = nb * cdim
    kb0 = a_col0 // tk
    a_spec = pl.BlockSpec((tm, tk), lambda i, j, k: (i, kb0 + k))
    grid = (M // tm, nj, K // tk)
    o_spec = pl.BlockSpec((tm, tn), lambda i, j, k: (i, j))
    outs = [(_sds((M, N), out_dtype), o_spec)] + [(_sds((M, N), d), o_spec) for d in extra_outs]
    return _mm(name, "nn", a, b3, grid, a_spec, b_spec, outs, (tm, tn), nsub, c, pro, epi, extras)


def mm_nt(name, a, b3, out_dtype, nsub, tm=1024, tn=1024, epi=None, extras=(), extra_outs=()):
    M = a.shape[0]
    nb, N, cdim = b3.shape
    tm, tn = _tile(M, tm), _tile(N, tn)
    if nb == 1:
        tk = _tile(cdim, 2048)
        nsub, c, nk = 1, tk, cdim // tk
        b_spec = pl.BlockSpec((1, tn, tk), lambda i, j, k: (0, j, k))
    else:
        c, nk, tk = cdim, nb // nsub, nsub * cdim
        b_spec = pl.BlockSpec((nsub, tn, cdim), lambda i, j, k: (k, j, 0))
    a_spec = pl.BlockSpec((tm, tk), lambda i, j, k: (i, k))
    grid = (M // tm, N // tn, nk)
    o_spec = pl.BlockSpec((tm, tn), lambda i, j, k: (i, j))
    outs = [(_sds((M, N), out_dtype), o_spec)] + [(_sds((M, N), d), o_spec) for d in extra_outs]
    return _mm(name, "nt", a, b3, grid, a_spec, b_spec, outs, (tm, tn), nsub, c, None, epi, extras)


def mm_tn(name, a, b, out_dtype, nb, nsub, tma=1024, tk=1024, pro=None, a_col0=0, a_cols=None):
    S = a.shape[0]
    Ka = a_cols or a.shape[1]
    N = b.shape[1]
    tk = _tile(S, tk)
    if nsub == 0:
        rows = Ka // nb
        tma = rows if rows <= tma else _tile(rows, tma)
        per = rows // tma
        tn = _tile(N, 1024)
        grid = (Ka // tma, N // tn, S // tk)
        o_spec = pl.BlockSpec((1, tma, tn), lambda i, j, k: (i // per, i % per, j))
        out = _sds((nb, rows, N), out_dtype)
        nsub_k, c = 1, tn
        b_spec = pl.BlockSpec((tk, tn), lambda i, j, k: (k, j))
    else:
        c = N // nb
        tma = _tile(Ka, tma)
        grid = (Ka // tma, nb // nsub, S // tk)
        o_spec = pl.BlockSpec((nsub, tma, c), lambda i, j, k: (j, i, 0))
        out = _sds((nb, Ka, c), out_dtype)
        nsub_k = nsub
        tn = nsub * c
        b_spec = pl.BlockSpec((tk, tn), lambda i, j, k: (k, j))
    ab0 = a_col0 // tma
    a_spec = pl.BlockSpec((tk, tma), lambda i, j, k: (k, ab0 + i))
    return _mm(name, "tn", a, b, grid, a_spec, b_spec, [(out, o_spec)], (tma, tn), nsub_k, c,
               pro, None, (), stacked_out=True)[0]


def _rowwise(name, fn, S, ts, tiled, bcast, tiled_out, acc_out):
    nt, nb, no, na = len(tiled), len(bcast), len(tiled_out), len(acc_out)

    def body(*refs):
        tin = [r[...] for r in refs[:nt]]
        bin_ = [r[...] for r in refs[nt:nt + nb]]
        o_refs = refs[nt + nb:nt + nb + no]
        a_refs = refs[nt + nb + no:]
        touts, aouts = fn(tin, bin_)
        for r, v in zip(o_refs, touts):
            r[...] = v.astype(r.dtype)
        i = pl.program_id(0)

        @pl.when(i == 0)
        def _():
            for r, v in zip(a_refs, aouts):
                r[...] = v

        @pl.when(i > 0)
        def _():
            for r, v in zip(a_refs, aouts):
                r[...] += v

    in_specs = [pl.BlockSpec((ts, w), functools.partial(lambda i, cb: (i, cb), cb=cb))
                for (_, w, cb) in tiled]
    in_specs += [pl.BlockSpec(b.shape, lambda i: (0, 0)) for b in bcast]
    out_specs = [pl.BlockSpec((ts, w), lambda i: (i, 0)) for (w, _) in tiled_out]
    out_specs += [pl.BlockSpec((1, w), lambda i: (0, 0)) for w in acc_out]
    out_shape = [_sds((S, w), d) for (w, d) in tiled_out] + [_sds((1, w), F32) for w in acc_out]
    return pl.pallas_call(
        body, name=name, grid=(S // ts,), in_specs=in_specs, out_specs=out_specs,
        out_shape=out_shape, compiler_params=_params(("arbitrary",)),
    )(*[t[0] for t in tiled], *bcast)


def _ln_stats(v):
    mu = jnp.mean(v, axis=-1, keepdims=True)
    vc = v - mu
    var = jnp.mean(vc * vc, axis=-1, keepdims=True)
    rstd = lax.rsqrt(var + LN_EPS)
    return vc * rstd, rstd


def _ln_bwd(dxhat, xhat, rstd):
    return rstd * (dxhat - jnp.mean(dxhat, axis=-1, keepdims=True)
                   - xhat * jnp.mean(dxhat * xhat, axis=-1, keepdims=True))


def _colsum(v):
    return jnp.sum(v, axis=0, keepdims=True)


def _sigmoid(v):
    return 1.0 / (1.0 + jnp.exp(-v))


_GELU_C = math.sqrt(2.0 / math.pi)


def _gelu(v):
    return 0.5 * v * (1.0 + jnp.tanh(_GELU_C * (v + 0.044715 * v * v * v)))


def _gelu_grad(v):
    t = jnp.tanh(_GELU_C * (v + 0.044715 * v * v * v))
    return 0.5 * (1.0 + t) + 0.5 * v * (1.0 - t * t) * _GELU_C * (1.0 + 3 * 0.044715 * v * v)


def _disc(lr, li, ldt):
    dt = jnp.exp(ldt)
    mag = jnp.exp(lr * dt)
    ang = li * dt
    ab_re = mag * jnp.cos(ang)
    ab_im = mag * jnp.sin(ang)
    num_re = ab_re - 1.0
    num_im = ab_im
    den = lr * lr + li * li
    f_re = (num_re * lr + num_im * li) / den
    f_im = (num_im * lr - num_re * li) / den
    return ab_re, ab_im, f_re, f_im


def _cmul(ar, ai, br, bi):
    return ar * br - ai * bi, ar * bi + ai * br


def s5_disc(lam_re, lam_im, log_dt):
    G, P = lam_re.shape

    def body(lr_ref, li_ref, ldt_ref, f_ref, k_ref):
        ab_re, ab_im, f_re, f_im = _disc(lr_ref[...], li_ref[...], ldt_ref[...])
        f_ref[0] = f_re
        f_ref[1] = f_im
        pr, pi = [ab_re], [ab_im]
        for _ in range(SUBLANES - 1):
            nr, ni = _cmul(pr[-1], pi[-1], ab_re, ab_im)
            pr.append(nr)
            pi.append(ni)
        zero = jnp.zeros_like(ab_re)
        for n, sh in enumerate((1, 2, 4)):
            for r in range(SUBLANES):
                k_ref[2 * n, r] = pr[sh - 1] if r >= sh else zero
                k_ref[2 * n + 1, r] = pi[sh - 1] if r >= sh else zero
                k_ref[8 + 2 * n, r] = pr[sh - 1] if r + sh < SUBLANES else zero
                k_ref[8 + 2 * n + 1, r] = -pi[sh - 1] if r + sh < SUBLANES else zero
        for r in range(SUBLANES):
            k_ref[6, r] = pr[r]
            k_ref[7, r] = pi[r]
            k_ref[14, r] = pr[SUBLANES - 1 - r]
            k_ref[15, r] = -pi[SUBLANES - 1 - r]

    vm = pl.BlockSpec(memory_space=pltpu.VMEM)
    return pl.pallas_call(
        body, name="s5_disc", in_specs=[vm, vm, vm], out_specs=[vm, vm],
        out_shape=[_sds((2, G, P), F32), _sds((16, SUBLANES, G, P), F32)],
    )(lam_re, lam_im, log_dt)


def s5_disc_bwd(lam_re, lam_im, log_dt, d_ab, d_f):
    G, P = lam_re.shape

    def body(lr_ref, li_ref, ldt_ref, dab_ref, df_ref, glr_ref, gli_ref, gdt_ref):
        _, vjp = jax.vjp(_disc, lr_ref[...], li_ref[...], ldt_ref[...])
        glr, gli, gdt = vjp((dab_ref[0], dab_ref[1], df_ref[0], df_ref[1]))
        glr_ref[...] = glr
        gli_ref[...] = gli
        gdt_ref[...] = gdt

    vm = pl.BlockSpec(memory_space=pltpu.VMEM)
    return pl.pallas_call(
        body, name="s5_disc_bwd", in_specs=[vm] * 5, out_specs=[vm] * 3,
        out_shape=[_sds((G, P), F32), _sds((G, P), F32), _sds((G, 1), F32)],
    )(lam_re, lam_im, log_dt, d_ab, d_f)


def s5_bbar(f2, bt_re, bt_im):
    def body(f_ref, br_ref, bi_ref, o_ref):
        fr, fi = f_ref[0], f_ref[1]
        br, bi = br_ref[...], bi_ref[...]
        o_ref[0] = fr * br - fi * bi
        o_ref[1] = fr * bi + fi * br

    vm = pl.BlockSpec(memory_space=pltpu.VMEM)
    return pl.pallas_call(body, name="s5_bbar", in_specs=[vm] * 3, out_specs=vm,
                          out_shape=_sds((2,) + bt_re.shape, F32))(f2, bt_re, bt_im)


def s5_bbar_bwd(f2, bt_re, bt_im, dbb):
    def body(f_ref, br_ref, bi_ref, d_ref, gbr_ref, gbi_ref, gf_ref):
        fr, fi = f_ref[0], f_ref[1]
        br, bi = br_ref[...], bi_ref[...]
        dr, di = d_ref[0], d_ref[1]
        gbr_ref[...] = fr * dr + fi * di
        gbi_ref[...] = fr * di - fi * dr
        gf_ref[0] = _colsum(dr * br + di * bi)
        gf_ref[1] = _colsum(di * br - dr * bi)

    vm = pl.BlockSpec(memory_space=pltpu.VMEM)
    return pl.pallas_call(
        body, name="s5_bbar_bwd", in_specs=[vm] * 4, out_specs=[vm] * 3,
        out_shape=[_sds(bt_re.shape, F32), _sds(bt_re.shape, F32), _sds(f2.shape, F32)],
    )(f2, bt_re, bt_im, dbb)


def _scan_fwd(xs, k_ref, nst):
    ntile = xs.shape[0] // SUBLANES

    def step(t, carry):
        cr, ci = carry
        r0 = pl.multiple_of(t * SUBLANES, SUBLANES)
        xr = xs[pl.ds(r0, SUBLANES), 0:nst]
        xi = xs[pl.ds(r0, SUBLANES), nst:2 * nst]
        for n, sh in enumerate((1, 2, 4)):
            sr = pltpu.roll(xr, sh, 0)
            si = pltpu.roll(xi, sh, 0)
            mr, mi = k_ref[2 * n], k_ref[2 * n + 1]
            xr, xi = xr + mr * sr - mi * si, xi + mr * si + mi * sr
        pr, pi = k_ref[6], k_ref[7]
        xr, xi = xr + pr * cr - pi * ci, xi + pr * ci + pi * cr
        xs[pl.ds(r0, SUBLANES), 0:nst] = xr
        xs[pl.ds(r0, SUBLANES), nst:2 * nst] = xi
        return (jnp.broadcast_to(xr[SUBLANES - 1:SUBLANES, :], xr.shape),
                jnp.broadcast_to(xi[SUBLANES - 1:SUBLANES, :], xi.shape))

    zero = jnp.zeros((SUBLANES, nst), F32)
    lax.fori_loop(0, ntile, step, (zero, zero))


def _scan_bwd(g, xs, k_ref, nst):
    ntile = g.shape[0] // SUBLANES
    row = lax.broadcasted_iota(jnp.int32, (SUBLANES, nst), 0)

    def step(tt, carry):
        cr, ci, ar, ai = carry
        t = ntile - 1 - tt
        r0 = pl.multiple_of(t * SUBLANES, SUBLANES)
        gr = g[pl.ds(r0, SUBLANES), 0:nst]
        gi = g[pl.ds(r0, SUBLANES), nst:2 * nst]
        for n, sh in enumerate((1, 2, 4)):
            sr = pltpu.roll(gr, SUBLANES - sh, 0)
            si = pltpu.roll(gi, SUBLANES - sh, 0)
            mr, mi = k_ref[8 + 2 * n], k_ref[8 + 2 * n + 1]
            gr, gi = gr + mr * sr - mi * si, gi + mr * si + mi * sr
        qr, qi = k_ref[14], k_ref[15]
        gr, gi = gr + qr * cr - qi * ci, gi + qr * ci + qi * cr
        g[pl.ds(r0, SUBLANES), 0:nst] = gr
        g[pl.ds(r0, SUBLANES), nst:2 * nst] = gi
        p0 = pl.multiple_of(jnp.maximum(t - 1, 0) * SUBLANES, SUBLANES)
        live = (t > 0).astype(F32)
        xr = xs[pl.ds(r0, SUBLANES), 0:nst]
        xi = xs[pl.ds(r0, SUBLANES), nst:2 * nst]
        pr = xs[pl.ds(p0, SUBLANES), 0:nst][SUBLANES - 1:SUBLANES, :] * live
        pi = xs[pl.ds(p0, SUBLANES), nst:2 * nst][SUBLANES - 1:SUBLANES, :] * live
        xmr = jnp.where(row == 0, jnp.broadcast_to(pr, xr.shape), pltpu.roll(xr, 1, 0))
        xmi = jnp.where(row == 0, jnp.broadcast_to(pi, xi.shape), pltpu.roll(xi, 1, 0))
        ar = ar + gr * xmr + gi * xmi
        ai = ai + gi * xmr - gr * xmi
        return (jnp.broadcast_to(gr[0:1, :], gr.shape), jnp.broadcast_to(gi[0:1, :], gi.shape),
                ar, ai)

    zero = jnp.zeros((SUBLANES, nst), F32)
    _, _, ar, ai = lax.fori_loop(0, ntile, step, (zero, zero, zero, zero))
    return _colsum(ar), _colsum(ai)


def s5_fwd(proj, bmat, cmat, dskip, kconst):
    S = proj.shape[0]
    nb, cw, nst2 = bmat.shape
    nst = nst2 // 2

    def body(u_ref, b_ref, c_ref, d_ref, k_ref, z_ref, xs):
        u = u_ref[...]
        xs[...] = jnp.dot(u.astype(BF16), b_ref[0], preferred_element_type=F32)
        _scan_fwd(xs, k_ref, nst)
        y = jnp.dot(xs[...].astype(BF16), c_ref[0], preferred_element_type=F32) + d_ref[...] * u
        z_ref[...] = _gelu(y).astype(BF16)

    return pl.pallas_call(
        body, name="s5_fwd", grid=(nb,),
        in_specs=[pl.BlockSpec((S, cw), lambda b: (0, b)),
                  pl.BlockSpec((1, cw, nst2), lambda b: (b, 0, 0)),
                  pl.BlockSpec((1, nst2, cw), lambda b: (b, 0, 0)),
                  pl.BlockSpec((1, cw), lambda b: (0, b)),
                  pl.BlockSpec((16, SUBLANES, nst), lambda b: (0, 0, b))],
        out_specs=pl.BlockSpec((S, cw), lambda b: (0, b)),
        out_shape=_sds((S, nb * cw), BF16),
        scratch_shapes=[pltpu.VMEM((S, nst2), F32)],
        compiler_params=_params(("arbitrary",)),
    )(proj, bmat, cmat, dskip, kconst)


def s5_bwd(proj, dz, bmat, cmat, dskip, kconst):
    S = proj.shape[0]
    nb, cw, nst2 = bmat.shape
    nst = nst2 // 2

    def body(u_ref, dz_ref, b_ref, c_ref, d_ref, k_ref, du_ref, gb_ref, gc_ref, gd_ref, ga_ref,
             xs, g):
        u = u_ref[...]
        ub = u.astype(BF16)
        bm, cm, d = b_ref[0], c_ref[0], d_ref[...]
        xs[...] = jnp.dot(ub, bm, preferred_element_type=F32)
        _scan_fwd(xs, k_ref, nst)
        xsb = xs[...].astype(BF16)
        y = jnp.dot(xsb, cm, preferred_element_type=F32) + d * u
        dy = dz_ref[...].astype(F32) * _gelu_grad(y)
        gd_ref[...] = _colsum(dy * u)
        dyb = dy.astype(BF16)
        gc_ref[0] = lax.dot_general(xsb, dyb, (((0,), (0,)), ((), ())), preferred_element_type=F32)
        g[...] = lax.dot_general(dyb, cm, (((1,), (1,)), ((), ())), preferred_element_type=F32)
        ar, ai = _scan_bwd(g, xs, k_ref, nst)
        ga_ref[0, 0:1, :] = ar
        ga_ref[0, 1:2, :] = ai
        gb = g[...].astype(BF16)
        du = lax.dot_general(gb, bm, (((1,), (1,)), ((), ())), preferred_element_type=F32) + d * dy
        du_ref[...] = du.astype(BF16)
        gb_ref[0] = lax.dot_general(ub, gb, (((0,), (0,)), ((), ())), preferred_element_type=F32)

    return pl.pallas_call(
        body, name="s5_bwd", grid=(nb,),
        in_specs=[pl.BlockSpec((S, cw), lambda b: (0, b)),
                  pl.BlockSpec((S, cw), lambda b: (0, b)),
                  pl.BlockSpec((1, cw, nst2), lambda b: (b, 0, 0)),
                  pl.BlockSpec((1, nst2, cw), lambda b: (b, 0, 0)),
                  pl.BlockSpec((1, cw), lambda b: (0, b)),
                  pl.BlockSpec((16, SUBLANES, nst), lambda b: (0, 0, b))],
        out_specs=[pl.BlockSpec((S, cw), lambda b: (0, b)),
                   pl.BlockSpec((1, cw, nst2), lambda b: (b, 0, 0)),
                   pl.BlockSpec((1, nst2, cw), lambda b: (b, 0, 0)),
                   pl.BlockSpec((1, cw), lambda b: (0, b)),
                   pl.BlockSpec((1, 2, nst), lambda b: (b, 0, 0))],
        out_shape=[_sds((S, nb * cw), BF16), _sds((nb, cw, nst2), F32), _sds((nb, nst2, cw), F32),
                   _sds((1, nb * cw), F32), _sds((nb, 2, nst), F32)],
        scratch_shapes=[pltpu.VMEM((S, nst2), F32), pltpu.VMEM((S, nst2), F32)],
        compiler_params=_params(("arbitrary",)),
    )(proj, dz, bmat, cmat, dskip, kconst)


def _shift_rows(v, k, row, down):
    n = v.shape[0]
    if down:
        return jnp.where(row >= k, pltpu.roll(v, k, 0), 0.0)
    return jnp.where(row < n - k, pltpu.roll(v, n - k, 0), 0.0)


def _window(v, gi, row, down):
    sums = []
    s = v
    for k in (1, 2, 4, 8):
        s = s + _shift_rows(s, k, row, down)
        sums.append(s)
    out = sums[3]
    for n in (2, 1, 0):
        out = jnp.where(gi == n, sums[n], out)
    return out


def pool_fwd(proj, col0, width, gw):
    S = proj.shape[0]
    cb0 = col0 // gw

    def body(u_ref, o_ref):
        gi = pl.program_id(0)
        u = u_ref[...]
        row = lax.broadcasted_iota(jnp.int32, u.shape, 0)
        w = jnp.left_shift(2, gi)
        count = jnp.minimum(row + 1, w).astype(F32)
        o_ref[...] = (_window(u, gi, row, True) / count - u).astype(BF16)

    return pl.pallas_call(
        body, name="pool_fwd", grid=(len(POOL_WINDOWS),),
        in_specs=[pl.BlockSpec((S, gw), lambda g: (0, cb0 + g))],
        out_specs=pl.BlockSpec((S, gw), lambda g: (0, g)),
        out_shape=_sds((S, width), BF16), compiler_params=_params(("arbitrary",)),
    )(proj)


def pool_bwd(dpooled, gw):
    S, width = dpooled.shape

    def body(d_ref, o_ref):
        gi = pl.program_id(0)
        d = d_ref[...]
        row = lax.broadcasted_iota(jnp.int32, d.shape, 0)
        w = jnp.left_shift(2, gi)
        count = jnp.minimum(row + 1, w).astype(F32)
        o_ref[...] = (_window(d / count, gi, row, False) - d).astype(BF16)

    return pl.pallas_call(
        body, name="pool_bwd", grid=(len(POOL_WINDOWS),),
        in_specs=[pl.BlockSpec((S, gw), lambda g: (0, g))],
        out_specs=pl.BlockSpec((S, gw), lambda g: (0, g)),
        out_shape=_sds((S, width), BF16), compiler_params=_params(("arbitrary",)),
    )(dpooled)


def _place():
    x, y, c = lax.axis_index("x"), lax.axis_index("y"), lax.axis_index("c")
    chips = [(1 - x, y), (x, 1 - y), (1 - x, 1 - y)]
    return x, y, c, chips


HBM = pl.BlockSpec(memory_space=pltpu.HBM)


def all_gather(name, shards):
    n = len(shards)

    def body(*refs):
        ins, outs = refs[:n], refs[n:2 * n]
        send_sems, recv_sems, local_sems = refs[2 * n:]
        x, y, c, chips = _place()
        me, sibling = (x, y, c), (x, y, 1 - c)

        def slot(i, p):
            return outs[i].at[4 * p[0] + 2 * p[1] + p[2]]

        def copy(i, k, block, to, src=None):
            return pltpu.make_async_remote_copy(
                src_ref=slot(i, block) if src is None else src, dst_ref=slot(i, block),
                send_sem=send_sems.at[i, k], recv_sem=recv_sems.at[i, k],
                device_id=to, device_id_type=MESH)

        started = []
        for i in range(n):
            for j, chip in enumerate(chips):
                started.append(copy(i, 1 + j, me, (*chip, c), src=ins[i]))
                started[-1].start()
        for i in range(n):
            started.append(copy(i, 0, me, sibling, src=ins[i]))
            started[-1].start()
        mine = [pltpu.make_async_copy(ins[i], slot(i, me), local_sems.at[i]) for i in range(n)]
        for cp in mine:
            cp.start()
        for i in range(n):
            for j, chip in enumerate(chips):
                copy(i, 1 + j, (*chip, c), me).wait_recv()
                started.append(copy(i, 4 + j, (*chip, c), sibling))
                started[-1].start()
        for i in range(n):
            copy(i, 0, sibling, me).wait_recv()
            for j, chip in enumerate(chips):
                copy(i, 4 + j, (*chip, 1 - c), me).wait_recv()
        for cp in started:
            cp.wait_send()
        for cp in mine:
            cp.wait()

    return pl.pallas_call(
        body, name=name, in_specs=[HBM] * n, out_specs=[HBM] * n,
        out_shape=[_sds((NDEV,) + s.shape, s.dtype) for s in shards],
        scratch_shapes=[pltpu.SemaphoreType.DMA((n, 7)), pltpu.SemaphoreType.DMA((n, 7)),
                        pltpu.SemaphoreType.DMA((n,))],
    )(*shards)


def pair_exchange(name, grads):
    n = len(grads)

    def body(*refs):
        ins, outs = refs[:n], refs[n:2 * n]
        send_sems, recv_sems = refs[2 * n:]
        x, y, c, _ = _place()
        sibling = (x, y, 1 - c)
        cps = []
        for i in range(n):
            for q in range(NCHIP):
                cps.append(pltpu.make_async_remote_copy(
                    src_ref=ins[i].at[2 * q + 1 - c], dst_ref=outs[i].at[q],
                    send_sem=send_sems.at[i, q], recv_sem=recv_sems.at[i, q],
                    device_id=sibling, device_id_type=MESH))
                cps[-1].start()
        for cp in cps:
            cp.wait()

    return pl.pallas_call(
        body, name=name, in_specs=[HBM] * n, out_specs=[HBM] * n,
        out_shape=[_sds((NCHIP,) + g.shape[1:], g.dtype) for g in grads],
        scratch_shapes=[pltpu.SemaphoreType.DMA((n, NCHIP)), pltpu.SemaphoreType.DMA((n, NCHIP))],
    )(*grads)


def pair_sum(name, grad, got, my_c):
    shp = grad.shape[1:]
    r, cdim = shp[-2], shp[-1]
    lead = int(math.prod(shp[:-2])) if len(shp) > 2 else 1
    g5 = grad.reshape(NCHIP, 2, lead * r, cdim)
    t4 = got.reshape(NCHIP, lead * r, cdim)
    R = lead * r
    tr = _tile(R, max(8, (1 << 20) // cdim))

    def body(c_ref, g_ref, t_ref, o_ref):
        o_ref[...] = (g_ref[0].astype(F32) + t_ref[...].astype(F32)).astype(o_ref.dtype)

    out = pl.pallas_call(
        body, name=name,
        grid_spec=pltpu.PrefetchScalarGridSpec(
            num_scalar_prefetch=1, grid=(NCHIP, R // tr),
            in_specs=[pl.BlockSpec((1, 1, tr, cdim), lambda q, i, cr: (q, cr[0], i, 0)),
                      pl.BlockSpec((1, tr, cdim), lambda q, i, cr: (q, i, 0))],
            out_specs=pl.BlockSpec((1, tr, cdim), lambda q, i, cr: (q, i, 0))),
        out_shape=_sds((NCHIP, R, cdim), grad.dtype),
        compiler_params=_params(("parallel", "parallel")),
    )(my_c, g5, t4)
    return out


def chip_exchange(name, parts):
    n = len(parts)

    def body(*refs):
        ins, outs = refs[:n], refs[n:2 * n]
        send_sems, recv_sems = refs[2 * n:]
        x, y, c, chips = _place()
        cps = []
        for i in range(n):
            for j, chip in enumerate(chips):
                cps.append(pltpu.make_async_remote_copy(
                    src_ref=ins[i].at[2 * chip[0] + chip[1]], dst_ref=outs[i].at[j],
                    send_sem=send_sems.at[i, j], recv_sem=recv_sems.at[i, j],
                    device_id=(*chip, c), device_id_type=MESH))
                cps[-1].start()
        for cp in cps:
            cp.wait()

    return pl.pallas_call(
        body, name=name, in_specs=[HBM] * n, out_specs=[HBM] * n,
        out_shape=[_sds((3,) + p.shape[1:], p.dtype) for p in parts],
        scratch_shapes=[pltpu.SemaphoreType.DMA((n, 3)), pltpu.SemaphoreType.DMA((n, 3))],
    )(*parts)


def ada_fwd(c_row, w_ada, b_ada):
    D, cols = w_ada.shape

    def body(c_ref, w_ref, b_ref, mod_ref, call_ref, act8, part, s1, r1, s2, r2):
        x, y, c, _ = _place()
        me = 4 * x + 2 * y + c
        call_ref[me] = c_ref[...]
        cps = []
        for k in range(1, NDEV):
            to = (x ^ (k >> 2), y ^ ((k >> 1) & 1), c ^ (k & 1))
            cps.append(pltpu.make_async_remote_copy(
                src_ref=c_ref, dst_ref=call_ref.at[me], send_sem=s1.at[k - 1],
                recv_sem=r1.at[k - 1], device_id=to, device_id_type=MESH))
            cps[-1].start()
        for cp in cps:
            cp.wait()
        for b in range(NDEV):
            act8[b:b + 1, :] = call_ref[b]
        cv = act8[...]
        act = (cv * _sigmoid(cv)).astype(BF16)
        res = jnp.dot(act, w_ref[...].astype(BF16), preferred_element_type=F32)
        for b in range(NDEV):
            part[b] = res[b:b + 1, :]
        mod_ref[me] = part[me]
        cps = []
        for k in range(1, NDEV):
            to = (x ^ (k >> 2), y ^ ((k >> 1) & 1), c ^ (k & 1))
            dst = 4 * to[0] + 2 * to[1] + to[2]
            cps.append(pltpu.make_async_remote_copy(
                src_ref=part.at[dst], dst_ref=mod_ref.at[me], send_sem=s2.at[k - 1],
                recv_sem=r2.at[k - 1], device_id=to, device_id_type=MESH))
            cps[-1].start()
        for cp in cps:
            cp.wait()
        for b in range(NDEV):
            mod_ref[b] = mod_ref[b] + b_ref[b]

    vm = pl.BlockSpec(memory_space=pltpu.VMEM)
    return pl.pallas_call(
        body, name="ada_fwd", in_specs=[vm, vm, vm], out_specs=[vm, vm],
        out_shape=[_sds((NDEV, 1, cols), F32), _sds((NDEV, 1, D), F32)],
        scratch_shapes=[pltpu.VMEM((NDEV, D), F32), pltpu.VMEM((NDEV, 1, cols), F32),
                        pltpu.SemaphoreType.DMA((NDEV - 1,)), pltpu.SemaphoreType.DMA((NDEV - 1,)),
                        pltpu.SemaphoreType.DMA((NDEV - 1,)), pltpu.SemaphoreType.DMA((NDEV - 1,))],
        compiler_params=pltpu.CompilerParams(vmem_limit_bytes=VMEM_LIMIT),
    )(c_row, w_ada, b_ada.reshape(NDEV, 1, cols))


def _adamw_math(g, w, m, v):
    m2 = ADAM_B1 * m + (1.0 - ADAM_B1) * g
    v2 = ADAM_B2 * v + (1.0 - ADAM_B2) * (g * g)
    m_hat = m2 / (1.0 - ADAM_B1 ** ADAM_STEP)
    v_hat = v2 / (1.0 - ADAM_B2 ** ADAM_STEP)
    delta = -ADAM_LR * (m_hat / (jnp.sqrt(v_hat) + ADAM_EPS) + ADAM_WD * w)
    return delta, m2, v2


def adamw_sharded(name, part4, got3, w, m, v, my_chip):
    shape = w.shape
    cdim = shape[-1]
    R = int(math.prod(shape[:-1]))
    w2, m2, v2 = (t.reshape(R, cdim) for t in (w, m, v))
    tr = _tile(R, max(8, (1 << 19) // cdim))

    def body(q_ref, p_ref, t_ref, w_ref, m_ref, v_ref, g_out, d_out, m_out, v_out):
        g = p_ref[0].astype(F32)
        for j in range(3):
            g = g + t_ref[j].astype(F32)
        d, mn, vn = _adamw_math(g, w_ref[...], m_ref[...], v_ref[...])
        g_out[...] = g
        d_out[...] = d
        m_out[...] = mn
        v_out[...] = vn

    spec = pl.BlockSpec((tr, cdim), lambda i, qr: (i, 0))
    outs = pl.pallas_call(
        body, name=name,
        grid_spec=pltpu.PrefetchScalarGridSpec(
            num_scalar_prefetch=1, grid=(R // tr,),
            in_specs=[pl.BlockSpec((1, tr, cdim), lambda i, qr: (qr[0], i, 0)),
                      pl.BlockSpec((3, tr, cdim), lambda i, qr: (0, i, 0)), spec, spec, spec],
            out_specs=[spec] * 4),
        out_shape=[_sds((R, cdim), F32)] * 4,
        compiler_params=_params(("parallel",)),
    )(my_chip, part4.reshape(NCHIP, R, cdim), got3.reshape(3, R, cdim), w2, m2, v2)
    return [o.reshape(shape) for o in outs]


def adamw_small(parts, w, m, v):
    R = w.shape[0]
    tr = R

    def body(p_ref, w_ref, m_ref, v_ref, g_out, d_out, m_out, v_out):
        g = p_ref[0]
        for j in range(1, NDEV):
            g = g + p_ref[j]
        d, mn, vn = _adamw_math(g, w_ref[...], m_ref[...], v_ref[...])
        g_out[...] = g
        d_out[...] = d
        m_out[...] = mn
        v_out[...] = vn

    spec = pl.BlockSpec((tr, LANES), lambda i: (i, 0))
    return pl.pallas_call(
        body, name="adamw_small", grid=(R // tr,),
        in_specs=[pl.BlockSpec((NDEV, tr, LANES), lambda i: (0, i, 0)), spec, spec, spec],
        out_specs=[spec] * 4, out_shape=[_sds((R, LANES), F32)] * 4,
        compiler_params=_params(("parallel",)),
    )(parts, w, m, v)


def adamw_ada(c_all_t, dmod_all, w, m, v, my_dev):
    D, cols = w.shape
    tr = _tile(D, 256)

    def body(k_ref, c_ref, d_ref, w_ref, m_ref, v_ref, g_out, d_out, m_out, v_out):
        cv = c_ref[...]
        act = cv * _sigmoid(cv)
        dm = d_ref[...]
        g = act[:, 0:1] * dm[0:1, :]
        for b in range(1, NDEV):
            g = g + act[:, b:b + 1] * dm[b:b + 1, :]
        d, mn, vn = _adamw_math(g, w_ref[...], m_ref[...], v_ref[...])
        g_out[...] = g
        d_out[...] = d
        m_out[...] = mn
        v_out[...] = vn

    spec = pl.BlockSpec((tr, cols), lambda i, kr: (i, 0))
    return pl.pallas_call(
        body, name="adamw_ada",
        grid_spec=pltpu.PrefetchScalarGridSpec(
            num_scalar_prefetch=1, grid=(D // tr,),
            in_specs=[pl.BlockSpec((tr, NDEV), lambda i, kr: (i, 0)),
                      pl.BlockSpec((NDEV, cols), lambda i, kr: (0, kr[0])), spec, spec, spec],
            out_specs=[spec] * 4),
        out_shape=[_sds((D, cols), F32)] * 4,
        compiler_params=_params(("parallel",)),
    )(my_dev, c_all_t, dmod_all, w, m, v)


def _blockdiag(t, eye):
    nb, gpb, R, C = t.shape
    return jnp.einsum("bgrc,gk->bgrkc", t, eye).reshape(nb, gpb * R, gpb * C)


def _diag_blocks(t, gpb, R, C):
    nb = t.shape[0]
    t5 = t.reshape(nb, gpb, R, gpb, C)
    idx = jnp.arange(gpb)
    return jnp.moveaxis(t5[:, idx, :, idx, :], 0, 1)


def _small_pack(parts):
    rows = []
    for p in parts:
        flat = p.reshape(-1)
        pad = (-flat.shape[0]) % LANES
        if pad:
            flat = jnp.concatenate([flat, jnp.zeros((pad,), F32)])
        rows.append(flat.reshape(-1, LANES))
    buf = jnp.concatenate(rows, axis=0)
    pad = (-buf.shape[0]) % SUBLANES
    if pad:
        buf = jnp.concatenate([buf, jnp.zeros((pad, LANES), F32)], axis=0)
    return buf


def _small_unpack(buf, shapes):
    out, r = [], 0
    for s in shapes:
        n = int(math.prod(s))
        nr = -(-n // LANES)
        out.append(buf[r:r + nr].reshape(-1)[:n].reshape(s))
        r += nr
    return out


def kernel(x, c, w_ada, b_ada, w_in, lam_re, lam_im, log_dt, ssm_b_re, ssm_b_im, ssm_c_re, ssm_c_im, ssm_d, w_glu_val, w_glu_gate, w_pool, pool_scale, w_pool_out, w_out, ln1_g, ln1_b, w_ff1, w_ff2, ln2_g, ln2_b, loss_target, m_w_ada, m_b_ada, m_w_in, m_lam_re, m_lam_im, m_log_dt, m_ssm_b_re, m_ssm_b_im, m_ssm_c_re, m_ssm_c_im, m_ssm_d, m_w_glu_val, m_w_glu_gate, m_w_pool, m_pool_scale, m_w_pool_out, m_w_out, m_ln1_g, m_ln1_b, m_w_ff1, m_w_ff2, m_ln2_g, m_ln2_b, v_w_ada, v_b_ada, v_w_in, v_lam_re, v_lam_im, v_log_dt, v_ssm_b_re, v_ssm_b_im, v_ssm_c_re, v_ssm_c_im, v_ssm_d, v_w_glu_val, v_w_glu_gate, v_w_pool, v_pool_scale, v_w_pool_out, v_w_out, v_ln1_g, v_ln1_b, v_w_ff1, v_w_ff2, v_ln2_g, v_ln2_b):
    S, D = x.shape[1], x.shape[2]
    x2d, tgt = x[0], loss_target[0]
    W = D // 2
    G = W // SSM_GROUP
    P, H, GPB = SSM_STATE, SSM_GROUP, GROUPS_PER_BLOCK
    nblk = G // GPB
    gw = W // len(POOL_WINDOWS)
    ax, ay, ac = lax.axis_index("x"), lax.axis_index("y"), lax.axis_index("c")
    my_c = ac.astype(jnp.int32).reshape(1)
    my_chip = (2 * ax + ay).astype(jnp.int32).reshape(1)
    my_dev = (4 * ax + 2 * ay + ac).astype(jnp.int32).reshape(1)
    ts = _tile(S, 256)

    glu = jnp.stack([w_glu_val[0], w_glu_gate[0]]).astype(BF16)
    shards = [w_in[0].astype(BF16), glu, w_pool[0].astype(BF16), w_pool_out[0].astype(BF16),
              w_out[0].astype(BF16), w_ff1[0].astype(BF16), w_ff2[0].astype(BF16)]
    wg_in, wg_vg, wg_pool, wg_po, wg_out, wg_ff1, wg_ff2 = all_gather("gather_weights", shards)
    wg_vg = wg_vg.reshape(2 * NDEV, W, D // NDEV)
    nwin = len(POOL_WINDOWS)
    wp_full = jnp.transpose(wg_pool, (1, 0, 2, 3)).reshape(nwin, gw, gw)
    wout_full = wg_out.reshape(1, D, D)
    wff2_full = wg_ff2.reshape(1, 4 * D, D)

    mod, c_all = ada_fwd(c, w_ada[0], b_ada)
    mod = mod.reshape(6, 1, D)
    sh1, sc1, g1, sh2, sc2, g2 = (mod[i] for i in range(6))

    f2, kconst = s5_disc(lam_re[0], lam_im[0], log_dt[0].reshape(G, 1))
    kconst = kconst.reshape(16, SUBLANES, G * P)
    f2r = f2.reshape(2, 1, G * P)
    bt_re = jnp.transpose(ssm_b_re[0], (2, 0, 1)).reshape(H, G * P)
    bt_im = jnp.transpose(ssm_b_im[0], (2, 0, 1)).reshape(H, G * P)
    bbar = s5_bbar(f2r, bt_re, bt_im)
    eye = jnp.eye(GPB, dtype=F32)
    bb4 = jnp.transpose(bbar.reshape(2, H, nblk, GPB, P), (0, 2, 3, 1, 4))
    bmat = jnp.concatenate([_blockdiag(bb4[0], eye), _blockdiag(bb4[1], eye)], axis=2).astype(BF16)
    c4_re = jnp.transpose(ssm_c_re[0].reshape(nblk, GPB, H, P), (0, 1, 3, 2))
    c4_im = jnp.transpose(ssm_c_im[0].reshape(nblk, GPB, H, P), (0, 1, 3, 2))
    cmat = jnp.concatenate([_blockdiag(c4_re, eye), -_blockdiag(c4_im, eye)], axis=1).astype(BF16)

    def e1(t, b):
        xhat, _ = _ln_stats(t[0])
        return [xhat * (1.0 + b[0]) + b[1]], []
    (h1,) = _rowwise("ln_mod1", e1, S, ts, [(x2d, D, 0)], [sc1, sh1], [(D, BF16)], [])

    (proj,) = mm_nn("proj", h1, wg_in, F32, 1)
    z = s5_fwd(proj, bmat, cmat, ssm_d, kconst)
    (vt,) = mm_nn("glu", z, wg_vg, BF16, 4)
    pooled = pool_fwd(proj, W, W, gw)

    def pool_epi(acc, ex, outs):
        a = acc[...]
        outs[0][...] = a
        outs[1][...] = (a * ex[0][...]).astype(BF16)
    tmp = _tile(S, 1024)
    yp, ypool = _mm(
        "pool_mix", "nn", pooled, wp_full.astype(BF16), (S // tmp, nwin, 1),
        pl.BlockSpec((tmp, gw), lambda i, j, k: (i, j)), pl.BlockSpec((1, gw, gw), lambda i, j, k: (j, 0, 0)),
        [(_sds((S, W), F32), pl.BlockSpec((tmp, gw), lambda i, j, k: (i, j))),
         (_sds((S, W), BF16), pl.BlockSpec((tmp, gw), lambda i, j, k: (i, j)))],
        (tmp, gw), 1, gw, None, pool_epi,
        [(pool_scale, pl.BlockSpec((1, gw), lambda i, j, k: (0, j)))])
    (y_b,) = mm_nn("pool_out", ypool, wg_po, BF16, 4)

    cb = D // NDEV
    ga_cb, gb_cb = (2 * W) // cb, (2 * W + D) // cb
    tsm = _tile(S, 512)

    def merge_call(name, fn, ins, n_out):
        def body(*refs):
            vals = [r[...].astype(F32) for r in refs[:len(ins)]]
            for r, v in zip(refs[len(ins):], fn(*vals)):
                r[...] = v.astype(r.dtype)
        return pl.pallas_call(
            body, name=name, grid=(S // tsm, NDEV),
            in_specs=[pl.BlockSpec((tsm, cb), f) for (_, f) in ins],
            out_specs=[pl.BlockSpec((tsm, cb), f) for (_, f) in n_out],
            out_shape=[_sds(s, BF16) for (s, _) in n_out],
            compiler_params=_params(("parallel", "parallel")),
        )(*[a for (a, _) in ins])

    merge_ins = [(proj, lambda i, j: (i, ga_cb + j)), (proj, lambda i, j: (i, gb_cb + j)),
                 (vt, lambda i, j: (i, 2 * j)), (vt, lambda i, j: (i, 2 * j + 1)),
                 (y_b, lambda i, j: (i, j))]

    def merge_f(ga, gb, vv, tt, yb):
        return [_sigmoid(ga) * (vv * _sigmoid(tt)) + _sigmoid(gb) * yb]
    (merged,) = merge_call("merge", merge_f, merge_ins, [((S, D), lambda i, j: (i, j))])

    (mix,) = mm_nn("mix_out", merged, wout_full, F32, 1)

    def e3(t, b):
        xv, mx = t
        g1v, l1g, l1b, sc2v, sh2v = b
        r1 = ALPHA * xv + g1v * mx
        xh1, _ = _ln_stats(r1)
        x1 = xh1 * l1g + l1b
        xh, _ = _ln_stats(x1)
        return [r1, xh * (1.0 + sc2v) + sh2v], []
    r1, h2 = _rowwise("post_mix", e3, S, ts, [(x2d, D, 0), (mix, D, 0)],
                      [g1, ln1_g, ln1_b, sc2, sh2], [(D, F32), (D, BF16)], [])

    def relu_epi(acc, ex, outs):
        outs[0][...] = jnp.maximum(acc[...], 0.0).astype(BF16)
    (rl,) = mm_nn("ff1", h2, wg_ff1, BF16, 1, epi=relu_epi)

    def square(a):
        return a * a
    (y2,) = mm_nn("ff2", rl, wff2_full, F32, 1, pro=square)

    def e4(t, b):
        r1v, y2v, tg = t
        g2v, l1g, l1b, l2g, l2b = b
        xh1, _ = _ln_stats(r1v)
        x1 = xh1 * l1g + l1b
        r2 = ALPHA * x1 + g2v * y2v
        xh2, rs2 = _ln_stats(r2)
        err = xh2 * l2g + l2b - tg
        dx2 = err * (1.0 / D)
        dr2 = _ln_bwd(dx2 * l2g, xh2, rs2)
        lsum = jnp.sum(_colsum(err * err), axis=1, keepdims=True) * (0.5 / D)
        return ([ALPHA * dr2, g2v * dr2],
                [jnp.broadcast_to(lsum, (1, LANES)), _colsum(dx2 * xh2), _colsum(dx2), _colsum(dr2 * y2v)])
    dx1a, dy2, loss_acc, g_ln2g, g_ln2b, d_g2 = _rowwise(
        "head", e4, S, ts, [(r1, D, 0), (y2, D, 0), (tgt, D, 0)], [g2, ln1_g, ln1_b, ln2_g, ln2_b],
        [(D, F32), (D, BF16)], [LANES, D, D, D])

    tn_ff = _tile(4 * D, 1024)

    def dff_epi(acc, ex, outs):
        outs[0][...] = (acc[...] * (2.0 * ex[0][...].astype(F32))).astype(BF16)
    tmf = _tile(S, 1024)
    (da1,) = mm_nt("d_ff2", dy2, wff2_full, BF16, 1, tn=tn_ff, epi=dff_epi,
                   extras=[(rl, pl.BlockSpec((tmf, tn_ff), lambda i, j, k: (i, j)))])
    gw_ff2 = mm_tn("gw_ff2", rl, dy2, BF16, NDEV, 0, pro=square)
    (dh2,) = mm_nt("d_ff1", da1, wg_ff1, F32, 2)
    gw_ff1 = mm_tn("gw_ff1", h2, da1, BF16, NDEV, 1)

    def e5(t, b):
        dh2v, r1v, dx1av, mx = t
        sc2v, l1g, l1b, g1v = b
        xh1, rs1 = _ln_stats(r1v)
        x1 = xh1 * l1g + l1b
        xh, rs = _ln_stats(x1)
        dx1 = dx1av + _ln_bwd(dh2v * (1.0 + sc2v), xh, rs)
        dr1 = _ln_bwd(dx1 * l1g, xh1, rs1)
        return ([ALPHA * dr1, g1v * dr1],
                [_colsum(dh2v * xh), _colsum(dh2v), _colsum(dx1 * xh1), _colsum(dx1), _colsum(dr1 * mx)])
    dxa, dmix, d_sc2, d_sh2, g_ln1g, g_ln1b, d_g1 = _rowwise(
        "post_mix_bwd", e5, S, ts, [(dh2, D, 0), (r1, D, 0), (dx1a, D, 0), (mix, D, 0)],
        [sc2, ln1_g, ln1_b, g1], [(D, F32), (D, BF16)], [D, D, D, D, D])

    (dmerged,) = mm_nt("d_mix_out", dmix, wout_full, BF16, 1)
    gw_out = mm_tn("gw_out", merged, dmix, BF16, NDEV, 0)

    def merge_b(ga, gb, vv, tt, yb, dm):
        sa, sb, st = _sigmoid(ga), _sigmoid(gb), _sigmoid(tt)
        dya = dm * sa
        return [dm * (vv * st) * sa * (1.0 - sa), dm * yb * sb * (1.0 - sb),
                dya * st, dya * vv * st * (1.0 - st), dm * sb]
    dga, dgb_, dvt_v, dvt_t, dy_b = None, None, None, None, None
    outs_b = merge_call(
        "merge_bwd", merge_b, merge_ins + [(dmerged, lambda i, j: (i, j))],
        [((S, D), lambda i, j: (i, j)), ((S, D), lambda i, j: (i, j)),
         ((S, D), lambda i, j: (i, j)), ((S, D), lambda i, j: (i, j)), ((S, D), lambda i, j: (i, j))])
    dga, dgb_, dv_, dt_, dy_b = outs_b
    dvt = jnp.stack([dv_.reshape(S, NDEV, cb), dt_.reshape(S, NDEV, cb)], axis=2).reshape(S, 2 * D)

    (dypool,) = mm_nt("d_pool_out", dy_b, wg_po, F32, NDEV)
    gw_po = mm_tn("gw_pool_out", ypool, dy_b, BF16, NDEV, 4)

    def e7(t, b):
        return [t[0] * b[0]], [_colsum(t[0] * t[1])]
    dyp, g_pscale = _rowwise("pool_scale_bwd", e7, S, ts, [(dypool, W, 0), (yp, W, 0)],
                             [pool_scale], [(W, BF16)], [W])
    (dpooled,) = _mm(
        "d_pool_mix", "nt", dyp, wp_full.astype(BF16), (S // tmp, nwin, 1),
        pl.BlockSpec((tmp, gw), lambda i, j, k: (i, j)), pl.BlockSpec((1, gw, gw), lambda i, j, k: (j, 0, 0)),
        [(_sds((S, W), F32), pl.BlockSpec((tmp, gw), lambda i, j, k: (i, j)))], (tmp, gw), 1, gw)
    tkp = _tile(S, 1024)
    gw_pool = _mm(
        "gw_pool", "tn", pooled, dyp, (nwin, 1, S // tkp),
        pl.BlockSpec((tkp, gw), lambda i, j, k: (k, i)), pl.BlockSpec((tkp, gw), lambda i, j, k: (k, i)),
        [(_sds((nwin, gw, gw), BF16), pl.BlockSpec((1, gw, gw), lambda i, j, k: (i, 0, 0)))],
        (gw, gw), 1, gw, stacked_out=True)[0]
    du_pool = pool_bwd(dpooled, gw)

    (dz,) = mm_nt("d_glu", dvt, wg_vg, BF16, NDEV)
    gw_vg = mm_tn("gw_glu", z, dvt, BF16, 2 * NDEV, 4)
    du_ssm, g_bmat, g_cmat, g_d, g_a = s5_bwd(proj, dz, bmat, cmat, ssm_d, kconst)

    dproj = jnp.concatenate([du_ssm, du_pool, dga, dgb_], axis=1)
    (dh1,) = mm_nt("d_proj", dproj, wg_in, F32, 2)
    gw_in = mm_tn("gw_in", h1, dproj, BF16, NDEV, 1)

    def e10(t, b):
        dh1v, xv, dxav = t
        xh, rs = _ln_stats(xv)
        return ([dxav + _ln_bwd(dh1v * (1.0 + b[0]), xh, rs)],
                [_colsum(dh1v * xh), _colsum(dh1v)])
    grad_x, d_sc1, d_sh1 = _rowwise("ln_mod1_bwd", e10, S, ts, [(dh1, D, 0), (x2d, D, 0), (dxa, D, 0)],
                                    [sc1], [(D, F32)], [D, D])

    gb4 = _diag_blocks(g_bmat[:, :, :GPB * P], GPB, H, P), _diag_blocks(g_bmat[:, :, GPB * P:], GPB, H, P)
    dbb = jnp.stack([jnp.transpose(t, (2, 0, 1, 3)).reshape(H, G * P) for t in gb4])
    g_bt_re, g_bt_im, g_f = s5_bbar_bwd(f2r, bt_re, bt_im, dbb)
    g_b_re = jnp.transpose(g_bt_re.reshape(H, G, P), (1, 2, 0))
    g_b_im = jnp.transpose(g_bt_im.reshape(H, G, P), (1, 2, 0))
    gc_top = _diag_blocks(g_cmat[:, :GPB * P, :], GPB, P, H)
    gc_bot = _diag_blocks(g_cmat[:, GPB * P:, :], GPB, P, H)
    g_c_re = jnp.transpose(gc_top, (0, 1, 3, 2)).reshape(G, H, P)
    g_c_im = -jnp.transpose(gc_bot, (0, 1, 3, 2)).reshape(G, H, P)
    d_ab = jnp.transpose(g_a.reshape(nblk, 2, GPB, P), (1, 0, 2, 3)).reshape(2, G, P)
    g_lr, g_li, g_ldt = s5_disc_bwd(lam_re[0], lam_im[0], log_dt[0].reshape(G, 1), d_ab,
                                    g_f.reshape(2, G, P))

    dmod = jnp.concatenate([d_sh1, d_sc1, d_g1, d_sh2, d_sc2, d_g2], axis=1)
    small_names = [b_ada, lam_re, lam_im, log_dt, ssm_b_re, ssm_b_im, ssm_c_re, ssm_c_im, ssm_d,
                   pool_scale, ln1_g, ln1_b, ln2_g, ln2_b]
    small_m = [m_b_ada, m_lam_re, m_lam_im, m_log_dt, m_ssm_b_re, m_ssm_b_im, m_ssm_c_re, m_ssm_c_im,
               m_ssm_d, m_pool_scale, m_ln1_g, m_ln1_b, m_ln2_g, m_ln2_b]
    small_v = [v_b_ada, v_lam_re, v_lam_im, v_log_dt, v_ssm_b_re, v_ssm_b_im, v_ssm_c_re, v_ssm_c_im,
               v_ssm_d, v_pool_scale, v_ln1_g, v_ln1_b, v_ln2_g, v_ln2_b]
    small_g = [dmod, g_lr, g_li, g_ldt, g_b_re, g_b_im, g_c_re, g_c_im, g_d, g_pscale,
               g_ln1g, g_ln1b, g_ln2g, g_ln2b]
    (parts_all,) = all_gather("gather_small", [_small_pack(small_g)])
    sg, sd, sm, sv = adamw_small(parts_all, _small_pack(small_names), _small_pack(small_m),
                                 _small_pack(small_v))
    shapes = [t.shape for t in small_names]
    sg, sd, sm, sv = (_small_unpack(t, shapes) for t in (sg, sd, sm, sv))

    nmod = 6 * D
    dmod_all = parts_all[:, :nmod // LANES, :].reshape(NDEV, nmod)
    c_all_t = jnp.transpose(c_all.reshape(NDEV, D))
    ada_out = adamw_ada(c_all_t, dmod_all, w_ada[0], m_w_ada[0], v_w_ada[0], my_dev)

    gw_pool_st = jnp.transpose(gw_pool.reshape(nwin, NDEV, gw // NDEV, gw), (1, 0, 2, 3))
    gw_vg_st = gw_vg.reshape(NDEV, 2, W, D // NDEV)
    grads = [gw_in, gw_vg_st, gw_pool_st, gw_po, gw_out, gw_ff1, gw_ff2]
    got = pair_exchange("pair_exchange", grads)
    parts = [pair_sum("pair_sum_%d" % i, g, t, my_c) for i, (g, t) in enumerate(zip(grads, got))]
    got3 = chip_exchange("chip_exchange", parts)

    glu_w = jnp.stack([w_glu_val[0], w_glu_gate[0]])
    glu_m = jnp.stack([m_w_glu_val[0], m_w_glu_gate[0]])
    glu_v = jnp.stack([v_w_glu_val[0], v_w_glu_gate[0]])
    wmv = [(w_in[0], m_w_in[0], v_w_in[0]), (glu_w, glu_m, glu_v), (w_pool[0], m_w_pool[0], v_w_pool[0]),
           (w_pool_out[0], m_w_pool_out[0], v_w_pool_out[0]), (w_out[0], m_w_out[0], v_w_out[0]),
           (w_ff1[0], m_w_ff1[0], v_w_ff1[0]), (w_ff2[0], m_w_ff2[0], v_w_ff2[0])]
    upd = [adamw_sharded("adamw_%d" % i, p, t, w, m, v, my_chip)
           for i, (p, t, (w, m, v)) in enumerate(zip(parts, got3, wmv))]
    u_in, u_glu, u_pool, u_po, u_out, u_ff1, u_ff2 = upd

    loss = lax.psum(loss_acc[0, 0], ("x", "y", "c"))

    def pick(k):
        return [ada_out[k][None], sg_sd[k][0], u_in[k][None]] + [t for t in sg_sd[k][1:9]] + \
               [u_glu[k][0][None], u_glu[k][1][None], u_pool[k][None], sg_sd[k][9], u_po[k][None],
                u_out[k][None], sg_sd[k][10], sg_sd[k][11], u_ff1[k][None], u_ff2[k][None],
                sg_sd[k][12], sg_sd[k][13]]

    sg_sd = [sg, sd, sm, sv]
    return (loss, grad_x[None], *pick(0), *pick(1), *pick(2), *pick(3))
```

```python
import functools
import math

import jax
import jax.numpy as jnp
from jax import lax
from jax.experimental import pallas as pl
from jax.experimental.pallas import tpu as pltpu
from jax.experimental.pallas import tpu_sc as plsc

F32 = jnp.float32
BF16 = jnp.bfloat16
MESH = pl.DeviceIdType.MESH
NDEV = 8
NCHIP = 4

SSM_GROUP = 16
SSM_STATE = 64
GROUPS_PER_BLOCK = 8
POOL_WINDOWS = (2, 4, 8, 16)
LN_EPS = 1e-5
ALPHA = 2.0 ** 0.25
ADAM_LR, ADAM_B1, ADAM_B2, ADAM_EPS, ADAM_WD, ADAM_STEP = 0.001, 0.9, 0.999, 1e-08, 0.01, 10
SUBLANES = 8
LANES = 128
VMEM_LIMIT = 56 * 1024 * 1024


def _params(sem=None, vmem=VMEM_LIMIT):
    return pltpu.CompilerParams(dimension_semantics=sem, vmem_limit_bytes=vmem)


def _tile(n, pref):
    if n <= pref:
        return n
    t = 1 << (pref.bit_length() - 1)
    while n % t:
        t //= 2
    return t


def _cast_epi(acc, ex, outs):
    outs[0][...] = acc[...].astype(outs[0].dtype)


def _mm(name, kind, a, b, grid, a_spec, b_spec, outs, acc_shape, nsub=1, c=None,
        pro=None, epi=None, extras=(), stacked_out=False):
    nk = grid[2]
    n_ex, n_out = len(extras), len(outs)
    epi_fn = epi

    def body(*refs):
        a_ref, b_ref = refs[0], refs[1]
        ex = refs[2:2 + n_ex]
        out_refs = refs[2 + n_ex:2 + n_ex + n_out]
        acc = refs[-1]
        k = pl.program_id(2)

        @pl.when(k == 0)
        def _():
            acc[...] = jnp.zeros_like(acc)

        av = a_ref[...]
        if pro is not None:
            av = pro(av)
        if kind == "nn":
            for s in range(nsub):
                acc[:, s * c:(s + 1) * c] += jnp.dot(av, b_ref[s], preferred_element_type=F32)
        elif kind == "nt":
            t = acc[...]
            for s in range(nsub):
                t = t + lax.dot_general(av[:, s * c:(s + 1) * c], b_ref[s],
                                        (((1,), (1,)), ((), ())), preferred_element_type=F32)
            acc[...] = t
        else:
            acc[...] += lax.dot_general(av, b_ref[...], (((0,), (0,)), ((), ())),
                                        preferred_element_type=F32)

        @pl.when(k == nk - 1)
        def _():
            if epi_fn is not None:
                epi_fn(acc, ex, out_refs)
            elif stacked_out:
                for s in range(nsub):
                    out_refs[0][s] = acc[:, s * c:(s + 1) * c].astype(out_refs[0].dtype)
            else:
                _cast_epi(acc, ex, out_refs)

    res = pl.pallas_call(
        body, name=name, grid=grid,
        in_specs=[a_spec, b_spec] + [e[1] for e in extras],
        out_specs=[o[1] for o in outs],
        out_shape=[o[0] for o in outs],
        scratch_shapes=[pltpu.VMEM(acc_shape, F32)],
        compiler_params=_params(("parallel", "parallel", "arbitrary")),
    )(a, b, *[e[0] for e in extras])
    return res


def _sds(shape, dtype):
    return jax.ShapeDtypeStruct(shape, dtype)


def mm_nn(name, a, b3, out_dtype, nsub, tm=1024, tk=2048, tn=None, pro=None, epi=None,
          extras=(), extra_outs=(), a_col0=0):
    M = a.shape[0]
    nb, K, cdim = b3.shape
    tm, tk = _tile(M, tm), _tile(K, tk)
    if nb == 1:
        tn = _tile(cdim, tn or 1024)
        nsub, c, nj = 1, tn, cdim // tn
        b_spec = pl.BlockSpec((1, tk, tn), lambda i, j, k: (0, k, j))
        N = cdim
    else:
        c, nj, tn = cdim, nb // nsub, nsub * cdim
        b_spec = pl.BlockSpec((nsub, tk, cdim), lambda i, j, k: (j, k, 0))
        N = nb * cdim
    kb0 = a_col0 // tk
    a_spec = pl.BlockSpec((tm, tk), lambda i, j, k: (i, kb0 + k))
    grid = (M // tm, nj, K // tk)
    o_spec = pl.BlockSpec((tm, tn), lambda i, j, k: (i, j))
    outs = [(_sds((M, N), out_dtype), o_spec)] + [(_sds((M, N), d), o_spec) for d in extra_outs]
    return _mm(name, "nn", a, b3, grid, a_spec, b_spec, outs, (tm, tn), nsub, c, pro, epi, extras)


def mm_nt(name, a, b3, out_dtype, nsub, tm=1024, tn=1024, epi=None, extras=(), extra_outs=()):
    M = a.shape[0]
    nb, N, cdim = b3.shape
    tm, tn = _tile(M, tm), _tile(N, tn)
    if nb == 1:
        tk = _tile(cdim, 2048)
        nsub, c, nk = 1, tk, cdim // tk
        b_spec = pl.BlockSpec((1, tn, tk), lambda i, j, k: (0, j, k))
    else:
        c, nk, tk = cdim, nb // nsub, nsub * cdim
        b_spec = pl.BlockSpec((nsub, tn, cdim), lambda i, j, k: (k, j, 0))
    a_spec = pl.BlockSpec((tm, tk), lambda i, j, k: (i, k))
    grid = (M // tm, N // tn, nk)
    o_spec = pl.BlockSpec((tm, tn), lambda i, j, k: (i, j))
    outs = [(_sds((M, N), out_dtype), o_spec)] + [(_sds((M, N), d), o_spec) for d in extra_outs]
    return _mm(name, "nt", a, b3, grid, a_spec, b_spec, outs, (tm, tn), nsub, c, None, epi, extras)


def mm_tn(name, a, b, out_dtype, nb, nsub, tma=1024, tk=1024, pro=None, a_col0=0, a_cols=None):
    S = a.shape[0]
    Ka = a_cols or a.shape[1]
    N = b.shape[1]
    tk = _tile(S, tk)
    if nsub == 0:
        rows = Ka // nb
        tma = rows if rows <= tma else _tile(rows, tma)
        per = rows // tma
        tn = _tile(N, 1024)
        grid = (Ka // tma, N // tn, S // tk)
        o_spec = pl.BlockSpec((1, tma, tn), lambda i, j, k: (i // per, i % per, j))
        out = _sds((nb, rows, N), out_dtype)
        nsub_k, c = 1, tn
        b_spec = pl.BlockSpec((tk, tn), lambda i, j, k: (k, j))
    else:
        c = N // nb
        tma = _tile(Ka, tma)
        grid = (Ka // tma, nb // nsub, S // tk)
        o_spec = pl.BlockSpec((nsub, tma, c), lambda i, j, k: (j, i, 0))
        out = _sds((nb, Ka, c), out_dtype)
        nsub_k = nsub
        tn = nsub * c
        b_spec = pl.BlockSpec((tk, tn), lambda i, j, k: (k, j))
    ab0 = a_col0 // tma
    a_spec = pl.BlockSpec((tk, tma), lambda i, j, k: (k, ab0 + i))
    return _mm(name, "tn", a, b, grid, a_spec, b_spec, [(out, o_spec)], (tma, tn), nsub_k, c,
               pro, None, (), stacked_out=True)[0]


def _rowwise(name, fn, S, ts, tiled, bcast, tiled_out, acc_out):
    nt, nb, no, na = len(tiled), len(bcast), len(tiled_out), len(acc_out)

    def body(*refs):
        tin = [r[...] for r in refs[:nt]]
        bin_ = [r[...] for r in refs[nt:nt + nb]]
        o_refs = refs[nt + nb:nt + nb + no]
        a_refs = refs[nt + nb + no:]
        touts, aouts = fn(tin, bin_)
        for r, v in zip(o_refs, touts):
            r[...] = v.astype(r.dtype)
        i = pl.program_id(0)

        @pl.when(i == 0)
        def _():
            for r, v in zip(a_refs, aouts):
                r[...] = v

        @pl.when(i > 0)
        def _():
            for r, v in zip(a_refs, aouts):
                r[...] += v

    in_specs = [pl.BlockSpec((ts, w), functools.partial(lambda i, cb: (i, cb), cb=cb))
                for (_, w, cb) in tiled]
    in_specs += [pl.BlockSpec(b.shape, lambda i: (0, 0)) for b in bcast]
    out_specs = [pl.BlockSpec((ts, w), lambda i: (i, 0)) for (w, _) in tiled_out]
    out_specs += [pl.BlockSpec((1, w), lambda i: (0, 0)) for w in acc_out]
    out_shape = [_sds((S, w), d) for (w, d) in tiled_out] + [_sds((1, w), F32) for w in acc_out]
    return pl.pallas_call(
        body, name=name, grid=(S // ts,), in_specs=in_specs, out_specs=out_specs,
        out_shape=out_shape, compiler_params=_params(("arbitrary",)),
    )(*[t[0] for t in tiled], *bcast)


def _ln_stats(v):
    mu = jnp.mean(v, axis=-1, keepdims=True)
    vc = v - mu
    var = jnp.mean(vc * vc, axis=-1, keepdims=True)
    rstd = lax.rsqrt(var + LN_EPS)
    return vc * rstd, rstd


def _ln_bwd(dxhat, xhat, rstd):
    return rstd * (dxhat - jnp.mean(dxhat, axis=-1, keepdims=True)
                   - xhat * jnp.mean(dxhat * xhat, axis=-1, keepdims=True))


def _colsum(v):
    return jnp.sum(v, axis=0, keepdims=True)


def _sigmoid(v):
    return 1.0 / (1.0 + jnp.exp(-v))


_GELU_C = math.sqrt(2.0 / math.pi)


def _gelu(v):
    return 0.5 * v * (1.0 + jnp.tanh(_GELU_C * (v + 0.044715 * v * v * v)))


def _gelu_grad(v):
    t = jnp.tanh(_GELU_C * (v + 0.044715 * v * v * v))
    return 0.5 * (1.0 + t) + 0.5 * v * (1.0 - t * t) * _GELU_C * (1.0 + 3 * 0.044715 * v * v)


def _disc(lr, li, ldt):
    dt = jnp.exp(ldt)
    mag = jnp.exp(lr * dt)
    ang = li * dt
    ab_re = mag * jnp.cos(ang)
    ab_im = mag * jnp.sin(ang)
    num_re = ab_re - 1.0
    num_im = ab_im
    den = lr * lr + li * li
    f_re = (num_re * lr + num_im * li) / den
    f_im = (num_im * lr - num_re * li) / den
    return ab_re, ab_im, f_re, f_im


def _cmul(ar, ai, br, bi):
    return ar * br - ai * bi, ar * bi + ai * br


def s5_disc(lam_re, lam_im, log_dt):
    G, P = lam_re.shape

    def body(lr_ref, li_ref, ldt_ref, f_ref, k_ref):
        ab_re, ab_im, f_re, f_im = _disc(lr_ref[...], li_ref[...], ldt_ref[...])
        f_ref[0] = f_re
        f_ref[1] = f_im
        pr, pi = [ab_re], [ab_im]
        for _ in range(SUBLANES - 1):
            nr, ni = _cmul(pr[-1], pi[-1], ab_re, ab_im)
            pr.append(nr)
            pi.append(ni)
        zero = jnp.zeros_like(ab_re)
        for n, sh in enumerate((1, 2, 4)):
            for r in range(SUBLANES):
                k_ref[2 * n, r] = pr[sh - 1] if r >= sh else zero
                k_ref[2 * n + 1, r] = pi[sh - 1] if r >= sh else zero
                k_ref[8 + 2 * n, r] = pr[sh - 1] if r + sh < SUBLANES else zero
                k_ref[8 + 2 * n + 1, r] = -pi[sh - 1] if r + sh < SUBLANES else zero
        for r in range(SUBLANES):
            k_ref[6, r] = pr[r]
            k_ref[7, r] = pi[r]
            k_ref[14, r] = pr[SUBLANES - 1 - r]
            k_ref[15, r] = -pi[SUBLANES - 1 - r]

    vm = pl.BlockSpec(memory_space=pltpu.VMEM)
    return pl.pallas_call(
        body, name="s5_disc", in_specs=[vm, vm, vm], out_specs=[vm, vm],
        out_shape=[_sds((2, G, P), F32), _sds((16, SUBLANES, G, P), F32)],
    )(lam_re, lam_im, log_dt)


def s5_disc_bwd(lam_re, lam_im, log_dt, d_ab, d_f):
    G, P = lam_re.shape

    def body(lr_ref, li_ref, ldt_ref, dab_ref, df_ref, glr_ref, gli_ref, gdt_ref):
        _, vjp = jax.vjp(_disc, lr_ref[...], li_ref[...], ldt_ref[...])
        glr, gli, gdt = vjp((dab_ref[0], dab_ref[1], df_ref[0], df_ref[1]))
        glr_ref[...] = glr
        gli_ref[...] = gli
        gdt_ref[...] = gdt

    vm = pl.BlockSpec(memory_space=pltpu.VMEM)
    return pl.pallas_call(
        body, name="s5_disc_bwd", in_specs=[vm] * 5, out_specs=[vm] * 3,
        out_shape=[_sds((G, P), F32), _sds((G, P), F32), _sds((G, 1), F32)],
    )(lam_re, lam_im, log_dt, d_ab, d_f)


def s5_bbar(f2, bt_re, bt_im):
    def body(f_ref, br_ref, bi_ref, o_ref):
        fr, fi = f_ref[0], f_ref[1]
        br, bi = br_ref[...], bi_ref[...]
        o_ref[0] = fr * br - fi * bi
        o_ref[1] = fr * bi + fi * br

    vm = pl.BlockSpec(memory_space=pltpu.VMEM)
    return pl.pallas_call(body, name="s5_bbar", in_specs=[vm] * 3, out_specs=vm,
                          out_shape=_sds((2,) + bt_re.shape, F32))(f2, bt_re, bt_im)


def s5_bbar_bwd(f2, bt_re, bt_im, dbb):
    def body(f_ref, br_ref, bi_ref, d_ref, gbr_ref, gbi_ref, gf_ref):
        fr, fi = f_ref[0], f_ref[1]
        br, bi = br_ref[...], bi_ref[...]
        dr, di = d_ref[0], d_ref[1]
        gbr_ref[...] = fr * dr + fi * di
        gbi_ref[...] = fr * di - fi * dr
        gf_ref[0] = _colsum(dr * br + di * bi)
        gf_ref[1] = _colsum(di * br - dr * bi)

    vm = pl.BlockSpec(memory_space=pltpu.VMEM)
    return pl.pallas_call(
        body, name="s5_bbar_bwd", in_specs=[vm] * 4, out_specs=[vm] * 3,
        out_shape=[_sds(bt_re.shape, F32), _sds(bt_re.shape, F32), _sds(f2.shape, F32)],
    )(f2, bt_re, bt_im, dbb)


def _scan_fwd(xs, k_ref, nst):
    ntile = xs.shape[0] // SUBLANES

    def step(t, carry):
        cr, ci = carry
        r0 = pl.multiple_of(t * SUBLANES, SUBLANES)
        xr = xs[pl.ds(r0, SUBLANES), 0:nst]
        xi = xs[pl.ds(r0, SUBLANES), nst:2 * nst]
        for n, sh in enumerate((1, 2, 4)):
            sr = pltpu.roll(xr, sh, 0)
            si = pltpu.roll(xi, sh, 0)
            mr, mi = k_ref[2 * n], k_ref[2 * n + 1]
            xr, xi = xr + mr * sr - mi * si, xi + mr * si + mi * sr
        pr, pi = k_ref[6], k_ref[7]
        xr, xi = xr + pr * cr - pi * ci, xi + pr * ci + pi * cr
        xs[pl.ds(r0, SUBLANES), 0:nst] = xr
        xs[pl.ds(r0, SUBLANES), nst:2 * nst] = xi
        return (jnp.broadcast_to(xr[SUBLANES - 1:SUBLANES, :], xr.shape),
                jnp.broadcast_to(xi[SUBLANES - 1:SUBLANES, :], xi.shape))

    zero = jnp.zeros((SUBLANES, nst), F32)
    lax.fori_loop(0, ntile, step, (zero, zero))


def _scan_bwd(g, xs, k_ref, nst):
    ntile = g.shape[0] // SUBLANES
    row = lax.broadcasted_iota(jnp.int32, (SUBLANES, nst), 0)

    def step(tt, carry):
        cr, ci, ar, ai = carry
        t = ntile - 1 - tt
        r0 = pl.multiple_of(t * SUBLANES, SUBLANES)
        gr = g[pl.ds(r0, SUBLANES), 0:nst]
        gi = g[pl.ds(r0, SUBLANES), nst:2 * nst]
        for n, sh in enumerate((1, 2, 4)):
            sr = pltpu.roll(gr, SUBLANES - sh, 0)
            si = pltpu.roll(gi, SUBLANES - sh, 0)
            mr, mi = k_ref[8 + 2 * n], k_ref[8 + 2 * n + 1]
            gr, gi = gr + mr * sr - mi * si, gi + mr * si + mi * sr
        qr, qi = k_ref[14], k_ref[15]
        gr, gi = gr + qr * cr - qi * ci, gi + qr * ci + qi * cr
        g[pl.ds(r0, SUBLANES), 0:nst] = gr
        g[pl.ds(r0, SUBLANES), nst:2 * nst] = gi
        p0 = pl.multiple_of(jnp.maximum(t - 1, 0) * SUBLANES, SUBLANES)
        live = (t > 0).astype(F32)
        xr = xs[pl.ds(r0, SUBLANES), 0:nst]
        xi = xs[pl.ds(r0, SUBLANES), nst:2 * nst]
        pr = xs[pl.ds(p0, SUBLANES), 0:nst][SUBLANES - 1:SUBLANES, :] * live
        pi = xs[pl.ds(p0, SUBLANES), nst:2 * nst][SUBLANES - 1:SUBLANES, :] * live
        xmr = jnp.where(row == 0, jnp.broadcast_to(pr, xr.shape), pltpu.roll(xr, 1, 0))
        xmi = jnp.where(row == 0, jnp.broadcast_to(pi, xi.shape), pltpu.roll(xi, 1, 0))
        ar = ar + gr * xmr + gi * xmi
        ai = ai + gi * xmr - gr * xmi
        return (jnp.broadcast_to(gr[0:1, :], gr.shape), jnp.broadcast_to(gi[0:1, :], gi.shape),
                ar, ai)

    zero = jnp.zeros((SUBLANES, nst), F32)
    _, _, ar, ai = lax.fori_loop(0, ntile, step, (zero, zero, zero, zero))
    return _colsum(ar), _colsum(ai)


def s5_fwd(proj, bmat, cmat, dskip, kconst):
    S = proj.shape[0]
    nb, cw, nst2 = bmat.shape
    nst = nst2 // 2

    def body(u_ref, b_ref, c_ref, d_ref, k_ref, z_ref, xs):
        u = u_ref[...]
        xs[...] = jnp.dot(u.astype(BF16), b_ref[0], preferred_element_type=F32)
        _scan_fwd(xs, k_ref, nst)
        y = jnp.dot(xs[...].astype(BF16), c_ref[0], preferred_element_type=F32) + d_ref[...] * u
        z_ref[...] = _gelu(y).astype(BF16)

    return pl.pallas_call(
        body, name="s5_fwd", grid=(nb,),
        in_specs=[pl.BlockSpec((S, cw), lambda b: (0, b)),
                  pl.BlockSpec((1, cw, nst2), lambda b: (b, 0, 0)),
                  pl.BlockSpec((1, nst2, cw), lambda b: (b, 0, 0)),
                  pl.BlockSpec((1, cw), lambda b: (0, b)),
                  pl.BlockSpec((16, SUBLANES, nst), lambda b: (0, 0, b))],
        out_specs=pl.BlockSpec((S, cw), lambda b: (0, b)),
        out_shape=_sds((S, nb * cw), BF16),
        scratch_shapes=[pltpu.VMEM((S, nst2), F32)],
        compiler_params=_params(("arbitrary",)),
    )(proj, bmat, cmat, dskip, kconst)


def s5_bwd(proj, dz, bmat, cmat, dskip, kconst):
    S = proj.shape[0]
    nb, cw, nst2 = bmat.shape
    nst = nst2 // 2

    def body(u_ref, dz_ref, b_ref, c_ref, d_ref, k_ref, du_ref, gb_ref, gc_ref, gd_ref, ga_ref,
             xs, g):
        u = u_ref[...]
        ub = u.astype(BF16)
        bm, cm, d = b_ref[0], c_ref[0], d_ref[...]
        xs[...] = jnp.dot(ub, bm, preferred_element_type=F32)
        _scan_fwd(xs, k_ref, nst)
        xsb = xs[...].astype(BF16)
        y = jnp.dot(xsb, cm, preferred_element_type=F32) + d * u
        dy = dz_ref[...].astype(F32) * _gelu_grad(y)
        gd_ref[...] = _colsum(dy * u)
        dyb = dy.astype(BF16)
        gc_ref[0] = lax.dot_general(xsb, dyb, (((0,), (0,)), ((), ())), preferred_element_type=F32)
        g[...] = lax.dot_general(dyb, cm, (((1,), (1,)), ((), ())), preferred_element_type=F32)
        ar, ai = _scan_bwd(g, xs, k_ref, nst)
        ga_ref[0, 0:1, :] = ar
        ga_ref[0, 1:2, :] = ai
        gb = g[...].astype(BF16)
        du = lax.dot_general(gb, bm, (((1,), (1,)), ((), ())), preferred_element_type=F32) + d * dy
        du_ref[...] = du.astype(BF16)
        gb_ref[0] = lax.dot_general(ub, gb, (((0,), (0,)), ((), ())), preferred_element_type=F32)

    return pl.pallas_call(
        body, name="s5_bwd", grid=(nb,),
        in_specs=[pl.BlockSpec((S, cw), lambda b: (0, b)),
                  pl.BlockSpec((S, cw), lambda b: (0, b)),
                  pl.BlockSpec((1, cw, nst2), lambda b: (b, 0, 0)),
                  pl.BlockSpec((1, nst2, cw), lambda b: (b, 0, 0)),
                  pl.BlockSpec((1, cw), lambda b: (0, b)),
                  pl.BlockSpec((16, SUBLANES, nst), lambda b: (0, 0, b))],
        out_specs=[pl.BlockSpec((S, cw), lambda b: (0, b)),
                   pl.BlockSpec((1, cw, nst2), lambda b: (b, 0, 0)),
                   pl.BlockSpec((1, nst2, cw), lambda b: (b, 0, 0)),
                   pl.BlockSpec((1, cw), lambda b: (0, b)),
                   pl.BlockSpec((1, 2, nst), lambda b: (b, 0, 0))],
        out_shape=[_sds((S, nb * cw), BF16), _sds((nb, cw, nst2), F32), _sds((nb, nst2, cw), F32),
                   _sds((1, nb * cw), F32), _sds((nb, 2, nst), F32)],
        scratch_shapes=[pltpu.VMEM((S, nst2), F32), pltpu.VMEM((S, nst2), F32)],
        compiler_params=_params(("arbitrary",)),
    )(proj, dz, bmat, cmat, dskip, kconst)


def _shift_rows(v, k, row, down):
    n = v.shape[0]
    if down:
        return jnp.where(row >= k, pltpu.roll(v, k, 0), 0.0)
    return jnp.where(row < n - k, pltpu.roll(v, n - k, 0), 0.0)


def _window(v, gi, row, down):
    sums = []
    s = v
    for k in (1, 2, 4, 8):
        s = s + _shift_rows(s, k, row, down)
        sums.append(s)
    out = sums[3]
    for n in (2, 1, 0):
        out = jnp.where(gi == n, sums[n], out)
    return out


def pool_fwd(proj, col0, width, gw):
    S = proj.shape[0]
    cb0 = col0 // gw

    def body(u_ref, o_ref):
        gi = pl.program_id(0)
        u = u_ref[...]
        row = lax.broadcasted_iota(jnp.int32, u.shape, 0)
        w = jnp.left_shift(2, gi)
        count = jnp.minimum(row + 1, w).astype(F32)
        o_ref[...] = (_window(u, gi, row, True) / count - u).astype(BF16)

    return pl.pallas_call(
        body, name="pool_fwd", grid=(len(POOL_WINDOWS),),
        in_specs=[pl.BlockSpec((S, gw), lambda g: (0, cb0 + g))],
        out_specs=pl.BlockSpec((S, gw), lambda g: (0, g)),
        out_shape=_sds((S, width), BF16), compiler_params=_params(("arbitrary",)),
    )(proj)


def pool_bwd(dpooled, gw):
    S, width = dpooled.shape

    def body(d_ref, o_ref):
        gi = pl.program_id(0)
        d = d_ref[...]
        row = lax.broadcasted_iota(jnp.int32, d.shape, 0)
        w = jnp.left_shift(2, gi)
        count = jnp.minimum(row + 1, w).astype(F32)
        o_ref[...] = (_window(d / count, gi, row, False) - d).astype(BF16)

    return pl.pallas_call(
        body, name="pool_bwd", grid=(len(POOL_WINDOWS),),
        in_specs=[pl.BlockSpec((S, gw), lambda g: (0, g))],
        out_specs=pl.BlockSpec((S, gw), lambda g: (0, g)),
        out_shape=_sds((S, width), BF16), compiler_params=_params(("arbitrary",)),
    )(dpooled)


def _place():
    x, y, c = lax.axis_index("x"), lax.axis_index("y"), lax.axis_index("c")
    chips = [(1 - x, y), (x, 1 - y), (1 - x, 1 - y)]
    return x, y, c, chips


HBM = pl.BlockSpec(memory_space=pltpu.HBM)


def _gather_body(n, handshake):
    def body(*refs):
        ins, outs = refs[:n], refs[n:2 * n]
        send_sems, recv_sems, local_sems = refs[2 * n:]
        x, y, c, chips = _place()
        if handshake:
            barrier = pltpu.get_barrier_semaphore()
            for peer in [(x, y, 1 - c)] + [(*chip, c) for chip in chips]:
                pl.semaphore_signal(barrier, inc=1, device_id=peer, device_id_type=MESH)
            pl.semaphore_wait(barrier, 4)
        me, sibling = (x, y, c), (x, y, 1 - c)

        def slot(i, p):
            return outs[i].at[4 * p[0] + 2 * p[1] + p[2]]

        def copy(i, k, block, to, src=None):
            return pltpu.make_async_remote_copy(
                src_ref=slot(i, block) if src is None else src, dst_ref=slot(i, block),
                send_sem=send_sems.at[i, k], recv_sem=recv_sems.at[i, k],
                device_id=to, device_id_type=MESH)

        started = []
        for i in range(n):
            for j, chip in enumerate(chips):
                started.append(copy(i, 1 + j, me, (*chip, c), src=ins[i]))
                started[-1].start()
        for i in range(n):
            started.append(copy(i, 0, me, sibling, src=ins[i]))
            started[-1].start()
        mine = [pltpu.make_async_copy(ins[i], slot(i, me), local_sems.at[i]) for i in range(n)]
        for cp in mine:
            cp.start()
        for i in range(n):
            for j, chip in enumerate(chips):
                copy(i, 1 + j, (*chip, c), me).wait_recv()
                started.append(copy(i, 4 + j, (*chip, c), sibling))
                started[-1].start()
        for i in range(n):
            copy(i, 0, sibling, me).wait_recv()
            for j, chip in enumerate(chips):
                copy(i, 4 + j, (*chip, 1 - c), me).wait_recv()
        for cp in started:
            cp.wait_send()
        for cp in mine:
            cp.wait()

    return body


def all_gather(name, shards):
    n = len(shards)
    return pl.pallas_call(
        _gather_body(n, False), name=name, in_specs=[HBM] * n, out_specs=[HBM] * n,
        out_shape=[_sds((NDEV,) + s.shape, s.dtype) for s in shards],
        scratch_shapes=[pltpu.SemaphoreType.DMA((n, 7)), pltpu.SemaphoreType.DMA((n, 7)),
                        pltpu.SemaphoreType.DMA((n,))],
    )(*shards)


def seq_all_gather(name, shards, collective_id):
    n = len(shards)
    ins = [jax.new_ref(s, memory_space=pltpu.MemorySpace.HBM) for s in shards]
    outs = [jax.empty_ref(_sds((NDEV,) + s.shape, s.dtype), memory_space=pltpu.MemorySpace.HBM)
            for s in shards]
    body = _gather_body(n, True)

    @pl.kernel(mesh=plsc.ScalarSubcoreMesh(axis_name="sequencer", num_cores=1), name=name,
               scratch_types=(pltpu.SemaphoreType.DMA((n, 7)), pltpu.SemaphoreType.DMA((n, 7)),
                              pltpu.SemaphoreType.DMA((n,))),
               compiler_params=pltpu.CompilerParams(collective_id=collective_id))
    def launch(send_sems, recv_sems, local_sems):
        body(*ins, *outs, send_sems, recv_sems, local_sems)

    launch()
    return [o[...] for o in outs]


def pair_exchange(name, grads):
    n = len(grads)

    def body(*refs):
        ins, outs = refs[:n], refs[n:2 * n]
        send_sems, recv_sems = refs[2 * n:]
        x, y, c, _ = _place()
        sibling = (x, y, 1 - c)
        cps = []
        for i in range(n):
            for q in range(NCHIP):
                cps.append(pltpu.make_async_remote_copy(
                    src_ref=ins[i].at[2 * q + 1 - c], dst_ref=outs[i].at[q],
                    send_sem=send_sems.at[i, q], recv_sem=recv_sems.at[i, q],
                    device_id=sibling, device_id_type=MESH))
                cps[-1].start()
        for cp in cps:
            cp.wait()

    return pl.pallas_call(
        body, name=name, in_specs=[HBM] * n, out_specs=[HBM] * n,
        out_shape=[_sds((NCHIP,) + g.shape[1:], g.dtype) for g in grads],
        scratch_shapes=[pltpu.SemaphoreType.DMA((n, NCHIP)), pltpu.SemaphoreType.DMA((n, NCHIP))],
    )(*grads)


def pair_sum(name, grad, got, my_c):
    shp = grad.shape[1:]
    r, cdim = shp[-2], shp[-1]
    lead = int(math.prod(shp[:-2])) if len(shp) > 2 else 1
    g5 = grad.reshape(NCHIP, 2, lead * r, cdim)
    t4 = got.reshape(NCHIP, lead * r, cdim)
    R = lead * r
    tr = _tile(R, max(8, (1 << 20) // cdim))

    def body(c_ref, g_ref, t_ref, o_ref):
        o_ref[...] = (g_ref[0].astype(F32) + t_ref[...].astype(F32)).astype(o_ref.dtype)

    out = pl.pallas_call(
        body, name=name,
        grid_spec=pltpu.PrefetchScalarGridSpec(
            num_scalar_prefetch=1, grid=(NCHIP, R // tr),
            in_specs=[pl.BlockSpec((1, 1, tr, cdim), lambda q, i, cr: (q, cr[0], i, 0)),
                      pl.BlockSpec((1, tr, cdim), lambda q, i, cr: (q, i, 0))],
            out_specs=pl.BlockSpec((1, tr, cdim), lambda q, i, cr: (q, i, 0))),
        out_shape=_sds((NCHIP, R, cdim), grad.dtype),
        compiler_params=_params(("parallel", "parallel")),
    )(my_c, g5, t4)
    return out


def chip_exchange(name, parts):
    n = len(parts)

    def body(*refs):
        ins, outs = refs[:n], refs[n:2 * n]
        send_sems, recv_sems = refs[2 * n:]
        x, y, c, chips = _place()
        cps = []
        for i in range(n):
            for j, chip in enumerate(chips):
                cps.append(pltpu.make_async_remote_copy(
                    src_ref=ins[i].at[2 * chip[0] + chip[1]], dst_ref=outs[i].at[j],
                    send_sem=send_sems.at[i, j], recv_sem=recv_sems.at[i, j],
                    device_id=(*chip, c), device_id_type=MESH))
                cps[-1].start()
        for cp in cps:
            cp.wait()

    return pl.pallas_call(
        body, name=name, in_specs=[HBM] * n, out_specs=[HBM] * n,
        out_shape=[_sds((3,) + p.shape[1:], p.dtype) for p in parts],
        scratch_shapes=[pltpu.SemaphoreType.DMA((n, 3)), pltpu.SemaphoreType.DMA((n, 3))],
    )(*parts)


def ada_fwd(c_row, w_ada, b_ada):
    D, cols = w_ada.shape

    def body(c_ref, w_ref, b_ref, mod_ref, call_ref, act8, part, s1, r1, s2, r2):
        x, y, c, _ = _place()
        me = 4 * x + 2 * y + c
        call_ref[me] = c_ref[...]
        cps = []
        for k in range(1, NDEV):
            to = (x ^ (k >> 2), y ^ ((k >> 1) & 1), c ^ (k & 1))
            cps.append(pltpu.make_async_remote_copy(
                src_ref=c_ref, dst_ref=call_ref.at[me], send_sem=s1.at[k - 1],
                recv_sem=r1.at[k - 1], device_id=to, device_id_type=MESH))
            cps[-1].start()
        for cp in cps:
            cp.wait()
        for b in range(NDEV):
            act8[b:b + 1, :] = call_ref[b]
        cv = act8[...]
        act = (cv * _sigmoid(cv)).astype(BF16)
        res = jnp.dot(act, w_ref[...].astype(BF16), preferred_element_type=F32)
        for b in range(NDEV):
            part[b] = res[b:b + 1, :]
        mod_ref[me] = part[me]
        cps = []
        for k in range(1, NDEV):
            to = (x ^ (k >> 2), y ^ ((k >> 1) & 1), c ^ (k & 1))
            dst = 4 * to[0] + 2 * to[1] + to[2]
            cps.append(pltpu.make_async_remote_copy(
                src_ref=part.at[dst], dst_ref=mod_ref.at[me], send_sem=s2.at[k - 1],
                recv_sem=r2.at[k - 1], device_id=to, device_id_type=MESH))
            cps[-1].start()
        for cp in cps:
            cp.wait()
        for b in range(NDEV):
            mod_ref[b] = mod_ref[b] + b_ref[b]

    vm = pl.BlockSpec(memory_space=pltpu.VMEM)
    return pl.pallas_call(
        body, name="ada_fwd", in_specs=[vm, vm, vm], out_specs=[vm, vm],
        out_shape=[_sds((NDEV, 1, cols), F32), _sds((NDEV, 1, D), F32)],
        scratch_shapes=[pltpu.VMEM((NDEV, D), F32), pltpu.VMEM((NDEV, 1, cols), F32),
                        pltpu.SemaphoreType.DMA((NDEV - 1,)), pltpu.SemaphoreType.DMA((NDEV - 1,)),
                        pltpu.SemaphoreType.DMA((NDEV - 1,)), pltpu.SemaphoreType.DMA((NDEV - 1,))],
        compiler_params=pltpu.CompilerParams(vmem_limit_bytes=VMEM_LIMIT),
    )(c_row, w_ada, b_ada.reshape(NDEV, 1, cols))


def _adamw_math(g, w, m, v):
    m2 = ADAM_B1 * m + (1.0 - ADAM_B1) * g
    v2 = ADAM_B2 * v + (1.0 - ADAM_B2) * (g * g)
    m_hat = m2 / (1.0 - ADAM_B1 ** ADAM_STEP)
    v_hat = v2 / (1.0 - ADAM_B2 ** ADAM_STEP)
    delta = -ADAM_LR * (m_hat / (jnp.sqrt(v_hat) + ADAM_EPS) + ADAM_WD * w)
    return delta, m2, v2


def adamw_sharded(name, part4, got3, w, m, v, my_chip):
    shape = w.shape
    cdim = shape[-1]
    R = int(math.prod(shape[:-1]))
    w2, m2, v2 = (t.reshape(R, cdim) for t in (w, m, v))
    tr = _tile(R, max(8, (1 << 19) // cdim))

    def body(q_ref, p_ref, t_ref, w_ref, m_ref, v_ref, g_out, d_out, m_out, v_out):
        g = p_ref[0].astype(F32)
        for j in range(3):
            g = g + t_ref[j].astype(F32)
        d, mn, vn = _adamw_math(g, w_ref[...], m_ref[...], v_ref[...])
        g_out[...] = g
        d_out[...] = d
        m_out[...] = mn
        v_out[...] = vn

    spec = pl.BlockSpec((tr, cdim), lambda i, qr: (i, 0))
    outs = pl.pallas_call(
        body, name=name,
        grid_spec=pltpu.PrefetchScalarGridSpec(
            num_scalar_prefetch=1, grid=(R // tr,),
            in_specs=[pl.BlockSpec((1, tr, cdim), lambda i, qr: (qr[0], i, 0)),
                      pl.BlockSpec((3, tr, cdim), lambda i, qr: (0, i, 0)), spec, spec, spec],
            out_specs=[spec] * 4),
        out_shape=[_sds((R, cdim), F32)] * 4,
        compiler_params=_params(("parallel",)),
    )(my_chip, part4.reshape(NCHIP, R, cdim), got3.reshape(3, R, cdim), w2, m2, v2)
    return [o.reshape(shape) for o in outs]


def adamw_small(parts, w, m, v):
    R = w.shape[0]
    tr = R

    def body(p_ref, w_ref, m_ref, v_ref, g_out, d_out, m_out, v_out):
        g = p_ref[0]
        for j in range(1, NDEV):
            g = g + p_ref[j]
        d, mn, vn = _adamw_math(g, w_ref[...], m_ref[...], v_ref[...])
        g_out[...] = g
        d_out[...] = d
        m_out[...] = mn
        v_out[...] = vn

    spec = pl.BlockSpec((tr, LANES), lambda i: (i, 0))
    return pl.pallas_call(
        body, name="adamw_small", grid=(R // tr,),
        in_specs=[pl.BlockSpec((NDEV, tr, LANES), lambda i: (0, i, 0)), spec, spec, spec],
        out_specs=[spec] * 4, out_shape=[_sds((R, LANES), F32)] * 4,
        compiler_params=_params(("parallel",)),
    )(parts, w, m, v)


def adamw_ada(c_all_t, dmod_all, w, m, v, my_dev):
    D, cols = w.shape
    tr = _tile(D, 256)

    def body(k_ref, c_ref, d_ref, w_ref, m_ref, v_ref, g_out, d_out, m_out, v_out):
        cv = c_ref[...]
        act = cv * _sigmoid(cv)
        dm = d_ref[...]
        g = act[:, 0:1] * dm[0:1, :]
        for b in range(1, NDEV):
            g = g + act[:, b:b + 1] * dm[b:b + 1, :]
        d, mn, vn = _adamw_math(g, w_ref[...], m_ref[...], v_ref[...])
        g_out[...] = g
        d_out[...] = d
        m_out[...] = mn
        v_out[...] = vn

    spec = pl.BlockSpec((tr, cols), lambda i, kr: (i, 0))
    return pl.pallas_call(
        body, name="adamw_ada",
        grid_spec=pltpu.PrefetchScalarGridSpec(
            num_scalar_prefetch=1, grid=(D // tr,),
            in_specs=[pl.BlockSpec((tr, NDEV), lambda i, kr: (i, 0)),
                      pl.BlockSpec((NDEV, cols), lambda i, kr: (0, kr[0])), spec, spec, spec],
            out_specs=[spec] * 4),
        out_shape=[_sds((D, cols), F32)] * 4,
        compiler_params=_params(("parallel",)),
    )(my_dev, c_all_t, dmod_all, w, m, v)


def _blockdiag(t, eye):
    nb, gpb, R, C = t.shape
    return jnp.einsum("bgrc,gk->bgrkc", t, eye).reshape(nb, gpb * R, gpb * C)


def _diag_blocks(t, gpb, R, C):
    nb = t.shape[0]
    t5 = t.reshape(nb, gpb, R, gpb, C)
    idx = jnp.arange(gpb)
    return jnp.moveaxis(t5[:, idx, :, idx, :], 0, 1)


def _small_pack(parts):
    rows = []
    for p in parts:
        flat = p.reshape(-1)
        flat = jnp.pad(flat, (0, (-flat.shape[0]) % (SUBLANES * LANES)))
        rows.append(flat.reshape(-1, LANES))
    return jnp.concatenate(rows, axis=0)


def _small_unpack(buf, shapes):
    out, r = [], 0
    for s in shapes:
        n = int(math.prod(s))
        nr = -(-n // (SUBLANES * LANES)) * SUBLANES
        out.append(buf[r:r + nr].reshape(-1)[:n].reshape(s))
        r += nr
    return out


def kernel(x, c, w_ada, b_ada, w_in, lam_re, lam_im, log_dt, ssm_b_re, ssm_b_im, ssm_c_re, ssm_c_im, ssm_d, w_glu_val, w_glu_gate, w_pool, pool_scale, w_pool_out, w_out, ln1_g, ln1_b, w_ff1, w_ff2, ln2_g, ln2_b, loss_target, m_w_ada, m_b_ada, m_w_in, m_lam_re, m_lam_im, m_log_dt, m_ssm_b_re, m_ssm_b_im, m_ssm_c_re, m_ssm_c_im, m_ssm_d, m_w_glu_val, m_w_glu_gate, m_w_pool, m_pool_scale, m_w_pool_out, m_w_out, m_ln1_g, m_ln1_b, m_w_ff1, m_w_ff2, m_ln2_g, m_ln2_b, v_w_ada, v_b_ada, v_w_in, v_lam_re, v_lam_im, v_log_dt, v_ssm_b_re, v_ssm_b_im, v_ssm_c_re, v_ssm_c_im, v_ssm_d, v_w_glu_val, v_w_glu_gate, v_w_pool, v_pool_scale, v_w_pool_out, v_w_out, v_ln1_g, v_ln1_b, v_w_ff1, v_w_ff2, v_ln2_g, v_ln2_b):
    S, D = x.shape[1], x.shape[2]
    x2d, tgt = x[0], loss_target[0]
    W = D // 2
    G = W // SSM_GROUP
    P, H, GPB = SSM_STATE, SSM_GROUP, GROUPS_PER_BLOCK
    nblk = G // GPB
    gw = W // len(POOL_WINDOWS)
    ax, ay, ac = lax.axis_index("x"), lax.axis_index("y"), lax.axis_index("c")
    my_c = ac.astype(jnp.int32).reshape(1)
    my_chip = (2 * ax + ay).astype(jnp.int32).reshape(1)
    my_dev = (4 * ax + 2 * ay + ac).astype(jnp.int32).reshape(1)
    ts = _tile(S, 256)

    glu = jnp.stack([w_glu_val[0], w_glu_gate[0]]).astype(BF16)
    shards = [w_in[0].astype(BF16), glu, w_pool[0].astype(BF16), w_pool_out[0].astype(BF16),
              w_out[0].astype(BF16), w_ff1[0].astype(BF16), w_ff2[0].astype(BF16)]
    (wg_in,) = seq_all_gather("gather_w_in", shards[0:1], 1)
    wg_vg, wg_pool, wg_po, wg_out = seq_all_gather("gather_w_mix", shards[1:5], 2)
    wg_ff1, wg_ff2 = seq_all_gather("gather_w_ff", shards[5:7], 3)
    wg_vg = wg_vg.reshape(2 * NDEV, W, D // NDEV)
    nwin = len(POOL_WINDOWS)
    wp_full = jnp.transpose(wg_pool, (1, 0, 2, 3)).reshape(nwin, gw, gw)
    wout_full = wg_out.reshape(1, D, D)
    wff2_full = wg_ff2.reshape(1, 4 * D, D)

    mod, c_all = ada_fwd(c, w_ada[0], b_ada)
    mod = mod.reshape(6, 1, D)
    sh1, sc1, g1, sh2, sc2, g2 = (mod[i] for i in range(6))

    f2, kconst = s5_disc(lam_re[0], lam_im[0], log_dt[0].reshape(G, 1))
    kconst = kconst.reshape(16, SUBLANES, G * P)
    f2r = f2.reshape(2, 1, G * P)
    bt_re = jnp.transpose(ssm_b_re[0], (2, 0, 1)).reshape(H, G * P)
    bt_im = jnp.transpose(ssm_b_im[0], (2, 0, 1)).reshape(H, G * P)
    bbar = s5_bbar(f2r, bt_re, bt_im)
    eye = jnp.eye(GPB, dtype=F32)
    bb4 = jnp.transpose(bbar.reshape(2, H, nblk, GPB, P), (0, 2, 3, 1, 4))
    bmat = jnp.concatenate([_blockdiag(bb4[0], eye), _blockdiag(bb4[1], eye)], axis=2).astype(BF16)
    c4_re = jnp.transpose(ssm_c_re[0].reshape(nblk, GPB, H, P), (0, 1, 3, 2))
    c4_im = jnp.transpose(ssm_c_im[0].reshape(nblk, GPB, H, P), (0, 1, 3, 2))
    cmat = jnp.concatenate([_blockdiag(c4_re, eye), -_blockdiag(c4_im, eye)], axis=1).astype(BF16)

    def e1(t, b):
        xhat, _ = _ln_stats(t[0])
        return [xhat * (1.0 + b[0]) + b[1]], []
    (h1,) = _rowwise("ln_mod1", e1, S, ts, [(x2d, D, 0)], [sc1, sh1], [(D, BF16)], [])

    (proj,) = mm_nn("proj", h1, wg_in, F32, 1)
    z = s5_fwd(proj, bmat, cmat, ssm_d, kconst)
    (vt,) = mm_nn("glu", z, wg_vg, BF16, 4)
    pooled = pool_fwd(proj, W, W, gw)

    def pool_epi(acc, ex, outs):
        a = acc[...]
        outs[0][...] = a
        outs[1][...] = (a * ex[0][...]).astype(BF16)
    tmp = _tile(S, 1024)
    yp, ypool = _mm(
        "pool_mix", "nn", pooled, wp_full.astype(BF16), (S // tmp, nwin, 1),
        pl.BlockSpec((tmp, gw), lambda i, j, k: (i, j)), pl.BlockSpec((1, gw, gw), lambda i, j, k: (j, 0, 0)),
        [(_sds((S, W), F32), pl.BlockSpec((tmp, gw), lambda i, j, k: (i, j))),
         (_sds((S, W), BF16), pl.BlockSpec((tmp, gw), lambda i, j, k: (i, j)))],
        (tmp, gw), 1, gw, None, pool_epi,
        [(pool_scale, pl.BlockSpec((1, gw), lambda i, j, k: (0, j)))])
    (y_b,) = mm_nn("pool_out", ypool, wg_po, BF16, 4)

    cb = D // NDEV
    ga_cb, gb_cb = (2 * W) // cb, (2 * W + D) // cb
    tsm = _tile(S, 512)

    def merge_call(name, fn, ins, n_out):
        def body(*refs):
            vals = [r[...].astype(F32) for r in refs[:len(ins)]]
            for r, v in zip(refs[len(ins):], fn(*vals)):
                r[...] = v.astype(r.dtype)
        return pl.pallas_call(
            body, name=name, grid=(S // tsm, NDEV),
            in_specs=[pl.BlockSpec((tsm, cb), f) for (_, f) in ins],
            out_specs=[pl.BlockSpec((tsm, cb), f) for (_, f) in n_out],
            out_shape=[_sds(s, BF16) for (s, _) in n_out],
            compiler_params=_params(("parallel", "parallel")),
        )(*[a for (a, _) in ins])

    merge_ins = [(proj, lambda i, j: (i, ga_cb + j)), (proj, lambda i, j: (i, gb_cb + j)),
                 (vt, lambda i, j: (i, 2 * j)), (vt, lambda i, j: (i, 2 * j + 1)),
                 (y_b, lambda i, j: (i, j))]

    def merge_f(ga, gb, vv, tt, yb):
        return [_sigmoid(ga) * (vv * _sigmoid(tt)) + _sigmoid(gb) * yb]
    (merged,) = merge_call("merge", merge_f, merge_ins, [((S, D), lambda i, j: (i, j))])

    (mix,) = mm_nn("mix_out", merged, wout_full, F32, 1)

    def e3(t, b):
        xv, mx = t
        g1v, l1g, l1b, sc2v, sh2v = b
        r1 = ALPHA * xv + g1v * mx
        xh1, _ = _ln_stats(r1)
        x1 = xh1 * l1g + l1b
        xh, _ = _ln_stats(x1)
        return [r1, xh * (1.0 + sc2v) + sh2v], []
    r1, h2 = _rowwise("post_mix", e3, S, ts, [(x2d, D, 0), (mix, D, 0)],
                      [g1, ln1_g, ln1_b, sc2, sh2], [(D, F32), (D, BF16)], [])

    def relu_epi(acc, ex, outs):
        outs[0][...] = jnp.maximum(acc[...], 0.0).astype(BF16)
    (rl,) = mm_nn("ff1", h2, wg_ff1, BF16, 1, epi=relu_epi)

    def square(a):
        return a * a
    (y2,) = mm_nn("ff2", rl, wff2_full, F32, 1, pro=square)

    def e4(t, b):
        r1v, y2v, tg = t
        g2v, l1g, l1b, l2g, l2b = b
        xh1, _ = _ln_stats(r1v)
        x1 = xh1 * l1g + l1b
        r2 = ALPHA * x1 + g2v * y2v
        xh2, rs2 = _ln_stats(r2)
        err = xh2 * l2g + l2b - tg
        dx2 = err * (1.0 / D)
        dr2 = _ln_bwd(dx2 * l2g, xh2, rs2)
        lsum = jnp.sum(_colsum(err * err), axis=1, keepdims=True) * (0.5 / D)
        return ([ALPHA * dr2, g2v * dr2],
                [jnp.broadcast_to(lsum, (1, LANES)), _colsum(dx2 * xh2), _colsum(dx2), _colsum(dr2 * y2v)])
    dx1a, dy2, loss_acc, g_ln2g, g_ln2b, d_g2 = _rowwise(
        "head", e4, S, ts, [(r1, D, 0), (y2, D, 0), (tgt, D, 0)], [g2, ln1_g, ln1_b, ln2_g, ln2_b],
        [(D, F32), (D, BF16)], [LANES, D, D, D])

    tn_ff = _tile(4 * D, 1024)

    def dff_epi(acc, ex, outs):
        outs[0][...] = (acc[...] * (2.0 * ex[0][...].astype(F32))).astype(BF16)
    tmf = _tile(S, 1024)
    (da1,) = mm_nt("d_ff2", dy2, wff2_full, BF16, 1, tn=tn_ff, epi=dff_epi,
                   extras=[(rl, pl.BlockSpec((tmf, tn_ff), lambda i, j, k: (i, j)))])
    gw_ff2 = mm_tn("gw_ff2", rl, dy2, BF16, NDEV, 0, pro=square)
    (dh2,) = mm_nt("d_ff1", da1, wg_ff1, F32, 2)
    gw_ff1 = mm_tn("gw_ff1", h2, da1, BF16, NDEV, 1)

    def e5(t, b):
        dh2v, r1v, dx1av, mx = t
        sc2v, l1g, l1b, g1v = b
        xh1, rs1 = _ln_stats(r1v)
        x1 = xh1 * l1g + l1b
        xh, rs = _ln_stats(x1)
        dx1 = dx1av + _ln_bwd(dh2v * (1.0 + sc2v), xh, rs)
        dr1 = _ln_bwd(dx1 * l1g, xh1, rs1)
        return ([ALPHA * dr1, g1v * dr1],
                [_colsum(dh2v * xh), _colsum(dh2v), _colsum(dx1 * xh1), _colsum(dx1), _colsum(dr1 * mx)])
    dxa, dmix, d_sc2, d_sh2, g_ln1g, g_ln1b, d_g1 = _rowwise(
        "post_mix_bwd", e5, S, ts, [(dh2, D, 0), (r1, D, 0), (dx1a, D, 0), (mix, D, 0)],
        [sc2, ln1_g, ln1_b, g1], [(D, F32), (D, BF16)], [D, D, D, D, D])

    (dmerged,) = mm_nt("d_mix_out", dmix, wout_full, BF16, 1)
    gw_out = mm_tn("gw_out", merged, dmix, BF16, NDEV, 0)

    def merge_b(ga, gb, vv, tt, yb, dm):
        sa, sb, st = _sigmoid(ga), _sigmoid(gb), _sigmoid(tt)
        dya = dm * sa
        return [dm * (vv * st) * sa * (1.0 - sa), dm * yb * sb * (1.0 - sb),
                dya * st, dya * vv * st * (1.0 - st), dm * sb]
    dga, dgb_, dvt_v, dvt_t, dy_b = None, None, None, None, None
    outs_b = merge_call(
        "merge_bwd", merge_b, merge_ins + [(dmerged, lambda i, j: (i, j))],
        [((S, D), lambda i, j: (i, j)), ((S, D), lambda i, j: (i, j)),
         ((S, D), lambda i, j: (i, j)), ((S, D), lambda i, j: (i, j)), ((S, D), lambda i, j: (i, j))])
    dga, dgb_, dv_, dt_, dy_b = outs_b
    dvt = jnp.stack([dv_.reshape(S, NDEV, cb), dt_.reshape(S, NDEV, cb)], axis=2).reshape(S, 2 * D)

    (dypool,) = mm_nt("d_pool_out", dy_b, wg_po, F32, NDEV)
    gw_po = mm_tn("gw_pool_out", ypool, dy_b, BF16, NDEV, 4)

    def e7(t, b):
        return [t[0] * b[0]], [_colsum(t[0] * t[1])]
    dyp, g_pscale = _rowwise("pool_scale_bwd", e7, S, ts, [(dypool, W, 0), (yp, W, 0)],
                             [pool_scale], [(W, BF16)], [W])
    (dpooled,) = _mm(
        "d_pool_mix", "nt", dyp, wp_full.astype(BF16), (S // tmp, nwin, 1),
        pl.BlockSpec((tmp, gw), lambda i, j, k: (i, j)), pl.BlockSpec((1, gw, gw), lambda i, j, k: (j, 0, 0)),
        [(_sds((S, W), F32), pl.BlockSpec((tmp, gw), lambda i, j, k: (i, j)))], (tmp, gw), 1, gw)
    tkp = _tile(S, 1024)
    gw_pool = _mm(
        "gw_pool", "tn", pooled, dyp, (nwin, 1, S // tkp),
        pl.BlockSpec((tkp, gw), lambda i, j, k: (k, i)), pl.BlockSpec((tkp, gw), lambda i, j, k: (k, i)),
        [(_sds((nwin, gw, gw), BF16), pl.BlockSpec((1, gw, gw), lambda i, j, k: (i, 0, 0)))],
        (gw, gw), 1, gw, stacked_out=True)[0]
    du_pool = pool_bwd(dpooled, gw)

    (dz,) = mm_nt("d_glu", dvt, wg_vg, BF16, NDEV)
    gw_vg = mm_tn("gw_glu", z, dvt, BF16, 2 * NDEV, 4)
    du_ssm, g_bmat, g_cmat, g_d, g_a = s5_bwd(proj, dz, bmat, cmat, ssm_d, kconst)

    dproj = jnp.concatenate([du_ssm, du_pool, dga, dgb_], axis=1)
    (dh1,) = mm_nt("d_proj", dproj, wg_in, F32, 2)
    gw_in = mm_tn("gw_in", h1, dproj, BF16, NDEV, 1)

    def e10(t, b):
        dh1v, xv, dxav = t
        xh, rs = _ln_stats(xv)
        return ([dxav + _ln_bwd(dh1v * (1.0 + b[0]), xh, rs)],
                [_colsum(dh1v * xh), _colsum(dh1v)])
    grad_x, d_sc1, d_sh1 = _rowwise("ln_mod1_bwd", e10, S, ts, [(dh1, D, 0), (x2d, D, 0), (dxa, D, 0)],
                                    [sc1], [(D, F32)], [D, D])

    gb4 = _diag_blocks(g_bmat[:, :, :GPB * P], GPB, H, P), _diag_blocks(g_bmat[:, :, GPB * P:], GPB, H, P)
    dbb = jnp.stack([jnp.transpose(t, (2, 0, 1, 3)).reshape(H, G * P) for t in gb4])
    g_bt_re, g_bt_im, g_f = s5_bbar_bwd(f2r, bt_re, bt_im, dbb)
    g_b_re = jnp.transpose(g_bt_re.reshape(H, G, P), (1, 2, 0))
    g_b_im = jnp.transpose(g_bt_im.reshape(H, G, P), (1, 2, 0))
    gc_top = _diag_blocks(g_cmat[:, :GPB * P, :], GPB, P, H)
    gc_bot = _diag_blocks(g_cmat[:, GPB * P:, :], GPB, P, H)
    g_c_re = jnp.transpose(gc_top, (0, 1, 3, 2)).reshape(G, H, P)
    g_c_im = -jnp.transpose(gc_bot, (0, 1, 3, 2)).reshape(G, H, P)
    d_ab = jnp.transpose(g_a.reshape(nblk, 2, GPB, P), (1, 0, 2, 3)).reshape(2, G, P)
    g_lr, g_li, g_ldt = s5_disc_bwd(lam_re[0], lam_im[0], log_dt[0].reshape(G, 1), d_ab,
                                    g_f.reshape(2, G, P))

    dmod = jnp.concatenate([d_sh1, d_sc1, d_g1, d_sh2, d_sc2, d_g2], axis=1)
    small_names = [b_ada, lam_re, lam_im, log_dt, ssm_b_re, ssm_b_im, ssm_c_re, ssm_c_im, ssm_d,
                   pool_scale, ln1_g, ln1_b, ln2_g, ln2_b]
    small_m = [m_b_ada, m_lam_re, m_lam_im, m_log_dt, m_ssm_b_re, m_ssm_b_im, m_ssm_c_re, m_ssm_c_im,
               m_ssm_d, m_pool_scale, m_ln1_g, m_ln1_b, m_ln2_g, m_ln2_b]
    small_v = [v_b_ada, v_lam_re, v_lam_im, v_log_dt, v_ssm_b_re, v_ssm_b_im, v_ssm_c_re, v_ssm_c_im,
               v_ssm_d, v_pool_scale, v_ln1_g, v_ln1_b, v_ln2_g, v_ln2_b]
    small_g = [dmod, g_lr, g_li, g_ldt, g_b_re, g_b_im, g_c_re, g_c_im, g_d, g_pscale,
               g_ln1g, g_ln1b, g_ln2g, g_ln2b]
    (parts_all,) = all_gather("gather_small", [_small_pack(small_g)])
    sg, sd, sm, sv = adamw_small(parts_all, _small_pack(small_names), _small_pack(small_m),
                                 _small_pack(small_v))
    shapes = [t.shape for t in small_names]
    sg, sd, sm, sv = (_small_unpack(t, shapes) for t in (sg, sd, sm, sv))

    nmod = 6 * D
    dmod_all = parts_all[:, :nmod // LANES, :].reshape(NDEV, nmod)
    c_all_t = jnp.transpose(c_all.reshape(NDEV, D))
    ada_out = adamw_ada(c_all_t, dmod_all, w_ada[0], m_w_ada[0], v_w_ada[0], my_dev)

    gw_pool_st = jnp.transpose(gw_pool.reshape(nwin, NDEV, gw // NDEV, gw), (1, 0, 2, 3))
    gw_vg_st = gw_vg.reshape(NDEV, 2, W, D // NDEV)
    grads = [gw_in, gw_vg_st, gw_pool_st, gw_po, gw_out, gw_ff1, gw_ff2]
    got = pair_exchange("pair_exchange", grads)
    parts = [pair_sum("pair_sum_%d" % i, g, t, my_c) for i, (g, t) in enumerate(zip(grads, got))]
    got3 = chip_exchange("chip_exchange", parts)

    glu_w = jnp.stack([w_glu_val[0], w_glu_gate[0]])
    glu_m = jnp.stack([m_w_glu_val[0], m_w_glu_gate[0]])
    glu_v = jnp.stack([v_w_glu_val[0], v_w_glu_gate[0]])
    wmv = [(w_in[0], m_w_in[0], v_w_in[0]), (glu_w, glu_m, glu_v), (w_pool[0], m_w_pool[0], v_w_pool[0]),
           (w_pool_out[0], m_w_pool_out[0], v_w_pool_out[0]), (w_out[0], m_w_out[0], v_w_out[0]),
           (w_ff1[0], m_w_ff1[0], v_w_ff1[0]), (w_ff2[0], m_w_ff2[0], v_w_ff2[0])]
    upd = [adamw_sharded("adamw_%d" % i, p, t, w, m, v, my_chip)
           for i, (p, t, (w, m, v)) in enumerate(zip(parts, got3, wmv))]
    u_in, u_glu, u_pool, u_po, u_out, u_ff1, u_ff2 = upd

    loss = lax.psum(loss_acc[0, 0], ("x", "y", "c"))

    def pick(k):
        return [ada_out[k][None], sg_sd[k][0], u_in[k][None]] + [t for t in sg_sd[k][1:9]] + \
               [u_glu[k][0][None], u_glu[k][1][None], u_pool[k][None], sg_sd[k][9], u_po[k][None],
                u_out[k][None], sg_sd[k][10], sg_sd[k][11], u_ff1[k][None], u_ff2[k][None],
                sg_sd[k][12], sg_sd[k][13]]

    sg_sd = [sg, sd, sm, sv]
    return (loss, grad_x[None], *pick(0), *pick(1), *pick(2), *pick(3))
```

```python
import functools
import math

import jax
import jax.numpy as jnp
from jax import lax
from jax.experimental import pallas as pl
from jax.experimental.pallas import tpu as pltpu
from jax.experimental.pallas import tpu_sc as plsc

F32 = jnp.float32
BF16 = jnp.bfloat16
MESH = pl.DeviceIdType.MESH
NDEV = 8
NCHIP = 4

SSM_GROUP = 16
SSM_STATE = 64
GROUPS_PER_BLOCK = 8
POOL_WINDOWS = (2, 4, 8, 16)
LN_EPS = 1e-5
ALPHA = 2.0 ** 0.25
ADAM_LR, ADAM_B1, ADAM_B2, ADAM_EPS, ADAM_WD, ADAM_STEP = 0.001, 0.9, 0.999, 1e-08, 0.01, 10
SUBLANES = 8
LANES = 128
VMEM_LIMIT = 56 * 1024 * 1024


def _params(sem=None, vmem=VMEM_LIMIT):
    return pltpu.CompilerParams(dimension_semantics=sem, vmem_limit_bytes=vmem)


def _tile(n, pref):
    if n <= pref:
        return n
    t = 1 << (pref.bit_length() - 1)
    while n % t:
        t //= 2
    return t


def _cast_epi(acc, ex, outs):
    outs[0][...] = acc[...].astype(outs[0].dtype)


def _mm(name, kind, a, b, grid, a_spec, b_spec, outs, acc_shape, nsub=1, c=None,
        pro=None, epi=None, extras=(), stacked_out=False):
    nk = grid[2]
    n_ex, n_out = len(extras), len(outs)
    epi_fn = epi

    def body(*refs):
        a_ref, b_ref = refs[0], refs[1]
        ex = refs[2:2 + n_ex]
        out_refs = refs[2 + n_ex:2 + n_ex + n_out]
        acc = refs[-1]
        k = pl.program_id(2)

        @pl.when(k == 0)
        def _():
            acc[...] = jnp.zeros_like(acc)

        av = a_ref[...]
        if pro is not None:
            av = pro(av)
        if kind == "nn":
            for s in range(nsub):
                acc[:, s * c:(s + 1) * c] += jnp.dot(av, b_ref[s], preferred_element_type=F32)
        elif kind == "nt":
            t = acc[...]
            for s in range(nsub):
                t = t + lax.dot_general(av[:, s * c:(s + 1) * c], b_ref[s],
                                        (((1,), (1,)), ((), ())), preferred_element_type=F32)
            acc[...] = t
        else:
            acc[...] += lax.dot_general(av, b_ref[...], (((0,), (0,)), ((), ())),
                                        preferred_element_type=F32)

        @pl.when(k == nk - 1)
        def _():
            if epi_fn is not None:
                epi_fn(acc, ex, out_refs)
            elif stacked_out:
                for s in range(nsub):
                    out_refs[0][s] = acc[:, s * c:(s + 1) * c].astype(out_refs[0].dtype)
            else:
                _cast_epi(acc, ex, out_refs)

    res = pl.pallas_call(
        body, name=name, grid=grid,
        in_specs=[a_spec, b_spec] + [e[1] for e in extras],
        out_specs=[o[1] for o in outs],
        out_shape=[o[0] for o in outs],
        scratch_shapes=[pltpu.VMEM(acc_shape, F32)],
        compiler_params=_params(("parallel", "parallel", "arbitrary")),
    )(a, b, *[e[0] for e in extras])
    return res


def _sds(shape, dtype):
    return jax.ShapeDtypeStruct(shape, dtype)


def mm_nn(name, a, b3, out_dtype, nsub, tm=1024, tk=2048, tn=None, pro=None, epi=None,
          extras=(), extra_outs=(), a_col0=0):
    M = a.shape[0]
    nb, K, cdim = b3.shape
    tm, tk = _tile(M, tm), _tile(K, tk)
    if nb == 1:
        tn = _tile(cdim, tn or 1024)
        nsub, c, nj = 1, tn, cdim // tn
        b_spec = pl.BlockSpec((1, tk, tn), lambda i, j, k: (0, k, j))
        N = cdim
    else:
        c, nj, tn = cdim, nb // nsub, nsub * cdim
        b_spec = pl.BlockSpec((nsub, tk, cdim), lambda i, j, k: (j, k, 0))
        N = nb * cdim
    kb0 = a_col0 // tk
    a_spec = pl.BlockSpec((tm, tk), lambda i, j, k: (i, kb0 + k))
    grid = (M // tm, nj, K // tk)
    o_spec = pl.BlockSpec((tm, tn), lambda i, j, k: (i, j))
    outs = [(_sds((M, N), out_dtype), o_spec)] + [(_sds((M, N), d), o_spec) for d in extra_outs]
    return _mm(name, "nn", a, b3, grid, a_spec, b_spec, outs, (tm, tn), nsub, c, pro, epi, extras)


def mm_nt(name, a, b3, out_dtype, nsub, tm=1024, tn=1024, epi=None, extras=(), extra_outs=()):
    M = a.shape[0]
    nb, N, cdim = b3.shape
    tm, tn = _tile(M, tm), _tile(N, tn)
    if nb == 1:
        tk = _tile(cdim, 2048)
        nsub, c, nk = 1, tk, cdim // tk
        b_spec = pl.BlockSpec((1, tn, tk), lambda i, j, k: (0, j, k))
    else:
        c, nk, tk = cdim, nb // nsub, nsub * cdim
        b_spec = pl.BlockSpec((nsub, tn, cdim), lambda i, j, k: (k, j, 0))
    a_spec = pl.BlockSpec((tm, tk), lambda i, j, k: (i, k))
    grid = (M // tm, N // tn, nk)
    o_spec = pl.BlockSpec((tm, tn), lambda i, j, k: (i, j))
    outs = [(_sds((M, N), out_dtype), o_spec)] + [(_sds((M, N), d), o_spec) for d in extra_outs]
    return _mm(name, "nt", a, b3, grid, a_spec, b_spec, outs, (tm, tn), nsub, c, None, epi, extras)


def mm_tn(name, a, b, out_dtype, nb, nsub, tma=1024, tk=1024, pro=None, a_col0=0, a_cols=None):
    S = a.shape[0]
    Ka = a_cols or a.shape[1]
    N = b.shape[1]
    tk = _tile(S, tk)
    if nsub == 0:
        rows = Ka // nb
        tma = rows if rows <= tma else _tile(rows, tma)
        per = rows // tma
        tn = _tile(N, 1024)
        grid = (Ka // tma, N // tn, S // tk)
        o_spec = pl.BlockSpec((1, tma, tn), lambda i, j, k: (i // per, i % per, j))
        out = _sds((nb, rows, N), out_dtype)
        nsub_k, c = 1, tn
        b_spec = pl.BlockSpec((tk, tn), lambda i, j, k: (k, j))
    else:
        c = N // nb
        tma = _tile(Ka, tma)
        grid = (Ka // tma, nb // nsub, S // tk)
        o_spec = pl.BlockSpec((nsub, tma, c), lambda i, j, k: (j, i, 0))
        out = _sds((nb, Ka, c), out_dtype)
        nsub_k = nsub
        tn = nsub * c
        b_spec = pl.BlockSpec((tk, tn), lambda i, j, k: (k, j))
    ab0 = a_col0 // tma
    a_spec = pl.BlockSpec((tk, tma), lambda i, j, k: (k, ab0 + i))
    return _mm(name, "tn", a, b, grid, a_spec, b_spec, [(out, o_spec)], (tma, tn), nsub_k, c,
               pro, None, (), stacked_out=True)[0]


def _rowwise(name, fn, S, ts, tiled, bcast, tiled_out, acc_out):
    nt, nb, no, na = len(tiled), len(bcast), len(tiled_out), len(acc_out)

    def body(*refs):
        tin = [r[...] for r in refs[:nt]]
        bin_ = [r[...] for r in refs[nt:nt + nb]]
        o_refs = refs[nt + nb:nt + nb + no]
        a_refs = refs[nt + nb + no:]
        touts, aouts = fn(tin, bin_)
        for r, v in zip(o_refs, touts):
            r[...] = v.astype(r.dtype)
        i = pl.program_id(0)

        @pl.when(i == 0)
        def _():
            for r, v in zip(a_refs, aouts):
                r[...] = v

        @pl.when(i > 0)
        def _():
            for r, v in zip(a_refs, aouts):
                r[...] += v

    in_specs = [pl.BlockSpec((ts, w), functools.partial(lambda i, cb: (i, cb), cb=cb))
                for (_, w, cb) in tiled]
    in_specs += [pl.BlockSpec(b.shape, lambda i: (0, 0)) for b in bcast]
    out_specs = [pl.BlockSpec((ts, w), lambda i: (i, 0)) for (w, _) in tiled_out]
    out_specs += [pl.BlockSpec((1, w), lambda i: (0, 0)) for w in acc_out]
    out_shape = [_sds((S, w), d) for (w, d) in tiled_out] + [_sds((1, w), F32) for w in acc_out]
    return pl.pallas_call(
        body, name=name, grid=(S // ts,), in_specs=in_specs, out_specs=out_specs,
        out_shape=out_shape, compiler_params=_params(("arbitrary",)),
    )(*[t[0] for t in tiled], *bcast)


def _ln_stats(v):
    mu = jnp.mean(v, axis=-1, keepdims=True)
    vc = v - mu
    var = jnp.mean(vc * vc, axis=-1, keepdims=True)
    rstd = lax.rsqrt(var + LN_EPS)
    return vc * rstd, rstd


def _ln_bwd(dxhat, xhat, rstd):
    return rstd * (dxhat - jnp.mean(dxhat, axis=-1, keepdims=True)
                   - xhat * jnp.mean(dxhat * xhat, axis=-1, keepdims=True))


def _colsum(v):
    return jnp.sum(v, axis=0, keepdims=True)


def _sigmoid(v):
    return 1.0 / (1.0 + jnp.exp(-v))


_GELU_C = math.sqrt(2.0 / math.pi)


def _gelu(v):
    return 0.5 * v * (1.0 + jnp.tanh(_GELU_C * (v + 0.044715 * v * v * v)))


def _gelu_grad(v):
    t = jnp.tanh(_GELU_C * (v + 0.044715 * v * v * v))
    return 0.5 * (1.0 + t) + 0.5 * v * (1.0 - t * t) * _GELU_C * (1.0 + 3 * 0.044715 * v * v)


def _disc(lr, li, ldt):
    dt = jnp.exp(ldt)
    mag = jnp.exp(lr * dt)
    ang = li * dt
    ab_re = mag * jnp.cos(ang)
    ab_im = mag * jnp.sin(ang)
    num_re = ab_re - 1.0
    num_im = ab_im
    den = lr * lr + li * li
    f_re = (num_re * lr + num_im * li) / den
    f_im = (num_im * lr - num_re * li) / den
    return ab_re, ab_im, f_re, f_im


def _cmul(ar, ai, br, bi):
    return ar * br - ai * bi, ar * bi + ai * br


def s5_disc(lam_re, lam_im, log_dt):
    G, P = lam_re.shape

    def body(lr_ref, li_ref, ldt_ref, f_ref, k_ref):
        ab_re, ab_im, f_re, f_im = _disc(lr_ref[...], li_ref[...], ldt_ref[...])
        f_ref[0] = f_re
        f_ref[1] = f_im
        pr, pi = [ab_re], [ab_im]
        for _ in range(SUBLANES - 1):
            nr, ni = _cmul(pr[-1], pi[-1], ab_re, ab_im)
            pr.append(nr)
            pi.append(ni)
        zero = jnp.zeros_like(ab_re)
        for n, sh in enumerate((1, 2, 4)):
            for r in range(SUBLANES):
                k_ref[2 * n, r] = pr[sh - 1] if r >= sh else zero
                k_ref[2 * n + 1, r] = pi[sh - 1] if r >= sh else zero
                k_ref[8 + 2 * n, r] = pr[sh - 1] if r + sh < SUBLANES else zero
                k_ref[8 + 2 * n + 1, r] = -pi[sh - 1] if r + sh < SUBLANES else zero
        for r in range(SUBLANES):
            k_ref[6, r] = pr[r]
            k_ref[7, r] = pi[r]
            k_ref[14, r] = pr[SUBLANES - 1 - r]
            k_ref[15, r] = -pi[SUBLANES - 1 - r]

    vm = pl.BlockSpec(memory_space=pltpu.VMEM)
    return pl.pallas_call(
        body, name="s5_disc", in_specs=[vm, vm, vm], out_specs=[vm, vm],
        out_shape=[_sds((2, G, P), F32), _sds((16, SUBLANES, G, P), F32)],
    )(lam_re, lam_im, log_dt)


def s5_disc_bwd(lam_re, lam_im, log_dt, d_ab, d_f):
    G, P = lam_re.shape

    def body(lr_ref, li_ref, ldt_ref, dab_ref, df_ref, glr_ref, gli_ref, gdt_ref):
        _, vjp = jax.vjp(_disc, lr_ref[...], li_ref[...], ldt_ref[...])
        glr, gli, gdt = vjp((dab_ref[0], dab_ref[1], df_ref[0], df_ref[1]))
        glr_ref[...] = glr
        gli_ref[...] = gli
        gdt_ref[...] = gdt

    vm = pl.BlockSpec(memory_space=pltpu.VMEM)
    return pl.pallas_call(
        body, name="s5_disc_bwd", in_specs=[vm] * 5, out_specs=[vm] * 3,
        out_shape=[_sds((G, P), F32), _sds((G, P), F32), _sds((G, 1), F32)],
    )(lam_re, lam_im, log_dt, d_ab, d_f)


def s5_bbar(f2, bt_re, bt_im):
    def body(f_ref, br_ref, bi_ref, o_ref):
        fr, fi = f_ref[0], f_ref[1]
        br, bi = br_ref[...], bi_ref[...]
        o_ref[0] = fr * br - fi * bi
        o_ref[1] = fr * bi + fi * br

    vm = pl.BlockSpec(memory_space=pltpu.VMEM)
    return pl.pallas_call(body, name="s5_bbar", in_specs=[vm] * 3, out_specs=vm,
                          out_shape=_sds((2,) + bt_re.shape, F32))(f2, bt_re, bt_im)


def s5_bbar_bwd(f2, bt_re, bt_im, dbb):
    def body(f_ref, br_ref, bi_ref, d_ref, gbr_ref, gbi_ref, gf_ref):
        fr, fi = f_ref[0], f_ref[1]
        br, bi = br_ref[...], bi_ref[...]
        dr, di = d_ref[0], d_ref[1]
        gbr_ref[...] = fr * dr + fi * di
        gbi_ref[...] = fr * di - fi * dr
        gf_ref[0] = _colsum(dr * br + di * bi)
        gf_ref[1] = _colsum(di * br - dr * bi)

    vm = pl.BlockSpec(memory_space=pltpu.VMEM)
    return pl.pallas_call(
        body, name="s5_bbar_bwd", in_specs=[vm] * 4, out_specs=[vm] * 3,
        out_shape=[_sds(bt_re.shape, F32), _sds(bt_re.shape, F32), _sds(f2.shape, F32)],
    )(f2, bt_re, bt_im, dbb)


def _scan_fwd(xs, k_ref, nst):
    ntile = xs.shape[0] // SUBLANES

    def step(t, carry):
        cr, ci = carry
        r0 = pl.multiple_of(t * SUBLANES, SUBLANES)
        xr = xs[pl.ds(r0, SUBLANES), 0:nst]
        xi = xs[pl.ds(r0, SUBLANES), nst:2 * nst]
        for n, sh in enumerate((1, 2, 4)):
            sr = pltpu.roll(xr, sh, 0)
            si = pltpu.roll(xi, sh, 0)
            mr, mi = k_ref[2 * n], k_ref[2 * n + 1]
            xr, xi = xr + mr * sr - mi * si, xi + mr * si + mi * sr
        pr, pi = k_ref[6], k_ref[7]
        xr, xi = xr + pr * cr - pi * ci, xi + pr * ci + pi * cr
        xs[pl.ds(r0, SUBLANES), 0:nst] = xr
        xs[pl.ds(r0, SUBLANES), nst:2 * nst] = xi
        return (jnp.broadcast_to(xr[SUBLANES - 1:SUBLANES, :], xr.shape),
                jnp.broadcast_to(xi[SUBLANES - 1:SUBLANES, :], xi.shape))

    zero = jnp.zeros((SUBLANES, nst), F32)
    lax.fori_loop(0, ntile, step, (zero, zero))


def _scan_bwd(g, xs, k_ref, nst):
    ntile = g.shape[0] // SUBLANES
    row = lax.broadcasted_iota(jnp.int32, (SUBLANES, nst), 0)

    def step(tt, carry):
        cr, ci, ar, ai = carry
        t = ntile - 1 - tt
        r0 = pl.multiple_of(t * SUBLANES, SUBLANES)
        gr = g[pl.ds(r0, SUBLANES), 0:nst]
        gi = g[pl.ds(r0, SUBLANES), nst:2 * nst]
        for n, sh in enumerate((1, 2, 4)):
            sr = pltpu.roll(gr, SUBLANES - sh, 0)
            si = pltpu.roll(gi, SUBLANES - sh, 0)
            mr, mi = k_ref[8 + 2 * n], k_ref[8 + 2 * n + 1]
            gr, gi = gr + mr * sr - mi * si, gi + mr * si + mi * sr
        qr, qi = k_ref[14], k_ref[15]
        gr, gi = gr + qr * cr - qi * ci, gi + qr * ci + qi * cr
        g[pl.ds(r0, SUBLANES), 0:nst] = gr
        g[pl.ds(r0, SUBLANES), nst:2 * nst] = gi
        p0 = pl.multiple_of(jnp.maximum(t - 1, 0) * SUBLANES, SUBLANES)
        live = (t > 0).astype(F32)
        xr = xs[pl.ds(r0, SUBLANES), 0:nst]
        xi = xs[pl.ds(r0, SUBLANES), nst:2 * nst]
        pr = xs[pl.ds(p0, SUBLANES), 0:nst][SUBLANES - 1:SUBLANES, :] * live
        pi = xs[pl.ds(p0, SUBLANES), nst:2 * nst][SUBLANES - 1:SUBLANES, :] * live
        xmr = jnp.where(row == 0, jnp.broadcast_to(pr, xr.shape), pltpu.roll(xr, 1, 0))
        xmi = jnp.where(row == 0, jnp.broadcast_to(pi, xi.shape), pltpu.roll(xi, 1, 0))
        ar = ar + gr * xmr + gi * xmi
        ai = ai + gi * xmr - gr * xmi
        return (jnp.broadcast_to(gr[0:1, :], gr.shape), jnp.broadcast_to(gi[0:1, :], gi.shape),
                ar, ai)

    zero = jnp.zeros((SUBLANES, nst), F32)
    _, _, ar, ai = lax.fori_loop(0, ntile, step, (zero, zero, zero, zero))
    return _colsum(ar), _colsum(ai)


def s5_fwd(proj, bmat, cmat, dskip, kconst):
    S = proj.shape[0]
    nb, cw, nst2 = bmat.shape
    nst = nst2 // 2

    def body(u_ref, b_ref, c_ref, d_ref, k_ref, z_ref, xs):
        u = u_ref[...]
        xs[...] = jnp.dot(u.astype(BF16), b_ref[0], preferred_element_type=F32)
        _scan_fwd(xs, k_ref, nst)
        y = jnp.dot(xs[...].astype(BF16), c_ref[0], preferred_element_type=F32) + d_ref[...] * u
        z_ref[...] = _gelu(y).astype(BF16)

    return pl.pallas_call(
        body, name="s5_fwd", grid=(nb,),
        in_specs=[pl.BlockSpec((S, cw), lambda b: (0, b)),
                  pl.BlockSpec((1, cw, nst2), lambda b: (b, 0, 0)),
                  pl.BlockSpec((1, nst2, cw), lambda b: (b, 0, 0)),
                  pl.BlockSpec((1, cw), lambda b: (0, b)),
                  pl.BlockSpec((16, SUBLANES, nst), lambda b: (0, 0, b))],
        out_specs=pl.BlockSpec((S, cw), lambda b: (0, b)),
        out_shape=_sds((S, nb * cw), BF16),
        scratch_shapes=[pltpu.VMEM((S, nst2), F32)],
        compiler_params=_params(("arbitrary",)),
    )(proj, bmat, cmat, dskip, kconst)


def s5_bwd(proj, dz, bmat, cmat, dskip, kconst):
    S = proj.shape[0]
    nb, cw, nst2 = bmat.shape
    nst = nst2 // 2

    def body(u_ref, dz_ref, b_ref, c_ref, d_ref, k_ref, du_ref, gb_ref, gc_ref, gd_ref, ga_ref,
             xs, g):
        u = u_ref[...]
        ub = u.astype(BF16)
        bm, cm, d = b_ref[0], c_ref[0], d_ref[...]
        xs[...] = jnp.dot(ub, bm, preferred_element_type=F32)
        _scan_fwd(xs, k_ref, nst)
        xsb = xs[...].astype(BF16)
        y = jnp.dot(xsb, cm, preferred_element_type=F32) + d * u
        dy = dz_ref[...].astype(F32) * _gelu_grad(y)
        gd_ref[...] = _colsum(dy * u)
        dyb = dy.astype(BF16)
        gc_ref[0] = lax.dot_general(xsb, dyb, (((0,), (0,)), ((), ())), preferred_element_type=F32)
        g[...] = lax.dot_general(dyb, cm, (((1,), (1,)), ((), ())), preferred_element_type=F32)
        ar, ai = _scan_bwd(g, xs, k_ref, nst)
        ga_ref[0, 0:1, :] = ar
        ga_ref[0, 1:2, :] = ai
        gb = g[...].astype(BF16)
        du = lax.dot_general(gb, bm, (((1,), (1,)), ((), ())), preferred_element_type=F32) + d * dy
        du_ref[...] = du.astype(BF16)
        gb_ref[0] = lax.dot_general(ub, gb, (((0,), (0,)), ((), ())), preferred_element_type=F32)

    return pl.pallas_call(
        body, name="s5_bwd", grid=(nb,),
        in_specs=[pl.BlockSpec((S, cw), lambda b: (0, b)),
                  pl.BlockSpec((S, cw), lambda b: (0, b)),
                  pl.BlockSpec((1, cw, nst2), lambda b: (b, 0, 0)),
                  pl.BlockSpec((1, nst2, cw), lambda b: (b, 0, 0)),
                  pl.BlockSpec((1, cw), lambda b: (0, b)),
                  pl.BlockSpec((16, SUBLANES, nst), lambda b: (0, 0, b))],
        out_specs=[pl.BlockSpec((S, cw), lambda b: (0, b)),
                   pl.BlockSpec((1, cw, nst2), lambda b: (b, 0, 0)),
                   pl.BlockSpec((1, nst2, cw), lambda b: (b, 0, 0)),
                   pl.BlockSpec((1, cw), lambda b: (0, b)),
                   pl.BlockSpec((1, 2, nst), lambda b: (b, 0, 0))],
        out_shape=[_sds((S, nb * cw), BF16), _sds((nb, cw, nst2), F32), _sds((nb, nst2, cw), F32),
                   _sds((1, nb * cw), F32), _sds((nb, 2, nst), F32)],
        scratch_shapes=[pltpu.VMEM((S, nst2), F32), pltpu.VMEM((S, nst2), F32)],
        compiler_params=_params(("arbitrary",)),
    )(proj, dz, bmat, cmat, dskip, kconst)


def _shift_rows(v, k, row, down):
    n = v.shape[0]
    if down:
        return jnp.where(row >= k, pltpu.roll(v, k, 0), 0.0)
    return jnp.where(row < n - k, pltpu.roll(v, n - k, 0), 0.0)


def _window(v, gi, row, down):
    sums = []
    s = v
    for k in (1, 2, 4, 8):
        s = s + _shift_rows(s, k, row, down)
        sums.append(s)
    out = sums[3]
    for n in (2, 1, 0):
        out = jnp.where(gi == n, sums[n], out)
    return out


def pool_fwd(proj, col0, width, gw):
    S = proj.shape[0]
    cb0 = col0 // gw

    def body(u_ref, o_ref):
        gi = pl.program_id(0)
        u = u_ref[...]
        row = lax.broadcasted_iota(jnp.int32, u.shape, 0)
        w = jnp.left_shift(2, gi)
        count = jnp.minimum(row + 1, w).astype(F32)
        o_ref[...] = (_window(u, gi, row, True) / count - u).astype(BF16)

    return pl.pallas_call(
        body, name="pool_fwd", grid=(len(POOL_WINDOWS),),
        in_specs=[pl.BlockSpec((S, gw), lambda g: (0, cb0 + g))],
        out_specs=pl.BlockSpec((S, gw), lambda g: (0, g)),
        out_shape=_sds((S, width), BF16), compiler_params=_params(("arbitrary",)),
    )(proj)


def pool_bwd(dpooled, gw):
    S, width = dpooled.shape

    def body(d_ref, o_ref):
        gi = pl.program_id(0)
        d = d_ref[...]
        row = lax.broadcasted_iota(jnp.int32, d.shape, 0)
        w = jnp.left_shift(2, gi)
        count = jnp.minimum(row + 1, w).astype(F32)
        o_ref[...] = (_window(d / count, gi, row, False) - d).astype(BF16)

    return pl.pallas_call(
        body, name="pool_bwd", grid=(len(POOL_WINDOWS),),
        in_specs=[pl.BlockSpec((S, gw), lambda g: (0, g))],
        out_specs=pl.BlockSpec((S, gw), lambda g: (0, g)),
        out_shape=_sds((S, width), BF16), compiler_params=_params(("arbitrary",)),
    )(dpooled)


def _place():
    x, y, c = lax.axis_index("x"), lax.axis_index("y"), lax.axis_index("c")
    chips = [(1 - x, y), (x, 1 - y), (1 - x, 1 - y)]
    return x, y, c, chips


HBM = pl.BlockSpec(memory_space=pltpu.HBM)


def _gather_body(n, handshake):
    def body(*refs):
        ins, outs = refs[:n], refs[n:2 * n]
        send_sems, recv_sems, local_sems = refs[2 * n:]
        x, y, c, chips = _place()
        if handshake:
            barrier = pltpu.get_barrier_semaphore()
            for peer in [(x, y, 1 - c)] + [(*chip, c) for chip in chips]:
                pl.semaphore_signal(barrier, inc=1, device_id=peer, device_id_type=MESH)
            pl.semaphore_wait(barrier, 4)
        me, sibling = (x, y, c), (x, y, 1 - c)

        def slot(i, p):
            return outs[i].at[4 * p[0] + 2 * p[1] + p[2]]

        def copy(i, k, block, to, src=None):
            return pltpu.make_async_remote_copy(
                src_ref=slot(i, block) if src is None else src, dst_ref=slot(i, block),
                send_sem=send_sems.at[i, k], recv_sem=recv_sems.at[i, k],
                device_id=to, device_id_type=MESH)

        started = []
        for i in range(n):
            for j, chip in enumerate(chips):
                started.append(copy(i, 1 + j, me, (*chip, c), src=ins[i]))
                started[-1].start()
        for i in range(n):
            started.append(copy(i, 0, me, sibling, src=ins[i]))
            started[-1].start()
        mine = [pltpu.make_async_copy(ins[i], slot(i, me), local_sems.at[i]) for i in range(n)]
        for cp in mine:
            cp.start()
        for i in range(n):
            for j, chip in enumerate(chips):
                copy(i, 1 + j, (*chip, c), me).wait_recv()
                started.append(copy(i, 4 + j, (*chip, c), sibling))
                started[-1].start()
        for i in range(n):
            copy(i, 0, sibling, me).wait_recv()
            for j, chip in enumerate(chips):
                copy(i, 4 + j, (*chip, 1 - c), me).wait_recv()
        for cp in started:
            cp.wait_send()
        for cp in mine:
            cp.wait()

    return body


def _on_sequencer(name, body, arrays, out_sds, sems, collective_id):
    ins = [jax.new_ref(a, memory_space=pltpu.MemorySpace.HBM) for a in arrays]
    outs = [jax.empty_ref(s, memory_space=pltpu.MemorySpace.HBM) for s in out_sds]

    @pl.kernel(mesh=plsc.ScalarSubcoreMesh(axis_name="sequencer", num_cores=1), name=name,
               scratch_types=tuple(sems),
               compiler_params=pltpu.CompilerParams(collective_id=collective_id))
    def launch(*sem_refs):
        body(*ins, *outs, *sem_refs)

    launch()
    return [o[...] for o in outs]


def seq_all_gather(name, shards, collective_id):
    n = len(shards)
    return _on_sequencer(
        name, _gather_body(n, True), shards, [_sds((NDEV,) + s.shape, s.dtype) for s in shards],
        [pltpu.SemaphoreType.DMA((n, 7)), pltpu.SemaphoreType.DMA((n, 7)),
         pltpu.SemaphoreType.DMA((n,))], collective_id)


def pair_exchange(name, grads, collective_id):
    n = len(grads)

    def body(*refs):
        ins, outs = refs[:n], refs[n:2 * n]
        send_sems, recv_sems = refs[2 * n:]
        x, y, c, _ = _place()
        sibling = (x, y, 1 - c)
        barrier = pltpu.get_barrier_semaphore()
        pl.semaphore_signal(barrier, inc=1, device_id=sibling, device_id_type=MESH)
        pl.semaphore_wait(barrier, 1)
        cps = []
        for i in range(n):
            for q in range(NCHIP):
                cps.append(pltpu.make_async_remote_copy(
                    src_ref=ins[i].at[2 * q + 1 - c], dst_ref=outs[i].at[q],
                    send_sem=send_sems.at[i, q], recv_sem=recv_sems.at[i, q],
                    device_id=sibling, device_id_type=MESH))
                cps[-1].start()
        for cp in cps:
            cp.wait()

    return _on_sequencer(
        name, body, grads, [_sds((NCHIP,) + g.shape[1:], g.dtype) for g in grads],
        [pltpu.SemaphoreType.DMA((n, NCHIP)), pltpu.SemaphoreType.DMA((n, NCHIP))], collective_id)


def pair_sum(name, grad, got, my_c):
    shp = grad.shape[1:]
    r, cdim = shp[-2], shp[-1]
    lead = int(math.prod(shp[:-2])) if len(shp) > 2 else 1
    g5 = grad.reshape(NCHIP, 2, lead * r, cdim)
    t4 = got.reshape(NCHIP, lead * r, cdim)
    R = lead * r
    tr = _tile(R, max(8, (1 << 20) // cdim))

    def body(c_ref, g_ref, t_ref, o_ref):
        o_ref[...] = (g_ref[0].astype(F32) + t_ref[...].astype(F32)).astype(o_ref.dtype)

    out = pl.pallas_call(
        body, name=name,
        grid_spec=pltpu.PrefetchScalarGridSpec(
            num_scalar_prefetch=1, grid=(NCHIP, R // tr),
            in_specs=[pl.BlockSpec((1, 1, tr, cdim), lambda q, i, cr: (q, cr[0], i, 0)),
                      pl.BlockSpec((1, tr, cdim), lambda q, i, cr: (q, i, 0))],
            out_specs=pl.BlockSpec((1, tr, cdim), lambda q, i, cr: (q, i, 0))),
        out_shape=_sds((NCHIP, R, cdim), grad.dtype),
        compiler_params=_params(("parallel", "parallel")),
    )(my_c, g5, t4)
    return out


def chip_exchange(name, parts, collective_id):
    n = len(parts)

    def body(*refs):
        ins, outs = refs[:n], refs[n:2 * n]
        send_sems, recv_sems = refs[2 * n:]
        x, y, c, chips = _place()
        barrier = pltpu.get_barrier_semaphore()
        for chip in chips:
            pl.semaphore_signal(barrier, inc=1, device_id=(*chip, c), device_id_type=MESH)
        pl.semaphore_wait(barrier, 3)
        cps = []
        for i in range(n):
            for j, chip in enumerate(chips):
                cps.append(pltpu.make_async_remote_copy(
                    src_ref=ins[i].at[2 * chip[0] + chip[1]], dst_ref=outs[i].at[j],
                    send_sem=send_sems.at[i, j], recv_sem=recv_sems.at[i, j],
                    device_id=(*chip, c), device_id_type=MESH))
                cps[-1].start()
        for cp in cps:
            cp.wait()

    return _on_sequencer(
        name, body, parts, [_sds((3,) + p.shape[1:], p.dtype) for p in parts],
        [pltpu.SemaphoreType.DMA((n, 3)), pltpu.SemaphoreType.DMA((n, 3))], collective_id)


def ada_fwd(c_row, w_ada, b_ada):
    D, cols = w_ada.shape

    def body(c_ref, w_ref, b_ref, mod_ref, call_ref, act8, part, s1, r1, s2, r2):
        x, y, c, _ = _place()
        me = 4 * x + 2 * y + c
        call_ref[me] = c_ref[...]
        cps = []
        for k in range(1, NDEV):
            to = (x ^ (k >> 2), y ^ ((k >> 1) & 1), c ^ (k & 1))
            cps.append(pltpu.make_async_remote_copy(
                src_ref=c_ref, dst_ref=call_ref.at[me], send_sem=s1.at[k - 1],
                recv_sem=r1.at[k - 1], device_id=to, device_id_type=MESH))
            cps[-1].start()
        for cp in cps:
            cp.wait()
        for b in range(NDEV):
            act8[b:b + 1, :] = call_ref[b]
        cv = act8[...]
        act = (cv * _sigmoid(cv)).astype(BF16)
        res = jnp.dot(act, w_ref[...].astype(BF16), preferred_element_type=F32)
        for b in range(NDEV):
            part[b] = res[b:b + 1, :]
        mod_ref[me] = part[me]
        cps = []
        for k in range(1, NDEV):
            to = (x ^ (k >> 2), y ^ ((k >> 1) & 1), c ^ (k & 1))
            dst = 4 * to[0] + 2 * to[1] + to[2]
            cps.append(pltpu.make_async_remote_copy(
                src_ref=part.at[dst], dst_ref=mod_ref.at[me], send_sem=s2.at[k - 1],
                recv_sem=r2.at[k - 1], device_id=to, device_id_type=MESH))
            cps[-1].start()
        for cp in cps:
            cp.wait()
        for b in range(NDEV):
            mod_ref[b] = mod_ref[b] + b_ref[b]

    vm = pl.BlockSpec(memory_space=pltpu.VMEM)
    return pl.pallas_call(
        body, name="ada_fwd", in_specs=[vm, vm, vm], out_specs=[vm, vm],
        out_shape=[_sds((NDEV, 1, cols), F32), _sds((NDEV, 1, D), F32)],
        scratch_shapes=[pltpu.VMEM((NDEV, D), F32), pltpu.VMEM((NDEV, 1, cols), F32),
                        pltpu.SemaphoreType.DMA((NDEV - 1,)), pltpu.SemaphoreType.DMA((NDEV - 1,)),
                        pltpu.SemaphoreType.DMA((NDEV - 1,)), pltpu.SemaphoreType.DMA((NDEV - 1,))],
        compiler_params=pltpu.CompilerParams(vmem_limit_bytes=VMEM_LIMIT),
    )(c_row, w_ada, b_ada.reshape(NDEV, 1, cols))


def _adamw_math(g, w, m, v):
    m2 = ADAM_B1 * m + (1.0 - ADAM_B1) * g
    v2 = ADAM_B2 * v + (1.0 - ADAM_B2) * (g * g)
    m_hat = m2 / (1.0 - ADAM_B1 ** ADAM_STEP)
    v_hat = v2 / (1.0 - ADAM_B2 ** ADAM_STEP)
    delta = -ADAM_LR * (m_hat / (jnp.sqrt(v_hat) + ADAM_EPS) + ADAM_WD * w)
    return delta, m2, v2


def adamw_sharded(name, part4, got3, w, m, v, my_chip):
    shape = w.shape
    cdim = shape[-1]
    R = int(math.prod(shape[:-1]))
    w2, m2, v2 = (t.reshape(R, cdim) for t in (w, m, v))
    tr = _tile(R, max(8, (1 << 19) // cdim))

    def body(q_ref, p_ref, t_ref, w_ref, m_ref, v_ref, g_out, d_out, m_out, v_out):
        g = p_ref[0].astype(F32)
        for j in range(3):
            g = g + t_ref[j].astype(F32)
        d, mn, vn = _adamw_math(g, w_ref[...], m_ref[...], v_ref[...])
        g_out[...] = g
        d_out[...] = d
        m_out[...] = mn
        v_out[...] = vn

    spec = pl.BlockSpec((tr, cdim), lambda i, qr: (i, 0))
    outs = pl.pallas_call(
        body, name=name,
        grid_spec=pltpu.PrefetchScalarGridSpec(
            num_scalar_prefetch=1, grid=(R // tr,),
            in_specs=[pl.BlockSpec((1, tr, cdim), lambda i, qr: (qr[0], i, 0)),
                      pl.BlockSpec((3, tr, cdim), lambda i, qr: (0, i, 0)), spec, spec, spec],
            out_specs=[spec] * 4),
        out_shape=[_sds((R, cdim), F32)] * 4,
        compiler_params=_params(("parallel",)),
    )(my_chip, part4.reshape(NCHIP, R, cdim), got3.reshape(3, R, cdim), w2, m2, v2)
    return [o.reshape(shape) for o in outs]


def adamw_small(parts, w, m, v):
    R = w.shape[0]
    tr = R

    def body(p_ref, w_ref, m_ref, v_ref, g_out, d_out, m_out, v_out):
        g = p_ref[0]
        for j in range(1, NDEV):
            g = g + p_ref[j]
        d, mn, vn = _adamw_math(g, w_ref[...], m_ref[...], v_ref[...])
        g_out[...] = g
        d_out[...] = d
        m_out[...] = mn
        v_out[...] = vn

    spec = pl.BlockSpec((tr, LANES), lambda i: (i, 0))
    return pl.pallas_call(
        body, name="adamw_small", grid=(R // tr,),
        in_specs=[pl.BlockSpec((NDEV, tr, LANES), lambda i: (0, i, 0)), spec, spec, spec],
        out_specs=[spec] * 4, out_shape=[_sds((R, LANES), F32)] * 4,
        compiler_params=_params(("parallel",)),
    )(parts, w, m, v)


def adamw_ada(c_all_t, dmod_all, w, m, v, my_dev):
    D, cols = w.shape
    tr = _tile(D, 256)

    def body(k_ref, c_ref, d_ref, w_ref, m_ref, v_ref, g_out, d_out, m_out, v_out):
        cv = c_ref[...]
        act = cv * _sigmoid(cv)
        dm = d_ref[...]
        g = act[:, 0:1] * dm[0:1, :]
        for b in range(1, NDEV):
            g = g + act[:, b:b + 1] * dm[b:b + 1, :]
        d, mn, vn = _adamw_math(g, w_ref[...], m_ref[...], v_ref[...])
        g_out[...] = g
        d_out[...] = d
        m_out[...] = mn
        v_out[...] = vn

    spec = pl.BlockSpec((tr, cols), lambda i, kr: (i, 0))
    return pl.pallas_call(
        body, name="adamw_ada",
        grid_spec=pltpu.PrefetchScalarGridSpec(
            num_scalar_prefetch=1, grid=(D // tr,),
            in_specs=[pl.BlockSpec((tr, NDEV), lambda i, kr: (i, 0)),
                      pl.BlockSpec((NDEV, cols), lambda i, kr: (0, kr[0])), spec, spec, spec],
            out_specs=[spec] * 4),
        out_shape=[_sds((D, cols), F32)] * 4,
        compiler_params=_params(("parallel",)),
    )(my_dev, c_all_t, dmod_all, w, m, v)


def _blockdiag(t, eye):
    nb, gpb, R, C = t.shape
    return jnp.einsum("bgrc,gk->bgrkc", t, eye).reshape(nb, gpb * R, gpb * C)


def _diag_blocks(t, gpb, R, C):
    nb = t.shape[0]
    t5 = t.reshape(nb, gpb, R, gpb, C)
    idx = jnp.arange(gpb)
    return jnp.moveaxis(t5[:, idx, :, idx, :], 0, 1)


def _small_pack(parts):
    rows = []
    for p in parts:
        flat = p.reshape(-1)
        flat = jnp.pad(flat, (0, (-flat.shape[0]) % (SUBLANES * LANES)))
        rows.append(flat.reshape(-1, LANES))
    return jnp.concatenate(rows, axis=0)


def _small_unpack(buf, shapes):
    out, r = [], 0
    for s in shapes:
        n = int(math.prod(s))
        nr = -(-n // (SUBLANES * LANES)) * SUBLANES
        out.append(buf[r:r + nr].reshape(-1)[:n].reshape(s))
        r += nr
    return out


def kernel(x, c, w_ada, b_ada, w_in, lam_re, lam_im, log_dt, ssm_b_re, ssm_b_im, ssm_c_re, ssm_c_im, ssm_d, w_glu_val, w_glu_gate, w_pool, pool_scale, w_pool_out, w_out, ln1_g, ln1_b, w_ff1, w_ff2, ln2_g, ln2_b, loss_target, m_w_ada, m_b_ada, m_w_in, m_lam_re, m_lam_im, m_log_dt, m_ssm_b_re, m_ssm_b_im, m_ssm_c_re, m_ssm_c_im, m_ssm_d, m_w_glu_val, m_w_glu_gate, m_w_pool, m_pool_scale, m_w_pool_out, m_w_out, m_ln1_g, m_ln1_b, m_w_ff1, m_w_ff2, m_ln2_g, m_ln2_b, v_w_ada, v_b_ada, v_w_in, v_lam_re, v_lam_im, v_log_dt, v_ssm_b_re, v_ssm_b_im, v_ssm_c_re, v_ssm_c_im, v_ssm_d, v_w_glu_val, v_w_glu_gate, v_w_pool, v_pool_scale, v_w_pool_out, v_w_out, v_ln1_g, v_ln1_b, v_w_ff1, v_w_ff2, v_ln2_g, v_ln2_b):
    S, D = x.shape[1], x.shape[2]
    x2d, tgt = x[0], loss_target[0]
    W = D // 2
    G = W // SSM_GROUP
    P, H, GPB = SSM_STATE, SSM_GROUP, GROUPS_PER_BLOCK
    nblk = G // GPB
    gw = W // len(POOL_WINDOWS)
    ax, ay, ac = lax.axis_index("x"), lax.axis_index("y"), lax.axis_index("c")
    my_c = ac.astype(jnp.int32).reshape(1)
    my_chip = (2 * ax + ay).astype(jnp.int32).reshape(1)
    my_dev = (4 * ax + 2 * ay + ac).astype(jnp.int32).reshape(1)
    ts = _tile(S, 256)

    glu = jnp.stack([w_glu_val[0], w_glu_gate[0]]).astype(BF16)
    shards = [w_in[0].astype(BF16), glu, w_pool[0].astype(BF16), w_pool_out[0].astype(BF16),
              w_out[0].astype(BF16), w_ff1[0].astype(BF16), w_ff2[0].astype(BF16)]
    (wg_in,) = seq_all_gather("gather_w_in", shards[0:1], 1)
    wg_vg, wg_pool, wg_po, wg_out = seq_all_gather("gather_w_mix", shards[1:5], 2)
    wg_ff1, wg_ff2 = seq_all_gather("gather_w_ff", shards[5:7], 3)
    wg_vg = wg_vg.reshape(2 * NDEV, W, D // NDEV)
    nwin = len(POOL_WINDOWS)
    wp_full = jnp.transpose(wg_pool, (1, 0, 2, 3)).reshape(nwin, gw, gw)
    wout_full = wg_out.reshape(1, D, D)
    wff2_full = wg_ff2.reshape(1, 4 * D, D)

    mod, c_all = ada_fwd(c, w_ada[0], b_ada)
    mod = mod.reshape(6, 1, D)
    sh1, sc1, g1, sh2, sc2, g2 = (mod[i] for i in range(6))

    f2, kconst = s5_disc(lam_re[0], lam_im[0], log_dt[0].reshape(G, 1))
    kconst = kconst.reshape(16, SUBLANES, G * P)
    f2r = f2.reshape(2, 1, G * P)
    bt_re = jnp.transpose(ssm_b_re[0], (2, 0, 1)).reshape(H, G * P)
    bt_im = jnp.transpose(ssm_b_im[0], (2, 0, 1)).reshape(H, G * P)
    bbar = s5_bbar(f2r, bt_re, bt_im)
    eye = jnp.eye(GPB, dtype=F32)
    bb4 = jnp.transpose(bbar.reshape(2, H, nblk, GPB, P), (0, 2, 3, 1, 4))
    bmat = jnp.concatenate([_blockdiag(bb4[0], eye), _blockdiag(bb4[1], eye)], axis=2).astype(BF16)
    c4_re = jnp.transpose(ssm_c_re[0].reshape(nblk, GPB, H, P), (0, 1, 3, 2))
    c4_im = jnp.transpose(ssm_c_im[0].reshape(nblk, GPB, H, P), (0, 1, 3, 2))
    cmat = jnp.concatenate([_blockdiag(c4_re, eye), -_blockdiag(c4_im, eye)], axis=1).astype(BF16)

    def e1(t, b):
        xhat, _ = _ln_stats(t[0])
        return [xhat * (1.0 + b[0]) + b[1]], []
    (h1,) = _rowwise("ln_mod1", e1, S, ts, [(x2d, D, 0)], [sc1, sh1], [(D, BF16)], [])

    (proj,) = mm_nn("proj", h1, wg_in, F32, 1)
    z = s5_fwd(proj, bmat, cmat, ssm_d, kconst)
    (vt,) = mm_nn("glu", z, wg_vg, BF16, 4)
    pooled = pool_fwd(proj, W, W, gw)

    def pool_epi(acc, ex, outs):
        a = acc[...]
        outs[0][...] = a
        outs[1][...] = (a * ex[0][...]).astype(BF16)
    tmp = _tile(S, 1024)
    yp, ypool = _mm(
        "pool_mix", "nn", pooled, wp_full.astype(BF16), (S // tmp, nwin, 1),
        pl.BlockSpec((tmp, gw), lambda i, j, k: (i, j)), pl.BlockSpec((1, gw, gw), lambda i, j, k: (j, 0, 0)),
        [(_sds((S, W), F32), pl.BlockSpec((tmp, gw), lambda i, j, k: (i, j))),
         (_sds((S, W), BF16), pl.BlockSpec((tmp, gw), lambda i, j, k: (i, j)))],
        (tmp, gw), 1, gw, None, pool_epi,
        [(pool_scale, pl.BlockSpec((1, gw), lambda i, j, k: (0, j)))])
    (y_b,) = mm_nn("pool_out", ypool, wg_po, BF16, 4)

    cb = D // NDEV
    ga_cb, gb_cb = (2 * W) // cb, (2 * W + D) // cb
    tsm = _tile(S, 512)

    def merge_call(name, fn, ins, n_out):
        def body(*refs):
            vals = [r[...].astype(F32) for r in refs[:len(ins)]]
            for r, v in zip(refs[len(ins):], fn(*vals)):
                r[...] = v.astype(r.dtype)
        return pl.pallas_call(
            body, name=name, grid=(S // tsm, NDEV),
            in_specs=[pl.BlockSpec((tsm, w), f) for (_, w, f) in ins],
            out_specs=[pl.BlockSpec((tsm, w), lambda i, j: (i, j)) for (_, w) in n_out],
            out_shape=[_sds((S, cols), BF16) for (cols, _) in n_out],
            compiler_params=_params(("parallel", "parallel")),
        )(*[a for (a, _, _) in ins])

    merge_ins = [(proj, cb, lambda i, j: (i, ga_cb + j)), (proj, cb, lambda i, j: (i, gb_cb + j)),
                 (vt, 2 * cb, lambda i, j: (i, j)), (y_b, cb, lambda i, j: (i, j))]

    def merge_f(ga, gb, vtv, yb):
        return [_sigmoid(ga) * (vtv[:, :cb] * _sigmoid(vtv[:, cb:])) + _sigmoid(gb) * yb]
    (merged,) = merge_call("merge", merge_f, merge_ins, [(D, cb)])

    (mix,) = mm_nn("mix_out", merged, wout_full, F32, 1)

    def e3(t, b):
        xv, mx = t
        g1v, l1g, l1b, sc2v, sh2v = b
        r1 = ALPHA * xv + g1v * mx
        xh1, _ = _ln_stats(r1)
        x1 = xh1 * l1g + l1b
        xh, _ = _ln_stats(x1)
        return [r1, xh * (1.0 + sc2v) + sh2v], []
    r1, h2 = _rowwise("post_mix", e3, S, ts, [(x2d, D, 0), (mix, D, 0)],
                      [g1, ln1_g, ln1_b, sc2, sh2], [(D, F32), (D, BF16)], [])

    def relu_epi(acc, ex, outs):
        outs[0][...] = jnp.maximum(acc[...], 0.0).astype(BF16)
    (rl,) = mm_nn("ff1", h2, wg_ff1, BF16, 1, epi=relu_epi)

    def square(a):
        return a * a
    (y2,) = mm_nn("ff2", rl, wff2_full, F32, 1, pro=square)

    def e4(t, b):
        r1v, y2v, tg = t
        g2v, l1g, l1b, l2g, l2b = b
        xh1, _ = _ln_stats(r1v)
        x1 = xh1 * l1g + l1b
        r2 = ALPHA * x1 + g2v * y2v
        xh2, rs2 = _ln_stats(r2)
        err = xh2 * l2g + l2b - tg
        dx2 = err * (1.0 / D)
        dr2 = _ln_bwd(dx2 * l2g, xh2, rs2)
        lsum = jnp.sum(_colsum(err * err), axis=1, keepdims=True) * (0.5 / D)
        return ([ALPHA * dr2, g2v * dr2],
                [jnp.broadcast_to(lsum, (1, LANES)), _colsum(dx2 * xh2), _colsum(dx2), _colsum(dr2 * y2v)])
    dx1a, dy2, loss_acc, g_ln2g, g_ln2b, d_g2 = _rowwise(
        "head", e4, S, ts, [(r1, D, 0), (y2, D, 0), (tgt, D, 0)], [g2, ln1_g, ln1_b, ln2_g, ln2_b],
        [(D, F32), (D, BF16)], [LANES, D, D, D])

    tn_ff = _tile(4 * D, 1024)

    def dff_epi(acc, ex, outs):
        outs[0][...] = (acc[...] * (2.0 * ex[0][...].astype(F32))).astype(BF16)
    tmf = _tile(S, 1024)
    (da1,) = mm_nt("d_ff2", dy2, wff2_full, BF16, 1, tn=tn_ff, epi=dff_epi,
                   extras=[(rl, pl.BlockSpec((tmf, tn_ff), lambda i, j, k: (i, j)))])
    gw_ff2 = mm_tn("gw_ff2", rl, dy2, BF16, NDEV, 0, pro=square)
    gw_ff1 = mm_tn("gw_ff1", h2, da1, BF16, NDEV, 1)
    grads_a = [gw_ff2, gw_ff1]
    got_a = pair_exchange("pair_exchange_ff", grads_a, 4)
    (dh2,) = mm_nt("d_ff1", da1, wg_ff1, F32, 2)

    def e5(t, b):
        dh2v, r1v, dx1av, mx = t
        sc2v, l1g, l1b, g1v = b
        xh1, rs1 = _ln_stats(r1v)
        x1 = xh1 * l1g + l1b
        xh, rs = _ln_stats(x1)
        dx1 = dx1av + _ln_bwd(dh2v * (1.0 + sc2v), xh, rs)
        dr1 = _ln_bwd(dx1 * l1g, xh1, rs1)
        return ([ALPHA * dr1, g1v * dr1],
                [_colsum(dh2v * xh), _colsum(dh2v), _colsum(dx1 * xh1), _colsum(dx1), _colsum(dr1 * mx)])
    dxa, dmix, d_sc2, d_sh2, g_ln1g, g_ln1b, d_g1 = _rowwise(
        "post_mix_bwd", e5, S, ts, [(dh2, D, 0), (r1, D, 0), (dx1a, D, 0), (mix, D, 0)],
        [sc2, ln1_g, ln1_b, g1], [(D, F32), (D, BF16)], [D, D, D, D, D])

    (dmerged,) = mm_nt("d_mix_out", dmix, wout_full, BF16, 1)
    gw_out = mm_tn("gw_out", merged, dmix, BF16, NDEV, 0)
    parts_a = [pair_sum("pair_sum_ff%d" % i, g, t, my_c) for i, (g, t) in enumerate(zip(grads_a, got_a))]
    got3_a = chip_exchange("chip_exchange_ff", parts_a, 5)

    def merge_b(ga, gb, vtv, yb, dm):
        vv, tt = vtv[:, :cb], vtv[:, cb:]
        sa, sb, st = _sigmoid(ga), _sigmoid(gb), _sigmoid(tt)
        dya = dm * sa
        return [dm * (vv * st) * sa * (1.0 - sa), dm * yb * sb * (1.0 - sb),
                jnp.concatenate([dya * st, dya * vv * st * (1.0 - st)], axis=1), dm * sb]
    dga, dgb_, dvt, dy_b = merge_call(
        "merge_bwd", merge_b, merge_ins + [(dmerged, cb, lambda i, j: (i, j))],
        [(D, cb), (D, cb), (2 * D, 2 * cb), (D, cb)])

    (dypool,) = mm_nt("d_pool_out", dy_b, wg_po, F32, NDEV)
    gw_po = mm_tn("gw_pool_out", ypool, dy_b, BF16, NDEV, 4)

    def e7(t, b):
        return [t[0] * b[0]], [_colsum(t[0] * t[1])]
    dyp, g_pscale = _rowwise("pool_scale_bwd", e7, S, ts, [(dypool, W, 0), (yp, W, 0)],
                             [pool_scale], [(W, BF16)], [W])
    (dpooled,) = _mm(
        "d_pool_mix", "nt", dyp, wp_full.astype(BF16), (S // tmp, nwin, 1),
        pl.BlockSpec((tmp, gw), lambda i, j, k: (i, j)), pl.BlockSpec((1, gw, gw), lambda i, j, k: (j, 0, 0)),
        [(_sds((S, W), F32), pl.BlockSpec((tmp, gw), lambda i, j, k: (i, j)))], (tmp, gw), 1, gw)
    tkp = _tile(S, 1024)
    gw_pool = _mm(
        "gw_pool", "tn", pooled, dyp, (nwin, 1, S // tkp),
        pl.BlockSpec((tkp, gw), lambda i, j, k: (k, i)), pl.BlockSpec((tkp, gw), lambda i, j, k: (k, i)),
        [(_sds((nwin, gw, gw), BF16), pl.BlockSpec((1, gw, gw), lambda i, j, k: (i, 0, 0)))],
        (gw, gw), 1, gw, stacked_out=True)[0]
    du_pool = pool_bwd(dpooled, gw)

    (dz,) = mm_nt("d_glu", dvt, wg_vg, BF16, NDEV)
    gw_vg = mm_tn("gw_glu", z, dvt, BF16, 2 * NDEV, 4)
    gw_pool_st = jnp.transpose(gw_pool.reshape(nwin, NDEV, gw // NDEV, gw), (1, 0, 2, 3))
    grads_b = [gw_out, gw_po, gw_pool_st, gw_vg.reshape(NDEV, 2, W, D // NDEV)]
    got_b = pair_exchange("pair_exchange_mix", grads_b, 6)
    du_ssm, g_bmat, g_cmat, g_d, g_a = s5_bwd(proj, dz, bmat, cmat, ssm_d, kconst)
    parts_b = [pair_sum("pair_sum_mix%d" % i, g, t, my_c) for i, (g, t) in enumerate(zip(grads_b, got_b))]
    got3_b = chip_exchange("chip_exchange_mix", parts_b, 7)

    dproj = jnp.concatenate([du_ssm, du_pool, dga, dgb_], axis=1)
    gw_in = mm_tn("gw_in", h1, dproj, BF16, NDEV, 1)
    grads_c = [gw_in]
    got_c = pair_exchange("pair_exchange_in", grads_c, 8)
    (dh1,) = mm_nt("d_proj", dproj, wg_in, F32, 2)
    parts_c = [pair_sum("pair_sum_in", gw_in, got_c[0], my_c)]
    got3_c = chip_exchange("chip_exchange_in", parts_c, 9)

    def e10(t, b):
        dh1v, xv, dxav = t
        xh, rs = _ln_stats(xv)
        return ([dxav + _ln_bwd(dh1v * (1.0 + b[0]), xh, rs)],
                [_colsum(dh1v * xh), _colsum(dh1v)])
    grad_x, d_sc1, d_sh1 = _rowwise("ln_mod1_bwd", e10, S, ts, [(dh1, D, 0), (x2d, D, 0), (dxa, D, 0)],
                                    [sc1], [(D, F32)], [D, D])

    gb4 = _diag_blocks(g_bmat[:, :, :GPB * P], GPB, H, P), _diag_blocks(g_bmat[:, :, GPB * P:], GPB, H, P)
    dbb = jnp.stack([jnp.transpose(t, (2, 0, 1, 3)).reshape(H, G * P) for t in gb4])
    g_bt_re, g_bt_im, g_f = s5_bbar_bwd(f2r, bt_re, bt_im, dbb)
    g_b_re = jnp.transpose(g_bt_re.reshape(H, G, P), (1, 2, 0))
    g_b_im = jnp.transpose(g_bt_im.reshape(H, G, P), (1, 2, 0))
    gc_top = _diag_blocks(g_cmat[:, :GPB * P, :], GPB, P, H)
    gc_bot = _diag_blocks(g_cmat[:, GPB * P:, :], GPB, P, H)
    g_c_re = jnp.transpose(gc_top, (0, 1, 3, 2)).reshape(G, H, P)
    g_c_im = -jnp.transpose(gc_bot, (0, 1, 3, 2)).reshape(G, H, P)
    d_ab = jnp.transpose(g_a.reshape(nblk, 2, GPB, P), (1, 0, 2, 3)).reshape(2, G, P)
    g_lr, g_li, g_ldt = s5_disc_bwd(lam_re[0], lam_im[0], log_dt[0].reshape(G, 1), d_ab,
                                    g_f.reshape(2, G, P))

    dmod = jnp.concatenate([d_sh1, d_sc1, d_g1, d_sh2, d_sc2, d_g2], axis=1)
    small_names = [b_ada, lam_re, lam_im, log_dt, ssm_b_re, ssm_b_im, ssm_c_re, ssm_c_im, ssm_d,
                   pool_scale, ln1_g, ln1_b, ln2_g, ln2_b]
    small_m = [m_b_ada, m_lam_re, m_lam_im, m_log_dt, m_ssm_b_re, m_ssm_b_im, m_ssm_c_re, m_ssm_c_im,
               m_ssm_d, m_pool_scale, m_ln1_g, m_ln1_b, m_ln2_g, m_ln2_b]
    small_v = [v_b_ada, v_lam_re, v_lam_im, v_log_dt, v_ssm_b_re, v_ssm_b_im, v_ssm_c_re, v_ssm_c_im,
               v_ssm_d, v_pool_scale, v_ln1_g, v_ln1_b, v_ln2_g, v_ln2_b]
    small_g = [dmod, g_lr, g_li, g_ldt, g_b_re, g_b_im, g_c_re, g_c_im, g_d, g_pscale,
               g_ln1g, g_ln1b, g_ln2g, g_ln2b]
    (parts_all,) = seq_all_gather("gather_small", [_small_pack(small_g)], 10)
    sg, sd, sm, sv = adamw_small(parts_all, _small_pack(small_names), _small_pack(small_m),
                                 _small_pack(small_v))
    shapes = [t.shape for t in small_names]
    sg, sd, sm, sv = (_small_unpack(t, shapes) for t in (sg, sd, sm, sv))

    nmod = 6 * D
    dmod_all = parts_all[:, :nmod // LANES, :].reshape(NDEV, nmod)
    c_all_t = jnp.transpose(c_all.reshape(NDEV, D))
    ada_out = adamw_ada(c_all_t, dmod_all, w_ada[0], m_w_ada[0], v_w_ada[0], my_dev)

    glu_w = jnp.stack([w_glu_val[0], w_glu_gate[0]])
    glu_m = jnp.stack([m_w_glu_val[0], m_w_glu_gate[0]])
    glu_v = jnp.stack([v_w_glu_val[0], v_w_glu_gate[0]])
    wmv = [(w_ff2[0], m_w_ff2[0], v_w_ff2[0]), (w_ff1[0], m_w_ff1[0], v_w_ff1[0]),
           (w_out[0], m_w_out[0], v_w_out[0]), (w_pool_out[0], m_w_pool_out[0], v_w_pool_out[0]),
           (w_pool[0], m_w_pool[0], v_w_pool[0]), (glu_w, glu_m, glu_v),
           (w_in[0], m_w_in[0], v_w_in[0])]
    parts = parts_a + parts_b + parts_c
    got3 = got3_a + got3_b + got3_c
    upd = [adamw_sharded("adamw_%d" % i, p, t, w, m, v, my_chip)
           for i, (p, t, (w, m, v)) in enumerate(zip(parts, got3, wmv))]
    u_ff2, u_ff1, u_out, u_po, u_pool, u_glu, u_in = upd

    loss = lax.psum(loss_acc[0, 0], ("x", "y", "c"))

    def pick(k):
        return [ada_out[k][None], sg_sd[k][0], u_in[k][None]] + [t for t in sg_sd[k][1:9]] + \
               [u_glu[k][0][None], u_glu[k][1][None], u_pool[k][None], sg_sd[k][9], u_po[k][None],
                u_out[k][None], sg_sd[k][10], sg_sd[k][11], u_ff1[k][None], u_ff2[k][None],
                sg_sd[k][12], sg_sd[k][13]]

    sg_sd = [sg, sd, sm, sv]
    return (loss, grad_x[None], *pick(0), *pick(1), *pick(2), *pick(3))
```

```python
import functools
import math

import jax
import jax.numpy as jnp
from jax import lax
from jax.experimental import pallas as pl
from jax.experimental.pallas import tpu as pltpu
from jax.experimental.pallas import tpu_sc as plsc

F32 = jnp.float32
BF16 = jnp.bfloat16
MESH = pl.DeviceIdType.MESH
NDEV = 8
NCHIP = 4

SSM_GROUP = 16
SSM_STATE = 64
GROUPS_PER_BLOCK = 8
POOL_WINDOWS = (2, 4, 8, 16)
LN_EPS = 1e-5
ALPHA = 2.0 ** 0.25
ADAM_LR, ADAM_B1, ADAM_B2, ADAM_EPS, ADAM_WD, ADAM_STEP = 0.001, 0.9, 0.999, 1e-08, 0.01, 10
SUBLANES = 8
LANES = 128
VMEM_LIMIT = 56 * 1024 * 1024


def _params(sem=None, vmem=VMEM_LIMIT):
    return pltpu.CompilerParams(dimension_semantics=sem, vmem_limit_bytes=vmem)


def _tile(n, pref):
    if n <= pref:
        return n
    t = 1 << (pref.bit_length() - 1)
    while n % t:
        t //= 2
    return t


def _cast_epi(acc, ex, outs):
    outs[0][...] = acc[...].astype(outs[0].dtype)


ANY = pl.BlockSpec(memory_space=pl.ANY)


def _with_after(body, n_in, after):
    if not after:
        return body
    n_af = len(after)

    def wrapped(*refs):
        return body(*refs[:n_in], *refs[n_in + n_af:])
    return wrapped


def _mm(name, kind, a, b, grid, a_spec, b_spec, outs, acc_shape, nsub=1, c=None,
        pro=None, epi=None, extras=(), stacked_out=False, after=()):
    nk = grid[2]
    n_ex, n_out = len(extras), len(outs)
    epi_fn = epi

    def body(*refs):
        a_ref, b_ref = refs[0], refs[1]
        ex = refs[2:2 + n_ex]
        out_refs = refs[2 + n_ex:2 + n_ex + n_out]
        acc = refs[-1]
        k = pl.program_id(2)

        @pl.when(k == 0)
        def _():
            acc[...] = jnp.zeros_like(acc)

        av = a_ref[...]
        if pro is not None:
            av = pro(av)
        if kind == "nn":
            for s in range(nsub):
                acc[:, s * c:(s + 1) * c] += jnp.dot(av, b_ref[s], preferred_element_type=F32)
        elif kind == "nt":
            t = acc[...]
            for s in range(nsub):
                t = t + lax.dot_general(av[:, s * c:(s + 1) * c], b_ref[s],
                                        (((1,), (1,)), ((), ())), preferred_element_type=F32)
            acc[...] = t
        else:
            acc[...] += lax.dot_general(av, b_ref[...], (((0,), (0,)), ((), ())),
                                        preferred_element_type=F32)

        @pl.when(k == nk - 1)
        def _():
            if epi_fn is not None:
                epi_fn(acc, ex, out_refs)
            elif stacked_out:
                for s in range(nsub):
                    out_refs[0][s] = acc[:, s * c:(s + 1) * c].astype(out_refs[0].dtype)
            else:
                _cast_epi(acc, ex, out_refs)

    res = pl.pallas_call(
        _with_after(body, 2 + n_ex, after), name=name, grid=grid,
        in_specs=[a_spec, b_spec] + [e[1] for e in extras] + [ANY] * len(after),
        out_specs=[o[1] for o in outs],
        out_shape=[o[0] for o in outs],
        scratch_shapes=[pltpu.VMEM(acc_shape, F32)],
        compiler_params=_params(("parallel", "parallel", "arbitrary")),
    )(a, b, *[e[0] for e in extras], *after)
    return res


def _sds(shape, dtype):
    return jax.ShapeDtypeStruct(shape, dtype)


def mm_nn(name, a, b3, out_dtype, nsub, tm=1024, tk=2048, tn=None, pro=None, epi=None,
          extras=(), extra_outs=(), a_col0=0):
    M = a.shape[0]
    nb, K, cdim = b3.shape
    tm, tk = _tile(M, tm), _tile(K, tk)
    if nb == 1:
        tn = _tile(cdim, tn or 1024)
        nsub, c, nj = 1, tn, cdim // tn
        b_spec = pl.BlockSpec((1, tk, tn), lambda i, j, k: (0, k, j))
        N = cdim
    else:
        c, nj, tn = cdim, nb // nsub, nsub * cdim
        b_spec = pl.BlockSpec((nsub, tk, cdim), lambda i, j, k: (j, k, 0))
        N = nb * cdim
    kb0 = a_col0 // tk
    a_spec = pl.BlockSpec((tm, tk), lambda i, j, k: (i, kb0 + k))
    grid = (M // tm, nj, K // tk)
    o_spec = pl.BlockSpec((tm, tn), lambda i, j, k: (i, j))
    outs = [(_sds((M, N), out_dtype), o_spec)] + [(_sds((M, N), d), o_spec) for d in extra_outs]
    return _mm(name, "nn", a, b3, grid, a_spec, b_spec, outs, (tm, tn), nsub, c, pro, epi, extras)


def mm_nt(name, a, b3, out_dtype, nsub, tm=1024, tn=1024, epi=None, extras=(), extra_outs=(),
          after=()):
    M = a.shape[0]
    nb, N, cdim = b3.shape
    tm, tn = _tile(M, tm), _tile(N, tn)
    if nb == 1:
        tk = _tile(cdim, 2048)
        nsub, c, nk = 1, tk, cdim // tk
        b_spec = pl.BlockSpec((1, tn, tk), lambda i, j, k: (0, j, k))
    else:
        c, nk, tk = cdim, nb // nsub, nsub * cdim
        b_spec = pl.BlockSpec((nsub, tn, cdim), lambda i, j, k: (k, j, 0))
    a_spec = pl.BlockSpec((tm, tk), lambda i, j, k: (i, k))
    grid = (M // tm, N // tn, nk)
    o_spec = pl.BlockSpec((tm, tn), lambda i, j, k: (i, j))
    outs = [(_sds((M, N), out_dtype), o_spec)] + [(_sds((M, N), d), o_spec) for d in extra_outs]
    return _mm(name, "nt", a, b3, grid, a_spec, b_spec, outs, (tm, tn), nsub, c, None, epi, extras,
               after=after)


def mm_tn(name, a, b, out_dtype, nb, nsub, tma=1024, tk=1024, pro=None, a_col0=0, a_cols=None,
          after=()):
    S = a.shape[0]
    Ka = a_cols or a.shape[1]
    N = b.shape[1]
    tk = _tile(S, tk)
    if nsub == 0:
        rows = Ka // nb
        tma = rows if rows <= tma else _tile(rows, tma)
        per = rows // tma
        tn = _tile(N, 1024)
        grid = (Ka // tma, N // tn, S // tk)
        o_spec = pl.BlockSpec((1, tma, tn), lambda i, j, k: (i // per, i % per, j))
        out = _sds((nb, rows, N), out_dtype)
        nsub_k, c = 1, tn
        b_spec = pl.BlockSpec((tk, tn), lambda i, j, k: (k, j))
    else:
        c = N // nb
        tma = _tile(Ka, tma)
        grid = (Ka // tma, nb // nsub, S // tk)
        o_spec = pl.BlockSpec((nsub, tma, c), lambda i, j, k: (j, i, 0))
        out = _sds((nb, Ka, c), out_dtype)
        nsub_k = nsub
        tn = nsub * c
        b_spec = pl.BlockSpec((tk, tn), lambda i, j, k: (k, j))
    ab0 = a_col0 // tma
    a_spec = pl.BlockSpec((tk, tma), lambda i, j, k: (k, ab0 + i))
    return _mm(name, "tn", a, b, grid, a_spec, b_spec, [(out, o_spec)], (tma, tn), nsub_k, c,
               pro, None, (), stacked_out=True, after=after)[0]


def _rowwise(name, fn, S, ts, tiled, bcast, tiled_out, acc_out, after=()):
    nt, nb, no, na = len(tiled), len(bcast), len(tiled_out), len(acc_out)

    def body(*refs):
        tin = [r[...] for r in refs[:nt]]
        bin_ = [r[...] for r in refs[nt:nt + nb]]
        o_refs = refs[nt + nb:nt + nb + no]
        a_refs = refs[nt + nb + no:]
        touts, aouts = fn(tin, bin_)
        for r, v in zip(o_refs, touts):
            r[...] = v.astype(r.dtype)
        i = pl.program_id(0)

        @pl.when(i == 0)
        def _():
            for r, v in zip(a_refs, aouts):
                r[...] = v

        @pl.when(i > 0)
        def _():
            for r, v in zip(a_refs, aouts):
                r[...] += v

    in_specs = [pl.BlockSpec((ts, w), functools.partial(lambda i, cb: (i, cb), cb=cb))
                for (_, w, cb) in tiled]
    in_specs += [pl.BlockSpec(b.shape, lambda i: (0, 0)) for b in bcast]
    out_specs = [pl.BlockSpec((ts, w), lambda i: (i, 0)) for (w, _) in tiled_out]
    out_specs += [pl.BlockSpec((1, w), lambda i: (0, 0)) for w in acc_out]
    out_shape = [_sds((S, w), d) for (w, d) in tiled_out] + [_sds((1, w), F32) for w in acc_out]
    return pl.pallas_call(
        _with_after(body, nt + nb, after), name=name, grid=(S // ts,),
        in_specs=in_specs + [ANY] * len(after), out_specs=out_specs,
        out_shape=out_shape, compiler_params=_params(("arbitrary",)),
    )(*[t[0] for t in tiled], *bcast, *after)


def _ln_stats(v):
    mu = jnp.mean(v, axis=-1, keepdims=True)
    vc = v - mu
    var = jnp.mean(vc * vc, axis=-1, keepdims=True)
    rstd = lax.rsqrt(var + LN_EPS)
    return vc * rstd, rstd


def _ln_bwd(dxhat, xhat, rstd):
    return rstd * (dxhat - jnp.mean(dxhat, axis=-1, keepdims=True)
                   - xhat * jnp.mean(dxhat * xhat, axis=-1, keepdims=True))


def _colsum(v):
    return jnp.sum(v, axis=0, keepdims=True)


def _sigmoid(v):
    return 1.0 / (1.0 + jnp.exp(-v))


_GELU_C = math.sqrt(2.0 / math.pi)


def _gelu(v):
    return 0.5 * v * (1.0 + jnp.tanh(_GELU_C * (v + 0.044715 * v * v * v)))


def _gelu_grad(v):
    t = jnp.tanh(_GELU_C * (v + 0.044715 * v * v * v))
    return 0.5 * (1.0 + t) + 0.5 * v * (1.0 - t * t) * _GELU_C * (1.0 + 3 * 0.044715 * v * v)


def _disc(lr, li, ldt):
    dt = jnp.exp(ldt)
    mag = jnp.exp(lr * dt)
    ang = li * dt
    ab_re = mag * jnp.cos(ang)
    ab_im = mag * jnp.sin(ang)
    num_re = ab_re - 1.0
    num_im = ab_im
    den = lr * lr + li * li
    f_re = (num_re * lr + num_im * li) / den
    f_im = (num_im * lr - num_re * li) / den
    return ab_re, ab_im, f_re, f_im


def _cmul(ar, ai, br, bi):
    return ar * br - ai * bi, ar * bi + ai * br


def s5_disc(lam_re, lam_im, log_dt):
    G, P = lam_re.shape

    def body(lr_ref, li_ref, ldt_ref, f_ref, k_ref):
        ab_re, ab_im, f_re, f_im = _disc(lr_ref[...], li_ref[...], ldt_ref[...])
        f_ref[0] = f_re
        f_ref[1] = f_im
        pr, pi = [ab_re], [ab_im]
        for _ in range(SUBLANES - 1):
            nr, ni = _cmul(pr[-1], pi[-1], ab_re, ab_im)
            pr.append(nr)
            pi.append(ni)
        zero = jnp.zeros_like(ab_re)
        for n, sh in enumerate((1, 2, 4)):
            for r in range(SUBLANES):
                k_ref[2 * n, r] = pr[sh - 1] if r >= sh else zero
                k_ref[2 * n + 1, r] = pi[sh - 1] if r >= sh else zero
                k_ref[8 + 2 * n, r] = pr[sh - 1] if r + sh < SUBLANES else zero
                k_ref[8 + 2 * n + 1, r] = -pi[sh - 1] if r + sh < SUBLANES else zero
        for r in range(SUBLANES):
            k_ref[6, r] = pr[r]
            k_ref[7, r] = pi[r]
            k_ref[14, r] = pr[SUBLANES - 1 - r]
            k_ref[15, r] = -pi[SUBLANES - 1 - r]

    vm = pl.BlockSpec(memory_space=pltpu.VMEM)
    return pl.pallas_call(
        body, name="s5_disc", in_specs=[vm, vm, vm], out_specs=[vm, vm],
        out_shape=[_sds((2, G, P), F32), _sds((16, SUBLANES, G, P), F32)],
    )(lam_re, lam_im, log_dt)


def s5_disc_bwd(lam_re, lam_im, log_dt, d_ab, d_f):
    G, P = lam_re.shape

    def body(lr_ref, li_ref, ldt_ref, dab_ref, df_ref, glr_ref, gli_ref, gdt_ref):
        _, vjp = jax.vjp(_disc, lr_ref[...], li_ref[...], ldt_ref[...])
        glr, gli, gdt = vjp((dab_ref[0], dab_ref[1], df_ref[0], df_ref[1]))
        glr_ref[...] = glr
        gli_ref[...] = gli
        gdt_ref[...] = gdt

    vm = pl.BlockSpec(memory_space=pltpu.VMEM)
    return pl.pallas_call(
        body, name="s5_disc_bwd", in_specs=[vm] * 5, out_specs=[vm] * 3,
        out_shape=[_sds((G, P), F32), _sds((G, P), F32), _sds((G, 1), F32)],
    )(lam_re, lam_im, log_dt, d_ab, d_f)


def s5_bbar(f2, bt_re, bt_im):
    def body(f_ref, br_ref, bi_ref, o_ref):
        fr, fi = f_ref[0], f_ref[1]
        br, bi = br_ref[...], bi_ref[...]
        o_ref[0] = fr * br - fi * bi
        o_ref[1] = fr * bi + fi * br

    vm = pl.BlockSpec(memory_space=pltpu.VMEM)
    return pl.pallas_call(body, name="s5_bbar", in_specs=[vm] * 3, out_specs=vm,
                          out_shape=_sds((2,) + bt_re.shape, F32))(f2, bt_re, bt_im)


def s5_bbar_bwd(f2, bt_re, bt_im, dbb):
    def body(f_ref, br_ref, bi_ref, d_ref, gbr_ref, gbi_ref, gf_ref):
        fr, fi = f_ref[0], f_ref[1]
        br, bi = br_ref[...], bi_ref[...]
        dr, di = d_ref[0], d_ref[1]
        gbr_ref[...] = fr * dr + fi * di
        gbi_ref[...] = fr * di - fi * dr
        gf_ref[0] = _colsum(dr * br + di * bi)
        gf_ref[1] = _colsum(di * br - dr * bi)

    vm = pl.BlockSpec(memory_space=pltpu.VMEM)
    return pl.pallas_call(
        body, name="s5_bbar_bwd", in_specs=[vm] * 4, out_specs=[vm] * 3,
        out_shape=[_sds(bt_re.shape, F32), _sds(bt_re.shape, F32), _sds(f2.shape, F32)],
    )(f2, bt_re, bt_im, dbb)


def _scan_fwd(xs, k_ref, nst):
    ntile = xs.shape[0] // SUBLANES

    def step(t, carry):
        cr, ci = carry
        r0 = pl.multiple_of(t * SUBLANES, SUBLANES)
        xr = xs[pl.ds(r0, SUBLANES), 0:nst]
        xi = xs[pl.ds(r0, SUBLANES), nst:2 * nst]
        for n, sh in enumerate((1, 2, 4)):
            sr = pltpu.roll(xr, sh, 0)
            si = pltpu.roll(xi, sh, 0)
            mr, mi = k_ref[2 * n], k_ref[2 * n + 1]
            xr, xi = xr + mr * sr - mi * si, xi + mr * si + mi * sr
        pr, pi = k_ref[6], k_ref[7]
        xr, xi = xr + pr * cr - pi * ci, xi + pr * ci + pi * cr
        xs[pl.ds(r0, SUBLANES), 0:nst] = xr
        xs[pl.ds(r0, SUBLANES), nst:2 * nst] = xi
        return (jnp.broadcast_to(xr[SUBLANES - 1:SUBLANES, :], xr.shape),
                jnp.broadcast_to(xi[SUBLANES - 1:SUBLANES, :], xi.shape))

    zero = jnp.zeros((SUBLANES, nst), F32)
    lax.fori_loop(0, ntile, step, (zero, zero))


def _scan_bwd(g, xs, k_ref, nst):
    ntile = g.shape[0] // SUBLANES
    row = lax.broadcasted_iota(jnp.int32, (SUBLANES, nst), 0)

    def step(tt, carry):
        cr, ci, ar, ai = carry
        t = ntile - 1 - tt
        r0 = pl.multiple_of(t * SUBLANES, SUBLANES)
        gr = g[pl.ds(r0, SUBLANES), 0:nst]
        gi = g[pl.ds(r0, SUBLANES), nst:2 * nst]
        for n, sh in enumerate((1, 2, 4)):
            sr = pltpu.roll(gr, SUBLANES - sh, 0)
            si = pltpu.roll(gi, SUBLANES - sh, 0)
            mr, mi = k_ref[8 + 2 * n], k_ref[8 + 2 * n + 1]
            gr, gi = gr + mr * sr - mi * si, gi + mr * si + mi * sr
        qr, qi = k_ref[14], k_ref[15]
        gr, gi = gr + qr * cr - qi * ci, gi + qr * ci + qi * cr
        g[pl.ds(r0, SUBLANES), 0:nst] = gr
        g[pl.ds(r0, SUBLANES), nst:2 * nst] = gi
        p0 = pl.multiple_of(jnp.maximum(t - 1, 0) * SUBLANES, SUBLANES)
        live = (t > 0).astype(F32)
        xr = xs[pl.ds(r0, SUBLANES), 0:nst]
        xi = xs[pl.ds(r0, SUBLANES), nst:2 * nst]
        pr = xs[pl.ds(p0, SUBLANES), 0:nst][SUBLANES - 1:SUBLANES, :] * live
        pi = xs[pl.ds(p0, SUBLANES), nst:2 * nst][SUBLANES - 1:SUBLANES, :] * live
        xmr = jnp.where(row == 0, jnp.broadcast_to(pr, xr.shape), pltpu.roll(xr, 1, 0))
        xmi = jnp.where(row == 0, jnp.broadcast_to(pi, xi.shape), pltpu.roll(xi, 1, 0))
        ar = ar + gr * xmr + gi * xmi
        ai = ai + gi * xmr - gr * xmi
        return (jnp.broadcast_to(gr[0:1, :], gr.shape), jnp.broadcast_to(gi[0:1, :], gi.shape),
                ar, ai)

    zero = jnp.zeros((SUBLANES, nst), F32)
    _, _, ar, ai = lax.fori_loop(0, ntile, step, (zero, zero, zero, zero))
    return _colsum(ar), _colsum(ai)


def s5_fwd(proj, bmat, cmat, dskip, kconst):
    S = proj.shape[0]
    nb, cw, nst2 = bmat.shape
    nst = nst2 // 2

    def body(u_ref, b_ref, c_ref, d_ref, k_ref, z_ref, xs):
        u = u_ref[...]
        xs[...] = jnp.dot(u.astype(BF16), b_ref[0], preferred_element_type=F32)
        _scan_fwd(xs, k_ref, nst)
        y = jnp.dot(xs[...].astype(BF16), c_ref[0], preferred_element_type=F32) + d_ref[...] * u
        z_ref[...] = _gelu(y).astype(BF16)

    return pl.pallas_call(
        body, name="s5_fwd", grid=(nb,),
        in_specs=[pl.BlockSpec((S, cw), lambda b: (0, b)),
                  pl.BlockSpec((1, cw, nst2), lambda b: (b, 0, 0)),
                  pl.BlockSpec((1, nst2, cw), lambda b: (b, 0, 0)),
                  pl.BlockSpec((1, cw), lambda b: (0, b)),
                  pl.BlockSpec((16, SUBLANES, nst), lambda b: (0, 0, b))],
        out_specs=pl.BlockSpec((S, cw), lambda b: (0, b)),
        out_shape=_sds((S, nb * cw), BF16),
        scratch_shapes=[pltpu.VMEM((S, nst2), F32)],
        compiler_params=_params(("arbitrary",)),
    )(proj, bmat, cmat, dskip, kconst)


def s5_bwd(proj, dz, bmat, cmat, dskip, kconst, after=()):
    S = proj.shape[0]
    nb, cw, nst2 = bmat.shape
    nst = nst2 // 2

    def body(u_ref, dz_ref, b_ref, c_ref, d_ref, k_ref, du_ref, gb_ref, gc_ref, gd_ref, ga_ref,
             xs, g):
        u = u_ref[...]
        ub = u.astype(BF16)
        bm, cm, d = b_ref[0], c_ref[0], d_ref[...]
        xs[...] = jnp.dot(ub, bm, preferred_element_type=F32)
        _scan_fwd(xs, k_ref, nst)
        xsb = xs[...].astype(BF16)
        y = jnp.dot(xsb, cm, preferred_element_type=F32) + d * u
        dy = dz_ref[...].astype(F32) * _gelu_grad(y)
        gd_ref[...] = _colsum(dy * u)
        dyb = dy.astype(BF16)
        gc_ref[0] = lax.dot_general(xsb, dyb, (((0,), (0,)), ((), ())), preferred_element_type=F32)
        g[...] = lax.dot_general(dyb, cm, (((1,), (1,)), ((), ())), preferred_element_type=F32)
        ar, ai = _scan_bwd(g, xs, k_ref, nst)
        ga_ref[0, 0:1, :] = ar
        ga_ref[0, 1:2, :] = ai
        gb = g[...].astype(BF16)
        du = lax.dot_general(gb, bm, (((1,), (1,)), ((), ())), preferred_element_type=F32) + d * dy
        du_ref[...] = du.astype(BF16)
        gb_ref[0] = lax.dot_general(ub, gb, (((0,), (0,)), ((), ())), preferred_element_type=F32)

    return pl.pallas_call(
        _with_after(body, 6, after), name="s5_bwd", grid=(nb,),
        in_specs=[pl.BlockSpec((S, cw), lambda b: (0, b)),
                  pl.BlockSpec((S, cw), lambda b: (0, b)),
                  pl.BlockSpec((1, cw, nst2), lambda b: (b, 0, 0)),
                  pl.BlockSpec((1, nst2, cw), lambda b: (b, 0, 0)),
                  pl.BlockSpec((1, cw), lambda b: (0, b)),
                  pl.BlockSpec((16, SUBLANES, nst), lambda b: (0, 0, b))] + [ANY] * len(after),
        out_specs=[pl.BlockSpec((S, cw), lambda b: (0, b)),
                   pl.BlockSpec((1, cw, nst2), lambda b: (b, 0, 0)),
                   pl.BlockSpec((1, nst2, cw), lambda b: (b, 0, 0)),
                   pl.BlockSpec((1, cw), lambda b: (0, b)),
                   pl.BlockSpec((1, 2, nst), lambda b: (b, 0, 0))],
        out_shape=[_sds((S, nb * cw), BF16), _sds((nb, cw, nst2), F32), _sds((nb, nst2, cw), F32),
                   _sds((1, nb * cw), F32), _sds((nb, 2, nst), F32)],
        scratch_shapes=[pltpu.VMEM((S, nst2), F32), pltpu.VMEM((S, nst2), F32)],
        compiler_params=_params(("arbitrary",)),
    )(proj, dz, bmat, cmat, dskip, kconst, *after)


def _shift_rows(v, k, row, down):
    n = v.shape[0]
    if down:
        return jnp.where(row >= k, pltpu.roll(v, k, 0), 0.0)
    return jnp.where(row < n - k, pltpu.roll(v, n - k, 0), 0.0)


def _window(v, gi, row, down):
    sums = []
    s = v
    for k in (1, 2, 4, 8):
        s = s + _shift_rows(s, k, row, down)
        sums.append(s)
    out = sums[3]
    for n in (2, 1, 0):
        out = jnp.where(gi == n, sums[n], out)
    return out


def pool_fwd(proj, col0, width, gw):
    S = proj.shape[0]
    cb0 = col0 // gw

    def body(u_ref, o_ref):
        gi = pl.program_id(0)
        u = u_ref[...]
        row = lax.broadcasted_iota(jnp.int32, u.shape, 0)
        w = jnp.left_shift(2, gi)
        count = jnp.minimum(row + 1, w).astype(F32)
        o_ref[...] = (_window(u, gi, row, True) / count - u).astype(BF16)

    return pl.pallas_call(
        body, name="pool_fwd", grid=(len(POOL_WINDOWS),),
        in_specs=[pl.BlockSpec((S, gw), lambda g: (0, cb0 + g))],
        out_specs=pl.BlockSpec((S, gw), lambda g: (0, g)),
        out_shape=_sds((S, width), BF16), compiler_params=_params(("arbitrary",)),
    )(proj)


def pool_bwd(dpooled, gw):
    S, width = dpooled.shape

    def body(d_ref, o_ref):
        gi = pl.program_id(0)
        d = d_ref[...]
        row = lax.broadcasted_iota(jnp.int32, d.shape, 0)
        w = jnp.left_shift(2, gi)
        count = jnp.minimum(row + 1, w).astype(F32)
        o_ref[...] = (_window(d / count, gi, row, False) - d).astype(BF16)

    return pl.pallas_call(
        body, name="pool_bwd", grid=(len(POOL_WINDOWS),),
        in_specs=[pl.BlockSpec((S, gw), lambda g: (0, g))],
        out_specs=pl.BlockSpec((S, gw), lambda g: (0, g)),
        out_shape=_sds((S, width), BF16), compiler_params=_params(("arbitrary",)),
    )(dpooled)


def _place():
    x, y, c = lax.axis_index("x"), lax.axis_index("y"), lax.axis_index("c")
    chips = [(1 - x, y), (x, 1 - y), (1 - x, 1 - y)]
    return x, y, c, chips


HBM = pl.BlockSpec(memory_space=pltpu.HBM)


def _gather_body(n, handshake):
    def body(*refs):
        ins, outs = refs[:n], refs[n:2 * n]
        send_sems, recv_sems, local_sems = refs[2 * n:]
        x, y, c, chips = _place()
        if handshake:
            barrier = pltpu.get_barrier_semaphore()
            for peer in [(x, y, 1 - c)] + [(*chip, c) for chip in chips]:
                pl.semaphore_signal(barrier, inc=1, device_id=peer, device_id_type=MESH)
            pl.semaphore_wait(barrier, 4)
        me, sibling = (x, y, c), (x, y, 1 - c)

        def slot(i, p):
            return outs[i].at[4 * p[0] + 2 * p[1] + p[2]]

        def copy(i, k, block, to, src=None):
            return pltpu.make_async_remote_copy(
                src_ref=slot(i, block) if src is None else src, dst_ref=slot(i, block),
                send_sem=send_sems.at[i, k], recv_sem=recv_sems.at[i, k],
                device_id=to, device_id_type=MESH)

        started = []
        for i in range(n):
            for j, chip in enumerate(chips):
                started.append(copy(i, 1 + j, me, (*chip, c), src=ins[i]))
                started[-1].start()
        for i in range(n):
            started.append(copy(i, 0, me, sibling, src=ins[i]))
            started[-1].start()
        mine = [pltpu.make_async_copy(ins[i], slot(i, me), local_sems.at[i]) for i in range(n)]
        for cp in mine:
            cp.start()
        for i in range(n):
            for j, chip in enumerate(chips):
                copy(i, 1 + j, (*chip, c), me).wait_recv()
                started.append(copy(i, 4 + j, (*chip, c), sibling))
                started[-1].start()
        for i in range(n):
            copy(i, 0, sibling, me).wait_recv()
            for j, chip in enumerate(chips):
                copy(i, 4 + j, (*chip, 1 - c), me).wait_recv()
        for cp in started:
            cp.wait_send()
        for cp in mine:
            cp.wait()

    return body


def _on_sequencer(name, body, arrays, out_sds, sems, collective_id):
    ins = [jax.new_ref(a, memory_space=pltpu.MemorySpace.HBM) for a in arrays]
    outs = [jax.empty_ref(s, memory_space=pltpu.MemorySpace.HBM) for s in out_sds]

    @pl.kernel(mesh=plsc.ScalarSubcoreMesh(axis_name="sequencer", num_cores=1), name=name,
               scratch_types=tuple(sems),
               compiler_params=pltpu.CompilerParams(collective_id=collective_id))
    def launch(*sem_refs):
        body(*ins, *outs, *sem_refs)

    launch()
    return [o[...] for o in outs]


def seq_all_gather(name, shards, collective_id):
    n = len(shards)
    return _on_sequencer(
        name, _gather_body(n, True), shards, [_sds((NDEV,) + s.shape, s.dtype) for s in shards],
        [pltpu.SemaphoreType.DMA((n, 7)), pltpu.SemaphoreType.DMA((n, 7)),
         pltpu.SemaphoreType.DMA((n,))], collective_id)


def pair_exchange(name, grads, collective_id):
    def plan(srcs, lands):
        x, y, c, _ = _place()
        return ([(i, q, srcs[i].at[2 * q + 1 - c], lands[i].at[q], (x, y, 1 - c))
                 for i in range(len(srcs)) for q in range(NCHIP)], [(x, y, 1 - c)])

    return _split_exchange(name, grads, [_sds((NCHIP,) + g.shape[1:], g.dtype) for g in grads],
                           plan, NCHIP, collective_id)


SEM = pl.BlockSpec(memory_space=pltpu.SEMAPHORE)


def _split_exchange(name, srcs, land_sds, plan, ncopy, collective_id):
    n = len(srcs)
    nsem = n * ncopy
    effect = pltpu.SideEffectType.DATAFLOW_SIDE_EFFECTING

    def descriptors(src_refs, land_refs, send_sems, recv_sems):
        copies, peers = plan(src_refs, land_refs)
        return [pltpu.make_async_remote_copy(src_ref=s, dst_ref=d, send_sem=send_sems[i * ncopy + k],
                                             recv_sem=recv_sems[i * ncopy + k], device_id=to,
                                             device_id_type=MESH) for (i, k, s, d, to) in copies], peers

    def start_body(*refs):
        src_refs, land_refs = refs[:n], refs[n:2 * n]
        send_sems, recv_sems = refs[2 * n:2 * n + nsem], refs[2 * n + nsem:2 * n + 2 * nsem]
        token = refs[-1]
        cps, peers = descriptors(src_refs, land_refs, send_sems, recv_sems)
        barrier = pltpu.get_barrier_semaphore()
        for peer in peers:
            pl.semaphore_signal(barrier, inc=1, device_id=peer, device_id_type=MESH)
        pl.semaphore_wait(barrier, len(peers))
        for cp in cps:
            cp.start()
        token[...] = jnp.zeros_like(token)

    lands = [pltpu.with_memory_space_constraint(lax.empty(s.shape, s.dtype), pltpu.HBM) for s in land_sds]
    srcs = [pltpu.with_memory_space_constraint(s, pltpu.HBM) for s in srcs]
    res = pl.pallas_call(
        start_body, name=name + "_start",
        out_shape=(pltpu.SemaphoreType.DMA(()),) * (2 * nsem)
        + tuple(pltpu.HBM(s.shape, s.dtype) for s in srcs)
        + tuple(pltpu.HBM(s.shape, s.dtype) for s in land_sds) + (_sds((SUBLANES, LANES), F32),),
        in_specs=[HBM] * (2 * n),
        out_specs=(SEM,) * (2 * nsem) + (HBM,) * (2 * n) + (pl.BlockSpec(memory_space=pltpu.VMEM),),
        input_output_aliases={i: 2 * nsem + i for i in range(2 * n)},
        compiler_params=pltpu.CompilerParams(has_side_effects=effect, collective_id=collective_id),
    )(*srcs, *lands)
    sems = res[:2 * nsem]
    thru = res[2 * nsem:2 * nsem + 2 * n]
    token = res[-1]

    def wait(after):
        def wait_body(*refs):
            src_refs, land_refs = refs[:n], refs[n:2 * n]
            cps, _ = descriptors(src_refs, land_refs, refs[2 * n:2 * n + nsem],
                                 refs[2 * n + nsem:2 * n + 2 * nsem])
            for cp in cps:
                cp.wait_send()
            for cp in cps:
                cp.wait_recv()

        out = pl.pallas_call(
            wait_body, name=name + "_wait",
            out_shape=tuple(pltpu.HBM(s.shape, s.dtype) for s in srcs)
            + tuple(pltpu.HBM(s.shape, s.dtype) for s in land_sds),
            in_specs=[HBM] * (2 * n) + [SEM] * (2 * nsem) + [pl.BlockSpec(memory_space=pl.ANY)],
            out_specs=(HBM,) * (2 * n),
            input_output_aliases={i: i for i in range(2 * n)},
            compiler_params=pltpu.CompilerParams(has_side_effects=effect),
        )(*thru, *sems, after)
        return list(out[:n]), list(out[n:])

    return token, wait


def pair_sum(name, grad, got, my_c):
    shp = grad.shape[1:]
    r, cdim = shp[-2], shp[-1]
    lead = int(math.prod(shp[:-2])) if len(shp) > 2 else 1
    g5 = grad.reshape(NCHIP, 2, lead * r, cdim)
    t4 = got.reshape(NCHIP, lead * r, cdim)
    R = lead * r
    tr = _tile(R, max(8, (1 << 20) // cdim))

    def body(c_ref, g_ref, t_ref, o_ref):
        o_ref[...] = (g_ref[0].astype(F32) + t_ref[...].astype(F32)).astype(o_ref.dtype)

    out = pl.pallas_call(
        body, name=name,
        grid_spec=pltpu.PrefetchScalarGridSpec(
            num_scalar_prefetch=1, grid=(NCHIP, R // tr),
            in_specs=[pl.BlockSpec((1, 1, tr, cdim), lambda q, i, cr: (q, cr[0], i, 0)),
                      pl.BlockSpec((1, tr, cdim), lambda q, i, cr: (q, i, 0))],
            out_specs=pl.BlockSpec((1, tr, cdim), lambda q, i, cr: (q, i, 0))),
        out_shape=_sds((NCHIP, R, cdim), grad.dtype),
        compiler_params=_params(("parallel", "parallel")),
    )(my_c, g5, t4)
    return out


def chip_exchange(name, parts, collective_id):
    def plan(srcs, lands):
        x, y, c, chips = _place()
        return ([(i, j, srcs[i].at[2 * chip[0] + chip[1]], lands[i].at[j], (*chip, c))
                 for i in range(len(srcs)) for j, chip in enumerate(chips)],
                [(*chip, c) for chip in chips])

    return _split_exchange(name, parts, [_sds((3,) + p.shape[1:], p.dtype) for p in parts],
                           plan, 3, collective_id)


def ada_fwd(c_row, w_ada, b_ada):
    D, cols = w_ada.shape

    def body(c_ref, w_ref, b_ref, mod_ref, call_ref, act8, part, s1, r1, s2, r2):
        x, y, c, _ = _place()
        me = 4 * x + 2 * y + c
        call_ref[me] = c_ref[...]
        cps = []
        for k in range(1, NDEV):
            to = (x ^ (k >> 2), y ^ ((k >> 1) & 1), c ^ (k & 1))
            cps.append(pltpu.make_async_remote_copy(
                src_ref=c_ref, dst_ref=call_ref.at[me], send_sem=s1.at[k - 1],
                recv_sem=r1.at[k - 1], device_id=to, device_id_type=MESH))
            cps[-1].start()
        for cp in cps:
            cp.wait()
        for b in range(NDEV):
            act8[b:b + 1, :] = call_ref[b]
        cv = act8[...]
        act = (cv * _sigmoid(cv)).astype(BF16)
        res = jnp.dot(act, w_ref[...].astype(BF16), preferred_element_type=F32)
        for b in range(NDEV):
            part[b] = res[b:b + 1, :]
        mod_ref[me] = part[me]
        cps = []
        for k in range(1, NDEV):
            to = (x ^ (k >> 2), y ^ ((k >> 1) & 1), c ^ (k & 1))
            dst = 4 * to[0] + 2 * to[1] + to[2]
            cps.append(pltpu.make_async_remote_copy(
                src_ref=part.at[dst], dst_ref=mod_ref.at[me], send_sem=s2.at[k - 1],
                recv_sem=r2.at[k - 1], device_id=to, device_id_type=MESH))
            cps[-1].start()
        for cp in cps:
            cp.wait()
        for b in range(NDEV):
            mod_ref[b] = mod_ref[b] + b_ref[b]

    vm = pl.BlockSpec(memory_space=pltpu.VMEM)
    return pl.pallas_call(
        body, name="ada_fwd", in_specs=[vm, vm, vm], out_specs=[vm, vm],
        out_shape=[_sds((NDEV, 1, cols), F32), _sds((NDEV, 1, D), F32)],
        scratch_shapes=[pltpu.VMEM((NDEV, D), F32), pltpu.VMEM((NDEV, 1, cols), F32),
                        pltpu.SemaphoreType.DMA((NDEV - 1,)), pltpu.SemaphoreType.DMA((NDEV - 1,)),
                        pltpu.SemaphoreType.DMA((NDEV - 1,)), pltpu.SemaphoreType.DMA((NDEV - 1,))],
        compiler_params=pltpu.CompilerParams(vmem_limit_bytes=VMEM_LIMIT),
    )(c_row, w_ada, b_ada.reshape(NDEV, 1, cols))


def _adamw_math(g, w, m, v):
    m2 = ADAM_B1 * m + (1.0 - ADAM_B1) * g
    v2 = ADAM_B2 * v + (1.0 - ADAM_B2) * (g * g)
    m_hat = m2 / (1.0 - ADAM_B1 ** ADAM_STEP)
    v_hat = v2 / (1.0 - ADAM_B2 ** ADAM_STEP)
    delta = -ADAM_LR * (m_hat / (jnp.sqrt(v_hat) + ADAM_EPS) + ADAM_WD * w)
    return delta, m2, v2


def adamw_sharded(name, part4, got3, w, m, v, my_chip):
    shape = w.shape
    cdim = shape[-1]
    R = int(math.prod(shape[:-1]))
    w2, m2, v2 = (t.reshape(R, cdim) for t in (w, m, v))
    tr = _tile(R, max(8, (1 << 19) // cdim))

    def body(q_ref, p_ref, t_ref, w_ref, m_ref, v_ref, g_out, d_out, m_out, v_out):
        g = p_ref[0].astype(F32)
        for j in range(3):
            g = g + t_ref[j].astype(F32)
        d, mn, vn = _adamw_math(g, w_ref[...], m_ref[...], v_ref[...])
        g_out[...] = g
        d_out[...] = d
        m_out[...] = mn
        v_out[...] = vn

    spec = pl.BlockSpec((tr, cdim), lambda i, qr: (i, 0))
    outs = pl.pallas_call(
        body, name=name,
        grid_spec=pltpu.PrefetchScalarGridSpec(
            num_scalar_prefetch=1, grid=(R // tr,),
            in_specs=[pl.BlockSpec((1, tr, cdim), lambda i, qr: (qr[0], i, 0)),
                      pl.BlockSpec((3, tr, cdim), lambda i, qr: (0, i, 0)), spec, spec, spec],
            out_specs=[spec] * 4),
        out_shape=[_sds((R, cdim), F32)] * 4,
        compiler_params=_params(("parallel",)),
    )(my_chip, part4.reshape(NCHIP, R, cdim), got3.reshape(3, R, cdim), w2, m2, v2)
    return [o.reshape(shape) for o in outs]


def adamw_small(parts, w, m, v, after=()):
    R = w.shape[0]
    tr = R

    def body(p_ref, w_ref, m_ref, v_ref, g_out, d_out, m_out, v_out):
        g = p_ref[0]
        for j in range(1, NDEV):
            g = g + p_ref[j]
        d, mn, vn = _adamw_math(g, w_ref[...], m_ref[...], v_ref[...])
        g_out[...] = g
        d_out[...] = d
        m_out[...] = mn
        v_out[...] = vn

    spec = pl.BlockSpec((tr, LANES), lambda i: (i, 0))
    return pl.pallas_call(
        _with_after(body, 4, after), name="adamw_small", grid=(R // tr,),
        in_specs=[pl.BlockSpec((NDEV, tr, LANES), lambda i: (0, i, 0)), spec, spec, spec]
        + [ANY] * len(after),
        out_specs=[spec] * 4, out_shape=[_sds((R, LANES), F32)] * 4,
        compiler_params=_params(("parallel",)),
    )(parts, w, m, v, *after)


def adamw_ada(c_all_t, dmod_all, w, m, v, my_dev):
    D, cols = w.shape
    tr = _tile(D, 256)

    def body(k_ref, c_ref, d_ref, w_ref, m_ref, v_ref, g_out, d_out, m_out, v_out):
        cv = c_ref[...]
        act = cv * _sigmoid(cv)
        dm = d_ref[...]
        g = act[:, 0:1] * dm[0:1, :]
        for b in range(1, NDEV):
            g = g + act[:, b:b + 1] * dm[b:b + 1, :]
        d, mn, vn = _adamw_math(g, w_ref[...], m_ref[...], v_ref[...])
        g_out[...] = g
        d_out[...] = d
        m_out[...] = mn
        v_out[...] = vn

    spec = pl.BlockSpec((tr, cols), lambda i, kr: (i, 0))
    return pl.pallas_call(
        body, name="adamw_ada",
        grid_spec=pltpu.PrefetchScalarGridSpec(
            num_scalar_prefetch=1, grid=(D // tr,),
            in_specs=[pl.BlockSpec((tr, NDEV), lambda i, kr: (i, 0)),
                      pl.BlockSpec((NDEV, cols), lambda i, kr: (0, kr[0])), spec, spec, spec],
            out_specs=[spec] * 4),
        out_shape=[_sds((D, cols), F32)] * 4,
        compiler_params=_params(("parallel",)),
    )(my_dev, c_all_t, dmod_all, w, m, v)


def _blockdiag(t, eye):
    nb, gpb, R, C = t.shape
    return jnp.einsum("bgrc,gk->bgrkc", t, eye).reshape(nb, gpb * R, gpb * C)


def _diag_blocks(t, gpb, R, C):
    nb = t.shape[0]
    t5 = t.reshape(nb, gpb, R, gpb, C)
    idx = jnp.arange(gpb)
    return jnp.moveaxis(t5[:, idx, :, idx, :], 0, 1)


def _small_pack(parts):
    rows = []
    for p in parts:
        flat = p.reshape(-1)
        flat = jnp.pad(flat, (0, (-flat.shape[0]) % (SUBLANES * LANES)))
        rows.append(flat.reshape(-1, LANES))
    return jnp.concatenate(rows, axis=0)


def _small_unpack(buf, shapes):
    out, r = [], 0
    for s in shapes:
        n = int(math.prod(s))
        nr = -(-n // (SUBLANES * LANES)) * SUBLANES
        out.append(buf[r:r + nr].reshape(-1)[:n].reshape(s))
        r += nr
    return out


def kernel(x, c, w_ada, b_ada, w_in, lam_re, lam_im, log_dt, ssm_b_re, ssm_b_im, ssm_c_re, ssm_c_im, ssm_d, w_glu_val, w_glu_gate, w_pool, pool_scale, w_pool_out, w_out, ln1_g, ln1_b, w_ff1, w_ff2, ln2_g, ln2_b, loss_target, m_w_ada, m_b_ada, m_w_in, m_lam_re, m_lam_im, m_log_dt, m_ssm_b_re, m_ssm_b_im, m_ssm_c_re, m_ssm_c_im, m_ssm_d, m_w_glu_val, m_w_glu_gate, m_w_pool, m_pool_scale, m_w_pool_out, m_w_out, m_ln1_g, m_ln1_b, m_w_ff1, m_w_ff2, m_ln2_g, m_ln2_b, v_w_ada, v_b_ada, v_w_in, v_lam_re, v_lam_im, v_log_dt, v_ssm_b_re, v_ssm_b_im, v_ssm_c_re, v_ssm_c_im, v_ssm_d, v_w_glu_val, v_w_glu_gate, v_w_pool, v_pool_scale, v_w_pool_out, v_w_out, v_ln1_g, v_ln1_b, v_w_ff1, v_w_ff2, v_ln2_g, v_ln2_b):
    S, D = x.shape[1], x.shape[2]
    x2d, tgt = x[0], loss_target[0]
    W = D // 2
    G = W // SSM_GROUP
    P, H, GPB = SSM_STATE, SSM_GROUP, GROUPS_PER_BLOCK
    nblk = G // GPB
    gw = W // len(POOL_WINDOWS)
    ax, ay, ac = lax.axis_index("x"), lax.axis_index("y"), lax.axis_index("c")
    my_c = ac.astype(jnp.int32).reshape(1)
    my_chip = (2 * ax + ay).astype(jnp.int32).reshape(1)
    my_dev = (4 * ax + 2 * ay + ac).astype(jnp.int32).reshape(1)
    ts = _tile(S, 256)

    glu = jnp.stack([w_glu_val[0], w_glu_gate[0]]).astype(BF16)
    shards = [w_in[0].astype(BF16), glu, w_pool[0].astype(BF16), w_pool_out[0].astype(BF16),
              w_out[0].astype(BF16), w_ff1[0].astype(BF16), w_ff2[0].astype(BF16)]
    (wg_in,) = seq_all_gather("gather_w_in", shards[0:1], 1)
    wg_vg, wg_pool, wg_po, wg_out = seq_all_gather("gather_w_mix", shards[1:5], 2)
    wg_ff1, wg_ff2 = seq_all_gather("gather_w_ff", shards[5:7], 3)
    wg_vg = wg_vg.reshape(2 * NDEV, W, D // NDEV)
    nwin = len(POOL_WINDOWS)
    wp_full = jnp.transpose(wg_pool, (1, 0, 2, 3)).reshape(nwin, gw, gw)
    wout_full = wg_out.reshape(1, D, D)
    wff2_full = wg_ff2.reshape(1, 4 * D, D)

    mod, c_all = ada_fwd(c, w_ada[0], b_ada)
    mod = mod.reshape(6, 1, D)
    sh1, sc1, g1, sh2, sc2, g2 = (mod[i] for i in range(6))

    f2, kconst = s5_disc(lam_re[0], lam_im[0], log_dt[0].reshape(G, 1))
    kconst = kconst.reshape(16, SUBLANES, G * P)
    f2r = f2.reshape(2, 1, G * P)
    bt_re = jnp.transpose(ssm_b_re[0], (2, 0, 1)).reshape(H, G * P)
    bt_im = jnp.transpose(ssm_b_im[0], (2, 0, 1)).reshape(H, G * P)
    bbar = s5_bbar(f2r, bt_re, bt_im)
    eye = jnp.eye(GPB, dtype=F32)
    bb4 = jnp.transpose(bbar.reshape(2, H, nblk, GPB, P), (0, 2, 3, 1, 4))
    bmat = jnp.concatenate([_blockdiag(bb4[0], eye), _blockdiag(bb4[1], eye)], axis=2).astype(BF16)
    c4_re = jnp.transpose(ssm_c_re[0].reshape(nblk, GPB, H, P), (0, 1, 3, 2))
    c4_im = jnp.transpose(ssm_c_im[0].reshape(nblk, GPB, H, P), (0, 1, 3, 2))
    cmat = jnp.concatenate([_blockdiag(c4_re, eye), -_blockdiag(c4_im, eye)], axis=1).astype(BF16)

    def e1(t, b):
        xhat, _ = _ln_stats(t[0])
        return [xhat * (1.0 + b[0]) + b[1]], []
    (h1,) = _rowwise("ln_mod1", e1, S, ts, [(x2d, D, 0)], [sc1, sh1], [(D, BF16)], [])

    (proj,) = mm_nn("proj", h1, wg_in, F32, 1)
    z = s5_fwd(proj, bmat, cmat, ssm_d, kconst)
    (vt,) = mm_nn("glu", z, wg_vg, BF16, 4)
    pooled = pool_fwd(proj, W, W, gw)

    def pool_epi(acc, ex, outs):
        a = acc[...]
        outs[0][...] = a
        outs[1][...] = (a * ex[0][...]).astype(BF16)
    tmp = _tile(S, 1024)
    yp, ypool = _mm(
        "pool_mix", "nn", pooled, wp_full.astype(BF16), (S // tmp, nwin, 1),
        pl.BlockSpec((tmp, gw), lambda i, j, k: (i, j)), pl.BlockSpec((1, gw, gw), lambda i, j, k: (j, 0, 0)),
        [(_sds((S, W), F32), pl.BlockSpec((tmp, gw), lambda i, j, k: (i, j))),
         (_sds((S, W), BF16), pl.BlockSpec((tmp, gw), lambda i, j, k: (i, j)))],
        (tmp, gw), 1, gw, None, pool_epi,
        [(pool_scale, pl.BlockSpec((1, gw), lambda i, j, k: (0, j)))])
    (y_b,) = mm_nn("pool_out", ypool, wg_po, BF16, 4)

    cb = D // NDEV
    ga_cb, gb_cb = (2 * W) // cb, (2 * W + D) // cb
    tsm = _tile(S, 512)

    def merge_call(name, fn, ins, n_out, after=()):
        def body(*refs):
            vals = [r[...].astype(F32) for r in refs[:len(ins)]]
            for r, v in zip(refs[len(ins):], fn(*vals)):
                r[...] = v.astype(r.dtype)
        return pl.pallas_call(
            _with_after(body, len(ins), after), name=name, grid=(S // tsm, NDEV),
            in_specs=[pl.BlockSpec((tsm, w), f) for (_, w, f) in ins] + [ANY] * len(after),
            out_specs=[pl.BlockSpec((tsm, w), lambda i, j: (i, j)) for (_, w) in n_out],
            out_shape=[_sds((S, cols), BF16) for (cols, _) in n_out],
            compiler_params=_params(("parallel", "parallel")),
        )(*[a for (a, _, _) in ins], *after)

    merge_ins = [(proj, cb, lambda i, j: (i, ga_cb + j)), (proj, cb, lambda i, j: (i, gb_cb + j)),
                 (vt, 2 * cb, lambda i, j: (i, j)), (y_b, cb, lambda i, j: (i, j))]

    def merge_f(ga, gb, vtv, yb):
        return [_sigmoid(ga) * (vtv[:, :cb] * _sigmoid(vtv[:, cb:])) + _sigmoid(gb) * yb]
    (merged,) = merge_call("merge", merge_f, merge_ins, [(D, cb)])

    (mix,) = mm_nn("mix_out", merged, wout_full, F32, 1)

    def e3(t, b):
        xv, mx = t
        g1v, l1g, l1b, sc2v, sh2v = b
        r1 = ALPHA * xv + g1v * mx
        xh1, _ = _ln_stats(r1)
        x1 = xh1 * l1g + l1b
        xh, _ = _ln_stats(x1)
        return [r1, xh * (1.0 + sc2v) + sh2v], []
    r1, h2 = _rowwise("post_mix", e3, S, ts, [(x2d, D, 0), (mix, D, 0)],
                      [g1, ln1_g, ln1_b, sc2, sh2], [(D, F32), (D, BF16)], [])

    def relu_epi(acc, ex, outs):
        outs[0][...] = jnp.maximum(acc[...], 0.0).astype(BF16)
    (rl,) = mm_nn("ff1", h2, wg_ff1, BF16, 1, epi=relu_epi)

    def square(a):
        return a * a
    (y2,) = mm_nn("ff2", rl, wff2_full, F32, 1, pro=square)

    def e4(t, b):
        r1v, y2v, tg = t
        g2v, l1g, l1b, l2g, l2b = b
        xh1, _ = _ln_stats(r1v)
        x1 = xh1 * l1g + l1b
        r2 = ALPHA * x1 + g2v * y2v
        xh2, rs2 = _ln_stats(r2)
        err = xh2 * l2g + l2b - tg
        dx2 = err * (1.0 / D)
        dr2 = _ln_bwd(dx2 * l2g, xh2, rs2)
        lsum = jnp.sum(_colsum(err * err), axis=1, keepdims=True) * (0.5 / D)
        return ([ALPHA * dr2, g2v * dr2],
                [jnp.broadcast_to(lsum, (1, LANES)), _colsum(dx2 * xh2), _colsum(dx2), _colsum(dr2 * y2v)])
    dx1a, dy2, loss_acc, g_ln2g, g_ln2b, d_g2 = _rowwise(
        "head", e4, S, ts, [(r1, D, 0), (y2, D, 0), (tgt, D, 0)], [g2, ln1_g, ln1_b, ln2_g, ln2_b],
        [(D, F32), (D, BF16)], [LANES, D, D, D])

    tn_ff = _tile(4 * D, 1024)

    def dff_epi(acc, ex, outs):
        outs[0][...] = (acc[...] * (2.0 * ex[0][...].astype(F32))).astype(BF16)
    tmf = _tile(S, 1024)
    (da1,) = mm_nt("d_ff2", dy2, wff2_full, BF16, 1, tn=tn_ff, epi=dff_epi,
                   extras=[(rl, pl.BlockSpec((tmf, tn_ff), lambda i, j, k: (i, j)))])
    gw_ff2 = mm_tn("gw_ff2", rl, dy2, BF16, NDEV, 0, pro=square)
    gw_ff1 = mm_tn("gw_ff1", h2, da1, BF16, NDEV, 1)
    tok, wait_pair_a = pair_exchange("pair_exchange_ff", [gw_ff2, gw_ff1], 4)
    (dh2,) = mm_nt("d_ff1", da1, wg_ff1, F32, 2, after=[tok])

    def e5(t, b):
        dh2v, r1v, dx1av, mx = t
        sc2v, l1g, l1b, g1v = b
        xh1, rs1 = _ln_stats(r1v)
        x1 = xh1 * l1g + l1b
        xh, rs = _ln_stats(x1)
        dx1 = dx1av + _ln_bwd(dh2v * (1.0 + sc2v), xh, rs)
        dr1 = _ln_bwd(dx1 * l1g, xh1, rs1)
        return ([ALPHA * dr1, g1v * dr1],
                [_colsum(dh2v * xh), _colsum(dh2v), _colsum(dx1 * xh1), _colsum(dx1), _colsum(dr1 * mx)])
    dxa, dmix, d_sc2, d_sh2, g_ln1g, g_ln1b, d_g1 = _rowwise(
        "post_mix_bwd", e5, S, ts, [(dh2, D, 0), (r1, D, 0), (dx1a, D, 0), (mix, D, 0)],
        [sc2, ln1_g, ln1_b, g1], [(D, F32), (D, BF16)], [D, D, D, D, D])

    (dmerged,) = mm_nt("d_mix_out", dmix, wout_full, BF16, 1)
    gw_out = mm_tn("gw_out", merged, dmix, BF16, NDEV, 0)
    grads_a, got_a = wait_pair_a(gw_out)
    parts_a = [pair_sum("pair_sum_ff%d" % i, g, t, my_c) for i, (g, t) in enumerate(zip(grads_a, got_a))]
    tok, wait_chip_a = chip_exchange("chip_exchange_ff", parts_a, 5)

    def merge_b(ga, gb, vtv, yb, dm):
        vv, tt = vtv[:, :cb], vtv[:, cb:]
        sa, sb, st = _sigmoid(ga), _sigmoid(gb), _sigmoid(tt)
        dya = dm * sa
        return [dm * (vv * st) * sa * (1.0 - sa), dm * yb * sb * (1.0 - sb),
                jnp.concatenate([dya * st, dya * vv * st * (1.0 - st)], axis=1), dm * sb]
    dga, dgb_, dvt, dy_b = merge_call(
        "merge_bwd", merge_b, merge_ins + [(dmerged, cb, lambda i, j: (i, j))],
        [(D, cb), (D, cb), (2 * D, 2 * cb), (D, cb)], after=[tok])

    (dypool,) = mm_nt("d_pool_out", dy_b, wg_po, F32, NDEV)
    gw_po = mm_tn("gw_pool_out", ypool, dy_b, BF16, NDEV, 4)

    def e7(t, b):
        return [t[0] * b[0]], [_colsum(t[0] * t[1])]
    dyp, g_pscale = _rowwise("pool_scale_bwd", e7, S, ts, [(dypool, W, 0), (yp, W, 0)],
                             [pool_scale], [(W, BF16)], [W])
    (dpooled,) = _mm(
        "d_pool_mix", "nt", dyp, wp_full.astype(BF16), (S // tmp, nwin, 1),
        pl.BlockSpec((tmp, gw), lambda i, j, k: (i, j)), pl.BlockSpec((1, gw, gw), lambda i, j, k: (j, 0, 0)),
        [(_sds((S, W), F32), pl.BlockSpec((tmp, gw), lambda i, j, k: (i, j)))], (tmp, gw), 1, gw)
    tkp = _tile(S, 1024)
    gw_pool = _mm(
        "gw_pool", "tn", pooled, dyp, (nwin, 1, S // tkp),
        pl.BlockSpec((tkp, gw), lambda i, j, k: (k, i)), pl.BlockSpec((tkp, gw), lambda i, j, k: (k, i)),
        [(_sds((nwin, gw, gw), BF16), pl.BlockSpec((1, gw, gw), lambda i, j, k: (i, 0, 0)))],
        (gw, gw), 1, gw, stacked_out=True)[0]
    du_pool = pool_bwd(dpooled, gw)

    (dz,) = mm_nt("d_glu", dvt, wg_vg, BF16, NDEV)
    gw_vg = mm_tn("gw_glu", z, dvt, BF16, 2 * NDEV, 4)
    gw_pool_st = jnp.transpose(gw_pool.reshape(nwin, NDEV, gw // NDEV, gw), (1, 0, 2, 3))
    grads_b = [gw_out, gw_po, gw_pool_st, gw_vg.reshape(NDEV, 2, W, D // NDEV)]
    tok, wait_pair_b = pair_exchange("pair_exchange_mix", grads_b, 6)
    du_ssm, g_bmat, g_cmat, g_d, g_a = s5_bwd(proj, dz, bmat, cmat, ssm_d, kconst, after=[tok])
    grads_b, got_b = wait_pair_b(du_ssm)
    parts_b = [pair_sum("pair_sum_mix%d" % i, g, t, my_c) for i, (g, t) in enumerate(zip(grads_b, got_b))]
    tok, wait_chip_b = chip_exchange("chip_exchange_mix", parts_b, 7)

    dproj = jnp.concatenate([du_ssm, du_pool, dga, dgb_], axis=1)
    gw_in = mm_tn("gw_in", h1, dproj, BF16, NDEV, 1, after=[tok])
    tok, wait_pair_c = pair_exchange("pair_exchange_in", [gw_in], 8)
    (dh1,) = mm_nt("d_proj", dproj, wg_in, F32, 2, after=[tok])
    grads_c, got_c = wait_pair_c(dh1)
    parts_c = [pair_sum("pair_sum_in", grads_c[0], got_c[0], my_c)]
    tok, wait_chip_c = chip_exchange("chip_exchange_in", parts_c, 9)

    def e10(t, b):
        dh1v, xv, dxav = t
        xh, rs = _ln_stats(xv)
        return ([dxav + _ln_bwd(dh1v * (1.0 + b[0]), xh, rs)],
                [_colsum(dh1v * xh), _colsum(dh1v)])
    grad_x, d_sc1, d_sh1 = _rowwise("ln_mod1_bwd", e10, S, ts, [(dh1, D, 0), (x2d, D, 0), (dxa, D, 0)],
                                    [sc1], [(D, F32)], [D, D], after=[tok])

    gb4 = _diag_blocks(g_bmat[:, :, :GPB * P], GPB, H, P), _diag_blocks(g_bmat[:, :, GPB * P:], GPB, H, P)
    dbb = jnp.stack([jnp.transpose(t, (2, 0, 1, 3)).reshape(H, G * P) for t in gb4])
    g_bt_re, g_bt_im, g_f = s5_bbar_bwd(f2r, bt_re, bt_im, dbb)
    g_b_re = jnp.transpose(g_bt_re.reshape(H, G, P), (1, 2, 0))
    g_b_im = jnp.transpose(g_bt_im.reshape(H, G, P), (1, 2, 0))
    gc_top = _diag_blocks(g_cmat[:, :GPB * P, :], GPB, P, H)
    gc_bot = _diag_blocks(g_cmat[:, GPB * P:, :], GPB, P, H)
    g_c_re = jnp.transpose(gc_top, (0, 1, 3, 2)).reshape(G, H, P)
    g_c_im = -jnp.transpose(gc_bot, (0, 1, 3, 2)).reshape(G, H, P)
    d_ab = jnp.transpose(g_a.reshape(nblk, 2, GPB, P), (1, 0, 2, 3)).reshape(2, G, P)
    g_lr, g_li, g_ldt = s5_disc_bwd(lam_re[0], lam_im[0], log_dt[0].reshape(G, 1), d_ab,
                                    g_f.reshape(2, G, P))

    dmod = jnp.concatenate([d_sh1, d_sc1, d_g1, d_sh2, d_sc2, d_g2], axis=1)
    small_names = [b_ada, lam_re, lam_im, log_dt, ssm_b_re, ssm_b_im, ssm_c_re, ssm_c_im, ssm_d,
                   pool_scale, ln1_g, ln1_b, ln2_g, ln2_b]
    small_m = [m_b_ada, m_lam_re, m_lam_im, m_log_dt, m_ssm_b_re, m_ssm_b_im, m_ssm_c_re, m_ssm_c_im,
               m_ssm_d, m_pool_scale, m_ln1_g, m_ln1_b, m_ln2_g, m_ln2_b]
    small_v = [v_b_ada, v_lam_re, v_lam_im, v_log_dt, v_ssm_b_re, v_ssm_b_im, v_ssm_c_re, v_ssm_c_im,
               v_ssm_d, v_pool_scale, v_ln1_g, v_ln1_b, v_ln2_g, v_ln2_b]
    small_g = [dmod, g_lr, g_li, g_ldt, g_b_re, g_b_im, g_c_re, g_c_im, g_d, g_pscale,
               g_ln1g, g_ln1b, g_ln2g, g_ln2b]
    packed_g = _small_pack(small_g)
    (parts_all,) = seq_all_gather("gather_small", [packed_g], 10)
    parts_a, got3_a = wait_chip_a(packed_g)
    glu_w = jnp.stack([w_glu_val[0], w_glu_gate[0]])
    glu_m = jnp.stack([m_w_glu_val[0], m_w_glu_gate[0]])
    glu_v = jnp.stack([v_w_glu_val[0], v_w_glu_gate[0]])
    wmv = [(w_ff2[0], m_w_ff2[0], v_w_ff2[0]), (w_ff1[0], m_w_ff1[0], v_w_ff1[0]),
           (w_out[0], m_w_out[0], v_w_out[0]), (w_pool_out[0], m_w_pool_out[0], v_w_pool_out[0]),
           (w_pool[0], m_w_pool[0], v_w_pool[0]), (glu_w, glu_m, glu_v)]
    upd = [adamw_sharded("adamw_%d" % i, p, t, w, m, v, my_chip)
           for i, (p, t, (w, m, v)) in enumerate(zip(parts_a, got3_a, wmv[:2]))]
    parts_b, got3_b = wait_chip_b(upd[-1][0])
    upd += [adamw_sharded("adamw_%d" % (2 + i), p, t, w, m, v, my_chip)
            for i, (p, t, (w, m, v)) in enumerate(zip(parts_b, got3_b, wmv[2:]))]
    u_ff2, u_ff1, u_out, u_po, u_pool, u_glu = upd

    sg, sd, sm, sv = adamw_small(parts_all, _small_pack(small_names), _small_pack(small_m),
                                 _small_pack(small_v), after=[upd[-1][0]])
    shapes = [t.shape for t in small_names]
    sg, sd, sm, sv = (_small_unpack(t, shapes) for t in (sg, sd, sm, sv))

    nmod = 6 * D
    dmod_all = parts_all[:, :nmod // LANES, :].reshape(NDEV, nmod)
    c_all_t = jnp.transpose(c_all.reshape(NDEV, D))
    ada_out = adamw_ada(c_all_t, dmod_all, w_ada[0], m_w_ada[0], v_w_ada[0], my_dev)
    parts_c, got3_c = wait_chip_c(ada_out[0])
    u_in = adamw_sharded("adamw_6", parts_c[0], got3_c[0], w_in[0], m_w_in[0], v_w_in[0], my_chip)

    loss = lax.psum(loss_acc[0, 0], ("x", "y", "c"))

    def pick(k):
        return [ada_out[k][None], sg_sd[k][0], u_in[k][None]] + [t for t in sg_sd[k][1:9]] + \
               [u_glu[k][0][None], u_glu[k][1][None], u_pool[k][None], sg_sd[k][9], u_po[k][None],
                u_out[k][None], sg_sd[k][10], sg_sd[k][11], u_ff1[k][None], u_ff2[k][None],
                sg_sd[k][12], sg_sd[k][13]]

    sg_sd = [sg, sd, sm, sv]
    return (loss, grad_x[None], *pick(0), *pick(1), *pick(2), *pick(3))
```

```python
import functools
import math

import jax
import jax.numpy as jnp
from jax import lax
from jax.experimental import pallas as pl
from jax.experimental.pallas import tpu as pltpu
from jax.experimental.pallas import tpu_sc as plsc

F32 = jnp.float32
BF16 = jnp.bfloat16
MESH = pl.DeviceIdType.MESH
NDEV = 8
NCHIP = 4

SSM_GROUP = 16
SSM_STATE = 64
GROUPS_PER_BLOCK = 8
POOL_WINDOWS = (2, 4, 8, 16)
LN_EPS = 1e-5
ALPHA = 2.0 ** 0.25
ADAM_LR, ADAM_B1, ADAM_B2, ADAM_EPS, ADAM_WD, ADAM_STEP = 0.001, 0.9, 0.999, 1e-08, 0.01, 10
SUBLANES = 8
LANES = 128
VMEM_LIMIT = 56 * 1024 * 1024


def _params(sem=None, vmem=VMEM_LIMIT):
    return pltpu.CompilerParams(dimension_semantics=sem, vmem_limit_bytes=vmem)


def _tile(n, pref):
    if n <= pref:
        return n
    t = 1 << (pref.bit_length() - 1)
    while n % t:
        t //= 2
    return t


def _cast_epi(acc, ex, outs):
    outs[0][...] = acc[...].astype(outs[0].dtype)


ANY = pl.BlockSpec(memory_space=pl.ANY)


def _with_after(body, n_in, after):
    if not after:
        return body
    n_af = len(after)

    def wrapped(*refs):
        return body(*refs[:n_in], *refs[n_in + n_af:])
    return wrapped


def _mm(name, kind, a, b, grid, a_spec, b_spec, outs, acc_shape, nsub=1, c=None,
        pro=None, epi=None, extras=(), stacked_out=False, after=()):
    nk = grid[2]
    n_ex, n_out = len(extras), len(outs)
    epi_fn = epi

    def body(*refs):
        a_ref, b_ref = refs[0], refs[1]
        ex = refs[2:2 + n_ex]
        out_refs = refs[2 + n_ex:2 + n_ex + n_out]
        acc = refs[-1]
        k = pl.program_id(2)

        @pl.when(k == 0)
        def _():
            acc[...] = jnp.zeros_like(acc)

        av = a_ref[...]
        if pro is not None:
            av = pro(av)
        if kind == "nn":
            for s in range(nsub):
                acc[:, s * c:(s + 1) * c] += jnp.dot(av, b_ref[s], preferred_element_type=F32)
        elif kind == "nt":
            t = acc[...]
            for s in range(nsub):
                t = t + lax.dot_general(av[:, s * c:(s + 1) * c], b_ref[s],
                                        (((1,), (1,)), ((), ())), preferred_element_type=F32)
            acc[...] = t
        else:
            acc[...] += lax.dot_general(av, b_ref[...], (((0,), (0,)), ((), ())),
                                        preferred_element_type=F32)

        @pl.when(k == nk - 1)
        def _():
            if epi_fn is not None:
                epi_fn(acc, ex, out_refs)
            elif stacked_out:
                for s in range(nsub):
                    out_refs[0][s] = acc[:, s * c:(s + 1) * c].astype(out_refs[0].dtype)
            else:
                _cast_epi(acc, ex, out_refs)

    res = pl.pallas_call(
        _with_after(body, 2 + n_ex, after), name=name, grid=grid,
        in_specs=[a_spec, b_spec] + [e[1] for e in extras] + [ANY] * len(after),
        out_specs=[o[1] for o in outs],
        out_shape=[o[0] for o in outs],
        scratch_shapes=[pltpu.VMEM(acc_shape, F32)],
        compiler_params=_params(("parallel", "parallel", "arbitrary")),
    )(a, b, *[e[0] for e in extras], *after)
    return res


def _sds(shape, dtype):
    return jax.ShapeDtypeStruct(shape, dtype)


def mm_nn(name, a, b3, out_dtype, nsub, tm=1024, tk=2048, tn=None, pro=None, epi=None,
          extras=(), extra_outs=(), a_col0=0):
    M = a.shape[0]
    nb, K, cdim = b3.shape
    tm, tk = _tile(M, tm), _tile(K, tk)
    if nb == 1:
        tn = _tile(cdim, tn or 1024)
        nsub, c, nj = 1, tn, cdim // tn
        b_spec = pl.BlockSpec((1, tk, tn), lambda i, j, k: (0, k, j))
        N = cdim
    else:
        c, nj, tn = cdim, nb // nsub, nsub * cdim
        b_spec = pl.BlockSpec((nsub, tk, cdim), lambda i, j, k: (j, k, 0))
        N = nb * cdim
    kb0 = a_col0 // tk
    a_spec = pl.BlockSpec((tm, tk), lambda i, j, k: (i, kb0 + k))
    grid = (M // tm, nj, K // tk)
    o_spec = pl.BlockSpec((tm, tn), lambda i, j, k: (i, j))
    outs = [(_sds((M, N), out_dtype), o_spec)] + [(_sds((M, N), d), o_spec) for d in extra_outs]
    return _mm(name, "nn", a, b3, grid, a_spec, b_spec, outs, (tm, tn), nsub, c, pro, epi, extras)


def mm_nt(name, a, b3, out_dtype, nsub, tm=1024, tn=1024, epi=None, extras=(), extra_outs=(),
          after=()):
    M = a.shape[0]
    nb, N, cdim = b3.shape
    tm, tn = _tile(M, tm), _tile(N, tn)
    if nb == 1:
        tk = _tile(cdim, 2048)
        nsub, c, nk = 1, tk, cdim // tk
        b_spec = pl.BlockSpec((1, tn, tk), lambda i, j, k: (0, j, k))
    else:
        c, nk, tk = cdim, nb // nsub, nsub * cdim
        b_spec = pl.BlockSpec((nsub, tn, cdim), lambda i, j, k: (k, j, 0))
    a_spec = pl.BlockSpec((tm, tk), lambda i, j, k: (i, k))
    grid = (M // tm, N // tn, nk)
    o_spec = pl.BlockSpec((tm, tn), lambda i, j, k: (i, j))
    outs = [(_sds((M, N), out_dtype), o_spec)] + [(_sds((M, N), d), o_spec) for d in extra_outs]
    return _mm(name, "nt", a, b3, grid, a_spec, b_spec, outs, (tm, tn), nsub, c, None, epi, extras,
               after=after)


def mm_tn(name, a, b, out_dtype, nb, nsub, tma=1024, tk=1024, pro=None, a_col0=0, a_cols=None,
          after=()):
    S = a.shape[0]
    Ka = a_cols or a.shape[1]
    N = b.shape[1]
    tk = _tile(S, tk)
    if nsub == 0:
        rows = Ka // nb
        tma = rows if rows <= tma else _tile(rows, tma)
        per = rows // tma
        tn = _tile(N, 1024)
        grid = (Ka // tma, N // tn, S // tk)
        o_spec = pl.BlockSpec((1, tma, tn), lambda i, j, k: (i // per, i % per, j))
        out = _sds((nb, rows, N), out_dtype)
        nsub_k, c = 1, tn
        b_spec = pl.BlockSpec((tk, tn), lambda i, j, k: (k, j))
    else:
        c = N // nb
        tma = _tile(Ka, tma)
        grid = (Ka // tma, nb // nsub, S // tk)
        o_spec = pl.BlockSpec((nsub, tma, c), lambda i, j, k: (j, i, 0))
        out = _sds((nb, Ka, c), out_dtype)
        nsub_k = nsub
        tn = nsub * c
        b_spec = pl.BlockSpec((tk, tn), lambda i, j, k: (k, j))
    ab0 = a_col0 // tma
    a_spec = pl.BlockSpec((tk, tma), lambda i, j, k: (k, ab0 + i))
    return _mm(name, "tn", a, b, grid, a_spec, b_spec, [(out, o_spec)], (tma, tn), nsub_k, c,
               pro, None, (), stacked_out=True, after=after)[0]


def _rowwise(name, fn, S, ts, tiled, bcast, tiled_out, acc_out, after=()):
    nt, nb, no, na = len(tiled), len(bcast), len(tiled_out), len(acc_out)

    def body(*refs):
        tin = [r[...] for r in refs[:nt]]
        bin_ = [r[...] for r in refs[nt:nt + nb]]
        o_refs = refs[nt + nb:nt + nb + no]
        a_refs = refs[nt + nb + no:]
        touts, aouts = fn(tin, bin_)
        for r, v in zip(o_refs, touts):
            r[...] = v.astype(r.dtype)
        i = pl.program_id(0)

        @pl.when(i == 0)
        def _():
            for r, v in zip(a_refs, aouts):
                r[...] = v

        @pl.when(i > 0)
        def _():
            for r, v in zip(a_refs, aouts):
                r[...] += v

    in_specs = [pl.BlockSpec((ts, w), functools.partial(lambda i, cb: (i, cb), cb=cb))
                for (_, w, cb) in tiled]
    in_specs += [pl.BlockSpec(b.shape, lambda i: (0, 0)) for b in bcast]
    out_specs = [pl.BlockSpec((ts, w), lambda i: (i, 0)) for (w, _) in tiled_out]
    out_specs += [pl.BlockSpec((1, w), lambda i: (0, 0)) for w in acc_out]
    out_shape = [_sds((S, w), d) for (w, d) in tiled_out] + [_sds((1, w), F32) for w in acc_out]
    return pl.pallas_call(
        _with_after(body, nt + nb, after), name=name, grid=(S // ts,),
        in_specs=in_specs + [ANY] * len(after), out_specs=out_specs,
        out_shape=out_shape, compiler_params=_params(("arbitrary",)),
    )(*[t[0] for t in tiled], *bcast, *after)


def _ln_stats(v):
    mu = jnp.mean(v, axis=-1, keepdims=True)
    vc = v - mu
    var = jnp.mean(vc * vc, axis=-1, keepdims=True)
    rstd = lax.rsqrt(var + LN_EPS)
    return vc * rstd, rstd


def _ln_bwd(dxhat, xhat, rstd):
    return rstd * (dxhat - jnp.mean(dxhat, axis=-1, keepdims=True)
                   - xhat * jnp.mean(dxhat * xhat, axis=-1, keepdims=True))


def _colsum(v):
    return jnp.sum(v, axis=0, keepdims=True)


def _sigmoid(v):
    return 1.0 / (1.0 + jnp.exp(-v))


_GELU_C = math.sqrt(2.0 / math.pi)


def _gelu(v):
    return 0.5 * v * (1.0 + jnp.tanh(_GELU_C * (v + 0.044715 * v * v * v)))


def _gelu_grad(v):
    t = jnp.tanh(_GELU_C * (v + 0.044715 * v * v * v))
    return 0.5 * (1.0 + t) + 0.5 * v * (1.0 - t * t) * _GELU_C * (1.0 + 3 * 0.044715 * v * v)


def _disc(lr, li, ldt):
    dt = jnp.exp(ldt)
    mag = jnp.exp(lr * dt)
    ang = li * dt
    ab_re = mag * jnp.cos(ang)
    ab_im = mag * jnp.sin(ang)
    num_re = ab_re - 1.0
    num_im = ab_im
    den = lr * lr + li * li
    f_re = (num_re * lr + num_im * li) / den
    f_im = (num_im * lr - num_re * li) / den
    return ab_re, ab_im, f_re, f_im


def _cmul(ar, ai, br, bi):
    return ar * br - ai * bi, ar * bi + ai * br


def s5_disc(lam_re, lam_im, log_dt):
    G, P = lam_re.shape

    def body(lr_ref, li_ref, ldt_ref, f_ref, k_ref):
        ab_re, ab_im, f_re, f_im = _disc(lr_ref[...], li_ref[...], ldt_ref[...])
        f_ref[0] = f_re
        f_ref[1] = f_im
        pr, pi = [ab_re], [ab_im]
        for _ in range(SUBLANES - 1):
            nr, ni = _cmul(pr[-1], pi[-1], ab_re, ab_im)
            pr.append(nr)
            pi.append(ni)
        zero = jnp.zeros_like(ab_re)
        for n, sh in enumerate((1, 2, 4)):
            for r in range(SUBLANES):
                k_ref[2 * n, r] = pr[sh - 1] if r >= sh else zero
                k_ref[2 * n + 1, r] = pi[sh - 1] if r >= sh else zero
                k_ref[8 + 2 * n, r] = pr[sh - 1] if r + sh < SUBLANES else zero
                k_ref[8 + 2 * n + 1, r] = -pi[sh - 1] if r + sh < SUBLANES else zero
        for r in range(SUBLANES):
            k_ref[6, r] = pr[r]
            k_ref[7, r] = pi[r]
            k_ref[14, r] = pr[SUBLANES - 1 - r]
            k_ref[15, r] = -pi[SUBLANES - 1 - r]

    vm = pl.BlockSpec(memory_space=pltpu.VMEM)
    return pl.pallas_call(
        body, name="s5_disc", in_specs=[vm, vm, vm], out_specs=[vm, vm],
        out_shape=[_sds((2, G, P), F32), _sds((16, SUBLANES, G, P), F32)],
    )(lam_re, lam_im, log_dt)


def s5_disc_bwd(lam_re, lam_im, log_dt, d_ab, d_f):
    G, P = lam_re.shape

    def body(lr_ref, li_ref, ldt_ref, dab_ref, df_ref, glr_ref, gli_ref, gdt_ref):
        _, vjp = jax.vjp(_disc, lr_ref[...], li_ref[...], ldt_ref[...])
        glr, gli, gdt = vjp((dab_ref[0], dab_ref[1], df_ref[0], df_ref[1]))
        glr_ref[...] = glr
        gli_ref[...] = gli
        gdt_ref[...] = gdt

    vm = pl.BlockSpec(memory_space=pltpu.VMEM)
    return pl.pallas_call(
        body, name="s5_disc_bwd", in_specs=[vm] * 5, out_specs=[vm] * 3,
        out_shape=[_sds((G, P), F32), _sds((G, P), F32), _sds((G, 1), F32)],
    )(lam_re, lam_im, log_dt, d_ab, d_f)


def s5_bbar(f2, bt_re, bt_im):
    def body(f_ref, br_ref, bi_ref, o_ref):
        fr, fi = f_ref[0], f_ref[1]
        br, bi = br_ref[...], bi_ref[...]
        o_ref[0] = fr * br - fi * bi
        o_ref[1] = fr * bi + fi * br

    vm = pl.BlockSpec(memory_space=pltpu.VMEM)
    return pl.pallas_call(body, name="s5_bbar", in_specs=[vm] * 3, out_specs=vm,
                          out_shape=_sds((2,) + bt_re.shape, F32))(f2, bt_re, bt_im)


def s5_bbar_bwd(f2, bt_re, bt_im, dbb):
    def body(f_ref, br_ref, bi_ref, d_ref, gbr_ref, gbi_ref, gf_ref):
        fr, fi = f_ref[0], f_ref[1]
        br, bi = br_ref[...], bi_ref[...]
        dr, di = d_ref[0], d_ref[1]
        gbr_ref[...] = fr * dr + fi * di
        gbi_ref[...] = fr * di - fi * dr
        gf_ref[0] = _colsum(dr * br + di * bi)
        gf_ref[1] = _colsum(di * br - dr * bi)

    vm = pl.BlockSpec(memory_space=pltpu.VMEM)
    return pl.pallas_call(
        body, name="s5_bbar_bwd", in_specs=[vm] * 4, out_specs=[vm] * 3,
        out_shape=[_sds(bt_re.shape, F32), _sds(bt_re.shape, F32), _sds(f2.shape, F32)],
    )(f2, bt_re, bt_im, dbb)


def _scan_fwd(xs, k_ref, nst):
    ntile = xs.shape[0] // SUBLANES

    def step(t, carry):
        cr, ci = carry
        r0 = pl.multiple_of(t * SUBLANES, SUBLANES)
        xr = xs[pl.ds(r0, SUBLANES), 0:nst]
        xi = xs[pl.ds(r0, SUBLANES), nst:2 * nst]
        for n, sh in enumerate((1, 2, 4)):
            sr = pltpu.roll(xr, sh, 0)
            si = pltpu.roll(xi, sh, 0)
            mr, mi = k_ref[2 * n], k_ref[2 * n + 1]
            xr, xi = xr + mr * sr - mi * si, xi + mr * si + mi * sr
        pr, pi = k_ref[6], k_ref[7]
        xr, xi = xr + pr * cr - pi * ci, xi + pr * ci + pi * cr
        xs[pl.ds(r0, SUBLANES), 0:nst] = xr
        xs[pl.ds(r0, SUBLANES), nst:2 * nst] = xi
        return (jnp.broadcast_to(xr[SUBLANES - 1:SUBLANES, :], xr.shape),
                jnp.broadcast_to(xi[SUBLANES - 1:SUBLANES, :], xi.shape))

    zero = jnp.zeros((SUBLANES, nst), F32)
    lax.fori_loop(0, ntile, step, (zero, zero))


def _scan_bwd(g, xs, k_ref, nst):
    ntile = g.shape[0] // SUBLANES
    row = lax.broadcasted_iota(jnp.int32, (SUBLANES, nst), 0)

    def step(tt, carry):
        cr, ci, ar, ai = carry
        t = ntile - 1 - tt
        r0 = pl.multiple_of(t * SUBLANES, SUBLANES)
        gr = g[pl.ds(r0, SUBLANES), 0:nst]
        gi = g[pl.ds(r0, SUBLANES), nst:2 * nst]
        for n, sh in enumerate((1, 2, 4)):
            sr = pltpu.roll(gr, SUBLANES - sh, 0)
            si = pltpu.roll(gi, SUBLANES - sh, 0)
            mr, mi = k_ref[8 + 2 * n], k_ref[8 + 2 * n + 1]
            gr, gi = gr + mr * sr - mi * si, gi + mr * si + mi * sr
        qr, qi = k_ref[14], k_ref[15]
        gr, gi = gr + qr * cr - qi * ci, gi + qr * ci + qi * cr
        g[pl.ds(r0, SUBLANES), 0:nst] = gr
        g[pl.ds(r0, SUBLANES), nst:2 * nst] = gi
        p0 = pl.multiple_of(jnp.maximum(t - 1, 0) * SUBLANES, SUBLANES)
        live = (t > 0).astype(F32)
        xr = xs[pl.ds(r0, SUBLANES), 0:nst]
        xi = xs[pl.ds(r0, SUBLANES), nst:2 * nst]
        pr = xs[pl.ds(p0, SUBLANES), 0:nst][SUBLANES - 1:SUBLANES, :] * live
        pi = xs[pl.ds(p0, SUBLANES), nst:2 * nst][SUBLANES - 1:SUBLANES, :] * live
        xmr = jnp.where(row == 0, jnp.broadcast_to(pr, xr.shape), pltpu.roll(xr, 1, 0))
        xmi = jnp.where(row == 0, jnp.broadcast_to(pi, xi.shape), pltpu.roll(xi, 1, 0))
        ar = ar + gr * xmr + gi * xmi
        ai = ai + gi * xmr - gr * xmi
        return (jnp.broadcast_to(gr[0:1, :], gr.shape), jnp.broadcast_to(gi[0:1, :], gi.shape),
                ar, ai)

    zero = jnp.zeros((SUBLANES, nst), F32)
    _, _, ar, ai = lax.fori_loop(0, ntile, step, (zero, zero, zero, zero))
    return _colsum(ar), _colsum(ai)


def s5_fwd(proj, bmat, cmat, dskip, kconst):
    S = proj.shape[0]
    nb, cw, nst2 = bmat.shape
    nst = nst2 // 2

    def body(u_ref, b_ref, c_ref, d_ref, k_ref, z_ref, xs):
        u = u_ref[...]
        xs[...] = jnp.dot(u.astype(BF16), b_ref[0], preferred_element_type=F32)
        _scan_fwd(xs, k_ref, nst)
        y = jnp.dot(xs[...].astype(BF16), c_ref[0], preferred_element_type=F32) + d_ref[...] * u
        z_ref[...] = _gelu(y).astype(BF16)

    return pl.pallas_call(
        body, name="s5_fwd", grid=(nb,),
        in_specs=[pl.BlockSpec((S, cw), lambda b: (0, b)),
                  pl.BlockSpec((1, cw, nst2), lambda b: (b, 0, 0)),
                  pl.BlockSpec((1, nst2, cw), lambda b: (b, 0, 0)),
                  pl.BlockSpec((1, cw), lambda b: (0, b)),
                  pl.BlockSpec((16, SUBLANES, nst), lambda b: (0, 0, b))],
        out_specs=pl.BlockSpec((S, cw), lambda b: (0, b)),
        out_shape=_sds((S, nb * cw), BF16),
        scratch_shapes=[pltpu.VMEM((S, nst2), F32)],
        compiler_params=_params(("arbitrary",)),
    )(proj, bmat, cmat, dskip, kconst)


def s5_bwd(proj, dz, bmat, cmat, dskip, kconst, after=()):
    S = proj.shape[0]
    nb, cw, nst2 = bmat.shape
    nst = nst2 // 2

    def body(u_ref, dz_ref, b_ref, c_ref, d_ref, k_ref, du_ref, gb_ref, gc_ref, gd_ref, ga_ref,
             xs, g):
        u = u_ref[...]
        ub = u.astype(BF16)
        bm, cm, d = b_ref[0], c_ref[0], d_ref[...]
        xs[...] = jnp.dot(ub, bm, preferred_element_type=F32)
        _scan_fwd(xs, k_ref, nst)
        xsb = xs[...].astype(BF16)
        y = jnp.dot(xsb, cm, preferred_element_type=F32) + d * u
        dy = dz_ref[...].astype(F32) * _gelu_grad(y)
        gd_ref[...] = _colsum(dy * u)
        dyb = dy.astype(BF16)
        gc_ref[0] = lax.dot_general(xsb, dyb, (((0,), (0,)), ((), ())), preferred_element_type=F32)
        g[...] = lax.dot_general(dyb, cm, (((1,), (1,)), ((), ())), preferred_element_type=F32)
        ar, ai = _scan_bwd(g, xs, k_ref, nst)
        ga_ref[0, 0:1, :] = ar
        ga_ref[0, 1:2, :] = ai
        gb = g[...].astype(BF16)
        du = lax.dot_general(gb, bm, (((1,), (1,)), ((), ())), preferred_element_type=F32) + d * dy
        du_ref[...] = du.astype(BF16)
        gb_ref[0] = lax.dot_general(ub, gb, (((0,), (0,)), ((), ())), preferred_element_type=F32)

    return pl.pallas_call(
        _with_after(body, 6, after), name="s5_bwd", grid=(nb,),
        in_specs=[pl.BlockSpec((S, cw), lambda b: (0, b)),
                  pl.BlockSpec((S, cw), lambda b: (0, b)),
                  pl.BlockSpec((1, cw, nst2), lambda b: (b, 0, 0)),
                  pl.BlockSpec((1, nst2, cw), lambda b: (b, 0, 0)),
                  pl.BlockSpec((1, cw), lambda b: (0, b)),
                  pl.BlockSpec((16, SUBLANES, nst), lambda b: (0, 0, b))] + [ANY] * len(after),
        out_specs=[pl.BlockSpec((S, cw), lambda b: (0, b)),
                   pl.BlockSpec((1, cw, nst2), lambda b: (b, 0, 0)),
                   pl.BlockSpec((1, nst2, cw), lambda b: (b, 0, 0)),
                   pl.BlockSpec((1, cw), lambda b: (0, b)),
                   pl.BlockSpec((1, 2, nst), lambda b: (b, 0, 0))],
        out_shape=[_sds((S, nb * cw), BF16), _sds((nb, cw, nst2), F32), _sds((nb, nst2, cw), F32),
                   _sds((1, nb * cw), F32), _sds((nb, 2, nst), F32)],
        scratch_shapes=[pltpu.VMEM((S, nst2), F32), pltpu.VMEM((S, nst2), F32)],
        compiler_params=_params(("arbitrary",)),
    )(proj, dz, bmat, cmat, dskip, kconst, *after)


def _shift_rows(v, k, row, down):
    n = v.shape[0]
    if down:
        return jnp.where(row >= k, pltpu.roll(v, k, 0), 0.0)
    return jnp.where(row < n - k, pltpu.roll(v, n - k, 0), 0.0)


def _window(v, gi, row, down):
    sums = []
    s = v
    for k in (1, 2, 4, 8):
        s = s + _shift_rows(s, k, row, down)
        sums.append(s)
    out = sums[3]
    for n in (2, 1, 0):
        out = jnp.where(gi == n, sums[n], out)
    return out


def pool_fwd(proj, col0, width, gw):
    S = proj.shape[0]
    cb0 = col0 // gw

    def body(u_ref, o_ref):
        gi = pl.program_id(0)
        u = u_ref[...]
        row = lax.broadcasted_iota(jnp.int32, u.shape, 0)
        w = jnp.left_shift(2, gi)
        count = jnp.minimum(row + 1, w).astype(F32)
        o_ref[...] = (_window(u, gi, row, True) / count - u).astype(BF16)

    return pl.pallas_call(
        body, name="pool_fwd", grid=(len(POOL_WINDOWS),),
        in_specs=[pl.BlockSpec((S, gw), lambda g: (0, cb0 + g))],
        out_specs=pl.BlockSpec((S, gw), lambda g: (0, g)),
        out_shape=_sds((S, width), BF16), compiler_params=_params(("arbitrary",)),
    )(proj)


def pool_bwd(dpooled, gw):
    S, width = dpooled.shape

    def body(d_ref, o_ref):
        gi = pl.program_id(0)
        d = d_ref[...]
        row = lax.broadcasted_iota(jnp.int32, d.shape, 0)
        w = jnp.left_shift(2, gi)
        count = jnp.minimum(row + 1, w).astype(F32)
        o_ref[...] = (_window(d / count, gi, row, False) - d).astype(BF16)

    return pl.pallas_call(
        body, name="pool_bwd", grid=(len(POOL_WINDOWS),),
        in_specs=[pl.BlockSpec((S, gw), lambda g: (0, g))],
        out_specs=pl.BlockSpec((S, gw), lambda g: (0, g)),
        out_shape=_sds((S, width), BF16), compiler_params=_params(("arbitrary",)),
    )(dpooled)


def _place():
    x, y, c = lax.axis_index("x"), lax.axis_index("y"), lax.axis_index("c")
    chips = [(1 - x, y), (x, 1 - y), (1 - x, 1 - y)]
    return x, y, c, chips


HBM = pl.BlockSpec(memory_space=pltpu.HBM)


def _gather_body(n, handshake):
    def body(*refs):
        ins, outs = refs[:n], refs[n:2 * n]
        send_sems, recv_sems, local_sems = refs[2 * n:]
        x, y, c, chips = _place()
        if handshake:
            barrier = pltpu.get_barrier_semaphore()
            for peer in [(x, y, 1 - c)] + [(*chip, c) for chip in chips]:
                pl.semaphore_signal(barrier, inc=1, device_id=peer, device_id_type=MESH)
            pl.semaphore_wait(barrier, 4)
        me, sibling = (x, y, c), (x, y, 1 - c)

        def slot(i, p):
            return outs[i].at[4 * p[0] + 2 * p[1] + p[2]]

        def copy(i, k, block, to, src=None):
            return pltpu.make_async_remote_copy(
                src_ref=slot(i, block) if src is None else src, dst_ref=slot(i, block),
                send_sem=send_sems.at[i, k], recv_sem=recv_sems.at[i, k],
                device_id=to, device_id_type=MESH)

        started = []
        for i in range(n):
            for j, chip in enumerate(chips):
                started.append(copy(i, 1 + j, me, (*chip, c), src=ins[i]))
                started[-1].start()
        for i in range(n):
            started.append(copy(i, 0, me, sibling, src=ins[i]))
            started[-1].start()
        mine = [pltpu.make_async_copy(ins[i], slot(i, me), local_sems.at[i]) for i in range(n)]
        for cp in mine:
            cp.start()
        for i in range(n):
            for j, chip in enumerate(chips):
                copy(i, 1 + j, (*chip, c), me).wait_recv()
                started.append(copy(i, 4 + j, (*chip, c), sibling))
                started[-1].start()
        for i in range(n):
            copy(i, 0, sibling, me).wait_recv()
            for j, chip in enumerate(chips):
                copy(i, 4 + j, (*chip, 1 - c), me).wait_recv()
        for cp in started:
            cp.wait_send()
        for cp in mine:
            cp.wait()

    return body


def _routed_gather_body(n):
    def body(*refs):
        ins, outs = refs[:n], refs[n:2 * n]
        send_sems, recv_sems, local_sems = refs[2 * n:]
        x, y, c, (xn, yn, dg) = _place()
        me, sibling = (x, y, c), (x, y, 1 - c)
        barrier = pltpu.get_barrier_semaphore()
        for peer in (sibling, (*xn, c), (*yn, c)):
            pl.semaphore_signal(barrier, inc=1, device_id=peer, device_id_type=MESH)
        pl.semaphore_wait(barrier, 3)

        def piece(i, p, h):
            rows = ins[i].shape[0] // 2
            return outs[i].at[4 * p[0] + 2 * p[1] + p[2], pl.ds(h * rows, rows)]

        def copy(i, k, src, dst, to):
            return pltpu.make_async_remote_copy(src_ref=src, dst_ref=dst, send_sem=send_sems.at[i, k],
                                                recv_sem=recv_sems.at[i, k], device_id=to,
                                                device_id_type=MESH)

        started = []

        def go(cp):
            cp.start()
            started.append(cp)

        for i in range(n):
            rows = ins[i].shape[0] // 2
            for h in range(2):
                own = ins[i].at[pl.ds(h * rows, rows)]
                go(copy(i, 1 + h, own, piece(i, me, h), (*xn, c)))
                go(copy(i, 3 + h, own, piece(i, me, h), (*yn, c)))
        for i in range(n):
            go(copy(i, 0, ins[i], outs[i].at[4 * x + 2 * y + c], sibling))
        mine = [pltpu.make_async_copy(ins[i], outs[i].at[4 * x + 2 * y + c], local_sems.at[i])
                for i in range(n)]
        for cp in mine:
            cp.start()
        for i in range(n):
            for k, chip, h, onward, ksib in ((1, xn, 0, (5, yn), 7), (4, yn, 1, (6, xn), 10),
                                            (2, xn, 1, None, 8), (3, yn, 0, None, 9),
                                            (5, dg, 0, None, 11), (6, dg, 1, None, 12)):
                got = piece(i, (*chip, c), h)
                copy(i, k, got, got, me).wait_recv()
                if onward is not None:
                    go(copy(i, onward[0], got, got, (*onward[1], c)))
                go(copy(i, ksib, got, got, sibling))
        for i in range(n):
            block = outs[i].at[4 * x + 2 * y + 1 - c]
            copy(i, 0, block, block, me).wait_recv()
            for ksib, chip, h in ((7, xn, 0), (10, yn, 1), (8, xn, 1), (9, yn, 0), (11, dg, 0), (12, dg, 1)):
                got = piece(i, (*chip, 1 - c), h)
                copy(i, ksib, got, got, me).wait_recv()
        for cp in started:
            cp.wait_send()
        for cp in mine:
            cp.wait()

    return body


def _on_sequencer(name, body, arrays, out_sds, sems, collective_id):
    ins = [jax.new_ref(a, memory_space=pltpu.MemorySpace.HBM) for a in arrays]
    outs = [jax.empty_ref(s, memory_space=pltpu.MemorySpace.HBM) for s in out_sds]

    @pl.kernel(mesh=plsc.ScalarSubcoreMesh(axis_name="sequencer", num_cores=1), name=name,
               scratch_types=tuple(sems),
               compiler_params=pltpu.CompilerParams(collective_id=collective_id))
    def launch(*sem_refs):
        body(*ins, *outs, *sem_refs)

    launch()
    return [o[...] for o in outs]


def seq_all_gather(name, shards, collective_id, routed=True):
    n = len(shards)
    nsem = 13 if routed else 7
    return _on_sequencer(
        name, _routed_gather_body(n) if routed else _gather_body(n, True), shards,
        [_sds((NDEV,) + s.shape, s.dtype) for s in shards],
        [pltpu.SemaphoreType.DMA((n, nsem)), pltpu.SemaphoreType.DMA((n, nsem)),
         pltpu.SemaphoreType.DMA((n,))], collective_id)


def pair_exchange(name, grads, collective_id):
    def plan(srcs, lands):
        x, y, c, _ = _place()
        return ([(i, q, srcs[i].at[2 * q + 1 - c], lands[i].at[q], (x, y, 1 - c))
                 for i in range(len(srcs)) for q in range(NCHIP)], [(x, y, 1 - c)])

    return _split_exchange(name, grads, [_sds((NCHIP,) + g.shape[1:], g.dtype) for g in grads],
                           plan, NCHIP, collective_id)


SEM = pl.BlockSpec(memory_space=pltpu.SEMAPHORE)


def _split_exchange(name, srcs, land_sds, plan, ncopy, collective_id):
    n = len(srcs)
    nsem = n * ncopy
    effect = pltpu.SideEffectType.DATAFLOW_SIDE_EFFECTING

    def descriptors(src_refs, land_refs, send_sems, recv_sems):
        copies, peers = plan(src_refs, land_refs)
        return [pltpu.make_async_remote_copy(src_ref=s, dst_ref=d, send_sem=send_sems[i * ncopy + k],
                                             recv_sem=recv_sems[i * ncopy + k], device_id=to,
                                             device_id_type=MESH) for (i, k, s, d, to) in copies], peers

    def start_body(*refs):
        src_refs, land_refs = refs[:n], refs[n:2 * n]
        send_sems, recv_sems = refs[2 * n:2 * n + nsem], refs[2 * n + nsem:2 * n + 2 * nsem]
        token = refs[-1]
        cps, peers = descriptors(src_refs, land_refs, send_sems, recv_sems)
        barrier = pltpu.get_barrier_semaphore()
        for peer in peers:
            pl.semaphore_signal(barrier, inc=1, device_id=peer, device_id_type=MESH)
        pl.semaphore_wait(barrier, len(peers))
        for cp in cps:
            cp.start()
        token[...] = jnp.zeros_like(token)

    lands = [pltpu.with_memory_space_constraint(lax.empty(s.shape, s.dtype), pltpu.HBM) for s in land_sds]
    srcs = [pltpu.with_memory_space_constraint(s, pltpu.HBM) for s in srcs]
    res = pl.pallas_call(
        start_body, name=name + "_start",
        out_shape=(pltpu.SemaphoreType.DMA(()),) * (2 * nsem)
        + tuple(pltpu.HBM(s.shape, s.dtype) for s in srcs)
        + tuple(pltpu.HBM(s.shape, s.dtype) for s in land_sds) + (_sds((SUBLANES, LANES), F32),),
        in_specs=[HBM] * (2 * n),
        out_specs=(SEM,) * (2 * nsem) + (HBM,) * (2 * n) + (pl.BlockSpec(memory_space=pltpu.VMEM),),
        input_output_aliases={i: 2 * nsem + i for i in range(2 * n)},
        compiler_params=pltpu.CompilerParams(has_side_effects=effect, collective_id=collective_id),
    )(*srcs, *lands)
    sems = res[:2 * nsem]
    thru = res[2 * nsem:2 * nsem + 2 * n]
    token = res[-1]

    def wait(after):
        def wait_body(*refs):
            src_refs, land_refs = refs[:n], refs[n:2 * n]
            cps, _ = descriptors(src_refs, land_refs, refs[2 * n:2 * n + nsem],
                                 refs[2 * n + nsem:2 * n + 2 * nsem])
            for cp in cps:
                cp.wait_send()
            for cp in cps:
                cp.wait_recv()

        out = pl.pallas_call(
            wait_body, name=name + "_wait",
            out_shape=tuple(pltpu.HBM(s.shape, s.dtype) for s in srcs)
            + tuple(pltpu.HBM(s.shape, s.dtype) for s in land_sds),
            in_specs=[HBM] * (2 * n) + [SEM] * (2 * nsem) + [pl.BlockSpec(memory_space=pl.ANY)],
            out_specs=(HBM,) * (2 * n),
            input_output_aliases={i: i for i in range(2 * n)},
            compiler_params=pltpu.CompilerParams(has_side_effects=effect),
        )(*thru, *sems, after)
        return list(out[:n]), list(out[n:])

    return token, wait


def pair_sum(name, grad, got, my_c):
    shp = grad.shape[1:]
    r, cdim = shp[-2], shp[-1]
    lead = int(math.prod(shp[:-2])) if len(shp) > 2 else 1
    g5 = grad.reshape(NCHIP, 2, lead * r, cdim)
    t4 = got.reshape(NCHIP, lead * r, cdim)
    R = lead * r
    tr = _tile(R, max(8, (1 << 20) // cdim))

    def body(c_ref, g_ref, t_ref, o_ref):
        o_ref[...] = (g_ref[0].astype(F32) + t_ref[...].astype(F32)).astype(o_ref.dtype)

    out = pl.pallas_call(
        body, name=name,
        grid_spec=pltpu.PrefetchScalarGridSpec(
            num_scalar_prefetch=1, grid=(NCHIP, R // tr),
            in_specs=[pl.BlockSpec((1, 1, tr, cdim), lambda q, i, cr: (q, cr[0], i, 0)),
                      pl.BlockSpec((1, tr, cdim), lambda q, i, cr: (q, i, 0))],
            out_specs=pl.BlockSpec((1, tr, cdim), lambda q, i, cr: (q, i, 0))),
        out_shape=_sds((NCHIP, R, cdim), grad.dtype),
        compiler_params=_params(("parallel", "parallel")),
    )(my_c, g5, t4)
    return out


def chip_exchange(name, parts, collective_id):
    def plan(srcs, lands):
        x, y, c, chips = _place()
        return ([(i, j, srcs[i].at[2 * chip[0] + chip[1]], lands[i].at[j], (*chip, c))
                 for i in range(len(srcs)) for j, chip in enumerate(chips)],
                [(*chip, c) for chip in chips])

    return _split_exchange(name, parts, [_sds((3,) + p.shape[1:], p.dtype) for p in parts],
                           plan, 3, collective_id)


def ada_fwd(c_row, w_ada, b_ada):
    D, cols = w_ada.shape

    def body(c_ref, w_ref, b_ref, mod_ref, call_ref, act8, part, s1, r1, s2, r2):
        x, y, c, _ = _place()
        me = 4 * x + 2 * y + c
        call_ref[me] = c_ref[...]
        cps = []
        for k in range(1, NDEV):
            to = (x ^ (k >> 2), y ^ ((k >> 1) & 1), c ^ (k & 1))
            cps.append(pltpu.make_async_remote_copy(
                src_ref=c_ref, dst_ref=call_ref.at[me], send_sem=s1.at[k - 1],
                recv_sem=r1.at[k - 1], device_id=to, device_id_type=MESH))
            cps[-1].start()
        for cp in cps:
            cp.wait()
        for b in range(NDEV):
            act8[b:b + 1, :] = call_ref[b]
        cv = act8[...]
        act = (cv * _sigmoid(cv)).astype(BF16)
        res = jnp.dot(act, w_ref[...].astype(BF16), preferred_element_type=F32)
        for b in range(NDEV):
            part[b] = res[b:b + 1, :]
        mod_ref[me] = part[me]
        cps = []
        for k in range(1, NDEV):
            to = (x ^ (k >> 2), y ^ ((k >> 1) & 1), c ^ (k & 1))
            dst = 4 * to[0] + 2 * to[1] + to[2]
            cps.append(pltpu.make_async_remote_copy(
                src_ref=part.at[dst], dst_ref=mod_ref.at[me], send_sem=s2.at[k - 1],
                recv_sem=r2.at[k - 1], device_id=to, device_id_type=MESH))
            cps[-1].start()
        for cp in cps:
            cp.wait()
        for b in range(NDEV):
            mod_ref[b] = mod_ref[b] + b_ref[b]

    vm = pl.BlockSpec(memory_space=pltpu.VMEM)
    return pl.pallas_call(
        body, name="ada_fwd", in_specs=[vm, vm, vm], out_specs=[vm, vm],
        out_shape=[_sds((NDEV, 1, cols), F32), _sds((NDEV, 1, D), F32)],
        scratch_shapes=[pltpu.VMEM((NDEV, D), F32), pltpu.VMEM((NDEV, 1, cols), F32),
                        pltpu.SemaphoreType.DMA((NDEV - 1,)), pltpu.SemaphoreType.DMA((NDEV - 1,)),
                        pltpu.SemaphoreType.DMA((NDEV - 1,)), pltpu.SemaphoreType.DMA((NDEV - 1,))],
        compiler_params=pltpu.CompilerParams(vmem_limit_bytes=VMEM_LIMIT),
    )(c_row, w_ada, b_ada.reshape(NDEV, 1, cols))


def _adamw_math(g, w, m, v):
    m2 = ADAM_B1 * m + (1.0 - ADAM_B1) * g
    v2 = ADAM_B2 * v + (1.0 - ADAM_B2) * (g * g)
    m_hat = m2 / (1.0 - ADAM_B1 ** ADAM_STEP)
    v_hat = v2 / (1.0 - ADAM_B2 ** ADAM_STEP)
    delta = -ADAM_LR * (m_hat / (jnp.sqrt(v_hat) + ADAM_EPS) + ADAM_WD * w)
    return delta, m2, v2


def adamw_sharded(name, part4, got3, w, m, v, my_chip):
    shape = w.shape
    cdim = shape[-1]
    R = int(math.prod(shape[:-1]))
    w2, m2, v2 = (t.reshape(R, cdim) for t in (w, m, v))
    tr = _tile(R, max(8, (1 << 19) // cdim))

    def body(q_ref, p_ref, t_ref, w_ref, m_ref, v_ref, g_out, d_out, m_out, v_out):
        g = p_ref[0].astype(F32)
        for j in range(3):
            g = g + t_ref[j].astype(F32)
        d, mn, vn = _adamw_math(g, w_ref[...], m_ref[...], v_ref[...])
        g_out[...] = g
        d_out[...] = d
        m_out[...] = mn
        v_out[...] = vn

    spec = pl.BlockSpec((tr, cdim), lambda i, qr: (i, 0))
    outs = pl.pallas_call(
        body, name=name,
        grid_spec=pltpu.PrefetchScalarGridSpec(
            num_scalar_prefetch=1, grid=(R // tr,),
            in_specs=[pl.BlockSpec((1, tr, cdim), lambda i, qr: (qr[0], i, 0)),
                      pl.BlockSpec((3, tr, cdim), lambda i, qr: (0, i, 0)), spec, spec, spec],
            out_specs=[spec] * 4),
        out_shape=[_sds((R, cdim), F32)] * 4,
        compiler_params=_params(("parallel",)),
    )(my_chip, part4.reshape(NCHIP, R, cdim), got3.reshape(3, R, cdim), w2, m2, v2)
    return [o.reshape(shape) for o in outs]


def adamw_small(parts, w, m, v, after=()):
    R = w.shape[0]
    tr = R

    def body(p_ref, w_ref, m_ref, v_ref, g_out, d_out, m_out, v_out):
        g = p_ref[0]
        for j in range(1, NDEV):
            g = g + p_ref[j]
        d, mn, vn = _adamw_math(g, w_ref[...], m_ref[...], v_ref[...])
        g_out[...] = g
        d_out[...] = d
        m_out[...] = mn
        v_out[...] = vn

    spec = pl.BlockSpec((tr, LANES), lambda i: (i, 0))
    return pl.pallas_call(
        _with_after(body, 4, after), name="adamw_small", grid=(R // tr,),
        in_specs=[pl.BlockSpec((NDEV, tr, LANES), lambda i: (0, i, 0)), spec, spec, spec]
        + [ANY] * len(after),
        out_specs=[spec] * 4, out_shape=[_sds((R, LANES), F32)] * 4,
        compiler_params=_params(("parallel",)),
    )(parts, w, m, v, *after)


def adamw_ada(c_all_t, dmod_all, w, m, v, my_dev):
    D, cols = w.shape
    tr = _tile(D, 256)

    def body(k_ref, c_ref, d_ref, w_ref, m_ref, v_ref, g_out, d_out, m_out, v_out):
        cv = c_ref[...]
        act = cv * _sigmoid(cv)
        dm = d_ref[...]
        g = act[:, 0:1] * dm[0:1, :]
        for b in range(1, NDEV):
            g = g + act[:, b:b + 1] * dm[b:b + 1, :]
        d, mn, vn = _adamw_math(g, w_ref[...], m_ref[...], v_ref[...])
        g_out[...] = g
        d_out[...] = d
        m_out[...] = mn
        v_out[...] = vn

    spec = pl.BlockSpec((tr, cols), lambda i, kr: (i, 0))
    return pl.pallas_call(
        body, name="adamw_ada",
        grid_spec=pltpu.PrefetchScalarGridSpec(
            num_scalar_prefetch=1, grid=(D // tr,),
            in_specs=[pl.BlockSpec((tr, NDEV), lambda i, kr: (i, 0)),
                      pl.BlockSpec((NDEV, cols), lambda i, kr: (0, kr[0])), spec, spec, spec],
            out_specs=[spec] * 4),
        out_shape=[_sds((D, cols), F32)] * 4,
        compiler_params=_params(("parallel",)),
    )(my_dev, c_all_t, dmod_all, w, m, v)


def _blockdiag(t, eye):
    nb, gpb, R, C = t.shape
    return jnp.einsum("bgrc,gk->bgrkc", t, eye).reshape(nb, gpb * R, gpb * C)


def _diag_blocks(t, gpb, R, C):
    nb = t.shape[0]
    t5 = t.reshape(nb, gpb, R, gpb, C)
    idx = jnp.arange(gpb)
    return jnp.moveaxis(t5[:, idx, :, idx, :], 0, 1)


def _small_pack(parts):
    rows = []
    for p in parts:
        flat = p.reshape(-1)
        flat = jnp.pad(flat, (0, (-flat.shape[0]) % (SUBLANES * LANES)))
        rows.append(flat.reshape(-1, LANES))
    return jnp.concatenate(rows, axis=0)


def _small_unpack(buf, shapes):
    out, r = [], 0
    for s in shapes:
        n = int(math.prod(s))
        nr = -(-n // (SUBLANES * LANES)) * SUBLANES
        out.append(buf[r:r + nr].reshape(-1)[:n].reshape(s))
        r += nr
    return out


def kernel(x, c, w_ada, b_ada, w_in, lam_re, lam_im, log_dt, ssm_b_re, ssm_b_im, ssm_c_re, ssm_c_im, ssm_d, w_glu_val, w_glu_gate, w_pool, pool_scale, w_pool_out, w_out, ln1_g, ln1_b, w_ff1, w_ff2, ln2_g, ln2_b, loss_target, m_w_ada, m_b_ada, m_w_in, m_lam_re, m_lam_im, m_log_dt, m_ssm_b_re, m_ssm_b_im, m_ssm_c_re, m_ssm_c_im, m_ssm_d, m_w_glu_val, m_w_glu_gate, m_w_pool, m_pool_scale, m_w_pool_out, m_w_out, m_ln1_g, m_ln1_b, m_w_ff1, m_w_ff2, m_ln2_g, m_ln2_b, v_w_ada, v_b_ada, v_w_in, v_lam_re, v_lam_im, v_log_dt, v_ssm_b_re, v_ssm_b_im, v_ssm_c_re, v_ssm_c_im, v_ssm_d, v_w_glu_val, v_w_glu_gate, v_w_pool, v_pool_scale, v_w_pool_out, v_w_out, v_ln1_g, v_ln1_b, v_w_ff1, v_w_ff2, v_ln2_g, v_ln2_b):
    S, D = x.shape[1], x.shape[2]
    x2d, tgt = x[0], loss_target[0]
    W = D // 2
    G = W // SSM_GROUP
    P, H, GPB = SSM_STATE, SSM_GROUP, GROUPS_PER_BLOCK
    nblk = G // GPB
    gw = W // len(POOL_WINDOWS)
    ax, ay, ac = lax.axis_index("x"), lax.axis_index("y"), lax.axis_index("c")
    my_c = ac.astype(jnp.int32).reshape(1)
    my_chip = (2 * ax + ay).astype(jnp.int32).reshape(1)
    my_dev = (4 * ax + 2 * ay + ac).astype(jnp.int32).reshape(1)
    ts = _tile(S, 256)

    glu = jnp.stack([w_glu_val[0], w_glu_gate[0]]).astype(BF16)
    shards = [w_in[0].astype(BF16), glu, w_pool[0].astype(BF16), w_pool_out[0].astype(BF16),
              w_out[0].astype(BF16), w_ff1[0].astype(BF16), w_ff2[0].astype(BF16)]
    (wg_in,) = seq_all_gather("gather_w_in", shards[0:1], 1)
    wg_vg, wg_pool, wg_po, wg_out = seq_all_gather("gather_w_mix", shards[1:5], 2)
    wg_ff1, wg_ff2 = seq_all_gather("gather_w_ff", shards[5:7], 3)
    wg_vg = wg_vg.reshape(2 * NDEV, W, D // NDEV)
    nwin = len(POOL_WINDOWS)
    wp_full = jnp.transpose(wg_pool, (1, 0, 2, 3)).reshape(nwin, gw, gw)
    wout_full = wg_out.reshape(1, D, D)
    wff2_full = wg_ff2.reshape(1, 4 * D, D)

    mod, c_all = ada_fwd(c, w_ada[0], b_ada)
    mod = mod.reshape(6, 1, D)
    sh1, sc1, g1, sh2, sc2, g2 = (mod[i] for i in range(6))

    f2, kconst = s5_disc(lam_re[0], lam_im[0], log_dt[0].reshape(G, 1))
    kconst = kconst.reshape(16, SUBLANES, G * P)
    f2r = f2.reshape(2, 1, G * P)
    bt_re = jnp.transpose(ssm_b_re[0], (2, 0, 1)).reshape(H, G * P)
    bt_im = jnp.transpose(ssm_b_im[0], (2, 0, 1)).reshape(H, G * P)
    bbar = s5_bbar(f2r, bt_re, bt_im)
    eye = jnp.eye(GPB, dtype=F32)
    bb4 = jnp.transpose(bbar.reshape(2, H, nblk, GPB, P), (0, 2, 3, 1, 4))
    bmat = jnp.concatenate([_blockdiag(bb4[0], eye), _blockdiag(bb4[1], eye)], axis=2).astype(BF16)
    c4_re = jnp.transpose(ssm_c_re[0].reshape(nblk, GPB, H, P), (0, 1, 3, 2))
    c4_im = jnp.transpose(ssm_c_im[0].reshape(nblk, GPB, H, P), (0, 1, 3, 2))
    cmat = jnp.concatenate([_blockdiag(c4_re, eye), -_blockdiag(c4_im, eye)], axis=1).astype(BF16)

    def e1(t, b):
        xhat, _ = _ln_stats(t[0])
        return [xhat * (1.0 + b[0]) + b[1]], []
    (h1,) = _rowwise("ln_mod1", e1, S, ts, [(x2d, D, 0)], [sc1, sh1], [(D, BF16)], [])

    (proj,) = mm_nn("proj", h1, wg_in, F32, 1)
    z = s5_fwd(proj, bmat, cmat, ssm_d, kconst)
    (vt,) = mm_nn("glu", z, wg_vg, BF16, 4)
    pooled = pool_fwd(proj, W, W, gw)

    def pool_epi(acc, ex, outs):
        a = acc[...]
        outs[0][...] = a
        outs[1][...] = (a * ex[0][...]).astype(BF16)
    tmp = _tile(S, 1024)
    yp, ypool = _mm(
        "pool_mix", "nn", pooled, wp_full.astype(BF16), (S // tmp, nwin, 1),
        pl.BlockSpec((tmp, gw), lambda i, j, k: (i, j)), pl.BlockSpec((1, gw, gw), lambda i, j, k: (j, 0, 0)),
        [(_sds((S, W), F32), pl.BlockSpec((tmp, gw), lambda i, j, k: (i, j))),
         (_sds((S, W), BF16), pl.BlockSpec((tmp, gw), lambda i, j, k: (i, j)))],
        (tmp, gw), 1, gw, None, pool_epi,
        [(pool_scale, pl.BlockSpec((1, gw), lambda i, j, k: (0, j)))])
    (y_b,) = mm_nn("pool_out", ypool, wg_po, BF16, 4)

    cb = D // NDEV
    ga_cb, gb_cb = (2 * W) // cb, (2 * W + D) // cb
    tsm = _tile(S, 512)

    def merge_call(name, fn, ins, n_out, after=()):
        def body(*refs):
            vals = [r[...].astype(F32) for r in refs[:len(ins)]]
            for r, v in zip(refs[len(ins):], fn(*vals)):
                r[...] = v.astype(r.dtype)
        return pl.pallas_call(
            _with_after(body, len(ins), after), name=name, grid=(S // tsm, NDEV),
            in_specs=[pl.BlockSpec((tsm, w), f) for (_, w, f) in ins] + [ANY] * len(after),
            out_specs=[pl.BlockSpec((tsm, w), lambda i, j: (i, j)) for (_, w) in n_out],
            out_shape=[_sds((S, cols), BF16) for (cols, _) in n_out],
            compiler_params=_params(("parallel", "parallel")),
        )(*[a for (a, _, _) in ins], *after)

    merge_ins = [(proj, cb, lambda i, j: (i, ga_cb + j)), (proj, cb, lambda i, j: (i, gb_cb + j)),
                 (vt, 2 * cb, lambda i, j: (i, j)), (y_b, cb, lambda i, j: (i, j))]

    def merge_f(ga, gb, vtv, yb):
        return [_sigmoid(ga) * (vtv[:, :cb] * _sigmoid(vtv[:, cb:])) + _sigmoid(gb) * yb]
    (merged,) = merge_call("merge", merge_f, merge_ins, [(D, cb)])

    (mix,) = mm_nn("mix_out", merged, wout_full, F32, 1)

    def e3(t, b):
        xv, mx = t
        g1v, l1g, l1b, sc2v, sh2v = b
        r1 = ALPHA * xv + g1v * mx
        xh1, _ = _ln_stats(r1)
        x1 = xh1 * l1g + l1b
        xh, _ = _ln_stats(x1)
        return [r1, xh * (1.0 + sc2v) + sh2v], []
    r1, h2 = _rowwise("post_mix", e3, S, ts, [(x2d, D, 0), (mix, D, 0)],
                      [g1, ln1_g, ln1_b, sc2, sh2], [(D, F32), (D, BF16)], [])

    def relu_epi(acc, ex, outs):
        outs[0][...] = jnp.maximum(acc[...], 0.0).astype(BF16)
    (rl,) = mm_nn("ff1", h2, wg_ff1, BF16, 1, epi=relu_epi)

    def square(a):
        return a * a
    (y2,) = mm_nn("ff2", rl, wff2_full, F32, 1, pro=square)

    def e4(t, b):
        r1v, y2v, tg = t
        g2v, l1g, l1b, l2g, l2b = b
        xh1, _ = _ln_stats(r1v)
        x1 = xh1 * l1g + l1b
        r2 = ALPHA * x1 + g2v * y2v
        xh2, rs2 = _ln_stats(r2)
        err = xh2 * l2g + l2b - tg
        dx2 = err * (1.0 / D)
        dr2 = _ln_bwd(dx2 * l2g, xh2, rs2)
        lsum = jnp.sum(_colsum(err * err), axis=1, keepdims=True) * (0.5 / D)
        return ([ALPHA * dr2, g2v * dr2],
                [jnp.broadcast_to(lsum, (1, LANES)), _colsum(dx2 * xh2), _colsum(dx2), _colsum(dr2 * y2v)])
    dx1a, dy2, loss_acc, g_ln2g, g_ln2b, d_g2 = _rowwise(
        "head", e4, S, ts, [(r1, D, 0), (y2, D, 0), (tgt, D, 0)], [g2, ln1_g, ln1_b, ln2_g, ln2_b],
        [(D, F32), (D, BF16)], [LANES, D, D, D])

    tn_ff = _tile(4 * D, 1024)

    def dff_epi(acc, ex, outs):
        outs[0][...] = (acc[...] * (2.0 * ex[0][...].astype(F32))).astype(BF16)
    tmf = _tile(S, 1024)
    (da1,) = mm_nt("d_ff2", dy2, wff2_full, BF16, 1, tn=tn_ff, epi=dff_epi,
                   extras=[(rl, pl.BlockSpec((tmf, tn_ff), lambda i, j, k: (i, j)))])
    gw_ff2 = mm_tn("gw_ff2", rl, dy2, BF16, NDEV, 0, pro=square)
    gw_ff1 = mm_tn("gw_ff1", h2, da1, BF16, NDEV, 1)
    tok, wait_pair_a = pair_exchange("pair_exchange_ff", [gw_ff2, gw_ff1], 4)
    (dh2,) = mm_nt("d_ff1", da1, wg_ff1, F32, 2, after=[tok])

    def e5(t, b):
        dh2v, r1v, dx1av, mx = t
        sc2v, l1g, l1b, g1v = b
        xh1, rs1 = _ln_stats(r1v)
        x1 = xh1 * l1g + l1b
        xh, rs = _ln_stats(x1)
        dx1 = dx1av + _ln_bwd(dh2v * (1.0 + sc2v), xh, rs)
        dr1 = _ln_bwd(dx1 * l1g, xh1, rs1)
        return ([ALPHA * dr1, g1v * dr1],
                [_colsum(dh2v * xh), _colsum(dh2v), _colsum(dx1 * xh1), _colsum(dx1), _colsum(dr1 * mx)])
    dxa, dmix, d_sc2, d_sh2, g_ln1g, g_ln1b, d_g1 = _rowwise(
        "post_mix_bwd", e5, S, ts, [(dh2, D, 0), (r1, D, 0), (dx1a, D, 0), (mix, D, 0)],
        [sc2, ln1_g, ln1_b, g1], [(D, F32), (D, BF16)], [D, D, D, D, D])

    (dmerged,) = mm_nt("d_mix_out", dmix, wout_full, BF16, 1)
    gw_out = mm_tn("gw_out", merged, dmix, BF16, NDEV, 0)
    grads_a, got_a = wait_pair_a(gw_out)
    parts_a = [pair_sum("pair_sum_ff%d" % i, g, t, my_c) for i, (g, t) in enumerate(zip(grads_a, got_a))]
    tok, wait_chip_a = chip_exchange("chip_exchange_ff", parts_a, 5)

    def merge_b(ga, gb, vtv, yb, dm):
        vv, tt = vtv[:, :cb], vtv[:, cb:]
        sa, sb, st = _sigmoid(ga), _sigmoid(gb), _sigmoid(tt)
        dya = dm * sa
        return [dm * (vv * st) * sa * (1.0 - sa), dm * yb * sb * (1.0 - sb),
                jnp.concatenate([dya * st, dya * vv * st * (1.0 - st)], axis=1), dm * sb]
    dga, dgb_, dvt, dy_b = merge_call(
        "merge_bwd", merge_b, merge_ins + [(dmerged, cb, lambda i, j: (i, j))],
        [(D, cb), (D, cb), (2 * D, 2 * cb), (D, cb)], after=[tok])

    (dypool,) = mm_nt("d_pool_out", dy_b, wg_po, F32, NDEV)
    gw_po = mm_tn("gw_pool_out", ypool, dy_b, BF16, NDEV, 4)

    def e7(t, b):
        return [t[0] * b[0]], [_colsum(t[0] * t[1])]
    dyp, g_pscale = _rowwise("pool_scale_bwd", e7, S, ts, [(dypool, W, 0), (yp, W, 0)],
                             [pool_scale], [(W, BF16)], [W])
    (dpooled,) = _mm(
        "d_pool_mix", "nt", dyp, wp_full.astype(BF16), (S // tmp, nwin, 1),
        pl.BlockSpec((tmp, gw), lambda i, j, k: (i, j)), pl.BlockSpec((1, gw, gw), lambda i, j, k: (j, 0, 0)),
        [(_sds((S, W), F32), pl.BlockSpec((tmp, gw), lambda i, j, k: (i, j)))], (tmp, gw), 1, gw)
    tkp = _tile(S, 1024)
    gw_pool = _mm(
        "gw_pool", "tn", pooled, dyp, (nwin, 1, S // tkp),
        pl.BlockSpec((tkp, gw), lambda i, j, k: (k, i)), pl.BlockSpec((tkp, gw), lambda i, j, k: (k, i)),
        [(_sds((nwin, gw, gw), BF16), pl.BlockSpec((1, gw, gw), lambda i, j, k: (i, 0, 0)))],
        (gw, gw), 1, gw, stacked_out=True)[0]
    du_pool = pool_bwd(dpooled, gw)

    (dz,) = mm_nt("d_glu", dvt, wg_vg, BF16, NDEV)
    gw_vg = mm_tn("gw_glu", z, dvt, BF16, 2 * NDEV, 4)
    gw_pool_st = jnp.transpose(gw_pool.reshape(nwin, NDEV, gw // NDEV, gw), (1, 0, 2, 3))
    grads_b = [gw_out, gw_po, gw_pool_st, gw_vg.reshape(NDEV, 2, W, D // NDEV)]
    tok, wait_pair_b = pair_exchange("pair_exchange_mix", grads_b, 6)
    du_ssm, g_bmat, g_cmat, g_d, g_a = s5_bwd(proj, dz, bmat, cmat, ssm_d, kconst, after=[tok])
    grads_b, got_b = wait_pair_b(du_ssm)
    parts_b = [pair_sum("pair_sum_mix%d" % i, g, t, my_c) for i, (g, t) in enumerate(zip(grads_b, got_b))]
    tok, wait_chip_b = chip_exchange("chip_exchange_mix", parts_b, 7)

    dproj = jnp.concatenate([du_ssm, du_pool, dga, dgb_], axis=1)
    gw_in = mm_tn("gw_in", h1, dproj, BF16, NDEV, 1, after=[tok])
    tok, wait_pair_c = pair_exchange("pair_exchange_in", [gw_in], 8)
    (dh1,) = mm_nt("d_proj", dproj, wg_in, F32, 2, after=[tok])
    grads_c, got_c = wait_pair_c(dh1)
    parts_c = [pair_sum("pair_sum_in", grads_c[0], got_c[0], my_c)]
    tok, wait_chip_c = chip_exchange("chip_exchange_in", parts_c, 9)

    def e10(t, b):
        dh1v, xv, dxav = t
        xh, rs = _ln_stats(xv)
        return ([dxav + _ln_bwd(dh1v * (1.0 + b[0]), xh, rs)],
                [_colsum(dh1v * xh), _colsum(dh1v)])
    grad_x, d_sc1, d_sh1 = _rowwise("ln_mod1_bwd", e10, S, ts, [(dh1, D, 0), (x2d, D, 0), (dxa, D, 0)],
                                    [sc1], [(D, F32)], [D, D], after=[tok])

    gb4 = _diag_blocks(g_bmat[:, :, :GPB * P], GPB, H, P), _diag_blocks(g_bmat[:, :, GPB * P:], GPB, H, P)
    dbb = jnp.stack([jnp.transpose(t, (2, 0, 1, 3)).reshape(H, G * P) for t in gb4])
    g_bt_re, g_bt_im, g_f = s5_bbar_bwd(f2r, bt_re, bt_im, dbb)
    g_b_re = jnp.transpose(g_bt_re.reshape(H, G, P), (1, 2, 0))
    g_b_im = jnp.transpose(g_bt_im.reshape(H, G, P), (1, 2, 0))
    gc_top = _diag_blocks(g_cmat[:, :GPB * P, :], GPB, P, H)
    gc_bot = _diag_blocks(g_cmat[:, GPB * P:, :], GPB, P, H)
    g_c_re = jnp.transpose(gc_top, (0, 1, 3, 2)).reshape(G, H, P)
    g_c_im = -jnp.transpose(gc_bot, (0, 1, 3, 2)).reshape(G, H, P)
    d_ab = jnp.transpose(g_a.reshape(nblk, 2, GPB, P), (1, 0, 2, 3)).reshape(2, G, P)
    g_lr, g_li, g_ldt = s5_disc_bwd(lam_re[0], lam_im[0], log_dt[0].reshape(G, 1), d_ab,
                                    g_f.reshape(2, G, P))

    dmod = jnp.concatenate([d_sh1, d_sc1, d_g1, d_sh2, d_sc2, d_g2], axis=1)
    small_names = [b_ada, lam_re, lam_im, log_dt, ssm_b_re, ssm_b_im, ssm_c_re, ssm_c_im, ssm_d,
                   pool_scale, ln1_g, ln1_b, ln2_g, ln2_b]
    small_m = [m_b_ada, m_lam_re, m_lam_im, m_log_dt, m_ssm_b_re, m_ssm_b_im, m_ssm_c_re, m_ssm_c_im,
               m_ssm_d, m_pool_scale, m_ln1_g, m_ln1_b, m_ln2_g, m_ln2_b]
    small_v = [v_b_ada, v_lam_re, v_lam_im, v_log_dt, v_ssm_b_re, v_ssm_b_im, v_ssm_c_re, v_ssm_c_im,
               v_ssm_d, v_pool_scale, v_ln1_g, v_ln1_b, v_ln2_g, v_ln2_b]
    small_g = [dmod, g_lr, g_li, g_ldt, g_b_re, g_b_im, g_c_re, g_c_im, g_d, g_pscale,
               g_ln1g, g_ln1b, g_ln2g, g_ln2b, loss_acc]
    zero_row = jnp.zeros((1, LANES), F32)
    packed_g = _small_pack(small_g)
    (parts_all,) = seq_all_gather("gather_small", [packed_g], 10, routed=False)
    parts_a, got3_a = wait_chip_a(packed_g)
    glu_w = jnp.stack([w_glu_val[0], w_glu_gate[0]])
    glu_m = jnp.stack([m_w_glu_val[0], m_w_glu_gate[0]])
    glu_v = jnp.stack([v_w_glu_val[0], v_w_glu_gate[0]])
    wmv = [(w_ff2[0], m_w_ff2[0], v_w_ff2[0]), (w_ff1[0], m_w_ff1[0], v_w_ff1[0]),
           (w_out[0], m_w_out[0], v_w_out[0]), (w_pool_out[0], m_w_pool_out[0], v_w_pool_out[0]),
           (w_pool[0], m_w_pool[0], v_w_pool[0]), (glu_w, glu_m, glu_v)]
    upd = [adamw_sharded("adamw_%d" % i, p, t, w, m, v, my_chip)
           for i, (p, t, (w, m, v)) in enumerate(zip(parts_a, got3_a, wmv[:2]))]
    parts_b, got3_b = wait_chip_b(upd[-1][0])
    upd += [adamw_sharded("adamw_%d" % (2 + i), p, t, w, m, v, my_chip)
            for i, (p, t, (w, m, v)) in enumerate(zip(parts_b, got3_b, wmv[2:]))]
    u_ff2, u_ff1, u_out, u_po, u_pool, u_glu = upd

    sg, sd, sm, sv = adamw_small(parts_all, _small_pack(small_names + [zero_row]),
                                 _small_pack(small_m + [zero_row]), _small_pack(small_v + [zero_row]),
                                 after=[upd[-1][0]])
    shapes = [t.shape for t in small_names]
    loss = _small_unpack(sg, shapes + [(1, LANES)])[-1][0, 0]
    sg, sd, sm, sv = (_small_unpack(t, shapes) for t in (sg, sd, sm, sv))

    nmod = 6 * D
    dmod_all = parts_all[:, :nmod // LANES, :].reshape(NDEV, nmod)
    c_all_t = jnp.transpose(c_all.reshape(NDEV, D))
    ada_out = adamw_ada(c_all_t, dmod_all, w_ada[0], m_w_ada[0], v_w_ada[0], my_dev)
    parts_c, got3_c = wait_chip_c(ada_out[0])
    u_in = adamw_sharded("adamw_6", parts_c[0], got3_c[0], w_in[0], m_w_in[0], v_w_in[0], my_chip)

    def pick(k):
        return [ada_out[k][None], sg_sd[k][0], u_in[k][None]] + [t for t in sg_sd[k][1:9]] + \
               [u_glu[k][0][None], u_glu[k][1][None], u_pool[k][None], sg_sd[k][9], u_po[k][None],
                u_out[k][None], sg_sd[k][10], sg_sd[k][11], u_ff1[k][None], u_ff2[k][None],
                sg_sd[k][12], sg_sd[k][13]]

    sg_sd = [sg, sd, sm, sv]
    return (loss, grad_x[None], *pick(0), *pick(1), *pick(2), *pick(3))
```

```python
import functools
import math

import jax
import jax.numpy as jnp
from jax import lax
from jax.experimental import pallas as pl
from jax.experimental.pallas import tpu as pltpu
from jax.experimental.pallas import tpu_sc as plsc

F32 = jnp.float32
BF16 = jnp.bfloat16
MESH = pl.DeviceIdType.MESH
NDEV = 8
NCHIP = 4

SSM_GROUP = 16
SSM_STATE = 64
GROUPS_PER_BLOCK = 8
POOL_WINDOWS = (2, 4, 8, 16)
LN_EPS = 1e-5
ALPHA = 2.0 ** 0.25
ADAM_LR, ADAM_B1, ADAM_B2, ADAM_EPS, ADAM_WD, ADAM_STEP = 0.001, 0.9, 0.999, 1e-08, 0.01, 10
SUBLANES = 8
LANES = 128
VMEM_LIMIT = 56 * 1024 * 1024


def _params(sem=None, vmem=VMEM_LIMIT):
    return pltpu.CompilerParams(dimension_semantics=sem, vmem_limit_bytes=vmem)


def _tile(n, pref):
    if n <= pref:
        return n
    t = 1 << (pref.bit_length() - 1)
    while n % t:
        t //= 2
    return t


def _cast_epi(vals, ex, outs):
    c = vals[0].shape[1]
    for s, v in enumerate(vals):
        outs[0][:, s * c:(s + 1) * c] = v.astype(outs[0].dtype)


ANY = pl.BlockSpec(memory_space=pl.ANY)


def _with_after(body, n_in, after):
    if not after:
        return body
    n_af = len(after)

    def wrapped(*refs):
        return body(*refs[:n_in], *refs[n_in + n_af:])
    return wrapped


def _mm(name, kind, a, b, grid, a_spec, b_spec, outs, acc_shape, nsub=1, c=None,
        pro=None, epi=None, extras=(), stacked_out=False, after=()):
    nk = grid[2]
    n_ex, n_out = len(extras), len(outs)

    def finish(vals, ex, out_refs):
        if epi is not None:
            epi(vals, ex, out_refs)
        elif stacked_out:
            for s, v in enumerate(vals):
                out_refs[0][s] = v.astype(out_refs[0].dtype)
        else:
            _cast_epi(vals, ex, out_refs)

    def body(*refs):
        a_ref, b_ref = refs[0], refs[1]
        ex = refs[2:2 + n_ex]
        out_refs = refs[2 + n_ex:2 + n_ex + n_out]
        k = pl.program_id(2)
        av = a_ref[...]
        if pro is not None:
            av = pro(av)
        if kind == "nn":
            prods = [jnp.dot(av, b_ref[s], preferred_element_type=F32) for s in range(nsub)]
        elif kind == "nt":
            t = None
            for s in range(nsub):
                d = lax.dot_general(av[:, s * c:(s + 1) * c], b_ref[s], (((1,), (1,)), ((), ())),
                                    preferred_element_type=F32)
                t = d if t is None else t + d
            prods = [t]
        else:
            t = lax.dot_general(av, b_ref[...], (((0,), (0,)), ((), ())), preferred_element_type=F32)
            prods = [t[:, s * c:(s + 1) * c] for s in range(nsub)] if stacked_out else [t]
        if nk == 1:
            finish(prods, ex, out_refs)
            return
        acc = refs[-1]
        w = prods[0].shape[1]

        @pl.when(k == 0)
        def _():
            for s, p in enumerate(prods):
                acc[:, s * w:(s + 1) * w] = p

        @pl.when(jnp.logical_and(k > 0, k < nk - 1))
        def _():
            for s, p in enumerate(prods):
                acc[:, s * w:(s + 1) * w] += p

        @pl.when(k == nk - 1)
        def _():
            finish([acc[:, s * w:(s + 1) * w] + p for s, p in enumerate(prods)], ex, out_refs)

    res = pl.pallas_call(
        _with_after(body, 2 + n_ex, after), name=name, grid=grid,
        in_specs=[a_spec, b_spec] + [e[1] for e in extras] + [ANY] * len(after),
        out_specs=[o[1] for o in outs],
        out_shape=[o[0] for o in outs],
        scratch_shapes=[pltpu.VMEM(acc_shape, F32)] if nk > 1 else [],
        compiler_params=_params(("parallel", "parallel", "arbitrary")),
    )(a, b, *[e[0] for e in extras], *after)
    return res


def _sds(shape, dtype):
    return jax.ShapeDtypeStruct(shape, dtype)


def mm_nn(name, a, b3, out_dtype, nsub, tm=1024, tk=2048, tn=None, pro=None, epi=None,
          extras=(), extra_outs=(), a_col0=0):
    M = a.shape[0]
    nb, K, cdim = b3.shape
    tm, tk = _tile(M, tm), _tile(K, tk)
    if nb == 1:
        tn = _tile(cdim, tn or 1024)
        nsub, c, nj = 1, tn, cdim // tn
        b_spec = pl.BlockSpec((1, tk, tn), lambda i, j, k: (0, k, j))
        N = cdim
    else:
        c, nj, tn = cdim, nb // nsub, nsub * cdim
        b_spec = pl.BlockSpec((nsub, tk, cdim), lambda i, j, k: (j, k, 0))
        N = nb * cdim
    kb0 = a_col0 // tk
    a_spec = pl.BlockSpec((tm, tk), lambda i, j, k: (i, kb0 + k))
    grid = (M // tm, nj, K // tk)
    o_spec = pl.BlockSpec((tm, tn), lambda i, j, k: (i, j))
    outs = [(_sds((M, N), out_dtype), o_spec)] + [(_sds((M, N), d), o_spec) for d in extra_outs]
    return _mm(name, "nn", a, b3, grid, a_spec, b_spec, outs, (tm, tn), nsub, c, pro, epi, extras)


def mm_nt(name, a, b3, out_dtype, nsub, tm=1024, tn=1024, epi=None, extras=(), extra_outs=(),
          after=()):
    M = a.shape[0]
    nb, N, cdim = b3.shape
    tm, tn = _tile(M, tm), _tile(N, tn)
    if nb == 1:
        tk = _tile(cdim, 2048)
        nsub, c, nk = 1, tk, cdim // tk
        b_spec = pl.BlockSpec((1, tn, tk), lambda i, j, k: (0, j, k))
    else:
        c, nk, tk = cdim, nb // nsub, nsub * cdim
        b_spec = pl.BlockSpec((nsub, tn, cdim), lambda i, j, k: (k, j, 0))
    a_spec = pl.BlockSpec((tm, tk), lambda i, j, k: (i, k))
    grid = (M // tm, N // tn, nk)
    o_spec = pl.BlockSpec((tm, tn), lambda i, j, k: (i, j))
    outs = [(_sds((M, N), out_dtype), o_spec)] + [(_sds((M, N), d), o_spec) for d in extra_outs]
    return _mm(name, "nt", a, b3, grid, a_spec, b_spec, outs, (tm, tn), nsub, c, None, epi, extras,
               after=after)


def mm_tn(name, a, b, out_dtype, nb, nsub, tma=1024, tk=2048, pro=None, a_col0=0, a_cols=None,
          after=()):
    S = a.shape[0]
    Ka = a_cols or a.shape[1]
    N = b.shape[1]
    tk = _tile(S, tk)
    if nsub == 0:
        tma, tn = _tile(Ka, tma), _tile(N, 1024)
        grid = (Ka // tma, N // tn, S // tk)
        ab0 = a_col0 // tma
        res = _mm(name, "tn", a, b, grid, pl.BlockSpec((tk, tma), lambda i, j, k: (k, ab0 + i)),
                  pl.BlockSpec((tk, tn), lambda i, j, k: (k, j)),
                  [(_sds((Ka, N), out_dtype), pl.BlockSpec((tma, tn), lambda i, j, k: (i, j)))],
                  (tma, tn), 1, tn, pro, None, (), after=after)[0]
        return res.reshape(nb, Ka // nb, N)
    else:
        c = N // nb
        tma = _tile(Ka, tma)
        grid = (Ka // tma, nb // nsub, S // tk)
        o_spec = pl.BlockSpec((nsub, tma, c), lambda i, j, k: (j, i, 0))
        out = _sds((nb, Ka, c), out_dtype)
        nsub_k = nsub
        tn = nsub * c
        b_spec = pl.BlockSpec((tk, tn), lambda i, j, k: (k, j))
    ab0 = a_col0 // tma
    a_spec = pl.BlockSpec((tk, tma), lambda i, j, k: (k, ab0 + i))
    return _mm(name, "tn", a, b, grid, a_spec, b_spec, [(out, o_spec)], (tma, tn), nsub_k, c,
               pro, None, (), stacked_out=True, after=after)[0]


def _rowwise(name, fn, S, ts, tiled, bcast, tiled_out, acc_out, after=()):
    nt, nb, no, na = len(tiled), len(bcast), len(tiled_out), len(acc_out)

    def body(*refs):
        tin = [r[...] for r in refs[:nt]]
        bin_ = [r[...] for r in refs[nt:nt + nb]]
        o_refs = refs[nt + nb:nt + nb + no]
        a_refs = refs[nt + nb + no:]
        touts, aouts = fn(tin, bin_)
        for r, v in zip(o_refs, touts):
            r[...] = v.astype(r.dtype)
        i = pl.program_id(0)

        @pl.when(i == 0)
        def _():
            for r, v in zip(a_refs, aouts):
                r[...] = v

        @pl.when(i > 0)
        def _():
            for r, v in zip(a_refs, aouts):
                r[...] += v

    in_specs = [pl.BlockSpec((ts, w), functools.partial(lambda i, cb: (i, cb), cb=cb))
                for (_, w, cb) in tiled]
    in_specs += [pl.BlockSpec(b.shape, lambda i: (0, 0)) for b in bcast]
    out_specs = [pl.BlockSpec((ts, w), lambda i: (i, 0)) for (w, _) in tiled_out]
    out_specs += [pl.BlockSpec((1, w), lambda i: (0, 0)) for w in acc_out]
    out_shape = [_sds((S, w), d) for (w, d) in tiled_out] + [_sds((1, w), F32) for w in acc_out]
    return pl.pallas_call(
        _with_after(body, nt + nb, after), name=name, grid=(S // ts,),
        in_specs=in_specs + [ANY] * len(after), out_specs=out_specs,
        out_shape=out_shape, compiler_params=_params(("arbitrary",)),
    )(*[t[0] for t in tiled], *bcast, *after)


def _ln_stats(v):
    mu = jnp.mean(v, axis=-1, keepdims=True)
    vc = v - mu
    var = jnp.mean(vc * vc, axis=-1, keepdims=True)
    rstd = lax.rsqrt(var + LN_EPS)
    return vc * rstd, rstd


def _ln_bwd(dxhat, xhat, rstd):
    return rstd * (dxhat - jnp.mean(dxhat, axis=-1, keepdims=True)
                   - xhat * jnp.mean(dxhat * xhat, axis=-1, keepdims=True))


def _colsum(v):
    return jnp.sum(v, axis=0, keepdims=True)


def _sigmoid(v):
    return 1.0 / (1.0 + jnp.exp(-v))


_GELU_C = math.sqrt(2.0 / math.pi)


def _gelu(v):
    return 0.5 * v * (1.0 + jnp.tanh(_GELU_C * (v + 0.044715 * v * v * v)))


def _gelu_grad(v):
    t = jnp.tanh(_GELU_C * (v + 0.044715 * v * v * v))
    return 0.5 * (1.0 + t) + 0.5 * v * (1.0 - t * t) * _GELU_C * (1.0 + 3 * 0.044715 * v * v)


def _disc(lr, li, ldt):
    dt = jnp.exp(ldt)
    mag = jnp.exp(lr * dt)
    ang = li * dt
    ab_re = mag * jnp.cos(ang)
    ab_im = mag * jnp.sin(ang)
    num_re = ab_re - 1.0
    num_im = ab_im
    den = lr * lr + li * li
    f_re = (num_re * lr + num_im * li) / den
    f_im = (num_im * lr - num_re * li) / den
    return ab_re, ab_im, f_re, f_im


def _cmul(ar, ai, br, bi):
    return ar * br - ai * bi, ar * bi + ai * br


def s5_disc(lam_re, lam_im, log_dt):
    G, P = lam_re.shape

    def body(lr_ref, li_ref, ldt_ref, f_ref, k_ref):
        ab_re, ab_im, f_re, f_im = _disc(lr_ref[...], li_ref[...], ldt_ref[...])
        f_ref[0] = f_re
        f_ref[1] = f_im
        pr, pi = [ab_re], [ab_im]
        for _ in range(SUBLANES - 1):
            nr, ni = _cmul(pr[-1], pi[-1], ab_re, ab_im)
            pr.append(nr)
            pi.append(ni)
        zero = jnp.zeros_like(ab_re)
        for n, sh in enumerate((1, 2, 4)):
            for r in range(SUBLANES):
                k_ref[2 * n, r] = pr[sh - 1] if r >= sh else zero
                k_ref[2 * n + 1, r] = pi[sh - 1] if r >= sh else zero
                k_ref[8 + 2 * n, r] = pr[sh - 1] if r + sh < SUBLANES else zero
                k_ref[8 + 2 * n + 1, r] = -pi[sh - 1] if r + sh < SUBLANES else zero
        for r in range(SUBLANES):
            k_ref[6, r] = pr[r]
            k_ref[7, r] = pi[r]
            k_ref[14, r] = pr[SUBLANES - 1 - r]
            k_ref[15, r] = -pi[SUBLANES - 1 - r]

    vm = pl.BlockSpec(memory_space=pltpu.VMEM)
    return pl.pallas_call(
        body, name="s5_disc", in_specs=[vm, vm, vm], out_specs=[vm, vm],
        out_shape=[_sds((2, G, P), F32), _sds((16, SUBLANES, G, P), F32)],
    )(lam_re, lam_im, log_dt)


def s5_disc_bwd(lam_re, lam_im, log_dt, d_ab, d_f):
    G, P = lam_re.shape

    def body(lr_ref, li_ref, ldt_ref, dab_ref, df_ref, glr_ref, gli_ref, gdt_ref):
        _, vjp = jax.vjp(_disc, lr_ref[...], li_ref[...], ldt_ref[...])
        glr, gli, gdt = vjp((dab_ref[0], dab_ref[1], df_ref[0], df_ref[1]))
        glr_ref[...] = glr
        gli_ref[...] = gli
        gdt_ref[...] = gdt

    vm = pl.BlockSpec(memory_space=pltpu.VMEM)
    return pl.pallas_call(
        body, name="s5_disc_bwd", in_specs=[vm] * 5, out_specs=[vm] * 3,
        out_shape=[_sds((G, P), F32), _sds((G, P), F32), _sds((G, 1), F32)],
    )(lam_re, lam_im, log_dt, d_ab, d_f)


def s5_bbar(f2, bt_re, bt_im):
    def body(f_ref, br_ref, bi_ref, o_ref):
        fr, fi = f_ref[0], f_ref[1]
        br, bi = br_ref[...], bi_ref[...]
        o_ref[0] = fr * br - fi * bi
        o_ref[1] = fr * bi + fi * br

    vm = pl.BlockSpec(memory_space=pltpu.VMEM)
    return pl.pallas_call(body, name="s5_bbar", in_specs=[vm] * 3, out_specs=vm,
                          out_shape=_sds((2,) + bt_re.shape, F32))(f2, bt_re, bt_im)


def s5_bbar_bwd(f2, bt_re, bt_im, dbb):
    def body(f_ref, br_ref, bi_ref, d_ref, gbr_ref, gbi_ref, gf_ref):
        fr, fi = f_ref[0], f_ref[1]
        br, bi = br_ref[...], bi_ref[...]
        dr, di = d_ref[0], d_ref[1]
        gbr_ref[...] = fr * dr + fi * di
        gbi_ref[...] = fr * di - fi * dr
        gf_ref[0] = _colsum(dr * br + di * bi)
        gf_ref[1] = _colsum(di * br - dr * bi)

    vm = pl.BlockSpec(memory_space=pltpu.VMEM)
    return pl.pallas_call(
        body, name="s5_bbar_bwd", in_specs=[vm] * 4, out_specs=[vm] * 3,
        out_shape=[_sds(bt_re.shape, F32), _sds(bt_re.shape, F32), _sds(f2.shape, F32)],
    )(f2, bt_re, bt_im, dbb)


def _scan_fwd(xs, k_ref, nst):
    ntile = xs.shape[0] // SUBLANES

    def step(t, carry):
        cr, ci = carry
        r0 = pl.multiple_of(t * SUBLANES, SUBLANES)
        xr = xs[pl.ds(r0, SUBLANES), 0:nst]
        xi = xs[pl.ds(r0, SUBLANES), nst:2 * nst]
        for n, sh in enumerate((1, 2, 4)):
            sr = pltpu.roll(xr, sh, 0)
            si = pltpu.roll(xi, sh, 0)
            mr, mi = k_ref[2 * n], k_ref[2 * n + 1]
            xr, xi = xr + mr * sr - mi * si, xi + mr * si + mi * sr
        pr, pi = k_ref[6], k_ref[7]
        xr, xi = xr + pr * cr - pi * ci, xi + pr * ci + pi * cr
        xs[pl.ds(r0, SUBLANES), 0:nst] = xr
        xs[pl.ds(r0, SUBLANES), nst:2 * nst] = xi
        return (jnp.broadcast_to(xr[SUBLANES - 1:SUBLANES, :], xr.shape),
                jnp.broadcast_to(xi[SUBLANES - 1:SUBLANES, :], xi.shape))

    zero = jnp.zeros((SUBLANES, nst), F32)
    lax.fori_loop(0, ntile, step, (zero, zero))


def _scan_bwd(g, xs, k_ref, nst):
    ntile = g.shape[0] // SUBLANES
    row = lax.broadcasted_iota(jnp.int32, (SUBLANES, nst), 0)

    def step(tt, carry):
        cr, ci, ar, ai = carry
        t = ntile - 1 - tt
        r0 = pl.multiple_of(t * SUBLANES, SUBLANES)
        gr = g[pl.ds(r0, SUBLANES), 0:nst]
        gi = g[pl.ds(r0, SUBLANES), nst:2 * nst]
        for n, sh in enumerate((1, 2, 4)):
            sr = pltpu.roll(gr, SUBLANES - sh, 0)
            si = pltpu.roll(gi, SUBLANES - sh, 0)
            mr, mi = k_ref[8 + 2 * n], k_ref[8 + 2 * n + 1]
            gr, gi = gr + mr * sr - mi * si, gi + mr * si + mi * sr
        qr, qi = k_ref[14], k_ref[15]
        gr, gi = gr + qr * cr - qi * ci, gi + qr * ci + qi * cr
        g[pl.ds(r0, SUBLANES), 0:nst] = gr
        g[pl.ds(r0, SUBLANES), nst:2 * nst] = gi
        p0 = pl.multiple_of(jnp.maximum(t - 1, 0) * SUBLANES, SUBLANES)
        live = (t > 0).astype(F32)
        xr = xs[pl.ds(r0, SUBLANES), 0:nst]
        xi = xs[pl.ds(r0, SUBLANES), nst:2 * nst]
        pr = xs[pl.ds(p0, SUBLANES), 0:nst][SUBLANES - 1:SUBLANES, :] * live
        pi = xs[pl.ds(p0, SUBLANES), nst:2 * nst][SUBLANES - 1:SUBLANES, :] * live
        xmr = jnp.where(row == 0, jnp.broadcast_to(pr, xr.shape), pltpu.roll(xr, 1, 0))
        xmi = jnp.where(row == 0, jnp.broadcast_to(pi, xi.shape), pltpu.roll(xi, 1, 0))
        ar = ar + gr * xmr + gi * xmi
        ai = ai + gi * xmr - gr * xmi
        return (jnp.broadcast_to(gr[0:1, :], gr.shape), jnp.broadcast_to(gi[0:1, :], gi.shape),
                ar, ai)

    zero = jnp.zeros((SUBLANES, nst), F32)
    _, _, ar, ai = lax.fori_loop(0, ntile, step, (zero, zero, zero, zero))
    return _colsum(ar), _colsum(ai)


def s5_fwd(proj, bmat, cmat, dskip, kconst):
    S = proj.shape[0]
    nb, cw, nst2 = bmat.shape
    nst = nst2 // 2

    def body(u_ref, b_ref, c_ref, d_ref, k_ref, z_ref, xsb_ref, xs):
        u = u_ref[...]
        xs[...] = jnp.dot(u.astype(BF16), b_ref[0], preferred_element_type=F32)
        _scan_fwd(xs, k_ref, nst)
        xsb = xs[...].astype(BF16)
        xsb_ref[...] = xsb
        y = jnp.dot(xsb, c_ref[0], preferred_element_type=F32) + d_ref[...] * u
        z_ref[...] = _gelu(y).astype(BF16)

    return pl.pallas_call(
        body, name="s5_fwd", grid=(nb,),
        in_specs=[pl.BlockSpec((S, cw), lambda b: (0, b)),
                  pl.BlockSpec((1, cw, nst2), lambda b: (b, 0, 0)),
                  pl.BlockSpec((1, nst2, cw), lambda b: (b, 0, 0)),
                  pl.BlockSpec((1, cw), lambda b: (0, b)),
                  pl.BlockSpec((16, SUBLANES, nst), lambda b: (0, 0, b))],
        out_specs=[pl.BlockSpec((S, cw), lambda b: (0, b)), pl.BlockSpec((S, nst2), lambda b: (0, b))],
        out_shape=[_sds((S, nb * cw), BF16), _sds((S, nb * nst2), BF16)],
        scratch_shapes=[pltpu.VMEM((S, nst2), F32)],
        compiler_params=_params(("arbitrary",)),
    )(proj, bmat, cmat, dskip, kconst)


def s5_bwd(proj, xsb_all, dz, bmat, cmat, dskip, kconst, after=()):
    S = proj.shape[0]
    nb, cw, nst2 = bmat.shape
    nst = nst2 // 2

    def body(u_ref, xsb_ref, dz_ref, b_ref, c_ref, d_ref, k_ref, du_ref, gb_ref, gc_ref, gd_ref,
             ga_ref, xs, g):
        u = u_ref[...]
        ub = u.astype(BF16)
        bm, cm, d = b_ref[0], c_ref[0], d_ref[...]
        xsb = xsb_ref[...]
        xs[...] = xsb.astype(F32)
        y = jnp.dot(xsb, cm, preferred_element_type=F32) + d * u
        dy = dz_ref[...].astype(F32) * _gelu_grad(y)
        gd_ref[...] = _colsum(dy * u)
        dyb = dy.astype(BF16)
        gc_ref[0] = lax.dot_general(xsb, dyb, (((0,), (0,)), ((), ())), preferred_element_type=F32)
        g[...] = lax.dot_general(dyb, cm, (((1,), (1,)), ((), ())), preferred_element_type=F32)
        ar, ai = _scan_bwd(g, xs, k_ref, nst)
        ga_ref[0, 0:1, :] = ar
        ga_ref[0, 1:2, :] = ai
        gb = g[...].astype(BF16)
        du = lax.dot_general(gb, bm, (((1,), (1,)), ((), ())), preferred_element_type=F32) + d * dy
        du_ref[...] = du.astype(BF16)
        gb_ref[0] = lax.dot_general(ub, gb, (((0,), (0,)), ((), ())), preferred_element_type=F32)

    return pl.pallas_call(
        _with_after(body, 7, after), name="s5_bwd", grid=(nb,),
        in_specs=[pl.BlockSpec((S, cw), lambda b: (0, b)),
                  pl.BlockSpec((S, nst2), lambda b: (0, b)),
                  pl.BlockSpec((S, cw), lambda b: (0, b)),
                  pl.BlockSpec((1, cw, nst2), lambda b: (b, 0, 0)),
                  pl.BlockSpec((1, nst2, cw), lambda b: (b, 0, 0)),
                  pl.BlockSpec((1, cw), lambda b: (0, b)),
                  pl.BlockSpec((16, SUBLANES, nst), lambda b: (0, 0, b))] + [ANY] * len(after),
        out_specs=[pl.BlockSpec((S, cw), lambda b: (0, b)),
                   pl.BlockSpec((1, cw, nst2), lambda b: (b, 0, 0)),
                   pl.BlockSpec((1, nst2, cw), lambda b: (b, 0, 0)),
                   pl.BlockSpec((1, cw), lambda b: (0, b)),
                   pl.BlockSpec((1, 2, nst), lambda b: (b, 0, 0))],
        out_shape=[_sds((S, nb * cw), BF16), _sds((nb, cw, nst2), F32), _sds((nb, nst2, cw), F32),
                   _sds((1, nb * cw), F32), _sds((nb, 2, nst), F32)],
        scratch_shapes=[pltpu.VMEM((S, nst2), F32), pltpu.VMEM((S, nst2), F32)],
        compiler_params=_params(("arbitrary",)),
    )(proj, xsb_all, dz, bmat, cmat, dskip, kconst, *after)


def _shift_rows(v, k, row, down):
    n = v.shape[0]
    if down:
        return jnp.where(row >= k, pltpu.roll(v, k, 0), 0.0)
    return jnp.where(row < n - k, pltpu.roll(v, n - k, 0), 0.0)


def _window(v, gi, row, down):
    sums = []
    s = v
    for k in (1, 2, 4, 8):
        s = s + _shift_rows(s, k, row, down)
        sums.append(s)
    out = sums[3]
    for n in (2, 1, 0):
        out = jnp.where(gi == n, sums[n], out)
    return out


def pool_fwd(proj, col0, width, gw):
    S = proj.shape[0]
    cb0 = col0 // gw

    def body(u_ref, o_ref):
        gi = pl.program_id(0)
        u = u_ref[...]
        row = lax.broadcasted_iota(jnp.int32, u.shape, 0)
        w = jnp.left_shift(2, gi)
        count = jnp.minimum(row + 1, w).astype(F32)
        o_ref[...] = (_window(u, gi, row, True) / count - u).astype(BF16)

    return pl.pallas_call(
        body, name="pool_fwd", grid=(len(POOL_WINDOWS),),
        in_specs=[pl.BlockSpec((S, gw), lambda g: (0, cb0 + g))],
        out_specs=pl.BlockSpec((S, gw), lambda g: (0, g)),
        out_shape=_sds((S, width), BF16), compiler_params=_params(("arbitrary",)),
    )(proj)


def pool_bwd(dpooled, gw):
    S, width = dpooled.shape

    def body(d_ref, o_ref):
        gi = pl.program_id(0)
        d = d_ref[...]
        row = lax.broadcasted_iota(jnp.int32, d.shape, 0)
        w = jnp.left_shift(2, gi)
        count = jnp.minimum(row + 1, w).astype(F32)
        o_ref[...] = (_window(d / count, gi, row, False) - d).astype(BF16)

    return pl.pallas_call(
        body, name="pool_bwd", grid=(len(POOL_WINDOWS),),
        in_specs=[pl.BlockSpec((S, gw), lambda g: (0, g))],
        out_specs=pl.BlockSpec((S, gw), lambda g: (0, g)),
        out_shape=_sds((S, width), BF16), compiler_params=_params(("arbitrary",)),
    )(dpooled)


def _place():
    x, y, c = lax.axis_index("x"), lax.axis_index("y"), lax.axis_index("c")
    chips = [(1 - x, y), (x, 1 - y), (1 - x, 1 - y)]
    return x, y, c, chips


HBM = pl.BlockSpec(memory_space=pltpu.HBM)


def _gather_body(n, handshake):
    def body(*refs):
        ins, outs = refs[:n], refs[n:2 * n]
        send_sems, recv_sems, local_sems = refs[2 * n:]
        x, y, c, chips = _place()
        if handshake:
            barrier = pltpu.get_barrier_semaphore()
            for peer in [(x, y, 1 - c)] + [(*chip, c) for chip in chips]:
                pl.semaphore_signal(barrier, inc=1, device_id=peer, device_id_type=MESH)
            pl.semaphore_wait(barrier, 4)
        me, sibling = (x, y, c), (x, y, 1 - c)

        def slot(i, p):
            return outs[i].at[4 * p[0] + 2 * p[1] + p[2]]

        def copy(i, k, block, to, src=None):
            return pltpu.make_async_remote_copy(
                src_ref=slot(i, block) if src is None else src, dst_ref=slot(i, block),
                send_sem=send_sems.at[i, k], recv_sem=recv_sems.at[i, k],
                device_id=to, device_id_type=MESH)

        started = []
        for i in range(n):
            for j, chip in enumerate(chips):
                started.append(copy(i, 1 + j, me, (*chip, c), src=ins[i]))
                started[-1].start()
        for i in range(n):
            started.append(copy(i, 0, me, sibling, src=ins[i]))
            started[-1].start()
        mine = [pltpu.make_async_copy(ins[i], slot(i, me), local_sems.at[i]) for i in range(n)]
        for cp in mine:
            cp.start()
        for i in range(n):
            for j, chip in enumerate(chips):
                copy(i, 1 + j, (*chip, c), me).wait_recv()
                started.append(copy(i, 4 + j, (*chip, c), sibling))
                started[-1].start()
        for i in range(n):
            copy(i, 0, sibling, me).wait_recv()
            for j, chip in enumerate(chips):
                copy(i, 4 + j, (*chip, 1 - c), me).wait_recv()
        for cp in started:
            cp.wait_send()
        for cp in mine:
            cp.wait()

    return body


def _routed_gather_body(n):
    def body(*refs):
        ins, outs = refs[:n], refs[n:2 * n]
        send_sems, recv_sems, local_sems = refs[2 * n:]
        x, y, c, (xn, yn, dg) = _place()
        me, sibling = (x, y, c), (x, y, 1 - c)
        barrier = pltpu.get_barrier_semaphore()
        for peer in (sibling, (*xn, c), (*yn, c)):
            pl.semaphore_signal(barrier, inc=1, device_id=peer, device_id_type=MESH)
        pl.semaphore_wait(barrier, 3)

        def piece(i, p, h):
            rows = ins[i].shape[0] // 2
            return outs[i].at[4 * p[0] + 2 * p[1] + p[2], pl.ds(h * rows, rows)]

        def copy(i, k, src, dst, to):
            return pltpu.make_async_remote_copy(src_ref=src, dst_ref=dst, send_sem=send_sems.at[i, k],
                                                recv_sem=recv_sems.at[i, k], device_id=to,
                                                device_id_type=MESH)

        started = []

        def go(cp):
            cp.start()
            started.append(cp)

        for i in range(n):
            rows = ins[i].shape[0] // 2
            for h in range(2):
                own = ins[i].at[pl.ds(h * rows, rows)]
                go(copy(i, 1 + h, own, piece(i, me, h), (*xn, c)))
                go(copy(i, 3 + h, own, piece(i, me, h), (*yn, c)))
        for i in range(n):
            go(copy(i, 0, ins[i], outs[i].at[4 * x + 2 * y + c], sibling))
        mine = [pltpu.make_async_copy(ins[i], outs[i].at[4 * x + 2 * y + c], local_sems.at[i])
                for i in range(n)]
        for cp in mine:
            cp.start()
        for i in range(n):
            for k, chip, h, onward, ksib in ((1, xn, 0, (5, yn), 7), (4, yn, 1, (6, xn), 10),
                                            (2, xn, 1, None, 8), (3, yn, 0, None, 9),
                                            (5, dg, 0, None, 11), (6, dg, 1, None, 12)):
                got = piece(i, (*chip, c), h)
                copy(i, k, got, got, me).wait_recv()
                if onward is not None:
                    go(copy(i, onward[0], got, got, (*onward[1], c)))
                go(copy(i, ksib, got, got, sibling))
        for i in range(n):
            block = outs[i].at[4 * x + 2 * y + 1 - c]
            copy(i, 0, block, block, me).wait_recv()
            for ksib, chip, h in ((7, xn, 0), (10, yn, 1), (8, xn, 1), (9, yn, 0), (11, dg, 0), (12, dg, 1)):
                got = piece(i, (*chip, 1 - c), h)
                copy(i, ksib, got, got, me).wait_recv()
        for cp in started:
            cp.wait_send()
        for cp in mine:
            cp.wait()

    return body


def _on_sequencer(name, body, arrays, out_sds, sems, collective_id):
    ins = [jax.new_ref(a, memory_space=pltpu.MemorySpace.HBM) for a in arrays]
    outs = [jax.empty_ref(s, memory_space=pltpu.MemorySpace.HBM) for s in out_sds]

    @pl.kernel(mesh=plsc.ScalarSubcoreMesh(axis_name="sequencer", num_cores=1), name=name,
               scratch_types=tuple(sems),
               compiler_params=pltpu.CompilerParams(collective_id=collective_id))
    def launch(*sem_refs):
        body(*ins, *outs, *sem_refs)

    launch()
    return [o[...] for o in outs]


def seq_all_gather(name, shards, collective_id, routed=True):
    n = len(shards)
    nsem = 13 if routed else 7
    return _on_sequencer(
        name, _routed_gather_body(n) if routed else _gather_body(n, True), shards,
        [_sds((NDEV,) + s.shape, s.dtype) for s in shards],
        [pltpu.SemaphoreType.DMA((n, nsem)), pltpu.SemaphoreType.DMA((n, nsem)),
         pltpu.SemaphoreType.DMA((n,))], collective_id)


def pair_exchange(name, grads, collective_id):
    def plan(srcs, lands):
        x, y, c, _ = _place()
        return ([(i, q, srcs[i].at[2 * q + 1 - c], lands[i].at[q], (x, y, 1 - c))
                 for i in range(len(srcs)) for q in range(NCHIP)], [(x, y, 1 - c)])

    return _split_exchange(name, grads, [_sds((NCHIP,) + g.shape[1:], g.dtype) for g in grads],
                           plan, NCHIP, collective_id)


SEM = pl.BlockSpec(memory_space=pltpu.SEMAPHORE)


def _split_exchange(name, srcs, land_sds, plan, ncopy, collective_id):
    n = len(srcs)
    nsem = n * ncopy
    effect = pltpu.SideEffectType.DATAFLOW_SIDE_EFFECTING

    def descriptors(src_refs, land_refs, send_sems, recv_sems):
        copies, peers = plan(src_refs, land_refs)
        return [pltpu.make_async_remote_copy(src_ref=s, dst_ref=d, send_sem=send_sems[i * ncopy + k],
                                             recv_sem=recv_sems[i * ncopy + k], device_id=to,
                                             device_id_type=MESH) for (i, k, s, d, to) in copies], peers

    def start_body(*refs):
        src_refs, land_refs = refs[:n], refs[n:2 * n]
        send_sems, recv_sems = refs[2 * n:2 * n + nsem], refs[2 * n + nsem:2 * n + 2 * nsem]
        token = refs[-1]
        cps, peers = descriptors(src_refs, land_refs, send_sems, recv_sems)
        barrier = pltpu.get_barrier_semaphore()
        for peer in peers:
            pl.semaphore_signal(barrier, inc=1, device_id=peer, device_id_type=MESH)
        pl.semaphore_wait(barrier, len(peers))
        for cp in cps:
            cp.start()
        token[...] = jnp.zeros_like(token)

    lands = [pltpu.with_memory_space_constraint(lax.empty(s.shape, s.dtype), pltpu.HBM) for s in land_sds]
    srcs = [pltpu.with_memory_space_constraint(s, pltpu.HBM) for s in srcs]
    res = pl.pallas_call(
        start_body, name=name + "_start",
        out_shape=(pltpu.SemaphoreType.DMA(()),) * (2 * nsem)
        + tuple(pltpu.HBM(s.shape, s.dtype) for s in srcs)
        + tuple(pltpu.HBM(s.shape, s.dtype) for s in land_sds) + (_sds((SUBLANES, LANES), F32),),
        in_specs=[HBM] * (2 * n),
        out_specs=(SEM,) * (2 * nsem) + (HBM,) * (2 * n) + (pl.BlockSpec(memory_space=pltpu.VMEM),),
        input_output_aliases={i: 2 * nsem + i for i in range(2 * n)},
        compiler_params=pltpu.CompilerParams(has_side_effects=effect, collective_id=collective_id),
    )(*srcs, *lands)
    sems = res[:2 * nsem]
    thru = res[2 * nsem:2 * nsem + 2 * n]
    token = res[-1]

    def wait(after):
        def wait_body(*refs):
            src_refs, land_refs = refs[:n], refs[n:2 * n]
            cps, _ = descriptors(src_refs, land_refs, refs[2 * n:2 * n + nsem],
                                 refs[2 * n + nsem:2 * n + 2 * nsem])
            for cp in cps:
                cp.wait_send()
            for cp in cps:
                cp.wait_recv()

        out = pl.pallas_call(
            wait_body, name=name + "_wait",
            out_shape=tuple(pltpu.HBM(s.shape, s.dtype) for s in srcs)
            + tuple(pltpu.HBM(s.shape, s.dtype) for s in land_sds),
            in_specs=[HBM] * (2 * n) + [SEM] * (2 * nsem) + [pl.BlockSpec(memory_space=pl.ANY)],
            out_specs=(HBM,) * (2 * n),
            input_output_aliases={i: i for i in range(2 * n)},
            compiler_params=pltpu.CompilerParams(has_side_effects=effect),
        )(*thru, *sems, after)
        return list(out[:n]), list(out[n:])

    return token, wait


def pair_sum(name, grad, got, my_c):
    shp = grad.shape[1:]
    r, cdim = shp[-2], shp[-1]
    lead = int(math.prod(shp[:-2])) if len(shp) > 2 else 1
    g5 = grad.reshape(NCHIP, 2, lead * r, cdim)
    t4 = got.reshape(NCHIP, lead * r, cdim)
    R = lead * r
    tr = _tile(R, max(8, (1 << 20) // cdim))

    def body(c_ref, g_ref, t_ref, o_ref):
        o_ref[...] = (g_ref[0].astype(F32) + t_ref[...].astype(F32)).astype(o_ref.dtype)

    out = pl.pallas_call(
        body, name=name,
        grid_spec=pltpu.PrefetchScalarGridSpec(
            num_scalar_prefetch=1, grid=(NCHIP, R // tr),
            in_specs=[pl.BlockSpec((1, 1, tr, cdim), lambda q, i, cr: (q, cr[0], i, 0)),
                      pl.BlockSpec((1, tr, cdim), lambda q, i, cr: (q, i, 0))],
            out_specs=pl.BlockSpec((1, tr, cdim), lambda q, i, cr: (q, i, 0))),
        out_shape=_sds((NCHIP, R, cdim), grad.dtype),
        compiler_params=_params(("parallel", "parallel")),
    )(my_c, g5, t4)
    return out


def chip_exchange(name, parts, collective_id):
    def plan(srcs, lands):
        x, y, c, chips = _place()
        return ([(i, j, srcs[i].at[2 * chip[0] + chip[1]], lands[i].at[j], (*chip, c))
                 for i in range(len(srcs)) for j, chip in enumerate(chips)],
                [(*chip, c) for chip in chips])

    return _split_exchange(name, parts, [_sds((3,) + p.shape[1:], p.dtype) for p in parts],
                           plan, 3, collective_id)


def ada_fwd(c_row, w_ada, b_ada):
    D, cols = w_ada.shape

    def body(c_ref, w_ref, b_ref, mod_ref, call_ref, act8, part, s1, r1, s2, r2):
        x, y, c, _ = _place()
        me = 4 * x + 2 * y + c
        call_ref[me] = c_ref[...]
        cps = []
        for k in range(1, NDEV):
            to = (x ^ (k >> 2), y ^ ((k >> 1) & 1), c ^ (k & 1))
            cps.append(pltpu.make_async_remote_copy(
                src_ref=c_ref, dst_ref=call_ref.at[me], send_sem=s1.at[k - 1],
                recv_sem=r1.at[k - 1], device_id=to, device_id_type=MESH))
            cps[-1].start()
        for cp in cps:
            cp.wait()
        for b in range(NDEV):
            act8[b:b + 1, :] = call_ref[b]
        cv = act8[...]
        act = (cv * _sigmoid(cv)).astype(BF16)
        res = jnp.dot(act, w_ref[...].astype(BF16), preferred_element_type=F32)
        for b in range(NDEV):
            part[b] = res[b:b + 1, :]
        mod_ref[me] = part[me]
        cps = []
        for k in range(1, NDEV):
            to = (x ^ (k >> 2), y ^ ((k >> 1) & 1), c ^ (k & 1))
            dst = 4 * to[0] + 2 * to[1] + to[2]
            cps.append(pltpu.make_async_remote_copy(
                src_ref=part.at[dst], dst_ref=mod_ref.at[me], send_sem=s2.at[k - 1],
                recv_sem=r2.at[k - 1], device_id=to, device_id_type=MESH))
            cps[-1].start()
        for cp in cps:
            cp.wait()
        for b in range(NDEV):
            mod_ref[b] = mod_ref[b] + b_ref[b]

    vm = pl.BlockSpec(memory_space=pltpu.VMEM)
    return pl.pallas_call(
        body, name="ada_fwd", in_specs=[vm, vm, vm], out_specs=[vm, vm],
        out_shape=[_sds((NDEV, 1, cols), F32), _sds((NDEV, 1, D), F32)],
        scratch_shapes=[pltpu.VMEM((NDEV, D), F32), pltpu.VMEM((NDEV, 1, cols), F32),
                        pltpu.SemaphoreType.DMA((NDEV - 1,)), pltpu.SemaphoreType.DMA((NDEV - 1,)),
                        pltpu.SemaphoreType.DMA((NDEV - 1,)), pltpu.SemaphoreType.DMA((NDEV - 1,))],
        compiler_params=pltpu.CompilerParams(vmem_limit_bytes=VMEM_LIMIT),
    )(c_row, w_ada, b_ada.reshape(NDEV, 1, cols))


def _adamw_math(g, w, m, v):
    m2 = ADAM_B1 * m + (1.0 - ADAM_B1) * g
    v2 = ADAM_B2 * v + (1.0 - ADAM_B2) * (g * g)
    m_hat = m2 / (1.0 - ADAM_B1 ** ADAM_STEP)
    v_hat = v2 / (1.0 - ADAM_B2 ** ADAM_STEP)
    delta = -ADAM_LR * (m_hat / (jnp.sqrt(v_hat) + ADAM_EPS) + ADAM_WD * w)
    return delta, m2, v2


def adamw_sharded(name, part4, got3, w, m, v, my_chip):
    shape = w.shape
    cdim = shape[-1]
    R = int(math.prod(shape[:-1]))
    w2, m2, v2 = (t.reshape(R, cdim) for t in (w, m, v))
    tr = _tile(R, max(8, (1 << 19) // cdim))

    def body(q_ref, p_ref, t_ref, w_ref, m_ref, v_ref, g_out, d_out, m_out, v_out):
        g = p_ref[0].astype(F32)
        for j in range(3):
            g = g + t_ref[j].astype(F32)
        d, mn, vn = _adamw_math(g, w_ref[...], m_ref[...], v_ref[...])
        g_out[...] = g
        d_out[...] = d
        m_out[...] = mn
        v_out[...] = vn

    spec = pl.BlockSpec((tr, cdim), lambda i, qr: (i, 0))
    outs = pl.pallas_call(
        body, name=name,
        grid_spec=pltpu.PrefetchScalarGridSpec(
            num_scalar_prefetch=1, grid=(R // tr,),
            in_specs=[pl.BlockSpec((1, tr, cdim), lambda i, qr: (qr[0], i, 0)),
                      pl.BlockSpec((3, tr, cdim), lambda i, qr: (0, i, 0)), spec, spec, spec],
            out_specs=[spec] * 4),
        out_shape=[_sds((R, cdim), F32)] * 4,
        compiler_params=_params(("parallel",)),
    )(my_chip, part4.reshape(NCHIP, R, cdim), got3.reshape(3, R, cdim), w2, m2, v2)
    return [o.reshape(shape) for o in outs]


def adamw_small(parts, w, m, v, after=()):
    R = w.shape[0]
    tr = R

    def body(p_ref, w_ref, m_ref, v_ref, g_out, d_out, m_out, v_out):
        g = p_ref[0]
        for j in range(1, NDEV):
            g = g + p_ref[j]
        d, mn, vn = _adamw_math(g, w_ref[...], m_ref[...], v_ref[...])
        g_out[...] = g
        d_out[...] = d
        m_out[...] = mn
        v_out[...] = vn

    spec = pl.BlockSpec((tr, LANES), lambda i: (i, 0))
    return pl.pallas_call(
        _with_after(body, 4, after), name="adamw_small", grid=(R // tr,),
        in_specs=[pl.BlockSpec((NDEV, tr, LANES), lambda i: (0, i, 0)), spec, spec, spec]
        + [ANY] * len(after),
        out_specs=[spec] * 4, out_shape=[_sds((R, LANES), F32)] * 4,
        compiler_params=_params(("parallel",)),
    )(parts, w, m, v, *after)


def adamw_ada(c_all_t, dmod_all, w, m, v, my_dev):
    D, cols = w.shape
    tr = _tile(D, 256)

    def body(k_ref, c_ref, d_ref, w_ref, m_ref, v_ref, g_out, d_out, m_out, v_out):
        cv = c_ref[...]
        act = cv * _sigmoid(cv)
        dm = d_ref[...]
        g = act[:, 0:1] * dm[0:1, :]
        for b in range(1, NDEV):
            g = g + act[:, b:b + 1] * dm[b:b + 1, :]
        d, mn, vn = _adamw_math(g, w_ref[...], m_ref[...], v_ref[...])
        g_out[...] = g
        d_out[...] = d
        m_out[...] = mn
        v_out[...] = vn

    spec = pl.BlockSpec((tr, cols), lambda i, kr: (i, 0))
    return pl.pallas_call(
        body, name="adamw_ada",
        grid_spec=pltpu.PrefetchScalarGridSpec(
            num_scalar_prefetch=1, grid=(D // tr,),
            in_specs=[pl.BlockSpec((tr, NDEV), lambda i, kr: (i, 0)),
                      pl.BlockSpec((NDEV, cols), lambda i, kr: (0, kr[0])), spec, spec, spec],
            out_specs=[spec] * 4),
        out_shape=[_sds((D, cols), F32)] * 4,
        compiler_params=_params(("parallel",)),
    )(my_dev, c_all_t, dmod_all, w, m, v)


def _blockdiag(t, eye):
    nb, gpb, R, C = t.shape
    return jnp.einsum("bgrc,gk->bgrkc", t, eye).reshape(nb, gpb * R, gpb * C)


def _diag_blocks(t, gpb, R, C):
    nb = t.shape[0]
    t5 = t.reshape(nb, gpb, R, gpb, C)
    idx = jnp.arange(gpb)
    return jnp.moveaxis(t5[:, idx, :, idx, :], 0, 1)


def _small_pack(parts):
    rows = []
    for p in parts:
        flat = p.reshape(-1)
        flat = jnp.pad(flat, (0, (-flat.shape[0]) % (SUBLANES * LANES)))
        rows.append(flat.reshape(-1, LANES))
    return jnp.concatenate(rows, axis=0)


def _small_unpack(buf, shapes):
    out, r = [], 0
    for s in shapes:
        n = int(math.prod(s))
        nr = -(-n // (SUBLANES * LANES)) * SUBLANES
        out.append(buf[r:r + nr].reshape(-1)[:n].reshape(s))
        r += nr
    return out


def kernel(x, c, w_ada, b_ada, w_in, lam_re, lam_im, log_dt, ssm_b_re, ssm_b_im, ssm_c_re, ssm_c_im, ssm_d, w_glu_val, w_glu_gate, w_pool, pool_scale, w_pool_out, w_out, ln1_g, ln1_b, w_ff1, w_ff2, ln2_g, ln2_b, loss_target, m_w_ada, m_b_ada, m_w_in, m_lam_re, m_lam_im, m_log_dt, m_ssm_b_re, m_ssm_b_im, m_ssm_c_re, m_ssm_c_im, m_ssm_d, m_w_glu_val, m_w_glu_gate, m_w_pool, m_pool_scale, m_w_pool_out, m_w_out, m_ln1_g, m_ln1_b, m_w_ff1, m_w_ff2, m_ln2_g, m_ln2_b, v_w_ada, v_b_ada, v_w_in, v_lam_re, v_lam_im, v_log_dt, v_ssm_b_re, v_ssm_b_im, v_ssm_c_re, v_ssm_c_im, v_ssm_d, v_w_glu_val, v_w_glu_gate, v_w_pool, v_pool_scale, v_w_pool_out, v_w_out, v_ln1_g, v_ln1_b, v_w_ff1, v_w_ff2, v_ln2_g, v_ln2_b):
    S, D = x.shape[1], x.shape[2]
    x2d, tgt = x[0], loss_target[0]
    W = D // 2
    G = W // SSM_GROUP
    P, H, GPB = SSM_STATE, SSM_GROUP, GROUPS_PER_BLOCK
    nblk = G // GPB
    gw = W // len(POOL_WINDOWS)
    ax, ay, ac = lax.axis_index("x"), lax.axis_index("y"), lax.axis_index("c")
    my_c = ac.astype(jnp.int32).reshape(1)
    my_chip = (2 * ax + ay).astype(jnp.int32).reshape(1)
    my_dev = (4 * ax + 2 * ay + ac).astype(jnp.int32).reshape(1)
    ts = _tile(S, 256)

    glu = jnp.stack([w_glu_val[0], w_glu_gate[0]]).astype(BF16)
    shards = [w_in[0].astype(BF16), glu, w_pool[0].astype(BF16), w_pool_out[0].astype(BF16),
              w_out[0].astype(BF16), w_ff1[0].astype(BF16), w_ff2[0].astype(BF16)]
    (wg_in,) = seq_all_gather("gather_w_in", shards[0:1], 1)
    wg_vg, wg_pool, wg_po, wg_out = seq_all_gather("gather_w_mix", shards[1:5], 2)
    wg_ff1, wg_ff2 = seq_all_gather("gather_w_ff", shards[5:7], 3)
    wg_vg = wg_vg.reshape(2 * NDEV, W, D // NDEV)
    nwin = len(POOL_WINDOWS)
    wp_full = jnp.transpose(wg_pool, (1, 0, 2, 3)).reshape(nwin, gw, gw)
    wout_full = wg_out.reshape(1, D, D)
    wff2_full = wg_ff2.reshape(1, 4 * D, D)

    mod, c_all = ada_fwd(c, w_ada[0], b_ada)
    mod = mod.reshape(6, 1, D)
    sh1, sc1, g1, sh2, sc2, g2 = (mod[i] for i in range(6))

    f2, kconst = s5_disc(lam_re[0], lam_im[0], log_dt[0].reshape(G, 1))
    kconst = kconst.reshape(16, SUBLANES, G * P)
    f2r = f2.reshape(2, 1, G * P)
    bt_re = jnp.transpose(ssm_b_re[0], (2, 0, 1)).reshape(H, G * P)
    bt_im = jnp.transpose(ssm_b_im[0], (2, 0, 1)).reshape(H, G * P)
    bbar = s5_bbar(f2r, bt_re, bt_im)
    eye = jnp.eye(GPB, dtype=F32)
    bb4 = jnp.transpose(bbar.reshape(2, H, nblk, GPB, P), (0, 2, 3, 1, 4))
    bmat = jnp.concatenate([_blockdiag(bb4[0], eye), _blockdiag(bb4[1], eye)], axis=2).astype(BF16)
    c4_re = jnp.transpose(ssm_c_re[0].reshape(nblk, GPB, H, P), (0, 1, 3, 2))
    c4_im = jnp.transpose(ssm_c_im[0].reshape(nblk, GPB, H, P), (0, 1, 3, 2))
    cmat = jnp.concatenate([_blockdiag(c4_re, eye), -_blockdiag(c4_im, eye)], axis=1).astype(BF16)

    def e1(t, b):
        xhat, _ = _ln_stats(t[0])
        return [xhat * (1.0 + b[0]) + b[1]], []
    (h1,) = _rowwise("ln_mod1", e1, S, ts, [(x2d, D, 0)], [sc1, sh1], [(D, BF16)], [])

    (proj,) = mm_nn("proj", h1, wg_in, F32, 1)
    z, xsb_all = s5_fwd(proj, bmat, cmat, ssm_d, kconst)
    (vt,) = mm_nn("glu", z, wg_vg, BF16, 4)
    pooled = pool_fwd(proj, W, W, gw)

    def pool_epi(vals, ex, outs):
        a = vals[0]
        outs[0][...] = a
        outs[1][...] = (a * ex[0][...]).astype(BF16)
    tmp = _tile(S, 1024)
    yp, ypool = _mm(
        "pool_mix", "nn", pooled, wp_full.astype(BF16), (S // tmp, nwin, 1),
        pl.BlockSpec((tmp, gw), lambda i, j, k: (i, j)), pl.BlockSpec((1, gw, gw), lambda i, j, k: (j, 0, 0)),
        [(_sds((S, W), F32), pl.BlockSpec((tmp, gw), lambda i, j, k: (i, j))),
         (_sds((S, W), BF16), pl.BlockSpec((tmp, gw), lambda i, j, k: (i, j)))],
        (tmp, gw), 1, gw, None, pool_epi,
        [(pool_scale, pl.BlockSpec((1, gw), lambda i, j, k: (0, j)))])
    (y_b,) = mm_nn("pool_out", ypool, wg_po, BF16, 4)

    cb = D // NDEV
    ga_cb, gb_cb = (2 * W) // cb, (2 * W + D) // cb
    tsm = _tile(S, 512)

    def merge_call(name, fn, ins, n_out, after=()):
        def body(*refs):
            vals = [r[...].astype(F32) for r in refs[:len(ins)]]
            for r, v in zip(refs[len(ins):], fn(*vals)):
                r[...] = v.astype(r.dtype)
        return pl.pallas_call(
            _with_after(body, len(ins), after), name=name, grid=(S // tsm, NDEV),
            in_specs=[pl.BlockSpec((tsm, w), f) for (_, w, f) in ins] + [ANY] * len(after),
            out_specs=[pl.BlockSpec((tsm, w), lambda i, j: (i, j)) for (_, w) in n_out],
            out_shape=[_sds((S, cols), BF16) for (cols, _) in n_out],
            compiler_params=_params(("parallel", "parallel")),
        )(*[a for (a, _, _) in ins], *after)

    merge_ins = [(proj, cb, lambda i, j: (i, ga_cb + j)), (proj, cb, lambda i, j: (i, gb_cb + j)),
                 (vt, 2 * cb, lambda i, j: (i, j)), (y_b, cb, lambda i, j: (i, j))]

    def merge_f(ga, gb, vtv, yb):
        return [_sigmoid(ga) * (vtv[:, :cb] * _sigmoid(vtv[:, cb:])) + _sigmoid(gb) * yb]
    (merged,) = merge_call("merge", merge_f, merge_ins, [(D, cb)])

    (mix,) = mm_nn("mix_out", merged, wout_full, F32, 1)

    def e3(t, b):
        xv, mx = t
        g1v, l1g, l1b, sc2v, sh2v = b
        r1 = ALPHA * xv + g1v * mx
        xh1, _ = _ln_stats(r1)
        x1 = xh1 * l1g + l1b
        xh, _ = _ln_stats(x1)
        return [r1, xh * (1.0 + sc2v) + sh2v], []
    r1, h2 = _rowwise("post_mix", e3, S, ts, [(x2d, D, 0), (mix, D, 0)],
                      [g1, ln1_g, ln1_b, sc2, sh2], [(D, F32), (D, BF16)], [])

    def relu_epi(vals, ex, outs):
        outs[0][...] = jnp.maximum(vals[0], 0.0).astype(BF16)
    (rl,) = mm_nn("ff1", h2, wg_ff1, BF16, 1, epi=relu_epi)

    def square(a):
        return a * a
    (y2,) = mm_nn("ff2", rl, wff2_full, F32, 1, pro=square)

    def e4(t, b):
        r1v, y2v, tg = t
        g2v, l1g, l1b, l2g, l2b = b
        xh1, _ = _ln_stats(r1v)
        x1 = xh1 * l1g + l1b
        r2 = ALPHA * x1 + g2v * y2v
        xh2, rs2 = _ln_stats(r2)
        err = xh2 * l2g + l2b - tg
        dx2 = err * (1.0 / D)
        dr2 = _ln_bwd(dx2 * l2g, xh2, rs2)
        lsum = jnp.sum(_colsum(err * err), axis=1, keepdims=True) * (0.5 / D)
        return ([ALPHA * dr2, g2v * dr2],
                [jnp.broadcast_to(lsum, (1, LANES)), _colsum(dx2 * xh2), _colsum(dx2), _colsum(dr2 * y2v)])
    dx1a, dy2, loss_acc, g_ln2g, g_ln2b, d_g2 = _rowwise(
        "head", e4, S, ts, [(r1, D, 0), (y2, D, 0), (tgt, D, 0)], [g2, ln1_g, ln1_b, ln2_g, ln2_b],
        [(D, F32), (D, BF16)], [LANES, D, D, D])

    tn_ff = _tile(4 * D, 1024)

    def dff_epi(vals, ex, outs):
        outs[0][...] = (vals[0] * (2.0 * ex[0][...].astype(F32))).astype(BF16)
    tmf = _tile(S, 1024)
    (da1,) = mm_nt("d_ff2", dy2, wff2_full, BF16, 1, tn=tn_ff, epi=dff_epi,
                   extras=[(rl, pl.BlockSpec((tmf, tn_ff), lambda i, j, k: (i, j)))])
    gw_ff2 = mm_tn("gw_ff2", rl, dy2, BF16, NDEV, 0, pro=square)
    gw_ff1 = mm_tn("gw_ff1", h2, da1, BF16, NDEV, 1)
    tok, wait_pair_a = pair_exchange("pair_exchange_ff", [gw_ff2, gw_ff1], 4)
    (dh2,) = mm_nt("d_ff1", da1, wg_ff1, F32, 2, after=[tok])

    def e5(t, b):
        dh2v, r1v, dx1av, mx = t
        sc2v, l1g, l1b, g1v = b
        xh1, rs1 = _ln_stats(r1v)
        x1 = xh1 * l1g + l1b
        xh, rs = _ln_stats(x1)
        dx1 = dx1av + _ln_bwd(dh2v * (1.0 + sc2v), xh, rs)
        dr1 = _ln_bwd(dx1 * l1g, xh1, rs1)
        return ([ALPHA * dr1, g1v * dr1],
                [_colsum(dh2v * xh), _colsum(dh2v), _colsum(dx1 * xh1), _colsum(dx1), _colsum(dr1 * mx)])
    dxa, dmix, d_sc2, d_sh2, g_ln1g, g_ln1b, d_g1 = _rowwise(
        "post_mix_bwd", e5, S, ts, [(dh2, D, 0), (r1, D, 0), (dx1a, D, 0), (mix, D, 0)],
        [sc2, ln1_g, ln1_b, g1], [(D, F32), (D, BF16)], [D, D, D, D, D])

    (dmerged,) = mm_nt("d_mix_out", dmix, wout_full, BF16, 1)
    gw_out = mm_tn("gw_out", merged, dmix, BF16, NDEV, 0)
    grads_a, got_a = wait_pair_a(gw_out)
    parts_a = [pair_sum("pair_sum_ff%d" % i, g, t, my_c) for i, (g, t) in enumerate(zip(grads_a, got_a))]
    tok, wait_chip_a = chip_exchange("chip_exchange_ff", parts_a, 5)

    def merge_b(ga, gb, vtv, yb, dm):
        vv, tt = vtv[:, :cb], vtv[:, cb:]
        sa, sb, st = _sigmoid(ga), _sigmoid(gb), _sigmoid(tt)
        dya = dm * sa
        return [dm * (vv * st) * sa * (1.0 - sa), dm * yb * sb * (1.0 - sb),
                jnp.concatenate([dya * st, dya * vv * st * (1.0 - st)], axis=1), dm * sb]
    dga, dgb_, dvt, dy_b = merge_call(
        "merge_bwd", merge_b, merge_ins + [(dmerged, cb, lambda i, j: (i, j))],
        [(D, cb), (D, cb), (2 * D, 2 * cb), (D, cb)], after=[tok])

    (dypool,) = mm_nt("d_pool_out", dy_b, wg_po, F32, NDEV)
    gw_po = mm_tn("gw_pool_out", ypool, dy_b, BF16, NDEV, 4)

    def e7(t, b):
        return [t[0] * b[0]], [_colsum(t[0] * t[1])]
    dyp, g_pscale = _rowwise("pool_scale_bwd", e7, S, ts, [(dypool, W, 0), (yp, W, 0)],
                             [pool_scale], [(W, BF16)], [W])
    (dpooled,) = _mm(
        "d_pool_mix", "nt", dyp, wp_full.astype(BF16), (S // tmp, nwin, 1),
        pl.BlockSpec((tmp, gw), lambda i, j, k: (i, j)), pl.BlockSpec((1, gw, gw), lambda i, j, k: (j, 0, 0)),
        [(_sds((S, W), F32), pl.BlockSpec((tmp, gw), lambda i, j, k: (i, j)))], (tmp, gw), 1, gw)
    tkp = _tile(S, 2048)
    gw_pool = _mm(
        "gw_pool", "tn", pooled, dyp, (nwin, 1, S // tkp),
        pl.BlockSpec((tkp, gw), lambda i, j, k: (k, i)), pl.BlockSpec((tkp, gw), lambda i, j, k: (k, i)),
        [(_sds((nwin, gw, gw), BF16), pl.BlockSpec((1, gw, gw), lambda i, j, k: (i, 0, 0)))],
        (gw, gw), 1, gw, stacked_out=True)[0]
    du_pool = pool_bwd(dpooled, gw)

    (dz,) = mm_nt("d_glu", dvt, wg_vg, BF16, NDEV)
    gw_vg = mm_tn("gw_glu", z, dvt, BF16, 2 * NDEV, 4)
    gw_pool_st = jnp.transpose(gw_pool.reshape(nwin, NDEV, gw // NDEV, gw), (1, 0, 2, 3))
    grads_b = [gw_out, gw_po, gw_pool_st, gw_vg.reshape(NDEV, 2, W, D // NDEV)]
    tok, wait_pair_b = pair_exchange("pair_exchange_mix", grads_b, 6)
    du_ssm, g_bmat, g_cmat, g_d, g_a = s5_bwd(proj, xsb_all, dz, bmat, cmat, ssm_d, kconst, after=[tok])
    grads_b, got_b = wait_pair_b(du_ssm)
    parts_b = [pair_sum("pair_sum_mix%d" % i, g, t, my_c) for i, (g, t) in enumerate(zip(grads_b, got_b))]
    tok, wait_chip_b = chip_exchange("chip_exchange_mix", parts_b, 7)

    dproj = jnp.concatenate([du_ssm, du_pool, dga, dgb_], axis=1)
    gw_in = mm_tn("gw_in", h1, dproj, BF16, NDEV, 1, after=[tok])
    tok, wait_pair_c = pair_exchange("pair_exchange_in", [gw_in], 8)
    (dh1,) = mm_nt("d_proj", dproj, wg_in, F32, 2, after=[tok])
    grads_c, got_c = wait_pair_c(dh1)
    parts_c = [pair_sum("pair_sum_in", grads_c[0], got_c[0], my_c)]
    tok, wait_chip_c = chip_exchange("chip_exchange_in", parts_c, 9)

    def e10(t, b):
        dh1v, xv, dxav = t
        xh, rs = _ln_stats(xv)
        return ([dxav + _ln_bwd(dh1v * (1.0 + b[0]), xh, rs)],
                [_colsum(dh1v * xh), _colsum(dh1v)])
    grad_x, d_sc1, d_sh1 = _rowwise("ln_mod1_bwd", e10, S, ts, [(dh1, D, 0), (x2d, D, 0), (dxa, D, 0)],
                                    [sc1], [(D, F32)], [D, D], after=[tok])

    gb4 = _diag_blocks(g_bmat[:, :, :GPB * P], GPB, H, P), _diag_blocks(g_bmat[:, :, GPB * P:], GPB, H, P)
    dbb = jnp.stack([jnp.transpose(t, (2, 0, 1, 3)).reshape(H, G * P) for t in gb4])
    g_bt_re, g_bt_im, g_f = s5_bbar_bwd(f2r, bt_re, bt_im, dbb)
    g_b_re = jnp.transpose(g_bt_re.reshape(H, G, P), (1, 2, 0))
    g_b_im = jnp.transpose(g_bt_im.reshape(H, G, P), (1, 2, 0))
    gc_top = _diag_blocks(g_cmat[:, :GPB * P, :], GPB, P, H)
    gc_bot = _diag_blocks(g_cmat[:, GPB * P:, :], GPB, P, H)
    g_c_re = jnp.transpose(gc_top, (0, 1, 3, 2)).reshape(G, H, P)
    g_c_im = -jnp.transpose(gc_bot, (0, 1, 3, 2)).reshape(G, H, P)
    d_ab = jnp.transpose(g_a.reshape(nblk, 2, GPB, P), (1, 0, 2, 3)).reshape(2, G, P)
    g_lr, g_li, g_ldt = s5_disc_bwd(lam_re[0], lam_im[0], log_dt[0].reshape(G, 1), d_ab,
                                    g_f.reshape(2, G, P))

    dmod = jnp.concatenate([d_sh1, d_sc1, d_g1, d_sh2, d_sc2, d_g2], axis=1)
    small_names = [b_ada, lam_re, lam_im, log_dt, ssm_b_re, ssm_b_im, ssm_c_re, ssm_c_im, ssm_d,
                   pool_scale, ln1_g, ln1_b, ln2_g, ln2_b]
    small_m = [m_b_ada, m_lam_re, m_lam_im, m_log_dt, m_ssm_b_re, m_ssm_b_im, m_ssm_c_re, m_ssm_c_im,
               m_ssm_d, m_pool_scale, m_ln1_g, m_ln1_b, m_ln2_g, m_ln2_b]
    small_v = [v_b_ada, v_lam_re, v_lam_im, v_log_dt, v_ssm_b_re, v_ssm_b_im, v_ssm_c_re, v_ssm_c_im,
               v_ssm_d, v_pool_scale, v_ln1_g, v_ln1_b, v_ln2_g, v_ln2_b]
    small_g = [dmod, g_lr, g_li, g_ldt, g_b_re, g_b_im, g_c_re, g_c_im, g_d, g_pscale,
               g_ln1g, g_ln1b, g_ln2g, g_ln2b, loss_acc]
    zero_row = jnp.zeros((1, LANES), F32)
    packed_g = _small_pack(small_g)
    (parts_all,) = seq_all_gather("gather_small", [packed_g], 10, routed=False)
    parts_a, got3_a = wait_chip_a(packed_g)
    glu_w = jnp.stack([w_glu_val[0], w_glu_gate[0]])
    glu_m = jnp.stack([m_w_glu_val[0], m_w_glu_gate[0]])
    glu_v = jnp.stack([v_w_glu_val[0], v_w_glu_gate[0]])
    wmv = [(w_ff2[0], m_w_ff2[0], v_w_ff2[0]), (w_ff1[0], m_w_ff1[0], v_w_ff1[0]),
           (w_out[0], m_w_out[0], v_w_out[0]), (w_pool_out[0], m_w_pool_out[0], v_w_pool_out[0]),
           (w_pool[0], m_w_pool[0], v_w_pool[0]), (glu_w, glu_m, glu_v)]
    upd = [adamw_sharded("adamw_%d" % i, p, t, w, m, v, my_chip)
           for i, (p, t, (w, m, v)) in enumerate(zip(parts_a, got3_a, wmv[:2]))]
    parts_b, got3_b = wait_chip_b(upd[-1][0])
    upd += [adamw_sharded("adamw_%d" % (2 + i), p, t, w, m, v, my_chip)
            for i, (p, t, (w, m, v)) in enumerate(zip(parts_b, got3_b, wmv[2:]))]
    u_ff2, u_ff1, u_out, u_po, u_pool, u_glu = upd

    sg, sd, sm, sv = adamw_small(parts_all, _small_pack(small_names + [zero_row]),
                                 _small_pack(small_m + [zero_row]), _small_pack(small_v + [zero_row]),
                                 after=[upd[-1][0]])
    shapes = [t.shape for t in small_names]
    loss = _small_unpack(sg, shapes + [(1, LANES)])[-1][0, 0]
    sg, sd, sm, sv = (_small_unpack(t, shapes) for t in (sg, sd, sm, sv))

    nmod = 6 * D
    dmod_all = parts_all[:, :nmod // LANES, :].reshape(NDEV, nmod)
    c_all_t = jnp.transpose(c_all.reshape(NDEV, D))
    ada_out = adamw_ada(c_all_t, dmod_all, w_ada[0], m_w_ada[0], v_w_ada[0], my_dev)
    parts_c, got3_c = wait_chip_c(ada_out[0])
    u_in = adamw_sharded("adamw_6", parts_c[0], got3_c[0], w_in[0], m_w_in[0], v_w_in[0], my_chip)

    def pick(k):
        return [ada_out[k][None], sg_sd[k][0], u_in[k][None]] + [t for t in sg_sd[k][1:9]] + \
               [u_glu[k][0][None], u_glu[k][1][None], u_pool[k][None], sg_sd[k][9], u_po[k][None],
                u_out[k][None], sg_sd[k][10], sg_sd[k][11], u_ff1[k][None], u_ff2[k][None],
                sg_sd[k][12], sg_sd[k][13]]

    sg_sd = [sg, sd, sm, sv]
    return (loss, grad_x[None], *pick(0), *pick(1), *pick(2), *pick(3))
```

```python
import functools
import math

import jax
import jax.numpy as jnp
from jax import lax
from jax.experimental import pallas as pl
from jax.experimental.pallas import tpu as pltpu
from jax.experimental.pallas import tpu_sc as plsc

F32 = jnp.float32
BF16 = jnp.bfloat16
MESH = pl.DeviceIdType.MESH
NDEV = 8
NCHIP = 4

SSM_GROUP = 16
SSM_STATE = 64
GROUPS_PER_BLOCK = 8
POOL_WINDOWS = (2, 4, 8, 16)
LN_EPS = 1e-5
ALPHA = 2.0 ** 0.25
ADAM_LR, ADAM_B1, ADAM_B2, ADAM_EPS, ADAM_WD, ADAM_STEP = 0.001, 0.9, 0.999, 1e-08, 0.01, 10
SUBLANES = 8
LANES = 128
VMEM_LIMIT = 56 * 1024 * 1024


def _params(sem=None, vmem=VMEM_LIMIT):
    return pltpu.CompilerParams(dimension_semantics=sem, vmem_limit_bytes=vmem)


def _tile(n, pref):
    if n <= pref:
        return n
    t = 1 << (pref.bit_length() - 1)
    while n % t:
        t //= 2
    return t


def _cast_epi(vals, ex, outs):
    c = vals[0].shape[1]
    for s, v in enumerate(vals):
        outs[0][:, s * c:(s + 1) * c] = v.astype(outs[0].dtype)


ANY = pl.BlockSpec(memory_space=pl.ANY)


def _with_after(body, n_in, after):
    if not after:
        return body
    n_af = len(after)

    def wrapped(*refs):
        return body(*refs[:n_in], *refs[n_in + n_af:])
    return wrapped


def _mm(name, kind, a, b, grid, a_spec, b_spec, outs, acc_shape, nsub=1, c=None,
        pro=None, epi=None, extras=(), stacked_out=False, after=()):
    nk = grid[2]
    n_ex, n_out = len(extras), len(outs)

    def finish(vals, ex, out_refs):
        if epi is not None:
            epi(vals, ex, out_refs)
        elif stacked_out:
            for s, v in enumerate(vals):
                out_refs[0][s] = v.astype(out_refs[0].dtype)
        else:
            _cast_epi(vals, ex, out_refs)

    def body(*refs):
        a_ref, b_ref = refs[0], refs[1]
        ex = refs[2:2 + n_ex]
        out_refs = refs[2 + n_ex:2 + n_ex + n_out]
        k = pl.program_id(2)
        av = a_ref[...]
        if pro is not None:
            av = pro(av)
        if kind == "nn":
            prods = [jnp.dot(av, b_ref[s], preferred_element_type=F32) for s in range(nsub)]
        elif kind == "nt":
            t = None
            for s in range(nsub):
                d = lax.dot_general(av[:, s * c:(s + 1) * c], b_ref[s], (((1,), (1,)), ((), ())),
                                    preferred_element_type=F32)
                t = d if t is None else t + d
            prods = [t]
        else:
            t = lax.dot_general(av, b_ref[...], (((0,), (0,)), ((), ())), preferred_element_type=F32)
            prods = [t[:, s * c:(s + 1) * c] for s in range(nsub)] if stacked_out else [t]
        if nk == 1:
            finish(prods, ex, out_refs)
            return
        acc = refs[-1]
        w = prods[0].shape[1]

        @pl.when(k == 0)
        def _():
            for s, p in enumerate(prods):
                acc[:, s * w:(s + 1) * w] = p

        @pl.when(jnp.logical_and(k > 0, k < nk - 1))
        def _():
            for s, p in enumerate(prods):
                acc[:, s * w:(s + 1) * w] += p

        @pl.when(k == nk - 1)
        def _():
            finish([acc[:, s * w:(s + 1) * w] + p for s, p in enumerate(prods)], ex, out_refs)

    res = pl.pallas_call(
        _with_after(body, 2 + n_ex, after), name=name, grid=grid,
        in_specs=[a_spec, b_spec] + [e[1] for e in extras] + [ANY] * len(after),
        out_specs=[o[1] for o in outs],
        out_shape=[o[0] for o in outs],
        scratch_shapes=[pltpu.VMEM(acc_shape, F32)] if nk > 1 else [],
        compiler_params=_params(("parallel", "parallel", "arbitrary")),
    )(a, b, *[e[0] for e in extras], *after)
    return res


def _sds(shape, dtype):
    return jax.ShapeDtypeStruct(shape, dtype)


def mm_nn(name, a, b3, out_dtype, nsub, tm=1024, tk=2048, tn=None, pro=None, epi=None,
          extras=(), extra_outs=(), a_col0=0, after=()):
    M = a.shape[0]
    nb, K, cdim = b3.shape
    tm, tk = _tile(M, tm), _tile(K, tk)
    if nb == 1:
        tn = _tile(cdim, tn or 1024)
        nsub, c, nj = 1, tn, cdim // tn
        b_spec = pl.BlockSpec((1, tk, tn), lambda i, j, k: (0, k, j))
        N = cdim
    else:
        c, nj, tn = cdim, nb // nsub, nsub * cdim
        b_spec = pl.BlockSpec((nsub, tk, cdim), lambda i, j, k: (j, k, 0))
        N = nb * cdim
    kb0 = a_col0 // tk
    a_spec = pl.BlockSpec((tm, tk), lambda i, j, k: (i, kb0 + k))
    grid = (M // tm, nj, K // tk)
    o_spec = pl.BlockSpec((tm, tn), lambda i, j, k: (i, j))
    outs = [(_sds((M, N), out_dtype), o_spec)] + [(_sds((M, N), d), o_spec) for d in extra_outs]
    return _mm(name, "nn", a, b3, grid, a_spec, b_spec, outs, (tm, tn), nsub, c, pro, epi, extras,
               after=after)


def mm_nt(name, a, b3, out_dtype, nsub, tm=1024, tn=1024, epi=None, extras=(), extra_outs=(),
          after=()):
    M = a.shape[0]
    nb, N, cdim = b3.shape
    tm, tn = _tile(M, tm), _tile(N, tn)
    if nb == 1:
        tk = _tile(cdim, 2048)
        nsub, c, nk = 1, tk, cdim // tk
        b_spec = pl.BlockSpec((1, tn, tk), lambda i, j, k: (0, j, k))
    else:
        c, nk, tk = cdim, nb // nsub, nsub * cdim
        b_spec = pl.BlockSpec((nsub, tn, cdim), lambda i, j, k: (k, j, 0))
    a_spec = pl.BlockSpec((tm, tk), lambda i, j, k: (i, k))
    grid = (M // tm, N // tn, nk)
    o_spec = pl.BlockSpec((tm, tn), lambda i, j, k: (i, j))
    outs = [(_sds((M, N), out_dtype), o_spec)] + [(_sds((M, N), d), o_spec) for d in extra_outs]
    return _mm(name, "nt", a, b3, grid, a_spec, b_spec, outs, (tm, tn), nsub, c, None, epi, extras,
               after=after)


def mm_tn(name, a, b, out_dtype, nb, nsub, tma=1024, tk=2048, pro=None, a_col0=0, a_cols=None,
          after=()):
    S = a.shape[0]
    Ka = a_cols or a.shape[1]
    N = b.shape[1]
    tk = _tile(S, tk)
    if nsub == 0:
        tma, tn = _tile(Ka, tma), _tile(N, 1024)
        grid = (Ka // tma, N // tn, S // tk)
        ab0 = a_col0 // tma
        res = _mm(name, "tn", a, b, grid, pl.BlockSpec((tk, tma), lambda i, j, k: (k, ab0 + i)),
                  pl.BlockSpec((tk, tn), lambda i, j, k: (k, j)),
                  [(_sds((Ka, N), out_dtype), pl.BlockSpec((tma, tn), lambda i, j, k: (i, j)))],
                  (tma, tn), 1, tn, pro, None, (), after=after)[0]
        return res.reshape(nb, Ka // nb, N)
    else:
        c = N // nb
        tma = _tile(Ka, tma)
        grid = (Ka // tma, nb // nsub, S // tk)
        o_spec = pl.BlockSpec((nsub, tma, c), lambda i, j, k: (j, i, 0))
        out = _sds((nb, Ka, c), out_dtype)
        nsub_k = nsub
        tn = nsub * c
        b_spec = pl.BlockSpec((tk, tn), lambda i, j, k: (k, j))
    ab0 = a_col0 // tma
    a_spec = pl.BlockSpec((tk, tma), lambda i, j, k: (k, ab0 + i))
    return _mm(name, "tn", a, b, grid, a_spec, b_spec, [(out, o_spec)], (tma, tn), nsub_k, c,
               pro, None, (), stacked_out=True, after=after)[0]


def _rowwise(name, fn, S, ts, tiled, bcast, tiled_out, acc_out, after=()):
    nt, nb, no, na = len(tiled), len(bcast), len(tiled_out), len(acc_out)

    def body(*refs):
        tin = [r[...] for r in refs[:nt]]
        bin_ = [r[...] for r in refs[nt:nt + nb]]
        o_refs = refs[nt + nb:nt + nb + no]
        a_refs = refs[nt + nb + no:]
        touts, aouts = fn(tin, bin_)
        for r, v in zip(o_refs, touts):
            r[...] = v.astype(r.dtype)
        i = pl.program_id(0)

        @pl.when(i == 0)
        def _():
            for r, v in zip(a_refs, aouts):
                r[...] = v

        @pl.when(i > 0)
        def _():
            for r, v in zip(a_refs, aouts):
                r[...] += v

    in_specs = [pl.BlockSpec((ts, w), functools.partial(lambda i, cb: (i, cb), cb=cb))
                for (_, w, cb) in tiled]
    in_specs += [pl.BlockSpec(b.shape, lambda i: (0, 0)) for b in bcast]
    out_specs = [pl.BlockSpec((ts, w), lambda i: (i, 0)) for (w, _) in tiled_out]
    out_specs += [pl.BlockSpec((1, w), lambda i: (0, 0)) for w in acc_out]
    out_shape = [_sds((S, w), d) for (w, d) in tiled_out] + [_sds((1, w), F32) for w in acc_out]
    return pl.pallas_call(
        _with_after(body, nt + nb, after), name=name, grid=(S // ts,),
        in_specs=in_specs + [ANY] * len(after), out_specs=out_specs,
        out_shape=out_shape, compiler_params=_params(("arbitrary",)),
    )(*[t[0] for t in tiled], *bcast, *after)


def _ln_stats(v):
    mu = jnp.mean(v, axis=-1, keepdims=True)
    vc = v - mu
    var = jnp.mean(vc * vc, axis=-1, keepdims=True)
    rstd = lax.rsqrt(var + LN_EPS)
    return vc * rstd, rstd


def _ln_bwd(dxhat, xhat, rstd):
    return rstd * (dxhat - jnp.mean(dxhat, axis=-1, keepdims=True)
                   - xhat * jnp.mean(dxhat * xhat, axis=-1, keepdims=True))


def _colsum(v):
    return jnp.sum(v, axis=0, keepdims=True)


def _sigmoid(v):
    return 1.0 / (1.0 + jnp.exp(-v))


_GELU_C = math.sqrt(2.0 / math.pi)


def _gelu(v):
    return 0.5 * v * (1.0 + jnp.tanh(_GELU_C * (v + 0.044715 * v * v * v)))


def _gelu_grad(v):
    t = jnp.tanh(_GELU_C * (v + 0.044715 * v * v * v))
    return 0.5 * (1.0 + t) + 0.5 * v * (1.0 - t * t) * _GELU_C * (1.0 + 3 * 0.044715 * v * v)


def _disc(lr, li, ldt):
    dt = jnp.exp(ldt)
    mag = jnp.exp(lr * dt)
    ang = li * dt
    ab_re = mag * jnp.cos(ang)
    ab_im = mag * jnp.sin(ang)
    num_re = ab_re - 1.0
    num_im = ab_im
    den = lr * lr + li * li
    f_re = (num_re * lr + num_im * li) / den
    f_im = (num_im * lr - num_re * li) / den
    return ab_re, ab_im, f_re, f_im


def _cmul(ar, ai, br, bi):
    return ar * br - ai * bi, ar * bi + ai * br


def s5_disc(lam_re, lam_im, log_dt):
    G, P = lam_re.shape

    def body(lr_ref, li_ref, ldt_ref, f_ref, k_ref):
        ab_re, ab_im, f_re, f_im = _disc(lr_ref[...], li_ref[...], ldt_ref[...])
        f_ref[0] = f_re
        f_ref[1] = f_im
        pr, pi = [ab_re], [ab_im]
        for _ in range(SUBLANES - 1):
            nr, ni = _cmul(pr[-1], pi[-1], ab_re, ab_im)
            pr.append(nr)
            pi.append(ni)
        zero = jnp.zeros_like(ab_re)
        for n, sh in enumerate((1, 2, 4)):
            for r in range(SUBLANES):
                k_ref[2 * n, r] = pr[sh - 1] if r >= sh else zero
                k_ref[2 * n + 1, r] = pi[sh - 1] if r >= sh else zero
                k_ref[8 + 2 * n, r] = pr[sh - 1] if r + sh < SUBLANES else zero
                k_ref[8 + 2 * n + 1, r] = -pi[sh - 1] if r + sh < SUBLANES else zero
        for r in range(SUBLANES):
            k_ref[6, r] = pr[r]
            k_ref[7, r] = pi[r]
            k_ref[14, r] = pr[SUBLANES - 1 - r]
            k_ref[15, r] = -pi[SUBLANES - 1 - r]

    vm = pl.BlockSpec(memory_space=pltpu.VMEM)
    return pl.pallas_call(
        body, name="s5_disc", in_specs=[vm, vm, vm], out_specs=[vm, vm],
        out_shape=[_sds((2, G, P), F32), _sds((16, SUBLANES, G, P), F32)],
    )(lam_re, lam_im, log_dt)


def s5_disc_bwd(lam_re, lam_im, log_dt, d_ab, d_f):
    G, P = lam_re.shape

    def body(lr_ref, li_ref, ldt_ref, dab_ref, df_ref, glr_ref, gli_ref, gdt_ref):
        _, vjp = jax.vjp(_disc, lr_ref[...], li_ref[...], ldt_ref[...])
        glr, gli, gdt = vjp((dab_ref[0], dab_ref[1], df_ref[0], df_ref[1]))
        glr_ref[...] = glr
        gli_ref[...] = gli
        gdt_ref[...] = gdt

    vm = pl.BlockSpec(memory_space=pltpu.VMEM)
    return pl.pallas_call(
        body, name="s5_disc_bwd", in_specs=[vm] * 5, out_specs=[vm] * 3,
        out_shape=[_sds((G, P), F32), _sds((G, P), F32), _sds((G, 1), F32)],
    )(lam_re, lam_im, log_dt, d_ab, d_f)


def s5_bbar(f2, bt_re, bt_im):
    def body(f_ref, br_ref, bi_ref, o_ref):
        fr, fi = f_ref[0], f_ref[1]
        br, bi = br_ref[...], bi_ref[...]
        o_ref[0] = fr * br - fi * bi
        o_ref[1] = fr * bi + fi * br

    vm = pl.BlockSpec(memory_space=pltpu.VMEM)
    return pl.pallas_call(body, name="s5_bbar", in_specs=[vm] * 3, out_specs=vm,
                          out_shape=_sds((2,) + bt_re.shape, F32))(f2, bt_re, bt_im)


def s5_bbar_bwd(f2, bt_re, bt_im, dbb):
    def body(f_ref, br_ref, bi_ref, d_ref, gbr_ref, gbi_ref, gf_ref):
        fr, fi = f_ref[0], f_ref[1]
        br, bi = br_ref[...], bi_ref[...]
        dr, di = d_ref[0], d_ref[1]
        gbr_ref[...] = fr * dr + fi * di
        gbi_ref[...] = fr * di - fi * dr
        gf_ref[0] = _colsum(dr * br + di * bi)
        gf_ref[1] = _colsum(di * br - dr * bi)

    vm = pl.BlockSpec(memory_space=pltpu.VMEM)
    return pl.pallas_call(
        body, name="s5_bbar_bwd", in_specs=[vm] * 4, out_specs=[vm] * 3,
        out_shape=[_sds(bt_re.shape, F32), _sds(bt_re.shape, F32), _sds(f2.shape, F32)],
    )(f2, bt_re, bt_im, dbb)


def _scan_fwd(xs, k_ref, nst):
    ntile = xs.shape[0] // SUBLANES

    def step(t, carry):
        cr, ci = carry
        r0 = pl.multiple_of(t * SUBLANES, SUBLANES)
        xr = xs[pl.ds(r0, SUBLANES), 0:nst]
        xi = xs[pl.ds(r0, SUBLANES), nst:2 * nst]
        for n, sh in enumerate((1, 2, 4)):
            sr = pltpu.roll(xr, sh, 0)
            si = pltpu.roll(xi, sh, 0)
            mr, mi = k_ref[2 * n], k_ref[2 * n + 1]
            xr, xi = xr + mr * sr - mi * si, xi + mr * si + mi * sr
        pr, pi = k_ref[6], k_ref[7]
        xr, xi = xr + pr * cr - pi * ci, xi + pr * ci + pi * cr
        xs[pl.ds(r0, SUBLANES), 0:nst] = xr
        xs[pl.ds(r0, SUBLANES), nst:2 * nst] = xi
        return (jnp.broadcast_to(xr[SUBLANES - 1:SUBLANES, :], xr.shape),
                jnp.broadcast_to(xi[SUBLANES - 1:SUBLANES, :], xi.shape))

    zero = jnp.zeros((SUBLANES, nst), F32)
    lax.fori_loop(0, ntile, step, (zero, zero))


def _scan_bwd(g, xs, k_ref, nst):
    ntile = g.shape[0] // SUBLANES
    row = lax.broadcasted_iota(jnp.int32, (SUBLANES, nst), 0)

    def step(tt, carry):
        cr, ci, ar, ai = carry
        t = ntile - 1 - tt
        r0 = pl.multiple_of(t * SUBLANES, SUBLANES)
        gr = g[pl.ds(r0, SUBLANES), 0:nst]
        gi = g[pl.ds(r0, SUBLANES), nst:2 * nst]
        for n, sh in enumerate((1, 2, 4)):
            sr = pltpu.roll(gr, SUBLANES - sh, 0)
            si = pltpu.roll(gi, SUBLANES - sh, 0)
            mr, mi = k_ref[8 + 2 * n], k_ref[8 + 2 * n + 1]
            gr, gi = gr + mr * sr - mi * si, gi + mr * si + mi * sr
        qr, qi = k_ref[14], k_ref[15]
        gr, gi = gr + qr * cr - qi * ci, gi + qr * ci + qi * cr
        g[pl.ds(r0, SUBLANES), 0:nst] = gr
        g[pl.ds(r0, SUBLANES), nst:2 * nst] = gi
        p0 = pl.multiple_of(jnp.maximum(t - 1, 0) * SUBLANES, SUBLANES)
        live = (t > 0).astype(F32)
        xr = xs[pl.ds(r0, SUBLANES), 0:nst]
        xi = xs[pl.ds(r0, SUBLANES), nst:2 * nst]
        pr = xs[pl.ds(p0, SUBLANES), 0:nst][SUBLANES - 1:SUBLANES, :] * live
        pi = xs[pl.ds(p0, SUBLANES), nst:2 * nst][SUBLANES - 1:SUBLANES, :] * live
        xmr = jnp.where(row == 0, jnp.broadcast_to(pr, xr.shape), pltpu.roll(xr, 1, 0))
        xmi = jnp.where(row == 0, jnp.broadcast_to(pi, xi.shape), pltpu.roll(xi, 1, 0))
        ar = ar + gr * xmr + gi * xmi
        ai = ai + gi * xmr - gr * xmi
        return (jnp.broadcast_to(gr[0:1, :], gr.shape), jnp.broadcast_to(gi[0:1, :], gi.shape),
                ar, ai)

    zero = jnp.zeros((SUBLANES, nst), F32)
    _, _, ar, ai = lax.fori_loop(0, ntile, step, (zero, zero, zero, zero))
    return _colsum(ar), _colsum(ai)


def s5_fwd(proj, bmat, cmat, dskip, kconst):
    S = proj.shape[0]
    nb, cw, nst2 = bmat.shape
    nst = nst2 // 2

    def body(u_ref, b_ref, c_ref, d_ref, k_ref, z_ref, xsb_ref, xs):
        u = u_ref[...]
        xs[...] = jnp.dot(u.astype(BF16), b_ref[0], preferred_element_type=F32)
        _scan_fwd(xs, k_ref, nst)
        xsb = xs[...].astype(BF16)
        xsb_ref[...] = xsb
        y = jnp.dot(xsb, c_ref[0], preferred_element_type=F32) + d_ref[...] * u
        z_ref[...] = _gelu(y).astype(BF16)

    return pl.pallas_call(
        body, name="s5_fwd", grid=(nb,),
        in_specs=[pl.BlockSpec((S, cw), lambda b: (0, b)),
                  pl.BlockSpec((1, cw, nst2), lambda b: (b, 0, 0)),
                  pl.BlockSpec((1, nst2, cw), lambda b: (b, 0, 0)),
                  pl.BlockSpec((1, cw), lambda b: (0, b)),
                  pl.BlockSpec((16, SUBLANES, nst), lambda b: (0, 0, b))],
        out_specs=[pl.BlockSpec((S, cw), lambda b: (0, b)), pl.BlockSpec((S, nst2), lambda b: (0, b))],
        out_shape=[_sds((S, nb * cw), BF16), _sds((S, nb * nst2), BF16)],
        scratch_shapes=[pltpu.VMEM((S, nst2), F32)],
        compiler_params=_params(("arbitrary",)),
    )(proj, bmat, cmat, dskip, kconst)


def s5_bwd(proj, xsb_all, dz, bmat, cmat, dskip, kconst, after=()):
    S = proj.shape[0]
    nb, cw, nst2 = bmat.shape
    nst = nst2 // 2

    def body(u_ref, xsb_ref, dz_ref, b_ref, c_ref, d_ref, k_ref, du_ref, gb_ref, gc_ref, gd_ref,
             ga_ref, xs, g):
        u = u_ref[...]
        ub = u.astype(BF16)
        bm, cm, d = b_ref[0], c_ref[0], d_ref[...]
        xsb = xsb_ref[...]
        xs[...] = xsb.astype(F32)
        y = jnp.dot(xsb, cm, preferred_element_type=F32) + d * u
        dy = dz_ref[...].astype(F32) * _gelu_grad(y)
        gd_ref[...] = _colsum(dy * u)
        dyb = dy.astype(BF16)
        gc_ref[0] = lax.dot_general(xsb, dyb, (((0,), (0,)), ((), ())), preferred_element_type=F32)
        g[...] = lax.dot_general(dyb, cm, (((1,), (1,)), ((), ())), preferred_element_type=F32)
        ar, ai = _scan_bwd(g, xs, k_ref, nst)
        ga_ref[0, 0:1, :] = ar
        ga_ref[0, 1:2, :] = ai
        gb = g[...].astype(BF16)
        du = lax.dot_general(gb, bm, (((1,), (1,)), ((), ())), preferred_element_type=F32) + d * dy
        du_ref[...] = du.astype(BF16)
        gb_ref[0] = lax.dot_general(ub, gb, (((0,), (0,)), ((), ())), preferred_element_type=F32)

    return pl.pallas_call(
        _with_after(body, 7, after), name="s5_bwd", grid=(nb,),
        in_specs=[pl.BlockSpec((S, cw), lambda b: (0, b)),
                  pl.BlockSpec((S, nst2), lambda b: (0, b)),
                  pl.BlockSpec((S, cw), lambda b: (0, b)),
                  pl.BlockSpec((1, cw, nst2), lambda b: (b, 0, 0)),
                  pl.BlockSpec((1, nst2, cw), lambda b: (b, 0, 0)),
                  pl.BlockSpec((1, cw), lambda b: (0, b)),
                  pl.BlockSpec((16, SUBLANES, nst), lambda b: (0, 0, b))] + [ANY] * len(after),
        out_specs=[pl.BlockSpec((S, cw), lambda b: (0, b)),
                   pl.BlockSpec((1, cw, nst2), lambda b: (b, 0, 0)),
                   pl.BlockSpec((1, nst2, cw), lambda b: (b, 0, 0)),
                   pl.BlockSpec((1, cw), lambda b: (0, b)),
                   pl.BlockSpec((1, 2, nst), lambda b: (b, 0, 0))],
        out_shape=[_sds((S, nb * cw), BF16), _sds((nb, cw, nst2), F32), _sds((nb, nst2, cw), F32),
                   _sds((1, nb * cw), F32), _sds((nb, 2, nst), F32)],
        scratch_shapes=[pltpu.VMEM((S, nst2), F32), pltpu.VMEM((S, nst2), F32)],
        compiler_params=_params(("arbitrary",)),
    )(proj, xsb_all, dz, bmat, cmat, dskip, kconst, *after)


def _shift_rows(v, k, row, down):
    n = v.shape[0]
    if down:
        return jnp.where(row >= k, pltpu.roll(v, k, 0), 0.0)
    return jnp.where(row < n - k, pltpu.roll(v, n - k, 0), 0.0)


def _window(v, gi, row, down):
    sums = []
    s = v
    for k in (1, 2, 4, 8):
        s = s + _shift_rows(s, k, row, down)
        sums.append(s)
    out = sums[3]
    for n in (2, 1, 0):
        out = jnp.where(gi == n, sums[n], out)
    return out


def pool_fwd(proj, col0, width, gw):
    S = proj.shape[0]
    cb0 = col0 // gw

    def body(u_ref, o_ref):
        gi = pl.program_id(0)
        u = u_ref[...]
        row = lax.broadcasted_iota(jnp.int32, u.shape, 0)
        w = jnp.left_shift(2, gi)
        count = jnp.minimum(row + 1, w).astype(F32)
        o_ref[...] = (_window(u, gi, row, True) / count - u).astype(BF16)

    return pl.pallas_call(
        body, name="pool_fwd", grid=(len(POOL_WINDOWS),),
        in_specs=[pl.BlockSpec((S, gw), lambda g: (0, cb0 + g))],
        out_specs=pl.BlockSpec((S, gw), lambda g: (0, g)),
        out_shape=_sds((S, width), BF16), compiler_params=_params(("arbitrary",)),
    )(proj)


def pool_bwd(dpooled, gw):
    S, width = dpooled.shape

    def body(d_ref, o_ref):
        gi = pl.program_id(0)
        d = d_ref[...]
        row = lax.broadcasted_iota(jnp.int32, d.shape, 0)
        w = jnp.left_shift(2, gi)
        count = jnp.minimum(row + 1, w).astype(F32)
        o_ref[...] = (_window(d / count, gi, row, False) - d).astype(BF16)

    return pl.pallas_call(
        body, name="pool_bwd", grid=(len(POOL_WINDOWS),),
        in_specs=[pl.BlockSpec((S, gw), lambda g: (0, g))],
        out_specs=pl.BlockSpec((S, gw), lambda g: (0, g)),
        out_shape=_sds((S, width), BF16), compiler_params=_params(("arbitrary",)),
    )(dpooled)


def _place():
    x, y, c = lax.axis_index("x"), lax.axis_index("y"), lax.axis_index("c")
    chips = [(1 - x, y), (x, 1 - y), (1 - x, 1 - y)]
    return x, y, c, chips


HBM = pl.BlockSpec(memory_space=pltpu.HBM)


def _gather_body(n, handshake):
    def body(*refs):
        ins, outs = refs[:n], refs[n:2 * n]
        send_sems, recv_sems, local_sems = refs[2 * n:]
        x, y, c, chips = _place()
        if handshake:
            barrier = pltpu.get_barrier_semaphore()
            for peer in [(x, y, 1 - c)] + [(*chip, c) for chip in chips]:
                pl.semaphore_signal(barrier, inc=1, device_id=peer, device_id_type=MESH)
            pl.semaphore_wait(barrier, 4)
        me, sibling = (x, y, c), (x, y, 1 - c)

        def slot(i, p):
            return outs[i].at[4 * p[0] + 2 * p[1] + p[2]]

        def copy(i, k, block, to, src=None):
            return pltpu.make_async_remote_copy(
                src_ref=slot(i, block) if src is None else src, dst_ref=slot(i, block),
                send_sem=send_sems.at[i, k], recv_sem=recv_sems.at[i, k],
                device_id=to, device_id_type=MESH)

        started = []
        for i in range(n):
            for j, chip in enumerate(chips):
                started.append(copy(i, 1 + j, me, (*chip, c), src=ins[i]))
                started[-1].start()
        for i in range(n):
            started.append(copy(i, 0, me, sibling, src=ins[i]))
            started[-1].start()
        mine = [pltpu.make_async_copy(ins[i], slot(i, me), local_sems.at[i]) for i in range(n)]
        for cp in mine:
            cp.start()
        for i in range(n):
            for j, chip in enumerate(chips):
                copy(i, 1 + j, (*chip, c), me).wait_recv()
                started.append(copy(i, 4 + j, (*chip, c), sibling))
                started[-1].start()
        for i in range(n):
            copy(i, 0, sibling, me).wait_recv()
            for j, chip in enumerate(chips):
                copy(i, 4 + j, (*chip, 1 - c), me).wait_recv()
        for cp in started:
            cp.wait_send()
        for cp in mine:
            cp.wait()

    return body


def _routed_gather_body(n):
    def body(*refs):
        ins, outs = refs[:n], refs[n:2 * n]
        send_sems, recv_sems, local_sems = refs[2 * n:]
        x, y, c, (xn, yn, dg) = _place()
        me, sibling = (x, y, c), (x, y, 1 - c)
        barrier = pltpu.get_barrier_semaphore()
        for peer in (sibling, (*xn, c), (*yn, c)):
            pl.semaphore_signal(barrier, inc=1, device_id=peer, device_id_type=MESH)
        pl.semaphore_wait(barrier, 3)

        def piece(i, p, h):
            rows = ins[i].shape[0] // 2
            return outs[i].at[4 * p[0] + 2 * p[1] + p[2], pl.ds(h * rows, rows)]

        def copy(i, k, src, dst, to):
            return pltpu.make_async_remote_copy(src_ref=src, dst_ref=dst, send_sem=send_sems.at[i, k],
                                                recv_sem=recv_sems.at[i, k], device_id=to,
                                                device_id_type=MESH)

        started = []

        def go(cp):
            cp.start()
            started.append(cp)

        for i in range(n):
            rows = ins[i].shape[0] // 2
            for h in range(2):
                own = ins[i].at[pl.ds(h * rows, rows)]
                go(copy(i, 1 + h, own, piece(i, me, h), (*xn, c)))
                go(copy(i, 3 + h, own, piece(i, me, h), (*yn, c)))
        for i in range(n):
            go(copy(i, 0, ins[i], outs[i].at[4 * x + 2 * y + c], sibling))
        mine = [pltpu.make_async_copy(ins[i], outs[i].at[4 * x + 2 * y + c], local_sems.at[i])
                for i in range(n)]
        for cp in mine:
            cp.start()
        for i in range(n):
            for k, chip, h, onward, ksib in ((1, xn, 0, (5, yn), 7), (4, yn, 1, (6, xn), 10),
                                            (2, xn, 1, None, 8), (3, yn, 0, None, 9),
                                            (5, dg, 0, None, 11), (6, dg, 1, None, 12)):
                got = piece(i, (*chip, c), h)
                copy(i, k, got, got, me).wait_recv()
                if onward is not None:
                    go(copy(i, onward[0], got, got, (*onward[1], c)))
                go(copy(i, ksib, got, got, sibling))
        for i in range(n):
            block = outs[i].at[4 * x + 2 * y + 1 - c]
            copy(i, 0, block, block, me).wait_recv()
            for ksib, chip, h in ((7, xn, 0), (10, yn, 1), (8, xn, 1), (9, yn, 0), (11, dg, 0), (12, dg, 1)):
                got = piece(i, (*chip, 1 - c), h)
                copy(i, ksib, got, got, me).wait_recv()
        for cp in started:
            cp.wait_send()
        for cp in mine:
            cp.wait()

    return body


def _on_sequencer(name, body, arrays, out_sds, sems, collective_id):
    ins = [jax.new_ref(a, memory_space=pltpu.MemorySpace.HBM) for a in arrays]
    outs = [jax.empty_ref(s, memory_space=pltpu.MemorySpace.HBM) for s in out_sds]

    @pl.kernel(mesh=plsc.ScalarSubcoreMesh(axis_name="sequencer", num_cores=1), name=name,
               scratch_types=tuple(sems),
               compiler_params=pltpu.CompilerParams(collective_id=collective_id))
    def launch(*sem_refs):
        body(*ins, *outs, *sem_refs)

    launch()
    return [o[...] for o in outs]


def seq_all_gather(name, shards, collective_id, routed=True):
    n = len(shards)
    nsem = 13 if routed else 7
    return _on_sequencer(
        name, _routed_gather_body(n) if routed else _gather_body(n, True), shards,
        [_sds((NDEV,) + s.shape, s.dtype) for s in shards],
        [pltpu.SemaphoreType.DMA((n, nsem)), pltpu.SemaphoreType.DMA((n, nsem)),
         pltpu.SemaphoreType.DMA((n,))], collective_id)


def pair_exchange(name, grads, collective_id):
    def plan(srcs, lands):
        x, y, c, _ = _place()
        return ([(i, q, srcs[i].at[2 * q + 1 - c], lands[i].at[q], (x, y, 1 - c))
                 for i in range(len(srcs)) for q in range(NCHIP)], [(x, y, 1 - c)])

    return _split_exchange(name, grads, [_sds((NCHIP,) + g.shape[1:], g.dtype) for g in grads],
                           plan, NCHIP, collective_id)


SEM = pl.BlockSpec(memory_space=pltpu.SEMAPHORE)


def _split_exchange(name, srcs, land_sds, plan, ncopy, collective_id):
    n = len(srcs)
    nsem = n * ncopy
    effect = pltpu.SideEffectType.DATAFLOW_SIDE_EFFECTING

    def descriptors(src_refs, land_refs, send_sems, recv_sems):
        copies, peers = plan(src_refs, land_refs)
        return [pltpu.make_async_remote_copy(src_ref=s, dst_ref=d, send_sem=send_sems[i * ncopy + k],
                                             recv_sem=recv_sems[i * ncopy + k], device_id=to,
                                             device_id_type=MESH) for (i, k, s, d, to) in copies], peers

    def start_body(*refs):
        src_refs, land_refs = refs[:n], refs[n:2 * n]
        send_sems, recv_sems = refs[2 * n:2 * n + nsem], refs[2 * n + nsem:2 * n + 2 * nsem]
        token = refs[-1]
        cps, peers = descriptors(src_refs, land_refs, send_sems, recv_sems)
        barrier = pltpu.get_barrier_semaphore()
        for peer in peers:
            pl.semaphore_signal(barrier, inc=1, device_id=peer, device_id_type=MESH)
        pl.semaphore_wait(barrier, len(peers))
        for cp in cps:
            cp.start()
        token[...] = jnp.zeros_like(token)

    lands = [pltpu.with_memory_space_constraint(lax.empty(s.shape, s.dtype), pltpu.HBM) for s in land_sds]
    srcs = [pltpu.with_memory_space_constraint(s, pltpu.HBM) for s in srcs]
    res = pl.pallas_call(
        start_body, name=name + "_start",
        out_shape=(pltpu.SemaphoreType.DMA(()),) * (2 * nsem)
        + tuple(pltpu.HBM(s.shape, s.dtype) for s in srcs)
        + tuple(pltpu.HBM(s.shape, s.dtype) for s in land_sds) + (_sds((SUBLANES, LANES), F32),),
        in_specs=[HBM] * (2 * n),
        out_specs=(SEM,) * (2 * nsem) + (HBM,) * (2 * n) + (pl.BlockSpec(memory_space=pltpu.VMEM),),
        input_output_aliases={i: 2 * nsem + i for i in range(2 * n)},
        compiler_params=pltpu.CompilerParams(has_side_effects=effect, collective_id=collective_id),
    )(*srcs, *lands)
    sems = res[:2 * nsem]
    thru = res[2 * nsem:2 * nsem + 2 * n]
    token = res[-1]

    def wait(after):
        def wait_body(*refs):
            src_refs, land_refs = refs[:n], refs[n:2 * n]
            cps, _ = descriptors(src_refs, land_refs, refs[2 * n:2 * n + nsem],
                                 refs[2 * n + nsem:2 * n + 2 * nsem])
            for cp in cps:
                cp.wait_send()
            for cp in cps:
                cp.wait_recv()

        out = pl.pallas_call(
            wait_body, name=name + "_wait",
            out_shape=tuple(pltpu.HBM(s.shape, s.dtype) for s in srcs)
            + tuple(pltpu.HBM(s.shape, s.dtype) for s in land_sds),
            in_specs=[HBM] * (2 * n) + [SEM] * (2 * nsem) + [pl.BlockSpec(memory_space=pl.ANY)],
            out_specs=(HBM,) * (2 * n),
            input_output_aliases={i: i for i in range(2 * n)},
            compiler_params=pltpu.CompilerParams(has_side_effects=effect),
        )(*thru, *sems, after)
        return list(out[:n]), list(out[n:])

    return token, wait


def pair_sum(name, grad, got, place):
    shp = grad.shape[1:]
    r, cdim = shp[-2], shp[-1]
    lead = int(math.prod(shp[:-2])) if len(shp) > 2 else 1
    g5 = grad.reshape(NCHIP, 2, lead * r, cdim)
    t4 = got.reshape(NCHIP, lead * r, cdim)
    R = lead * r
    tr = _tile(R, max(8, (1 << 20) // cdim))

    def body(p_ref, g_ref, t_ref, o_ref):
        o_ref[...] = (g_ref[0].astype(F32) + t_ref[...].astype(F32)).astype(o_ref.dtype)

    out = pl.pallas_call(
        body, name=name,
        grid_spec=pltpu.PrefetchScalarGridSpec(
            num_scalar_prefetch=1, grid=(NCHIP - 1, R // tr),
            in_specs=[pl.BlockSpec((1, 1, tr, cdim), lambda j, i, p: (p[1] ^ (j + 1), p[0], i, 0)),
                      pl.BlockSpec((1, tr, cdim), lambda j, i, p: (p[1] ^ (j + 1), i, 0))],
            out_specs=pl.BlockSpec((1, tr, cdim), lambda j, i, p: (p[1] ^ (j + 1), i, 0))),
        out_shape=_sds((NCHIP, R, cdim), grad.dtype),
        compiler_params=_params(("parallel", "parallel")),
    )(place, g5, t4)
    return out


def chip_exchange(name, parts, collective_id):
    def plan(srcs, lands):
        x, y, c, chips = _place()
        return ([(i, j, srcs[i].at[2 * chip[0] + chip[1]], lands[i].at[j], (*chip, c))
                 for i in range(len(srcs)) for j, chip in enumerate(chips)],
                [(*chip, c) for chip in chips])

    return _split_exchange(name, parts, [_sds((3,) + p.shape[1:], p.dtype) for p in parts],
                           plan, 3, collective_id)


def ada_fwd(c_row, w_ada, b_ada):
    D, cols = w_ada.shape

    def body(c_ref, w_ref, b_ref, mod_ref, call_ref, act8, part, s1, r1, s2, r2):
        x, y, c, _ = _place()
        me = 4 * x + 2 * y + c
        call_ref[me] = c_ref[...]
        cps = []
        for k in range(1, NDEV):
            to = (x ^ (k >> 2), y ^ ((k >> 1) & 1), c ^ (k & 1))
            cps.append(pltpu.make_async_remote_copy(
                src_ref=c_ref, dst_ref=call_ref.at[me], send_sem=s1.at[k - 1],
                recv_sem=r1.at[k - 1], device_id=to, device_id_type=MESH))
            cps[-1].start()
        for cp in cps:
            cp.wait()
        for b in range(NDEV):
            act8[b:b + 1, :] = call_ref[b]
        cv = act8[...]
        act = (cv * _sigmoid(cv)).astype(BF16)
        res = jnp.dot(act, w_ref[...].astype(BF16), preferred_element_type=F32)
        for b in range(NDEV):
            part[b] = res[b:b + 1, :]
        mod_ref[me] = part[me]
        cps = []
        for k in range(1, NDEV):
            to = (x ^ (k >> 2), y ^ ((k >> 1) & 1), c ^ (k & 1))
            dst = 4 * to[0] + 2 * to[1] + to[2]
            cps.append(pltpu.make_async_remote_copy(
                src_ref=part.at[dst], dst_ref=mod_ref.at[me], send_sem=s2.at[k - 1],
                recv_sem=r2.at[k - 1], device_id=to, device_id_type=MESH))
            cps[-1].start()
        for cp in cps:
            cp.wait()
        for b in range(NDEV):
            mod_ref[b] = mod_ref[b] + b_ref[b]

    vm = pl.BlockSpec(memory_space=pltpu.VMEM)
    return pl.pallas_call(
        body, name="ada_fwd", in_specs=[vm, vm, vm], out_specs=[vm, vm],
        out_shape=[_sds((NDEV, 1, cols), F32), _sds((NDEV, 1, D), F32)],
        scratch_shapes=[pltpu.VMEM((NDEV, D), F32), pltpu.VMEM((NDEV, 1, cols), F32),
                        pltpu.SemaphoreType.DMA((NDEV - 1,)), pltpu.SemaphoreType.DMA((NDEV - 1,)),
                        pltpu.SemaphoreType.DMA((NDEV - 1,)), pltpu.SemaphoreType.DMA((NDEV - 1,))],
        compiler_params=pltpu.CompilerParams(vmem_limit_bytes=VMEM_LIMIT),
    )(c_row, w_ada, b_ada.reshape(NDEV, 1, cols))


def _adamw_math(g, w, m, v):
    m2 = ADAM_B1 * m + (1.0 - ADAM_B1) * g
    v2 = ADAM_B2 * v + (1.0 - ADAM_B2) * (g * g)
    m_hat = m2 / (1.0 - ADAM_B1 ** ADAM_STEP)
    v_hat = v2 / (1.0 - ADAM_B2 ** ADAM_STEP)
    delta = -ADAM_LR * (m_hat / (jnp.sqrt(v_hat) + ADAM_EPS) + ADAM_WD * w)
    return delta, m2, v2


def adamw_sharded(name, grad8, pair4, got3, w, m, v, place):
    shape = w.shape
    cdim = shape[-1]
    R = int(math.prod(shape[:-1]))
    w2, m2, v2 = (t.reshape(R, cdim) for t in (w, m, v))
    tr = _tile(R, max(8, (1 << 19) // cdim))

    def body(q_ref, own_ref, sib_ref, t_ref, w_ref, m_ref, v_ref, g_out, d_out, m_out, v_out):
        g = own_ref[0].astype(F32) + sib_ref[0].astype(F32)
        for j in range(3):
            g = g + t_ref[j].astype(F32)
        d, mn, vn = _adamw_math(g, w_ref[...], m_ref[...], v_ref[...])
        g_out[...] = g
        d_out[...] = d
        m_out[...] = mn
        v_out[...] = vn

    spec = pl.BlockSpec((tr, cdim), lambda i, qr: (i, 0))
    outs = pl.pallas_call(
        body, name=name,
        grid_spec=pltpu.PrefetchScalarGridSpec(
            num_scalar_prefetch=1, grid=(R // tr,),
            in_specs=[pl.BlockSpec((1, tr, cdim), lambda i, qr: (qr[2], i, 0)),
                      pl.BlockSpec((1, tr, cdim), lambda i, qr: (qr[1], i, 0)),
                      pl.BlockSpec((3, tr, cdim), lambda i, qr: (0, i, 0)), spec, spec, spec],
            out_specs=[spec] * 4),
        out_shape=[_sds((R, cdim), F32)] * 4,
        compiler_params=_params(("parallel",)),
    )(place, grad8.reshape(NDEV, R, cdim), pair4.reshape(NCHIP, R, cdim),
      got3.reshape(3, R, cdim), w2, m2, v2)
    return [o.reshape(shape) for o in outs]


def adamw_small(parts, w, m, v, after=()):
    R = w.shape[0]
    tr = R

    def body(p_ref, w_ref, m_ref, v_ref, g_out, d_out, m_out, v_out):
        g = p_ref[0]
        for j in range(1, NDEV):
            g = g + p_ref[j]
        d, mn, vn = _adamw_math(g, w_ref[...], m_ref[...], v_ref[...])
        g_out[...] = g
        d_out[...] = d
        m_out[...] = mn
        v_out[...] = vn

    spec = pl.BlockSpec((tr, LANES), lambda i: (i, 0))
    return pl.pallas_call(
        _with_after(body, 4, after), name="adamw_small", grid=(R // tr,),
        in_specs=[pl.BlockSpec((NDEV, tr, LANES), lambda i: (0, i, 0)), spec, spec, spec]
        + [ANY] * len(after),
        out_specs=[spec] * 4, out_shape=[_sds((R, LANES), F32)] * 4,
        compiler_params=_params(("parallel",)),
    )(parts, w, m, v, *after)


def adamw_ada(c_all_t, dmod_all, w, m, v, my_dev):
    D, cols = w.shape
    tr = _tile(D, 256)

    def body(k_ref, c_ref, d_ref, w_ref, m_ref, v_ref, g_out, d_out, m_out, v_out):
        cv = c_ref[...]
        act = cv * _sigmoid(cv)
        dm = d_ref[...]
        g = act[:, 0:1] * dm[0:1, :]
        for b in range(1, NDEV):
            g = g + act[:, b:b + 1] * dm[b:b + 1, :]
        d, mn, vn = _adamw_math(g, w_ref[...], m_ref[...], v_ref[...])
        g_out[...] = g
        d_out[...] = d
        m_out[...] = mn
        v_out[...] = vn

    spec = pl.BlockSpec((tr, cols), lambda i, kr: (i, 0))
    return pl.pallas_call(
        body, name="adamw_ada",
        grid_spec=pltpu.PrefetchScalarGridSpec(
            num_scalar_prefetch=1, grid=(D // tr,),
            in_specs=[pl.BlockSpec((tr, NDEV), lambda i, kr: (i, 0)),
                      pl.BlockSpec((NDEV, cols), lambda i, kr: (0, kr[0])), spec, spec, spec],
            out_specs=[spec] * 4),
        out_shape=[_sds((D, cols), F32)] * 4,
        compiler_params=_params(("parallel",)),
    )(my_dev, c_all_t, dmod_all, w, m, v)


def _blockdiag(t, eye):
    nb, gpb, R, C = t.shape
    return jnp.einsum("bgrc,gk->bgrkc", t, eye).reshape(nb, gpb * R, gpb * C)


def _diag_blocks(t, gpb, R, C):
    nb = t.shape[0]
    t5 = t.reshape(nb, gpb, R, gpb, C)
    idx = jnp.arange(gpb)
    return jnp.moveaxis(t5[:, idx, :, idx, :], 0, 1)


def _small_pack(parts):
    rows = []
    for p in parts:
        flat = p.reshape(-1)
        flat = jnp.pad(flat, (0, (-flat.shape[0]) % (SUBLANES * LANES)))
        rows.append(flat.reshape(-1, LANES))
    return jnp.concatenate(rows, axis=0)


def _small_unpack(buf, shapes):
    out, r = [], 0
    for s in shapes:
        n = int(math.prod(s))
        nr = -(-n // (SUBLANES * LANES)) * SUBLANES
        out.append(buf[r:r + nr].reshape(-1)[:n].reshape(s))
        r += nr
    return out


def kernel(x, c, w_ada, b_ada, w_in, lam_re, lam_im, log_dt, ssm_b_re, ssm_b_im, ssm_c_re, ssm_c_im, ssm_d, w_glu_val, w_glu_gate, w_pool, pool_scale, w_pool_out, w_out, ln1_g, ln1_b, w_ff1, w_ff2, ln2_g, ln2_b, loss_target, m_w_ada, m_b_ada, m_w_in, m_lam_re, m_lam_im, m_log_dt, m_ssm_b_re, m_ssm_b_im, m_ssm_c_re, m_ssm_c_im, m_ssm_d, m_w_glu_val, m_w_glu_gate, m_w_pool, m_pool_scale, m_w_pool_out, m_w_out, m_ln1_g, m_ln1_b, m_w_ff1, m_w_ff2, m_ln2_g, m_ln2_b, v_w_ada, v_b_ada, v_w_in, v_lam_re, v_lam_im, v_log_dt, v_ssm_b_re, v_ssm_b_im, v_ssm_c_re, v_ssm_c_im, v_ssm_d, v_w_glu_val, v_w_glu_gate, v_w_pool, v_pool_scale, v_w_pool_out, v_w_out, v_ln1_g, v_ln1_b, v_w_ff1, v_w_ff2, v_ln2_g, v_ln2_b):
    S, D = x.shape[1], x.shape[2]
    x2d, tgt = x[0], loss_target[0]
    W = D // 2
    G = W // SSM_GROUP
    P, H, GPB = SSM_STATE, SSM_GROUP, GROUPS_PER_BLOCK
    nblk = G // GPB
    gw = W // len(POOL_WINDOWS)
    ax, ay, ac = lax.axis_index("x"), lax.axis_index("y"), lax.axis_index("c")
    my_dev = (4 * ax + 2 * ay + ac).astype(jnp.int32).reshape(1)
    place = jnp.stack([ac, 2 * ax + ay, 4 * ax + 2 * ay + ac]).astype(jnp.int32)
    ts = _tile(S, 256)

    glu = jnp.stack([w_glu_val[0], w_glu_gate[0]]).astype(BF16)
    shards = [w_in[0].astype(BF16), glu, w_pool[0].astype(BF16), w_pool_out[0].astype(BF16),
              w_out[0].astype(BF16), w_ff1[0].astype(BF16), w_ff2[0].astype(BF16)]
    wg_in, wg_pool = seq_all_gather("gather_w_in", [shards[0], shards[2]], 1)
    wg_vg, wg_po, wg_out = seq_all_gather("gather_w_mix", [shards[1], shards[3], shards[4]], 2)
    wg_ff1, wg_ff2 = seq_all_gather("gather_w_ff", shards[5:7], 3)
    wg_vg = wg_vg.reshape(2 * NDEV, W, D // NDEV)
    nwin = len(POOL_WINDOWS)
    wp_full = jnp.transpose(wg_pool, (1, 0, 2, 3)).reshape(nwin, gw, gw)
    wout_full = wg_out.reshape(1, D, D)
    wff2_full = wg_ff2.reshape(1, 4 * D, D)

    small_names = [b_ada, lam_re, lam_im, log_dt, ssm_b_re, ssm_b_im, ssm_c_re, ssm_c_im, ssm_d,
                   pool_scale, ln1_g, ln1_b, ln2_g, ln2_b]
    small_m = [m_b_ada, m_lam_re, m_lam_im, m_log_dt, m_ssm_b_re, m_ssm_b_im, m_ssm_c_re, m_ssm_c_im,
               m_ssm_d, m_pool_scale, m_ln1_g, m_ln1_b, m_ln2_g, m_ln2_b]
    small_v = [v_b_ada, v_lam_re, v_lam_im, v_log_dt, v_ssm_b_re, v_ssm_b_im, v_ssm_c_re, v_ssm_c_im,
               v_ssm_d, v_pool_scale, v_ln1_g, v_ln1_b, v_ln2_g, v_ln2_b]
    zero_row = jnp.zeros((1, LANES), F32)
    packed_wmv = [_small_pack(t + [zero_row]) for t in (small_names, small_m, small_v)]

    mod, c_all = ada_fwd(c, w_ada[0], b_ada)
    mod = mod.reshape(6, 1, D)
    sh1, sc1, g1, sh2, sc2, g2 = (mod[i] for i in range(6))

    f2, kconst = s5_disc(lam_re[0], lam_im[0], log_dt[0].reshape(G, 1))
    kconst = kconst.reshape(16, SUBLANES, G * P)
    f2r = f2.reshape(2, 1, G * P)
    bt_re = jnp.transpose(ssm_b_re[0], (2, 0, 1)).reshape(H, G * P)
    bt_im = jnp.transpose(ssm_b_im[0], (2, 0, 1)).reshape(H, G * P)
    bbar = s5_bbar(f2r, bt_re, bt_im)
    eye = jnp.eye(GPB, dtype=F32)
    bb4 = jnp.transpose(bbar.reshape(2, H, nblk, GPB, P), (0, 2, 3, 1, 4))
    bmat = jnp.concatenate([_blockdiag(bb4[0], eye), _blockdiag(bb4[1], eye)], axis=2).astype(BF16)
    c4_re = jnp.transpose(ssm_c_re[0].reshape(nblk, GPB, H, P), (0, 1, 3, 2))
    c4_im = jnp.transpose(ssm_c_im[0].reshape(nblk, GPB, H, P), (0, 1, 3, 2))
    cmat = jnp.concatenate([_blockdiag(c4_re, eye), -_blockdiag(c4_im, eye)], axis=1).astype(BF16)

    def e1(t, b):
        xhat, _ = _ln_stats(t[0])
        return [xhat * (1.0 + b[0]) + b[1]], []
    (h1,) = _rowwise("ln_mod1", e1, S, ts, [(x2d, D, 0)], [sc1, sh1], [(D, BF16)], [])

    (proj,) = mm_nn("proj", h1, wg_in, F32, 1)
    z, xsb_all = s5_fwd(proj, bmat, cmat, ssm_d, kconst)
    (vt,) = mm_nn("glu", z, wg_vg, BF16, 4)
    pooled = pool_fwd(proj, W, W, gw)

    def pool_epi(vals, ex, outs):
        a = vals[0]
        outs[0][...] = a
        outs[1][...] = (a * ex[0][...]).astype(BF16)
    tmp = _tile(S, 1024)
    yp, ypool = _mm(
        "pool_mix", "nn", pooled, wp_full.astype(BF16), (S // tmp, nwin, 1),
        pl.BlockSpec((tmp, gw), lambda i, j, k: (i, j)), pl.BlockSpec((1, gw, gw), lambda i, j, k: (j, 0, 0)),
        [(_sds((S, W), F32), pl.BlockSpec((tmp, gw), lambda i, j, k: (i, j))),
         (_sds((S, W), BF16), pl.BlockSpec((tmp, gw), lambda i, j, k: (i, j)))],
        (tmp, gw), 1, gw, None, pool_epi,
        [(pool_scale, pl.BlockSpec((1, gw), lambda i, j, k: (0, j)))])
    (y_b,) = mm_nn("pool_out", ypool, wg_po, BF16, 4)

    cb = D // NDEV
    ga_cb, gb_cb = (2 * W) // cb, (2 * W + D) // cb
    tsm = _tile(S, 512)

    def merge_call(name, fn, ins, n_out, after=()):
        def body(*refs):
            vals = [r[...].astype(F32) for r in refs[:len(ins)]]
            for r, v in zip(refs[len(ins):], fn(*vals)):
                r[...] = v.astype(r.dtype)
        return pl.pallas_call(
            _with_after(body, len(ins), after), name=name, grid=(S // tsm, NDEV),
            in_specs=[pl.BlockSpec((tsm, w), f) for (_, w, f) in ins] + [ANY] * len(after),
            out_specs=[pl.BlockSpec((tsm, w), lambda i, j: (i, j)) for (_, w) in n_out],
            out_shape=[_sds((S, cols), BF16) for (cols, _) in n_out],
            compiler_params=_params(("parallel", "parallel")),
        )(*[a for (a, _, _) in ins], *after)

    merge_ins = [(proj, cb, lambda i, j: (i, ga_cb + j)), (proj, cb, lambda i, j: (i, gb_cb + j)),
                 (vt, 2 * cb, lambda i, j: (i, j)), (y_b, cb, lambda i, j: (i, j))]

    def merge_f(ga, gb, vtv, yb):
        return [_sigmoid(ga) * (vtv[:, :cb] * _sigmoid(vtv[:, cb:])) + _sigmoid(gb) * yb]
    (merged,) = merge_call("merge", merge_f, merge_ins, [(D, cb)])

    (mix,) = mm_nn("mix_out", merged, wout_full, F32, 1)

    def e3(t, b):
        xv, mx = t
        g1v, l1g, l1b, sc2v, sh2v = b
        r1 = ALPHA * xv + g1v * mx
        xh1, _ = _ln_stats(r1)
        x1 = xh1 * l1g + l1b
        xh, _ = _ln_stats(x1)
        return [r1, xh * (1.0 + sc2v) + sh2v], []
    r1, h2 = _rowwise("post_mix", e3, S, ts, [(x2d, D, 0), (mix, D, 0)],
                      [g1, ln1_g, ln1_b, sc2, sh2], [(D, F32), (D, BF16)], [], after=packed_wmv)

    def relu_epi(vals, ex, outs):
        outs[0][...] = jnp.maximum(vals[0], 0.0).astype(BF16)
    (rl,) = mm_nn("ff1", h2, wg_ff1, BF16, 1, epi=relu_epi)

    def square(a):
        return a * a
    (y2,) = mm_nn("ff2", rl, wff2_full, F32, 1, pro=square)

    def e4(t, b):
        r1v, y2v, tg = t
        g2v, l1g, l1b, l2g, l2b = b
        xh1, _ = _ln_stats(r1v)
        x1 = xh1 * l1g + l1b
        r2 = ALPHA * x1 + g2v * y2v
        xh2, rs2 = _ln_stats(r2)
        err = xh2 * l2g + l2b - tg
        dx2 = err * (1.0 / D)
        dr2 = _ln_bwd(dx2 * l2g, xh2, rs2)
        lsum = jnp.sum(_colsum(err * err), axis=1, keepdims=True) * (0.5 / D)
        return ([ALPHA * dr2, g2v * dr2],
                [jnp.broadcast_to(lsum, (1, LANES)), _colsum(dx2 * xh2), _colsum(dx2), _colsum(dr2 * y2v)])
    dx1a, dy2, loss_acc, g_ln2g, g_ln2b, d_g2 = _rowwise(
        "head", e4, S, ts, [(r1, D, 0), (y2, D, 0), (tgt, D, 0)], [g2, ln1_g, ln1_b, ln2_g, ln2_b],
        [(D, F32), (D, BF16)], [LANES, D, D, D])

    tn_ff = _tile(4 * D, 1024)

    def dff_epi(vals, ex, outs):
        outs[0][...] = (vals[0] * (2.0 * ex[0][...].astype(F32))).astype(BF16)
    tmf = _tile(S, 1024)
    (da1,) = mm_nt("d_ff2", dy2, wff2_full, BF16, 1, tn=tn_ff, epi=dff_epi,
                   extras=[(rl, pl.BlockSpec((tmf, tn_ff), lambda i, j, k: (i, j)))])
    gw_ff2 = mm_tn("gw_ff2", rl, dy2, BF16, NDEV, 0, pro=square)
    gw_ff1 = mm_tn("gw_ff1", h2, da1, BF16, NDEV, 1)
    tok, wait_pair_a = pair_exchange("pair_exchange_ff", [gw_ff2, gw_ff1], 4)
    (dh2,) = mm_nt("d_ff1", da1, wg_ff1, F32, 2, after=[tok])

    def e5(t, b):
        dh2v, r1v, dx1av, mx = t
        sc2v, l1g, l1b, g1v = b
        xh1, rs1 = _ln_stats(r1v)
        x1 = xh1 * l1g + l1b
        xh, rs = _ln_stats(x1)
        dx1 = dx1av + _ln_bwd(dh2v * (1.0 + sc2v), xh, rs)
        dr1 = _ln_bwd(dx1 * l1g, xh1, rs1)
        return ([ALPHA * dr1, g1v * dr1],
                [_colsum(dh2v * xh), _colsum(dh2v), _colsum(dx1 * xh1), _colsum(dx1), _colsum(dr1 * mx)])
    dxa, dmix, d_sc2, d_sh2, g_ln1g, g_ln1b, d_g1 = _rowwise(
        "post_mix_bwd", e5, S, ts, [(dh2, D, 0), (r1, D, 0), (dx1a, D, 0), (mix, D, 0)],
        [sc2, ln1_g, ln1_b, g1], [(D, F32), (D, BF16)], [D, D, D, D, D])

    (dmerged,) = mm_nt("d_mix_out", dmix, wout_full, BF16, 1)
    gw_out = mm_tn("gw_out", merged, dmix, BF16, NDEV, 0)
    grads_a, got_a = wait_pair_a(gw_out)
    parts_a = [pair_sum("pair_sum_ff%d" % i, g, t, place) for i, (g, t) in enumerate(zip(grads_a, got_a))]
    tok, wait_chip_a = chip_exchange("chip_exchange_ff", parts_a, 5)

    def merge_b(ga, gb, vtv, yb, dm):
        vv, tt = vtv[:, :cb], vtv[:, cb:]
        sa, sb, st = _sigmoid(ga), _sigmoid(gb), _sigmoid(tt)
        dya = dm * sa
        return [dm * (vv * st) * sa * (1.0 - sa), dm * yb * sb * (1.0 - sb),
                jnp.concatenate([dya * st, dya * vv * st * (1.0 - st)], axis=1), dm * sb]
    dga, dgb_, dvt, dy_b = merge_call(
        "merge_bwd", merge_b, merge_ins + [(dmerged, cb, lambda i, j: (i, j))],
        [(D, cb), (D, cb), (2 * D, 2 * cb), (D, cb)], after=[tok])

    (dypool,) = mm_nt("d_pool_out", dy_b, wg_po, F32, NDEV)
    gw_po = mm_tn("gw_pool_out", ypool, dy_b, BF16, NDEV, 4)

    def e7(t, b):
        return [t[0] * b[0]], [_colsum(t[0] * t[1])]
    dyp, g_pscale = _rowwise("pool_scale_bwd", e7, S, ts, [(dypool, W, 0), (yp, W, 0)],
                             [pool_scale], [(W, BF16)], [W])
    (dpooled,) = _mm(
        "d_pool_mix", "nt", dyp, wp_full.astype(BF16), (S // tmp, nwin, 1),
        pl.BlockSpec((tmp, gw), lambda i, j, k: (i, j)), pl.BlockSpec((1, gw, gw), lambda i, j, k: (j, 0, 0)),
        [(_sds((S, W), F32), pl.BlockSpec((tmp, gw), lambda i, j, k: (i, j)))], (tmp, gw), 1, gw)
    tkp = _tile(S, 2048)
    gw_pool = _mm(
        "gw_pool", "tn", pooled, dyp, (nwin, 1, S // tkp),
        pl.BlockSpec((tkp, gw), lambda i, j, k: (k, i)), pl.BlockSpec((tkp, gw), lambda i, j, k: (k, i)),
        [(_sds((nwin, gw, gw), BF16), pl.BlockSpec((1, gw, gw), lambda i, j, k: (i, 0, 0)))],
        (gw, gw), 1, gw, stacked_out=True)[0]
    du_pool = pool_bwd(dpooled, gw)

    (dz,) = mm_nt("d_glu", dvt, wg_vg, BF16, NDEV)
    gw_vg = mm_tn("gw_glu", z, dvt, BF16, 2 * NDEV, 4)
    gw_pool_st = jnp.transpose(gw_pool.reshape(nwin, NDEV, gw // NDEV, gw), (1, 0, 2, 3))
    grads_b = [gw_out, gw_po, gw_pool_st, gw_vg.reshape(NDEV, 2, W, D // NDEV)]
    tok, wait_pair_b = pair_exchange("pair_exchange_mix", grads_b, 6)
    du_ssm, g_bmat, g_cmat, g_d, g_a = s5_bwd(proj, xsb_all, dz, bmat, cmat, ssm_d, kconst, after=[tok])
    grads_b, got_b = wait_pair_b(du_ssm)
    parts_b = [pair_sum("pair_sum_mix%d" % i, g, t, place) for i, (g, t) in enumerate(zip(grads_b, got_b))]
    tok, wait_chip_b = chip_exchange("chip_exchange_mix", parts_b, 7)

    dproj = jnp.concatenate([du_ssm, du_pool, dga, dgb_], axis=1)
    gw_in = mm_tn("gw_in", h1, dproj, BF16, NDEV, 1, after=[tok])
    tok, wait_pair_c = pair_exchange("pair_exchange_in", [gw_in], 8)
    (dh1,) = mm_nt("d_proj", dproj, wg_in, F32, 2, after=[tok])
    grads_c, got_c = wait_pair_c(dh1)
    parts_c = [pair_sum("pair_sum_in", grads_c[0], got_c[0], place)]
    tok, wait_chip_c = chip_exchange("chip_exchange_in", parts_c, 9)

    def e10(t, b):
        dh1v, xv, dxav = t
        xh, rs = _ln_stats(xv)
        return ([dxav + _ln_bwd(dh1v * (1.0 + b[0]), xh, rs)],
                [_colsum(dh1v * xh), _colsum(dh1v)])
    grad_x, d_sc1, d_sh1 = _rowwise("ln_mod1_bwd", e10, S, ts, [(dh1, D, 0), (x2d, D, 0), (dxa, D, 0)],
                                    [sc1], [(D, F32)], [D, D], after=[tok])

    gb4 = _diag_blocks(g_bmat[:, :, :GPB * P], GPB, H, P), _diag_blocks(g_bmat[:, :, GPB * P:], GPB, H, P)
    dbb = jnp.stack([jnp.transpose(t, (2, 0, 1, 3)).reshape(H, G * P) for t in gb4])
    g_bt_re, g_bt_im, g_f = s5_bbar_bwd(f2r, bt_re, bt_im, dbb)
    g_b_re = jnp.transpose(g_bt_re.reshape(H, G, P), (1, 2, 0))
    g_b_im = jnp.transpose(g_bt_im.reshape(H, G, P), (1, 2, 0))
    gc_top = _diag_blocks(g_cmat[:, :GPB * P, :], GPB, P, H)
    gc_bot = _diag_blocks(g_cmat[:, GPB * P:, :], GPB, P, H)
    g_c_re = jnp.transpose(gc_top, (0, 1, 3, 2)).reshape(G, H, P)
    g_c_im = -jnp.transpose(gc_bot, (0, 1, 3, 2)).reshape(G, H, P)
    d_ab = jnp.transpose(g_a.reshape(nblk, 2, GPB, P), (1, 0, 2, 3)).reshape(2, G, P)
    g_lr, g_li, g_ldt = s5_disc_bwd(lam_re[0], lam_im[0], log_dt[0].reshape(G, 1), d_ab,
                                    g_f.reshape(2, G, P))

    dmod = jnp.concatenate([d_sh1, d_sc1, d_g1, d_sh2, d_sc2, d_g2], axis=1)
    small_g = [dmod, g_lr, g_li, g_ldt, g_b_re, g_b_im, g_c_re, g_c_im, g_d, g_pscale,
               g_ln1g, g_ln1b, g_ln2g, g_ln2b, loss_acc]
    packed_g = _small_pack(small_g)
    (parts_all,) = seq_all_gather("gather_small", [packed_g], 10, routed=False)
    _, got3_a = wait_chip_a(packed_g)
    glu_w = jnp.stack([w_glu_val[0], w_glu_gate[0]])
    glu_m = jnp.stack([m_w_glu_val[0], m_w_glu_gate[0]])
    glu_v = jnp.stack([v_w_glu_val[0], v_w_glu_gate[0]])
    wmv = [(w_ff2[0], m_w_ff2[0], v_w_ff2[0]), (w_ff1[0], m_w_ff1[0], v_w_ff1[0]),
           (w_out[0], m_w_out[0], v_w_out[0]), (w_pool_out[0], m_w_pool_out[0], v_w_pool_out[0]),
           (w_pool[0], m_w_pool[0], v_w_pool[0]), (glu_w, glu_m, glu_v)]
    upd = [adamw_sharded("adamw_%d" % i, g, p, t, w, m, v, place)
           for i, (g, p, t, (w, m, v)) in enumerate(zip(grads_a, got_a, got3_a, wmv[:2]))]
    _, got3_b = wait_chip_b(upd[-1][0])
    upd += [adamw_sharded("adamw_%d" % (2 + i), g, p, t, w, m, v, place)
            for i, (g, p, t, (w, m, v)) in enumerate(zip(grads_b, got_b, got3_b, wmv[2:]))]
    u_ff2, u_ff1, u_out, u_po, u_pool, u_glu = upd

    sg, sd, sm, sv = adamw_small(parts_all, *packed_wmv, after=[upd[-1][0]])
    shapes = [t.shape for t in small_names]
    loss = _small_unpack(sg, shapes + [(1, LANES)])[-1][0, 0]
    sg, sd, sm, sv = (_small_unpack(t, shapes) for t in (sg, sd, sm, sv))

    nmod = 6 * D
    dmod_all = parts_all[:, :nmod // LANES, :].reshape(NDEV, nmod)
    c_all_t = jnp.transpose(c_all.reshape(NDEV, D))
    ada_out = adamw_ada(c_all_t, dmod_all, w_ada[0], m_w_ada[0], v_w_ada[0], my_dev)
    _, got3_c = wait_chip_c(ada_out[0])
    u_in = adamw_sharded("adamw_6", grads_c[0], got_c[0], got3_c[0], w_in[0], m_w_in[0], v_w_in[0], place)

    def pick(k):
        return [ada_out[k][None], sg_sd[k][0], u_in[k][None]] + [t for t in sg_sd[k][1:9]] + \
               [u_glu[k][0][None], u_glu[k][1][None], u_pool[k][None], sg_sd[k][9], u_po[k][None],
                u_out[k][None], sg_sd[k][10], sg_sd[k][11], u_ff1[k][None], u_ff2[k][None],
                sg_sd[k][12], sg_sd[k][13]]

    sg_sd = [sg, sd, sm, sv]
    return (loss, grad_x[None], *pick(0), *pick(1), *pick(2), *pick(3))
```

```python
import functools
import math

import jax
import jax.numpy as jnp
from jax import lax
from jax.experimental import pallas as pl
from jax.experimental.pallas import tpu as pltpu
from jax.experimental.pallas import tpu_sc as plsc

F32 = jnp.float32
BF16 = jnp.bfloat16
MESH = pl.DeviceIdType.MESH
NDEV = 8
NCHIP = 4

SSM_GROUP = 16
SSM_STATE = 64
GROUPS_PER_BLOCK = 8
POOL_WINDOWS = (2, 4, 8, 16)
LN_EPS = 1e-5
ALPHA = 2.0 ** 0.25
ADAM_LR, ADAM_B1, ADAM_B2, ADAM_EPS, ADAM_WD, ADAM_STEP = 0.001, 0.9, 0.999, 1e-08, 0.01, 10
SUBLANES = 8
LANES = 128
VMEM_LIMIT = 56 * 1024 * 1024


def _params(sem=None, vmem=VMEM_LIMIT):
    return pltpu.CompilerParams(dimension_semantics=sem, vmem_limit_bytes=vmem)


def _tile(n, pref):
    if n <= pref:
        return n
    t = 1 << (pref.bit_length() - 1)
    while n % t:
        t //= 2
    return t


def _cast_epi(vals, ex, outs):
    c = vals[0].shape[1]
    for s, v in enumerate(vals):
        outs[0][:, s * c:(s + 1) * c] = v.astype(outs[0].dtype)


ANY = pl.BlockSpec(memory_space=pl.ANY)


def _with_after(body, n_in, after):
    if not after:
        return body
    n_af = len(after)

    def wrapped(*refs):
        return body(*refs[:n_in], *refs[n_in + n_af:])
    return wrapped


def _mm(name, kind, a, b, grid, a_spec, b_spec, outs, acc_shape, nsub=1, c=None,
        pro=None, epi=None, extras=(), stacked_out=False, after=()):
    nk = grid[2]
    n_ex, n_out = len(extras), len(outs)

    def finish(vals, ex, out_refs):
        if epi is not None:
            epi(vals, ex, out_refs)
        elif stacked_out:
            for s, v in enumerate(vals):
                out_refs[0][s] = v.astype(out_refs[0].dtype)
        else:
            _cast_epi(vals, ex, out_refs)

    def body(*refs):
        a_ref, b_ref = refs[0], refs[1]
        ex = refs[2:2 + n_ex]
        out_refs = refs[2 + n_ex:2 + n_ex + n_out]
        k = pl.program_id(2)
        av = a_ref[...]
        if pro is not None:
            av = pro(av)
        if kind == "nn":
            prods = [jnp.dot(av, b_ref[s], preferred_element_type=F32) for s in range(nsub)]
        elif kind == "nt":
            t = None
            for s in range(nsub):
                d = lax.dot_general(av[:, s * c:(s + 1) * c], b_ref[s], (((1,), (1,)), ((), ())),
                                    preferred_element_type=F32)
                t = d if t is None else t + d
            prods = [t]
        else:
            t = lax.dot_general(av, b_ref[...], (((0,), (0,)), ((), ())), preferred_element_type=F32)
            prods = [t[:, s * c:(s + 1) * c] for s in range(nsub)] if stacked_out else [t]
        if nk == 1:
            finish(prods, ex, out_refs)
            return
        acc = refs[-1]
        w = prods[0].shape[1]

        @pl.when(k == 0)
        def _():
            for s, p in enumerate(prods):
                acc[:, s * w:(s + 1) * w] = p

        @pl.when(jnp.logical_and(k > 0, k < nk - 1))
        def _():
            for s, p in enumerate(prods):
                acc[:, s * w:(s + 1) * w] += p

        @pl.when(k == nk - 1)
        def _():
            finish([acc[:, s * w:(s + 1) * w] + p for s, p in enumerate(prods)], ex, out_refs)

    res = pl.pallas_call(
        _with_after(body, 2 + n_ex, after), name=name, grid=grid,
        in_specs=[a_spec, b_spec] + [e[1] for e in extras] + [ANY] * len(after),
        out_specs=[o[1] for o in outs],
        out_shape=[o[0] for o in outs],
        scratch_shapes=[pltpu.VMEM(acc_shape, F32)] if nk > 1 else [],
        compiler_params=_params(("parallel", "parallel", "arbitrary")),
    )(a, b, *[e[0] for e in extras], *after)
    return res


def _sds(shape, dtype):
    return jax.ShapeDtypeStruct(shape, dtype)


def mm_nn(name, a, b3, out_dtype, nsub, tm=1024, tk=2048, tn=None, pro=None, epi=None,
          extras=(), extra_outs=(), a_col0=0, after=()):
    M = a.shape[0]
    nb, K, cdim = b3.shape
    tm, tk = _tile(M, tm), _tile(K, tk)
    if nb == 1:
        tn = _tile(cdim, tn or 1024)
        nsub, c, nj = 1, tn, cdim // tn
        b_spec = pl.BlockSpec((1, tk, tn), lambda i, j, k: (0, k, j))
        N = cdim
    else:
        c, nj, tn = cdim, nb // nsub, nsub * cdim
        b_spec = pl.BlockSpec((nsub, tk, cdim), lambda i, j, k: (j, k, 0))
        N = nb * cdim
    kb0 = a_col0 // tk
    a_spec = pl.BlockSpec((tm, tk), lambda i, j, k: (i, kb0 + k))
    grid = (M // tm, nj, K // tk)
    o_spec = pl.BlockSpec((tm, tn), lambda i, j, k: (i, j))
    outs = [(_sds((M, N), out_dtype), o_spec)] + [(_sds((M, N), d), o_spec) for d in extra_outs]
    return _mm(name, "nn", a, b3, grid, a_spec, b_spec, outs, (tm, tn), nsub, c, pro, epi, extras,
               after=after)


def mm_nt(name, a, b3, out_dtype, nsub, tm=1024, tn=1024, epi=None, extras=(), extra_outs=(),
          after=()):
    M = a.shape[0]
    nb, N, cdim = b3.shape
    tm, tn = _tile(M, tm), _tile(N, tn)
    if nb == 1:
        tk = _tile(cdim, 2048)
        nsub, c, nk = 1, tk, cdim // tk
        b_spec = pl.BlockSpec((1, tn, tk), lambda i, j, k: (0, j, k))
    else:
        c, nk, tk = cdim, nb // nsub, nsub * cdim
        b_spec = pl.BlockSpec((nsub, tn, cdim), lambda i, j, k: (k, j, 0))
    a_spec = pl.BlockSpec((tm, tk), lambda i, j, k: (i, k))
    grid = (M // tm, N // tn, nk)
    o_spec = pl.BlockSpec((tm, tn), lambda i, j, k: (i, j))
    outs = [(_sds((M, N), out_dtype), o_spec)] + [(_sds((M, N), d), o_spec) for d in extra_outs]
    return _mm(name, "nt", a, b3, grid, a_spec, b_spec, outs, (tm, tn), nsub, c, None, epi, extras,
               after=after)


def mm_tn(name, a, b, out_dtype, nb, nsub, tma=1024, tk=2048, pro=None, a_col0=0, a_cols=None,
          after=()):
    S = a.shape[0]
    Ka = a_cols or a.shape[1]
    N = b.shape[1]
    tk = _tile(S, tk)
    if nsub == 0:
        tma, tn = _tile(Ka, tma), _tile(N, 1024)
        grid = (Ka // tma, N // tn, S // tk)
        ab0 = a_col0 // tma
        res = _mm(name, "tn", a, b, grid, pl.BlockSpec((tk, tma), lambda i, j, k: (k, ab0 + i)),
                  pl.BlockSpec((tk, tn), lambda i, j, k: (k, j)),
                  [(_sds((Ka, N), out_dtype), pl.BlockSpec((tma, tn), lambda i, j, k: (i, j)))],
                  (tma, tn), 1, tn, pro, None, (), after=after)[0]
        return res.reshape(nb, Ka // nb, N)
    else:
        c = N // nb
        tma = _tile(Ka, tma)
        grid = (Ka // tma, nb // nsub, S // tk)
        o_spec = pl.BlockSpec((nsub, tma, c), lambda i, j, k: (j, i, 0))
        out = _sds((nb, Ka, c), out_dtype)
        nsub_k = nsub
        tn = nsub * c
        b_spec = pl.BlockSpec((tk, tn), lambda i, j, k: (k, j))
    ab0 = a_col0 // tma
    a_spec = pl.BlockSpec((tk, tma), lambda i, j, k: (k, ab0 + i))
    return _mm(name, "tn", a, b, grid, a_spec, b_spec, [(out, o_spec)], (tma, tn), nsub_k, c,
               pro, None, (), stacked_out=True, after=after)[0]


def _rowwise(name, fn, S, ts, tiled, bcast, tiled_out, acc_out, after=()):
    nt, nb, no, na = len(tiled), len(bcast), len(tiled_out), len(acc_out)

    def body(*refs):
        tin = [r[...] for r in refs[:nt]]
        bin_ = [r[...] for r in refs[nt:nt + nb]]
        o_refs = refs[nt + nb:nt + nb + no]
        a_refs = refs[nt + nb + no:]
        touts, aouts = fn(tin, bin_)
        for r, v in zip(o_refs, touts):
            r[...] = v.astype(r.dtype)
        i = pl.program_id(0)

        @pl.when(i == 0)
        def _():
            for r, v in zip(a_refs, aouts):
                r[...] = v

        @pl.when(i > 0)
        def _():
            for r, v in zip(a_refs, aouts):
                r[...] += v

    in_specs = [pl.BlockSpec((ts, w), functools.partial(lambda i, cb: (i, cb), cb=cb))
                for (_, w, cb) in tiled]
    in_specs += [pl.BlockSpec(b.shape, lambda i: (0, 0)) for b in bcast]
    out_specs = [pl.BlockSpec((ts, w), lambda i: (i, 0)) for (w, _) in tiled_out]
    out_specs += [pl.BlockSpec((1, w), lambda i: (0, 0)) for w in acc_out]
    out_shape = [_sds((S, w), d) for (w, d) in tiled_out] + [_sds((1, w), F32) for w in acc_out]
    return pl.pallas_call(
        _with_after(body, nt + nb, after), name=name, grid=(S // ts,),
        in_specs=in_specs + [ANY] * len(after), out_specs=out_specs,
        out_shape=out_shape, compiler_params=_params(("arbitrary",)),
    )(*[t[0] for t in tiled], *bcast, *after)


def _ln_stats(v):
    mu = jnp.mean(v, axis=-1, keepdims=True)
    vc = v - mu
    var = jnp.mean(vc * vc, axis=-1, keepdims=True)
    rstd = lax.rsqrt(var + LN_EPS)
    return vc * rstd, rstd


def _ln_bwd(dxhat, xhat, rstd):
    return rstd * (dxhat - jnp.mean(dxhat, axis=-1, keepdims=True)
                   - xhat * jnp.mean(dxhat * xhat, axis=-1, keepdims=True))


def _colsum(v):
    return jnp.sum(v, axis=0, keepdims=True)


def _sigmoid(v):
    return 1.0 / (1.0 + jnp.exp(-v))


_GELU_C = math.sqrt(2.0 / math.pi)


def _gelu(v):
    return 0.5 * v * (1.0 + jnp.tanh(_GELU_C * (v + 0.044715 * v * v * v)))


def _gelu_grad(v):
    t = jnp.tanh(_GELU_C * (v + 0.044715 * v * v * v))
    return 0.5 * (1.0 + t) + 0.5 * v * (1.0 - t * t) * _GELU_C * (1.0 + 3 * 0.044715 * v * v)


def _disc(lr, li, ldt):
    dt = jnp.exp(ldt)
    mag = jnp.exp(lr * dt)
    ang = li * dt
    ab_re = mag * jnp.cos(ang)
    ab_im = mag * jnp.sin(ang)
    num_re = ab_re - 1.0
    num_im = ab_im
    den = lr * lr + li * li
    f_re = (num_re * lr + num_im * li) / den
    f_im = (num_im * lr - num_re * li) / den
    return ab_re, ab_im, f_re, f_im


def _cmul(ar, ai, br, bi):
    return ar * br - ai * bi, ar * bi + ai * br


def s5_disc(lam_re, lam_im, log_dt):
    G, P = lam_re.shape

    def body(lr_ref, li_ref, ldt_ref, f_ref, k_ref):
        ab_re, ab_im, f_re, f_im = _disc(lr_ref[...], li_ref[...], ldt_ref[...])
        f_ref[0] = f_re
        f_ref[1] = f_im
        pr, pi = [ab_re], [ab_im]
        for _ in range(SUBLANES - 1):
            nr, ni = _cmul(pr[-1], pi[-1], ab_re, ab_im)
            pr.append(nr)
            pi.append(ni)
        zero = jnp.zeros_like(ab_re)
        for n, sh in enumerate((1, 2, 4)):
            for r in range(SUBLANES):
                k_ref[2 * n, r] = pr[sh - 1] if r >= sh else zero
                k_ref[2 * n + 1, r] = pi[sh - 1] if r >= sh else zero
                k_ref[8 + 2 * n, r] = pr[sh - 1] if r + sh < SUBLANES else zero
                k_ref[8 + 2 * n + 1, r] = -pi[sh - 1] if r + sh < SUBLANES else zero
        for r in range(SUBLANES):
            k_ref[6, r] = pr[r]
            k_ref[7, r] = pi[r]
            k_ref[14, r] = pr[SUBLANES - 1 - r]
            k_ref[15, r] = -pi[SUBLANES - 1 - r]

    vm = pl.BlockSpec(memory_space=pltpu.VMEM)
    return pl.pallas_call(
        body, name="s5_disc", in_specs=[vm, vm, vm], out_specs=[vm, vm],
        out_shape=[_sds((2, G, P), F32), _sds((16, SUBLANES, G, P), F32)],
    )(lam_re, lam_im, log_dt)


def s5_disc_bwd(lam_re, lam_im, log_dt, d_ab, d_f):
    G, P = lam_re.shape

    def body(lr_ref, li_ref, ldt_ref, dab_ref, df_ref, glr_ref, gli_ref, gdt_ref):
        _, vjp = jax.vjp(_disc, lr_ref[...], li_ref[...], ldt_ref[...])
        glr, gli, gdt = vjp((dab_ref[0], dab_ref[1], df_ref[0], df_ref[1]))
        glr_ref[...] = glr
        gli_ref[...] = gli
        gdt_ref[...] = gdt

    vm = pl.BlockSpec(memory_space=pltpu.VMEM)
    return pl.pallas_call(
        body, name="s5_disc_bwd", in_specs=[vm] * 5, out_specs=[vm] * 3,
        out_shape=[_sds((G, P), F32), _sds((G, P), F32), _sds((G, 1), F32)],
    )(lam_re, lam_im, log_dt, d_ab, d_f)


def _group_mask(cw, nst):
    row = lax.broadcasted_iota(jnp.int32, (cw, 2 * nst), 0) // SSM_GROUP
    col = (lax.broadcasted_iota(jnp.int32, (cw, 2 * nst), 1) % nst) // SSM_STATE
    return row == col


def _spread(t, mask):
    reps = mask.shape[0] // t.shape[0]
    return jnp.where(mask, jnp.tile(t, (reps, 1)), 0.0).astype(BF16)


def _gather_groups(t, mask):
    t = jnp.where(mask, t, 0.0)
    out = t[0:SSM_GROUP]
    for g in range(1, t.shape[0] // SSM_GROUP):
        out = out + t[g * SSM_GROUP:(g + 1) * SSM_GROUP]
    return out


def _s5_operands(f_ref, br_ref, bi_ref, cr_ref, ci_ref, mask):
    fr, fi = f_ref[0], f_ref[1]
    br, bi = br_ref[...], bi_ref[...]
    bm = _spread(jnp.concatenate([fr * br - fi * bi, fr * bi + fi * br], axis=1), mask)
    cm = _spread(jnp.concatenate([cr_ref[...], -ci_ref[...]], axis=1), mask)
    return bm, cm


def _scan_fwd(xs, k_ref, nst):
    ntile = xs.shape[0] // SUBLANES

    def step(t, carry):
        cr, ci = carry
        r0 = pl.multiple_of(t * SUBLANES, SUBLANES)
        xr = xs[pl.ds(r0, SUBLANES), 0:nst]
        xi = xs[pl.ds(r0, SUBLANES), nst:2 * nst]
        for n, sh in enumerate((1, 2, 4)):
            sr = pltpu.roll(xr, sh, 0)
            si = pltpu.roll(xi, sh, 0)
            mr, mi = k_ref[2 * n], k_ref[2 * n + 1]
            xr, xi = xr + mr * sr - mi * si, xi + mr * si + mi * sr
        pr, pi = k_ref[6], k_ref[7]
        xr, xi = xr + pr * cr - pi * ci, xi + pr * ci + pi * cr
        xs[pl.ds(r0, SUBLANES), 0:nst] = xr
        xs[pl.ds(r0, SUBLANES), nst:2 * nst] = xi
        return (jnp.broadcast_to(xr[SUBLANES - 1:SUBLANES, :], xr.shape),
                jnp.broadcast_to(xi[SUBLANES - 1:SUBLANES, :], xi.shape))

    zero = jnp.zeros((SUBLANES, nst), F32)
    lax.fori_loop(0, ntile, step, (zero, zero))


def _scan_bwd(g, xs, k_ref, nst):
    ntile = g.shape[0] // SUBLANES
    row = lax.broadcasted_iota(jnp.int32, (SUBLANES, nst), 0)

    def step(tt, carry):
        cr, ci, ar, ai = carry
        t = ntile - 1 - tt
        r0 = pl.multiple_of(t * SUBLANES, SUBLANES)
        gr = g[pl.ds(r0, SUBLANES), 0:nst]
        gi = g[pl.ds(r0, SUBLANES), nst:2 * nst]
        for n, sh in enumerate((1, 2, 4)):
            sr = pltpu.roll(gr, SUBLANES - sh, 0)
            si = pltpu.roll(gi, SUBLANES - sh, 0)
            mr, mi = k_ref[8 + 2 * n], k_ref[8 + 2 * n + 1]
            gr, gi = gr + mr * sr - mi * si, gi + mr * si + mi * sr
        qr, qi = k_ref[14], k_ref[15]
        gr, gi = gr + qr * cr - qi * ci, gi + qr * ci + qi * cr
        g[pl.ds(r0, SUBLANES), 0:nst] = gr
        g[pl.ds(r0, SUBLANES), nst:2 * nst] = gi
        p0 = pl.multiple_of(jnp.maximum(t - 1, 0) * SUBLANES, SUBLANES)
        live = (t > 0).astype(F32)
        xr = xs[pl.ds(r0, SUBLANES), 0:nst]
        xi = xs[pl.ds(r0, SUBLANES), nst:2 * nst]
        pr = xs[pl.ds(p0, SUBLANES), 0:nst][SUBLANES - 1:SUBLANES, :] * live
        pi = xs[pl.ds(p0, SUBLANES), nst:2 * nst][SUBLANES - 1:SUBLANES, :] * live
        xmr = jnp.where(row == 0, jnp.broadcast_to(pr, xr.shape), pltpu.roll(xr, 1, 0))
        xmi = jnp.where(row == 0, jnp.broadcast_to(pi, xi.shape), pltpu.roll(xi, 1, 0))
        ar = ar + gr * xmr + gi * xmi
        ai = ai + gi * xmr - gr * xmi
        return (jnp.broadcast_to(gr[0:1, :], gr.shape), jnp.broadcast_to(gi[0:1, :], gi.shape),
                ar, ai)

    zero = jnp.zeros((SUBLANES, nst), F32)
    _, _, ar, ai = lax.fori_loop(0, ntile, step, (zero, zero, zero, zero))
    return _colsum(ar), _colsum(ai)


def _s5_param_specs(cw, nst):
    hp = pl.BlockSpec((SSM_GROUP, nst), lambda b: (0, b))
    return [pl.BlockSpec((2, 1, nst), lambda b: (0, 0, b)), hp, hp, hp, hp,
            pl.BlockSpec((1, cw), lambda b: (0, b)),
            pl.BlockSpec((16, SUBLANES, nst), lambda b: (0, 0, b))]


def s5_fwd(proj, params, nb):
    S = proj.shape[0]
    nst = params[1].shape[1] // nb
    cw = nst // SSM_STATE * SSM_GROUP

    def body(u_ref, f_ref, br_ref, bi_ref, cr_ref, ci_ref, d_ref, k_ref, z_ref, xsb_ref, xs):
        bm, cm = _s5_operands(f_ref, br_ref, bi_ref, cr_ref, ci_ref, _group_mask(cw, nst))
        u = u_ref[...]
        xs[...] = jnp.dot(u.astype(BF16), bm, preferred_element_type=F32)
        _scan_fwd(xs, k_ref, nst)
        xsb = xs[...].astype(BF16)
        xsb_ref[...] = xsb
        y = lax.dot_general(xsb, cm, (((1,), (1,)), ((), ())), preferred_element_type=F32)
        z_ref[...] = _gelu(y + d_ref[...] * u).astype(BF16)

    return pl.pallas_call(
        body, name="s5_fwd", grid=(nb,),
        in_specs=[pl.BlockSpec((S, cw), lambda b: (0, b))] + _s5_param_specs(cw, nst),
        out_specs=[pl.BlockSpec((S, cw), lambda b: (0, b)), pl.BlockSpec((S, 2 * nst), lambda b: (0, b))],
        out_shape=[_sds((S, nb * cw), BF16), _sds((S, nb * 2 * nst), BF16)],
        scratch_shapes=[pltpu.VMEM((S, 2 * nst), F32)],
        compiler_params=_params(("arbitrary",)),
    )(proj, *params)


def s5_bwd(proj, xsb_all, dz, params, nb, after=()):
    S = proj.shape[0]
    nst = params[1].shape[1] // nb
    cw = nst // SSM_STATE * SSM_GROUP

    def body(u_ref, xsb_ref, dz_ref, f_ref, br_ref, bi_ref, cr_ref, ci_ref, d_ref, k_ref,
             du_ref, gbr_ref, gbi_ref, gcr_ref, gci_ref, gf_ref, gd_ref, ga_ref, xs, g):
        mask = _group_mask(cw, nst)
        bm, cm = _s5_operands(f_ref, br_ref, bi_ref, cr_ref, ci_ref, mask)
        u = u_ref[...]
        ub = u.astype(BF16)
        d = d_ref[...]
        xsb = xsb_ref[...]
        xs[...] = xsb.astype(F32)
        y = lax.dot_general(xsb, cm, (((1,), (1,)), ((), ())), preferred_element_type=F32) + d * u
        dy = dz_ref[...].astype(F32) * _gelu_grad(y)
        gd_ref[...] = _colsum(dy * u)
        dyb = dy.astype(BF16)
        gc = _gather_groups(lax.dot_general(dyb, xsb, (((0,), (0,)), ((), ())),
                                            preferred_element_type=F32), mask)
        gcr_ref[...] = gc[:, :nst]
        gci_ref[...] = -gc[:, nst:]
        g[...] = jnp.dot(dyb, cm, preferred_element_type=F32)
        ar, ai = _scan_bwd(g, xs, k_ref, nst)
        ga_ref[0, 0:1, :] = ar
        ga_ref[0, 1:2, :] = ai
        gb = g[...].astype(BF16)
        du = lax.dot_general(gb, bm, (((1,), (1,)), ((), ())), preferred_element_type=F32) + d * dy
        du_ref[...] = du.astype(BF16)
        gbb = _gather_groups(lax.dot_general(ub, gb, (((0,), (0,)), ((), ())),
                                             preferred_element_type=F32), mask)
        dr, di = gbb[:, :nst], gbb[:, nst:]
        fr, fi = f_ref[0], f_ref[1]
        br, bi = br_ref[...], bi_ref[...]
        gbr_ref[...] = fr * dr + fi * di
        gbi_ref[...] = fr * di - fi * dr
        gf_ref[0] = _colsum(dr * br + di * bi)
        gf_ref[1] = _colsum(di * br - dr * bi)

    hp = pl.BlockSpec((SSM_GROUP, nst), lambda b: (0, b))
    hp_sds = _sds((SSM_GROUP, nb * nst), F32)
    return pl.pallas_call(
        _with_after(body, 10, after), name="s5_bwd", grid=(nb,),
        in_specs=[pl.BlockSpec((S, cw), lambda b: (0, b)),
                  pl.BlockSpec((S, 2 * nst), lambda b: (0, b)),
                  pl.BlockSpec((S, cw), lambda b: (0, b))] + _s5_param_specs(cw, nst)
        + [ANY] * len(after),
        out_specs=[pl.BlockSpec((S, cw), lambda b: (0, b)), hp, hp, hp, hp,
                   pl.BlockSpec((2, 1, nst), lambda b: (0, 0, b)),
                   pl.BlockSpec((1, cw), lambda b: (0, b)),
                   pl.BlockSpec((1, 2, nst), lambda b: (b, 0, 0))],
        out_shape=[_sds((S, nb * cw), BF16), hp_sds, hp_sds, hp_sds, hp_sds,
                   _sds((2, 1, nb * nst), F32), _sds((1, nb * cw), F32), _sds((nb, 2, nst), F32)],
        scratch_shapes=[pltpu.VMEM((S, 2 * nst), F32), pltpu.VMEM((S, 2 * nst), F32)],
        compiler_params=_params(("arbitrary",)),
    )(proj, xsb_all, dz, *params, *after)


def _shift_rows(v, k, row, down):
    n = v.shape[0]
    if down:
        return jnp.where(row >= k, pltpu.roll(v, k, 0), 0.0)
    return jnp.where(row < n - k, pltpu.roll(v, n - k, 0), 0.0)


def _window(v, gi, row, down):
    sums = []
    s = v
    for k in (1, 2, 4, 8):
        s = s + _shift_rows(s, k, row, down)
        sums.append(s)
    out = sums[3]
    for n in (2, 1, 0):
        out = jnp.where(gi == n, sums[n], out)
    return out


def pool_fwd(proj, col0, width, gw):
    S = proj.shape[0]
    cb0 = col0 // gw

    def body(u_ref, o_ref):
        gi = pl.program_id(0)
        u = u_ref[...]
        row = lax.broadcasted_iota(jnp.int32, u.shape, 0)
        w = jnp.left_shift(2, gi)
        count = jnp.minimum(row + 1, w).astype(F32)
        o_ref[...] = (_window(u, gi, row, True) / count - u).astype(BF16)

    return pl.pallas_call(
        body, name="pool_fwd", grid=(len(POOL_WINDOWS),),
        in_specs=[pl.BlockSpec((S, gw), lambda g: (0, cb0 + g))],
        out_specs=pl.BlockSpec((S, gw), lambda g: (0, g)),
        out_shape=_sds((S, width), BF16), compiler_params=_params(("arbitrary",)),
    )(proj)


def pool_bwd(dpooled, gw):
    S, width = dpooled.shape

    def body(d_ref, o_ref):
        gi = pl.program_id(0)
        d = d_ref[...]
        row = lax.broadcasted_iota(jnp.int32, d.shape, 0)
        w = jnp.left_shift(2, gi)
        count = jnp.minimum(row + 1, w).astype(F32)
        o_ref[...] = (_window(d / count, gi, row, False) - d).astype(BF16)

    return pl.pallas_call(
        body, name="pool_bwd", grid=(len(POOL_WINDOWS),),
        in_specs=[pl.BlockSpec((S, gw), lambda g: (0, g))],
        out_specs=pl.BlockSpec((S, gw), lambda g: (0, g)),
        out_shape=_sds((S, width), BF16), compiler_params=_params(("arbitrary",)),
    )(dpooled)


def _place():
    x, y, c = lax.axis_index("x"), lax.axis_index("y"), lax.axis_index("c")
    chips = [(1 - x, y), (x, 1 - y), (1 - x, 1 - y)]
    return x, y, c, chips


HBM = pl.BlockSpec(memory_space=pltpu.HBM)


def _gather_body(n, handshake):
    def body(*refs):
        ins, outs = refs[:n], refs[n:2 * n]
        send_sems, recv_sems, local_sems = refs[2 * n:]
        x, y, c, chips = _place()
        if handshake:
            barrier = pltpu.get_barrier_semaphore()
            for peer in [(x, y, 1 - c)] + [(*chip, c) for chip in chips]:
                pl.semaphore_signal(barrier, inc=1, device_id=peer, device_id_type=MESH)
            pl.semaphore_wait(barrier, 4)
        me, sibling = (x, y, c), (x, y, 1 - c)

        def slot(i, p):
            return outs[i].at[4 * p[0] + 2 * p[1] + p[2]]

        def copy(i, k, block, to, src=None):
            return pltpu.make_async_remote_copy(
                src_ref=slot(i, block) if src is None else src, dst_ref=slot(i, block),
                send_sem=send_sems.at[i, k], recv_sem=recv_sems.at[i, k],
                device_id=to, device_id_type=MESH)

        started = []
        for i in range(n):
            for j, chip in enumerate(chips):
                started.append(copy(i, 1 + j, me, (*chip, c), src=ins[i]))
                started[-1].start()
        for i in range(n):
            started.append(copy(i, 0, me, sibling, src=ins[i]))
            started[-1].start()
        mine = [pltpu.make_async_copy(ins[i], slot(i, me), local_sems.at[i]) for i in range(n)]
        for cp in mine:
            cp.start()
        for i in range(n):
            for j, chip in enumerate(chips):
                copy(i, 1 + j, (*chip, c), me).wait_recv()
                started.append(copy(i, 4 + j, (*chip, c), sibling))
                started[-1].start()
        for i in range(n):
            copy(i, 0, sibling, me).wait_recv()
            for j, chip in enumerate(chips):
                copy(i, 4 + j, (*chip, 1 - c), me).wait_recv()
        for cp in started:
            cp.wait_send()
        for cp in mine:
            cp.wait()

    return body


def _routed_gather_body(n):
    def body(*refs):
        ins, outs = refs[:n], refs[n:2 * n]
        send_sems, recv_sems, local_sems = refs[2 * n:]
        x, y, c, (xn, yn, dg) = _place()
        me, sibling = (x, y, c), (x, y, 1 - c)
        barrier = pltpu.get_barrier_semaphore()
        for peer in (sibling, (*xn, c), (*yn, c)):
            pl.semaphore_signal(barrier, inc=1, device_id=peer, device_id_type=MESH)
        pl.semaphore_wait(barrier, 3)

        def piece(i, p, h):
            rows = ins[i].shape[0] // 2
            return outs[i].at[4 * p[0] + 2 * p[1] + p[2], pl.ds(h * rows, rows)]

        def copy(i, k, src, dst, to):
            return pltpu.make_async_remote_copy(src_ref=src, dst_ref=dst, send_sem=send_sems.at[i, k],
                                                recv_sem=recv_sems.at[i, k], device_id=to,
                                                device_id_type=MESH)

        started = []

        def go(cp):
            cp.start()
            started.append(cp)

        for i in range(n):
            rows = ins[i].shape[0] // 2
            for h in range(2):
                own = ins[i].at[pl.ds(h * rows, rows)]
                go(copy(i, 1 + h, own, piece(i, me, h), (*xn, c)))
                go(copy(i, 3 + h, own, piece(i, me, h), (*yn, c)))
        for i in range(n):
            go(copy(i, 0, ins[i], outs[i].at[4 * x + 2 * y + c], sibling))
        mine = [pltpu.make_async_copy(ins[i], outs[i].at[4 * x + 2 * y + c], local_sems.at[i])
                for i in range(n)]
        for cp in mine:
            cp.start()
        for i in range(n):
            for k, chip, h, onward, ksib in ((1, xn, 0, (5, yn), 7), (4, yn, 1, (6, xn), 10),
                                            (2, xn, 1, None, 8), (3, yn, 0, None, 9),
                                            (5, dg, 0, None, 11), (6, dg, 1, None, 12)):
                got = piece(i, (*chip, c), h)
                copy(i, k, got, got, me).wait_recv()
                if onward is not None:
                    go(copy(i, onward[0], got, got, (*onward[1], c)))
                go(copy(i, ksib, got, got, sibling))
        for i in range(n):
            block = outs[i].at[4 * x + 2 * y + 1 - c]
            copy(i, 0, block, block, me).wait_recv()
            for ksib, chip, h in ((7, xn, 0), (10, yn, 1), (8, xn, 1), (9, yn, 0), (11, dg, 0), (12, dg, 1)):
                got = piece(i, (*chip, 1 - c), h)
                copy(i, ksib, got, got, me).wait_recv()
        for cp in started:
            cp.wait_send()
        for cp in mine:
            cp.wait()

    return body


def _on_sequencer(name, body, arrays, out_sds, sems, collective_id):
    ins = [jax.new_ref(a, memory_space=pltpu.MemorySpace.HBM) for a in arrays]
    outs = [jax.empty_ref(s, memory_space=pltpu.MemorySpace.HBM) for s in out_sds]

    @pl.kernel(mesh=plsc.ScalarSubcoreMesh(axis_name="sequencer", num_cores=1), name=name,
               scratch_types=tuple(sems),
               compiler_params=pltpu.CompilerParams(collective_id=collective_id))
    def launch(*sem_refs):
        body(*ins, *outs, *sem_refs)

    launch()
    return [o[...] for o in outs]


def seq_all_gather(name, shards, collective_id, routed=True):
    n = len(shards)
    nsem = 13 if routed else 7
    return _on_sequencer(
        name, _routed_gather_body(n) if routed else _gather_body(n, True), shards,
        [_sds((NDEV,) + s.shape, s.dtype) for s in shards],
        [pltpu.SemaphoreType.DMA((n, nsem)), pltpu.SemaphoreType.DMA((n, nsem)),
         pltpu.SemaphoreType.DMA((n,))], collective_id)


def pair_exchange(name, grads, collective_id):
    def plan(srcs, lands):
        x, y, c, _ = _place()
        return ([(i, q, srcs[i].at[2 * q + 1 - c], lands[i].at[q], (x, y, 1 - c))
                 for i in range(len(srcs)) for q in range(NCHIP)], [(x, y, 1 - c)])

    return _split_exchange(name, grads, [_sds((NCHIP,) + g.shape[1:], g.dtype) for g in grads],
                           plan, NCHIP, collective_id)


SEM = pl.BlockSpec(memory_space=pltpu.SEMAPHORE)


def _split_exchange(name, srcs, land_sds, plan, ncopy, collective_id):
    n = len(srcs)
    nsem = n * ncopy
    effect = pltpu.SideEffectType.DATAFLOW_SIDE_EFFECTING

    def descriptors(src_refs, land_refs, send_sems, recv_sems):
        copies, peers = plan(src_refs, land_refs)
        return [pltpu.make_async_remote_copy(src_ref=s, dst_ref=d, send_sem=send_sems[i * ncopy + k],
                                             recv_sem=recv_sems[i * ncopy + k], device_id=to,
                                             device_id_type=MESH) for (i, k, s, d, to) in copies], peers

    def start_body(*refs):
        src_refs, land_refs = refs[:n], refs[n:2 * n]
        send_sems, recv_sems = refs[2 * n:2 * n + nsem], refs[2 * n + nsem:2 * n + 2 * nsem]
        token = refs[-1]
        cps, peers = descriptors(src_refs, land_refs, send_sems, recv_sems)
        barrier = pltpu.get_barrier_semaphore()
        for peer in peers:
            pl.semaphore_signal(barrier, inc=1, device_id=peer, device_id_type=MESH)
        pl.semaphore_wait(barrier, len(peers))
        for cp in cps:
            cp.start()
        token[...] = jnp.zeros_like(token)

    lands = [pltpu.with_memory_space_constraint(lax.empty(s.shape, s.dtype), pltpu.HBM) for s in land_sds]
    srcs = [pltpu.with_memory_space_constraint(s, pltpu.HBM) for s in srcs]
    res = pl.pallas_call(
        start_body, name=name + "_start",
        out_shape=(pltpu.SemaphoreType.DMA(()),) * (2 * nsem)
        + tuple(pltpu.HBM(s.shape, s.dtype) for s in srcs)
        + tuple(pltpu.HBM(s.shape, s.dtype) for s in land_sds) + (_sds((SUBLANES, LANES), F32),),
        in_specs=[HBM] * (2 * n),
        out_specs=(SEM,) * (2 * nsem) + (HBM,) * (2 * n) + (pl.BlockSpec(memory_space=pltpu.VMEM),),
        input_output_aliases={i: 2 * nsem + i for i in range(2 * n)},
        compiler_params=pltpu.CompilerParams(has_side_effects=effect, collective_id=collective_id),
    )(*srcs, *lands)
    sems = res[:2 * nsem]
    thru = res[2 * nsem:2 * nsem + 2 * n]
    token = res[-1]

    def wait(after):
        def wait_body(*refs):
            src_refs, land_refs = refs[:n], refs[n:2 * n]
            cps, _ = descriptors(src_refs, land_refs, refs[2 * n:2 * n + nsem],
                                 refs[2 * n + nsem:2 * n + 2 * nsem])
            for cp in cps:
                cp.wait_send()
            for cp in cps:
                cp.wait_recv()

        out = pl.pallas_call(
            wait_body, name=name + "_wait",
            out_shape=tuple(pltpu.HBM(s.shape, s.dtype) for s in srcs)
            + tuple(pltpu.HBM(s.shape, s.dtype) for s in land_sds),
            in_specs=[HBM] * (2 * n) + [SEM] * (2 * nsem) + [pl.BlockSpec(memory_space=pl.ANY)],
            out_specs=(HBM,) * (2 * n),
            input_output_aliases={i: i for i in range(2 * n)},
            compiler_params=pltpu.CompilerParams(has_side_effects=effect),
        )(*thru, *sems, after)
        return list(out[:n]), list(out[n:])

    return token, wait


def pair_sum(name, grad, got, place):
    shp = grad.shape[1:]
    r, cdim = shp[-2], shp[-1]
    lead = int(math.prod(shp[:-2])) if len(shp) > 2 else 1
    g5 = grad.reshape(NCHIP, 2, lead * r, cdim)
    t4 = got.reshape(NCHIP, lead * r, cdim)
    R = lead * r
    tr = _tile(R, max(8, (1 << 20) // cdim))

    def body(p_ref, g_ref, t_ref, o_ref):
        o_ref[...] = (g_ref[0].astype(F32) + t_ref[...].astype(F32)).astype(o_ref.dtype)

    out = pl.pallas_call(
        body, name=name,
        grid_spec=pltpu.PrefetchScalarGridSpec(
            num_scalar_prefetch=1, grid=(NCHIP - 1, R // tr),
            in_specs=[pl.BlockSpec((1, 1, tr, cdim), lambda j, i, p: (p[1] ^ (j + 1), p[0], i, 0)),
                      pl.BlockSpec((1, tr, cdim), lambda j, i, p: (p[1] ^ (j + 1), i, 0))],
            out_specs=pl.BlockSpec((1, tr, cdim), lambda j, i, p: (p[1] ^ (j + 1), i, 0))),
        out_shape=_sds((NCHIP, R, cdim), grad.dtype),
        compiler_params=_params(("parallel", "parallel")),
    )(place, g5, t4)
    return out


def chip_exchange(name, parts, collective_id):
    def plan(srcs, lands):
        x, y, c, chips = _place()
        return ([(i, j, srcs[i].at[2 * chip[0] + chip[1]], lands[i].at[j], (*chip, c))
                 for i in range(len(srcs)) for j, chip in enumerate(chips)],
                [(*chip, c) for chip in chips])

    return _split_exchange(name, parts, [_sds((3,) + p.shape[1:], p.dtype) for p in parts],
                           plan, 3, collective_id)


def ada_fwd(c_row, w_ada, b_ada):
    D, cols = w_ada.shape

    def body(c_ref, w_ref, b_ref, mod_ref, call_ref, act8, part, s1, r1, s2, r2):
        x, y, c, _ = _place()
        me = 4 * x + 2 * y + c
        call_ref[me] = c_ref[...]
        cps = []
        for k in range(1, NDEV):
            to = (x ^ (k >> 2), y ^ ((k >> 1) & 1), c ^ (k & 1))
            cps.append(pltpu.make_async_remote_copy(
                src_ref=c_ref, dst_ref=call_ref.at[me], send_sem=s1.at[k - 1],
                recv_sem=r1.at[k - 1], device_id=to, device_id_type=MESH))
            cps[-1].start()
        for cp in cps:
            cp.wait()
        for b in range(NDEV):
            act8[b:b + 1, :] = call_ref[b]
        cv = act8[...]
        act = (cv * _sigmoid(cv)).astype(BF16)
        res = jnp.dot(act, w_ref[...].astype(BF16), preferred_element_type=F32)
        for b in range(NDEV):
            part[b] = res[b:b + 1, :]
        mod_ref[me] = part[me]
        cps = []
        for k in range(1, NDEV):
            to = (x ^ (k >> 2), y ^ ((k >> 1) & 1), c ^ (k & 1))
            dst = 4 * to[0] + 2 * to[1] + to[2]
            cps.append(pltpu.make_async_remote_copy(
                src_ref=part.at[dst], dst_ref=mod_ref.at[me], send_sem=s2.at[k - 1],
                recv_sem=r2.at[k - 1], device_id=to, device_id_type=MESH))
            cps[-1].start()
        for cp in cps:
            cp.wait()
        for b in range(NDEV):
            mod_ref[b] = mod_ref[b] + b_ref[b]

    vm = pl.BlockSpec(memory_space=pltpu.VMEM)
    return pl.pallas_call(
        body, name="ada_fwd", in_specs=[vm, vm, vm], out_specs=[vm, vm],
        out_shape=[_sds((NDEV, 1, cols), F32), _sds((NDEV, 1, D), F32)],
        scratch_shapes=[pltpu.VMEM((NDEV, D), F32), pltpu.VMEM((NDEV, 1, cols), F32),
                        pltpu.SemaphoreType.DMA((NDEV - 1,)), pltpu.SemaphoreType.DMA((NDEV - 1,)),
                        pltpu.SemaphoreType.DMA((NDEV - 1,)), pltpu.SemaphoreType.DMA((NDEV - 1,))],
        compiler_params=pltpu.CompilerParams(vmem_limit_bytes=VMEM_LIMIT),
    )(c_row, w_ada, b_ada.reshape(NDEV, 1, cols))


def _adamw_math(g, w, m, v):
    m2 = ADAM_B1 * m + (1.0 - ADAM_B1) * g
    v2 = ADAM_B2 * v + (1.0 - ADAM_B2) * (g * g)
    m_hat = m2 / (1.0 - ADAM_B1 ** ADAM_STEP)
    v_hat = v2 / (1.0 - ADAM_B2 ** ADAM_STEP)
    delta = -ADAM_LR * (m_hat / (jnp.sqrt(v_hat) + ADAM_EPS) + ADAM_WD * w)
    return delta, m2, v2


def adamw_sharded(name, grad8, pair4, got3, w, m, v, place):
    shape = w.shape
    cdim = shape[-1]
    R = int(math.prod(shape[:-1]))
    w2, m2, v2 = (t.reshape(R, cdim) for t in (w, m, v))
    tr = _tile(R, max(8, (1 << 19) // cdim))

    def body(q_ref, own_ref, sib_ref, t_ref, w_ref, m_ref, v_ref, g_out, d_out, m_out, v_out):
        g = own_ref[0].astype(F32) + sib_ref[0].astype(F32)
        for j in range(3):
            g = g + t_ref[j].astype(F32)
        d, mn, vn = _adamw_math(g, w_ref[...], m_ref[...], v_ref[...])
        g_out[...] = g
        d_out[...] = d
        m_out[...] = mn
        v_out[...] = vn

    spec = pl.BlockSpec((tr, cdim), lambda i, qr: (i, 0))
    outs = pl.pallas_call(
        body, name=name,
        grid_spec=pltpu.PrefetchScalarGridSpec(
            num_scalar_prefetch=1, grid=(R // tr,),
            in_specs=[pl.BlockSpec((1, tr, cdim), lambda i, qr: (qr[2], i, 0)),
                      pl.BlockSpec((1, tr, cdim), lambda i, qr: (qr[1], i, 0)),
                      pl.BlockSpec((3, tr, cdim), lambda i, qr: (0, i, 0)), spec, spec, spec],
            out_specs=[spec] * 4),
        out_shape=[_sds((R, cdim), F32)] * 4,
        compiler_params=_params(("parallel",)),
    )(place, grad8.reshape(NDEV, R, cdim), pair4.reshape(NCHIP, R, cdim),
      got3.reshape(3, R, cdim), w2, m2, v2)
    return [o.reshape(shape) for o in outs]


def adamw_small(parts, w, m, v, after=()):
    R = w.shape[0]
    tr = R

    def body(p_ref, w_ref, m_ref, v_ref, g_out, d_out, m_out, v_out):
        g = p_ref[0]
        for j in range(1, NDEV):
            g = g + p_ref[j]
        d, mn, vn = _adamw_math(g, w_ref[...], m_ref[...], v_ref[...])
        g_out[...] = g
        d_out[...] = d
        m_out[...] = mn
        v_out[...] = vn

    spec = pl.BlockSpec((tr, LANES), lambda i: (i, 0))
    return pl.pallas_call(
        _with_after(body, 4, after), name="adamw_small", grid=(R // tr,),
        in_specs=[pl.BlockSpec((NDEV, tr, LANES), lambda i: (0, i, 0)), spec, spec, spec]
        + [ANY] * len(after),
        out_specs=[spec] * 4, out_shape=[_sds((R, LANES), F32)] * 4,
        compiler_params=_params(("parallel",)),
    )(parts, w, m, v, *after)


def adamw_ada(c_all_t, dmod_all, w, m, v, my_dev):
    D, cols = w.shape
    tr = _tile(D, 256)

    def body(k_ref, c_ref, d_ref, w_ref, m_ref, v_ref, g_out, d_out, m_out, v_out):
        cv = c_ref[...]
        act = cv * _sigmoid(cv)
        dm = d_ref[...]
        g = act[:, 0:1] * dm[0:1, :]
        for b in range(1, NDEV):
            g = g + act[:, b:b + 1] * dm[b:b + 1, :]
        d, mn, vn = _adamw_math(g, w_ref[...], m_ref[...], v_ref[...])
        g_out[...] = g
        d_out[...] = d
        m_out[...] = mn
        v_out[...] = vn

    spec = pl.BlockSpec((tr, cols), lambda i, kr: (i, 0))
    return pl.pallas_call(
        body, name="adamw_ada",
        grid_spec=pltpu.PrefetchScalarGridSpec(
            num_scalar_prefetch=1, grid=(D // tr,),
            in_specs=[pl.BlockSpec((tr, NDEV), lambda i, kr: (i, 0)),
                      pl.BlockSpec((NDEV, cols), lambda i, kr: (0, kr[0])), spec, spec, spec],
            out_specs=[spec] * 4),
        out_shape=[_sds((D, cols), F32)] * 4,
        compiler_params=_params(("parallel",)),
    )(my_dev, c_all_t, dmod_all, w, m, v)


def _small_pack(parts):
    rows = []
    for p in parts:
        flat = p.reshape(-1)
        flat = jnp.pad(flat, (0, (-flat.shape[0]) % (SUBLANES * LANES)))
        rows.append(flat.reshape(-1, LANES))
    return jnp.concatenate(rows, axis=0)


def _small_unpack(buf, shapes):
    out, r = [], 0
    for s in shapes:
        n = int(math.prod(s))
        nr = -(-n // (SUBLANES * LANES)) * SUBLANES
        out.append(buf[r:r + nr].reshape(-1)[:n].reshape(s))
        r += nr
    return out


def kernel(x, c, w_ada, b_ada, w_in, lam_re, lam_im, log_dt, ssm_b_re, ssm_b_im, ssm_c_re, ssm_c_im, ssm_d, w_glu_val, w_glu_gate, w_pool, pool_scale, w_pool_out, w_out, ln1_g, ln1_b, w_ff1, w_ff2, ln2_g, ln2_b, loss_target, m_w_ada, m_b_ada, m_w_in, m_lam_re, m_lam_im, m_log_dt, m_ssm_b_re, m_ssm_b_im, m_ssm_c_re, m_ssm_c_im, m_ssm_d, m_w_glu_val, m_w_glu_gate, m_w_pool, m_pool_scale, m_w_pool_out, m_w_out, m_ln1_g, m_ln1_b, m_w_ff1, m_w_ff2, m_ln2_g, m_ln2_b, v_w_ada, v_b_ada, v_w_in, v_lam_re, v_lam_im, v_log_dt, v_ssm_b_re, v_ssm_b_im, v_ssm_c_re, v_ssm_c_im, v_ssm_d, v_w_glu_val, v_w_glu_gate, v_w_pool, v_pool_scale, v_w_pool_out, v_w_out, v_ln1_g, v_ln1_b, v_w_ff1, v_w_ff2, v_ln2_g, v_ln2_b):
    S, D = x.shape[1], x.shape[2]
    x2d, tgt = x[0], loss_target[0]
    W = D // 2
    G = W // SSM_GROUP
    P, H, GPB = SSM_STATE, SSM_GROUP, GROUPS_PER_BLOCK
    nblk = G // GPB
    gw = W // len(POOL_WINDOWS)
    ax, ay, ac = lax.axis_index("x"), lax.axis_index("y"), lax.axis_index("c")
    my_dev = (4 * ax + 2 * ay + ac).astype(jnp.int32).reshape(1)
    place = jnp.stack([ac, 2 * ax + ay, 4 * ax + 2 * ay + ac]).astype(jnp.int32)
    ts = _tile(S, 256)

    glu = jnp.stack([w_glu_val[0], w_glu_gate[0]]).astype(BF16)
    shards = [w_in[0].astype(BF16), glu, w_pool[0].astype(BF16), w_pool_out[0].astype(BF16),
              w_out[0].astype(BF16), w_ff1[0].astype(BF16), w_ff2[0].astype(BF16)]
    wg_in, wg_pool = seq_all_gather("gather_w_in", [shards[0], shards[2]], 1)
    wg_vg, wg_po, wg_out = seq_all_gather("gather_w_mix", [shards[1], shards[3], shards[4]], 2)
    wg_ff1, wg_ff2 = seq_all_gather("gather_w_ff", shards[5:7], 3)
    wg_vg = wg_vg.reshape(2 * NDEV, W, D // NDEV)
    nwin = len(POOL_WINDOWS)
    wp_full = jnp.transpose(wg_pool, (1, 0, 2, 3)).reshape(nwin, gw, gw)
    wout_full = wg_out.reshape(1, D, D)
    wff2_full = wg_ff2.reshape(1, 4 * D, D)

    small_names = [b_ada, lam_re, lam_im, log_dt, ssm_b_re, ssm_b_im, ssm_c_re, ssm_c_im, ssm_d,
                   pool_scale, ln1_g, ln1_b, ln2_g, ln2_b]
    small_m = [m_b_ada, m_lam_re, m_lam_im, m_log_dt, m_ssm_b_re, m_ssm_b_im, m_ssm_c_re, m_ssm_c_im,
               m_ssm_d, m_pool_scale, m_ln1_g, m_ln1_b, m_ln2_g, m_ln2_b]
    small_v = [v_b_ada, v_lam_re, v_lam_im, v_log_dt, v_ssm_b_re, v_ssm_b_im, v_ssm_c_re, v_ssm_c_im,
               v_ssm_d, v_pool_scale, v_ln1_g, v_ln1_b, v_ln2_g, v_ln2_b]
    zero_row = jnp.zeros((1, LANES), F32)
    packed_wmv = [_small_pack(t + [zero_row]) for t in (small_names, small_m, small_v)]

    mod, c_all = ada_fwd(c, w_ada[0], b_ada)
    mod = mod.reshape(6, 1, D)
    sh1, sc1, g1, sh2, sc2, g2 = (mod[i] for i in range(6))

    f2, kconst = s5_disc(lam_re[0], lam_im[0], log_dt[0].reshape(G, 1))
    kconst = kconst.reshape(16, SUBLANES, G * P)
    f2r = f2.reshape(2, 1, G * P)
    bt_re = jnp.transpose(ssm_b_re[0], (2, 0, 1)).reshape(H, G * P)
    bt_im = jnp.transpose(ssm_b_im[0], (2, 0, 1)).reshape(H, G * P)
    ct_re = jnp.transpose(ssm_c_re[0], (1, 0, 2)).reshape(H, G * P)
    ct_im = jnp.transpose(ssm_c_im[0], (1, 0, 2)).reshape(H, G * P)
    s5_params = (f2r, bt_re, bt_im, ct_re, ct_im, ssm_d, kconst)

    def e1(t, b):
        xhat, _ = _ln_stats(t[0])
        return [xhat * (1.0 + b[0]) + b[1]], []
    (h1,) = _rowwise("ln_mod1", e1, S, ts, [(x2d, D, 0)], [sc1, sh1], [(D, BF16)], [])

    (proj,) = mm_nn("proj", h1, wg_in, F32, 2)
    z, xsb_all = s5_fwd(proj, s5_params, nblk)
    (vt,) = mm_nn("glu", z, wg_vg, BF16, 4)
    pooled = pool_fwd(proj, W, W, gw)

    def pool_epi(vals, ex, outs):
        a = vals[0]
        outs[0][...] = a
        outs[1][...] = (a * ex[0][...]).astype(BF16)
    tmp = _tile(S, 1024)
    yp, ypool = _mm(
        "pool_mix", "nn", pooled, wp_full.astype(BF16), (S // tmp, nwin, 1),
        pl.BlockSpec((tmp, gw), lambda i, j, k: (i, j)), pl.BlockSpec((1, gw, gw), lambda i, j, k: (j, 0, 0)),
        [(_sds((S, W), F32), pl.BlockSpec((tmp, gw), lambda i, j, k: (i, j))),
         (_sds((S, W), BF16), pl.BlockSpec((tmp, gw), lambda i, j, k: (i, j)))],
        (tmp, gw), 1, gw, None, pool_epi,
        [(pool_scale, pl.BlockSpec((1, gw), lambda i, j, k: (0, j)))])
    (y_b,) = mm_nn("pool_out", ypool, wg_po, BF16, 4)

    cb = D // NDEV
    ga_cb, gb_cb = (2 * W) // cb, (2 * W + D) // cb
    tsm = _tile(S, 512)

    def merge_call(name, fn, ins, n_out, after=()):
        def body(*refs):
            vals = [r[...].astype(F32) for r in refs[:len(ins)]]
            for r, v in zip(refs[len(ins):], fn(*vals)):
                r[...] = v.astype(r.dtype)
        return pl.pallas_call(
            _with_after(body, len(ins), after), name=name, grid=(S // tsm, NDEV),
            in_specs=[pl.BlockSpec((tsm, w), f) for (_, w, f) in ins] + [ANY] * len(after),
            out_specs=[pl.BlockSpec((tsm, w), lambda i, j: (i, j)) for (_, w) in n_out],
            out_shape=[_sds((S, cols), BF16) for (cols, _) in n_out],
            compiler_params=_params(("parallel", "parallel")),
        )(*[a for (a, _, _) in ins], *after)

    merge_ins = [(proj, cb, lambda i, j: (i, ga_cb + j)), (proj, cb, lambda i, j: (i, gb_cb + j)),
                 (vt, 2 * cb, lambda i, j: (i, j)), (y_b, cb, lambda i, j: (i, j))]

    def merge_f(ga, gb, vtv, yb):
        return [_sigmoid(ga) * (vtv[:, :cb] * _sigmoid(vtv[:, cb:])) + _sigmoid(gb) * yb]
    (merged,) = merge_call("merge", merge_f, merge_ins, [(D, cb)])

    (mix,) = mm_nn("mix_out", merged, wout_full, F32, 1)

    def e3(t, b):
        xv, mx = t
        g1v, l1g, l1b, sc2v, sh2v = b
        r1 = ALPHA * xv + g1v * mx
        xh1, _ = _ln_stats(r1)
        x1 = xh1 * l1g + l1b
        xh, _ = _ln_stats(x1)
        return [r1, xh * (1.0 + sc2v) + sh2v], []
    r1, h2 = _rowwise("post_mix", e3, S, ts, [(x2d, D, 0), (mix, D, 0)],
                      [g1, ln1_g, ln1_b, sc2, sh2], [(D, F32), (D, BF16)], [], after=packed_wmv)

    def relu_epi(vals, ex, outs):
        outs[0][...] = jnp.maximum(vals[0], 0.0).astype(BF16)
    (rl,) = mm_nn("ff1", h2, wg_ff1, BF16, 1, epi=relu_epi)

    def square(a):
        return a * a
    (y2,) = mm_nn("ff2", rl, wff2_full, F32, 1, pro=square)

    def e4(t, b):
        r1v, y2v, tg = t
        g2v, l1g, l1b, l2g, l2b = b
        xh1, _ = _ln_stats(r1v)
        x1 = xh1 * l1g + l1b
        r2 = ALPHA * x1 + g2v * y2v
        xh2, rs2 = _ln_stats(r2)
        err = xh2 * l2g + l2b - tg
        dx2 = err * (1.0 / D)
        dr2 = _ln_bwd(dx2 * l2g, xh2, rs2)
        lsum = jnp.sum(_colsum(err * err), axis=1, keepdims=True) * (0.5 / D)
        return ([ALPHA * dr2, g2v * dr2],
                [jnp.broadcast_to(lsum, (1, LANES)), _colsum(dx2 * xh2), _colsum(dx2), _colsum(dr2 * y2v)])
    dx1a, dy2, loss_acc, g_ln2g, g_ln2b, d_g2 = _rowwise(
        "head", e4, S, ts, [(r1, D, 0), (y2, D, 0), (tgt, D, 0)], [g2, ln1_g, ln1_b, ln2_g, ln2_b],
        [(D, F32), (D, BF16)], [LANES, D, D, D])

    tn_ff = _tile(4 * D, 1024)

    def dff_epi(vals, ex, outs):
        outs[0][...] = (vals[0] * (2.0 * ex[0][...].astype(F32))).astype(BF16)
    tmf = _tile(S, 1024)
    (da1,) = mm_nt("d_ff2", dy2, wff2_full, BF16, 1, tn=tn_ff, epi=dff_epi,
                   extras=[(rl, pl.BlockSpec((tmf, tn_ff), lambda i, j, k: (i, j)))])
    gw_ff2 = mm_tn("gw_ff2", rl, dy2, BF16, NDEV, 0, pro=square)
    gw_ff1 = mm_tn("gw_ff1", h2, da1, BF16, NDEV, 1)
    tok, wait_pair_a = pair_exchange("pair_exchange_ff", [gw_ff2, gw_ff1], 4)
    (dh2,) = mm_nt("d_ff1", da1, wg_ff1, F32, 4, after=[tok])

    def e5(t, b):
        dh2v, r1v, dx1av, mx = t
        sc2v, l1g, l1b, g1v = b
        xh1, rs1 = _ln_stats(r1v)
        x1 = xh1 * l1g + l1b
        xh, rs = _ln_stats(x1)
        dx1 = dx1av + _ln_bwd(dh2v * (1.0 + sc2v), xh, rs)
        dr1 = _ln_bwd(dx1 * l1g, xh1, rs1)
        return ([ALPHA * dr1, g1v * dr1],
                [_colsum(dh2v * xh), _colsum(dh2v), _colsum(dx1 * xh1), _colsum(dx1), _colsum(dr1 * mx)])
    dxa, dmix, d_sc2, d_sh2, g_ln1g, g_ln1b, d_g1 = _rowwise(
        "post_mix_bwd", e5, S, ts, [(dh2, D, 0), (r1, D, 0), (dx1a, D, 0), (mix, D, 0)],
        [sc2, ln1_g, ln1_b, g1], [(D, F32), (D, BF16)], [D, D, D, D, D])

    (dmerged,) = mm_nt("d_mix_out", dmix, wout_full, BF16, 1)
    gw_out = mm_tn("gw_out", merged, dmix, BF16, NDEV, 0)
    grads_a, got_a = wait_pair_a(gw_out)
    parts_a = [pair_sum("pair_sum_ff%d" % i, g, t, place) for i, (g, t) in enumerate(zip(grads_a, got_a))]
    tok, wait_chip_a = chip_exchange("chip_exchange_ff", parts_a, 5)

    def merge_b(ga, gb, vtv, yb, dm):
        vv, tt = vtv[:, :cb], vtv[:, cb:]
        sa, sb, st = _sigmoid(ga), _sigmoid(gb), _sigmoid(tt)
        dya = dm * sa
        return [dm * (vv * st) * sa * (1.0 - sa), dm * yb * sb * (1.0 - sb),
                jnp.concatenate([dya * st, dya * vv * st * (1.0 - st)], axis=1), dm * sb]
    dga, dgb_, dvt, dy_b = merge_call(
        "merge_bwd", merge_b, merge_ins + [(dmerged, cb, lambda i, j: (i, j))],
        [(D, cb), (D, cb), (2 * D, 2 * cb), (D, cb)], after=[tok])

    (dypool,) = mm_nt("d_pool_out", dy_b, wg_po, F32, NDEV)
    gw_po = mm_tn("gw_pool_out", ypool, dy_b, BF16, NDEV, 4)

    def e7(t, b):
        return [t[0] * b[0]], [_colsum(t[0] * t[1])]
    dyp, g_pscale = _rowwise("pool_scale_bwd", e7, S, ts, [(dypool, W, 0), (yp, W, 0)],
                             [pool_scale], [(W, BF16)], [W])
    (dpooled,) = _mm(
        "d_pool_mix", "nt", dyp, wp_full.astype(BF16), (S // tmp, nwin, 1),
        pl.BlockSpec((tmp, gw), lambda i, j, k: (i, j)), pl.BlockSpec((1, gw, gw), lambda i, j, k: (j, 0, 0)),
        [(_sds((S, W), F32), pl.BlockSpec((tmp, gw), lambda i, j, k: (i, j)))], (tmp, gw), 1, gw)
    tkp = _tile(S, 2048)
    gw_pool = _mm(
        "gw_pool", "tn", pooled, dyp, (nwin, 1, S // tkp),
        pl.BlockSpec((tkp, gw), lambda i, j, k: (k, i)), pl.BlockSpec((tkp, gw), lambda i, j, k: (k, i)),
        [(_sds((nwin, gw, gw), BF16), pl.BlockSpec((1, gw, gw), lambda i, j, k: (i, 0, 0)))],
        (gw, gw), 1, gw, stacked_out=True)[0]
    du_pool = pool_bwd(dpooled, gw)

    (dz,) = mm_nt("d_glu", dvt, wg_vg, BF16, 2 * NDEV)
    gw_vg = mm_tn("gw_glu", z, dvt, BF16, 2 * NDEV, 4)
    gw_pool_st = jnp.transpose(gw_pool.reshape(nwin, NDEV, gw // NDEV, gw), (1, 0, 2, 3))
    grads_b = [gw_out, gw_po, gw_pool_st, gw_vg.reshape(NDEV, 2, W, D // NDEV)]
    tok, wait_pair_b = pair_exchange("pair_exchange_mix", grads_b, 6)
    du_ssm, g_bt_re, g_bt_im, g_ct_re, g_ct_im, g_f, g_d, g_a = s5_bwd(
        proj, xsb_all, dz, s5_params, nblk, after=[tok])
    grads_b, got_b = wait_pair_b(du_ssm)
    parts_b = [pair_sum("pair_sum_mix%d" % i, g, t, place) for i, (g, t) in enumerate(zip(grads_b, got_b))]
    tok, wait_chip_b = chip_exchange("chip_exchange_mix", parts_b, 7)

    dproj = jnp.concatenate([du_ssm, du_pool, dga, dgb_], axis=1)
    gw_in = mm_tn("gw_in", h1, dproj, BF16, NDEV, 1, after=[tok])
    tok, wait_pair_c = pair_exchange("pair_exchange_in", [gw_in], 8)
    (dh1,) = mm_nt("d_proj", dproj, wg_in, F32, 4, after=[tok])
    grads_c, got_c = wait_pair_c(dh1)
    parts_c = [pair_sum("pair_sum_in", grads_c[0], got_c[0], place)]
    tok, wait_chip_c = chip_exchange("chip_exchange_in", parts_c, 9)

    def e10(t, b):
        dh1v, xv, dxav = t
        xh, rs = _ln_stats(xv)
        return ([dxav + _ln_bwd(dh1v * (1.0 + b[0]), xh, rs)],
                [_colsum(dh1v * xh), _colsum(dh1v)])
    grad_x, d_sc1, d_sh1 = _rowwise("ln_mod1_bwd", e10, S, ts, [(dh1, D, 0), (x2d, D, 0), (dxa, D, 0)],
                                    [sc1], [(D, F32)], [D, D], after=[tok])

    g_b_re = jnp.transpose(g_bt_re.reshape(H, G, P), (1, 2, 0))
    g_b_im = jnp.transpose(g_bt_im.reshape(H, G, P), (1, 2, 0))
    g_c_re = jnp.transpose(g_ct_re.reshape(H, G, P), (1, 0, 2))
    g_c_im = jnp.transpose(g_ct_im.reshape(H, G, P), (1, 0, 2))
    d_ab = jnp.transpose(g_a.reshape(nblk, 2, GPB, P), (1, 0, 2, 3)).reshape(2, G, P)
    g_lr, g_li, g_ldt = s5_disc_bwd(lam_re[0], lam_im[0], log_dt[0].reshape(G, 1), d_ab,
                                    g_f.reshape(2, G, P))

    dmod = jnp.concatenate([d_sh1, d_sc1, d_g1, d_sh2, d_sc2, d_g2], axis=1)
    small_g = [dmod, g_lr, g_li, g_ldt, g_b_re, g_b_im, g_c_re, g_c_im, g_d, g_pscale,
               g_ln1g, g_ln1b, g_ln2g, g_ln2b, loss_acc]
    packed_g = _small_pack(small_g)
    (parts_all,) = seq_all_gather("gather_small", [packed_g], 10, routed=False)
    _, got3_a = wait_chip_a(packed_g)
    glu_w = jnp.stack([w_glu_val[0], w_glu_gate[0]])
    glu_m = jnp.stack([m_w_glu_val[0], m_w_glu_gate[0]])
    glu_v = jnp.stack([v_w_glu_val[0], v_w_glu_gate[0]])
    wmv = [(w_ff2[0], m_w_ff2[0], v_w_ff2[0]), (w_ff1[0], m_w_ff1[0], v_w_ff1[0]),
           (w_out[0], m_w_out[0], v_w_out[0]), (w_pool_out[0], m_w_pool_out[0], v_w_pool_out[0]),
           (w_pool[0], m_w_pool[0], v_w_pool[0]), (glu_w, glu_m, glu_v)]
    upd = [adamw_sharded("adamw_%d" % i, g, p, t, w, m, v, place)
           for i, (g, p, t, (w, m, v)) in enumerate(zip(grads_a, got_a, got3_a, wmv[:2]))]
    _, got3_b = wait_chip_b(upd[-1][0])
    upd += [adamw_sharded("adamw_%d" % (2 + i), g, p, t, w, m, v, place)
            for i, (g, p, t, (w, m, v)) in enumerate(zip(grads_b, got_b, got3_b, wmv[2:]))]
    u_ff2, u_ff1, u_out, u_po, u_pool, u_glu = upd

    sg, sd, sm, sv = adamw_small(parts_all, *packed_wmv, after=[upd[-1][0]])
    shapes = [t.shape for t in small_names]
    loss = _small_unpack(sg, shapes + [(1, LANES)])[-1][0, 0]
    sg, sd, sm, sv = (_small_unpack(t, shapes) for t in (sg, sd, sm, sv))

    nmod = 6 * D
    dmod_all = parts_all[:, :nmod // LANES, :].reshape(NDEV, nmod)
    c_all_t = jnp.transpose(c_all.reshape(NDEV, D))
    ada_out = adamw_ada(c_all_t, dmod_all, w_ada[0], m_w_ada[0], v_w_ada[0], my_dev)
    _, got3_c = wait_chip_c(ada_out[0])
    u_in = adamw_sharded("adamw_6", grads_c[0], got_c[0], got3_c[0], w_in[0], m_w_in[0], v_w_in[0], place)

    def pick(k):
        return [ada_out[k][None], sg_sd[k][0], u_in[k][None]] + [t for t in sg_sd[k][1:9]] + \
               [u_glu[k][0][None], u_glu[k][1][None], u_pool[k][None], sg_sd[k][9], u_po[k][None],
                u_out[k][None], sg_sd[k][10], sg_sd[k][11], u_ff1[k][None], u_ff2[k][None],
                sg_sd[k][12], sg_sd[k][13]]

    sg_sd = [sg, sd, sm, sv]
    return (loss, grad_x[None], *pick(0), *pick(1), *pick(2), *pick(3))
```

```python
import functools
import math

import jax
import jax.numpy as jnp
from jax import lax
from jax.experimental import pallas as pl
from jax.experimental.pallas import tpu as pltpu
from jax.experimental.pallas import tpu_sc as plsc

F32 = jnp.float32
BF16 = jnp.bfloat16
MESH = pl.DeviceIdType.MESH
NDEV = 8
NCHIP = 4

SSM_GROUP = 16
SSM_STATE = 64
GROUPS_PER_BLOCK = 8
POOL_WINDOWS = (2, 4, 8, 16)
LN_EPS = 1e-5
ALPHA = 2.0 ** 0.25
ADAM_LR, ADAM_B1, ADAM_B2, ADAM_EPS, ADAM_WD, ADAM_STEP = 0.001, 0.9, 0.999, 1e-08, 0.01, 10
SUBLANES = 8
LANES = 128
VMEM_LIMIT = 56 * 1024 * 1024


def _params(sem=None, vmem=VMEM_LIMIT):
    return pltpu.CompilerParams(dimension_semantics=sem, vmem_limit_bytes=vmem)


def _tile(n, pref):
    if n <= pref:
        return n
    t = 1 << (pref.bit_length() - 1)
    while n % t:
        t //= 2
    return t


def _cast_epi(vals, ex, outs):
    c = vals[0].shape[1]
    for s, v in enumerate(vals):
        outs[0][:, s * c:(s + 1) * c] = v.astype(outs[0].dtype)


ANY = pl.BlockSpec(memory_space=pl.ANY)


def _with_after(body, n_in, after):
    if not after:
        return body
    n_af = len(after)

    def wrapped(*refs):
        return body(*refs[:n_in], *refs[n_in + n_af:])
    return wrapped


def _mm(name, kind, a, b, grid, a_spec, b_spec, outs, acc_shape, nsub=1, c=None,
        pro=None, epi=None, extras=(), stacked_out=False, after=()):
    nk = grid[2]
    n_ex, n_out = len(extras), len(outs)

    def finish(vals, ex, out_refs):
        if epi is not None:
            epi(vals, ex, out_refs)
        elif stacked_out:
            for s, v in enumerate(vals):
                out_refs[0][s] = v.astype(out_refs[0].dtype)
        else:
            _cast_epi(vals, ex, out_refs)

    def body(*refs):
        a_ref, b_ref = refs[0], refs[1]
        ex = refs[2:2 + n_ex]
        out_refs = refs[2 + n_ex:2 + n_ex + n_out]
        k = pl.program_id(2)
        av = a_ref[...]
        if pro is not None:
            av = pro(av)
        if kind == "nn":
            prods = [jnp.dot(av, b_ref[s], preferred_element_type=F32) for s in range(nsub)]
        elif kind == "nt":
            t = None
            for s in range(nsub):
                d = lax.dot_general(av[:, s * c:(s + 1) * c], b_ref[s], (((1,), (1,)), ((), ())),
                                    preferred_element_type=F32)
                t = d if t is None else t + d
            prods = [t]
        else:
            t = lax.dot_general(av, b_ref[...], (((0,), (0,)), ((), ())), preferred_element_type=F32)
            prods = [t[:, s * c:(s + 1) * c] for s in range(nsub)] if stacked_out else [t]
        if nk == 1:
            finish(prods, ex, out_refs)
            return
        acc = refs[-1]
        w = prods[0].shape[1]

        @pl.when(k == 0)
        def _():
            for s, p in enumerate(prods):
                acc[:, s * w:(s + 1) * w] = p

        @pl.when(jnp.logical_and(k > 0, k < nk - 1))
        def _():
            for s, p in enumerate(prods):
                acc[:, s * w:(s + 1) * w] += p

        @pl.when(k == nk - 1)
        def _():
            finish([acc[:, s * w:(s + 1) * w] + p for s, p in enumerate(prods)], ex, out_refs)

    res = pl.pallas_call(
        _with_after(body, 2 + n_ex, after), name=name, grid=grid,
        in_specs=[a_spec, b_spec] + [e[1] for e in extras] + [ANY] * len(after),
        out_specs=[o[1] for o in outs],
        out_shape=[o[0] for o in outs],
        scratch_shapes=[pltpu.VMEM(acc_shape, F32)] if nk > 1 else [],
        compiler_params=_params(("parallel", "parallel", "arbitrary")),
    )(a, b, *[e[0] for e in extras], *after)
    return res


def _sds(shape, dtype):
    return jax.ShapeDtypeStruct(shape, dtype)


def mm_nn(name, a, b3, out_dtype, nsub, tm=1024, tk=2048, tn=None, pro=None, epi=None,
          extras=(), extra_outs=(), a_col0=0, after=()):
    M = a.shape[0]
    nb, K, cdim = b3.shape
    tm, tk = _tile(M, tm), _tile(K, tk)
    if nb == 1:
        tn = _tile(cdim, tn or 1024)
        nsub, c, nj = 1, tn, cdim // tn
        b_spec = pl.BlockSpec((1, tk, tn), lambda i, j, k: (0, k, j))
        N = cdim
    else:
        c, nj, tn = cdim, nb // nsub, nsub * cdim
        b_spec = pl.BlockSpec((nsub, tk, cdim), lambda i, j, k: (j, k, 0))
        N = nb * cdim
    kb0 = a_col0 // tk
    a_spec = pl.BlockSpec((tm, tk), lambda i, j, k: (i, kb0 + k))
    grid = (M // tm, nj, K // tk)
    o_spec = pl.BlockSpec((tm, tn), lambda i, j, k: (i, j))
    outs = [(_sds((M, N), out_dtype), o_spec)] + [(_sds((M, N), d), o_spec) for d in extra_outs]
    return _mm(name, "nn", a, b3, grid, a_spec, b_spec, outs, (tm, tn), nsub, c, pro, epi, extras,
               after=after)


def mm_nt(name, a, b3, out_dtype, nsub, tm=1024, tn=1024, epi=None, extras=(), extra_outs=(),
          after=()):
    M = a.shape[0]
    nb, N, cdim = b3.shape
    tm, tn = _tile(M, tm), _tile(N, tn)
    if nb == 1:
        tk = _tile(cdim, 2048)
        nsub, c, nk = 1, tk, cdim // tk
        b_spec = pl.BlockSpec((1, tn, tk), lambda i, j, k: (0, j, k))
    else:
        c, nk, tk = cdim, nb // nsub, nsub * cdim
        b_spec = pl.BlockSpec((nsub, tn, cdim), lambda i, j, k: (k, j, 0))
    a_spec = pl.BlockSpec((tm, tk), lambda i, j, k: (i, k))
    grid = (M // tm, N // tn, nk)
    o_spec = pl.BlockSpec((tm, tn), lambda i, j, k: (i, j))
    outs = [(_sds((M, N), out_dtype), o_spec)] + [(_sds((M, N), d), o_spec) for d in extra_outs]
    return _mm(name, "nt", a, b3, grid, a_spec, b_spec, outs, (tm, tn), nsub, c, None, epi, extras,
               after=after)


def mm_tn(name, a, b, out_dtype, nb, nsub, tma=1024, tk=2048, pro=None, a_col0=0, a_cols=None,
          after=()):
    S = a.shape[0]
    Ka = a_cols or a.shape[1]
    N = b.shape[1]
    tk = _tile(S, tk)
    if nsub == 0:
        tma, tn = _tile(Ka, tma), _tile(N, 1024)
        grid = (Ka // tma, N // tn, S // tk)
        ab0 = a_col0 // tma
        res = _mm(name, "tn", a, b, grid, pl.BlockSpec((tk, tma), lambda i, j, k: (k, ab0 + i)),
                  pl.BlockSpec((tk, tn), lambda i, j, k: (k, j)),
                  [(_sds((Ka, N), out_dtype), pl.BlockSpec((tma, tn), lambda i, j, k: (i, j)))],
                  (tma, tn), 1, tn, pro, None, (), after=after)[0]
        return res.reshape(nb, Ka // nb, N)
    else:
        c = N // nb
        tma = _tile(Ka, tma)
        grid = (Ka // tma, nb // nsub, S // tk)
        o_spec = pl.BlockSpec((nsub, tma, c), lambda i, j, k: (j, i, 0))
        out = _sds((nb, Ka, c), out_dtype)
        nsub_k = nsub
        tn = nsub * c
        b_spec = pl.BlockSpec((tk, tn), lambda i, j, k: (k, j))
    ab0 = a_col0 // tma
    a_spec = pl.BlockSpec((tk, tma), lambda i, j, k: (k, ab0 + i))
    return _mm(name, "tn", a, b, grid, a_spec, b_spec, [(out, o_spec)], (tma, tn), nsub_k, c,
               pro, None, (), stacked_out=True, after=after)[0]


def _rowwise(name, fn, S, ts, tiled, bcast, tiled_out, acc_out, after=()):
    nt, nb, no, na = len(tiled), len(bcast), len(tiled_out), len(acc_out)

    def body(*refs):
        tin = [r[...] for r in refs[:nt]]
        bin_ = [r[...] for r in refs[nt:nt + nb]]
        o_refs = refs[nt + nb:nt + nb + no]
        a_refs = refs[nt + nb + no:]
        touts, aouts = fn(tin, bin_)
        for r, v in zip(o_refs, touts):
            r[...] = v.astype(r.dtype)
        i = pl.program_id(0)

        @pl.when(i == 0)
        def _():
            for r, v in zip(a_refs, aouts):
                r[...] = v

        @pl.when(i > 0)
        def _():
            for r, v in zip(a_refs, aouts):
                r[...] += v

    in_specs = [pl.BlockSpec((ts, w), functools.partial(lambda i, cb: (i, cb), cb=cb))
                for (_, w, cb) in tiled]
    in_specs += [pl.BlockSpec(b.shape, lambda i: (0, 0)) for b in bcast]
    out_specs = [pl.BlockSpec((ts, w), lambda i: (i, 0)) for (w, _) in tiled_out]
    out_specs += [pl.BlockSpec((1, w), lambda i: (0, 0)) for w in acc_out]
    out_shape = [_sds((S, w), d) for (w, d) in tiled_out] + [_sds((1, w), F32) for w in acc_out]
    return pl.pallas_call(
        _with_after(body, nt + nb, after), name=name, grid=(S // ts,),
        in_specs=in_specs + [ANY] * len(after), out_specs=out_specs,
        out_shape=out_shape, compiler_params=_params(("arbitrary",)),
    )(*[t[0] for t in tiled], *bcast, *after)


def _ln_stats(v):
    mu = jnp.mean(v, axis=-1, keepdims=True)
    vc = v - mu
    var = jnp.mean(vc * vc, axis=-1, keepdims=True)
    rstd = lax.rsqrt(var + LN_EPS)
    return vc * rstd, rstd


def _ln_bwd(dxhat, xhat, rstd):
    return rstd * (dxhat - jnp.mean(dxhat, axis=-1, keepdims=True)
                   - xhat * jnp.mean(dxhat * xhat, axis=-1, keepdims=True))


def _colsum(v):
    return jnp.sum(v, axis=0, keepdims=True)


def _sigmoid(v):
    return 1.0 / (1.0 + jnp.exp(-v))


_GELU_C = math.sqrt(2.0 / math.pi)


def _gelu(v):
    return 0.5 * v * (1.0 + jnp.tanh(_GELU_C * (v + 0.044715 * v * v * v)))


def _gelu_grad(v):
    t = jnp.tanh(_GELU_C * (v + 0.044715 * v * v * v))
    return 0.5 * (1.0 + t) + 0.5 * v * (1.0 - t * t) * _GELU_C * (1.0 + 3 * 0.044715 * v * v)


def _disc(lr, li, ldt):
    dt = jnp.exp(ldt)
    mag = jnp.exp(lr * dt)
    ang = li * dt
    ab_re = mag * jnp.cos(ang)
    ab_im = mag * jnp.sin(ang)
    num_re = ab_re - 1.0
    num_im = ab_im
    den = lr * lr + li * li
    f_re = (num_re * lr + num_im * li) / den
    f_im = (num_im * lr - num_re * li) / den
    return ab_re, ab_im, f_re, f_im


def _cmul(ar, ai, br, bi):
    return ar * br - ai * bi, ar * bi + ai * br


def s5_disc(lam_re, lam_im, log_dt):
    G, P = lam_re.shape

    def body(lr_ref, li_ref, ldt_ref, f_ref, k_ref):
        ab_re, ab_im, f_re, f_im = _disc(lr_ref[...], li_ref[...], ldt_ref[...])
        f_ref[0] = f_re
        f_ref[1] = f_im
        pr, pi = [ab_re], [ab_im]
        for _ in range(SUBLANES - 1):
            nr, ni = _cmul(pr[-1], pi[-1], ab_re, ab_im)
            pr.append(nr)
            pi.append(ni)
        zero = jnp.zeros_like(ab_re)
        for n, sh in enumerate((1, 2, 4)):
            for r in range(SUBLANES):
                k_ref[2 * n, r] = pr[sh - 1] if r >= sh else zero
                k_ref[2 * n + 1, r] = pi[sh - 1] if r >= sh else zero
                k_ref[8 + 2 * n, r] = pr[sh - 1] if r + sh < SUBLANES else zero
                k_ref[8 + 2 * n + 1, r] = -pi[sh - 1] if r + sh < SUBLANES else zero
        for r in range(SUBLANES):
            k_ref[6, r] = pr[r]
            k_ref[7, r] = pi[r]
            k_ref[14, r] = pr[SUBLANES - 1 - r]
            k_ref[15, r] = -pi[SUBLANES - 1 - r]

    vm = pl.BlockSpec(memory_space=pltpu.VMEM)
    return pl.pallas_call(
        body, name="s5_disc", in_specs=[vm, vm, vm], out_specs=[vm, vm],
        out_shape=[_sds((2, G, P), F32), _sds((16, SUBLANES, G, P), F32)],
    )(lam_re, lam_im, log_dt)


def s5_disc_bwd(lam_re, lam_im, log_dt, d_ab, d_f):
    G, P = lam_re.shape

    def body(lr_ref, li_ref, ldt_ref, dab_ref, df_ref, glr_ref, gli_ref, gdt_ref):
        _, vjp = jax.vjp(_disc, lr_ref[...], li_ref[...], ldt_ref[...])
        glr, gli, gdt = vjp((dab_ref[0], dab_ref[1], df_ref[0], df_ref[1]))
        glr_ref[...] = glr
        gli_ref[...] = gli
        gdt_ref[...] = gdt

    vm = pl.BlockSpec(memory_space=pltpu.VMEM)
    return pl.pallas_call(
        body, name="s5_disc_bwd", in_specs=[vm] * 5, out_specs=[vm] * 3,
        out_shape=[_sds((G, P), F32), _sds((G, P), F32), _sds((G, 1), F32)],
    )(lam_re, lam_im, log_dt, d_ab, d_f)


def _group_mask(cw, nst):
    row = lax.broadcasted_iota(jnp.int32, (cw, 2 * nst), 0) // SSM_GROUP
    col = (lax.broadcasted_iota(jnp.int32, (cw, 2 * nst), 1) % nst) // SSM_STATE
    return row == col


def _spread(t, mask):
    reps = mask.shape[0] // t.shape[0]
    return jnp.where(mask, jnp.tile(t, (reps, 1)), 0.0).astype(BF16)


def _gather_groups(t, mask):
    t = jnp.where(mask, t, 0.0)
    out = t[0:SSM_GROUP]
    for g in range(1, t.shape[0] // SSM_GROUP):
        out = out + t[g * SSM_GROUP:(g + 1) * SSM_GROUP]
    return out


def _s5_operands(f_ref, br_ref, bi_ref, cr_ref, ci_ref, mask):
    fr, fi = f_ref[0], f_ref[1]
    br, bi = br_ref[...], bi_ref[...]
    bm = _spread(jnp.concatenate([fr * br - fi * bi, fr * bi + fi * br], axis=1), mask)
    cm = _spread(jnp.concatenate([cr_ref[...], -ci_ref[...]], axis=1), mask)
    return bm, cm


def _scan_fwd(xs, k_ref, nst):
    ntile = xs.shape[0] // SUBLANES

    def step(t, carry):
        cr, ci = carry
        r0 = pl.multiple_of(t * SUBLANES, SUBLANES)
        xr = xs[pl.ds(r0, SUBLANES), 0:nst]
        xi = xs[pl.ds(r0, SUBLANES), nst:2 * nst]
        for n, sh in enumerate((1, 2, 4)):
            sr = pltpu.roll(xr, sh, 0)
            si = pltpu.roll(xi, sh, 0)
            mr, mi = k_ref[2 * n], k_ref[2 * n + 1]
            xr, xi = xr + mr * sr - mi * si, xi + mr * si + mi * sr
        pr, pi = k_ref[6], k_ref[7]
        xr, xi = xr + pr * cr - pi * ci, xi + pr * ci + pi * cr
        xs[pl.ds(r0, SUBLANES), 0:nst] = xr
        xs[pl.ds(r0, SUBLANES), nst:2 * nst] = xi
        return (jnp.broadcast_to(xr[SUBLANES - 1:SUBLANES, :], xr.shape),
                jnp.broadcast_to(xi[SUBLANES - 1:SUBLANES, :], xi.shape))

    zero = jnp.zeros((SUBLANES, nst), F32)
    lax.fori_loop(0, ntile, step, (zero, zero))


def _scan_bwd(g, xs, k_ref, nst):
    ntile = g.shape[0] // SUBLANES
    row = lax.broadcasted_iota(jnp.int32, (SUBLANES, nst), 0)

    def step(tt, carry):
        cr, ci, ar, ai = carry
        t = ntile - 1 - tt
        r0 = pl.multiple_of(t * SUBLANES, SUBLANES)
        gr = g[pl.ds(r0, SUBLANES), 0:nst]
        gi = g[pl.ds(r0, SUBLANES), nst:2 * nst]
        for n, sh in enumerate((1, 2, 4)):
            sr = pltpu.roll(gr, SUBLANES - sh, 0)
            si = pltpu.roll(gi, SUBLANES - sh, 0)
            mr, mi = k_ref[8 + 2 * n], k_ref[8 + 2 * n + 1]
            gr, gi = gr + mr * sr - mi * si, gi + mr * si + mi * sr
        qr, qi = k_ref[14], k_ref[15]
        gr, gi = gr + qr * cr - qi * ci, gi + qr * ci + qi * cr
        g[pl.ds(r0, SUBLANES), 0:nst] = gr
        g[pl.ds(r0, SUBLANES), nst:2 * nst] = gi
        p0 = pl.multiple_of(jnp.maximum(t - 1, 0) * SUBLANES, SUBLANES)
        live = (t > 0).astype(F32)
        xr = xs[pl.ds(r0, SUBLANES), 0:nst]
        xi = xs[pl.ds(r0, SUBLANES), nst:2 * nst]
        pr = xs[pl.ds(p0, SUBLANES), 0:nst][SUBLANES - 1:SUBLANES, :] * live
        pi = xs[pl.ds(p0, SUBLANES), nst:2 * nst][SUBLANES - 1:SUBLANES, :] * live
        xmr = jnp.where(row == 0, jnp.broadcast_to(pr, xr.shape), pltpu.roll(xr, 1, 0))
        xmi = jnp.where(row == 0, jnp.broadcast_to(pi, xi.shape), pltpu.roll(xi, 1, 0))
        ar = ar + gr * xmr + gi * xmi
        ai = ai + gi * xmr - gr * xmi
        return (jnp.broadcast_to(gr[0:1, :], gr.shape), jnp.broadcast_to(gi[0:1, :], gi.shape),
                ar, ai)

    zero = jnp.zeros((SUBLANES, nst), F32)
    _, _, ar, ai = lax.fori_loop(0, ntile, step, (zero, zero, zero, zero))
    return _colsum(ar), _colsum(ai)


def _s5_param_specs(cw, nst):
    hp = pl.BlockSpec((SSM_GROUP, nst), lambda b: (0, b))
    return [pl.BlockSpec((2, 1, nst), lambda b: (0, 0, b)), hp, hp, hp, hp,
            pl.BlockSpec((1, cw), lambda b: (0, b)),
            pl.BlockSpec((16, SUBLANES, nst), lambda b: (0, 0, b))]


def s5_fwd(proj, params, nb):
    S = proj.shape[0]
    nst = params[1].shape[1] // nb
    cw = nst // SSM_STATE * SSM_GROUP

    def body(u_ref, f_ref, br_ref, bi_ref, cr_ref, ci_ref, d_ref, k_ref, z_ref, xsb_ref, xs):
        bm, cm = _s5_operands(f_ref, br_ref, bi_ref, cr_ref, ci_ref, _group_mask(cw, nst))
        u = u_ref[...]
        xs[...] = jnp.dot(u.astype(BF16), bm, preferred_element_type=F32)
        _scan_fwd(xs, k_ref, nst)
        xsb = xs[...].astype(BF16)
        xsb_ref[...] = xsb
        y = lax.dot_general(xsb, cm, (((1,), (1,)), ((), ())), preferred_element_type=F32)
        z_ref[...] = _gelu(y + d_ref[...] * u).astype(BF16)

    return pl.pallas_call(
        body, name="s5_fwd", grid=(nb,),
        in_specs=[pl.BlockSpec((S, cw), lambda b: (0, b))] + _s5_param_specs(cw, nst),
        out_specs=[pl.BlockSpec((S, cw), lambda b: (0, b)), pl.BlockSpec((S, 2 * nst), lambda b: (0, b))],
        out_shape=[_sds((S, nb * cw), BF16), _sds((S, nb * 2 * nst), BF16)],
        scratch_shapes=[pltpu.VMEM((S, 2 * nst), F32)],
        compiler_params=_params(("arbitrary",)),
    )(proj, *params)


def s5_bwd(proj, xsb_all, dz, params, nb, after=()):
    S = proj.shape[0]
    nst = params[1].shape[1] // nb
    cw = nst // SSM_STATE * SSM_GROUP

    def body(u_ref, xsb_ref, dz_ref, f_ref, br_ref, bi_ref, cr_ref, ci_ref, d_ref, k_ref,
             du_ref, gbr_ref, gbi_ref, gcr_ref, gci_ref, gf_ref, gd_ref, ga_ref, xs, g):
        mask = _group_mask(cw, nst)
        bm, cm = _s5_operands(f_ref, br_ref, bi_ref, cr_ref, ci_ref, mask)
        u = u_ref[...]
        ub = u.astype(BF16)
        d = d_ref[...]
        xsb = xsb_ref[...]
        xs[...] = xsb.astype(F32)
        y = lax.dot_general(xsb, cm, (((1,), (1,)), ((), ())), preferred_element_type=F32) + d * u
        dy = dz_ref[...].astype(F32) * _gelu_grad(y)
        gd_ref[...] = _colsum(dy * u)
        dyb = dy.astype(BF16)
        gc = _gather_groups(lax.dot_general(dyb, xsb, (((0,), (0,)), ((), ())),
                                            preferred_element_type=F32), mask)
        gcr_ref[...] = gc[:, :nst]
        gci_ref[...] = -gc[:, nst:]
        g[...] = jnp.dot(dyb, cm, preferred_element_type=F32)
        ar, ai = _scan_bwd(g, xs, k_ref, nst)
        ga_ref[0, 0:1, :] = ar
        ga_ref[0, 1:2, :] = ai
        gb = g[...].astype(BF16)
        du = lax.dot_general(gb, bm, (((1,), (1,)), ((), ())), preferred_element_type=F32) + d * dy
        du_ref[...] = du.astype(BF16)
        gbb = _gather_groups(lax.dot_general(ub, gb, (((0,), (0,)), ((), ())),
                                             preferred_element_type=F32), mask)
        dr, di = gbb[:, :nst], gbb[:, nst:]
        fr, fi = f_ref[0], f_ref[1]
        br, bi = br_ref[...], bi_ref[...]
        gbr_ref[...] = fr * dr + fi * di
        gbi_ref[...] = fr * di - fi * dr
        gf_ref[0] = _colsum(dr * br + di * bi)
        gf_ref[1] = _colsum(di * br - dr * bi)

    hp = pl.BlockSpec((SSM_GROUP, nst), lambda b: (0, b))
    hp_sds = _sds((SSM_GROUP, nb * nst), F32)
    return pl.pallas_call(
        _with_after(body, 10, after), name="s5_bwd", grid=(nb,),
        in_specs=[pl.BlockSpec((S, cw), lambda b: (0, b)),
                  pl.BlockSpec((S, 2 * nst), lambda b: (0, b)),
                  pl.BlockSpec((S, cw), lambda b: (0, b))] + _s5_param_specs(cw, nst)
        + [ANY] * len(after),
        out_specs=[pl.BlockSpec((S, cw), lambda b: (0, b)), hp, hp, hp, hp,
                   pl.BlockSpec((2, 1, nst), lambda b: (0, 0, b)),
                   pl.BlockSpec((1, cw), lambda b: (0, b)),
                   pl.BlockSpec((1, 2, nst), lambda b: (b, 0, 0))],
        out_shape=[_sds((S, nb * cw), BF16), hp_sds, hp_sds, hp_sds, hp_sds,
                   _sds((2, 1, nb * nst), F32), _sds((1, nb * cw), F32), _sds((nb, 2, nst), F32)],
        scratch_shapes=[pltpu.VMEM((S, 2 * nst), F32), pltpu.VMEM((S, 2 * nst), F32)],
        compiler_params=_params(("arbitrary",)),
    )(proj, xsb_all, dz, *params, *after)


def _shift_rows(v, k, row, down):
    n = v.shape[0]
    if down:
        return jnp.where(row >= k, pltpu.roll(v, k, 0), 0.0)
    return jnp.where(row < n - k, pltpu.roll(v, n - k, 0), 0.0)


def _window(v, gi, row, down):
    sums = []
    s = v
    for k in (1, 2, 4, 8):
        s = s + _shift_rows(s, k, row, down)
        sums.append(s)
    out = sums[3]
    for n in (2, 1, 0):
        out = jnp.where(gi == n, sums[n], out)
    return out


def pool_fwd(proj, col0, width, gw):
    S = proj.shape[0]
    cb0 = col0 // gw

    def body(u_ref, o_ref):
        gi = pl.program_id(0)
        u = u_ref[...]
        row = lax.broadcasted_iota(jnp.int32, u.shape, 0)
        w = jnp.left_shift(2, gi)
        count = jnp.minimum(row + 1, w).astype(F32)
        o_ref[...] = (_window(u, gi, row, True) / count - u).astype(BF16)

    return pl.pallas_call(
        body, name="pool_fwd", grid=(len(POOL_WINDOWS),),
        in_specs=[pl.BlockSpec((S, gw), lambda g: (0, cb0 + g))],
        out_specs=pl.BlockSpec((S, gw), lambda g: (0, g)),
        out_shape=_sds((S, width), BF16), compiler_params=_params(("arbitrary",)),
    )(proj)


def pool_bwd(dpooled, gw):
    S, width = dpooled.shape

    def body(d_ref, o_ref):
        gi = pl.program_id(0)
        d = d_ref[...]
        row = lax.broadcasted_iota(jnp.int32, d.shape, 0)
        w = jnp.left_shift(2, gi)
        count = jnp.minimum(row + 1, w).astype(F32)
        o_ref[...] = (_window(d / count, gi, row, False) - d).astype(BF16)

    return pl.pallas_call(
        body, name="pool_bwd", grid=(len(POOL_WINDOWS),),
        in_specs=[pl.BlockSpec((S, gw), lambda g: (0, g))],
        out_specs=pl.BlockSpec((S, gw), lambda g: (0, g)),
        out_shape=_sds((S, width), BF16), compiler_params=_params(("arbitrary",)),
    )(dpooled)


def _place():
    x, y, c = lax.axis_index("x"), lax.axis_index("y"), lax.axis_index("c")
    chips = [(1 - x, y), (x, 1 - y), (1 - x, 1 - y)]
    return x, y, c, chips


HBM = pl.BlockSpec(memory_space=pltpu.HBM)


def _gather_body(n, handshake):
    def body(*refs):
        ins, outs = refs[:n], refs[n:2 * n]
        send_sems, recv_sems, local_sems = refs[2 * n:]
        x, y, c, chips = _place()
        if handshake:
            barrier = pltpu.get_barrier_semaphore()
            for peer in [(x, y, 1 - c)] + [(*chip, c) for chip in chips]:
                pl.semaphore_signal(barrier, inc=1, device_id=peer, device_id_type=MESH)
            pl.semaphore_wait(barrier, 4)
        me, sibling = (x, y, c), (x, y, 1 - c)

        def slot(i, p):
            return outs[i].at[4 * p[0] + 2 * p[1] + p[2]]

        def copy(i, k, block, to, src=None):
            return pltpu.make_async_remote_copy(
                src_ref=slot(i, block) if src is None else src, dst_ref=slot(i, block),
                send_sem=send_sems.at[i, k], recv_sem=recv_sems.at[i, k],
                device_id=to, device_id_type=MESH)

        started = []
        for i in range(n):
            for j, chip in enumerate(chips):
                started.append(copy(i, 1 + j, me, (*chip, c), src=ins[i]))
                started[-1].start()
        for i in range(n):
            started.append(copy(i, 0, me, sibling, src=ins[i]))
            started[-1].start()
        mine = [pltpu.make_async_copy(ins[i], slot(i, me), local_sems.at[i]) for i in range(n)]
        for cp in mine:
            cp.start()
        for i in range(n):
            for j, chip in enumerate(chips):
                copy(i, 1 + j, (*chip, c), me).wait_recv()
                started.append(copy(i, 4 + j, (*chip, c), sibling))
                started[-1].start()
        for i in range(n):
            copy(i, 0, sibling, me).wait_recv()
            for j, chip in enumerate(chips):
                copy(i, 4 + j, (*chip, 1 - c), me).wait_recv()
        for cp in started:
            cp.wait_send()
        for cp in mine:
            cp.wait()

    return body


def _routed_gather_body(n):
    def body(*refs):
        ins, outs = refs[:n], refs[n:2 * n]
        send_sems, recv_sems, local_sems = refs[2 * n:]
        x, y, c, (xn, yn, dg) = _place()
        me, sibling = (x, y, c), (x, y, 1 - c)
        barrier = pltpu.get_barrier_semaphore()
        for peer in (sibling, (*xn, c), (*yn, c)):
            pl.semaphore_signal(barrier, inc=1, device_id=peer, device_id_type=MESH)
        pl.semaphore_wait(barrier, 3)

        def piece(i, p, h):
            rows = ins[i].shape[0] // 2
            return outs[i].at[4 * p[0] + 2 * p[1] + p[2], pl.ds(h * rows, rows)]

        def copy(i, k, src, dst, to):
            return pltpu.make_async_remote_copy(src_ref=src, dst_ref=dst, send_sem=send_sems.at[i, k],
                                                recv_sem=recv_sems.at[i, k], device_id=to,
                                                device_id_type=MESH)

        started = []

        def go(cp):
            cp.start()
            started.append(cp)

        for i in range(n):
            rows = ins[i].shape[0] // 2
            for h in range(2):
                own = ins[i].at[pl.ds(h * rows, rows)]
                go(copy(i, 1 + h, own, piece(i, me, h), (*xn, c)))
                go(copy(i, 3 + h, own, piece(i, me, h), (*yn, c)))
        for i in range(n):
            go(copy(i, 0, ins[i], outs[i].at[4 * x + 2 * y + c], sibling))
        mine = [pltpu.make_async_copy(ins[i], outs[i].at[4 * x + 2 * y + c], local_sems.at[i])
                for i in range(n)]
        for cp in mine:
            cp.start()
        for i in range(n):
            for k, chip, h, onward, ksib in ((1, xn, 0, (5, yn), 7), (4, yn, 1, (6, xn), 10),
                                            (2, xn, 1, None, 8), (3, yn, 0, None, 9),
                                            (5, dg, 0, None, 11), (6, dg, 1, None, 12)):
                got = piece(i, (*chip, c), h)
                copy(i, k, got, got, me).wait_recv()
                if onward is not None:
                    go(copy(i, onward[0], got, got, (*onward[1], c)))
                go(copy(i, ksib, got, got, sibling))
        for i in range(n):
            block = outs[i].at[4 * x + 2 * y + 1 - c]
            copy(i, 0, block, block, me).wait_recv()
            for ksib, chip, h in ((7, xn, 0), (10, yn, 1), (8, xn, 1), (9, yn, 0), (11, dg, 0), (12, dg, 1)):
                got = piece(i, (*chip, 1 - c), h)
                copy(i, ksib, got, got, me).wait_recv()
        for cp in started:
            cp.wait_send()
        for cp in mine:
            cp.wait()

    return body


def _on_sequencer(name, body, arrays, out_sds, sems, collective_id):
    ins = [jax.new_ref(a, memory_space=pltpu.MemorySpace.HBM) for a in arrays]
    outs = [jax.empty_ref(s, memory_space=pltpu.MemorySpace.HBM) for s in out_sds]

    @pl.kernel(mesh=plsc.ScalarSubcoreMesh(axis_name="sequencer", num_cores=1), name=name,
               scratch_types=tuple(sems),
               compiler_params=pltpu.CompilerParams(collective_id=collective_id))
    def launch(*sem_refs):
        body(*ins, *outs, *sem_refs)

    launch()
    return [o[...] for o in outs]


def seq_all_gather(name, shards, collective_id, routed=True):
    n = len(shards)
    nsem = 13 if routed else 7
    return _on_sequencer(
        name, _routed_gather_body(n) if routed else _gather_body(n, True), shards,
        [_sds((NDEV,) + s.shape, s.dtype) for s in shards],
        [pltpu.SemaphoreType.DMA((n, nsem)), pltpu.SemaphoreType.DMA((n, nsem)),
         pltpu.SemaphoreType.DMA((n,))], collective_id)


def pair_exchange(name, grads, collective_id):
    def plan(srcs, lands):
        x, y, c, _ = _place()
        return ([(i, q, srcs[i].at[2 * q + 1 - c], lands[i].at[q], (x, y, 1 - c))
                 for i in range(len(srcs)) for q in range(NCHIP)], [(x, y, 1 - c)])

    return _split_exchange(name, grads, [_sds((NCHIP,) + g.shape[1:], g.dtype) for g in grads],
                           plan, NCHIP, collective_id)


SEM = pl.BlockSpec(memory_space=pltpu.SEMAPHORE)


def _split_exchange(name, srcs, land_sds, plan, ncopy, collective_id):
    n = len(srcs)
    nsem = n * ncopy
    effect = pltpu.SideEffectType.DATAFLOW_SIDE_EFFECTING

    def descriptors(src_refs, land_refs, send_sems, recv_sems):
        copies, peers = plan(src_refs, land_refs)
        return [pltpu.make_async_remote_copy(src_ref=s, dst_ref=d, send_sem=send_sems[i * ncopy + k],
                                             recv_sem=recv_sems[i * ncopy + k], device_id=to,
                                             device_id_type=MESH) for (i, k, s, d, to) in copies], peers

    def start_body(*refs):
        src_refs, land_refs = refs[:n], refs[n:2 * n]
        send_sems, recv_sems = refs[2 * n:2 * n + nsem], refs[2 * n + nsem:2 * n + 2 * nsem]
        token = refs[-1]
        cps, peers = descriptors(src_refs, land_refs, send_sems, recv_sems)
        barrier = pltpu.get_barrier_semaphore()
        for peer in peers:
            pl.semaphore_signal(barrier, inc=1, device_id=peer, device_id_type=MESH)
        pl.semaphore_wait(barrier, len(peers))
        for cp in cps:
            cp.start()
        token[...] = jnp.zeros_like(token)

    lands = [pltpu.with_memory_space_constraint(lax.empty(s.shape, s.dtype), pltpu.HBM) for s in land_sds]
    srcs = [pltpu.with_memory_space_constraint(s, pltpu.HBM) for s in srcs]
    res = pl.pallas_call(
        start_body, name=name + "_start",
        out_shape=(pltpu.SemaphoreType.DMA(()),) * (2 * nsem)
        + tuple(pltpu.HBM(s.shape, s.dtype) for s in srcs)
        + tuple(pltpu.HBM(s.shape, s.dtype) for s in land_sds) + (_sds((SUBLANES, LANES), F32),),
        in_specs=[HBM] * (2 * n),
        out_specs=(SEM,) * (2 * nsem) + (HBM,) * (2 * n) + (pl.BlockSpec(memory_space=pltpu.VMEM),),
        input_output_aliases={i: 2 * nsem + i for i in range(2 * n)},
        compiler_params=pltpu.CompilerParams(has_side_effects=effect, collective_id=collective_id),
    )(*srcs, *lands)
    sems = res[:2 * nsem]
    thru = res[2 * nsem:2 * nsem + 2 * n]
    token = res[-1]

    def wait(after):
        def wait_body(*refs):
            src_refs, land_refs = refs[:n], refs[n:2 * n]
            cps, _ = descriptors(src_refs, land_refs, refs[2 * n:2 * n + nsem],
                                 refs[2 * n + nsem:2 * n + 2 * nsem])
            for cp in cps:
                cp.wait_send()
            for cp in cps:
                cp.wait_recv()

        out = pl.pallas_call(
            wait_body, name=name + "_wait",
            out_shape=tuple(pltpu.HBM(s.shape, s.dtype) for s in srcs)
            + tuple(pltpu.HBM(s.shape, s.dtype) for s in land_sds),
            in_specs=[HBM] * (2 * n) + [SEM] * (2 * nsem) + [pl.BlockSpec(memory_space=pl.ANY)],
            out_specs=(HBM,) * (2 * n),
            input_output_aliases={i: i for i in range(2 * n)},
            compiler_params=pltpu.CompilerParams(has_side_effects=effect),
        )(*thru, *sems, after)
        return list(out[:n]), list(out[n:])

    return token, wait


def pair_sum(name, grad, got, place):
    shp = grad.shape[1:]
    r, cdim = shp[-2], shp[-1]
    lead = int(math.prod(shp[:-2])) if len(shp) > 2 else 1
    g5 = grad.reshape(NCHIP, 2, lead * r, cdim)
    t4 = got.reshape(NCHIP, lead * r, cdim)
    R = lead * r
    tr = _tile(R, max(8, (1 << 20) // cdim))

    def body(p_ref, g_ref, t_ref, o_ref):
        o_ref[...] = (g_ref[0].astype(F32) + t_ref[...].astype(F32)).astype(o_ref.dtype)

    out = pl.pallas_call(
        body, name=name,
        grid_spec=pltpu.PrefetchScalarGridSpec(
            num_scalar_prefetch=1, grid=(NCHIP - 1, R // tr),
            in_specs=[pl.BlockSpec((1, 1, tr, cdim), lambda j, i, p: (p[1] ^ (j + 1), p[0], i, 0)),
                      pl.BlockSpec((1, tr, cdim), lambda j, i, p: (p[1] ^ (j + 1), i, 0))],
            out_specs=pl.BlockSpec((1, tr, cdim), lambda j, i, p: (p[1] ^ (j + 1), i, 0))),
        out_shape=_sds((NCHIP, R, cdim), grad.dtype),
        compiler_params=_params(("parallel", "parallel")),
    )(place, g5, t4)
    return out


def chip_exchange(name, parts, collective_id):
    def plan(srcs, lands):
        x, y, c, chips = _place()
        return ([(i, j, srcs[i].at[2 * chip[0] + chip[1]], lands[i].at[j], (*chip, c))
                 for i in range(len(srcs)) for j, chip in enumerate(chips)],
                [(*chip, c) for chip in chips])

    return _split_exchange(name, parts, [_sds((3,) + p.shape[1:], p.dtype) for p in parts],
                           plan, 3, collective_id)


def ada_fwd(c_row, w_ada, b_ada):
    D, cols = w_ada.shape

    def body(c_ref, w_ref, b_ref, mod_ref, call_ref, act8, part, s1, r1, s2, r2):
        x, y, c, _ = _place()
        me = 4 * x + 2 * y + c
        call_ref[me] = c_ref[...]
        cps = []
        for k in range(1, NDEV):
            to = (x ^ (k >> 2), y ^ ((k >> 1) & 1), c ^ (k & 1))
            cps.append(pltpu.make_async_remote_copy(
                src_ref=c_ref, dst_ref=call_ref.at[me], send_sem=s1.at[k - 1],
                recv_sem=r1.at[k - 1], device_id=to, device_id_type=MESH))
            cps[-1].start()
        for cp in cps:
            cp.wait()
        for b in range(NDEV):
            act8[b:b + 1, :] = call_ref[b]
        cv = act8[...]
        act = (cv * _sigmoid(cv)).astype(BF16)
        res = jnp.dot(act, w_ref[...].astype(BF16), preferred_element_type=F32)
        for b in range(NDEV):
            part[b] = res[b:b + 1, :]
        mod_ref[me] = part[me]
        cps = []
        for k in range(1, NDEV):
            to = (x ^ (k >> 2), y ^ ((k >> 1) & 1), c ^ (k & 1))
            dst = 4 * to[0] + 2 * to[1] + to[2]
            cps.append(pltpu.make_async_remote_copy(
                src_ref=part.at[dst], dst_ref=mod_ref.at[me], send_sem=s2.at[k - 1],
                recv_sem=r2.at[k - 1], device_id=to, device_id_type=MESH))
            cps[-1].start()
        for cp in cps:
            cp.wait()
        for b in range(NDEV):
            mod_ref[b] = mod_ref[b] + b_ref[b]

    vm = pl.BlockSpec(memory_space=pltpu.VMEM)
    return pl.pallas_call(
        body, name="ada_fwd", in_specs=[vm, vm, vm], out_specs=[vm, vm],
        out_shape=[_sds((NDEV, 1, cols), F32), _sds((NDEV, 1, D), F32)],
        scratch_shapes=[pltpu.VMEM((NDEV, D), F32), pltpu.VMEM((NDEV, 1, cols), F32),
                        pltpu.SemaphoreType.DMA((NDEV - 1,)), pltpu.SemaphoreType.DMA((NDEV - 1,)),
                        pltpu.SemaphoreType.DMA((NDEV - 1,)), pltpu.SemaphoreType.DMA((NDEV - 1,))],
        compiler_params=pltpu.CompilerParams(vmem_limit_bytes=VMEM_LIMIT),
    )(c_row, w_ada, b_ada.reshape(NDEV, 1, cols))


def _adamw_math(g, w, m, v):
    m2 = ADAM_B1 * m + (1.0 - ADAM_B1) * g
    v2 = ADAM_B2 * v + (1.0 - ADAM_B2) * (g * g)
    m_hat = m2 / (1.0 - ADAM_B1 ** ADAM_STEP)
    v_hat = v2 / (1.0 - ADAM_B2 ** ADAM_STEP)
    delta = -ADAM_LR * (m_hat / (jnp.sqrt(v_hat) + ADAM_EPS) + ADAM_WD * w)
    return delta, m2, v2


def adamw_sharded(name, grad8, pair4, got3, w, m, v, place):
    shape = w.shape
    cdim = shape[-1]
    R = int(math.prod(shape[:-1]))
    w2, m2, v2 = (t.reshape(R, cdim) for t in (w, m, v))
    tr = _tile(R, max(8, (1 << 19) // cdim))

    def body(q_ref, own_ref, sib_ref, t_ref, w_ref, m_ref, v_ref, g_out, d_out, m_out, v_out):
        g = own_ref[0].astype(F32) + sib_ref[0].astype(F32)
        for j in range(3):
            g = g + t_ref[j].astype(F32)
        d, mn, vn = _adamw_math(g, w_ref[...], m_ref[...], v_ref[...])
        g_out[...] = g
        d_out[...] = d
        m_out[...] = mn
        v_out[...] = vn

    spec = pl.BlockSpec((tr, cdim), lambda i, qr: (i, 0))
    outs = pl.pallas_call(
        body, name=name,
        grid_spec=pltpu.PrefetchScalarGridSpec(
            num_scalar_prefetch=1, grid=(R // tr,),
            in_specs=[pl.BlockSpec((1, tr, cdim), lambda i, qr: (qr[2], i, 0)),
                      pl.BlockSpec((1, tr, cdim), lambda i, qr: (qr[1], i, 0)),
                      pl.BlockSpec((3, tr, cdim), lambda i, qr: (0, i, 0)), spec, spec, spec],
            out_specs=[spec] * 4),
        out_shape=[_sds((R, cdim), F32)] * 4,
        compiler_params=_params(("parallel",)),
    )(place, grad8.reshape(NDEV, R, cdim), pair4.reshape(NCHIP, R, cdim),
      got3.reshape(3, R, cdim), w2, m2, v2)
    return [o.reshape(shape) for o in outs]


def adamw_small(parts, w, m, v, after=()):
    R = w.shape[0]
    tr = R

    def body(p_ref, w_ref, m_ref, v_ref, g_out, d_out, m_out, v_out):
        g = p_ref[0]
        for j in range(1, NDEV):
            g = g + p_ref[j]
        d, mn, vn = _adamw_math(g, w_ref[...], m_ref[...], v_ref[...])
        g_out[...] = g
        d_out[...] = d
        m_out[...] = mn
        v_out[...] = vn

    spec = pl.BlockSpec((tr, LANES), lambda i: (i, 0))
    return pl.pallas_call(
        _with_after(body, 4, after), name="adamw_small", grid=(R // tr,),
        in_specs=[pl.BlockSpec((NDEV, tr, LANES), lambda i: (0, i, 0)), spec, spec, spec]
        + [ANY] * len(after),
        out_specs=[spec] * 4, out_shape=[_sds((R, LANES), F32)] * 4,
        compiler_params=_params(("parallel",)),
    )(parts, w, m, v, *after)


def adamw_ada(c_all_t, dmod_all, w, m, v, my_dev):
    D, cols = w.shape
    tr = _tile(D, 256)

    def body(k_ref, c_ref, d_ref, w_ref, m_ref, v_ref, g_out, d_out, m_out, v_out):
        cv = c_ref[...]
        act = cv * _sigmoid(cv)
        dm = d_ref[...]
        g = act[:, 0:1] * dm[0:1, :]
        for b in range(1, NDEV):
            g = g + act[:, b:b + 1] * dm[b:b + 1, :]
        d, mn, vn = _adamw_math(g, w_ref[...], m_ref[...], v_ref[...])
        g_out[...] = g
        d_out[...] = d
        m_out[...] = mn
        v_out[...] = vn

    spec = pl.BlockSpec((tr, cols), lambda i, kr: (i, 0))
    return pl.pallas_call(
        body, name="adamw_ada",
        grid_spec=pltpu.PrefetchScalarGridSpec(
            num_scalar_prefetch=1, grid=(D // tr,),
            in_specs=[pl.BlockSpec((tr, NDEV), lambda i, kr: (i, 0)),
                      pl.BlockSpec((NDEV, cols), lambda i, kr: (0, kr[0])), spec, spec, spec],
            out_specs=[spec] * 4),
        out_shape=[_sds((D, cols), F32)] * 4,
        compiler_params=_params(("parallel",)),
    )(my_dev, c_all_t, dmod_all, w, m, v)


def _small_pack(parts):
    rows = []
    for p in parts:
        flat = p.reshape(-1)
        flat = jnp.pad(flat, (0, (-flat.shape[0]) % (SUBLANES * LANES)))
        rows.append(flat.reshape(-1, LANES))
    return jnp.concatenate(rows, axis=0)


def _small_unpack(buf, shapes):
    out, r = [], 0
    for s in shapes:
        n = int(math.prod(s))
        nr = -(-n // (SUBLANES * LANES)) * SUBLANES
        out.append(buf[r:r + nr].reshape(-1)[:n].reshape(s))
        r += nr
    return out


def kernel(x, c, w_ada, b_ada, w_in, lam_re, lam_im, log_dt, ssm_b_re, ssm_b_im, ssm_c_re, ssm_c_im, ssm_d, w_glu_val, w_glu_gate, w_pool, pool_scale, w_pool_out, w_out, ln1_g, ln1_b, w_ff1, w_ff2, ln2_g, ln2_b, loss_target, m_w_ada, m_b_ada, m_w_in, m_lam_re, m_lam_im, m_log_dt, m_ssm_b_re, m_ssm_b_im, m_ssm_c_re, m_ssm_c_im, m_ssm_d, m_w_glu_val, m_w_glu_gate, m_w_pool, m_pool_scale, m_w_pool_out, m_w_out, m_ln1_g, m_ln1_b, m_w_ff1, m_w_ff2, m_ln2_g, m_ln2_b, v_w_ada, v_b_ada, v_w_in, v_lam_re, v_lam_im, v_log_dt, v_ssm_b_re, v_ssm_b_im, v_ssm_c_re, v_ssm_c_im, v_ssm_d, v_w_glu_val, v_w_glu_gate, v_w_pool, v_pool_scale, v_w_pool_out, v_w_out, v_ln1_g, v_ln1_b, v_w_ff1, v_w_ff2, v_ln2_g, v_ln2_b):
    S, D = x.shape[1], x.shape[2]
    x2d, tgt = x[0], loss_target[0]
    W = D // 2
    G = W // SSM_GROUP
    P, H, GPB = SSM_STATE, SSM_GROUP, GROUPS_PER_BLOCK
    nblk = G // GPB
    gw = W // len(POOL_WINDOWS)
    ax, ay, ac = lax.axis_index("x"), lax.axis_index("y"), lax.axis_index("c")
    my_dev = (4 * ax + 2 * ay + ac).astype(jnp.int32).reshape(1)
    place = jnp.stack([ac, 2 * ax + ay, 4 * ax + 2 * ay + ac]).astype(jnp.int32)
    ts = _tile(S, 256)

    glu = jnp.stack([w_glu_val[0], w_glu_gate[0]]).astype(BF16)
    shards = [w_in[0].astype(BF16), glu, w_pool[0].astype(BF16), w_pool_out[0].astype(BF16),
              w_out[0].astype(BF16), w_ff1[0].astype(BF16), w_ff2[0].astype(BF16)]
    wg_in, wg_pool = seq_all_gather("gather_w_in", [shards[0], shards[2]], 1)
    wg_vg, wg_po, wg_out = seq_all_gather("gather_w_mix", [shards[1], shards[3], shards[4]], 2)
    (wg_ff1,) = seq_all_gather("gather_w_ff1", shards[5:6], 3)
    (wg_ff2,) = seq_all_gather("gather_w_ff2", shards[6:7], 11)
    wg_vg = wg_vg.reshape(2 * NDEV, W, D // NDEV)
    nwin = len(POOL_WINDOWS)
    wp_full = jnp.transpose(wg_pool, (1, 0, 2, 3)).reshape(nwin, gw, gw)
    wout_full = wg_out.reshape(1, D, D)
    wff2_full = wg_ff2.reshape(1, 4 * D, D)

    small_names = [b_ada, lam_re, lam_im, log_dt, ssm_b_re, ssm_b_im, ssm_c_re, ssm_c_im, ssm_d,
                   pool_scale, ln1_g, ln1_b, ln2_g, ln2_b]
    small_m = [m_b_ada, m_lam_re, m_lam_im, m_log_dt, m_ssm_b_re, m_ssm_b_im, m_ssm_c_re, m_ssm_c_im,
               m_ssm_d, m_pool_scale, m_ln1_g, m_ln1_b, m_ln2_g, m_ln2_b]
    small_v = [v_b_ada, v_lam_re, v_lam_im, v_log_dt, v_ssm_b_re, v_ssm_b_im, v_ssm_c_re, v_ssm_c_im,
               v_ssm_d, v_pool_scale, v_ln1_g, v_ln1_b, v_ln2_g, v_ln2_b]
    zero_row = jnp.zeros((1, LANES), F32)
    packed_wmv = [_small_pack(t + [zero_row]) for t in (small_names, small_m, small_v)]

    mod, c_all = ada_fwd(c, w_ada[0], b_ada)
    mod = mod.reshape(6, 1, D)
    sh1, sc1, g1, sh2, sc2, g2 = (mod[i] for i in range(6))

    f2, kconst = s5_disc(lam_re[0], lam_im[0], log_dt[0].reshape(G, 1))
    kconst = kconst.reshape(16, SUBLANES, G * P)
    f2r = f2.reshape(2, 1, G * P)
    bt_re = jnp.transpose(ssm_b_re[0], (2, 0, 1)).reshape(H, G * P)
    bt_im = jnp.transpose(ssm_b_im[0], (2, 0, 1)).reshape(H, G * P)
    ct_re = jnp.transpose(ssm_c_re[0], (1, 0, 2)).reshape(H, G * P)
    ct_im = jnp.transpose(ssm_c_im[0], (1, 0, 2)).reshape(H, G * P)
    s5_params = (f2r, bt_re, bt_im, ct_re, ct_im, ssm_d, kconst)

    def e1(t, b):
        xhat, _ = _ln_stats(t[0])
        return [xhat * (1.0 + b[0]) + b[1]], []
    (h1,) = _rowwise("ln_mod1", e1, S, ts, [(x2d, D, 0)], [sc1, sh1], [(D, BF16)], [])

    (proj,) = mm_nn("proj", h1, wg_in, F32, 2)
    z, xsb_all = s5_fwd(proj, s5_params, nblk)
    (vt,) = mm_nn("glu", z, wg_vg, BF16, 4)
    pooled = pool_fwd(proj, W, W, gw)

    def pool_epi(vals, ex, outs):
        a = vals[0]
        outs[0][...] = a
        outs[1][...] = (a * ex[0][...]).astype(BF16)
    tmp = _tile(S, 1024)
    yp, ypool = _mm(
        "pool_mix", "nn", pooled, wp_full.astype(BF16), (S // tmp, nwin, 1),
        pl.BlockSpec((tmp, gw), lambda i, j, k: (i, j)), pl.BlockSpec((1, gw, gw), lambda i, j, k: (j, 0, 0)),
        [(_sds((S, W), F32), pl.BlockSpec((tmp, gw), lambda i, j, k: (i, j))),
         (_sds((S, W), BF16), pl.BlockSpec((tmp, gw), lambda i, j, k: (i, j)))],
        (tmp, gw), 1, gw, None, pool_epi,
        [(pool_scale, pl.BlockSpec((1, gw), lambda i, j, k: (0, j)))])
    (y_b,) = mm_nn("pool_out", ypool, wg_po, BF16, 4)

    cb = D // NDEV
    ga_cb, gb_cb = (2 * W) // cb, (2 * W + D) // cb
    tsm = _tile(S, 512)

    def merge_call(name, fn, ins, n_out, after=()):
        def body(*refs):
            vals = [r[...].astype(F32) for r in refs[:len(ins)]]
            for r, v in zip(refs[len(ins):], fn(*vals)):
                r[...] = v.astype(r.dtype)
        return pl.pallas_call(
            _with_after(body, len(ins), after), name=name, grid=(S // tsm, NDEV),
            in_specs=[pl.BlockSpec((tsm, w), f) for (_, w, f) in ins] + [ANY] * len(after),
            out_specs=[pl.BlockSpec((tsm, w), lambda i, j: (i, j)) for (_, w) in n_out],
            out_shape=[_sds((S, cols), BF16) for (cols, _) in n_out],
            compiler_params=_params(("parallel", "parallel")),
        )(*[a for (a, _, _) in ins], *after)

    merge_ins = [(proj, cb, lambda i, j: (i, ga_cb + j)), (proj, cb, lambda i, j: (i, gb_cb + j)),
                 (vt, 2 * cb, lambda i, j: (i, j)), (y_b, cb, lambda i, j: (i, j))]

    def merge_f(ga, gb, vtv, yb):
        return [_sigmoid(ga) * (vtv[:, :cb] * _sigmoid(vtv[:, cb:])) + _sigmoid(gb) * yb]
    (merged,) = merge_call("merge", merge_f, merge_ins, [(D, cb)])

    (mix,) = mm_nn("mix_out", merged, wout_full, F32, 1)

    def e3(t, b):
        xv, mx = t
        g1v, l1g, l1b, sc2v, sh2v = b
        r1 = ALPHA * xv + g1v * mx
        xh1, _ = _ln_stats(r1)
        x1 = xh1 * l1g + l1b
        xh, _ = _ln_stats(x1)
        return [r1, xh * (1.0 + sc2v) + sh2v], []
    r1, h2 = _rowwise("post_mix", e3, S, ts, [(x2d, D, 0), (mix, D, 0)],
                      [g1, ln1_g, ln1_b, sc2, sh2], [(D, F32), (D, BF16)], [], after=packed_wmv)

    def relu_epi(vals, ex, outs):
        outs[0][...] = jnp.maximum(vals[0], 0.0).astype(BF16)
    (rl,) = mm_nn("ff1", h2, wg_ff1, BF16, 1, epi=relu_epi)

    def square(a):
        return a * a
    (y2,) = mm_nn("ff2", rl, wff2_full, F32, 1, pro=square)

    def e4(t, b):
        r1v, y2v, tg = t
        g2v, l1g, l1b, l2g, l2b = b
        xh1, _ = _ln_stats(r1v)
        x1 = xh1 * l1g + l1b
        r2 = ALPHA * x1 + g2v * y2v
        xh2, rs2 = _ln_stats(r2)
        err = xh2 * l2g + l2b - tg
        dx2 = err * (1.0 / D)
        dr2 = _ln_bwd(dx2 * l2g, xh2, rs2)
        lsum = jnp.sum(_colsum(err * err), axis=1, keepdims=True) * (0.5 / D)
        return ([ALPHA * dr2, g2v * dr2],
                [jnp.broadcast_to(lsum, (1, LANES)), _colsum(dx2 * xh2), _colsum(dx2), _colsum(dr2 * y2v)])
    dx1a, dy2, loss_acc, g_ln2g, g_ln2b, d_g2 = _rowwise(
        "head", e4, S, ts, [(r1, D, 0), (y2, D, 0), (tgt, D, 0)], [g2, ln1_g, ln1_b, ln2_g, ln2_b],
        [(D, F32), (D, BF16)], [LANES, D, D, D])

    tn_ff = _tile(4 * D, 1024)

    def dff_epi(vals, ex, outs):
        outs[0][...] = (vals[0] * (2.0 * ex[0][...].astype(F32))).astype(BF16)
    tmf = _tile(S, 1024)
    (da1,) = mm_nt("d_ff2", dy2, wff2_full, BF16, 1, tn=tn_ff, epi=dff_epi,
                   extras=[(rl, pl.BlockSpec((tmf, tn_ff), lambda i, j, k: (i, j)))])
    gw_ff2 = mm_tn("gw_ff2", rl, dy2, BF16, NDEV, 0, pro=square)
    gw_ff1 = mm_tn("gw_ff1", h2, da1, BF16, NDEV, 1)
    tok, wait_pair_a = pair_exchange("pair_exchange_ff", [gw_ff2, gw_ff1], 4)
    (dh2,) = mm_nt("d_ff1", da1, wg_ff1, F32, 4, after=[tok])

    def e5(t, b):
        dh2v, r1v, dx1av, mx = t
        sc2v, l1g, l1b, g1v = b
        xh1, rs1 = _ln_stats(r1v)
        x1 = xh1 * l1g + l1b
        xh, rs = _ln_stats(x1)
        dx1 = dx1av + _ln_bwd(dh2v * (1.0 + sc2v), xh, rs)
        dr1 = _ln_bwd(dx1 * l1g, xh1, rs1)
        return ([ALPHA * dr1, g1v * dr1],
                [_colsum(dh2v * xh), _colsum(dh2v), _colsum(dx1 * xh1), _colsum(dx1), _colsum(dr1 * mx)])
    dxa, dmix, d_sc2, d_sh2, g_ln1g, g_ln1b, d_g1 = _rowwise(
        "post_mix_bwd", e5, S, ts, [(dh2, D, 0), (r1, D, 0), (dx1a, D, 0), (mix, D, 0)],
        [sc2, ln1_g, ln1_b, g1], [(D, F32), (D, BF16)], [D, D, D, D, D])

    (dmerged,) = mm_nt("d_mix_out", dmix, wout_full, BF16, 1)
    gw_out = mm_tn("gw_out", merged, dmix, BF16, NDEV, 0)
    grads_a, got_a = wait_pair_a(gw_out)
    parts_a = [pair_sum("pair_sum_ff%d" % i, g, t, place) for i, (g, t) in enumerate(zip(grads_a, got_a))]
    tok, wait_chip_a = chip_exchange("chip_exchange_ff", parts_a, 5)

    def merge_b(ga, gb, vtv, yb, dm):
        vv, tt = vtv[:, :cb], vtv[:, cb:]
        sa, sb, st = _sigmoid(ga), _sigmoid(gb), _sigmoid(tt)
        dya = dm * sa
        return [dm * (vv * st) * sa * (1.0 - sa), dm * yb * sb * (1.0 - sb),
                jnp.concatenate([dya * st, dya * vv * st * (1.0 - st)], axis=1), dm * sb]
    dga, dgb_, dvt, dy_b = merge_call(
        "merge_bwd", merge_b, merge_ins + [(dmerged, cb, lambda i, j: (i, j))],
        [(D, cb), (D, cb), (2 * D, 2 * cb), (D, cb)], after=[tok])

    (dypool,) = mm_nt("d_pool_out", dy_b, wg_po, F32, NDEV)
    gw_po = mm_tn("gw_pool_out", ypool, dy_b, BF16, NDEV, 4)

    def e7(t, b):
        return [t[0] * b[0]], [_colsum(t[0] * t[1])]
    dyp, g_pscale = _rowwise("pool_scale_bwd", e7, S, ts, [(dypool, W, 0), (yp, W, 0)],
                             [pool_scale], [(W, BF16)], [W])
    (dpooled,) = _mm(
        "d_pool_mix", "nt", dyp, wp_full.astype(BF16), (S // tmp, nwin, 1),
        pl.BlockSpec((tmp, gw), lambda i, j, k: (i, j)), pl.BlockSpec((1, gw, gw), lambda i, j, k: (j, 0, 0)),
        [(_sds((S, W), F32), pl.BlockSpec((tmp, gw), lambda i, j, k: (i, j)))], (tmp, gw), 1, gw)
    tkp = _tile(S, 2048)
    gw_pool = _mm(
        "gw_pool", "tn", pooled, dyp, (nwin, 1, S // tkp),
        pl.BlockSpec((tkp, gw), lambda i, j, k: (k, i)), pl.BlockSpec((tkp, gw), lambda i, j, k: (k, i)),
        [(_sds((nwin, gw, gw), BF16), pl.BlockSpec((1, gw, gw), lambda i, j, k: (i, 0, 0)))],
        (gw, gw), 1, gw, stacked_out=True)[0]
    du_pool = pool_bwd(dpooled, gw)

    (dz,) = mm_nt("d_glu", dvt, wg_vg, BF16, 2 * NDEV)
    gw_vg = mm_tn("gw_glu", z, dvt, BF16, 2 * NDEV, 4)
    gw_pool_st = jnp.transpose(gw_pool.reshape(nwin, NDEV, gw // NDEV, gw), (1, 0, 2, 3))
    grads_b = [gw_out, gw_po, gw_pool_st, gw_vg.reshape(NDEV, 2, W, D // NDEV)]
    tok, wait_pair_b = pair_exchange("pair_exchange_mix", grads_b, 6)
    du_ssm, g_bt_re, g_bt_im, g_ct_re, g_ct_im, g_f, g_d, g_a = s5_bwd(
        proj, xsb_all, dz, s5_params, nblk, after=[tok])
    grads_b, got_b = wait_pair_b(du_ssm)
    parts_b = [pair_sum("pair_sum_mix%d" % i, g, t, place) for i, (g, t) in enumerate(zip(grads_b, got_b))]
    tok, wait_chip_b = chip_exchange("chip_exchange_mix", parts_b, 7)

    dproj = jnp.concatenate([du_ssm, du_pool, dga, dgb_], axis=1)
    gw_in = mm_tn("gw_in", h1, dproj, BF16, NDEV, 1, after=[tok])
    tok, wait_pair_c = pair_exchange("pair_exchange_in", [gw_in], 8)
    (dh1,) = mm_nt("d_proj", dproj, wg_in, F32, 4, after=[tok])
    grads_c, got_c = wait_pair_c(dh1)
    parts_c = [pair_sum("pair_sum_in", grads_c[0], got_c[0], place)]
    tok, wait_chip_c = chip_exchange("chip_exchange_in", parts_c, 9)

    def e10(t, b):
        dh1v, xv, dxav = t
        xh, rs = _ln_stats(xv)
        return ([dxav + _ln_bwd(dh1v * (1.0 + b[0]), xh, rs)],
                [_colsum(dh1v * xh), _colsum(dh1v)])
    grad_x, d_sc1, d_sh1 = _rowwise("ln_mod1_bwd", e10, S, ts, [(dh1, D, 0), (x2d, D, 0), (dxa, D, 0)],
                                    [sc1], [(D, F32)], [D, D], after=[tok])

    g_b_re = jnp.transpose(g_bt_re.reshape(H, G, P), (1, 2, 0))
    g_b_im = jnp.transpose(g_bt_im.reshape(H, G, P), (1, 2, 0))
    g_c_re = jnp.transpose(g_ct_re.reshape(H, G, P), (1, 0, 2))
    g_c_im = jnp.transpose(g_ct_im.reshape(H, G, P), (1, 0, 2))
    d_ab = jnp.transpose(g_a.reshape(nblk, 2, GPB, P), (1, 0, 2, 3)).reshape(2, G, P)
    g_lr, g_li, g_ldt = s5_disc_bwd(lam_re[0], lam_im[0], log_dt[0].reshape(G, 1), d_ab,
                                    g_f.reshape(2, G, P))

    dmod = jnp.concatenate([d_sh1, d_sc1, d_g1, d_sh2, d_sc2, d_g2], axis=1)
    small_g = [dmod, g_lr, g_li, g_ldt, g_b_re, g_b_im, g_c_re, g_c_im, g_d, g_pscale,
               g_ln1g, g_ln1b, g_ln2g, g_ln2b, loss_acc]
    packed_g = _small_pack(small_g)
    (parts_all,) = seq_all_gather("gather_small", [packed_g], 10, routed=False)
    _, got3_a = wait_chip_a(packed_g)
    glu_w = jnp.stack([w_glu_val[0], w_glu_gate[0]])
    glu_m = jnp.stack([m_w_glu_val[0], m_w_glu_gate[0]])
    glu_v = jnp.stack([v_w_glu_val[0], v_w_glu_gate[0]])
    wmv = [(w_ff2[0], m_w_ff2[0], v_w_ff2[0]), (w_ff1[0], m_w_ff1[0], v_w_ff1[0]),
           (w_out[0], m_w_out[0], v_w_out[0]), (w_pool_out[0], m_w_pool_out[0], v_w_pool_out[0]),
           (w_pool[0], m_w_pool[0], v_w_pool[0]), (glu_w, glu_m, glu_v)]
    upd = [adamw_sharded("adamw_%d" % i, g, p, t, w, m, v, place)
           for i, (g, p, t, (w, m, v)) in enumerate(zip(grads_a, got_a, got3_a, wmv[:2]))]
    _, got3_b = wait_chip_b(upd[-1][0])
    upd += [adamw_sharded("adamw_%d" % (2 + i), g, p, t, w, m, v, place)
            for i, (g, p, t, (w, m, v)) in enumerate(zip(grads_b, got_b, got3_b, wmv[2:]))]
    u_ff2, u_ff1, u_out, u_po, u_pool, u_glu = upd

    sg, sd, sm, sv = adamw_small(parts_all, *packed_wmv, after=[upd[-1][0]])
    shapes = [t.shape for t in small_names]
    loss = _small_unpack(sg, shapes + [(1, LANES)])[-1][0, 0]
    sg, sd, sm, sv = (_small_unpack(t, shapes) for t in (sg, sd, sm, sv))

    nmod = 6 * D
    dmod_all = parts_all[:, :nmod // LANES, :].reshape(NDEV, nmod)
    c_all_t = jnp.transpose(c_all.reshape(NDEV, D))
    ada_out = adamw_ada(c_all_t, dmod_all, w_ada[0], m_w_ada[0], v_w_ada[0], my_dev)
    _, got3_c = wait_chip_c(ada_out[0])
    u_in = adamw_sharded("adamw_6", grads_c[0], got_c[0], got3_c[0], w_in[0], m_w_in[0], v_w_in[0], place)

    def pick(k):
        return [ada_out[k][None], sg_sd[k][0], u_in[k][None]] + [t for t in sg_sd[k][1:9]] + \
               [u_glu[k][0][None], u_glu[k][1][None], u_pool[k][None], sg_sd[k][9], u_po[k][None],
                u_out[k][None], sg_sd[k][10], sg_sd[k][11], u_ff1[k][None], u_ff2[k][None],
                sg_sd[k][12], sg_sd[k][13]]

    sg_sd = [sg, sd, sm, sv]
    return (loss, grad_x[None], *pick(0), *pick(1), *pick(2), *pick(3))
```

```python
import functools
import math

import jax
import jax.numpy as jnp
from jax import lax
from jax.experimental import pallas as pl
from jax.experimental.pallas import tpu as pltpu
from jax.experimental.pallas import tpu_sc as plsc

F32 = jnp.float32
BF16 = jnp.bfloat16
MESH = pl.DeviceIdType.MESH
NDEV = 8
NCHIP = 4

SSM_GROUP = 16
SSM_STATE = 64
GROUPS_PER_BLOCK = 8
POOL_WINDOWS = (2, 4, 8, 16)
LN_EPS = 1e-5
ALPHA = 2.0 ** 0.25
ADAM_LR, ADAM_B1, ADAM_B2, ADAM_EPS, ADAM_WD, ADAM_STEP = 0.001, 0.9, 0.999, 1e-08, 0.01, 10
SUBLANES = 8
LANES = 128
VMEM_LIMIT = 56 * 1024 * 1024


def _params(sem=None, vmem=VMEM_LIMIT):
    return pltpu.CompilerParams(dimension_semantics=sem, vmem_limit_bytes=vmem)


def _tile(n, pref):
    if n <= pref:
        return n
    t = 1 << (pref.bit_length() - 1)
    while n % t:
        t //= 2
    return t


def _cast_epi(vals, ex, outs):
    c = vals[0].shape[1]
    for s, v in enumerate(vals):
        outs[0][:, s * c:(s + 1) * c] = v.astype(outs[0].dtype)


ANY = pl.BlockSpec(memory_space=pl.ANY)


def _with_after(body, n_in, after):
    if not after:
        return body
    n_af = len(after)

    def wrapped(*refs):
        return body(*refs[:n_in], *refs[n_in + n_af:])
    return wrapped


def _mm(name, kind, a, b, grid, a_spec, b_spec, outs, acc_shape, nsub=1, c=None,
        pro=None, epi=None, extras=(), stacked_out=False, after=(), side=None):
    nk = grid[2]
    n_ex, n_out = len(extras), len(outs)

    def finish(vals, ex, out_refs):
        if epi is not None:
            epi(vals, ex, out_refs)
        elif stacked_out:
            for s, v in enumerate(vals):
                out_refs[0][s] = v.astype(out_refs[0].dtype)
        else:
            _cast_epi(vals, ex, out_refs)

    def body(*refs):
        mm_step(refs[0], refs[1], refs[2:2 + n_ex], refs[2 + n_ex:2 + n_ex + n_out], refs[-1])

    def mm_step(a_ref, b_ref, ex, out_refs, acc):
        k = pl.program_id(2)
        av = a_ref[...]
        if pro is not None:
            av = pro(av)
        if kind == "nn":
            prods = [jnp.dot(av, b_ref[s], preferred_element_type=F32) for s in range(nsub)]
        elif kind == "nt":
            t = None
            for s in range(nsub):
                d = lax.dot_general(av[:, s * c:(s + 1) * c], b_ref[s], (((1,), (1,)), ((), ())),
                                    preferred_element_type=F32)
                t = d if t is None else t + d
            prods = [t]
        else:
            t = lax.dot_general(av, b_ref[...], (((0,), (0,)), ((), ())), preferred_element_type=F32)
            prods = [t[:, s * c:(s + 1) * c] for s in range(nsub)] if stacked_out else [t]
        if nk == 1:
            finish(prods, ex, out_refs)
            return
        w = prods[0].shape[1]

        @pl.when(k == 0)
        def _():
            for s, p in enumerate(prods):
                acc[:, s * w:(s + 1) * w] = p

        @pl.when(jnp.logical_and(k > 0, k < nk - 1))
        def _():
            for s, p in enumerate(prods):
                acc[:, s * w:(s + 1) * w] += p

        @pl.when(k == nk - 1)
        def _():
            finish([acc[:, s * w:(s + 1) * w] + p for s, p in enumerate(prods)], ex, out_refs)

    scratch = [pltpu.VMEM(acc_shape, F32)] if nk > 1 else []
    if side is None:
        return pl.pallas_call(
            _with_after(body, 2 + n_ex, after), name=name, grid=grid,
            in_specs=[a_spec, b_spec] + [e[1] for e in extras] + [ANY] * len(after),
            out_specs=[o[1] for o in outs],
            out_shape=[o[0] for o in outs],
            scratch_shapes=scratch,
            compiler_params=_params(("parallel", "parallel", "arbitrary")),
        )(a, b, *[e[0] for e in extras], *after)

    place, s_ins, s_outs, s_fn = side
    ns_in, ns_out, nj = len(s_ins), len(s_outs), grid[1]

    def side_body(p_ref, *refs):
        o0 = 2 + n_ex + ns_in
        s_fn(refs[2 + n_ex:o0], refs[o0 + n_out:o0 + n_out + ns_out])
        mm_step(refs[0], refs[1], refs[2:2 + n_ex], refs[o0:o0 + n_out], refs[-1])

    def lift(spec):
        return pl.BlockSpec(spec.block_shape, lambda i, j, k, p, f=spec.index_map: f(i, j, k))

    def per_step(shape, fn):
        return pl.BlockSpec(shape, lambda i, j, k, p, f=fn: f((i * nj + j) * nk + k, p))

    return pl.pallas_call(
        _with_after(side_body, 3 + n_ex + ns_in, after), name=name,
        grid_spec=pltpu.PrefetchScalarGridSpec(
            num_scalar_prefetch=1, grid=grid,
            in_specs=[lift(a_spec), lift(b_spec)] + [lift(e[1]) for e in extras]
            + [per_step(s[1], s[2]) for s in s_ins] + [ANY] * len(after),
            out_specs=[lift(o[1]) for o in outs] + [per_step(s[1], s[2]) for s in s_outs],
            scratch_shapes=scratch),
        out_shape=[o[0] for o in outs] + [s[0] for s in s_outs],
        compiler_params=_params(("parallel", "parallel", "arbitrary")),
    )(place, a, b, *[e[0] for e in extras], *[s[0] for s in s_ins], *after)


def _sds(shape, dtype):
    return jax.ShapeDtypeStruct(shape, dtype)


def mm_nn(name, a, b3, out_dtype, nsub, tm=1024, tk=2048, tn=None, pro=None, epi=None,
          extras=(), extra_outs=(), a_col0=0, after=()):
    M = a.shape[0]
    nb, K, cdim = b3.shape
    tm, tk = _tile(M, tm), _tile(K, tk)
    if nb == 1:
        tn = _tile(cdim, tn or 1024)
        nsub, c, nj = 1, tn, cdim // tn
        b_spec = pl.BlockSpec((1, tk, tn), lambda i, j, k: (0, k, j))
        N = cdim
    else:
        c, nj, tn = cdim, nb // nsub, nsub * cdim
        b_spec = pl.BlockSpec((nsub, tk, cdim), lambda i, j, k: (j, k, 0))
        N = nb * cdim
    kb0 = a_col0 // tk
    a_spec = pl.BlockSpec((tm, tk), lambda i, j, k: (i, kb0 + k))
    grid = (M // tm, nj, K // tk)
    o_spec = pl.BlockSpec((tm, tn), lambda i, j, k: (i, j))
    outs = [(_sds((M, N), out_dtype), o_spec)] + [(_sds((M, N), d), o_spec) for d in extra_outs]
    return _mm(name, "nn", a, b3, grid, a_spec, b_spec, outs, (tm, tn), nsub, c, pro, epi, extras,
               after=after)


def mm_nt(name, a, b3, out_dtype, nsub, tm=1024, tn=1024, epi=None, extras=(), extra_outs=(),
          after=(), side=None):
    M = a.shape[0]
    nb, N, cdim = b3.shape
    tm, tn = _tile(M, tm), _tile(N, tn)
    if nb == 1:
        tk = _tile(cdim, 2048)
        nsub, c, nk = 1, tk, cdim // tk
        b_spec = pl.BlockSpec((1, tn, tk), lambda i, j, k: (0, j, k))
    else:
        c, nk, tk = cdim, nb // nsub, nsub * cdim
        b_spec = pl.BlockSpec((nsub, tn, cdim), lambda i, j, k: (k, j, 0))
    a_spec = pl.BlockSpec((tm, tk), lambda i, j, k: (i, k))
    grid = (M // tm, N // tn, nk)
    o_spec = pl.BlockSpec((tm, tn), lambda i, j, k: (i, j))
    outs = [(_sds((M, N), out_dtype), o_spec)] + [(_sds((M, N), d), o_spec) for d in extra_outs]
    return _mm(name, "nt", a, b3, grid, a_spec, b_spec, outs, (tm, tn), nsub, c, None, epi, extras,
               after=after, side=side(grid[0] * grid[1] * grid[2]) if side else None)


def mm_tn(name, a, b, out_dtype, nb, nsub, tma=1024, tk=2048, pro=None, a_col0=0, a_cols=None,
          after=()):
    S = a.shape[0]
    Ka = a_cols or a.shape[1]
    N = b.shape[1]
    tk = _tile(S, tk)
    if nsub == 0:
        tma, tn = _tile(Ka, tma), _tile(N, 1024)
        grid = (Ka // tma, N // tn, S // tk)
        ab0 = a_col0 // tma
        res = _mm(name, "tn", a, b, grid, pl.BlockSpec((tk, tma), lambda i, j, k: (k, ab0 + i)),
                  pl.BlockSpec((tk, tn), lambda i, j, k: (k, j)),
                  [(_sds((Ka, N), out_dtype), pl.BlockSpec((tma, tn), lambda i, j, k: (i, j)))],
                  (tma, tn), 1, tn, pro, None, (), after=after)[0]
        return res.reshape(nb, Ka // nb, N)
    else:
        c = N // nb
        tma = _tile(Ka, tma)
        grid = (Ka // tma, nb // nsub, S // tk)
        o_spec = pl.BlockSpec((nsub, tma, c), lambda i, j, k: (j, i, 0))
        out = _sds((nb, Ka, c), out_dtype)
        nsub_k = nsub
        tn = nsub * c
        b_spec = pl.BlockSpec((tk, tn), lambda i, j, k: (k, j))
    ab0 = a_col0 // tma
    a_spec = pl.BlockSpec((tk, tma), lambda i, j, k: (k, ab0 + i))
    return _mm(name, "tn", a, b, grid, a_spec, b_spec, [(out, o_spec)], (tma, tn), nsub_k, c,
               pro, None, (), stacked_out=True, after=after)[0]


def _rowwise(name, fn, S, ts, tiled, bcast, tiled_out, acc_out, after=()):
    nt, nb, no, na = len(tiled), len(bcast), len(tiled_out), len(acc_out)

    def body(*refs):
        tin = [r[...] for r in refs[:nt]]
        bin_ = [r[...] for r in refs[nt:nt + nb]]
        o_refs = refs[nt + nb:nt + nb + no]
        a_refs = refs[nt + nb + no:]
        touts, aouts = fn(tin, bin_)
        for r, v in zip(o_refs, touts):
            r[...] = v.astype(r.dtype)
        i = pl.program_id(0)

        @pl.when(i == 0)
        def _():
            for r, v in zip(a_refs, aouts):
                r[...] = v

        @pl.when(i > 0)
        def _():
            for r, v in zip(a_refs, aouts):
                r[...] += v

    in_specs = [pl.BlockSpec((ts, w), functools.partial(lambda i, cb: (i, cb), cb=cb))
                for (_, w, cb) in tiled]
    in_specs += [pl.BlockSpec(b.shape, lambda i: (0, 0)) for b in bcast]
    out_specs = [pl.BlockSpec((ts, w), lambda i: (i, 0)) for (w, _) in tiled_out]
    out_specs += [pl.BlockSpec((1, w), lambda i: (0, 0)) for w in acc_out]
    out_shape = [_sds((S, w), d) for (w, d) in tiled_out] + [_sds((1, w), F32) for w in acc_out]
    return pl.pallas_call(
        _with_after(body, nt + nb, after), name=name, grid=(S // ts,),
        in_specs=in_specs + [ANY] * len(after), out_specs=out_specs,
        out_shape=out_shape, compiler_params=_params(("arbitrary",)),
    )(*[t[0] for t in tiled], *bcast, *after)


def _ln_stats(v):
    mu = jnp.mean(v, axis=-1, keepdims=True)
    vc = v - mu
    var = jnp.mean(vc * vc, axis=-1, keepdims=True)
    rstd = lax.rsqrt(var + LN_EPS)
    return vc * rstd, rstd


def _ln_bwd(dxhat, xhat, rstd):
    return rstd * (dxhat - jnp.mean(dxhat, axis=-1, keepdims=True)
                   - xhat * jnp.mean(dxhat * xhat, axis=-1, keepdims=True))


def _colsum(v):
    return jnp.sum(v, axis=0, keepdims=True)


def _sigmoid(v):
    return 1.0 / (1.0 + jnp.exp(-v))


_GELU_C = math.sqrt(2.0 / math.pi)


def _gelu(v):
    return 0.5 * v * (1.0 + jnp.tanh(_GELU_C * (v + 0.044715 * v * v * v)))


def _gelu_grad(v):
    t = jnp.tanh(_GELU_C * (v + 0.044715 * v * v * v))
    return 0.5 * (1.0 + t) + 0.5 * v * (1.0 - t * t) * _GELU_C * (1.0 + 3 * 0.044715 * v * v)


def _disc(lr, li, ldt):
    dt = jnp.exp(ldt)
    mag = jnp.exp(lr * dt)
    ang = li * dt
    ab_re = mag * jnp.cos(ang)
    ab_im = mag * jnp.sin(ang)
    num_re = ab_re - 1.0
    num_im = ab_im
    den = lr * lr + li * li
    f_re = (num_re * lr + num_im * li) / den
    f_im = (num_im * lr - num_re * li) / den
    return ab_re, ab_im, f_re, f_im


def _cmul(ar, ai, br, bi):
    return ar * br - ai * bi, ar * bi + ai * br


def s5_disc(lam_re, lam_im, log_dt):
    G, P = lam_re.shape

    def body(lr_ref, li_ref, ldt_ref, f_ref, k_ref):
        ab_re, ab_im, f_re, f_im = _disc(lr_ref[...], li_ref[...], ldt_ref[...])
        f_ref[0] = f_re
        f_ref[1] = f_im
        pr, pi = [ab_re], [ab_im]
        for _ in range(SUBLANES - 1):
            nr, ni = _cmul(pr[-1], pi[-1], ab_re, ab_im)
            pr.append(nr)
            pi.append(ni)
        zero = jnp.zeros_like(ab_re)
        for n, sh in enumerate((1, 2, 4)):
            for r in range(SUBLANES):
                k_ref[2 * n, r] = pr[sh - 1] if r >= sh else zero
                k_ref[2 * n + 1, r] = pi[sh - 1] if r >= sh else zero
                k_ref[8 + 2 * n, r] = pr[sh - 1] if r + sh < SUBLANES else zero
                k_ref[8 + 2 * n + 1, r] = -pi[sh - 1] if r + sh < SUBLANES else zero
        for r in range(SUBLANES):
            k_ref[6, r] = pr[r]
            k_ref[7, r] = pi[r]
            k_ref[14, r] = pr[SUBLANES - 1 - r]
            k_ref[15, r] = -pi[SUBLANES - 1 - r]

    vm = pl.BlockSpec(memory_space=pltpu.VMEM)
    return pl.pallas_call(
        body, name="s5_disc", in_specs=[vm, vm, vm], out_specs=[vm, vm],
        out_shape=[_sds((2, G, P), F32), _sds((16, SUBLANES, G, P), F32)],
    )(lam_re, lam_im, log_dt)


def s5_disc_bwd(lam_re, lam_im, log_dt, d_ab, d_f):
    G, P = lam_re.shape

    def body(lr_ref, li_ref, ldt_ref, dab_ref, df_ref, glr_ref, gli_ref, gdt_ref):
        _, vjp = jax.vjp(_disc, lr_ref[...], li_ref[...], ldt_ref[...])
        glr, gli, gdt = vjp((dab_ref[0], dab_ref[1], df_ref[0], df_ref[1]))
        glr_ref[...] = glr
        gli_ref[...] = gli
        gdt_ref[...] = gdt

    vm = pl.BlockSpec(memory_space=pltpu.VMEM)
    return pl.pallas_call(
        body, name="s5_disc_bwd", in_specs=[vm] * 5, out_specs=[vm] * 3,
        out_shape=[_sds((G, P), F32), _sds((G, P), F32), _sds((G, 1), F32)],
    )(lam_re, lam_im, log_dt, d_ab, d_f)


def _group_mask(cw, nst):
    row = lax.broadcasted_iota(jnp.int32, (cw, 2 * nst), 0) // SSM_GROUP
    col = (lax.broadcasted_iota(jnp.int32, (cw, 2 * nst), 1) % nst) // SSM_STATE
    return row == col


def _spread(t, mask):
    reps = mask.shape[0] // t.shape[0]
    return jnp.where(mask, jnp.tile(t, (reps, 1)), 0.0).astype(BF16)


def _gather_groups(t, mask):
    t = jnp.where(mask, t, 0.0)
    out = t[0:SSM_GROUP]
    for g in range(1, t.shape[0] // SSM_GROUP):
        out = out + t[g * SSM_GROUP:(g + 1) * SSM_GROUP]
    return out


def _s5_operands(f_ref, br_ref, bi_ref, cr_ref, ci_ref, mask):
    fr, fi = f_ref[0], f_ref[1]
    br, bi = br_ref[...], bi_ref[...]
    bm = _spread(jnp.concatenate([fr * br - fi * bi, fr * bi + fi * br], axis=1), mask)
    cm = _spread(jnp.concatenate([cr_ref[...], -ci_ref[...]], axis=1), mask)
    return bm, cm


def _scan_fwd(xs, k_ref, nst):
    ntile = xs.shape[0] // SUBLANES

    def step(t, carry):
        cr, ci = carry
        r0 = pl.multiple_of(t * SUBLANES, SUBLANES)
        xr = xs[pl.ds(r0, SUBLANES), 0:nst]
        xi = xs[pl.ds(r0, SUBLANES), nst:2 * nst]
        for n, sh in enumerate((1, 2, 4)):
            sr = pltpu.roll(xr, sh, 0)
            si = pltpu.roll(xi, sh, 0)
            mr, mi = k_ref[2 * n], k_ref[2 * n + 1]
            xr, xi = xr + mr * sr - mi * si, xi + mr * si + mi * sr
        pr, pi = k_ref[6], k_ref[7]
        xr, xi = xr + pr * cr - pi * ci, xi + pr * ci + pi * cr
        xs[pl.ds(r0, SUBLANES), 0:nst] = xr
        xs[pl.ds(r0, SUBLANES), nst:2 * nst] = xi
        return (jnp.broadcast_to(xr[SUBLANES - 1:SUBLANES, :], xr.shape),
                jnp.broadcast_to(xi[SUBLANES - 1:SUBLANES, :], xi.shape))

    zero = jnp.zeros((SUBLANES, nst), F32)
    lax.fori_loop(0, ntile, step, (zero, zero))


def _scan_bwd(g, xs, k_ref, nst):
    ntile = g.shape[0] // SUBLANES
    row = lax.broadcasted_iota(jnp.int32, (SUBLANES, nst), 0)

    def step(tt, carry):
        cr, ci, ar, ai = carry
        t = ntile - 1 - tt
        r0 = pl.multiple_of(t * SUBLANES, SUBLANES)
        gr = g[pl.ds(r0, SUBLANES), 0:nst]
        gi = g[pl.ds(r0, SUBLANES), nst:2 * nst]
        for n, sh in enumerate((1, 2, 4)):
            sr = pltpu.roll(gr, SUBLANES - sh, 0)
            si = pltpu.roll(gi, SUBLANES - sh, 0)
            mr, mi = k_ref[8 + 2 * n], k_ref[8 + 2 * n + 1]
            gr, gi = gr + mr * sr - mi * si, gi + mr * si + mi * sr
        qr, qi = k_ref[14], k_ref[15]
        gr, gi = gr + qr * cr - qi * ci, gi + qr * ci + qi * cr
        g[pl.ds(r0, SUBLANES), 0:nst] = gr
        g[pl.ds(r0, SUBLANES), nst:2 * nst] = gi
        p0 = pl.multiple_of(jnp.maximum(t - 1, 0) * SUBLANES, SUBLANES)
        live = (t > 0).astype(F32)
        xr = xs[pl.ds(r0, SUBLANES), 0:nst]
        xi = xs[pl.ds(r0, SUBLANES), nst:2 * nst]
        pr = xs[pl.ds(p0, SUBLANES), 0:nst][SUBLANES - 1:SUBLANES, :] * live
        pi = xs[pl.ds(p0, SUBLANES), nst:2 * nst][SUBLANES - 1:SUBLANES, :] * live
        xmr = jnp.where(row == 0, jnp.broadcast_to(pr, xr.shape), pltpu.roll(xr, 1, 0))
        xmi = jnp.where(row == 0, jnp.broadcast_to(pi, xi.shape), pltpu.roll(xi, 1, 0))
        ar = ar + gr * xmr + gi * xmi
        ai = ai + gi * xmr - gr * xmi
        return (jnp.broadcast_to(gr[0:1, :], gr.shape), jnp.broadcast_to(gi[0:1, :], gi.shape),
                ar, ai)

    zero = jnp.zeros((SUBLANES, nst), F32)
    _, _, ar, ai = lax.fori_loop(0, ntile, step, (zero, zero, zero, zero))
    return _colsum(ar), _colsum(ai)


def _s5_param_specs(cw, nst):
    hp = pl.BlockSpec((SSM_GROUP, nst), lambda b: (0, b))
    return [pl.BlockSpec((2, 1, nst), lambda b: (0, 0, b)), hp, hp, hp, hp,
            pl.BlockSpec((1, cw), lambda b: (0, b)),
            pl.BlockSpec((16, SUBLANES, nst), lambda b: (0, 0, b))]


def s5_fwd(proj, params, nb):
    S = proj.shape[0]
    nst = params[1].shape[1] // nb
    cw = nst // SSM_STATE * SSM_GROUP

    def body(u_ref, f_ref, br_ref, bi_ref, cr_ref, ci_ref, d_ref, k_ref, z_ref, xsb_ref, xs):
        bm, cm = _s5_operands(f_ref, br_ref, bi_ref, cr_ref, ci_ref, _group_mask(cw, nst))
        u = u_ref[...]
        xs[...] = jnp.dot(u.astype(BF16), bm, preferred_element_type=F32)
        _scan_fwd(xs, k_ref, nst)
        xsb = xs[...].astype(BF16)
        xsb_ref[...] = xsb
        y = lax.dot_general(xsb, cm, (((1,), (1,)), ((), ())), preferred_element_type=F32)
        z_ref[...] = _gelu(y + d_ref[...] * u).astype(BF16)

    return pl.pallas_call(
        body, name="s5_fwd", grid=(nb,),
        in_specs=[pl.BlockSpec((S, cw), lambda b: (0, b))] + _s5_param_specs(cw, nst),
        out_specs=[pl.BlockSpec((S, cw), lambda b: (0, b)), pl.BlockSpec((S, 2 * nst), lambda b: (0, b))],
        out_shape=[_sds((S, nb * cw), BF16), _sds((S, nb * 2 * nst), BF16)],
        scratch_shapes=[pltpu.VMEM((S, 2 * nst), F32)],
        compiler_params=_params(("arbitrary",)),
    )(proj, *params)


def s5_bwd(proj, xsb_all, dz, params, nb, after=()):
    S = proj.shape[0]
    nst = params[1].shape[1] // nb
    cw = nst // SSM_STATE * SSM_GROUP

    def body(u_ref, xsb_ref, dz_ref, f_ref, br_ref, bi_ref, cr_ref, ci_ref, d_ref, k_ref,
             du_ref, gbr_ref, gbi_ref, gcr_ref, gci_ref, gf_ref, gd_ref, ga_ref, xs, g):
        mask = _group_mask(cw, nst)
        bm, cm = _s5_operands(f_ref, br_ref, bi_ref, cr_ref, ci_ref, mask)
        u = u_ref[...]
        ub = u.astype(BF16)
        d = d_ref[...]
        xsb = xsb_ref[...]
        xs[...] = xsb.astype(F32)
        y = lax.dot_general(xsb, cm, (((1,), (1,)), ((), ())), preferred_element_type=F32) + d * u
        dy = dz_ref[...].astype(F32) * _gelu_grad(y)
        gd_ref[...] = _colsum(dy * u)
        dyb = dy.astype(BF16)
        gc = _gather_groups(lax.dot_general(dyb, xsb, (((0,), (0,)), ((), ())),
                                            preferred_element_type=F32), mask)
        gcr_ref[...] = gc[:, :nst]
        gci_ref[...] = -gc[:, nst:]
        g[...] = jnp.dot(dyb, cm, preferred_element_type=F32)
        ar, ai = _scan_bwd(g, xs, k_ref, nst)
        ga_ref[0, 0:1, :] = ar
        ga_ref[0, 1:2, :] = ai
        gb = g[...].astype(BF16)
        du = lax.dot_general(gb, bm, (((1,), (1,)), ((), ())), preferred_element_type=F32) + d * dy
        du_ref[...] = du.astype(BF16)
        gbb = _gather_groups(lax.dot_general(ub, gb, (((0,), (0,)), ((), ())),
                                             preferred_element_type=F32), mask)
        dr, di = gbb[:, :nst], gbb[:, nst:]
        fr, fi = f_ref[0], f_ref[1]
        br, bi = br_ref[...], bi_ref[...]
        gbr_ref[...] = fr * dr + fi * di
        gbi_ref[...] = fr * di - fi * dr
        gf_ref[0] = _colsum(dr * br + di * bi)
        gf_ref[1] = _colsum(di * br - dr * bi)

    hp = pl.BlockSpec((SSM_GROUP, nst), lambda b: (0, b))
    hp_sds = _sds((SSM_GROUP, nb * nst), F32)
    return pl.pallas_call(
        _with_after(body, 10, after), name="s5_bwd", grid=(nb,),
        in_specs=[pl.BlockSpec((S, cw), lambda b: (0, b)),
                  pl.BlockSpec((S, 2 * nst), lambda b: (0, b)),
                  pl.BlockSpec((S, cw), lambda b: (0, b))] + _s5_param_specs(cw, nst)
        + [ANY] * len(after),
        out_specs=[pl.BlockSpec((S, cw), lambda b: (0, b)), hp, hp, hp, hp,
                   pl.BlockSpec((2, 1, nst), lambda b: (0, 0, b)),
                   pl.BlockSpec((1, cw), lambda b: (0, b)),
                   pl.BlockSpec((1, 2, nst), lambda b: (b, 0, 0))],
        out_shape=[_sds((S, nb * cw), BF16), hp_sds, hp_sds, hp_sds, hp_sds,
                   _sds((2, 1, nb * nst), F32), _sds((1, nb * cw), F32), _sds((nb, 2, nst), F32)],
        scratch_shapes=[pltpu.VMEM((S, 2 * nst), F32), pltpu.VMEM((S, 2 * nst), F32)],
        compiler_params=_params(("arbitrary",)),
    )(proj, xsb_all, dz, *params, *after)


def _shift_rows(v, k, row, down):
    n = v.shape[0]
    if down:
        return jnp.where(row >= k, pltpu.roll(v, k, 0), 0.0)
    return jnp.where(row < n - k, pltpu.roll(v, n - k, 0), 0.0)


def _window(v, gi, row, down):
    sums = []
    s = v
    for k in (1, 2, 4, 8):
        s = s + _shift_rows(s, k, row, down)
        sums.append(s)
    out = sums[3]
    for n in (2, 1, 0):
        out = jnp.where(gi == n, sums[n], out)
    return out


def pool_fwd(proj, col0, width, gw):
    S = proj.shape[0]
    cb0 = col0 // gw

    def body(u_ref, o_ref):
        gi = pl.program_id(0)
        u = u_ref[...]
        row = lax.broadcasted_iota(jnp.int32, u.shape, 0)
        w = jnp.left_shift(2, gi)
        count = jnp.minimum(row + 1, w).astype(F32)
        o_ref[...] = (_window(u, gi, row, True) / count - u).astype(BF16)

    return pl.pallas_call(
        body, name="pool_fwd", grid=(len(POOL_WINDOWS),),
        in_specs=[pl.BlockSpec((S, gw), lambda g: (0, cb0 + g))],
        out_specs=pl.BlockSpec((S, gw), lambda g: (0, g)),
        out_shape=_sds((S, width), BF16), compiler_params=_params(("arbitrary",)),
    )(proj)


def pool_bwd(dpooled, gw):
    S, width = dpooled.shape

    def body(d_ref, o_ref):
        gi = pl.program_id(0)
        d = d_ref[...]
        row = lax.broadcasted_iota(jnp.int32, d.shape, 0)
        w = jnp.left_shift(2, gi)
        count = jnp.minimum(row + 1, w).astype(F32)
        o_ref[...] = (_window(d / count, gi, row, False) - d).astype(BF16)

    return pl.pallas_call(
        body, name="pool_bwd", grid=(len(POOL_WINDOWS),),
        in_specs=[pl.BlockSpec((S, gw), lambda g: (0, g))],
        out_specs=pl.BlockSpec((S, gw), lambda g: (0, g)),
        out_shape=_sds((S, width), BF16), compiler_params=_params(("arbitrary",)),
    )(dpooled)


def _place():
    x, y, c = lax.axis_index("x"), lax.axis_index("y"), lax.axis_index("c")
    chips = [(1 - x, y), (x, 1 - y), (1 - x, 1 - y)]
    return x, y, c, chips


HBM = pl.BlockSpec(memory_space=pltpu.HBM)


def _gather_body(n, handshake):
    def body(*refs):
        ins, outs = refs[:n], refs[n:2 * n]
        send_sems, recv_sems, local_sems = refs[2 * n:]
        x, y, c, chips = _place()
        if handshake:
            barrier = pltpu.get_barrier_semaphore()
            for peer in [(x, y, 1 - c)] + [(*chip, c) for chip in chips]:
                pl.semaphore_signal(barrier, inc=1, device_id=peer, device_id_type=MESH)
            pl.semaphore_wait(barrier, 4)
        me, sibling = (x, y, c), (x, y, 1 - c)

        def slot(i, p):
            return outs[i].at[4 * p[0] + 2 * p[1] + p[2]]

        def copy(i, k, block, to, src=None):
            return pltpu.make_async_remote_copy(
                src_ref=slot(i, block) if src is None else src, dst_ref=slot(i, block),
                send_sem=send_sems.at[i, k], recv_sem=recv_sems.at[i, k],
                device_id=to, device_id_type=MESH)

        started = []
        for i in range(n):
            for j, chip in enumerate(chips):
                started.append(copy(i, 1 + j, me, (*chip, c), src=ins[i]))
                started[-1].start()
        for i in range(n):
            started.append(copy(i, 0, me, sibling, src=ins[i]))
            started[-1].start()
        mine = [pltpu.make_async_copy(ins[i], slot(i, me), local_sems.at[i]) for i in range(n)]
        for cp in mine:
            cp.start()
        for i in range(n):
            for j, chip in enumerate(chips):
                copy(i, 1 + j, (*chip, c), me).wait_recv()
                started.append(copy(i, 4 + j, (*chip, c), sibling))
                started[-1].start()
        for i in range(n):
            copy(i, 0, sibling, me).wait_recv()
            for j, chip in enumerate(chips):
                copy(i, 4 + j, (*chip, 1 - c), me).wait_recv()
        for cp in started:
            cp.wait_send()
        for cp in mine:
            cp.wait()

    return body


def _routed_gather_body(n):
    def body(*refs):
        ins, outs = refs[:n], refs[n:2 * n]
        send_sems, recv_sems, local_sems = refs[2 * n:]
        x, y, c, (xn, yn, dg) = _place()
        me, sibling = (x, y, c), (x, y, 1 - c)
        barrier = pltpu.get_barrier_semaphore()
        for peer in (sibling, (*xn, c), (*yn, c)):
            pl.semaphore_signal(barrier, inc=1, device_id=peer, device_id_type=MESH)
        pl.semaphore_wait(barrier, 3)

        def piece(i, p, h):
            rows = ins[i].shape[0] // 2
            return outs[i].at[4 * p[0] + 2 * p[1] + p[2], pl.ds(h * rows, rows)]

        def copy(i, k, src, dst, to):
            return pltpu.make_async_remote_copy(src_ref=src, dst_ref=dst, send_sem=send_sems.at[i, k],
                                                recv_sem=recv_sems.at[i, k], device_id=to,
                                                device_id_type=MESH)

        started = []

        def go(cp):
            cp.start()
            started.append(cp)

        for i in range(n):
            rows = ins[i].shape[0] // 2
            for h in range(2):
                own = ins[i].at[pl.ds(h * rows, rows)]
                go(copy(i, 1 + h, own, piece(i, me, h), (*xn, c)))
                go(copy(i, 3 + h, own, piece(i, me, h), (*yn, c)))
        for i in range(n):
            go(copy(i, 0, ins[i], outs[i].at[4 * x + 2 * y + c], sibling))
        mine = [pltpu.make_async_copy(ins[i], outs[i].at[4 * x + 2 * y + c], local_sems.at[i])
                for i in range(n)]
        for cp in mine:
            cp.start()
        for i in range(n):
            for k, chip, h, onward, ksib in ((1, xn, 0, (5, yn), 7), (4, yn, 1, (6, xn), 10),
                                            (2, xn, 1, None, 8), (3, yn, 0, None, 9),
                                            (5, dg, 0, None, 11), (6, dg, 1, None, 12)):
                got = piece(i, (*chip, c), h)
                copy(i, k, got, got, me).wait_recv()
                if onward is not None:
                    go(copy(i, onward[0], got, got, (*onward[1], c)))
                go(copy(i, ksib, got, got, sibling))
        for i in range(n):
            block = outs[i].at[4 * x + 2 * y + 1 - c]
            copy(i, 0, block, block, me).wait_recv()
            for ksib, chip, h in ((7, xn, 0), (10, yn, 1), (8, xn, 1), (9, yn, 0), (11, dg, 0), (12, dg, 1)):
                got = piece(i, (*chip, 1 - c), h)
                copy(i, ksib, got, got, me).wait_recv()
        for cp in started:
            cp.wait_send()
        for cp in mine:
            cp.wait()

    return body


def _on_sequencer(name, body, arrays, out_sds, sems, collective_id):
    ins = [jax.new_ref(a, memory_space=pltpu.MemorySpace.HBM) for a in arrays]
    outs = [jax.empty_ref(s, memory_space=pltpu.MemorySpace.HBM) for s in out_sds]

    @pl.kernel(mesh=plsc.ScalarSubcoreMesh(axis_name="sequencer", num_cores=1), name=name,
               scratch_types=tuple(sems),
               compiler_params=pltpu.CompilerParams(collective_id=collective_id))
    def launch(*sem_refs):
        body(*ins, *outs, *sem_refs)

    launch()
    return [o[...] for o in outs]


def seq_all_gather(name, shards, collective_id, routed=True):
    n = len(shards)
    nsem = 13 if routed else 7
    return _on_sequencer(
        name, _routed_gather_body(n) if routed else _gather_body(n, True), shards,
        [_sds((NDEV,) + s.shape, s.dtype) for s in shards],
        [pltpu.SemaphoreType.DMA((n, nsem)), pltpu.SemaphoreType.DMA((n, nsem)),
         pltpu.SemaphoreType.DMA((n,))], collective_id)


def pair_exchange(name, grads, collective_id):
    def plan(srcs, lands):
        x, y, c, _ = _place()
        return ([(i, q, srcs[i].at[2 * q + 1 - c], lands[i].at[q], (x, y, 1 - c))
                 for i in range(len(srcs)) for q in range(NCHIP)], [(x, y, 1 - c)])

    return _split_exchange(name, grads, [_sds((NCHIP,) + g.shape[1:], g.dtype) for g in grads],
                           plan, NCHIP, collective_id)


SEM = pl.BlockSpec(memory_space=pltpu.SEMAPHORE)


def _split_exchange(name, srcs, land_sds, plan, ncopy, collective_id):
    n = len(srcs)
    nsem = n * ncopy
    effect = pltpu.SideEffectType.DATAFLOW_SIDE_EFFECTING

    def descriptors(src_refs, land_refs, send_sems, recv_sems):
        copies, peers = plan(src_refs, land_refs)
        return [pltpu.make_async_remote_copy(src_ref=s, dst_ref=d, send_sem=send_sems[i * ncopy + k],
                                             recv_sem=recv_sems[i * ncopy + k], device_id=to,
                                             device_id_type=MESH) for (i, k, s, d, to) in copies], peers

    def start_body(*refs):
        src_refs, land_refs = refs[:n], refs[n:2 * n]
        send_sems, recv_sems = refs[2 * n:2 * n + nsem], refs[2 * n + nsem:2 * n + 2 * nsem]
        token = refs[-1]
        cps, peers = descriptors(src_refs, land_refs, send_sems, recv_sems)
        barrier = pltpu.get_barrier_semaphore()
        for peer in peers:
            pl.semaphore_signal(barrier, inc=1, device_id=peer, device_id_type=MESH)
        pl.semaphore_wait(barrier, len(peers))
        for cp in cps:
            cp.start()
        token[...] = jnp.zeros_like(token)

    lands = [pltpu.with_memory_space_constraint(lax.empty(s.shape, s.dtype), pltpu.HBM) for s in land_sds]
    srcs = [pltpu.with_memory_space_constraint(s, pltpu.HBM) for s in srcs]
    res = pl.pallas_call(
        start_body, name=name + "_start",
        out_shape=(pltpu.SemaphoreType.DMA(()),) * (2 * nsem)
        + tuple(pltpu.HBM(s.shape, s.dtype) for s in srcs)
        + tuple(pltpu.HBM(s.shape, s.dtype) for s in land_sds) + (_sds((SUBLANES, LANES), F32),),
        in_specs=[HBM] * (2 * n),
        out_specs=(SEM,) * (2 * nsem) + (HBM,) * (2 * n) + (pl.BlockSpec(memory_space=pltpu.VMEM),),
        input_output_aliases={i: 2 * nsem + i for i in range(2 * n)},
        compiler_params=pltpu.CompilerParams(has_side_effects=effect, collective_id=collective_id),
    )(*srcs, *lands)
    sems = res[:2 * nsem]
    thru = res[2 * nsem:2 * nsem + 2 * n]
    token = res[-1]

    def wait(after):
        def wait_body(*refs):
            src_refs, land_refs = refs[:n], refs[n:2 * n]
            cps, _ = descriptors(src_refs, land_refs, refs[2 * n:2 * n + nsem],
                                 refs[2 * n + nsem:2 * n + 2 * nsem])
            for cp in cps:
                cp.wait_send()
            for cp in cps:
                cp.wait_recv()

        out = pl.pallas_call(
            wait_body, name=name + "_wait",
            out_shape=tuple(pltpu.HBM(s.shape, s.dtype) for s in srcs)
            + tuple(pltpu.HBM(s.shape, s.dtype) for s in land_sds),
            in_specs=[HBM] * (2 * n) + [SEM] * (2 * nsem) + [pl.BlockSpec(memory_space=pl.ANY)],
            out_specs=(HBM,) * (2 * n),
            input_output_aliases={i: i for i in range(2 * n)},
            compiler_params=pltpu.CompilerParams(has_side_effects=effect),
        )(*thru, *sems, after)
        return list(out[:n]), list(out[n:])

    return token, wait


def pair_sum(name, grad, got, place):
    shp = grad.shape[1:]
    r, cdim = shp[-2], shp[-1]
    lead = int(math.prod(shp[:-2])) if len(shp) > 2 else 1
    g5 = grad.reshape(NCHIP, 2, lead * r, cdim)
    t4 = got.reshape(NCHIP, lead * r, cdim)
    R = lead * r
    tr = _tile(R, max(8, (1 << 20) // cdim))

    def body(p_ref, g_ref, t_ref, o_ref):
        o_ref[...] = (g_ref[0].astype(F32) + t_ref[...].astype(F32)).astype(o_ref.dtype)

    out = pl.pallas_call(
        body, name=name,
        grid_spec=pltpu.PrefetchScalarGridSpec(
            num_scalar_prefetch=1, grid=(NCHIP - 1, R // tr),
            in_specs=[pl.BlockSpec((1, 1, tr, cdim), lambda j, i, p: (p[1] ^ (j + 1), p[0], i, 0)),
                      pl.BlockSpec((1, tr, cdim), lambda j, i, p: (p[1] ^ (j + 1), i, 0))],
            out_specs=pl.BlockSpec((1, tr, cdim), lambda j, i, p: (p[1] ^ (j + 1), i, 0))),
        out_shape=_sds((NCHIP, R, cdim), grad.dtype),
        compiler_params=_params(("parallel", "parallel")),
    )(place, g5, t4)
    return out


def chip_exchange(name, parts, collective_id):
    def plan(srcs, lands):
        x, y, c, chips = _place()
        return ([(i, j, srcs[i].at[2 * chip[0] + chip[1]], lands[i].at[j], (*chip, c))
                 for i in range(len(srcs)) for j, chip in enumerate(chips)],
                [(*chip, c) for chip in chips])

    return _split_exchange(name, parts, [_sds((3,) + p.shape[1:], p.dtype) for p in parts],
                           plan, 3, collective_id)


def ada_fwd(c_row, w_ada, b_ada):
    D, cols = w_ada.shape

    def body(c_ref, w_ref, b_ref, mod_ref, call_ref, act8, part, s1, r1, s2, r2):
        x, y, c, _ = _place()
        me = 4 * x + 2 * y + c
        call_ref[me] = c_ref[...]
        cps = []
        for k in range(1, NDEV):
            to = (x ^ (k >> 2), y ^ ((k >> 1) & 1), c ^ (k & 1))
            cps.append(pltpu.make_async_remote_copy(
                src_ref=c_ref, dst_ref=call_ref.at[me], send_sem=s1.at[k - 1],
                recv_sem=r1.at[k - 1], device_id=to, device_id_type=MESH))
            cps[-1].start()
        for cp in cps:
            cp.wait()
        for b in range(NDEV):
            act8[b:b + 1, :] = call_ref[b]
        cv = act8[...]
        act = (cv * _sigmoid(cv)).astype(BF16)
        res = jnp.dot(act, w_ref[...].astype(BF16), preferred_element_type=F32)
        for b in range(NDEV):
            part[b] = res[b:b + 1, :]
        mod_ref[me] = part[me]
        cps = []
        for k in range(1, NDEV):
            to = (x ^ (k >> 2), y ^ ((k >> 1) & 1), c ^ (k & 1))
            dst = 4 * to[0] + 2 * to[1] + to[2]
            cps.append(pltpu.make_async_remote_copy(
                src_ref=part.at[dst], dst_ref=mod_ref.at[me], send_sem=s2.at[k - 1],
                recv_sem=r2.at[k - 1], device_id=to, device_id_type=MESH))
            cps[-1].start()
        for cp in cps:
            cp.wait()
        for b in range(NDEV):
            mod_ref[b] = mod_ref[b] + b_ref[b]

    vm = pl.BlockSpec(memory_space=pltpu.VMEM)
    return pl.pallas_call(
        body, name="ada_fwd", in_specs=[vm, vm, vm], out_specs=[vm, vm],
        out_shape=[_sds((NDEV, 1, cols), F32), _sds((NDEV, 1, D), F32)],
        scratch_shapes=[pltpu.VMEM((NDEV, D), F32), pltpu.VMEM((NDEV, 1, cols), F32),
                        pltpu.SemaphoreType.DMA((NDEV - 1,)), pltpu.SemaphoreType.DMA((NDEV - 1,)),
                        pltpu.SemaphoreType.DMA((NDEV - 1,)), pltpu.SemaphoreType.DMA((NDEV - 1,))],
        compiler_params=pltpu.CompilerParams(vmem_limit_bytes=VMEM_LIMIT),
    )(c_row, w_ada, b_ada.reshape(NDEV, 1, cols))


def _adamw_math(g, w, m, v):
    m2 = ADAM_B1 * m + (1.0 - ADAM_B1) * g
    v2 = ADAM_B2 * v + (1.0 - ADAM_B2) * (g * g)
    m_hat = m2 / (1.0 - ADAM_B1 ** ADAM_STEP)
    v_hat = v2 / (1.0 - ADAM_B2 ** ADAM_STEP)
    delta = -ADAM_LR * (m_hat / (jnp.sqrt(v_hat) + ADAM_EPS) + ADAM_WD * w)
    return delta, m2, v2


def adamw_sharded(name, grad8, pair4, got3, w, m, v, place, after=()):
    shape = w.shape
    cdim = shape[-1]
    R = int(math.prod(shape[:-1]))
    w2, m2, v2 = (t.reshape(R, cdim) for t in (w, m, v))
    tr = _tile(R, max(8, (1 << 19) // cdim))

    def body(q_ref, own_ref, sib_ref, t_ref, w_ref, m_ref, v_ref, g_out, d_out, m_out, v_out):
        g = own_ref[0].astype(F32) + sib_ref[0].astype(F32)
        for j in range(3):
            g = g + t_ref[j].astype(F32)
        d, mn, vn = _adamw_math(g, w_ref[...], m_ref[...], v_ref[...])
        g_out[...] = g
        d_out[...] = d
        m_out[...] = mn
        v_out[...] = vn

    spec = pl.BlockSpec((tr, cdim), lambda i, qr: (i, 0))
    outs = pl.pallas_call(
        _with_after(body, 7, after), name=name,
        grid_spec=pltpu.PrefetchScalarGridSpec(
            num_scalar_prefetch=1, grid=(R // tr,),
            in_specs=[pl.BlockSpec((1, tr, cdim), lambda i, qr: (qr[2], i, 0)),
                      pl.BlockSpec((1, tr, cdim), lambda i, qr: (qr[1], i, 0)),
                      pl.BlockSpec((3, tr, cdim), lambda i, qr: (0, i, 0)), spec, spec, spec]
            + [ANY] * len(after),
            out_specs=[spec] * 4),
        out_shape=[_sds((R, cdim), F32)] * 4,
        compiler_params=_params(("parallel",)),
    )(place, grad8.reshape(NDEV, R, cdim), pair4.reshape(NCHIP, R, cdim),
      got3.reshape(3, R, cdim), w2, m2, v2, *after)
    return [o.reshape(shape) for o in outs]


def adamw_side(grad8, pair4, got3, w, m, v, place, nsteps):
    cdim = w.shape[-1]
    R = int(math.prod(w.shape[:-1]))
    tr = R // nsteps
    assert tr * nsteps == R and tr % 16 == 0

    def fn(ins, outs):
        own_ref, sib_ref, t_ref, w_ref, m_ref, v_ref = ins
        g = own_ref[0].astype(F32) + sib_ref[0].astype(F32)
        for j in range(3):
            g = g + t_ref[j].astype(F32)
        d, mn, vn = _adamw_math(g, w_ref[...], m_ref[...], v_ref[...])
        for r, val in zip(outs, (g, d, mn, vn)):
            r[...] = val

    rows = lambda t, p: (t, 0)
    s_ins = [(grad8.reshape(NDEV, R, cdim), (1, tr, cdim), lambda t, p: (p[2], t, 0)),
             (pair4.reshape(NCHIP, R, cdim), (1, tr, cdim), lambda t, p: (p[1], t, 0)),
             (got3.reshape(3, R, cdim), (3, tr, cdim), lambda t, p: (0, t, 0))]
    s_ins += [(t.reshape(R, cdim), (tr, cdim), rows) for t in (w, m, v)]
    s_outs = [(_sds((R, cdim), F32), (tr, cdim), rows)] * 4
    return place, s_ins, s_outs, fn


def adamw_small(parts, w, m, v, after=()):
    R = w.shape[0]
    tr = R

    def body(p_ref, w_ref, m_ref, v_ref, g_out, d_out, m_out, v_out):
        g = p_ref[0]
        for j in range(1, NDEV):
            g = g + p_ref[j]
        d, mn, vn = _adamw_math(g, w_ref[...], m_ref[...], v_ref[...])
        g_out[...] = g
        d_out[...] = d
        m_out[...] = mn
        v_out[...] = vn

    spec = pl.BlockSpec((tr, LANES), lambda i: (i, 0))
    return pl.pallas_call(
        _with_after(body, 4, after), name="adamw_small", grid=(R // tr,),
        in_specs=[pl.BlockSpec((NDEV, tr, LANES), lambda i: (0, i, 0)), spec, spec, spec]
        + [ANY] * len(after),
        out_specs=[spec] * 4, out_shape=[_sds((R, LANES), F32)] * 4,
        compiler_params=_params(("parallel",)),
    )(parts, w, m, v, *after)


def adamw_ada(c_all_t, dmod_all, w, m, v, my_dev):
    D, cols = w.shape
    tr = _tile(D, 256)

    def body(k_ref, c_ref, d_ref, w_ref, m_ref, v_ref, g_out, d_out, m_out, v_out):
        cv = c_ref[...]
        act = cv * _sigmoid(cv)
        dm = d_ref[...]
        g = act[:, 0:1] * dm[0:1, :]
        for b in range(1, NDEV):
            g = g + act[:, b:b + 1] * dm[b:b + 1, :]
        d, mn, vn = _adamw_math(g, w_ref[...], m_ref[...], v_ref[...])
        g_out[...] = g
        d_out[...] = d
        m_out[...] = mn
        v_out[...] = vn

    spec = pl.BlockSpec((tr, cols), lambda i, kr: (i, 0))
    return pl.pallas_call(
        body, name="adamw_ada",
        grid_spec=pltpu.PrefetchScalarGridSpec(
            num_scalar_prefetch=1, grid=(D // tr,),
            in_specs=[pl.BlockSpec((tr, NDEV), lambda i, kr: (i, 0)),
                      pl.BlockSpec((NDEV, cols), lambda i, kr: (0, kr[0])), spec, spec, spec],
            out_specs=[spec] * 4),
        out_shape=[_sds((D, cols), F32)] * 4,
        compiler_params=_params(("parallel",)),
    )(my_dev, c_all_t, dmod_all, w, m, v)


def _small_pack(parts):
    rows = []
    for p in parts:
        flat = p.reshape(-1)
        flat = jnp.pad(flat, (0, (-flat.shape[0]) % (SUBLANES * LANES)))
        rows.append(flat.reshape(-1, LANES))
    return jnp.concatenate(rows, axis=0)


def _small_unpack(buf, shapes):
    out, r = [], 0
    for s in shapes:
        n = int(math.prod(s))
        nr = -(-n // (SUBLANES * LANES)) * SUBLANES
        out.append(buf[r:r + nr].reshape(-1)[:n].reshape(s))
        r += nr
    return out


def kernel(x, c, w_ada, b_ada, w_in, lam_re, lam_im, log_dt, ssm_b_re, ssm_b_im, ssm_c_re, ssm_c_im, ssm_d, w_glu_val, w_glu_gate, w_pool, pool_scale, w_pool_out, w_out, ln1_g, ln1_b, w_ff1, w_ff2, ln2_g, ln2_b, loss_target, m_w_ada, m_b_ada, m_w_in, m_lam_re, m_lam_im, m_log_dt, m_ssm_b_re, m_ssm_b_im, m_ssm_c_re, m_ssm_c_im, m_ssm_d, m_w_glu_val, m_w_glu_gate, m_w_pool, m_pool_scale, m_w_pool_out, m_w_out, m_ln1_g, m_ln1_b, m_w_ff1, m_w_ff2, m_ln2_g, m_ln2_b, v_w_ada, v_b_ada, v_w_in, v_lam_re, v_lam_im, v_log_dt, v_ssm_b_re, v_ssm_b_im, v_ssm_c_re, v_ssm_c_im, v_ssm_d, v_w_glu_val, v_w_glu_gate, v_w_pool, v_pool_scale, v_w_pool_out, v_w_out, v_ln1_g, v_ln1_b, v_w_ff1, v_w_ff2, v_ln2_g, v_ln2_b):
    S, D = x.shape[1], x.shape[2]
    x2d, tgt = x[0], loss_target[0]
    W = D // 2
    G = W // SSM_GROUP
    P, H, GPB = SSM_STATE, SSM_GROUP, GROUPS_PER_BLOCK
    nblk = G // GPB
    gw = W // len(POOL_WINDOWS)
    ax, ay, ac = lax.axis_index("x"), lax.axis_index("y"), lax.axis_index("c")
    my_dev = (4 * ax + 2 * ay + ac).astype(jnp.int32).reshape(1)
    place = jnp.stack([ac, 2 * ax + ay, 4 * ax + 2 * ay + ac]).astype(jnp.int32)
    ts = _tile(S, 256)

    glu = jnp.stack([w_glu_val[0], w_glu_gate[0]]).astype(BF16)
    shards = [w_in[0].astype(BF16), glu, w_pool[0].astype(BF16), w_pool_out[0].astype(BF16),
              w_out[0].astype(BF16), w_ff1[0].astype(BF16), w_ff2[0].astype(BF16)]
    wg_in, wg_pool = seq_all_gather("gather_w_in", [shards[0], shards[2]], 1)
    wg_vg, wg_po, wg_out = seq_all_gather("gather_w_mix", [shards[1], shards[3], shards[4]], 2)
    (wg_ff1,) = seq_all_gather("gather_w_ff1", shards[5:6], 3)
    (wg_ff2,) = seq_all_gather("gather_w_ff2", shards[6:7], 11)
    wg_vg = wg_vg.reshape(2 * NDEV, W, D // NDEV)
    nwin = len(POOL_WINDOWS)
    wp_full = jnp.transpose(wg_pool, (1, 0, 2, 3)).reshape(nwin, gw, gw)
    wout_full = wg_out.reshape(1, D, D)
    wff2_full = wg_ff2.reshape(1, 4 * D, D)

    small_names = [b_ada, lam_re, lam_im, log_dt, ssm_b_re, ssm_b_im, ssm_c_re, ssm_c_im, ssm_d,
                   pool_scale, ln1_g, ln1_b, ln2_g, ln2_b]
    small_m = [m_b_ada, m_lam_re, m_lam_im, m_log_dt, m_ssm_b_re, m_ssm_b_im, m_ssm_c_re, m_ssm_c_im,
               m_ssm_d, m_pool_scale, m_ln1_g, m_ln1_b, m_ln2_g, m_ln2_b]
    small_v = [v_b_ada, v_lam_re, v_lam_im, v_log_dt, v_ssm_b_re, v_ssm_b_im, v_ssm_c_re, v_ssm_c_im,
               v_ssm_d, v_pool_scale, v_ln1_g, v_ln1_b, v_ln2_g, v_ln2_b]
    zero_row = jnp.zeros((1, LANES), F32)
    packed_wmv = [_small_pack(t + [zero_row]) for t in (small_names, small_m, small_v)]

    mod, c_all = ada_fwd(c, w_ada[0], b_ada)
    mod = mod.reshape(6, 1, D)
    sh1, sc1, g1, sh2, sc2, g2 = (mod[i] for i in range(6))

    f2, kconst = s5_disc(lam_re[0], lam_im[0], log_dt[0].reshape(G, 1))
    kconst = kconst.reshape(16, SUBLANES, G * P)
    f2r = f2.reshape(2, 1, G * P)
    bt_re = jnp.transpose(ssm_b_re[0], (2, 0, 1)).reshape(H, G * P)
    bt_im = jnp.transpose(ssm_b_im[0], (2, 0, 1)).reshape(H, G * P)
    ct_re = jnp.transpose(ssm_c_re[0], (1, 0, 2)).reshape(H, G * P)
    ct_im = jnp.transpose(ssm_c_im[0], (1, 0, 2)).reshape(H, G * P)
    s5_params = (f2r, bt_re, bt_im, ct_re, ct_im, ssm_d, kconst)

    def e1(t, b):
        xhat, _ = _ln_stats(t[0])
        return [xhat * (1.0 + b[0]) + b[1]], []
    (h1,) = _rowwise("ln_mod1", e1, S, ts, [(x2d, D, 0)], [sc1, sh1], [(D, BF16)], [])

    (proj,) = mm_nn("proj", h1, wg_in, F32, 2)
    z, xsb_all = s5_fwd(proj, s5_params, nblk)
    (vt,) = mm_nn("glu", z, wg_vg, BF16, 4)
    pooled = pool_fwd(proj, W, W, gw)

    def pool_epi(vals, ex, outs):
        a = vals[0]
        outs[0][...] = a
        outs[1][...] = (a * ex[0][...]).astype(BF16)
    tmp = _tile(S, 1024)
    yp, ypool = _mm(
        "pool_mix", "nn", pooled, wp_full.astype(BF16), (S // tmp, nwin, 1),
        pl.BlockSpec((tmp, gw), lambda i, j, k: (i, j)), pl.BlockSpec((1, gw, gw), lambda i, j, k: (j, 0, 0)),
        [(_sds((S, W), F32), pl.BlockSpec((tmp, gw), lambda i, j, k: (i, j))),
         (_sds((S, W), BF16), pl.BlockSpec((tmp, gw), lambda i, j, k: (i, j)))],
        (tmp, gw), 1, gw, None, pool_epi,
        [(pool_scale, pl.BlockSpec((1, gw), lambda i, j, k: (0, j)))])
    (y_b,) = mm_nn("pool_out", ypool, wg_po, BF16, 4)

    cb = D // NDEV
    ga_cb, gb_cb = (2 * W) // cb, (2 * W + D) // cb
    mcb = 4
    wm = mcb * cb
    tsm = _tile(S, 256)

    def merge_call(name, fn, ins, n_out, after=()):
        def body(*refs):
            vals = [r[...].astype(F32) for r in refs[:len(ins)]]
            for r, v in zip(refs[len(ins):], fn(*vals)):
                r[...] = v.astype(r.dtype)
        return pl.pallas_call(
            _with_after(body, len(ins), after), name=name, grid=(S // tsm, NDEV // mcb),
            in_specs=[pl.BlockSpec((tsm, w), f) for (_, w, f) in ins] + [ANY] * len(after),
            out_specs=[pl.BlockSpec((tsm, w), lambda i, j: (i, j)) for (_, w) in n_out],
            out_shape=[_sds((S, cols), BF16) for (cols, _) in n_out],
            compiler_params=_params(("parallel", "parallel")),
        )(*[a for (a, _, _) in ins], *after)

    merge_ins = [(proj, wm, lambda i, j: (i, ga_cb // mcb + j)), (proj, wm, lambda i, j: (i, gb_cb // mcb + j)),
                 (vt, 2 * wm, lambda i, j: (i, j)), (y_b, wm, lambda i, j: (i, j))]

    def val_gate(vtv):
        return (jnp.concatenate([vtv[:, 2 * q * cb:(2 * q + 1) * cb] for q in range(mcb)], axis=1),
                jnp.concatenate([vtv[:, (2 * q + 1) * cb:(2 * q + 2) * cb] for q in range(mcb)], axis=1))

    def merge_f(ga, gb, vtv, yb):
        vv, tt = val_gate(vtv)
        return [_sigmoid(ga) * (vv * _sigmoid(tt)) + _sigmoid(gb) * yb]
    (merged,) = merge_call("merge", merge_f, merge_ins, [(D, wm)])

    (mix,) = mm_nn("mix_out", merged, wout_full, F32, 1)

    def e3(t, b):
        xv, mx = t
        g1v, l1g, l1b, sc2v, sh2v = b
        r1 = ALPHA * xv + g1v * mx
        xh1, _ = _ln_stats(r1)
        x1 = xh1 * l1g + l1b
        xh, _ = _ln_stats(x1)
        return [r1, xh * (1.0 + sc2v) + sh2v], []
    r1, h2 = _rowwise("post_mix", e3, S, ts, [(x2d, D, 0), (mix, D, 0)],
                      [g1, ln1_g, ln1_b, sc2, sh2], [(D, F32), (D, BF16)], [], after=packed_wmv)

    def relu_epi(vals, ex, outs):
        outs[0][...] = jnp.maximum(vals[0], 0.0).astype(BF16)
    (rl,) = mm_nn("ff1", h2, wg_ff1, BF16, 1, epi=relu_epi)

    def square(a):
        return a * a
    (y2,) = mm_nn("ff2", rl, wff2_full, F32, 1, pro=square)

    def e4(t, b):
        r1v, y2v, tg = t
        g2v, l1g, l1b, l2g, l2b = b
        xh1, _ = _ln_stats(r1v)
        x1 = xh1 * l1g + l1b
        r2 = ALPHA * x1 + g2v * y2v
        xh2, rs2 = _ln_stats(r2)
        err = xh2 * l2g + l2b - tg
        dx2 = err * (1.0 / D)
        dr2 = _ln_bwd(dx2 * l2g, xh2, rs2)
        lsum = jnp.sum(_colsum(err * err), axis=1, keepdims=True) * (0.5 / D)
        return ([ALPHA * dr2, g2v * dr2],
                [jnp.broadcast_to(lsum, (1, LANES)), _colsum(dx2 * xh2), _colsum(dx2), _colsum(dr2 * y2v)])
    dx1a, dy2, loss_acc, g_ln2g, g_ln2b, d_g2 = _rowwise(
        "head", e4, S, ts, [(r1, D, 0), (y2, D, 0), (tgt, D, 0)], [g2, ln1_g, ln1_b, ln2_g, ln2_b],
        [(D, F32), (D, BF16)], [LANES, D, D, D])

    tn_ff = _tile(4 * D, 1024)

    def dff_epi(vals, ex, outs):
        outs[0][...] = (vals[0] * (2.0 * ex[0][...].astype(F32))).astype(BF16)
    tmf = _tile(S, 1024)
    (da1,) = mm_nt("d_ff2", dy2, wff2_full, BF16, 1, tn=tn_ff, epi=dff_epi,
                   extras=[(rl, pl.BlockSpec((tmf, tn_ff), lambda i, j, k: (i, j)))])
    gw_ff2 = mm_tn("gw_ff2", rl, dy2, BF16, NDEV, 0, pro=square)
    gw_ff1 = mm_tn("gw_ff1", h2, da1, BF16, NDEV, 1)
    tok, wait_pair_a = pair_exchange("pair_exchange_ff", [gw_ff2, gw_ff1], 4)
    (dh2,) = mm_nt("d_ff1", da1, wg_ff1, F32, 4, after=[tok])

    def e5(t, b):
        dh2v, r1v, dx1av, mx = t
        sc2v, l1g, l1b, g1v = b
        xh1, rs1 = _ln_stats(r1v)
        x1 = xh1 * l1g + l1b
        xh, rs = _ln_stats(x1)
        dx1 = dx1av + _ln_bwd(dh2v * (1.0 + sc2v), xh, rs)
        dr1 = _ln_bwd(dx1 * l1g, xh1, rs1)
        return ([ALPHA * dr1, g1v * dr1],
                [_colsum(dh2v * xh), _colsum(dh2v), _colsum(dx1 * xh1), _colsum(dx1), _colsum(dr1 * mx)])
    dxa, dmix, d_sc2, d_sh2, g_ln1g, g_ln1b, d_g1 = _rowwise(
        "post_mix_bwd", e5, S, ts, [(dh2, D, 0), (r1, D, 0), (dx1a, D, 0), (mix, D, 0)],
        [sc2, ln1_g, ln1_b, g1], [(D, F32), (D, BF16)], [D, D, D, D, D])

    (dmerged,) = mm_nt("d_mix_out", dmix, wout_full, BF16, 1)
    gw_out = mm_tn("gw_out", merged, dmix, BF16, NDEV, 0)
    grads_a, got_a = wait_pair_a(gw_out)
    parts_a = [pair_sum("pair_sum_ff%d" % i, g, t, place) for i, (g, t) in enumerate(zip(grads_a, got_a))]
    tok, wait_chip_a = chip_exchange("chip_exchange_ff", parts_a, 5)

    def merge_b(ga, gb, vtv, yb, dm):
        vv, tt = val_gate(vtv)
        sa, sb, st = _sigmoid(ga), _sigmoid(gb), _sigmoid(tt)
        dya = dm * sa
        dv, dt = dya * st, dya * vv * st * (1.0 - st)
        dvt_tile = jnp.concatenate([t[:, q * cb:(q + 1) * cb] for q in range(mcb) for t in (dv, dt)], axis=1)
        return [dm * (vv * st) * sa * (1.0 - sa), dm * yb * sb * (1.0 - sb), dvt_tile, dm * sb]
    dga, dgb_, dvt, dy_b = merge_call(
        "merge_bwd", merge_b, merge_ins + [(dmerged, wm, lambda i, j: (i, j))],
        [(D, wm), (D, wm), (2 * D, 2 * wm), (D, wm)], after=[tok])

    (dypool,) = mm_nt("d_pool_out", dy_b, wg_po, F32, NDEV)
    gw_po = mm_tn("gw_pool_out", ypool, dy_b, BF16, NDEV, 4)

    def e7(t, b):
        return [t[0] * b[0]], [_colsum(t[0] * t[1])]
    dyp, g_pscale = _rowwise("pool_scale_bwd", e7, S, ts, [(dypool, W, 0), (yp, W, 0)],
                             [pool_scale], [(W, BF16)], [W])
    (dpooled,) = _mm(
        "d_pool_mix", "nt", dyp, wp_full.astype(BF16), (S // tmp, nwin, 1),
        pl.BlockSpec((tmp, gw), lambda i, j, k: (i, j)), pl.BlockSpec((1, gw, gw), lambda i, j, k: (j, 0, 0)),
        [(_sds((S, W), F32), pl.BlockSpec((tmp, gw), lambda i, j, k: (i, j)))], (tmp, gw), 1, gw)
    tkp = _tile(S, 2048)
    gw_pool = _mm(
        "gw_pool", "tn", pooled, dyp, (nwin, 1, S // tkp),
        pl.BlockSpec((tkp, gw), lambda i, j, k: (k, i)), pl.BlockSpec((tkp, gw), lambda i, j, k: (k, i)),
        [(_sds((nwin, gw, gw), BF16), pl.BlockSpec((1, gw, gw), lambda i, j, k: (i, 0, 0)))],
        (gw, gw), 1, gw, stacked_out=True)[0]
    du_pool = pool_bwd(dpooled, gw)

    (dz,) = mm_nt("d_glu", dvt, wg_vg, BF16, 2 * NDEV)
    gw_vg = mm_tn("gw_glu", z, dvt, BF16, 2 * NDEV, 4)
    gw_pool_st = jnp.transpose(gw_pool.reshape(nwin, NDEV, gw // NDEV, gw), (1, 0, 2, 3))
    grads_b = [gw_out, gw_po, gw_pool_st, gw_vg.reshape(NDEV, 2, W, D // NDEV)]
    tok, wait_pair_b = pair_exchange("pair_exchange_mix", grads_b, 6)
    du_ssm, g_bt_re, g_bt_im, g_ct_re, g_ct_im, g_f, g_d, g_a = s5_bwd(
        proj, xsb_all, dz, s5_params, nblk, after=[tok])
    grads_b, got_b = wait_pair_b(du_ssm)
    parts_b = [pair_sum("pair_sum_mix%d" % i, g, t, place) for i, (g, t) in enumerate(zip(grads_b, got_b))]
    tok, wait_chip_b = chip_exchange("chip_exchange_mix", parts_b, 7)

    dproj = jnp.concatenate([du_ssm, du_pool, dga, dgb_], axis=1)
    gw_in = mm_tn("gw_in", h1, dproj, BF16, NDEV, 1, after=[tok])
    tok, wait_pair_c = pair_exchange("pair_exchange_in", [gw_in], 8)
    _, got3_a = wait_chip_a(tok)
    dh1, *u_ff2 = mm_nt(
        "d_proj", dproj, wg_in, F32, 2, after=[tok],
        side=lambda n: adamw_side(grads_a[0], got_a[0], got3_a[0], w_ff2[0], m_w_ff2[0], v_w_ff2[0], place, n))
    u_ff2 = [t.reshape(w_ff2[0].shape) for t in u_ff2]
    grads_c, got_c = wait_pair_c(dh1)
    parts_c = [pair_sum("pair_sum_in", grads_c[0], got_c[0], place)]
    tok, wait_chip_c = chip_exchange("chip_exchange_in", parts_c, 9)

    def e10(t, b):
        dh1v, xv, dxav = t
        xh, rs = _ln_stats(xv)
        return ([dxav + _ln_bwd(dh1v * (1.0 + b[0]), xh, rs)],
                [_colsum(dh1v * xh), _colsum(dh1v)])
    grad_x, d_sc1, d_sh1 = _rowwise("ln_mod1_bwd", e10, S, ts, [(dh1, D, 0), (x2d, D, 0), (dxa, D, 0)],
                                    [sc1], [(D, F32)], [D, D], after=[tok])

    g_b_re = jnp.transpose(g_bt_re.reshape(H, G, P), (1, 2, 0))
    g_b_im = jnp.transpose(g_bt_im.reshape(H, G, P), (1, 2, 0))
    g_c_re = jnp.transpose(g_ct_re.reshape(H, G, P), (1, 0, 2))
    g_c_im = jnp.transpose(g_ct_im.reshape(H, G, P), (1, 0, 2))
    d_ab = jnp.transpose(g_a.reshape(nblk, 2, GPB, P), (1, 0, 2, 3)).reshape(2, G, P)
    g_lr, g_li, g_ldt = s5_disc_bwd(lam_re[0], lam_im[0], log_dt[0].reshape(G, 1), d_ab,
                                    g_f.reshape(2, G, P))

    dmod = jnp.concatenate([d_sh1, d_sc1, d_g1, d_sh2, d_sc2, d_g2], axis=1)
    small_g = [dmod, g_lr, g_li, g_ldt, g_b_re, g_b_im, g_c_re, g_c_im, g_d, g_pscale,
               g_ln1g, g_ln1b, g_ln2g, g_ln2b, loss_acc]
    packed_g = _small_pack(small_g)
    (parts_all,) = seq_all_gather("gather_small", [packed_g], 10, routed=False)
    glu_w = jnp.stack([w_glu_val[0], w_glu_gate[0]])
    glu_m = jnp.stack([m_w_glu_val[0], m_w_glu_gate[0]])
    glu_v = jnp.stack([v_w_glu_val[0], v_w_glu_gate[0]])
    wmv = [(w_out[0], m_w_out[0], v_w_out[0]), (w_pool_out[0], m_w_pool_out[0], v_w_pool_out[0]),
           (w_pool[0], m_w_pool[0], v_w_pool[0]), (glu_w, glu_m, glu_v)]
    u_ff1 = adamw_sharded("adamw_1", grads_a[1], got_a[1], got3_a[1], w_ff1[0], m_w_ff1[0], v_w_ff1[0],
                          place, after=[packed_g])
    _, got3_b = wait_chip_b(u_ff1[0])
    upd = [adamw_sharded("adamw_%d" % (2 + i), g, p, t, w, m, v, place)
           for i, (g, p, t, (w, m, v)) in enumerate(zip(grads_b, got_b, got3_b, wmv))]
    u_out, u_po, u_pool, u_glu = upd

    sg, sd, sm, sv = adamw_small(parts_all, *packed_wmv, after=[upd[-1][0]])
    shapes = [t.shape for t in small_names]
    loss = _small_unpack(sg, shapes + [(1, LANES)])[-1][0, 0]
    sg, sd, sm, sv = (_small_unpack(t, shapes) for t in (sg, sd, sm, sv))

    nmod = 6 * D
    dmod_all = parts_all[:, :nmod // LANES, :].reshape(NDEV, nmod)
    c_all_t = jnp.transpose(c_all.reshape(NDEV, D))
    ada_out = adamw_ada(c_all_t, dmod_all, w_ada[0], m_w_ada[0], v_w_ada[0], my_dev)
    _, got3_c = wait_chip_c(ada_out[0])
    u_in = adamw_sharded("adamw_6", grads_c[0], got_c[0], got3_c[0], w_in[0], m_w_in[0], v_w_in[0], place)

    def pick(k):
        return [ada_out[k][None], sg_sd[k][0], u_in[k][None]] + [t for t in sg_sd[k][1:9]] + \
               [u_glu[k][0][None], u_glu[k][1][None], u_pool[k][None], sg_sd[k][9], u_po[k][None],
                u_out[k][None], sg_sd[k][10], sg_sd[k][11], u_ff1[k][None], u_ff2[k][None],
                sg_sd[k][12], sg_sd[k][13]]

    sg_sd = [sg, sd, sm, sv]
    return (loss, grad_x[None], *pick(0), *pick(1), *pick(2), *pick(3))
```

```python
import functools
import math

import jax
import jax.numpy as jnp
from jax import lax
from jax.experimental import pallas as pl
from jax.experimental.pallas import tpu as pltpu
from jax.experimental.pallas import tpu_sc as plsc

F32 = jnp.float32
BF16 = jnp.bfloat16
MESH = pl.DeviceIdType.MESH
NDEV = 8
NCHIP = 4

SSM_GROUP = 16
SSM_STATE = 64
GROUPS_PER_BLOCK = 8
POOL_WINDOWS = (2, 4, 8, 16)
LN_EPS = 1e-5
ALPHA = 2.0 ** 0.25
ADAM_LR, ADAM_B1, ADAM_B2, ADAM_EPS, ADAM_WD, ADAM_STEP = 0.001, 0.9, 0.999, 1e-08, 0.01, 10
SUBLANES = 8
LANES = 128
VMEM_LIMIT = 56 * 1024 * 1024


def _params(sem=None, vmem=VMEM_LIMIT):
    return pltpu.CompilerParams(dimension_semantics=sem, vmem_limit_bytes=vmem)


def _tile(n, pref):
    if n <= pref:
        return n
    t = 1 << (pref.bit_length() - 1)
    while n % t:
        t //= 2
    return t


def _cast_epi(vals, ex, outs):
    c = vals[0].shape[1]
    for s, v in enumerate(vals):
        outs[0][:, s * c:(s + 1) * c] = v.astype(outs[0].dtype)


ANY = pl.BlockSpec(memory_space=pl.ANY)


def _with_after(body, n_in, after):
    if not after:
        return body
    n_af = len(after)

    def wrapped(*refs):
        return body(*refs[:n_in], *refs[n_in + n_af:])
    return wrapped


def _mm(name, kind, a, b, grid, a_spec, b_spec, outs, acc_shape, nsub=1, c=None,
        pro=None, epi=None, extras=(), stacked_out=False, after=()):
    nk = grid[2]
    n_ex, n_out = len(extras), len(outs)

    def finish(vals, ex, out_refs):
        if epi is not None:
            epi(vals, ex, out_refs)
        elif stacked_out:
            for s, v in enumerate(vals):
                out_refs[0][s] = v.astype(out_refs[0].dtype)
        else:
            _cast_epi(vals, ex, out_refs)

    def body(*refs):
        mm_step(refs[0], refs[1], refs[2:2 + n_ex], refs[2 + n_ex:2 + n_ex + n_out], refs[-1])

    def mm_step(a_ref, b_ref, ex, out_refs, acc):
        k = pl.program_id(2)
        av = a_ref[...]
        if pro is not None:
            av = pro(av)
        if kind == "nn":
            prods = [jnp.dot(av, b_ref[s], preferred_element_type=F32) for s in range(nsub)]
        elif kind == "nt":
            t = None
            for s in range(nsub):
                d = lax.dot_general(av[:, s * c:(s + 1) * c], b_ref[s], (((1,), (1,)), ((), ())),
                                    preferred_element_type=F32)
                t = d if t is None else t + d
            prods = [t]
        else:
            t = lax.dot_general(av, b_ref[...], (((0,), (0,)), ((), ())), preferred_element_type=F32)
            prods = [t[:, s * c:(s + 1) * c] for s in range(nsub)] if stacked_out else [t]
        if nk == 1:
            finish(prods, ex, out_refs)
            return
        w = prods[0].shape[1]

        @pl.when(k == 0)
        def _():
            for s, p in enumerate(prods):
                acc[:, s * w:(s + 1) * w] = p

        @pl.when(jnp.logical_and(k > 0, k < nk - 1))
        def _():
            for s, p in enumerate(prods):
                acc[:, s * w:(s + 1) * w] += p

        @pl.when(k == nk - 1)
        def _():
            finish([acc[:, s * w:(s + 1) * w] + p for s, p in enumerate(prods)], ex, out_refs)

    return pl.pallas_call(
        _with_after(body, 2 + n_ex, after), name=name, grid=grid,
        in_specs=[a_spec, b_spec] + [e[1] for e in extras] + [ANY] * len(after),
        out_specs=[o[1] for o in outs],
        out_shape=[o[0] for o in outs],
        scratch_shapes=[pltpu.VMEM(acc_shape, F32)] if nk > 1 else [],
        compiler_params=_params(("parallel", "parallel", "arbitrary")),
    )(a, b, *[e[0] for e in extras], *after)


def _sds(shape, dtype):
    return jax.ShapeDtypeStruct(shape, dtype)


def mm_nn(name, a, b3, out_dtype, nsub, tm=1024, tk=2048, tn=None, pro=None, epi=None,
          extras=(), extra_outs=(), a_col0=0, after=()):
    M = a.shape[0]
    nb, K, cdim = b3.shape
    tm, tk = _tile(M, tm), _tile(K, tk)
    if nb == 1:
        tn = _tile(cdim, tn or 1024)
        nsub, c, nj = 1, tn, cdim // tn
        b_spec = pl.BlockSpec((1, tk, tn), lambda i, j, k: (0, k, j))
        N = cdim
    else:
        c, nj, tn = cdim, nb // nsub, nsub * cdim
        b_spec = pl.BlockSpec((nsub, tk, cdim), lambda i, j, k: (j, k, 0))
        N = nb * cdim
    kb0 = a_col0 // tk
    a_spec = pl.BlockSpec((tm, tk), lambda i, j, k: (i, kb0 + k))
    grid = (M // tm, nj, K // tk)
    o_spec = pl.BlockSpec((tm, tn), lambda i, j, k: (i, j))
    outs = [(_sds((M, N), out_dtype), o_spec)] + [(_sds((M, N), d), o_spec) for d in extra_outs]
    return _mm(name, "nn", a, b3, grid, a_spec, b_spec, outs, (tm, tn), nsub, c, pro, epi, extras,
               after=after)


def mm_nt(name, a, b3, out_dtype, nsub, tm=1024, tn=1024, epi=None, extras=(), extra_outs=(),
          after=()):
    M = a.shape[0]
    nb, N, cdim = b3.shape
    tm, tn = _tile(M, tm), _tile(N, tn)
    if nb == 1:
        tk = _tile(cdim, 2048)
        nsub, c, nk = 1, tk, cdim // tk
        b_spec = pl.BlockSpec((1, tn, tk), lambda i, j, k: (0, j, k))
    else:
        c, nk, tk = cdim, nb // nsub, nsub * cdim
        b_spec = pl.BlockSpec((nsub, tn, cdim), lambda i, j, k: (k, j, 0))
    a_spec = pl.BlockSpec((tm, tk), lambda i, j, k: (i, k))
    grid = (M // tm, N // tn, nk)
    o_spec = pl.BlockSpec((tm, tn), lambda i, j, k: (i, j))
    outs = [(_sds((M, N), out_dtype), o_spec)] + [(_sds((M, N), d), o_spec) for d in extra_outs]
    return _mm(name, "nt", a, b3, grid, a_spec, b_spec, outs, (tm, tn), nsub, c, None, epi, extras,
               after=after)


def mm_tn(name, a, b, out_dtype, nb, nsub, tma=1024, tk=2048, pro=None, a_col0=0, a_cols=None,
          after=()):
    S = a.shape[0]
    Ka = a_cols or a.shape[1]
    N = b.shape[1]
    tk = _tile(S, tk)
    if nsub == 0:
        tma, tn = _tile(Ka, tma), _tile(N, 1024)
        grid = (Ka // tma, N // tn, S // tk)
        ab0 = a_col0 // tma
        res = _mm(name, "tn", a, b, grid, pl.BlockSpec((tk, tma), lambda i, j, k: (k, ab0 + i)),
                  pl.BlockSpec((tk, tn), lambda i, j, k: (k, j)),
                  [(_sds((Ka, N), out_dtype), pl.BlockSpec((tma, tn), lambda i, j, k: (i, j)))],
                  (tma, tn), 1, tn, pro, None, (), after=after)[0]
        return res.reshape(nb, Ka // nb, N)
    else:
        c = N // nb
        tma = _tile(Ka, tma)
        grid = (Ka // tma, nb // nsub, S // tk)
        o_spec = pl.BlockSpec((nsub, tma, c), lambda i, j, k: (j, i, 0))
        out = _sds((nb, Ka, c), out_dtype)
        nsub_k = nsub
        tn = nsub * c
        b_spec = pl.BlockSpec((tk, tn), lambda i, j, k: (k, j))
    ab0 = a_col0 // tma
    a_spec = pl.BlockSpec((tk, tma), lambda i, j, k: (k, ab0 + i))
    return _mm(name, "tn", a, b, grid, a_spec, b_spec, [(out, o_spec)], (tma, tn), nsub_k, c,
               pro, None, (), stacked_out=True, after=after)[0]


def _rowwise(name, fn, S, ts, tiled, bcast, tiled_out, acc_out, after=()):
    nt, nb, no, na = len(tiled), len(bcast), len(tiled_out), len(acc_out)

    def body(*refs):
        tin = [r[...] for r in refs[:nt]]
        bin_ = [r[...] for r in refs[nt:nt + nb]]
        o_refs = refs[nt + nb:nt + nb + no]
        a_refs = refs[nt + nb + no:]
        touts, aouts = fn(tin, bin_)
        for r, v in zip(o_refs, touts):
            r[...] = v.astype(r.dtype)
        i = pl.program_id(0)

        @pl.when(i == 0)
        def _():
            for r, v in zip(a_refs, aouts):
                r[...] = v

        @pl.when(i > 0)
        def _():
            for r, v in zip(a_refs, aouts):
                r[...] += v

    in_specs = [pl.BlockSpec((ts, w), functools.partial(lambda i, cb: (i, cb), cb=cb))
                for (_, w, cb) in tiled]
    in_specs += [pl.BlockSpec(b.shape, lambda i: (0, 0)) for b in bcast]
    out_specs = [pl.BlockSpec((ts, w), lambda i: (i, 0)) for (w, _) in tiled_out]
    out_specs += [pl.BlockSpec((1, w), lambda i: (0, 0)) for w in acc_out]
    out_shape = [_sds((S, w), d) for (w, d) in tiled_out] + [_sds((1, w), F32) for w in acc_out]
    return pl.pallas_call(
        _with_after(body, nt + nb, after), name=name, grid=(S // ts,),
        in_specs=in_specs + [ANY] * len(after), out_specs=out_specs,
        out_shape=out_shape, compiler_params=_params(("arbitrary",)),
    )(*[t[0] for t in tiled], *bcast, *after)


def _ln_stats(v):
    mu = jnp.mean(v, axis=-1, keepdims=True)
    vc = v - mu
    var = jnp.mean(vc * vc, axis=-1, keepdims=True)
    rstd = lax.rsqrt(var + LN_EPS)
    return vc * rstd, rstd


def _ln_bwd(dxhat, xhat, rstd):
    return rstd * (dxhat - jnp.mean(dxhat, axis=-1, keepdims=True)
                   - xhat * jnp.mean(dxhat * xhat, axis=-1, keepdims=True))


def _colsum(v):
    return jnp.sum(v, axis=0, keepdims=True)


def _sigmoid(v):
    return 1.0 / (1.0 + jnp.exp(-v))


_GELU_C = math.sqrt(2.0 / math.pi)


def _gelu(v):
    return 0.5 * v * (1.0 + jnp.tanh(_GELU_C * (v + 0.044715 * v * v * v)))


def _gelu_grad(v):
    t = jnp.tanh(_GELU_C * (v + 0.044715 * v * v * v))
    return 0.5 * (1.0 + t) + 0.5 * v * (1.0 - t * t) * _GELU_C * (1.0 + 3 * 0.044715 * v * v)


def _disc(lr, li, ldt):
    dt = jnp.exp(ldt)
    mag = jnp.exp(lr * dt)
    ang = li * dt
    ab_re = mag * jnp.cos(ang)
    ab_im = mag * jnp.sin(ang)
    num_re = ab_re - 1.0
    num_im = ab_im
    den = lr * lr + li * li
    f_re = (num_re * lr + num_im * li) / den
    f_im = (num_im * lr - num_re * li) / den
    return ab_re, ab_im, f_re, f_im


def _cmul(ar, ai, br, bi):
    return ar * br - ai * bi, ar * bi + ai * br


def s5_disc(lam_re, lam_im, log_dt):
    G, P = lam_re.shape

    def body(lr_ref, li_ref, ldt_ref, f_ref, k_ref):
        ab_re, ab_im, f_re, f_im = _disc(lr_ref[...], li_ref[...], ldt_ref[...])
        f_ref[0] = f_re
        f_ref[1] = f_im
        pr, pi = [ab_re], [ab_im]
        for _ in range(SUBLANES - 1):
            nr, ni = _cmul(pr[-1], pi[-1], ab_re, ab_im)
            pr.append(nr)
            pi.append(ni)
        zero = jnp.zeros_like(ab_re)
        for n, sh in enumerate((1, 2, 4)):
            for r in range(SUBLANES):
                k_ref[2 * n, r] = pr[sh - 1] if r >= sh else zero
                k_ref[2 * n + 1, r] = pi[sh - 1] if r >= sh else zero
                k_ref[8 + 2 * n, r] = pr[sh - 1] if r + sh < SUBLANES else zero
                k_ref[8 + 2 * n + 1, r] = -pi[sh - 1] if r + sh < SUBLANES else zero
        for r in range(SUBLANES):
            k_ref[6, r] = pr[r]
            k_ref[7, r] = pi[r]
            k_ref[14, r] = pr[SUBLANES - 1 - r]
            k_ref[15, r] = -pi[SUBLANES - 1 - r]

    vm = pl.BlockSpec(memory_space=pltpu.VMEM)
    return pl.pallas_call(
        body, name="s5_disc", in_specs=[vm, vm, vm], out_specs=[vm, vm],
        out_shape=[_sds((2, G, P), F32), _sds((16, SUBLANES, G, P), F32)],
    )(lam_re, lam_im, log_dt)


def s5_disc_bwd(lam_re, lam_im, log_dt, d_ab, d_f):
    G, P = lam_re.shape

    def body(lr_ref, li_ref, ldt_ref, dab_ref, df_ref, glr_ref, gli_ref, gdt_ref):
        _, vjp = jax.vjp(_disc, lr_ref[...], li_ref[...], ldt_ref[...])
        glr, gli, gdt = vjp((dab_ref[0], dab_ref[1], df_ref[0], df_ref[1]))
        glr_ref[...] = glr
        gli_ref[...] = gli
        gdt_ref[...] = gdt

    vm = pl.BlockSpec(memory_space=pltpu.VMEM)
    return pl.pallas_call(
        body, name="s5_disc_bwd", in_specs=[vm] * 5, out_specs=[vm] * 3,
        out_shape=[_sds((G, P), F32), _sds((G, P), F32), _sds((G, 1), F32)],
    )(lam_re, lam_im, log_dt, d_ab, d_f)


def _group_mask(cw, nst):
    row = lax.broadcasted_iota(jnp.int32, (cw, 2 * nst), 0) // SSM_GROUP
    col = (lax.broadcasted_iota(jnp.int32, (cw, 2 * nst), 1) % nst) // SSM_STATE
    return row == col


def _spread(t, mask):
    reps = mask.shape[0] // t.shape[0]
    return jnp.where(mask, jnp.tile(t, (reps, 1)), 0.0).astype(BF16)


def _gather_groups(t, mask):
    t = jnp.where(mask, t, 0.0)
    out = t[0:SSM_GROUP]
    for g in range(1, t.shape[0] // SSM_GROUP):
        out = out + t[g * SSM_GROUP:(g + 1) * SSM_GROUP]
    return out


def _s5_operands(f_ref, br_ref, bi_ref, cr_ref, ci_ref, mask):
    fr, fi = f_ref[0], f_ref[1]
    br, bi = br_ref[...], bi_ref[...]
    bm = _spread(jnp.concatenate([fr * br - fi * bi, fr * bi + fi * br], axis=1), mask)
    cm = _spread(jnp.concatenate([cr_ref[...], -ci_ref[...]], axis=1), mask)
    return bm, cm


def _scan_fwd(xs, k_ref, nst):
    ntile = xs.shape[0] // SUBLANES

    def step(t, carry):
        cr, ci = carry
        r0 = pl.multiple_of(t * SUBLANES, SUBLANES)
        xr = xs[pl.ds(r0, SUBLANES), 0:nst]
        xi = xs[pl.ds(r0, SUBLANES), nst:2 * nst]
        for n, sh in enumerate((1, 2, 4)):
            sr = pltpu.roll(xr, sh, 0)
            si = pltpu.roll(xi, sh, 0)
            mr, mi = k_ref[2 * n], k_ref[2 * n + 1]
            xr, xi = xr + mr * sr - mi * si, xi + mr * si + mi * sr
        pr, pi = k_ref[6], k_ref[7]
        xr, xi = xr + pr * cr - pi * ci, xi + pr * ci + pi * cr
        xs[pl.ds(r0, SUBLANES), 0:nst] = xr
        xs[pl.ds(r0, SUBLANES), nst:2 * nst] = xi
        return (jnp.broadcast_to(xr[SUBLANES - 1:SUBLANES, :], xr.shape),
                jnp.broadcast_to(xi[SUBLANES - 1:SUBLANES, :], xi.shape))

    zero = jnp.zeros((SUBLANES, nst), F32)
    lax.fori_loop(0, ntile, step, (zero, zero))


def _scan_bwd(g, xs, k_ref, nst):
    ntile = g.shape[0] // SUBLANES
    row = lax.broadcasted_iota(jnp.int32, (SUBLANES, nst), 0)

    def step(tt, carry):
        cr, ci, ar, ai = carry
        t = ntile - 1 - tt
        r0 = pl.multiple_of(t * SUBLANES, SUBLANES)
        gr = g[pl.ds(r0, SUBLANES), 0:nst]
        gi = g[pl.ds(r0, SUBLANES), nst:2 * nst]
        for n, sh in enumerate((1, 2, 4)):
            sr = pltpu.roll(gr, SUBLANES - sh, 0)
            si = pltpu.roll(gi, SUBLANES - sh, 0)
            mr, mi = k_ref[8 + 2 * n], k_ref[8 + 2 * n + 1]
            gr, gi = gr + mr * sr - mi * si, gi + mr * si + mi * sr
        qr, qi = k_ref[14], k_ref[15]
        gr, gi = gr + qr * cr - qi * ci, gi + qr * ci + qi * cr
        g[pl.ds(r0, SUBLANES), 0:nst] = gr
        g[pl.ds(r0, SUBLANES), nst:2 * nst] = gi
        p0 = pl.multiple_of(jnp.maximum(t - 1, 0) * SUBLANES, SUBLANES)
        live = (t > 0).astype(F32)
        xr = xs[pl.ds(r0, SUBLANES), 0:nst]
        xi = xs[pl.ds(r0, SUBLANES), nst:2 * nst]
        pr = xs[pl.ds(p0, SUBLANES), 0:nst][SUBLANES - 1:SUBLANES, :] * live
        pi = xs[pl.ds(p0, SUBLANES), nst:2 * nst][SUBLANES - 1:SUBLANES, :] * live
        xmr = jnp.where(row == 0, jnp.broadcast_to(pr, xr.shape), pltpu.roll(xr, 1, 0))
        xmi = jnp.where(row == 0, jnp.broadcast_to(pi, xi.shape), pltpu.roll(xi, 1, 0))
        ar = ar + gr * xmr + gi * xmi
        ai = ai + gi * xmr - gr * xmi
        return (jnp.broadcast_to(gr[0:1, :], gr.shape), jnp.broadcast_to(gi[0:1, :], gi.shape),
                ar, ai)

    zero = jnp.zeros((SUBLANES, nst), F32)
    _, _, ar, ai = lax.fori_loop(0, ntile, step, (zero, zero, zero, zero))
    return _colsum(ar), _colsum(ai)


def _s5_param_specs(cw, nst):
    hp = pl.BlockSpec((SSM_GROUP, nst), lambda b: (0, b))
    return [pl.BlockSpec((2, 1, nst), lambda b: (0, 0, b)), hp, hp, hp, hp,
            pl.BlockSpec((1, cw), lambda b: (0, b)),
            pl.BlockSpec((16, SUBLANES, nst), lambda b: (0, 0, b))]


def s5_fwd(proj, params, nb):
    S = proj.shape[0]
    nst = params[1].shape[1] // nb
    cw = nst // SSM_STATE * SSM_GROUP

    def body(u_ref, f_ref, br_ref, bi_ref, cr_ref, ci_ref, d_ref, k_ref, z_ref, xsb_ref, xs):
        bm, cm = _s5_operands(f_ref, br_ref, bi_ref, cr_ref, ci_ref, _group_mask(cw, nst))
        u = u_ref[...]
        xs[...] = jnp.dot(u.astype(BF16), bm, preferred_element_type=F32)
        _scan_fwd(xs, k_ref, nst)
        xsb = xs[...].astype(BF16)
        xsb_ref[...] = xsb
        y = lax.dot_general(xsb, cm, (((1,), (1,)), ((), ())), preferred_element_type=F32)
        z_ref[...] = _gelu(y + d_ref[...] * u).astype(BF16)

    return pl.pallas_call(
        body, name="s5_fwd", grid=(nb,),
        in_specs=[pl.BlockSpec((S, cw), lambda b: (0, b))] + _s5_param_specs(cw, nst),
        out_specs=[pl.BlockSpec((S, cw), lambda b: (0, b)), pl.BlockSpec((S, 2 * nst), lambda b: (0, b))],
        out_shape=[_sds((S, nb * cw), BF16), _sds((S, nb * 2 * nst), BF16)],
        scratch_shapes=[pltpu.VMEM((S, 2 * nst), F32)],
        compiler_params=_params(("arbitrary",)),
    )(proj, *params)


def s5_bwd(proj, xsb_all, dz, params, nb, after=()):
    S = proj.shape[0]
    nst = params[1].shape[1] // nb
    cw = nst // SSM_STATE * SSM_GROUP

    def body(u_ref, xsb_ref, dz_ref, f_ref, br_ref, bi_ref, cr_ref, ci_ref, d_ref, k_ref,
             du_ref, gbr_ref, gbi_ref, gcr_ref, gci_ref, gf_ref, gd_ref, ga_ref, xs, g):
        mask = _group_mask(cw, nst)
        bm, cm = _s5_operands(f_ref, br_ref, bi_ref, cr_ref, ci_ref, mask)
        u = u_ref[...]
        ub = u.astype(BF16)
        d = d_ref[...]
        xsb = xsb_ref[...]
        xs[...] = xsb.astype(F32)
        y = lax.dot_general(xsb, cm, (((1,), (1,)), ((), ())), preferred_element_type=F32) + d * u
        dy = dz_ref[...].astype(F32) * _gelu_grad(y)
        gd_ref[...] = _colsum(dy * u)
        dyb = dy.astype(BF16)
        gc = _gather_groups(lax.dot_general(dyb, xsb, (((0,), (0,)), ((), ())),
                                            preferred_element_type=F32), mask)
        gcr_ref[...] = gc[:, :nst]
        gci_ref[...] = -gc[:, nst:]
        g[...] = jnp.dot(dyb, cm, preferred_element_type=F32)
        ar, ai = _scan_bwd(g, xs, k_ref, nst)
        ga_ref[0, 0:1, :] = ar
        ga_ref[0, 1:2, :] = ai
        gb = g[...].astype(BF16)
        du = lax.dot_general(gb, bm, (((1,), (1,)), ((), ())), preferred_element_type=F32) + d * dy
        du_ref[...] = du.astype(BF16)
        gbb = _gather_groups(lax.dot_general(ub, gb, (((0,), (0,)), ((), ())),
                                             preferred_element_type=F32), mask)
        dr, di = gbb[:, :nst], gbb[:, nst:]
        fr, fi = f_ref[0], f_ref[1]
        br, bi = br_ref[...], bi_ref[...]
        gbr_ref[...] = fr * dr + fi * di
        gbi_ref[...] = fr * di - fi * dr
        gf_ref[0] = _colsum(dr * br + di * bi)
        gf_ref[1] = _colsum(di * br - dr * bi)

    hp = pl.BlockSpec((SSM_GROUP, nst), lambda b: (0, b))
    hp_sds = _sds((SSM_GROUP, nb * nst), F32)
    return pl.pallas_call(
        _with_after(body, 10, after), name="s5_bwd", grid=(nb,),
        in_specs=[pl.BlockSpec((S, cw), lambda b: (0, b)),
                  pl.BlockSpec((S, 2 * nst), lambda b: (0, b)),
                  pl.BlockSpec((S, cw), lambda b: (0, b))] + _s5_param_specs(cw, nst)
        + [ANY] * len(after),
        out_specs=[pl.BlockSpec((S, cw), lambda b: (0, b)), hp, hp, hp, hp,
                   pl.BlockSpec((2, 1, nst), lambda b: (0, 0, b)),
                   pl.BlockSpec((1, cw), lambda b: (0, b)),
                   pl.BlockSpec((1, 2, nst), lambda b: (b, 0, 0))],
        out_shape=[_sds((S, nb * cw), BF16), hp_sds, hp_sds, hp_sds, hp_sds,
                   _sds((2, 1, nb * nst), F32), _sds((1, nb * cw), F32), _sds((nb, 2, nst), F32)],
        scratch_shapes=[pltpu.VMEM((S, 2 * nst), F32), pltpu.VMEM((S, 2 * nst), F32)],
        compiler_params=_params(("arbitrary",)),
    )(proj, xsb_all, dz, *params, *after)


def _shift_rows(v, k, row, down):
    n = v.shape[0]
    if down:
        return jnp.where(row >= k, pltpu.roll(v, k, 0), 0.0)
    return jnp.where(row < n - k, pltpu.roll(v, n - k, 0), 0.0)


def _window(v, gi, row, down):
    sums = []
    s = v
    for k in (1, 2, 4, 8):
        s = s + _shift_rows(s, k, row, down)
        sums.append(s)
    out = sums[3]
    for n in (2, 1, 0):
        out = jnp.where(gi == n, sums[n], out)
    return out


def pool_fwd(proj, col0, width, gw):
    S = proj.shape[0]
    cb0 = col0 // gw

    def body(u_ref, o_ref):
        gi = pl.program_id(0)
        u = u_ref[...]
        row = lax.broadcasted_iota(jnp.int32, u.shape, 0)
        w = jnp.left_shift(2, gi)
        count = jnp.minimum(row + 1, w).astype(F32)
        o_ref[...] = (_window(u, gi, row, True) / count - u).astype(BF16)

    return pl.pallas_call(
        body, name="pool_fwd", grid=(len(POOL_WINDOWS),),
        in_specs=[pl.BlockSpec((S, gw), lambda g: (0, cb0 + g))],
        out_specs=pl.BlockSpec((S, gw), lambda g: (0, g)),
        out_shape=_sds((S, width), BF16), compiler_params=_params(("arbitrary",)),
    )(proj)


def pool_bwd(dpooled, gw):
    S, width = dpooled.shape

    def body(d_ref, o_ref):
        gi = pl.program_id(0)
        d = d_ref[...]
        row = lax.broadcasted_iota(jnp.int32, d.shape, 0)
        w = jnp.left_shift(2, gi)
        count = jnp.minimum(row + 1, w).astype(F32)
        o_ref[...] = (_window(d / count, gi, row, False) - d).astype(BF16)

    return pl.pallas_call(
        body, name="pool_bwd", grid=(len(POOL_WINDOWS),),
        in_specs=[pl.BlockSpec((S, gw), lambda g: (0, g))],
        out_specs=pl.BlockSpec((S, gw), lambda g: (0, g)),
        out_shape=_sds((S, width), BF16), compiler_params=_params(("arbitrary",)),
    )(dpooled)


def _place():
    x, y, c = lax.axis_index("x"), lax.axis_index("y"), lax.axis_index("c")
    chips = [(1 - x, y), (x, 1 - y), (1 - x, 1 - y)]
    return x, y, c, chips


HBM = pl.BlockSpec(memory_space=pltpu.HBM)


def _routed_gather_body(n):
    def body(*refs):
        ins, outs = refs[:n], refs[n:2 * n]
        send_sems, recv_sems, local_sems = refs[2 * n:]
        x, y, c, (xn, yn, dg) = _place()
        me, sibling = (x, y, c), (x, y, 1 - c)
        barrier = pltpu.get_barrier_semaphore()
        for peer in (sibling, (*xn, c), (*yn, c)):
            pl.semaphore_signal(barrier, inc=1, device_id=peer, device_id_type=MESH)
        pl.semaphore_wait(barrier, 3)

        def piece(i, p, h):
            rows = ins[i].shape[0] // 2
            return outs[i].at[4 * p[0] + 2 * p[1] + p[2], pl.ds(h * rows, rows)]

        def copy(i, k, src, dst, to):
            return pltpu.make_async_remote_copy(src_ref=src, dst_ref=dst, send_sem=send_sems.at[i, k],
                                                recv_sem=recv_sems.at[i, k], device_id=to,
                                                device_id_type=MESH)

        started = []

        def go(cp):
            cp.start()
            started.append(cp)

        for i in range(n):
            rows = ins[i].shape[0] // 2
            for h in range(2):
                own = ins[i].at[pl.ds(h * rows, rows)]
                go(copy(i, 1 + h, own, piece(i, me, h), (*xn, c)))
                go(copy(i, 3 + h, own, piece(i, me, h), (*yn, c)))
        for i in range(n):
            go(copy(i, 0, ins[i], outs[i].at[4 * x + 2 * y + c], sibling))
        mine = [pltpu.make_async_copy(ins[i], outs[i].at[4 * x + 2 * y + c], local_sems.at[i])
                for i in range(n)]
        for cp in mine:
            cp.start()
        for i in range(n):
            for k, chip, h, onward, ksib in ((1, xn, 0, (5, yn), 7), (4, yn, 1, (6, xn), 10),
                                            (2, xn, 1, None, 8), (3, yn, 0, None, 9),
                                            (5, dg, 0, None, 11), (6, dg, 1, None, 12)):
                got = piece(i, (*chip, c), h)
                copy(i, k, got, got, me).wait_recv()
                if onward is not None:
                    go(copy(i, onward[0], got, got, (*onward[1], c)))
                go(copy(i, ksib, got, got, sibling))
        for i in range(n):
            block = outs[i].at[4 * x + 2 * y + 1 - c]
            copy(i, 0, block, block, me).wait_recv()
            for ksib, chip, h in ((7, xn, 0), (10, yn, 1), (8, xn, 1), (9, yn, 0), (11, dg, 0), (12, dg, 1)):
                got = piece(i, (*chip, 1 - c), h)
                copy(i, ksib, got, got, me).wait_recv()
        for cp in started:
            cp.wait_send()
        for cp in mine:
            cp.wait()

    return body


def _on_sequencer(name, body, arrays, out_sds, sems, collective_id):
    ins = [jax.new_ref(a, memory_space=pltpu.MemorySpace.HBM) for a in arrays]
    outs = [jax.empty_ref(s, memory_space=pltpu.MemorySpace.HBM) for s in out_sds]

    @pl.kernel(mesh=plsc.ScalarSubcoreMesh(axis_name="sequencer", num_cores=1), name=name,
               scratch_types=tuple(sems),
               compiler_params=pltpu.CompilerParams(collective_id=collective_id))
    def launch(*sem_refs):
        body(*ins, *outs, *sem_refs)

    launch()
    return [o[...] for o in outs]


def seq_all_gather(name, shards, collective_id):
    n = len(shards)
    return _on_sequencer(
        name, _routed_gather_body(n), shards, [_sds((NDEV,) + s.shape, s.dtype) for s in shards],
        [pltpu.SemaphoreType.DMA((n, 13)), pltpu.SemaphoreType.DMA((n, 13)),
         pltpu.SemaphoreType.DMA((n,))], collective_id)


def pair_exchange(name, grads, collective_id):
    def plan(srcs, lands):
        x, y, c, _ = _place()
        return ([(i, q, srcs[i].at[2 * q + 1 - c], lands[i].at[q], (x, y, 1 - c))
                 for i in range(len(srcs)) for q in range(NCHIP)], [(x, y, 1 - c)])

    return _split_exchange(name, grads, [_sds((NCHIP,) + g.shape[1:], g.dtype) for g in grads],
                           plan, NCHIP, collective_id)


SEM = pl.BlockSpec(memory_space=pltpu.SEMAPHORE)


def _split_exchange(name, srcs, land_sds, plan, ncopy, collective_id):
    n = len(srcs)
    nsem = n * ncopy
    effect = pltpu.SideEffectType.DATAFLOW_SIDE_EFFECTING

    def descriptors(src_refs, land_refs, send_sems, recv_sems):
        copies, peers = plan(src_refs, land_refs)
        return [pltpu.make_async_remote_copy(src_ref=s, dst_ref=d, send_sem=send_sems[i * ncopy + k],
                                             recv_sem=recv_sems[i * ncopy + k], device_id=to,
                                             device_id_type=MESH) for (i, k, s, d, to) in copies], peers

    def start_body(*refs):
        src_refs, land_refs = refs[:n], refs[n:2 * n]
        send_sems, recv_sems = refs[2 * n:2 * n + nsem], refs[2 * n + nsem:2 * n + 2 * nsem]
        token = refs[-1]
        cps, peers = descriptors(src_refs, land_refs, send_sems, recv_sems)
        barrier = pltpu.get_barrier_semaphore()
        for peer in peers:
            pl.semaphore_signal(barrier, inc=1, device_id=peer, device_id_type=MESH)
        pl.semaphore_wait(barrier, len(peers))
        for cp in cps:
            cp.start()
        token[...] = jnp.zeros_like(token)

    lands = [pltpu.with_memory_space_constraint(lax.empty(s.shape, s.dtype), pltpu.HBM) for s in land_sds]
    srcs = [pltpu.with_memory_space_constraint(s, pltpu.HBM) for s in srcs]
    res = pl.pallas_call(
        start_body, name=name + "_start",
        out_shape=(pltpu.SemaphoreType.DMA(()),) * (2 * nsem)
        + tuple(pltpu.HBM(s.shape, s.dtype) for s in srcs)
        + tuple(pltpu.HBM(s.shape, s.dtype) for s in land_sds) + (_sds((SUBLANES, LANES), F32),),
        in_specs=[HBM] * (2 * n),
        out_specs=(SEM,) * (2 * nsem) + (HBM,) * (2 * n) + (pl.BlockSpec(memory_space=pltpu.VMEM),),
        input_output_aliases={i: 2 * nsem + i for i in range(2 * n)},
        compiler_params=pltpu.CompilerParams(has_side_effects=effect, collective_id=collective_id),
    )(*srcs, *lands)
    sems = res[:2 * nsem]
    thru = res[2 * nsem:2 * nsem + 2 * n]
    token = res[-1]

    def wait(after):
        def wait_body(*refs):
            src_refs, land_refs = refs[:n], refs[n:2 * n]
            cps, _ = descriptors(src_refs, land_refs, refs[2 * n:2 * n + nsem],
                                 refs[2 * n + nsem:2 * n + 2 * nsem])
            for cp in cps:
                cp.wait_send()
            for cp in cps:
                cp.wait_recv()

        out = pl.pallas_call(
            wait_body, name=name + "_wait",
            out_shape=tuple(pltpu.HBM(s.shape, s.dtype) for s in srcs)
            + tuple(pltpu.HBM(s.shape, s.dtype) for s in land_sds),
            in_specs=[HBM] * (2 * n) + [SEM] * (2 * nsem) + [pl.BlockSpec(memory_space=pl.ANY)],
            out_specs=(HBM,) * (2 * n),
            input_output_aliases={i: i for i in range(2 * n)},
            compiler_params=pltpu.CompilerParams(has_side_effects=effect),
        )(*thru, *sems, after)
        return list(out[:n]), list(out[n:])

    return token, wait


def pair_sum(name, grad, got, place):
    shp = grad.shape[1:]
    r, cdim = shp[-2], shp[-1]
    lead = int(math.prod(shp[:-2])) if len(shp) > 2 else 1
    g5 = grad.reshape(NCHIP, 2, lead * r, cdim)
    t4 = got.reshape(NCHIP, lead * r, cdim)
    R = lead * r
    tr = _tile(R, max(8, (1 << 20) // cdim))

    def body(p_ref, g_ref, t_ref, o_ref):
        o_ref[...] = (g_ref[0].astype(F32) + t_ref[...].astype(F32)).astype(o_ref.dtype)

    out = pl.pallas_call(
        body, name=name,
        grid_spec=pltpu.PrefetchScalarGridSpec(
            num_scalar_prefetch=1, grid=(NCHIP - 1, R // tr),
            in_specs=[pl.BlockSpec((1, 1, tr, cdim), lambda j, i, p: (p[1] ^ (j + 1), p[0], i, 0)),
                      pl.BlockSpec((1, tr, cdim), lambda j, i, p: (p[1] ^ (j + 1), i, 0))],
            out_specs=pl.BlockSpec((1, tr, cdim), lambda j, i, p: (p[1] ^ (j + 1), i, 0))),
        out_shape=_sds((NCHIP, R, cdim), grad.dtype),
        compiler_params=_params(("parallel", "parallel")),
    )(place, g5, t4)
    return out


def chip_exchange(name, parts, collective_id):
    def plan(srcs, lands):
        x, y, c, chips = _place()
        return ([(i, j, srcs[i].at[2 * chip[0] + chip[1]], lands[i].at[j], (*chip, c))
                 for i in range(len(srcs)) for j, chip in enumerate(chips)],
                [(*chip, c) for chip in chips])

    return _split_exchange(name, parts, [_sds((3,) + p.shape[1:], p.dtype) for p in parts],
                           plan, 3, collective_id)


def ada_fwd(c_row, w_ada, b_ada):
    D, cols = w_ada.shape

    def body(c_ref, w_ref, b_ref, mod_ref, call_ref, act8, part, s1, r1, s2, r2):
        x, y, c, _ = _place()
        me = 4 * x + 2 * y + c
        call_ref[me] = c_ref[...]
        cps = []
        for k in range(1, NDEV):
            to = (x ^ (k >> 2), y ^ ((k >> 1) & 1), c ^ (k & 1))
            cps.append(pltpu.make_async_remote_copy(
                src_ref=c_ref, dst_ref=call_ref.at[me], send_sem=s1.at[k - 1],
                recv_sem=r1.at[k - 1], device_id=to, device_id_type=MESH))
            cps[-1].start()
        for cp in cps:
            cp.wait()
        for b in range(NDEV):
            act8[b:b + 1, :] = call_ref[b]
        cv = act8[...]
        act = (cv * _sigmoid(cv)).astype(BF16)
        res = jnp.dot(act, w_ref[...].astype(BF16), preferred_element_type=F32)
        for b in range(NDEV):
            part[b] = res[b:b + 1, :]
        mod_ref[me] = part[me]
        cps = []
        for k in range(1, NDEV):
            to = (x ^ (k >> 2), y ^ ((k >> 1) & 1), c ^ (k & 1))
            dst = 4 * to[0] + 2 * to[1] + to[2]
            cps.append(pltpu.make_async_remote_copy(
                src_ref=part.at[dst], dst_ref=mod_ref.at[me], send_sem=s2.at[k - 1],
                recv_sem=r2.at[k - 1], device_id=to, device_id_type=MESH))
            cps[-1].start()
        for cp in cps:
            cp.wait()
        for b in range(NDEV):
            mod_ref[b] = mod_ref[b] + b_ref[b]

    vm = pl.BlockSpec(memory_space=pltpu.VMEM)
    return pl.pallas_call(
        body, name="ada_fwd", in_specs=[vm, vm, vm], out_specs=[vm, vm],
        out_shape=[_sds((NDEV, 1, cols), F32), _sds((NDEV, 1, D), F32)],
        scratch_shapes=[pltpu.VMEM((NDEV, D), F32), pltpu.VMEM((NDEV, 1, cols), F32),
                        pltpu.SemaphoreType.DMA((NDEV - 1,)), pltpu.SemaphoreType.DMA((NDEV - 1,)),
                        pltpu.SemaphoreType.DMA((NDEV - 1,)), pltpu.SemaphoreType.DMA((NDEV - 1,))],
        compiler_params=pltpu.CompilerParams(vmem_limit_bytes=VMEM_LIMIT),
    )(c_row, w_ada, b_ada.reshape(NDEV, 1, cols))


def _adamw_math(g, w, m, v):
    m2 = ADAM_B1 * m + (1.0 - ADAM_B1) * g
    v2 = ADAM_B2 * v + (1.0 - ADAM_B2) * (g * g)
    m_hat = m2 / (1.0 - ADAM_B1 ** ADAM_STEP)
    v_hat = v2 / (1.0 - ADAM_B2 ** ADAM_STEP)
    delta = -ADAM_LR * (m_hat / (jnp.sqrt(v_hat) + ADAM_EPS) + ADAM_WD * w)
    return delta, m2, v2


def adamw_sharded(name, grad8, pair4, got3, w, m, v, place, after=()):
    shape = w.shape
    cdim = shape[-1]
    R = int(math.prod(shape[:-1]))
    w2, m2, v2 = (t.reshape(R, cdim) for t in (w, m, v))
    tr = _tile(R, max(8, (1 << 19) // cdim))

    def body(q_ref, own_ref, sib_ref, t_ref, w_ref, m_ref, v_ref, g_out, d_out, m_out, v_out):
        g = own_ref[0].astype(F32) + sib_ref[0].astype(F32)
        for j in range(3):
            g = g + t_ref[j].astype(F32)
        d, mn, vn = _adamw_math(g, w_ref[...], m_ref[...], v_ref[...])
        g_out[...] = g
        d_out[...] = d
        m_out[...] = mn
        v_out[...] = vn

    spec = pl.BlockSpec((tr, cdim), lambda i, qr: (i, 0))
    outs = pl.pallas_call(
        _with_after(body, 7, after), name=name,
        grid_spec=pltpu.PrefetchScalarGridSpec(
            num_scalar_prefetch=1, grid=(R // tr,),
            in_specs=[pl.BlockSpec((1, tr, cdim), lambda i, qr: (qr[2], i, 0)),
                      pl.BlockSpec((1, tr, cdim), lambda i, qr: (qr[1], i, 0)),
                      pl.BlockSpec((3, tr, cdim), lambda i, qr: (0, i, 0)), spec, spec, spec]
            + [ANY] * len(after),
            out_specs=[spec] * 4),
        out_shape=[_sds((R, cdim), F32)] * 4,
        compiler_params=_params(("parallel",)),
    )(place, grad8.reshape(NDEV, R, cdim), pair4.reshape(NCHIP, R, cdim),
      got3.reshape(3, R, cdim), w2, m2, v2, *after)
    return [o.reshape(shape) for o in outs]


def sum_small(parts, after=()):
    R = parts.shape[1]

    def body(p_ref, g_out):
        g = p_ref[0]
        for j in range(1, NDEV):
            g = g + p_ref[j]
        g_out[...] = g

    return pl.pallas_call(
        _with_after(body, 1, after), name="sum_small", grid=(1,),
        in_specs=[pl.BlockSpec((NDEV, R, LANES), lambda i: (0, 0, 0))] + [ANY] * len(after),
        out_specs=pl.BlockSpec((R, LANES), lambda i: (0, 0)), out_shape=_sds((R, LANES), F32),
        compiler_params=_params(("arbitrary",)),
    )(parts, *after)


def adamw_natural(gs, ws, ms, vs):
    n = len(ws)
    nblk = 8
    big = [w.ndim == 4 and w.shape[1] % nblk == 0 for w in ws]

    def spec(w, is_big):
        if is_big:
            return pl.BlockSpec((1, w.shape[1] // nblk) + w.shape[2:], lambda i: (0, i, 0, 0))
        return pl.BlockSpec(w.shape, functools.partial(lambda i, nd: (0,) * nd, nd=w.ndim))

    def body(*refs):
        g_refs, w_refs, m_refs, v_refs = (refs[k * n:(k + 1) * n] for k in range(4))
        d_outs, m_outs, v_outs = (refs[(4 + k) * n:(5 + k) * n] for k in range(3))

        def update(p):
            d, mn, vn = _adamw_math(g_refs[p][...], w_refs[p][...], m_refs[p][...], v_refs[p][...])
            d_outs[p][...] = d
            m_outs[p][...] = mn
            v_outs[p][...] = vn

        for p in range(n):
            if big[p]:
                update(p)

        @pl.when(pl.program_id(0) == 0)
        def _():
            for p in range(n):
                if not big[p]:
                    update(p)

    specs = [spec(w, b) for w, b in zip(ws, big)]
    outs = pl.pallas_call(
        body, name="adamw_natural", grid=(nblk,), in_specs=specs * 4, out_specs=specs * 3,
        out_shape=[_sds(w.shape, F32) for w in ws] * 3,
        compiler_params=_params(("arbitrary",)),
    )(*gs, *ws, *ms, *vs)
    return outs[:n], outs[n:2 * n], outs[2 * n:]


def adamw_ada(c_all_t, dmod_all, w, m, v, my_dev):
    D, cols = w.shape
    tr = _tile(D, 256)

    def body(k_ref, c_ref, d_ref, w_ref, m_ref, v_ref, g_out, d_out, m_out, v_out):
        cv = c_ref[...]
        act = cv * _sigmoid(cv)
        dm = d_ref[...]
        g = act[:, 0:1] * dm[0:1, :]
        for b in range(1, NDEV):
            g = g + act[:, b:b + 1] * dm[b:b + 1, :]
        d, mn, vn = _adamw_math(g, w_ref[...], m_ref[...], v_ref[...])
        g_out[...] = g
        d_out[...] = d
        m_out[...] = mn
        v_out[...] = vn

    spec = pl.BlockSpec((tr, cols), lambda i, kr: (i, 0))
    return pl.pallas_call(
        body, name="adamw_ada",
        grid_spec=pltpu.PrefetchScalarGridSpec(
            num_scalar_prefetch=1, grid=(D // tr,),
            in_specs=[pl.BlockSpec((tr, NDEV), lambda i, kr: (i, 0)),
                      pl.BlockSpec((NDEV, cols), lambda i, kr: (0, kr[0])), spec, spec, spec],
            out_specs=[spec] * 4),
        out_shape=[_sds((D, cols), F32)] * 4,
        compiler_params=_params(("parallel",)),
    )(my_dev, c_all_t, dmod_all, w, m, v)


def _small_pack(parts):
    rows = []
    for p in parts:
        flat = p.reshape(-1)
        flat = jnp.pad(flat, (0, (-flat.shape[0]) % (SUBLANES * LANES)))
        rows.append(flat.reshape(-1, LANES))
    return jnp.concatenate(rows, axis=0)


def _small_unpack(buf, shapes):
    out, r = [], 0
    for s in shapes:
        n = int(math.prod(s))
        nr = -(-n // (SUBLANES * LANES)) * SUBLANES
        out.append(buf[r:r + nr].reshape(-1)[:n].reshape(s))
        r += nr
    return out


def kernel(x, c, w_ada, b_ada, w_in, lam_re, lam_im, log_dt, ssm_b_re, ssm_b_im, ssm_c_re, ssm_c_im, ssm_d, w_glu_val, w_glu_gate, w_pool, pool_scale, w_pool_out, w_out, ln1_g, ln1_b, w_ff1, w_ff2, ln2_g, ln2_b, loss_target, m_w_ada, m_b_ada, m_w_in, m_lam_re, m_lam_im, m_log_dt, m_ssm_b_re, m_ssm_b_im, m_ssm_c_re, m_ssm_c_im, m_ssm_d, m_w_glu_val, m_w_glu_gate, m_w_pool, m_pool_scale, m_w_pool_out, m_w_out, m_ln1_g, m_ln1_b, m_w_ff1, m_w_ff2, m_ln2_g, m_ln2_b, v_w_ada, v_b_ada, v_w_in, v_lam_re, v_lam_im, v_log_dt, v_ssm_b_re, v_ssm_b_im, v_ssm_c_re, v_ssm_c_im, v_ssm_d, v_w_glu_val, v_w_glu_gate, v_w_pool, v_pool_scale, v_w_pool_out, v_w_out, v_ln1_g, v_ln1_b, v_w_ff1, v_w_ff2, v_ln2_g, v_ln2_b):
    S, D = x.shape[1], x.shape[2]
    x2d, tgt = x[0], loss_target[0]
    W = D // 2
    G = W // SSM_GROUP
    P, H, GPB = SSM_STATE, SSM_GROUP, GROUPS_PER_BLOCK
    nblk = G // GPB
    gw = W // len(POOL_WINDOWS)
    ax, ay, ac = lax.axis_index("x"), lax.axis_index("y"), lax.axis_index("c")
    my_dev = (4 * ax + 2 * ay + ac).astype(jnp.int32).reshape(1)
    place = jnp.stack([ac, 2 * ax + ay, 4 * ax + 2 * ay + ac]).astype(jnp.int32)
    ts = _tile(S, 256)

    glu = jnp.stack([w_glu_val[0], w_glu_gate[0]]).astype(BF16)
    shards = [w_in[0].astype(BF16), glu, w_pool[0].astype(BF16), w_pool_out[0].astype(BF16),
              w_out[0].astype(BF16), w_ff1[0].astype(BF16), w_ff2[0].astype(BF16)]
    wg_in, wg_pool = seq_all_gather("gather_w_in", [shards[0], shards[2]], 1)
    wg_vg, wg_po, wg_out = seq_all_gather("gather_w_mix", [shards[1], shards[3], shards[4]], 2)
    (wg_ff1,) = seq_all_gather("gather_w_ff1", shards[5:6], 3)
    (wg_ff2,) = seq_all_gather("gather_w_ff2", shards[6:7], 11)
    wg_vg = wg_vg.reshape(2 * NDEV, W, D // NDEV)
    nwin = len(POOL_WINDOWS)
    wp_full = jnp.transpose(wg_pool, (1, 0, 2, 3)).reshape(nwin, gw, gw)
    wout_full = wg_out.reshape(1, D, D)
    wff2_full = wg_ff2.reshape(1, 4 * D, D)

    small_names = [b_ada, lam_re, lam_im, log_dt, ssm_b_re, ssm_b_im, ssm_c_re, ssm_c_im, ssm_d,
                   pool_scale, ln1_g, ln1_b, ln2_g, ln2_b]
    small_m = [m_b_ada, m_lam_re, m_lam_im, m_log_dt, m_ssm_b_re, m_ssm_b_im, m_ssm_c_re, m_ssm_c_im,
               m_ssm_d, m_pool_scale, m_ln1_g, m_ln1_b, m_ln2_g, m_ln2_b]
    small_v = [v_b_ada, v_lam_re, v_lam_im, v_log_dt, v_ssm_b_re, v_ssm_b_im, v_ssm_c_re, v_ssm_c_im,
               v_ssm_d, v_pool_scale, v_ln1_g, v_ln1_b, v_ln2_g, v_ln2_b]

    mod, c_all = ada_fwd(c, w_ada[0], b_ada)
    mod = mod.reshape(6, 1, D)
    sh1, sc1, g1, sh2, sc2, g2 = (mod[i] for i in range(6))

    f2, kconst = s5_disc(lam_re[0], lam_im[0], log_dt[0].reshape(G, 1))
    kconst = kconst.reshape(16, SUBLANES, G * P)
    f2r = f2.reshape(2, 1, G * P)
    bt_re = jnp.transpose(ssm_b_re[0], (2, 0, 1)).reshape(H, G * P)
    bt_im = jnp.transpose(ssm_b_im[0], (2, 0, 1)).reshape(H, G * P)
    ct_re = jnp.transpose(ssm_c_re[0], (1, 0, 2)).reshape(H, G * P)
    ct_im = jnp.transpose(ssm_c_im[0], (1, 0, 2)).reshape(H, G * P)
    s5_params = (f2r, bt_re, bt_im, ct_re, ct_im, ssm_d, kconst)

    def e1(t, b):
        xhat, _ = _ln_stats(t[0])
        return [xhat * (1.0 + b[0]) + b[1]], []
    (h1,) = _rowwise("ln_mod1", e1, S, ts, [(x2d, D, 0)], [sc1, sh1], [(D, BF16)], [])

    (proj,) = mm_nn("proj", h1, wg_in, F32, 2)
    z, xsb_all = s5_fwd(proj, s5_params, nblk)
    (vt,) = mm_nn("glu", z, wg_vg, BF16, 4)
    pooled = pool_fwd(proj, W, W, gw)

    def pool_epi(vals, ex, outs):
        a = vals[0]
        outs[0][...] = a
        outs[1][...] = (a * ex[0][...]).astype(BF16)
    tmp = _tile(S, 1024)
    yp, ypool = _mm(
        "pool_mix", "nn", pooled, wp_full.astype(BF16), (S // tmp, nwin, 1),
        pl.BlockSpec((tmp, gw), lambda i, j, k: (i, j)), pl.BlockSpec((1, gw, gw), lambda i, j, k: (j, 0, 0)),
        [(_sds((S, W), F32), pl.BlockSpec((tmp, gw), lambda i, j, k: (i, j))),
         (_sds((S, W), BF16), pl.BlockSpec((tmp, gw), lambda i, j, k: (i, j)))],
        (tmp, gw), 1, gw, None, pool_epi,
        [(pool_scale, pl.BlockSpec((1, gw), lambda i, j, k: (0, j)))])
    (y_b,) = mm_nn("pool_out", ypool, wg_po, BF16, 4)

    cb = D // NDEV
    ga_cb, gb_cb = (2 * W) // cb, (2 * W + D) // cb
    mcb = 4
    wm = mcb * cb
    tsm = _tile(S, 256)

    def merge_call(name, fn, ins, n_out, after=()):
        def body(*refs):
            vals = [r[...].astype(F32) for r in refs[:len(ins)]]
            for r, v in zip(refs[len(ins):], fn(*vals)):
                r[...] = v.astype(r.dtype)
        return pl.pallas_call(
            _with_after(body, len(ins), after), name=name, grid=(S // tsm, NDEV // mcb),
            in_specs=[pl.BlockSpec((tsm, w), f) for (_, w, f) in ins] + [ANY] * len(after),
            out_specs=[pl.BlockSpec((tsm, w), lambda i, j: (i, j)) for (_, w) in n_out],
            out_shape=[_sds((S, cols), BF16) for (cols, _) in n_out],
            compiler_params=_params(("parallel", "parallel")),
        )(*[a for (a, _, _) in ins], *after)

    merge_ins = [(proj, wm, lambda i, j: (i, ga_cb // mcb + j)), (proj, wm, lambda i, j: (i, gb_cb // mcb + j)),
                 (vt, 2 * wm, lambda i, j: (i, j)), (y_b, wm, lambda i, j: (i, j))]

    def val_gate(vtv):
        return (jnp.concatenate([vtv[:, 2 * q * cb:(2 * q + 1) * cb] for q in range(mcb)], axis=1),
                jnp.concatenate([vtv[:, (2 * q + 1) * cb:(2 * q + 2) * cb] for q in range(mcb)], axis=1))

    def merge_f(ga, gb, vtv, yb):
        vv, tt = val_gate(vtv)
        return [_sigmoid(ga) * (vv * _sigmoid(tt)) + _sigmoid(gb) * yb]
    (merged,) = merge_call("merge", merge_f, merge_ins, [(D, wm)])

    (mix,) = mm_nn("mix_out", merged, wout_full, F32, 1)

    def e3(t, b):
        xv, mx = t
        g1v, l1g, l1b, sc2v, sh2v = b
        r1 = ALPHA * xv + g1v * mx
        xh1, _ = _ln_stats(r1)
        x1 = xh1 * l1g + l1b
        xh, _ = _ln_stats(x1)
        return [r1, xh * (1.0 + sc2v) + sh2v], []
    r1, h2 = _rowwise("post_mix", e3, S, ts, [(x2d, D, 0), (mix, D, 0)],
                      [g1, ln1_g, ln1_b, sc2, sh2], [(D, F32), (D, BF16)], [])

    def relu_epi(vals, ex, outs):
        outs[0][...] = jnp.maximum(vals[0], 0.0).astype(BF16)
    (rl,) = mm_nn("ff1", h2, wg_ff1, BF16, 1, epi=relu_epi)

    def square(a):
        return a * a
    (y2,) = mm_nn("ff2", rl, wff2_full, F32, 1, pro=square)

    def e4(t, b):
        r1v, y2v, tg = t
        g2v, l1g, l1b, l2g, l2b = b
        xh1, _ = _ln_stats(r1v)
        x1 = xh1 * l1g + l1b
        r2 = ALPHA * x1 + g2v * y2v
        xh2, rs2 = _ln_stats(r2)
        err = xh2 * l2g + l2b - tg
        dx2 = err * (1.0 / D)
        dr2 = _ln_bwd(dx2 * l2g, xh2, rs2)
        lsum = jnp.sum(_colsum(err * err), axis=1, keepdims=True) * (0.5 / D)
        return ([ALPHA * dr2, g2v * dr2],
                [jnp.broadcast_to(lsum, (1, LANES)), _colsum(dx2 * xh2), _colsum(dx2), _colsum(dr2 * y2v)])
    dx1a, dy2, loss_acc, g_ln2g, g_ln2b, d_g2 = _rowwise(
        "head", e4, S, ts, [(r1, D, 0), (y2, D, 0), (tgt, D, 0)], [g2, ln1_g, ln1_b, ln2_g, ln2_b],
        [(D, F32), (D, BF16)], [LANES, D, D, D])

    tn_ff = _tile(4 * D, 1024)

    def dff_epi(vals, ex, outs):
        outs[0][...] = (vals[0] * (2.0 * ex[0][...].astype(F32))).astype(BF16)
    tmf = _tile(S, 1024)
    (da1,) = mm_nt("d_ff2", dy2, wff2_full, BF16, 1, tn=tn_ff, epi=dff_epi,
                   extras=[(rl, pl.BlockSpec((tmf, tn_ff), lambda i, j, k: (i, j)))])
    gw_ff2 = mm_tn("gw_ff2", rl, dy2, BF16, NDEV, 0, pro=square)
    gw_ff1 = mm_tn("gw_ff1", h2, da1, BF16, NDEV, 1)
    tok, wait_pair_a = pair_exchange("pair_exchange_ff", [gw_ff2, gw_ff1], 4)
    (dh2,) = mm_nt("d_ff1", da1, wg_ff1, F32, 4, after=[tok])

    def e5(t, b):
        dh2v, r1v, dx1av, mx = t
        sc2v, l1g, l1b, g1v = b
        xh1, rs1 = _ln_stats(r1v)
        x1 = xh1 * l1g + l1b
        xh, rs = _ln_stats(x1)
        dx1 = dx1av + _ln_bwd(dh2v * (1.0 + sc2v), xh, rs)
        dr1 = _ln_bwd(dx1 * l1g, xh1, rs1)
        return ([ALPHA * dr1, g1v * dr1],
                [_colsum(dh2v * xh), _colsum(dh2v), _colsum(dx1 * xh1), _colsum(dx1), _colsum(dr1 * mx)])
    dxa, dmix, d_sc2, d_sh2, g_ln1g, g_ln1b, d_g1 = _rowwise(
        "post_mix_bwd", e5, S, ts, [(dh2, D, 0), (r1, D, 0), (dx1a, D, 0), (mix, D, 0)],
        [sc2, ln1_g, ln1_b, g1], [(D, F32), (D, BF16)], [D, D, D, D, D])

    (dmerged,) = mm_nt("d_mix_out", dmix, wout_full, BF16, 1)
    gw_out = mm_tn("gw_out", merged, dmix, BF16, NDEV, 0)
    grads_a, got_a = wait_pair_a(gw_out)
    parts_a = [pair_sum("pair_sum_ff%d" % i, g, t, place) for i, (g, t) in enumerate(zip(grads_a, got_a))]
    tok, wait_chip_a = chip_exchange("chip_exchange_ff", parts_a, 5)

    def merge_b(ga, gb, vtv, yb, dm):
        vv, tt = val_gate(vtv)
        sa, sb, st = _sigmoid(ga), _sigmoid(gb), _sigmoid(tt)
        dya = dm * sa
        dv, dt = dya * st, dya * vv * st * (1.0 - st)
        dvt_tile = jnp.concatenate([t[:, q * cb:(q + 1) * cb] for q in range(mcb) for t in (dv, dt)], axis=1)
        return [dm * (vv * st) * sa * (1.0 - sa), dm * yb * sb * (1.0 - sb), dvt_tile, dm * sb]
    dga, dgb_, dvt, dy_b = merge_call(
        "merge_bwd", merge_b, merge_ins + [(dmerged, wm, lambda i, j: (i, j))],
        [(D, wm), (D, wm), (2 * D, 2 * wm), (D, wm)], after=[tok])

    (dypool,) = mm_nt("d_pool_out", dy_b, wg_po, F32, NDEV)
    gw_po = mm_tn("gw_pool_out", ypool, dy_b, BF16, NDEV, 4)

    def e7(t, b):
        return [t[0] * b[0]], [_colsum(t[0] * t[1])]
    dyp, g_pscale = _rowwise("pool_scale_bwd", e7, S, ts, [(dypool, W, 0), (yp, W, 0)],
                             [pool_scale], [(W, BF16)], [W])
    (dpooled,) = _mm(
        "d_pool_mix", "nt", dyp, wp_full.astype(BF16), (S // tmp, nwin, 1),
        pl.BlockSpec((tmp, gw), lambda i, j, k: (i, j)), pl.BlockSpec((1, gw, gw), lambda i, j, k: (j, 0, 0)),
        [(_sds((S, W), F32), pl.BlockSpec((tmp, gw), lambda i, j, k: (i, j)))], (tmp, gw), 1, gw)
    tkp = _tile(S, 2048)
    gw_pool = _mm(
        "gw_pool", "tn", pooled, dyp, (nwin, 1, S // tkp),
        pl.BlockSpec((tkp, gw), lambda i, j, k: (k, i)), pl.BlockSpec((tkp, gw), lambda i, j, k: (k, i)),
        [(_sds((nwin, gw, gw), BF16), pl.BlockSpec((1, gw, gw), lambda i, j, k: (i, 0, 0)))],
        (gw, gw), 1, gw, stacked_out=True)[0]
    du_pool = pool_bwd(dpooled, gw)

    (dz,) = mm_nt("d_glu", dvt, wg_vg, BF16, 2 * NDEV)
    gw_vg = mm_tn("gw_glu", z, dvt, BF16, 2 * NDEV, 4)
    gw_pool_st = jnp.transpose(gw_pool.reshape(nwin, NDEV, gw // NDEV, gw), (1, 0, 2, 3))
    grads_b = [gw_out, gw_po, gw_pool_st, gw_vg.reshape(NDEV, 2, W, D // NDEV)]
    tok, wait_pair_b = pair_exchange("pair_exchange_mix", grads_b, 6)
    du_ssm, g_bt_re, g_bt_im, g_ct_re, g_ct_im, g_f, g_d, g_a = s5_bwd(
        proj, xsb_all, dz, s5_params, nblk, after=[tok])
    grads_b, got_b = wait_pair_b(du_ssm)
    parts_b = [pair_sum("pair_sum_mix%d" % i, g, t, place) for i, (g, t) in enumerate(zip(grads_b, got_b))]
    tok, wait_chip_b = chip_exchange("chip_exchange_mix", parts_b, 7)

    dproj = jnp.concatenate([du_ssm, du_pool, dga, dgb_], axis=1)
    gw_in = mm_tn("gw_in", h1, dproj, BF16, NDEV, 1, after=[tok])
    tok, wait_pair_c = pair_exchange("pair_exchange_in", [gw_in], 8)
    (dh1,) = mm_nt("d_proj", dproj, wg_in, F32, 4, after=[tok])
    grads_c, got_c = wait_pair_c(dh1)
    parts_c = [pair_sum("pair_sum_in", grads_c[0], got_c[0], place)]
    tok, wait_chip_c = chip_exchange("chip_exchange_in", parts_c, 9)

    def e10(t, b):
        dh1v, xv, dxav = t
        xh, rs = _ln_stats(xv)
        return ([dxav + _ln_bwd(dh1v * (1.0 + b[0]), xh, rs)],
                [_colsum(dh1v * xh), _colsum(dh1v)])
    grad_x, d_sc1, d_sh1 = _rowwise("ln_mod1_bwd", e10, S, ts, [(dh1, D, 0), (x2d, D, 0), (dxa, D, 0)],
                                    [sc1], [(D, F32)], [D, D], after=[tok])

    g_b_re = jnp.transpose(g_bt_re.reshape(H, G, P), (1, 2, 0))
    g_b_im = jnp.transpose(g_bt_im.reshape(H, G, P), (1, 2, 0))
    g_c_re = jnp.transpose(g_ct_re.reshape(H, G, P), (1, 0, 2))
    g_c_im = jnp.transpose(g_ct_im.reshape(H, G, P), (1, 0, 2))
    d_ab = jnp.transpose(g_a.reshape(nblk, 2, GPB, P), (1, 0, 2, 3)).reshape(2, G, P)
    g_lr, g_li, g_ldt = s5_disc_bwd(lam_re[0], lam_im[0], log_dt[0].reshape(G, 1), d_ab,
                                    g_f.reshape(2, G, P))

    dmod = jnp.concatenate([d_sh1, d_sc1, d_g1, d_sh2, d_sc2, d_g2], axis=1)
    small_g = [dmod, g_lr, g_li, g_ldt, g_b_re, g_b_im, g_c_re, g_c_im, g_d, g_pscale,
               g_ln1g, g_ln1b, g_ln2g, g_ln2b, loss_acc]
    packed_g = _small_pack(small_g)
    (parts_all,) = seq_all_gather("gather_small", [packed_g], 10)
    glu_w = jnp.stack([w_glu_val[0], w_glu_gate[0]])
    glu_m = jnp.stack([m_w_glu_val[0], m_w_glu_gate[0]])
    glu_v = jnp.stack([v_w_glu_val[0], v_w_glu_gate[0]])
    wmv = [(w_ff2[0], m_w_ff2[0], v_w_ff2[0]), (w_ff1[0], m_w_ff1[0], v_w_ff1[0]),
           (w_out[0], m_w_out[0], v_w_out[0]), (w_pool_out[0], m_w_pool_out[0], v_w_pool_out[0]),
           (w_pool[0], m_w_pool[0], v_w_pool[0]), (glu_w, glu_m, glu_v)]
    _, got3_a = wait_chip_a(packed_g)
    upd = [adamw_sharded("adamw_%d" % i, g, p, t, w, m, v, place)
           for i, (g, p, t, (w, m, v)) in enumerate(zip(grads_a, got_a, got3_a, wmv[:2]))]
    _, got3_b = wait_chip_b(upd[-1][0])
    upd += [adamw_sharded("adamw_%d" % (2 + i), g, p, t, w, m, v, place)
            for i, (g, p, t, (w, m, v)) in enumerate(zip(grads_b, got_b, got3_b, wmv[2:]))]
    u_ff2, u_ff1, u_out, u_po, u_pool, u_glu = upd

    gsum = sum_small(parts_all, after=[upd[-1][0]])
    sg = _small_unpack(gsum, [t.shape for t in small_names] + [(1, LANES)])
    loss, sg = sg[-1][0, 0], sg[:-1]
    sd, sm, sv = adamw_natural(sg, small_names, small_m, small_v)

    nmod = 6 * D
    dmod_all = parts_all[:, :nmod // LANES, :].reshape(NDEV, nmod)
    c_all_t = jnp.transpose(c_all.reshape(NDEV, D))
    ada_out = adamw_ada(c_all_t, dmod_all, w_ada[0], m_w_ada[0], v_w_ada[0], my_dev)
    _, got3_c = wait_chip_c(ada_out[0])
    u_in = adamw_sharded("adamw_6", grads_c[0], got_c[0], got3_c[0], w_in[0], m_w_in[0], v_w_in[0], place)

    def pick(k):
        return [ada_out[k][None], sg_sd[k][0], u_in[k][None]] + [t for t in sg_sd[k][1:9]] + \
               [u_glu[k][0][None], u_glu[k][1][None], u_pool[k][None], sg_sd[k][9], u_po[k][None],
                u_out[k][None], sg_sd[k][10], sg_sd[k][11], u_ff1[k][None], u_ff2[k][None],
                sg_sd[k][12], sg_sd[k][13]]

    sg_sd = [sg, sd, sm, sv]
    return (loss, grad_x[None], *pick(0), *pick(1), *pick(2), *pick(3))
```

```python
import functools
import math

import jax
import jax.numpy as jnp
from jax import lax
from jax.experimental import pallas as pl
from jax.experimental.pallas import tpu as pltpu
from jax.experimental.pallas import tpu_sc as plsc

F32 = jnp.float32
BF16 = jnp.bfloat16
MESH = pl.DeviceIdType.MESH
NDEV = 8
NCHIP = 4

SSM_GROUP = 16
SSM_STATE = 64
GROUPS_PER_BLOCK = 8
POOL_WINDOWS = (2, 4, 8, 16)
LN_EPS = 1e-5
ALPHA = 2.0 ** 0.25
ADAM_LR, ADAM_B1, ADAM_B2, ADAM_EPS, ADAM_WD, ADAM_STEP = 0.001, 0.9, 0.999, 1e-08, 0.01, 10
SUBLANES = 8
LANES = 128
VMEM_LIMIT = 56 * 1024 * 1024


def _params(sem=None, vmem=VMEM_LIMIT):
    return pltpu.CompilerParams(dimension_semantics=sem, vmem_limit_bytes=vmem)


def _tile(n, pref):
    if n <= pref:
        return n
    t = 1 << (pref.bit_length() - 1)
    while n % t:
        t //= 2
    return t


def _cast_epi(vals, ex, outs):
    c = vals[0].shape[1]
    for s, v in enumerate(vals):
        outs[0][:, s * c:(s + 1) * c] = v.astype(outs[0].dtype)


ANY = pl.BlockSpec(memory_space=pl.ANY)


def _with_after(body, n_in, after):
    if not after:
        return body
    n_af = len(after)

    def wrapped(*refs):
        return body(*refs[:n_in], *refs[n_in + n_af:])
    return wrapped


def _mm(name, kind, a, b, grid, a_spec, b_spec, outs, acc_shape, nsub=1, c=None,
        pro=None, epi=None, extras=(), stacked_out=False, after=()):
    nk = grid[2]
    n_ex, n_out = len(extras), len(outs)

    def finish(vals, ex, out_refs):
        if epi is not None:
            epi(vals, ex, out_refs)
        elif stacked_out:
            for s, v in enumerate(vals):
                out_refs[0][s] = v.astype(out_refs[0].dtype)
        else:
            _cast_epi(vals, ex, out_refs)

    def body(*refs):
        mm_step(refs[0], refs[1], refs[2:2 + n_ex], refs[2 + n_ex:2 + n_ex + n_out], refs[-1])

    def mm_step(a_ref, b_ref, ex, out_refs, acc):
        k = pl.program_id(2)
        av = a_ref[...]
        if pro is not None:
            av = pro(av)
        if kind == "nn":
            prods = [jnp.dot(av, b_ref[s], preferred_element_type=F32) for s in range(nsub)]
        elif kind == "nt":
            t = None
            for s in range(nsub):
                d = lax.dot_general(av[:, s * c:(s + 1) * c], b_ref[s], (((1,), (1,)), ((), ())),
                                    preferred_element_type=F32)
                t = d if t is None else t + d
            prods = [t]
        else:
            t = lax.dot_general(av, b_ref[...], (((0,), (0,)), ((), ())), preferred_element_type=F32)
            prods = [t[:, s * c:(s + 1) * c] for s in range(nsub)] if stacked_out else [t]
        if nk == 1:
            finish(prods, ex, out_refs)
            return
        w = prods[0].shape[1]

        @pl.when(k == 0)
        def _():
            for s, p in enumerate(prods):
                acc[:, s * w:(s + 1) * w] = p

        @pl.when(jnp.logical_and(k > 0, k < nk - 1))
        def _():
            for s, p in enumerate(prods):
                acc[:, s * w:(s + 1) * w] += p

        @pl.when(k == nk - 1)
        def _():
            finish([acc[:, s * w:(s + 1) * w] + p for s, p in enumerate(prods)], ex, out_refs)

    return pl.pallas_call(
        _with_after(body, 2 + n_ex, after), name=name, grid=grid,
        in_specs=[a_spec, b_spec] + [e[1] for e in extras] + [ANY] * len(after),
        out_specs=[o[1] for o in outs],
        out_shape=[o[0] for o in outs],
        scratch_shapes=[pltpu.VMEM(acc_shape, F32)] if nk > 1 else [],
        compiler_params=_params(("parallel", "parallel", "arbitrary")),
    )(a, b, *[e[0] for e in extras], *after)


def _sds(shape, dtype):
    return jax.ShapeDtypeStruct(shape, dtype)


def mm_nn(name, a, b3, out_dtype, nsub, tm=1024, tk=2048, tn=None, pro=None, epi=None,
          extras=(), extra_outs=(), a_col0=0, after=()):
    M = a.shape[0]
    nb, K, cdim = b3.shape
    tm, tk = _tile(M, tm), _tile(K, tk)
    if nb == 1:
        tn = _tile(cdim, tn or 1024)
        nsub, c, nj = 1, tn, cdim // tn
        b_spec = pl.BlockSpec((1, tk, tn), lambda i, j, k: (0, k, j))
        N = cdim
    else:
        c, nj, tn = cdim, nb // nsub, nsub * cdim
        b_spec = pl.BlockSpec((nsub, tk, cdim), lambda i, j, k: (j, k, 0))
        N = nb * cdim
    kb0 = a_col0 // tk
    a_spec = pl.BlockSpec((tm, tk), lambda i, j, k: (i, kb0 + k))
    grid = (M // tm, nj, K // tk)
    o_spec = pl.BlockSpec((tm, tn), lambda i, j, k: (i, j))
    outs = [(_sds((M, N), out_dtype), o_spec)] + [(_sds((M, N), d), o_spec) for d in extra_outs]
    return _mm(name, "nn", a, b3, grid, a_spec, b_spec, outs, (tm, tn), nsub, c, pro, epi, extras,
               after=after)


def mm_nt(name, a, b3, out_dtype, nsub, tm=1024, tn=1024, epi=None, extras=(), extra_outs=(),
          after=()):
    M = a.shape[0]
    nb, N, cdim = b3.shape
    tm, tn = _tile(M, tm), _tile(N, tn)
    if nb == 1:
        tk = _tile(cdim, 2048)
        nsub, c, nk = 1, tk, cdim // tk
        b_spec = pl.BlockSpec((1, tn, tk), lambda i, j, k: (0, j, k))
    else:
        c, nk, tk = cdim, nb // nsub, nsub * cdim
        b_spec = pl.BlockSpec((nsub, tn, cdim), lambda i, j, k: (k, j, 0))
    a_spec = pl.BlockSpec((tm, tk), lambda i, j, k: (i, k))
    grid = (M // tm, N // tn, nk)
    o_spec = pl.BlockSpec((tm, tn), lambda i, j, k: (i, j))
    outs = [(_sds((M, N), out_dtype), o_spec)] + [(_sds((M, N), d), o_spec) for d in extra_outs]
    return _mm(name, "nt", a, b3, grid, a_spec, b_spec, outs, (tm, tn), nsub, c, None, epi, extras,
               after=after)


def mm_tn(name, a, b, out_dtype, nb, nsub, tma=1024, tk=2048, pro=None, a_col0=0, a_cols=None,
          after=()):
    S = a.shape[0]
    Ka = a_cols or a.shape[1]
    N = b.shape[1]
    tk = _tile(S, tk)
    if nsub == 0:
        tma, tn = _tile(Ka, tma), _tile(N, 1024)
        grid = (Ka // tma, N // tn, S // tk)
        ab0 = a_col0 // tma
        res = _mm(name, "tn", a, b, grid, pl.BlockSpec((tk, tma), lambda i, j, k: (k, ab0 + i)),
                  pl.BlockSpec((tk, tn), lambda i, j, k: (k, j)),
                  [(_sds((Ka, N), out_dtype), pl.BlockSpec((tma, tn), lambda i, j, k: (i, j)))],
                  (tma, tn), 1, tn, pro, None, (), after=after)[0]
        return res.reshape(nb, Ka // nb, N)
    else:
        c = N // nb
        tma = _tile(Ka, tma)
        grid = (Ka // tma, nb // nsub, S // tk)
        o_spec = pl.BlockSpec((nsub, tma, c), lambda i, j, k: (j, i, 0))
        out = _sds((nb, Ka, c), out_dtype)
        nsub_k = nsub
        tn = nsub * c
        b_spec = pl.BlockSpec((tk, tn), lambda i, j, k: (k, j))
    ab0 = a_col0 // tma
    a_spec = pl.BlockSpec((tk, tma), lambda i, j, k: (k, ab0 + i))
    return _mm(name, "tn", a, b, grid, a_spec, b_spec, [(out, o_spec)], (tma, tn), nsub_k, c,
               pro, None, (), stacked_out=True, after=after)[0]


def _rowwise(name, fn, S, ts, tiled, bcast, tiled_out, acc_out, after=()):
    nt, nb, no, na = len(tiled), len(bcast), len(tiled_out), len(acc_out)

    def body(*refs):
        tin = [r[...] for r in refs[:nt]]
        bin_ = [r[...] for r in refs[nt:nt + nb]]
        o_refs = refs[nt + nb:nt + nb + no]
        a_refs = refs[nt + nb + no:]
        touts, aouts = fn(tin, bin_)
        for r, v in zip(o_refs, touts):
            r[...] = v.astype(r.dtype)
        i = pl.program_id(0)

        @pl.when(i == 0)
        def _():
            for r, v in zip(a_refs, aouts):
                r[...] = v

        @pl.when(i > 0)
        def _():
            for r, v in zip(a_refs, aouts):
                r[...] += v

    in_specs = [pl.BlockSpec((ts, w), functools.partial(lambda i, cb: (i, cb), cb=cb))
                for (_, w, cb) in tiled]
    in_specs += [pl.BlockSpec(b.shape, lambda i: (0, 0)) for b in bcast]
    out_specs = [pl.BlockSpec((ts, w), lambda i: (i, 0)) for (w, _) in tiled_out]
    out_specs += [pl.BlockSpec((1, w), lambda i: (0, 0)) for w in acc_out]
    out_shape = [_sds((S, w), d) for (w, d) in tiled_out] + [_sds((1, w), F32) for w in acc_out]
    return pl.pallas_call(
        _with_after(body, nt + nb, after), name=name, grid=(S // ts,),
        in_specs=in_specs + [ANY] * len(after), out_specs=out_specs,
        out_shape=out_shape, compiler_params=_params(("arbitrary",)),
    )(*[t[0] for t in tiled], *bcast, *after)


def _ln_stats(v):
    mu = jnp.mean(v, axis=-1, keepdims=True)
    vc = v - mu
    var = jnp.mean(vc * vc, axis=-1, keepdims=True)
    rstd = lax.rsqrt(var + LN_EPS)
    return vc * rstd, rstd


def _ln_bwd(dxhat, xhat, rstd):
    return rstd * (dxhat - jnp.mean(dxhat, axis=-1, keepdims=True)
                   - xhat * jnp.mean(dxhat * xhat, axis=-1, keepdims=True))


def _colsum(v):
    return jnp.sum(v, axis=0, keepdims=True)


def _sigmoid(v):
    return 1.0 / (1.0 + jnp.exp(-v))


_GELU_C = math.sqrt(2.0 / math.pi)


def _gelu(v):
    return 0.5 * v * (1.0 + jnp.tanh(_GELU_C * (v + 0.044715 * v * v * v)))


def _gelu_grad(v):
    t = jnp.tanh(_GELU_C * (v + 0.044715 * v * v * v))
    return 0.5 * (1.0 + t) + 0.5 * v * (1.0 - t * t) * _GELU_C * (1.0 + 3 * 0.044715 * v * v)


def _disc(lr, li, ldt):
    dt = jnp.exp(ldt)
    mag = jnp.exp(lr * dt)
    ang = li * dt
    ab_re = mag * jnp.cos(ang)
    ab_im = mag * jnp.sin(ang)
    num_re = ab_re - 1.0
    num_im = ab_im
    den = lr * lr + li * li
    f_re = (num_re * lr + num_im * li) / den
    f_im = (num_im * lr - num_re * li) / den
    return ab_re, ab_im, f_re, f_im


def _cmul(ar, ai, br, bi):
    return ar * br - ai * bi, ar * bi + ai * br


SCAN_FOLD = 4
NCONST = 18


def s5_disc(lam_re, lam_im, log_dt):
    G, P = lam_re.shape

    def body(lr_ref, li_ref, ldt_ref, f_ref, k_ref):
        ab_re, ab_im, f_re, f_im = _disc(lr_ref[...], li_ref[...], ldt_ref[...])
        f_ref[0] = f_re
        f_ref[1] = f_im
        fr, fi = ab_re, ab_im
        for _ in range(SCAN_FOLD - 1):
            fr, fi = _cmul(fr, fi, ab_re, ab_im)
        pr, pi = [fr], [fi]
        for _ in range(SUBLANES - 1):
            nr, ni = _cmul(pr[-1], pi[-1], fr, fi)
            pr.append(nr)
            pi.append(ni)
        zero = jnp.zeros_like(ab_re)
        for r in range(SUBLANES):
            k_ref[16, r] = ab_re
            k_ref[17, r] = ab_im
        for n, sh in enumerate((1, 2, 4)):
            for r in range(SUBLANES):
                k_ref[2 * n, r] = pr[sh - 1] if r >= sh else zero
                k_ref[2 * n + 1, r] = pi[sh - 1] if r >= sh else zero
                k_ref[8 + 2 * n, r] = pr[sh - 1] if r + sh < SUBLANES else zero
                k_ref[8 + 2 * n + 1, r] = -pi[sh - 1] if r + sh < SUBLANES else zero
        for r in range(SUBLANES):
            k_ref[6, r] = pr[r]
            k_ref[7, r] = pi[r]
            k_ref[14, r] = pr[SUBLANES - 1 - r]
            k_ref[15, r] = -pi[SUBLANES - 1 - r]

    vm = pl.BlockSpec(memory_space=pltpu.VMEM)
    return pl.pallas_call(
        body, name="s5_disc", in_specs=[vm, vm, vm], out_specs=[vm, vm],
        out_shape=[_sds((2, G, P), F32), _sds((NCONST, SUBLANES, G, P), F32)],
    )(lam_re, lam_im, log_dt)


def s5_disc_bwd(lam_re, lam_im, log_dt, d_ab, d_f):
    G, P = lam_re.shape

    def body(lr_ref, li_ref, ldt_ref, dab_ref, df_ref, glr_ref, gli_ref, gdt_ref):
        _, vjp = jax.vjp(_disc, lr_ref[...], li_ref[...], ldt_ref[...])
        glr, gli, gdt = vjp((dab_ref[0], dab_ref[1], df_ref[0], df_ref[1]))
        glr_ref[...] = glr
        gli_ref[...] = gli
        gdt_ref[...] = gdt

    vm = pl.BlockSpec(memory_space=pltpu.VMEM)
    return pl.pallas_call(
        body, name="s5_disc_bwd", in_specs=[vm] * 5, out_specs=[vm] * 3,
        out_shape=[_sds((G, P), F32), _sds((G, P), F32), _sds((G, 1), F32)],
    )(lam_re, lam_im, log_dt, d_ab, d_f)


def _group_mask(cw, nst):
    row = lax.broadcasted_iota(jnp.int32, (cw, 2 * nst), 0) // SSM_GROUP
    col = (lax.broadcasted_iota(jnp.int32, (cw, 2 * nst), 1) % nst) // SSM_STATE
    return row == col


def _spread(t, mask):
    reps = mask.shape[0] // t.shape[0]
    return jnp.where(mask, jnp.tile(t, (reps, 1)), 0.0).astype(BF16)


def _gather_groups(t, mask):
    t = jnp.where(mask, t, 0.0)
    out = t[0:SSM_GROUP]
    for g in range(1, t.shape[0] // SSM_GROUP):
        out = out + t[g * SSM_GROUP:(g + 1) * SSM_GROUP]
    return out


def _s5_operands(f_ref, br_ref, bi_ref, cr_ref, ci_ref, mask):
    fr, fi = f_ref[0], f_ref[1]
    br, bi = br_ref[...], bi_ref[...]
    bm = _spread(jnp.concatenate([fr * br - fi * bi, fr * bi + fi * br], axis=1), mask)
    cm = _spread(jnp.concatenate([cr_ref[...], -ci_ref[...]], axis=1), mask)
    return bm, cm


def _planes_put(ref, val):
    for c in range(ref.shape[0]):
        ref[c] = val[:, c * LANES:(c + 1) * LANES]


def _planes_get(ref):
    return jnp.concatenate([ref[c] for c in range(ref.shape[0])], axis=1)


def _rows_ld(ref, start, lo, hi):
    rows = pl.ds(start, SUBLANES, stride=SCAN_FOLD)
    return jnp.concatenate([ref[c, rows, :] for c in range(lo // LANES, hi // LANES)], axis=1)


def _rows_st(ref, start, lo, val):
    rows = pl.ds(start, SUBLANES, stride=SCAN_FOLD)
    for k in range(val.shape[1] // LANES):
        ref[lo // LANES + k, rows, :] = val[:, k * LANES:(k + 1) * LANES]


def _phases(ref, base, lo, hi):
    return [_rows_ld(ref, base + j, lo, hi) for j in range(SCAN_FOLD)]


def _row_bcast(v, r):
    return jnp.broadcast_to(v[r:r + 1, :], v.shape)


def _scan_fwd(xs, k_ref, nst):
    m = SCAN_FOLD
    ngroup = xs.shape[1] // (SUBLANES * m)
    row = lax.broadcasted_iota(jnp.int32, (SUBLANES, nst), 0)

    def step(t, carry):
        cr, ci = carry
        base = pl.multiple_of(t * (SUBLANES * m), SUBLANES * m)
        ar, ai = k_ref[16], k_ref[17]
        pr, pi = _phases(xs, base, 0, nst), _phases(xs, base, nst, 2 * nst)
        vr, vi = pr[0], pi[0]
        for j in range(1, m):
            vr, vi = pr[j] + ar * vr - ai * vi, pi[j] + ar * vi + ai * vr
        for n, sh in enumerate((1, 2, 4)):
            sr = pltpu.roll(vr, sh, 0)
            si = pltpu.roll(vi, sh, 0)
            mr, mi = k_ref[2 * n], k_ref[2 * n + 1]
            vr, vi = vr + mr * sr - mi * si, vi + mr * si + mi * sr
        qr, qi = k_ref[6], k_ref[7]
        vr, vi = vr + qr * cr - qi * ci, vi + qr * ci + qi * cr
        _rows_st(xs, base + m - 1, 0, vr)
        _rows_st(xs, base + m - 1, nst, vi)
        xr = jnp.where(row == 0, cr, pltpu.roll(vr, 1, 0))
        xi = jnp.where(row == 0, ci, pltpu.roll(vi, 1, 0))
        for j in range(m - 1):
            xr, xi = pr[j] + ar * xr - ai * xi, pi[j] + ar * xi + ai * xr
            _rows_st(xs, base + j, 0, xr)
            _rows_st(xs, base + j, nst, xi)
        return _row_bcast(vr, SUBLANES - 1), _row_bcast(vi, SUBLANES - 1)

    zero = jnp.zeros((SUBLANES, nst), F32)
    lax.fori_loop(0, ngroup, step, (zero, zero))


def _scan_bwd(g, xs, k_ref, nst):
    m = SCAN_FOLD
    ngroup = g.shape[1] // (SUBLANES * m)
    row = lax.broadcasted_iota(jnp.int32, (SUBLANES, nst), 0)

    def step(tt, carry):
        cr, ci, dar, dai = carry
        t = ngroup - 1 - tt
        base = pl.multiple_of(t * (SUBLANES * m), SUBLANES * m)
        ar, ai = k_ref[16], -k_ref[17]
        dr, di = _phases(g, base, 0, nst), _phases(g, base, nst, 2 * nst)
        wr, wi = dr[m - 1], di[m - 1]
        for j in range(m - 2, -1, -1):
            wr, wi = dr[j] + ar * wr - ai * wi, di[j] + ar * wi + ai * wr
        for n, sh in enumerate((1, 2, 4)):
            sr = pltpu.roll(wr, SUBLANES - sh, 0)
            si = pltpu.roll(wi, SUBLANES - sh, 0)
            mr, mi = k_ref[8 + 2 * n], k_ref[8 + 2 * n + 1]
            wr, wi = wr + mr * sr - mi * si, wi + mr * si + mi * sr
        qr, qi = k_ref[14], k_ref[15]
        wr, wi = wr + qr * cr - qi * ci, wi + qr * ci + qi * cr
        gr, gi = [None] * m, [None] * m
        gr[0], gi[0] = wr, wi
        nr = jnp.where(row == SUBLANES - 1, cr, pltpu.roll(wr, SUBLANES - 1, 0))
        ni = jnp.where(row == SUBLANES - 1, ci, pltpu.roll(wi, SUBLANES - 1, 0))
        for j in range(m - 1, 0, -1):
            nr, ni = dr[j] + ar * nr - ai * ni, di[j] + ar * ni + ai * nr
            gr[j], gi[j] = nr, ni
        for j in range(m):
            _rows_st(g, base + j, 0, gr[j])
            _rows_st(g, base + j, nst, gi[j])
        xr, xi = _phases(xs, base, 0, nst), _phases(xs, base, nst, 2 * nst)
        pbase = pl.multiple_of(jnp.maximum(t - 1, 0) * (SUBLANES * m), SUBLANES * m)
        live = (t > 0).astype(F32)
        lr = _row_bcast(_rows_ld(xs, pbase + m - 1, 0, nst), SUBLANES - 1) * live
        li = _row_bcast(_rows_ld(xs, pbase + m - 1, nst, 2 * nst), SUBLANES - 1) * live
        xmr = [jnp.where(row == 0, lr, pltpu.roll(xr[m - 1], 1, 0))] + xr[:m - 1]
        xmi = [jnp.where(row == 0, li, pltpu.roll(xi[m - 1], 1, 0))] + xi[:m - 1]
        for j in range(m):
            dar = dar + gr[j] * xmr[j] + gi[j] * xmi[j]
            dai = dai + gi[j] * xmr[j] - gr[j] * xmi[j]
        return _row_bcast(wr, 0), _row_bcast(wi, 0), dar, dai

    zero = jnp.zeros((SUBLANES, nst), F32)
    _, _, dar, dai = lax.fori_loop(0, ngroup, step, (zero, zero, zero, zero))
    return _colsum(dar), _colsum(dai)


def _s5_param_specs(cw, nst):
    hp = pl.BlockSpec((SSM_GROUP, nst), lambda b: (0, b))
    return [pl.BlockSpec((2, 1, nst), lambda b: (0, 0, b)), hp, hp, hp, hp,
            pl.BlockSpec((1, cw), lambda b: (0, b)),
            pl.BlockSpec((NCONST, SUBLANES, nst), lambda b: (0, 0, b))]


def s5_fwd(proj, params, nb):
    S = proj.shape[0]
    nst = params[1].shape[1] // nb
    cw = nst // SSM_STATE * SSM_GROUP

    def body(u_ref, f_ref, br_ref, bi_ref, cr_ref, ci_ref, d_ref, k_ref, z_ref, xsb_ref, xs):
        bm, cm = _s5_operands(f_ref, br_ref, bi_ref, cr_ref, ci_ref, _group_mask(cw, nst))
        u = u_ref[...]
        _planes_put(xs, jnp.dot(u.astype(BF16), bm, preferred_element_type=F32))
        _scan_fwd(xs, k_ref, nst)
        xsb = _planes_get(xs).astype(BF16)
        xsb_ref[...] = xsb
        y = lax.dot_general(xsb, cm, (((1,), (1,)), ((), ())), preferred_element_type=F32)
        z_ref[...] = _gelu(y + d_ref[...] * u).astype(BF16)

    return pl.pallas_call(
        body, name="s5_fwd", grid=(nb,),
        in_specs=[pl.BlockSpec((S, cw), lambda b: (0, b))] + _s5_param_specs(cw, nst),
        out_specs=[pl.BlockSpec((S, cw), lambda b: (0, b)), pl.BlockSpec((S, 2 * nst), lambda b: (0, b))],
        out_shape=[_sds((S, nb * cw), BF16), _sds((S, nb * 2 * nst), BF16)],
        scratch_shapes=[pltpu.VMEM((2 * nst // LANES, S, LANES), F32)],
        compiler_params=_params(("arbitrary",)),
    )(proj, *params)


def s5_bwd(proj, xsb_all, dz, params, nb, after=()):
    S = proj.shape[0]
    nst = params[1].shape[1] // nb
    cw = nst // SSM_STATE * SSM_GROUP

    def body(u_ref, xsb_ref, dz_ref, f_ref, br_ref, bi_ref, cr_ref, ci_ref, d_ref, k_ref,
             du_ref, gbr_ref, gbi_ref, gcr_ref, gci_ref, gf_ref, gd_ref, ga_ref, xs, g):
        mask = _group_mask(cw, nst)
        bm, cm = _s5_operands(f_ref, br_ref, bi_ref, cr_ref, ci_ref, mask)
        u = u_ref[...]
        ub = u.astype(BF16)
        d = d_ref[...]
        xsb = xsb_ref[...]
        _planes_put(xs, xsb.astype(F32))
        y = lax.dot_general(xsb, cm, (((1,), (1,)), ((), ())), preferred_element_type=F32) + d * u
        dy = dz_ref[...].astype(F32) * _gelu_grad(y)
        gd_ref[...] = _colsum(dy * u)
        dyb = dy.astype(BF16)
        gc = _gather_groups(lax.dot_general(dyb, xsb, (((0,), (0,)), ((), ())),
                                            preferred_element_type=F32), mask)
        gcr_ref[...] = gc[:, :nst]
        gci_ref[...] = -gc[:, nst:]
        _planes_put(g, jnp.dot(dyb, cm, preferred_element_type=F32))
        ar, ai = _scan_bwd(g, xs, k_ref, nst)
        ga_ref[0, 0:1, :] = ar
        ga_ref[0, 1:2, :] = ai
        gb = _planes_get(g).astype(BF16)
        du = lax.dot_general(gb, bm, (((1,), (1,)), ((), ())), preferred_element_type=F32) + d * dy
        du_ref[...] = du.astype(BF16)
        gbb = _gather_groups(lax.dot_general(ub, gb, (((0,), (0,)), ((), ())),
                                             preferred_element_type=F32), mask)
        dr, di = gbb[:, :nst], gbb[:, nst:]
        fr, fi = f_ref[0], f_ref[1]
        br, bi = br_ref[...], bi_ref[...]
        gbr_ref[...] = fr * dr + fi * di
        gbi_ref[...] = fr * di - fi * dr
        gf_ref[0] = _colsum(dr * br + di * bi)
        gf_ref[1] = _colsum(di * br - dr * bi)

    hp = pl.BlockSpec((SSM_GROUP, nst), lambda b: (0, b))
    hp_sds = _sds((SSM_GROUP, nb * nst), F32)
    return pl.pallas_call(
        _with_after(body, 10, after), name="s5_bwd", grid=(nb,),
        in_specs=[pl.BlockSpec((S, cw), lambda b: (0, b)),
                  pl.BlockSpec((S, 2 * nst), lambda b: (0, b)),
                  pl.BlockSpec((S, cw), lambda b: (0, b))] + _s5_param_specs(cw, nst)
        + [ANY] * len(after),
        out_specs=[pl.BlockSpec((S, cw), lambda b: (0, b)), hp, hp, hp, hp,
                   pl.BlockSpec((2, 1, nst), lambda b: (0, 0, b)),
                   pl.BlockSpec((1, cw), lambda b: (0, b)),
                   pl.BlockSpec((1, 2, nst), lambda b: (b, 0, 0))],
        out_shape=[_sds((S, nb * cw), BF16), hp_sds, hp_sds, hp_sds, hp_sds,
                   _sds((2, 1, nb * nst), F32), _sds((1, nb * cw), F32), _sds((nb, 2, nst), F32)],
        scratch_shapes=[pltpu.VMEM((2 * nst // LANES, S, LANES), F32)] * 2,
        compiler_params=_params(("arbitrary",)),
    )(proj, xsb_all, dz, *params, *after)


def _shift_rows(v, k, row, down):
    n = v.shape[0]
    if down:
        return jnp.where(row >= k, pltpu.roll(v, k, 0), 0.0)
    return jnp.where(row < n - k, pltpu.roll(v, n - k, 0), 0.0)


def _window(v, gi, row, down):
    sums = []
    s = v
    for k in (1, 2, 4, 8):
        s = s + _shift_rows(s, k, row, down)
        sums.append(s)
    out = sums[3]
    for n in (2, 1, 0):
        out = jnp.where(gi == n, sums[n], out)
    return out


def pool_fwd(proj, col0, width, gw):
    S = proj.shape[0]
    cb0 = col0 // gw

    def body(u_ref, o_ref):
        gi = pl.program_id(0)
        u = u_ref[...]
        row = lax.broadcasted_iota(jnp.int32, u.shape, 0)
        w = jnp.left_shift(2, gi)
        count = jnp.minimum(row + 1, w).astype(F32)
        o_ref[...] = (_window(u, gi, row, True) / count - u).astype(BF16)

    return pl.pallas_call(
        body, name="pool_fwd", grid=(len(POOL_WINDOWS),),
        in_specs=[pl.BlockSpec((S, gw), lambda g: (0, cb0 + g))],
        out_specs=pl.BlockSpec((S, gw), lambda g: (0, g)),
        out_shape=_sds((S, width), BF16), compiler_params=_params(("arbitrary",)),
    )(proj)


def pool_bwd(dpooled, gw):
    S, width = dpooled.shape

    def body(d_ref, o_ref):
        gi = pl.program_id(0)
        d = d_ref[...]
        row = lax.broadcasted_iota(jnp.int32, d.shape, 0)
        w = jnp.left_shift(2, gi)
        count = jnp.minimum(row + 1, w).astype(F32)
        o_ref[...] = (_window(d / count, gi, row, False) - d).astype(BF16)

    return pl.pallas_call(
        body, name="pool_bwd", grid=(len(POOL_WINDOWS),),
        in_specs=[pl.BlockSpec((S, gw), lambda g: (0, g))],
        out_specs=pl.BlockSpec((S, gw), lambda g: (0, g)),
        out_shape=_sds((S, width), BF16), compiler_params=_params(("arbitrary",)),
    )(dpooled)


def _place():
    x, y, c = lax.axis_index("x"), lax.axis_index("y"), lax.axis_index("c")
    chips = [(1 - x, y), (x, 1 - y), (1 - x, 1 - y)]
    return x, y, c, chips


HBM = pl.BlockSpec(memory_space=pltpu.HBM)


def _routed_gather_body(n):
    def body(*refs):
        ins, outs = refs[:n], refs[n:2 * n]
        send_sems, recv_sems, local_sems = refs[2 * n:]
        x, y, c, (xn, yn, dg) = _place()
        me, sibling = (x, y, c), (x, y, 1 - c)
        barrier = pltpu.get_barrier_semaphore()
        for peer in (sibling, (*xn, c), (*yn, c)):
            pl.semaphore_signal(barrier, inc=1, device_id=peer, device_id_type=MESH)
        pl.semaphore_wait(barrier, 3)

        def piece(i, p, h):
            rows = ins[i].shape[0] // 2
            return outs[i].at[4 * p[0] + 2 * p[1] + p[2], pl.ds(h * rows, rows)]

        def copy(i, k, src, dst, to):
            return pltpu.make_async_remote_copy(src_ref=src, dst_ref=dst, send_sem=send_sems.at[i, k],
                                                recv_sem=recv_sems.at[i, k], device_id=to,
                                                device_id_type=MESH)

        started = []

        def go(cp):
            cp.start()
            started.append(cp)

        for i in range(n):
            rows = ins[i].shape[0] // 2
            for h in range(2):
                own = ins[i].at[pl.ds(h * rows, rows)]
                go(copy(i, 1 + h, own, piece(i, me, h), (*xn, c)))
                go(copy(i, 3 + h, own, piece(i, me, h), (*yn, c)))
        for i in range(n):
            go(copy(i, 0, ins[i], outs[i].at[4 * x + 2 * y + c], sibling))
        mine = [pltpu.make_async_copy(ins[i], outs[i].at[4 * x + 2 * y + c], local_sems.at[i])
                for i in range(n)]
        for cp in mine:
            cp.start()
        for i in range(n):
            for k, chip, h, onward, ksib in ((1, xn, 0, (5, yn), 7), (4, yn, 1, (6, xn), 10),
                                            (2, xn, 1, None, 8), (3, yn, 0, None, 9),
                                            (5, dg, 0, None, 11), (6, dg, 1, None, 12)):
                got = piece(i, (*chip, c), h)
                copy(i, k, got, got, me).wait_recv()
                if onward is not None:
                    go(copy(i, onward[0], got, got, (*onward[1], c)))
                go(copy(i, ksib, got, got, sibling))
        for i in range(n):
            block = outs[i].at[4 * x + 2 * y + 1 - c]
            copy(i, 0, block, block, me).wait_recv()
            for ksib, chip, h in ((7, xn, 0), (10, yn, 1), (8, xn, 1), (9, yn, 0), (11, dg, 0), (12, dg, 1)):
                got = piece(i, (*chip, 1 - c), h)
                copy(i, ksib, got, got, me).wait_recv()
        for cp in started:
            cp.wait_send()
        for cp in mine:
            cp.wait()

    return body


def _on_sequencer(name, body, arrays, out_sds, sems, collective_id):
    ins = [jax.new_ref(a, memory_space=pltpu.MemorySpace.HBM) for a in arrays]
    outs = [jax.empty_ref(s, memory_space=pltpu.MemorySpace.HBM) for s in out_sds]

    @pl.kernel(mesh=plsc.ScalarSubcoreMesh(axis_name="sequencer", num_cores=1), name=name,
               scratch_types=tuple(sems),
               compiler_params=pltpu.CompilerParams(collective_id=collective_id))
    def launch(*sem_refs):
        body(*ins, *outs, *sem_refs)

    launch()
    return [o[...] for o in outs]


def seq_all_gather(name, shards, collective_id):
    n = len(shards)
    return _on_sequencer(
        name, _routed_gather_body(n), shards, [_sds((NDEV,) + s.shape, s.dtype) for s in shards],
        [pltpu.SemaphoreType.DMA((n, 13)), pltpu.SemaphoreType.DMA((n, 13)),
         pltpu.SemaphoreType.DMA((n,))], collective_id)


def pair_exchange(name, grads, collective_id):
    def plan(srcs, lands):
        x, y, c, _ = _place()
        return ([(i, q, srcs[i].at[2 * q + 1 - c], lands[i].at[q], (x, y, 1 - c))
                 for i in range(len(srcs)) for q in range(NCHIP)], [(x, y, 1 - c)])

    return _split_exchange(name, grads, [_sds((NCHIP,) + g.shape[1:], g.dtype) for g in grads],
                           plan, NCHIP, collective_id)


SEM = pl.BlockSpec(memory_space=pltpu.SEMAPHORE)


def _split_exchange(name, srcs, land_sds, plan, ncopy, collective_id):
    n = len(srcs)
    nsem = n * ncopy
    effect = pltpu.SideEffectType.DATAFLOW_SIDE_EFFECTING

    def descriptors(src_refs, land_refs, send_sems, recv_sems):
        copies, peers = plan(src_refs, land_refs)
        return [pltpu.make_async_remote_copy(src_ref=s, dst_ref=d, send_sem=send_sems[i * ncopy + k],
                                             recv_sem=recv_sems[i * ncopy + k], device_id=to,
                                             device_id_type=MESH) for (i, k, s, d, to) in copies], peers

    def start_body(*refs):
        src_refs, land_refs = refs[:n], refs[n:2 * n]
        send_sems, recv_sems = refs[2 * n:2 * n + nsem], refs[2 * n + nsem:2 * n + 2 * nsem]
        token = refs[-1]
        cps, peers = descriptors(src_refs, land_refs, send_sems, recv_sems)
        barrier = pltpu.get_barrier_semaphore()
        for peer in peers:
            pl.semaphore_signal(barrier, inc=1, device_id=peer, device_id_type=MESH)
        pl.semaphore_wait(barrier, len(peers))
        for cp in cps:
            cp.start()
        token[...] = jnp.zeros_like(token)

    lands = [pltpu.with_memory_space_constraint(lax.empty(s.shape, s.dtype), pltpu.HBM) for s in land_sds]
    srcs = [pltpu.with_memory_space_constraint(s, pltpu.HBM) for s in srcs]
    res = pl.pallas_call(
        start_body, name=name + "_start",
        out_shape=(pltpu.SemaphoreType.DMA(()),) * (2 * nsem)
        + tuple(pltpu.HBM(s.shape, s.dtype) for s in srcs)
        + tuple(pltpu.HBM(s.shape, s.dtype) for s in land_sds) + (_sds((SUBLANES, LANES), F32),),
        in_specs=[HBM] * (2 * n),
        out_specs=(SEM,) * (2 * nsem) + (HBM,) * (2 * n) + (pl.BlockSpec(memory_space=pltpu.VMEM),),
        input_output_aliases={i: 2 * nsem + i for i in range(2 * n)},
        compiler_params=pltpu.CompilerParams(has_side_effects=effect, collective_id=collective_id),
    )(*srcs, *lands)
    sems = res[:2 * nsem]
    thru = res[2 * nsem:2 * nsem + 2 * n]
    token = res[-1]

    def wait(after):
        def wait_body(*refs):
            src_refs, land_refs = refs[:n], refs[n:2 * n]
            cps, _ = descriptors(src_refs, land_refs, refs[2 * n:2 * n + nsem],
                                 refs[2 * n + nsem:2 * n + 2 * nsem])
            for cp in cps:
                cp.wait_send()
            for cp in cps:
                cp.wait_recv()

        out = pl.pallas_call(
            wait_body, name=name + "_wait",
            out_shape=tuple(pltpu.HBM(s.shape, s.dtype) for s in srcs)
            + tuple(pltpu.HBM(s.shape, s.dtype) for s in land_sds),
            in_specs=[HBM] * (2 * n) + [SEM] * (2 * nsem) + [pl.BlockSpec(memory_space=pl.ANY)],
            out_specs=(HBM,) * (2 * n),
            input_output_aliases={i: i for i in range(2 * n)},
            compiler_params=pltpu.CompilerParams(has_side_effects=effect),
        )(*thru, *sems, after)
        return list(out[:n]), list(out[n:])

    return token, wait


def pair_sum(name, grad, got, place):
    shp = grad.shape[1:]
    r, cdim = shp[-2], shp[-1]
    lead = int(math.prod(shp[:-2])) if len(shp) > 2 else 1
    g5 = grad.reshape(NCHIP, 2, lead * r, cdim)
    t4 = got.reshape(NCHIP, lead * r, cdim)
    R = lead * r
    tr = _tile(R, max(8, (1 << 20) // cdim))

    def body(p_ref, g_ref, t_ref, o_ref):
        o_ref[...] = (g_ref[0].astype(F32) + t_ref[...].astype(F32)).astype(o_ref.dtype)

    out = pl.pallas_call(
        body, name=name,
        grid_spec=pltpu.PrefetchScalarGridSpec(
            num_scalar_prefetch=1, grid=(NCHIP - 1, R // tr),
            in_specs=[pl.BlockSpec((1, 1, tr, cdim), lambda j, i, p: (p[1] ^ (j + 1), p[0], i, 0)),
                      pl.BlockSpec((1, tr, cdim), lambda j, i, p: (p[1] ^ (j + 1), i, 0))],
            out_specs=pl.BlockSpec((1, tr, cdim), lambda j, i, p: (p[1] ^ (j + 1), i, 0))),
        out_shape=_sds((NCHIP, R, cdim), grad.dtype),
        compiler_params=_params(("parallel", "parallel")),
    )(place, g5, t4)
    return out


def chip_exchange(name, parts, collective_id):
    def plan(srcs, lands):
        x, y, c, chips = _place()
        return ([(i, j, srcs[i].at[2 * chip[0] + chip[1]], lands[i].at[j], (*chip, c))
                 for i in range(len(srcs)) for j, chip in enumerate(chips)],
                [(*chip, c) for chip in chips])

    return _split_exchange(name, parts, [_sds((3,) + p.shape[1:], p.dtype) for p in parts],
                           plan, 3, collective_id)


def ada_fwd(c_row, w_ada, b_ada):
    D, cols = w_ada.shape

    def body(c_ref, w_ref, b_ref, mod_ref, call_ref, act8, part, s1, r1, s2, r2):
        x, y, c, _ = _place()
        me = 4 * x + 2 * y + c
        call_ref[me] = c_ref[...]
        cps = []
        for k in range(1, NDEV):
            to = (x ^ (k >> 2), y ^ ((k >> 1) & 1), c ^ (k & 1))
            cps.append(pltpu.make_async_remote_copy(
                src_ref=c_ref, dst_ref=call_ref.at[me], send_sem=s1.at[k - 1],
                recv_sem=r1.at[k - 1], device_id=to, device_id_type=MESH))
            cps[-1].start()
        for cp in cps:
            cp.wait()
        for b in range(NDEV):
            act8[b:b + 1, :] = call_ref[b]
        cv = act8[...]
        act = (cv * _sigmoid(cv)).astype(BF16)
        res = jnp.dot(act, w_ref[...].astype(BF16), preferred_element_type=F32)
        for b in range(NDEV):
            part[b] = res[b:b + 1, :]
        mod_ref[me] = part[me]
        cps = []
        for k in range(1, NDEV):
            to = (x ^ (k >> 2), y ^ ((k >> 1) & 1), c ^ (k & 1))
            dst = 4 * to[0] + 2 * to[1] + to[2]
            cps.append(pltpu.make_async_remote_copy(
                src_ref=part.at[dst], dst_ref=mod_ref.at[me], send_sem=s2.at[k - 1],
                recv_sem=r2.at[k - 1], device_id=to, device_id_type=MESH))
            cps[-1].start()
        for cp in cps:
            cp.wait()
        for b in range(NDEV):
            mod_ref[b] = mod_ref[b] + b_ref[b]

    vm = pl.BlockSpec(memory_space=pltpu.VMEM)
    return pl.pallas_call(
        body, name="ada_fwd", in_specs=[vm, vm, vm], out_specs=[vm, vm],
        out_shape=[_sds((NDEV, 1, cols), F32), _sds((NDEV, 1, D), F32)],
        scratch_shapes=[pltpu.VMEM((NDEV, D), F32), pltpu.VMEM((NDEV, 1, cols), F32),
                        pltpu.SemaphoreType.DMA((NDEV - 1,)), pltpu.SemaphoreType.DMA((NDEV - 1,)),
                        pltpu.SemaphoreType.DMA((NDEV - 1,)), pltpu.SemaphoreType.DMA((NDEV - 1,))],
        compiler_params=pltpu.CompilerParams(vmem_limit_bytes=VMEM_LIMIT),
    )(c_row, w_ada, b_ada.reshape(NDEV, 1, cols))


def _adamw_math(g, w, m, v):
    m2 = ADAM_B1 * m + (1.0 - ADAM_B1) * g
    v2 = ADAM_B2 * v + (1.0 - ADAM_B2) * (g * g)
    m_hat = m2 / (1.0 - ADAM_B1 ** ADAM_STEP)
    v_hat = v2 / (1.0 - ADAM_B2 ** ADAM_STEP)
    delta = -ADAM_LR * (m_hat / (jnp.sqrt(v_hat) + ADAM_EPS) + ADAM_WD * w)
    return delta, m2, v2


def adamw_sharded(name, grad8, pair4, got3, w, m, v, place, after=()):
    shape = w.shape
    cdim = shape[-1]
    R = int(math.prod(shape[:-1]))
    w2, m2, v2 = (t.reshape(R, cdim) for t in (w, m, v))
    tr = _tile(R, max(8, (1 << 19) // cdim))

    def body(q_ref, own_ref, sib_ref, t_ref, w_ref, m_ref, v_ref, g_out, d_out, m_out, v_out):
        g = own_ref[0].astype(F32) + sib_ref[0].astype(F32)
        for j in range(3):
            g = g + t_ref[j].astype(F32)
        d, mn, vn = _adamw_math(g, w_ref[...], m_ref[...], v_ref[...])
        g_out[...] = g
        d_out[...] = d
        m_out[...] = mn
        v_out[...] = vn

    spec = pl.BlockSpec((tr, cdim), lambda i, qr: (i, 0))
    outs = pl.pallas_call(
        _with_after(body, 7, after), name=name,
        grid_spec=pltpu.PrefetchScalarGridSpec(
            num_scalar_prefetch=1, grid=(R // tr,),
            in_specs=[pl.BlockSpec((1, tr, cdim), lambda i, qr: (qr[2], i, 0)),
                      pl.BlockSpec((1, tr, cdim), lambda i, qr: (qr[1], i, 0)),
                      pl.BlockSpec((3, tr, cdim), lambda i, qr: (0, i, 0)), spec, spec, spec]
            + [ANY] * len(after),
            out_specs=[spec] * 4),
        out_shape=[_sds((R, cdim), F32)] * 4,
        compiler_params=_params(("parallel",)),
    )(place, grad8.reshape(NDEV, R, cdim), pair4.reshape(NCHIP, R, cdim),
      got3.reshape(3, R, cdim), w2, m2, v2, *after)
    return [o.reshape(shape) for o in outs]


def sum_small(parts, after=()):
    R = parts.shape[1]

    def body(p_ref, g_out):
        g = p_ref[0]
        for j in range(1, NDEV):
            g = g + p_ref[j]
        g_out[...] = g

    return pl.pallas_call(
        _with_after(body, 1, after), name="sum_small", grid=(1,),
        in_specs=[pl.BlockSpec((NDEV, R, LANES), lambda i: (0, 0, 0))] + [ANY] * len(after),
        out_specs=pl.BlockSpec((R, LANES), lambda i: (0, 0)), out_shape=_sds((R, LANES), F32),
        compiler_params=_params(("arbitrary",)),
    )(parts, *after)


def adamw_natural(gs, ws, ms, vs):
    n = len(ws)
    nblk = 8
    big = [w.ndim == 4 and w.shape[1] % nblk == 0 for w in ws]

    def spec(w, is_big):
        if is_big:
            return pl.BlockSpec((1, w.shape[1] // nblk) + w.shape[2:], lambda i: (0, i, 0, 0))
        return pl.BlockSpec(w.shape, functools.partial(lambda i, nd: (0,) * nd, nd=w.ndim))

    def body(*refs):
        g_refs, w_refs, m_refs, v_refs = (refs[k * n:(k + 1) * n] for k in range(4))
        d_outs, m_outs, v_outs = (refs[(4 + k) * n:(5 + k) * n] for k in range(3))

        def update(p):
            d, mn, vn = _adamw_math(g_refs[p][...], w_refs[p][...], m_refs[p][...], v_refs[p][...])
            d_outs[p][...] = d
            m_outs[p][...] = mn
            v_outs[p][...] = vn

        for p in range(n):
            if big[p]:
                update(p)

        @pl.when(pl.program_id(0) == 0)
        def _():
            for p in range(n):
                if not big[p]:
                    update(p)

    specs = [spec(w, b) for w, b in zip(ws, big)]
    outs = pl.pallas_call(
        body, name="adamw_natural", grid=(nblk,), in_specs=specs * 4, out_specs=specs * 3,
        out_shape=[_sds(w.shape, F32) for w in ws] * 3,
        compiler_params=_params(("arbitrary",)),
    )(*gs, *ws, *ms, *vs)
    return outs[:n], outs[n:2 * n], outs[2 * n:]


def adamw_ada(c_all_t, dmod_all, w, m, v, my_dev):
    D, cols = w.shape
    tr = _tile(D, 256)

    def body(k_ref, c_ref, d_ref, w_ref, m_ref, v_ref, g_out, d_out, m_out, v_out):
        cv = c_ref[...]
        act = cv * _sigmoid(cv)
        dm = d_ref[...]
        g = act[:, 0:1] * dm[0:1, :]
        for b in range(1, NDEV):
            g = g + act[:, b:b + 1] * dm[b:b + 1, :]
        d, mn, vn = _adamw_math(g, w_ref[...], m_ref[...], v_ref[...])
        g_out[...] = g
        d_out[...] = d
        m_out[...] = mn
        v_out[...] = vn

    spec = pl.BlockSpec((tr, cols), lambda i, kr: (i, 0))
    return pl.pallas_call(
        body, name="adamw_ada",
        grid_spec=pltpu.PrefetchScalarGridSpec(
            num_scalar_prefetch=1, grid=(D // tr,),
            in_specs=[pl.BlockSpec((tr, NDEV), lambda i, kr: (i, 0)),
                      pl.BlockSpec((NDEV, cols), lambda i, kr: (0, kr[0])), spec, spec, spec],
            out_specs=[spec] * 4),
        out_shape=[_sds((D, cols), F32)] * 4,
        compiler_params=_params(("parallel",)),
    )(my_dev, c_all_t, dmod_all, w, m, v)


def _small_pack(parts):
    rows = []
    for p in parts:
        flat = p.reshape(-1)
        flat = jnp.pad(flat, (0, (-flat.shape[0]) % (SUBLANES * LANES)))
        rows.append(flat.reshape(-1, LANES))
    return jnp.concatenate(rows, axis=0)


def _small_unpack(buf, shapes):
    out, r = [], 0
    for s in shapes:
        n = int(math.prod(s))
        nr = -(-n // (SUBLANES * LANES)) * SUBLANES
        out.append(buf[r:r + nr].reshape(-1)[:n].reshape(s))
        r += nr
    return out


def kernel(x, c, w_ada, b_ada, w_in, lam_re, lam_im, log_dt, ssm_b_re, ssm_b_im, ssm_c_re, ssm_c_im, ssm_d, w_glu_val, w_glu_gate, w_pool, pool_scale, w_pool_out, w_out, ln1_g, ln1_b, w_ff1, w_ff2, ln2_g, ln2_b, loss_target, m_w_ada, m_b_ada, m_w_in, m_lam_re, m_lam_im, m_log_dt, m_ssm_b_re, m_ssm_b_im, m_ssm_c_re, m_ssm_c_im, m_ssm_d, m_w_glu_val, m_w_glu_gate, m_w_pool, m_pool_scale, m_w_pool_out, m_w_out, m_ln1_g, m_ln1_b, m_w_ff1, m_w_ff2, m_ln2_g, m_ln2_b, v_w_ada, v_b_ada, v_w_in, v_lam_re, v_lam_im, v_log_dt, v_ssm_b_re, v_ssm_b_im, v_ssm_c_re, v_ssm_c_im, v_ssm_d, v_w_glu_val, v_w_glu_gate, v_w_pool, v_pool_scale, v_w_pool_out, v_w_out, v_ln1_g, v_ln1_b, v_w_ff1, v_w_ff2, v_ln2_g, v_ln2_b):
    S, D = x.shape[1], x.shape[2]
    x2d, tgt = x[0], loss_target[0]
    W = D // 2
    G = W // SSM_GROUP
    P, H, GPB = SSM_STATE, SSM_GROUP, GROUPS_PER_BLOCK
    nblk = G // GPB
    gw = W // len(POOL_WINDOWS)
    ax, ay, ac = lax.axis_index("x"), lax.axis_index("y"), lax.axis_index("c")
    my_dev = (4 * ax + 2 * ay + ac).astype(jnp.int32).reshape(1)
    place = jnp.stack([ac, 2 * ax + ay, 4 * ax + 2 * ay + ac]).astype(jnp.int32)
    ts = _tile(S, 256)

    glu = jnp.stack([w_glu_val[0], w_glu_gate[0]]).astype(BF16)
    shards = [w_in[0].astype(BF16), glu, w_pool[0].astype(BF16), w_pool_out[0].astype(BF16),
              w_out[0].astype(BF16), w_ff1[0].astype(BF16), w_ff2[0].astype(BF16)]
    wg_in, wg_pool = seq_all_gather("gather_w_in", [shards[0], shards[2]], 1)
    wg_vg, wg_po, wg_out = seq_all_gather("gather_w_mix", [shards[1], shards[3], shards[4]], 2)
    (wg_ff1,) = seq_all_gather("gather_w_ff1", shards[5:6], 3)
    (wg_ff2,) = seq_all_gather("gather_w_ff2", shards[6:7], 11)
    wg_vg = wg_vg.reshape(2 * NDEV, W, D // NDEV)
    nwin = len(POOL_WINDOWS)
    wp_full = jnp.transpose(wg_pool, (1, 0, 2, 3)).reshape(nwin, gw, gw)
    wout_full = wg_out.reshape(1, D, D)
    wff2_full = wg_ff2.reshape(1, 4 * D, D)

    small_names = [b_ada, lam_re, lam_im, log_dt, ssm_b_re, ssm_b_im, ssm_c_re, ssm_c_im, ssm_d,
                   pool_scale, ln1_g, ln1_b, ln2_g, ln2_b]
    small_m = [m_b_ada, m_lam_re, m_lam_im, m_log_dt, m_ssm_b_re, m_ssm_b_im, m_ssm_c_re, m_ssm_c_im,
               m_ssm_d, m_pool_scale, m_ln1_g, m_ln1_b, m_ln2_g, m_ln2_b]
    small_v = [v_b_ada, v_lam_re, v_lam_im, v_log_dt, v_ssm_b_re, v_ssm_b_im, v_ssm_c_re, v_ssm_c_im,
               v_ssm_d, v_pool_scale, v_ln1_g, v_ln1_b, v_ln2_g, v_ln2_b]

    mod, c_all = ada_fwd(c, w_ada[0], b_ada)
    mod = mod.reshape(6, 1, D)
    sh1, sc1, g1, sh2, sc2, g2 = (mod[i] for i in range(6))

    f2, kconst = s5_disc(lam_re[0], lam_im[0], log_dt[0].reshape(G, 1))
    kconst = kconst.reshape(NCONST, SUBLANES, G * P)
    f2r = f2.reshape(2, 1, G * P)
    bt_re = jnp.transpose(ssm_b_re[0], (2, 0, 1)).reshape(H, G * P)
    bt_im = jnp.transpose(ssm_b_im[0], (2, 0, 1)).reshape(H, G * P)
    ct_re = jnp.transpose(ssm_c_re[0], (1, 0, 2)).reshape(H, G * P)
    ct_im = jnp.transpose(ssm_c_im[0], (1, 0, 2)).reshape(H, G * P)
    s5_params = (f2r, bt_re, bt_im, ct_re, ct_im, ssm_d, kconst)

    def e1(t, b):
        xhat, _ = _ln_stats(t[0])
        return [xhat * (1.0 + b[0]) + b[1]], []
    (h1,) = _rowwise("ln_mod1", e1, S, ts, [(x2d, D, 0)], [sc1, sh1], [(D, BF16)], [])

    (proj,) = mm_nn("proj", h1, wg_in, F32, 2)
    z, xsb_all = s5_fwd(proj, s5_params, nblk)
    (vt,) = mm_nn("glu", z, wg_vg, BF16, 4)
    pooled = pool_fwd(proj, W, W, gw)

    def pool_epi(vals, ex, outs):
        a = vals[0]
        outs[0][...] = a
        outs[1][...] = (a * ex[0][...]).astype(BF16)
    tmp = _tile(S, 1024)
    yp, ypool = _mm(
        "pool_mix", "nn", pooled, wp_full.astype(BF16), (S // tmp, nwin, 1),
        pl.BlockSpec((tmp, gw), lambda i, j, k: (i, j)), pl.BlockSpec((1, gw, gw), lambda i, j, k: (j, 0, 0)),
        [(_sds((S, W), F32), pl.BlockSpec((tmp, gw), lambda i, j, k: (i, j))),
         (_sds((S, W), BF16), pl.BlockSpec((tmp, gw), lambda i, j, k: (i, j)))],
        (tmp, gw), 1, gw, None, pool_epi,
        [(pool_scale, pl.BlockSpec((1, gw), lambda i, j, k: (0, j)))])
    (y_b,) = mm_nn("pool_out", ypool, wg_po, BF16, 4)

    cb = D // NDEV
    ga_cb, gb_cb = (2 * W) // cb, (2 * W + D) // cb
    mcb = 4
    wm = mcb * cb
    tsm = _tile(S, 256)

    def merge_call(name, fn, ins, n_out, after=()):
        def body(*refs):
            vals = [r[...].astype(F32) for r in refs[:len(ins)]]
            for r, v in zip(refs[len(ins):], fn(*vals)):
                r[...] = v.astype(r.dtype)
        return pl.pallas_call(
            _with_after(body, len(ins), after), name=name, grid=(S // tsm, NDEV // mcb),
            in_specs=[pl.BlockSpec((tsm, w), f) for (_, w, f) in ins] + [ANY] * len(after),
            out_specs=[pl.BlockSpec((tsm, w), lambda i, j: (i, j)) for (_, w) in n_out],
            out_shape=[_sds((S, cols), BF16) for (cols, _) in n_out],
            compiler_params=_params(("parallel", "parallel")),
        )(*[a for (a, _, _) in ins], *after)

    merge_ins = [(proj, wm, lambda i, j: (i, ga_cb // mcb + j)), (proj, wm, lambda i, j: (i, gb_cb // mcb + j)),
                 (vt, 2 * wm, lambda i, j: (i, j)), (y_b, wm, lambda i, j: (i, j))]

    def val_gate(vtv):
        return (jnp.concatenate([vtv[:, 2 * q * cb:(2 * q + 1) * cb] for q in range(mcb)], axis=1),
                jnp.concatenate([vtv[:, (2 * q + 1) * cb:(2 * q + 2) * cb] for q in range(mcb)], axis=1))

    def merge_f(ga, gb, vtv, yb):
        vv, tt = val_gate(vtv)
        return [_sigmoid(ga) * (vv * _sigmoid(tt)) + _sigmoid(gb) * yb]
    (merged,) = merge_call("merge", merge_f, merge_ins, [(D, wm)])

    (mix,) = mm_nn("mix_out", merged, wout_full, F32, 1)

    def e3(t, b):
        xv, mx = t
        g1v, l1g, l1b, sc2v, sh2v = b
        r1 = ALPHA * xv + g1v * mx
        xh1, _ = _ln_stats(r1)
        x1 = xh1 * l1g + l1b
        xh, _ = _ln_stats(x1)
        return [r1, xh * (1.0 + sc2v) + sh2v], []
    r1, h2 = _rowwise("post_mix", e3, S, ts, [(x2d, D, 0), (mix, D, 0)],
                      [g1, ln1_g, ln1_b, sc2, sh2], [(D, F32), (D, BF16)], [])

    def relu_epi(vals, ex, outs):
        outs[0][...] = jnp.maximum(vals[0], 0.0).astype(BF16)
    (rl,) = mm_nn("ff1", h2, wg_ff1, BF16, 1, epi=relu_epi)

    def square(a):
        return a * a
    (y2,) = mm_nn("ff2", rl, wff2_full, F32, 1, pro=square)

    def e4(t, b):
        r1v, y2v, tg = t
        g2v, l1g, l1b, l2g, l2b = b
        xh1, _ = _ln_stats(r1v)
        x1 = xh1 * l1g + l1b
        r2 = ALPHA * x1 + g2v * y2v
        xh2, rs2 = _ln_stats(r2)
        err = xh2 * l2g + l2b - tg
        dx2 = err * (1.0 / D)
        dr2 = _ln_bwd(dx2 * l2g, xh2, rs2)
        lsum = jnp.sum(_colsum(err * err), axis=1, keepdims=True) * (0.5 / D)
        return ([ALPHA * dr2, g2v * dr2],
                [jnp.broadcast_to(lsum, (1, LANES)), _colsum(dx2 * xh2), _colsum(dx2), _colsum(dr2 * y2v)])
    dx1a, dy2, loss_acc, g_ln2g, g_ln2b, d_g2 = _rowwise(
        "head", e4, S, ts, [(r1, D, 0), (y2, D, 0), (tgt, D, 0)], [g2, ln1_g, ln1_b, ln2_g, ln2_b],
        [(D, F32), (D, BF16)], [LANES, D, D, D])

    tn_ff = _tile(4 * D, 1024)

    def dff_epi(vals, ex, outs):
        outs[0][...] = (vals[0] * (2.0 * ex[0][...].astype(F32))).astype(BF16)
    tmf = _tile(S, 1024)
    (da1,) = mm_nt("d_ff2", dy2, wff2_full, BF16, 1, tn=tn_ff, epi=dff_epi,
                   extras=[(rl, pl.BlockSpec((tmf, tn_ff), lambda i, j, k: (i, j)))])
    gw_ff2 = mm_tn("gw_ff2", rl, dy2, BF16, NDEV, 0, pro=square)
    gw_ff1 = mm_tn("gw_ff1", h2, da1, BF16, NDEV, 1)
    tok, wait_pair_a = pair_exchange("pair_exchange_ff", [gw_ff2, gw_ff1], 4)
    (dh2,) = mm_nt("d_ff1", da1, wg_ff1, F32, 4, after=[tok])

    def e5(t, b):
        dh2v, r1v, dx1av, mx = t
        sc2v, l1g, l1b, g1v = b
        xh1, rs1 = _ln_stats(r1v)
        x1 = xh1 * l1g + l1b
        xh, rs = _ln_stats(x1)
        dx1 = dx1av + _ln_bwd(dh2v * (1.0 + sc2v), xh, rs)
        dr1 = _ln_bwd(dx1 * l1g, xh1, rs1)
        return ([ALPHA * dr1, g1v * dr1],
                [_colsum(dh2v * xh), _colsum(dh2v), _colsum(dx1 * xh1), _colsum(dx1), _colsum(dr1 * mx)])
    dxa, dmix, d_sc2, d_sh2, g_ln1g, g_ln1b, d_g1 = _rowwise(
        "post_mix_bwd", e5, S, ts, [(dh2, D, 0), (r1, D, 0), (dx1a, D, 0), (mix, D, 0)],
        [sc2, ln1_g, ln1_b, g1], [(D, F32), (D, BF16)], [D, D, D, D, D])

    (dmerged,) = mm_nt("d_mix_out", dmix, wout_full, BF16, 1)
    gw_out = mm_tn("gw_out", merged, dmix, BF16, NDEV, 0)
    grads_a, got_a = wait_pair_a(gw_out)
    parts_a = [pair_sum("pair_sum_ff%d" % i, g, t, place) for i, (g, t) in enumerate(zip(grads_a, got_a))]
    tok, wait_chip_a = chip_exchange("chip_exchange_ff", parts_a, 5)

    def merge_b(ga, gb, vtv, yb, dm):
        vv, tt = val_gate(vtv)
        sa, sb, st = _sigmoid(ga), _sigmoid(gb), _sigmoid(tt)
        dya = dm * sa
        dv, dt = dya * st, dya * vv * st * (1.0 - st)
        dvt_tile = jnp.concatenate([t[:, q * cb:(q + 1) * cb] for q in range(mcb) for t in (dv, dt)], axis=1)
        return [dm * (vv * st) * sa * (1.0 - sa), dm * yb * sb * (1.0 - sb), dvt_tile, dm * sb]
    dga, dgb_, dvt, dy_b = merge_call(
        "merge_bwd", merge_b, merge_ins + [(dmerged, wm, lambda i, j: (i, j))],
        [(D, wm), (D, wm), (2 * D, 2 * wm), (D, wm)], after=[tok])

    (dypool,) = mm_nt("d_pool_out", dy_b, wg_po, F32, NDEV)
    gw_po = mm_tn("gw_pool_out", ypool, dy_b, BF16, NDEV, 4)

    def e7(t, b):
        return [t[0] * b[0]], [_colsum(t[0] * t[1])]
    dyp, g_pscale = _rowwise("pool_scale_bwd", e7, S, ts, [(dypool, W, 0), (yp, W, 0)],
                             [pool_scale], [(W, BF16)], [W])
    (dpooled,) = _mm(
        "d_pool_mix", "nt", dyp, wp_full.astype(BF16), (S // tmp, nwin, 1),
        pl.BlockSpec((tmp, gw), lambda i, j, k: (i, j)), pl.BlockSpec((1, gw, gw), lambda i, j, k: (j, 0, 0)),
        [(_sds((S, W), F32), pl.BlockSpec((tmp, gw), lambda i, j, k: (i, j)))], (tmp, gw), 1, gw)
    tkp = _tile(S, 2048)
    gw_pool = _mm(
        "gw_pool", "tn", pooled, dyp, (nwin, 1, S // tkp),
        pl.BlockSpec((tkp, gw), lambda i, j, k: (k, i)), pl.BlockSpec((tkp, gw), lambda i, j, k: (k, i)),
        [(_sds((nwin, gw, gw), BF16), pl.BlockSpec((1, gw, gw), lambda i, j, k: (i, 0, 0)))],
        (gw, gw), 1, gw, stacked_out=True)[0]
    du_pool = pool_bwd(dpooled, gw)

    (dz,) = mm_nt("d_glu", dvt, wg_vg, BF16, 2 * NDEV)
    gw_vg = mm_tn("gw_glu", z, dvt, BF16, 2 * NDEV, 4)
    gw_pool_st = jnp.transpose(gw_pool.reshape(nwin, NDEV, gw // NDEV, gw), (1, 0, 2, 3))
    grads_b = [gw_out, gw_po, gw_pool_st, gw_vg.reshape(NDEV, 2, W, D // NDEV)]
    tok, wait_pair_b = pair_exchange("pair_exchange_mix", grads_b, 6)
    du_ssm, g_bt_re, g_bt_im, g_ct_re, g_ct_im, g_f, g_d, g_a = s5_bwd(
        proj, xsb_all, dz, s5_params, nblk, after=[tok])
    grads_b, got_b = wait_pair_b(du_ssm)
    parts_b = [pair_sum("pair_sum_mix%d" % i, g, t, place) for i, (g, t) in enumerate(zip(grads_b, got_b))]
    tok, wait_chip_b = chip_exchange("chip_exchange_mix", parts_b, 7)

    dproj = jnp.concatenate([du_ssm, du_pool, dga, dgb_], axis=1)
    gw_in = mm_tn("gw_in", h1, dproj, BF16, NDEV, 1, after=[tok])
    tok, wait_pair_c = pair_exchange("pair_exchange_in", [gw_in], 8)
    (dh1,) = mm_nt("d_proj", dproj, wg_in, F32, 4, after=[tok])
    grads_c, got_c = wait_pair_c(dh1)
    parts_c = [pair_sum("pair_sum_in", grads_c[0], got_c[0], place)]
    tok, wait_chip_c = chip_exchange("chip_exchange_in", parts_c, 9)

    def e10(t, b):
        dh1v, xv, dxav = t
        xh, rs = _ln_stats(xv)
        return ([dxav + _ln_bwd(dh1v * (1.0 + b[0]), xh, rs)],
                [_colsum(dh1v * xh), _colsum(dh1v)])
    grad_x, d_sc1, d_sh1 = _rowwise("ln_mod1_bwd", e10, S, ts, [(dh1, D, 0), (x2d, D, 0), (dxa, D, 0)],
                                    [sc1], [(D, F32)], [D, D], after=[tok])

    g_b_re = jnp.transpose(g_bt_re.reshape(H, G, P), (1, 2, 0))
    g_b_im = jnp.transpose(g_bt_im.reshape(H, G, P), (1, 2, 0))
    g_c_re = jnp.transpose(g_ct_re.reshape(H, G, P), (1, 0, 2))
    g_c_im = jnp.transpose(g_ct_im.reshape(H, G, P), (1, 0, 2))
    d_ab = jnp.transpose(g_a.reshape(nblk, 2, GPB, P), (1, 0, 2, 3)).reshape(2, G, P)
    g_lr, g_li, g_ldt = s5_disc_bwd(lam_re[0], lam_im[0], log_dt[0].reshape(G, 1), d_ab,
                                    g_f.reshape(2, G, P))

    dmod = jnp.concatenate([d_sh1, d_sc1, d_g1, d_sh2, d_sc2, d_g2], axis=1)
    small_g = [dmod, g_lr, g_li, g_ldt, g_b_re, g_b_im, g_c_re, g_c_im, g_d, g_pscale,
               g_ln1g, g_ln1b, g_ln2g, g_ln2b, loss_acc]
    packed_g = _small_pack(small_g)
    (parts_all,) = seq_all_gather("gather_small", [packed_g], 10)
    glu_w = jnp.stack([w_glu_val[0], w_glu_gate[0]])
    glu_m = jnp.stack([m_w_glu_val[0], m_w_glu_gate[0]])
    glu_v = jnp.stack([v_w_glu_val[0], v_w_glu_gate[0]])
    wmv = [(w_ff2[0], m_w_ff2[0], v_w_ff2[0]), (w_ff1[0], m_w_ff1[0], v_w_ff1[0]),
           (w_out[0], m_w_out[0], v_w_out[0]), (w_pool_out[0], m_w_pool_out[0], v_w_pool_out[0]),
           (w_pool[0], m_w_pool[0], v_w_pool[0]), (glu_w, glu_m, glu_v)]
    _, got3_a = wait_chip_a(packed_g)
    upd = [adamw_sharded("adamw_%d" % i, g, p, t, w, m, v, place)
           for i, (g, p, t, (w, m, v)) in enumerate(zip(grads_a, got_a, got3_a, wmv[:2]))]
    _, got3_b = wait_chip_b(upd[-1][0])
    upd += [adamw_sharded("adamw_%d" % (2 + i), g, p, t, w, m, v, place)
            for i, (g, p, t, (w, m, v)) in enumerate(zip(grads_b, got_b, got3_b, wmv[2:]))]
    u_ff2, u_ff1, u_out, u_po, u_pool, u_glu = upd

    gsum = sum_small(parts_all, after=[upd[-1][0]])
    sg = _small_unpack(gsum, [t.shape for t in small_names] + [(1, LANES)])
    loss, sg = sg[-1][0, 0], sg[:-1]
    sd, sm, sv = adamw_natural(sg, small_names, small_m, small_v)

    nmod = 6 * D
    dmod_all = parts_all[:, :nmod // LANES, :].reshape(NDEV, nmod)
    c_all_t = jnp.transpose(c_all.reshape(NDEV, D))
    ada_out = adamw_ada(c_all_t, dmod_all, w_ada[0], m_w_ada[0], v_w_ada[0], my_dev)
    _, got3_c = wait_chip_c(ada_out[0])
    u_in = adamw_sharded("adamw_6", grads_c[0], got_c[0], got3_c[0], w_in[0], m_w_in[0], v_w_in[0], place)

    def pick(k):
        return [ada_out[k][None], sg_sd[k][0], u_in[k][None]] + [t for t in sg_sd[k][1:9]] + \
               [u_glu[k][0][None], u_glu[k][1][None], u_pool[k][None], sg_sd[k][9], u_po[k][None],
                u_out[k][None], sg_sd[k][10], sg_sd[k][11], u_ff1[k][None], u_ff2[k][None],
                sg_sd[k][12], sg_sd[k][13]]

    sg_sd = [sg, sd, sm, sv]
    return (loss, grad_x[None], *pick(0), *pick(1), *pick(2), *pick(3))
```

```python
import functools
import math

import jax
import jax.numpy as jnp
from jax import lax
from jax.experimental import pallas as pl
from jax.experimental.pallas import tpu as pltpu
from jax.experimental.pallas import tpu_sc as plsc

F32 = jnp.float32
BF16 = jnp.bfloat16
MESH = pl.DeviceIdType.MESH
NDEV = 8
NCHIP = 4

SSM_GROUP = 16
SSM_STATE = 64
GROUPS_PER_BLOCK = 8
POOL_WINDOWS = (2, 4, 8, 16)
LN_EPS = 1e-5
ALPHA = 2.0 ** 0.25
ADAM_LR, ADAM_B1, ADAM_B2, ADAM_EPS, ADAM_WD, ADAM_STEP = 0.001, 0.9, 0.999, 1e-08, 0.01, 10
SUBLANES = 8
LANES = 128
VMEM_LIMIT = 56 * 1024 * 1024


def _params(sem=None, vmem=VMEM_LIMIT):
    return pltpu.CompilerParams(dimension_semantics=sem, vmem_limit_bytes=vmem)


def _tile(n, pref):
    if n <= pref:
        return n
    t = 1 << (pref.bit_length() - 1)
    while n % t:
        t //= 2
    return t


def _cast_epi(vals, ex, outs):
    c = vals[0].shape[1]
    for s, v in enumerate(vals):
        outs[0][:, s * c:(s + 1) * c] = v.astype(outs[0].dtype)


ANY = pl.BlockSpec(memory_space=pl.ANY)


def _with_after(body, n_in, after):
    if not after:
        return body
    n_af = len(after)

    def wrapped(*refs):
        return body(*refs[:n_in], *refs[n_in + n_af:])
    return wrapped


def _mm(name, kind, a, b, grid, a_spec, b_spec, outs, acc_shape, nsub=1, c=None,
        pro=None, epi=None, extras=(), stacked_out=False, after=()):
    nk = grid[2]
    n_ex, n_out = len(extras), len(outs)

    def finish(vals, ex, out_refs):
        if epi is not None:
            epi(vals, ex, out_refs)
        elif stacked_out:
            for s, v in enumerate(vals):
                out_refs[0][s] = v.astype(out_refs[0].dtype)
        else:
            _cast_epi(vals, ex, out_refs)

    def body(*refs):
        mm_step(refs[0], refs[1], refs[2:2 + n_ex], refs[2 + n_ex:2 + n_ex + n_out], refs[-1])

    def mm_step(a_ref, b_ref, ex, out_refs, acc):
        k = pl.program_id(2)
        av = a_ref[...]
        if pro is not None:
            av = pro(av)
        if kind == "nn":
            prods = [jnp.dot(av, b_ref[s], preferred_element_type=F32) for s in range(nsub)]
        elif kind == "nt":
            t = None
            for s in range(nsub):
                d = lax.dot_general(av[:, s * c:(s + 1) * c], b_ref[s], (((1,), (1,)), ((), ())),
                                    preferred_element_type=F32)
                t = d if t is None else t + d
            prods = [t]
        else:
            t = lax.dot_general(av, b_ref[...], (((0,), (0,)), ((), ())), preferred_element_type=F32)
            prods = [t[:, s * c:(s + 1) * c] for s in range(nsub)] if stacked_out else [t]
        if nk == 1:
            finish(prods, ex, out_refs)
            return
        w = prods[0].shape[1]

        @pl.when(k == 0)
        def _():
            for s, p in enumerate(prods):
                acc[:, s * w:(s + 1) * w] = p

        @pl.when(jnp.logical_and(k > 0, k < nk - 1))
        def _():
            for s, p in enumerate(prods):
                acc[:, s * w:(s + 1) * w] += p

        @pl.when(k == nk - 1)
        def _():
            finish([acc[:, s * w:(s + 1) * w] + p for s, p in enumerate(prods)], ex, out_refs)

    return pl.pallas_call(
        _with_after(body, 2 + n_ex, after), name=name, grid=grid,
        in_specs=[a_spec, b_spec] + [e[1] for e in extras] + [ANY] * len(after),
        out_specs=[o[1] for o in outs],
        out_shape=[o[0] for o in outs],
        scratch_shapes=[pltpu.VMEM(acc_shape, F32)] if nk > 1 else [],
        compiler_params=_params(("parallel", "parallel", "arbitrary")),
    )(a, b, *[e[0] for e in extras], *after)


def _sds(shape, dtype):
    return jax.ShapeDtypeStruct(shape, dtype)


def mm_nn(name, a, b3, out_dtype, nsub, tm=1024, tk=2048, tn=None, pro=None, epi=None,
          extras=(), extra_outs=(), a_col0=0, after=()):
    M = a.shape[0]
    nb, K, cdim = b3.shape
    tm, tk = _tile(M, tm), _tile(K, tk)
    if nb == 1:
        tn = _tile(cdim, tn or 1024)
        nsub, c, nj = 1, tn, cdim // tn
        b_spec = pl.BlockSpec((1, tk, tn), lambda i, j, k: (0, k, j))
        N = cdim
    else:
        c, nj, tn = cdim, nb // nsub, nsub * cdim
        b_spec = pl.BlockSpec((nsub, tk, cdim), lambda i, j, k: (j, k, 0))
        N = nb * cdim
    kb0 = a_col0 // tk
    a_spec = pl.BlockSpec((tm, tk), lambda i, j, k: (i, kb0 + k))
    grid = (M // tm, nj, K // tk)
    o_spec = pl.BlockSpec((tm, tn), lambda i, j, k: (i, j))
    outs = [(_sds((M, N), out_dtype), o_spec)] + [(_sds((M, N), d), o_spec) for d in extra_outs]
    return _mm(name, "nn", a, b3, grid, a_spec, b_spec, outs, (tm, tn), nsub, c, pro, epi, extras,
               after=after)


def mm_nt(name, a, b3, out_dtype, nsub, tm=1024, tn=1024, epi=None, extras=(), extra_outs=(),
          after=()):
    M = a.shape[0]
    nb, N, cdim = b3.shape
    tm, tn = _tile(M, tm), _tile(N, tn)
    if nb == 1:
        tk = _tile(cdim, 2048)
        nsub, c, nk = 1, tk, cdim // tk
        b_spec = pl.BlockSpec((1, tn, tk), lambda i, j, k: (0, j, k))
    else:
        c, nk, tk = cdim, nb // nsub, nsub * cdim
        b_spec = pl.BlockSpec((nsub, tn, cdim), lambda i, j, k: (k, j, 0))
    a_spec = pl.BlockSpec((tm, tk), lambda i, j, k: (i, k))
    grid = (M // tm, N // tn, nk)
    o_spec = pl.BlockSpec((tm, tn), lambda i, j, k: (i, j))
    outs = [(_sds((M, N), out_dtype), o_spec)] + [(_sds((M, N), d), o_spec) for d in extra_outs]
    return _mm(name, "nt", a, b3, grid, a_spec, b_spec, outs, (tm, tn), nsub, c, None, epi, extras,
               after=after)


def mm_tn(name, a, b, out_dtype, nb, nsub, tma=1024, tk=2048, pro=None, a_col0=0, a_cols=None,
          after=()):
    S = a.shape[0]
    Ka = a_cols or a.shape[1]
    N = b.shape[1]
    tk = _tile(S, tk)
    if nsub == 0:
        tma, tn = _tile(Ka, tma), _tile(N, 1024)
        grid = (Ka // tma, N // tn, S // tk)
        ab0 = a_col0 // tma
        res = _mm(name, "tn", a, b, grid, pl.BlockSpec((tk, tma), lambda i, j, k: (k, ab0 + i)),
                  pl.BlockSpec((tk, tn), lambda i, j, k: (k, j)),
                  [(_sds((Ka, N), out_dtype), pl.BlockSpec((tma, tn), lambda i, j, k: (i, j)))],
                  (tma, tn), 1, tn, pro, None, (), after=after)[0]
        return res.reshape(nb, Ka // nb, N)
    else:
        c = N // nb
        tma = _tile(Ka, tma)
        grid = (Ka // tma, nb // nsub, S // tk)
        o_spec = pl.BlockSpec((nsub, tma, c), lambda i, j, k: (j, i, 0))
        out = _sds((nb, Ka, c), out_dtype)
        nsub_k = nsub
        tn = nsub * c
        b_spec = pl.BlockSpec((tk, tn), lambda i, j, k: (k, j))
    ab0 = a_col0 // tma
    a_spec = pl.BlockSpec((tk, tma), lambda i, j, k: (k, ab0 + i))
    return _mm(name, "tn", a, b, grid, a_spec, b_spec, [(out, o_spec)], (tma, tn), nsub_k, c,
               pro, None, (), stacked_out=True, after=after)[0]


def _rowwise(name, fn, S, ts, tiled, bcast, tiled_out, acc_out, after=()):
    nt, nb, no, na = len(tiled), len(bcast), len(tiled_out), len(acc_out)

    def body(*refs):
        tin = [r[...] for r in refs[:nt]]
        bin_ = [r[...] for r in refs[nt:nt + nb]]
        o_refs = refs[nt + nb:nt + nb + no]
        a_refs = refs[nt + nb + no:]
        touts, aouts = fn(tin, bin_)
        for r, v in zip(o_refs, touts):
            r[...] = v.astype(r.dtype)
        i = pl.program_id(0)

        @pl.when(i == 0)
        def _():
            for r, v in zip(a_refs, aouts):
                r[...] = v

        @pl.when(i > 0)
        def _():
            for r, v in zip(a_refs, aouts):
                r[...] += v

    in_specs = [pl.BlockSpec((ts, w), functools.partial(lambda i, cb: (i, cb), cb=cb))
                for (_, w, cb) in tiled]
    in_specs += [pl.BlockSpec(b.shape, lambda i: (0, 0)) for b in bcast]
    out_specs = [pl.BlockSpec((ts, w), lambda i: (i, 0)) for (w, _) in tiled_out]
    out_specs += [pl.BlockSpec((1, w), lambda i: (0, 0)) for w in acc_out]
    out_shape = [_sds((S, w), d) for (w, d) in tiled_out] + [_sds((1, w), F32) for w in acc_out]
    return pl.pallas_call(
        _with_after(body, nt + nb, after), name=name, grid=(S // ts,),
        in_specs=in_specs + [ANY] * len(after), out_specs=out_specs,
        out_shape=out_shape, compiler_params=_params(("arbitrary",)),
    )(*[t[0] for t in tiled], *bcast, *after)


def _ln_stats(v):
    mu = jnp.mean(v, axis=-1, keepdims=True)
    vc = v - mu
    var = jnp.mean(vc * vc, axis=-1, keepdims=True)
    rstd = lax.rsqrt(var + LN_EPS)
    return vc * rstd, rstd


def _ln_bwd(dxhat, xhat, rstd):
    return rstd * (dxhat - jnp.mean(dxhat, axis=-1, keepdims=True)
                   - xhat * jnp.mean(dxhat * xhat, axis=-1, keepdims=True))


def _colsum(v):
    return jnp.sum(v, axis=0, keepdims=True)


def _sigmoid(v):
    return 1.0 / (1.0 + jnp.exp(-v))


_GELU_C = math.sqrt(2.0 / math.pi)


def _gelu(v):
    return 0.5 * v * (1.0 + jnp.tanh(_GELU_C * (v + 0.044715 * v * v * v)))


def _gelu_grad(v):
    t = jnp.tanh(_GELU_C * (v + 0.044715 * v * v * v))
    return 0.5 * (1.0 + t) + 0.5 * v * (1.0 - t * t) * _GELU_C * (1.0 + 3 * 0.044715 * v * v)


def _disc(lr, li, ldt):
    dt = jnp.exp(ldt)
    mag = jnp.exp(lr * dt)
    ang = li * dt
    ab_re = mag * jnp.cos(ang)
    ab_im = mag * jnp.sin(ang)
    num_re = ab_re - 1.0
    num_im = ab_im
    den = lr * lr + li * li
    f_re = (num_re * lr + num_im * li) / den
    f_im = (num_im * lr - num_re * li) / den
    return ab_re, ab_im, f_re, f_im


def _cmul(ar, ai, br, bi):
    return ar * br - ai * bi, ar * bi + ai * br


SCAN_FOLD = 4
NCONST = 18


def s5_disc(lam_re, lam_im, log_dt):
    G, P = lam_re.shape

    def body(lr_ref, li_ref, ldt_ref, f_ref, k_ref):
        ab_re, ab_im, f_re, f_im = _disc(lr_ref[...], li_ref[...], ldt_ref[...])
        f_ref[0] = f_re
        f_ref[1] = f_im
        fr, fi = ab_re, ab_im
        for _ in range(SCAN_FOLD - 1):
            fr, fi = _cmul(fr, fi, ab_re, ab_im)
        pr, pi = [fr], [fi]
        for _ in range(SUBLANES - 1):
            nr, ni = _cmul(pr[-1], pi[-1], fr, fi)
            pr.append(nr)
            pi.append(ni)
        zero = jnp.zeros_like(ab_re)
        for r in range(SUBLANES):
            k_ref[16, r] = ab_re
            k_ref[17, r] = ab_im
        for n, sh in enumerate((1, 2, 4)):
            for r in range(SUBLANES):
                k_ref[2 * n, r] = pr[sh - 1] if r >= sh else zero
                k_ref[2 * n + 1, r] = pi[sh - 1] if r >= sh else zero
                k_ref[8 + 2 * n, r] = pr[sh - 1] if r + sh < SUBLANES else zero
                k_ref[8 + 2 * n + 1, r] = -pi[sh - 1] if r + sh < SUBLANES else zero
        for r in range(SUBLANES):
            k_ref[6, r] = pr[r]
            k_ref[7, r] = pi[r]
            k_ref[14, r] = pr[SUBLANES - 1 - r]
            k_ref[15, r] = -pi[SUBLANES - 1 - r]

    vm = pl.BlockSpec(memory_space=pltpu.VMEM)
    return pl.pallas_call(
        body, name="s5_disc", in_specs=[vm, vm, vm], out_specs=[vm, vm],
        out_shape=[_sds((2, G, P), F32), _sds((NCONST, SUBLANES, G, P), F32)],
    )(lam_re, lam_im, log_dt)


def s5_disc_bwd(lam_re, lam_im, log_dt, d_ab, d_f):
    G, P = lam_re.shape

    def body(lr_ref, li_ref, ldt_ref, dab_ref, df_ref, glr_ref, gli_ref, gdt_ref):
        _, vjp = jax.vjp(_disc, lr_ref[...], li_ref[...], ldt_ref[...])
        glr, gli, gdt = vjp((dab_ref[0], dab_ref[1], df_ref[0], df_ref[1]))
        glr_ref[...] = glr
        gli_ref[...] = gli
        gdt_ref[...] = gdt

    vm = pl.BlockSpec(memory_space=pltpu.VMEM)
    return pl.pallas_call(
        body, name="s5_disc_bwd", in_specs=[vm] * 5, out_specs=[vm] * 3,
        out_shape=[_sds((G, P), F32), _sds((G, P), F32), _sds((G, 1), F32)],
    )(lam_re, lam_im, log_dt, d_ab, d_f)


def _group_mask(cw, nst):
    row = lax.broadcasted_iota(jnp.int32, (cw, 2 * nst), 0) // SSM_GROUP
    col = (lax.broadcasted_iota(jnp.int32, (cw, 2 * nst), 1) % nst) // SSM_STATE
    return row == col


def _spread(t, mask):
    reps = mask.shape[0] // t.shape[0]
    return jnp.where(mask, jnp.tile(t, (reps, 1)), 0.0).astype(BF16)


def _gather_groups(t, mask):
    t = jnp.where(mask, t, 0.0)
    out = t[0:SSM_GROUP]
    for g in range(1, t.shape[0] // SSM_GROUP):
        out = out + t[g * SSM_GROUP:(g + 1) * SSM_GROUP]
    return out


def _s5_operands(f_ref, br_ref, bi_ref, cr_ref, ci_ref, mask):
    fr, fi = f_ref[0], f_ref[1]
    br, bi = br_ref[...], bi_ref[...]
    bm = _spread(jnp.concatenate([fr * br - fi * bi, fr * bi + fi * br], axis=1), mask)
    cm = _spread(jnp.concatenate([cr_ref[...], -ci_ref[...]], axis=1), mask)
    return bm, cm


def _planes_put(ref, val):
    for c in range(ref.shape[0]):
        ref[c] = val[:, c * LANES:(c + 1) * LANES]


def _planes_get(ref):
    return jnp.concatenate([ref[c] for c in range(ref.shape[0])], axis=1)


def _rows_ld(ref, start, lo, hi):
    rows = pl.ds(start, SUBLANES, stride=SCAN_FOLD)
    return jnp.concatenate([ref[c, rows, :] for c in range(lo // LANES, hi // LANES)], axis=1)


def _rows_st(ref, start, lo, val):
    rows = pl.ds(start, SUBLANES, stride=SCAN_FOLD)
    for k in range(val.shape[1] // LANES):
        ref[lo // LANES + k, rows, :] = val[:, k * LANES:(k + 1) * LANES]


def _phases(ref, base, lo, hi):
    return [_rows_ld(ref, base + j, lo, hi) for j in range(SCAN_FOLD)]


def _row_bcast(v, r):
    return jnp.broadcast_to(v[r:r + 1, :], v.shape)


def _scan_fwd(xs, k_ref, nst):
    m = SCAN_FOLD
    ngroup = xs.shape[1] // (SUBLANES * m)
    row = lax.broadcasted_iota(jnp.int32, (SUBLANES, nst), 0)

    def step(t, carry):
        cr, ci = carry
        base = pl.multiple_of(t * (SUBLANES * m), SUBLANES * m)
        ar, ai = k_ref[16], k_ref[17]
        pr, pi = _phases(xs, base, 0, nst), _phases(xs, base, nst, 2 * nst)
        vr, vi = pr[0], pi[0]
        for j in range(1, m):
            vr, vi = pr[j] + ar * vr - ai * vi, pi[j] + ar * vi + ai * vr
        for n, sh in enumerate((1, 2, 4)):
            sr = pltpu.roll(vr, sh, 0)
            si = pltpu.roll(vi, sh, 0)
            mr, mi = k_ref[2 * n], k_ref[2 * n + 1]
            vr, vi = vr + mr * sr - mi * si, vi + mr * si + mi * sr
        qr, qi = k_ref[6], k_ref[7]
        vr, vi = vr + qr * cr - qi * ci, vi + qr * ci + qi * cr
        _rows_st(xs, base + m - 1, 0, vr)
        _rows_st(xs, base + m - 1, nst, vi)
        xr = jnp.where(row == 0, cr, pltpu.roll(vr, 1, 0))
        xi = jnp.where(row == 0, ci, pltpu.roll(vi, 1, 0))
        for j in range(m - 1):
            xr, xi = pr[j] + ar * xr - ai * xi, pi[j] + ar * xi + ai * xr
            _rows_st(xs, base + j, 0, xr)
            _rows_st(xs, base + j, nst, xi)
        return _row_bcast(vr, SUBLANES - 1), _row_bcast(vi, SUBLANES - 1)

    zero = jnp.zeros((SUBLANES, nst), F32)
    lax.fori_loop(0, ngroup, step, (zero, zero))


def _scan_bwd(g, xs, k_ref, nst):
    m = SCAN_FOLD
    ngroup = g.shape[1] // (SUBLANES * m)
    row = lax.broadcasted_iota(jnp.int32, (SUBLANES, nst), 0)

    def step(tt, carry):
        cr, ci, dar, dai = carry
        t = ngroup - 1 - tt
        base = pl.multiple_of(t * (SUBLANES * m), SUBLANES * m)
        ar, ai = k_ref[16], -k_ref[17]
        dr, di = _phases(g, base, 0, nst), _phases(g, base, nst, 2 * nst)
        wr, wi = dr[m - 1], di[m - 1]
        for j in range(m - 2, -1, -1):
            wr, wi = dr[j] + ar * wr - ai * wi, di[j] + ar * wi + ai * wr
        for n, sh in enumerate((1, 2, 4)):
            sr = pltpu.roll(wr, SUBLANES - sh, 0)
            si = pltpu.roll(wi, SUBLANES - sh, 0)
            mr, mi = k_ref[8 + 2 * n], k_ref[8 + 2 * n + 1]
            wr, wi = wr + mr * sr - mi * si, wi + mr * si + mi * sr
        qr, qi = k_ref[14], k_ref[15]
        wr, wi = wr + qr * cr - qi * ci, wi + qr * ci + qi * cr
        gr, gi = [None] * m, [None] * m
        gr[0], gi[0] = wr, wi
        nr = jnp.where(row == SUBLANES - 1, cr, pltpu.roll(wr, SUBLANES - 1, 0))
        ni = jnp.where(row == SUBLANES - 1, ci, pltpu.roll(wi, SUBLANES - 1, 0))
        for j in range(m - 1, 0, -1):
            nr, ni = dr[j] + ar * nr - ai * ni, di[j] + ar * ni + ai * nr
            gr[j], gi[j] = nr, ni
        for j in range(m):
            _rows_st(g, base + j, 0, gr[j])
            _rows_st(g, base + j, nst, gi[j])
        xr, xi = _phases(xs, base, 0, nst), _phases(xs, base, nst, 2 * nst)
        pbase = pl.multiple_of(jnp.maximum(t - 1, 0) * (SUBLANES * m), SUBLANES * m)
        live = (t > 0).astype(F32)
        lr = _row_bcast(_rows_ld(xs, pbase + m - 1, 0, nst), SUBLANES - 1) * live
        li = _row_bcast(_rows_ld(xs, pbase + m - 1, nst, 2 * nst), SUBLANES - 1) * live
        xmr = [jnp.where(row == 0, lr, pltpu.roll(xr[m - 1], 1, 0))] + xr[:m - 1]
        xmi = [jnp.where(row == 0, li, pltpu.roll(xi[m - 1], 1, 0))] + xi[:m - 1]
        for j in range(m):
            dar = dar + gr[j] * xmr[j] + gi[j] * xmi[j]
            dai = dai + gi[j] * xmr[j] - gr[j] * xmi[j]
        return _row_bcast(wr, 0), _row_bcast(wi, 0), dar, dai

    zero = jnp.zeros((SUBLANES, nst), F32)
    _, _, dar, dai = lax.fori_loop(0, ngroup, step, (zero, zero, zero, zero))
    return _colsum(dar), _colsum(dai)


def _s5_param_specs(cw, nst):
    hp = pl.BlockSpec((SSM_GROUP, nst), lambda b: (0, b))
    return [pl.BlockSpec((2, 1, nst), lambda b: (0, 0, b)), hp, hp, hp, hp,
            pl.BlockSpec((1, cw), lambda b: (0, b)),
            pl.BlockSpec((NCONST, SUBLANES, nst), lambda b: (0, 0, b))]


def s5_fwd(proj, params, nb):
    S = proj.shape[0]
    nst = params[1].shape[1] // nb
    cw = nst // SSM_STATE * SSM_GROUP

    def body(u_ref, f_ref, br_ref, bi_ref, cr_ref, ci_ref, d_ref, k_ref, z_ref, xsb_ref, zp_ref, xs):
        bm, cm = _s5_operands(f_ref, br_ref, bi_ref, cr_ref, ci_ref, _group_mask(cw, nst))
        u = u_ref[...]
        _planes_put(xs, jnp.dot(u.astype(BF16), bm, preferred_element_type=F32))
        _scan_fwd(xs, k_ref, nst)
        xsb = _planes_get(xs).astype(BF16)
        xsb_ref[...] = xsb
        y = lax.dot_general(xsb, cm, (((1,), (1,)), ((), ())), preferred_element_type=F32)
        y = y + d_ref[...] * u
        z_ref[...] = _gelu(y).astype(BF16)
        zp_ref[...] = _gelu_grad(y).astype(BF16)

    return pl.pallas_call(
        body, name="s5_fwd", grid=(nb,),
        in_specs=[pl.BlockSpec((S, cw), lambda b: (0, b))] + _s5_param_specs(cw, nst),
        out_specs=[pl.BlockSpec((S, cw), lambda b: (0, b)), pl.BlockSpec((S, 2 * nst), lambda b: (0, b)),
                   pl.BlockSpec((S, cw), lambda b: (0, b))],
        out_shape=[_sds((S, nb * cw), BF16), _sds((S, nb * 2 * nst), BF16), _sds((S, nb * cw), BF16)],
        scratch_shapes=[pltpu.VMEM((2 * nst // LANES, S, LANES), F32)],
        compiler_params=_params(("arbitrary",)),
    )(proj, *params)


def s5_bwd(proj, xsb_all, dz, zp, params, nb, after=()):
    S = proj.shape[0]
    nst = params[1].shape[1] // nb
    cw = nst // SSM_STATE * SSM_GROUP

    def body(u_ref, xsb_ref, dz_ref, zp_ref, f_ref, br_ref, bi_ref, cr_ref, ci_ref, d_ref, k_ref,
             du_ref, gbr_ref, gbi_ref, gcr_ref, gci_ref, gf_ref, gd_ref, ga_ref, xs, g):
        mask = _group_mask(cw, nst)
        bm, cm = _s5_operands(f_ref, br_ref, bi_ref, cr_ref, ci_ref, mask)
        u = u_ref[...]
        ub = u.astype(BF16)
        d = d_ref[...]
        xsb = xsb_ref[...]
        _planes_put(xs, xsb.astype(F32))
        dy = dz_ref[...].astype(F32) * zp_ref[...].astype(F32)
        gd_ref[...] = _colsum(dy * u)
        dyb = dy.astype(BF16)
        gc = _gather_groups(lax.dot_general(dyb, xsb, (((0,), (0,)), ((), ())),
                                            preferred_element_type=F32), mask)
        gcr_ref[...] = gc[:, :nst]
        gci_ref[...] = -gc[:, nst:]
        _planes_put(g, jnp.dot(dyb, cm, preferred_element_type=F32))
        ar, ai = _scan_bwd(g, xs, k_ref, nst)
        ga_ref[0, 0:1, :] = ar
        ga_ref[0, 1:2, :] = ai
        gb = _planes_get(g).astype(BF16)
        du = lax.dot_general(gb, bm, (((1,), (1,)), ((), ())), preferred_element_type=F32) + d * dy
        du_ref[...] = du.astype(BF16)
        gbb = _gather_groups(lax.dot_general(ub, gb, (((0,), (0,)), ((), ())),
                                             preferred_element_type=F32), mask)
        dr, di = gbb[:, :nst], gbb[:, nst:]
        fr, fi = f_ref[0], f_ref[1]
        br, bi = br_ref[...], bi_ref[...]
        gbr_ref[...] = fr * dr + fi * di
        gbi_ref[...] = fr * di - fi * dr
        gf_ref[0] = _colsum(dr * br + di * bi)
        gf_ref[1] = _colsum(di * br - dr * bi)

    hp = pl.BlockSpec((SSM_GROUP, nst), lambda b: (0, b))
    hp_sds = _sds((SSM_GROUP, nb * nst), F32)
    return pl.pallas_call(
        _with_after(body, 11, after), name="s5_bwd", grid=(nb,),
        in_specs=[pl.BlockSpec((S, cw), lambda b: (0, b)),
                  pl.BlockSpec((S, 2 * nst), lambda b: (0, b)),
                  pl.BlockSpec((S, cw), lambda b: (0, b)),
                  pl.BlockSpec((S, cw), lambda b: (0, b))] + _s5_param_specs(cw, nst)
        + [ANY] * len(after),
        out_specs=[pl.BlockSpec((S, cw), lambda b: (0, b)), hp, hp, hp, hp,
                   pl.BlockSpec((2, 1, nst), lambda b: (0, 0, b)),
                   pl.BlockSpec((1, cw), lambda b: (0, b)),
                   pl.BlockSpec((1, 2, nst), lambda b: (b, 0, 0))],
        out_shape=[_sds((S, nb * cw), BF16), hp_sds, hp_sds, hp_sds, hp_sds,
                   _sds((2, 1, nb * nst), F32), _sds((1, nb * cw), F32), _sds((nb, 2, nst), F32)],
        scratch_shapes=[pltpu.VMEM((2 * nst // LANES, S, LANES), F32)] * 2,
        compiler_params=_params(("arbitrary",)),
    )(proj, xsb_all, dz, zp, *params, *after)


def _shift_rows(v, k, row, down):
    n = v.shape[0]
    if down:
        return jnp.where(row >= k, pltpu.roll(v, k, 0), 0.0)
    return jnp.where(row < n - k, pltpu.roll(v, n - k, 0), 0.0)


def _window(v, gi, row, down):
    sums = []
    s = v
    for k in (1, 2, 4, 8):
        s = s + _shift_rows(s, k, row, down)
        sums.append(s)
    out = sums[3]
    for n in (2, 1, 0):
        out = jnp.where(gi == n, sums[n], out)
    return out


def pool_fwd(proj, col0, width, gw):
    S = proj.shape[0]
    cb0 = col0 // gw

    def body(u_ref, o_ref):
        gi = pl.program_id(0)
        u = u_ref[...]
        row = lax.broadcasted_iota(jnp.int32, u.shape, 0)
        w = jnp.left_shift(2, gi)
        count = jnp.minimum(row + 1, w).astype(F32)
        o_ref[...] = (_window(u, gi, row, True) / count - u).astype(BF16)

    return pl.pallas_call(
        body, name="pool_fwd", grid=(len(POOL_WINDOWS),),
        in_specs=[pl.BlockSpec((S, gw), lambda g: (0, cb0 + g))],
        out_specs=pl.BlockSpec((S, gw), lambda g: (0, g)),
        out_shape=_sds((S, width), BF16), compiler_params=_params(("arbitrary",)),
    )(proj)


def pool_bwd(dpooled, gw):
    S, width = dpooled.shape

    def body(d_ref, o_ref):
        gi = pl.program_id(0)
        d = d_ref[...]
        row = lax.broadcasted_iota(jnp.int32, d.shape, 0)
        w = jnp.left_shift(2, gi)
        count = jnp.minimum(row + 1, w).astype(F32)
        o_ref[...] = (_window(d / count, gi, row, False) - d).astype(BF16)

    return pl.pallas_call(
        body, name="pool_bwd", grid=(len(POOL_WINDOWS),),
        in_specs=[pl.BlockSpec((S, gw), lambda g: (0, g))],
        out_specs=pl.BlockSpec((S, gw), lambda g: (0, g)),
        out_shape=_sds((S, width), BF16), compiler_params=_params(("arbitrary",)),
    )(dpooled)


def _place():
    x, y, c = lax.axis_index("x"), lax.axis_index("y"), lax.axis_index("c")
    chips = [(1 - x, y), (x, 1 - y), (1 - x, 1 - y)]
    return x, y, c, chips


HBM = pl.BlockSpec(memory_space=pltpu.HBM)


def _routed_gather_body(n):
    def body(*refs):
        ins, outs = refs[:n], refs[n:2 * n]
        send_sems, recv_sems, local_sems = refs[2 * n:]
        x, y, c, (xn, yn, dg) = _place()
        me, sibling = (x, y, c), (x, y, 1 - c)
        barrier = pltpu.get_barrier_semaphore()
        for peer in (sibling, (*xn, c), (*yn, c)):
            pl.semaphore_signal(barrier, inc=1, device_id=peer, device_id_type=MESH)
        pl.semaphore_wait(barrier, 3)

        def piece(i, p, h):
            rows = ins[i].shape[0] // 2
            return outs[i].at[4 * p[0] + 2 * p[1] + p[2], pl.ds(h * rows, rows)]

        def copy(i, k, src, dst, to):
            return pltpu.make_async_remote_copy(src_ref=src, dst_ref=dst, send_sem=send_sems.at[i, k],
                                                recv_sem=recv_sems.at[i, k], device_id=to,
                                                device_id_type=MESH)

        started = []

        def go(cp):
            cp.start()
            started.append(cp)

        for i in range(n):
            rows = ins[i].shape[0] // 2
            for h in range(2):
                own = ins[i].at[pl.ds(h * rows, rows)]
                go(copy(i, 1 + h, own, piece(i, me, h), (*xn, c)))
                go(copy(i, 3 + h, own, piece(i, me, h), (*yn, c)))
        for i in range(n):
            go(copy(i, 0, ins[i], outs[i].at[4 * x + 2 * y + c], sibling))
        mine = [pltpu.make_async_copy(ins[i], outs[i].at[4 * x + 2 * y + c], local_sems.at[i])
                for i in range(n)]
        for cp in mine:
            cp.start()
        for i in range(n):
            for k, chip, h, onward, ksib in ((1, xn, 0, (5, yn), 7), (4, yn, 1, (6, xn), 10),
                                            (2, xn, 1, None, 8), (3, yn, 0, None, 9),
                                            (5, dg, 0, None, 11), (6, dg, 1, None, 12)):
                got = piece(i, (*chip, c), h)
                copy(i, k, got, got, me).wait_recv()
                if onward is not None:
                    go(copy(i, onward[0], got, got, (*onward[1], c)))
                go(copy(i, ksib, got, got, sibling))
        for i in range(n):
            block = outs[i].at[4 * x + 2 * y + 1 - c]
            copy(i, 0, block, block, me).wait_recv()
            for ksib, chip, h in ((7, xn, 0), (10, yn, 1), (8, xn, 1), (9, yn, 0), (11, dg, 0), (12, dg, 1)):
                got = piece(i, (*chip, 1 - c), h)
                copy(i, ksib, got, got, me).wait_recv()
        for cp in started:
            cp.wait_send()
        for cp in mine:
            cp.wait()

    return body


def _on_sequencer(name, body, arrays, out_sds, sems, collective_id):
    ins = [jax.new_ref(a, memory_space=pltpu.MemorySpace.HBM) for a in arrays]
    outs = [jax.empty_ref(s, memory_space=pltpu.MemorySpace.HBM) for s in out_sds]

    @pl.kernel(mesh=plsc.ScalarSubcoreMesh(axis_name="sequencer", num_cores=1), name=name,
               scratch_types=tuple(sems),
               compiler_params=pltpu.CompilerParams(collective_id=collective_id))
    def launch(*sem_refs):
        body(*ins, *outs, *sem_refs)

    launch()
    return [o[...] for o in outs]


def seq_all_gather(name, shards, collective_id):
    n = len(shards)
    return _on_sequencer(
        name, _routed_gather_body(n), shards, [_sds((NDEV,) + s.shape, s.dtype) for s in shards],
        [pltpu.SemaphoreType.DMA((n, 13)), pltpu.SemaphoreType.DMA((n, 13)),
         pltpu.SemaphoreType.DMA((n,))], collective_id)


def pair_exchange(name, grads, collective_id):
    def plan(srcs, lands):
        x, y, c, _ = _place()
        return ([(i, q, srcs[i].at[2 * q + 1 - c], lands[i].at[q], (x, y, 1 - c))
                 for i in range(len(srcs)) for q in range(NCHIP)], [(x, y, 1 - c)])

    return _split_exchange(name, grads, [_sds((NCHIP,) + g.shape[1:], g.dtype) for g in grads],
                           plan, NCHIP, collective_id)


SEM = pl.BlockSpec(memory_space=pltpu.SEMAPHORE)


def _split_exchange(name, srcs, land_sds, plan, ncopy, collective_id):
    n = len(srcs)
    nsem = n * ncopy
    effect = pltpu.SideEffectType.DATAFLOW_SIDE_EFFECTING

    def descriptors(src_refs, land_refs, send_sems, recv_sems):
        copies, peers = plan(src_refs, land_refs)
        return [pltpu.make_async_remote_copy(src_ref=s, dst_ref=d, send_sem=send_sems[i * ncopy + k],
                                             recv_sem=recv_sems[i * ncopy + k], device_id=to,
                                             device_id_type=MESH) for (i, k, s, d, to) in copies], peers

    def start_body(*refs):
        src_refs, land_refs = refs[:n], refs[n:2 * n]
        send_sems, recv_sems = refs[2 * n:2 * n + nsem], refs[2 * n + nsem:2 * n + 2 * nsem]
        token = refs[-1]
        cps, peers = descriptors(src_refs, land_refs, send_sems, recv_sems)
        barrier = pltpu.get_barrier_semaphore()
        for peer in peers:
            pl.semaphore_signal(barrier, inc=1, device_id=peer, device_id_type=MESH)
        pl.semaphore_wait(barrier, len(peers))
        for cp in cps:
            cp.start()
        token[...] = jnp.zeros_like(token)

    lands = [pltpu.with_memory_space_constraint(lax.empty(s.shape, s.dtype), pltpu.HBM) for s in land_sds]
    srcs = [pltpu.with_memory_space_constraint(s, pltpu.HBM) for s in srcs]
    res = pl.pallas_call(
        start_body, name=name + "_start",
        out_shape=(pltpu.SemaphoreType.DMA(()),) * (2 * nsem)
        + tuple(pltpu.HBM(s.shape, s.dtype) for s in srcs)
        + tuple(pltpu.HBM(s.shape, s.dtype) for s in land_sds) + (_sds((SUBLANES, LANES), F32),),
        in_specs=[HBM] * (2 * n),
        out_specs=(SEM,) * (2 * nsem) + (HBM,) * (2 * n) + (pl.BlockSpec(memory_space=pltpu.VMEM),),
        input_output_aliases={i: 2 * nsem + i for i in range(2 * n)},
        compiler_params=pltpu.CompilerParams(has_side_effects=effect, collective_id=collective_id),
    )(*srcs, *lands)
    sems = res[:2 * nsem]
    thru = res[2 * nsem:2 * nsem + 2 * n]
    token = res[-1]

    def wait(after):
        def wait_body(*refs):
            src_refs, land_refs = refs[:n], refs[n:2 * n]
            cps, _ = descriptors(src_refs, land_refs, refs[2 * n:2 * n + nsem],
                                 refs[2 * n + nsem:2 * n + 2 * nsem])
            for cp in cps:
                cp.wait_send()
            for cp in cps:
                cp.wait_recv()

        out = pl.pallas_call(
            wait_body, name=name + "_wait",
            out_shape=tuple(pltpu.HBM(s.shape, s.dtype) for s in srcs)
            + tuple(pltpu.HBM(s.shape, s.dtype) for s in land_sds),
            in_specs=[HBM] * (2 * n) + [SEM] * (2 * nsem) + [pl.BlockSpec(memory_space=pl.ANY)],
            out_specs=(HBM,) * (2 * n),
            input_output_aliases={i: i for i in range(2 * n)},
            compiler_params=pltpu.CompilerParams(has_side_effects=effect),
        )(*thru, *sems, after)
        return list(out[:n]), list(out[n:])

    return token, wait


def pair_sum(name, grad, got, place):
    shp = grad.shape[1:]
    r, cdim = shp[-2], shp[-1]
    lead = int(math.prod(shp[:-2])) if len(shp) > 2 else 1
    g5 = grad.reshape(NCHIP, 2, lead * r, cdim)
    t4 = got.reshape(NCHIP, lead * r, cdim)
    R = lead * r
    tr = _tile(R, max(8, (1 << 20) // cdim))

    def body(p_ref, g_ref, t_ref, o_ref):
        o_ref[...] = (g_ref[0].astype(F32) + t_ref[...].astype(F32)).astype(o_ref.dtype)

    out = pl.pallas_call(
        body, name=name,
        grid_spec=pltpu.PrefetchScalarGridSpec(
            num_scalar_prefetch=1, grid=(NCHIP - 1, R // tr),
            in_specs=[pl.BlockSpec((1, 1, tr, cdim), lambda j, i, p: (p[1] ^ (j + 1), p[0], i, 0)),
                      pl.BlockSpec((1, tr, cdim), lambda j, i, p: (p[1] ^ (j + 1), i, 0))],
            out_specs=pl.BlockSpec((1, tr, cdim), lambda j, i, p: (p[1] ^ (j + 1), i, 0))),
        out_shape=_sds((NCHIP, R, cdim), grad.dtype),
        compiler_params=_params(("parallel", "parallel")),
    )(place, g5, t4)
    return out


def chip_exchange(name, parts, collective_id):
    def plan(srcs, lands):
        x, y, c, chips = _place()
        return ([(i, j, srcs[i].at[2 * chip[0] + chip[1]], lands[i].at[j], (*chip, c))
                 for i in range(len(srcs)) for j, chip in enumerate(chips)],
                [(*chip, c) for chip in chips])

    return _split_exchange(name, parts, [_sds((3,) + p.shape[1:], p.dtype) for p in parts],
                           plan, 3, collective_id)


def ada_fwd(c_row, w_ada, b_ada):
    D, cols = w_ada.shape

    def body(c_ref, w_ref, b_ref, mod_ref, call_ref, act8, part, s1, r1, s2, r2):
        x, y, c, _ = _place()
        me = 4 * x + 2 * y + c
        call_ref[me] = c_ref[...]
        cps = []
        for k in range(1, NDEV):
            to = (x ^ (k >> 2), y ^ ((k >> 1) & 1), c ^ (k & 1))
            cps.append(pltpu.make_async_remote_copy(
                src_ref=c_ref, dst_ref=call_ref.at[me], send_sem=s1.at[k - 1],
                recv_sem=r1.at[k - 1], device_id=to, device_id_type=MESH))
            cps[-1].start()
        for cp in cps:
            cp.wait()
        for b in range(NDEV):
            act8[b:b + 1, :] = call_ref[b]
        cv = act8[...]
        act = (cv * _sigmoid(cv)).astype(BF16)
        res = jnp.dot(act, w_ref[...].astype(BF16), preferred_element_type=F32)
        for b in range(NDEV):
            part[b] = res[b:b + 1, :]
        mod_ref[me] = part[me]
        cps = []
        for k in range(1, NDEV):
            to = (x ^ (k >> 2), y ^ ((k >> 1) & 1), c ^ (k & 1))
            dst = 4 * to[0] + 2 * to[1] + to[2]
            cps.append(pltpu.make_async_remote_copy(
                src_ref=part.at[dst], dst_ref=mod_ref.at[me], send_sem=s2.at[k - 1],
                recv_sem=r2.at[k - 1], device_id=to, device_id_type=MESH))
            cps[-1].start()
        for cp in cps:
            cp.wait()
        for b in range(NDEV):
            mod_ref[b] = mod_ref[b] + b_ref[b]

    vm = pl.BlockSpec(memory_space=pltpu.VMEM)
    return pl.pallas_call(
        body, name="ada_fwd", in_specs=[vm, vm, vm], out_specs=[vm, vm],
        out_shape=[_sds((NDEV, 1, cols), F32), _sds((NDEV, 1, D), F32)],
        scratch_shapes=[pltpu.VMEM((NDEV, D), F32), pltpu.VMEM((NDEV, 1, cols), F32),
                        pltpu.SemaphoreType.DMA((NDEV - 1,)), pltpu.SemaphoreType.DMA((NDEV - 1,)),
                        pltpu.SemaphoreType.DMA((NDEV - 1,)), pltpu.SemaphoreType.DMA((NDEV - 1,))],
        compiler_params=pltpu.CompilerParams(vmem_limit_bytes=VMEM_LIMIT),
    )(c_row, w_ada, b_ada.reshape(NDEV, 1, cols))


def _adamw_math(g, w, m, v):
    m2 = ADAM_B1 * m + (1.0 - ADAM_B1) * g
    v2 = ADAM_B2 * v + (1.0 - ADAM_B2) * (g * g)
    m_hat = m2 / (1.0 - ADAM_B1 ** ADAM_STEP)
    v_hat = v2 / (1.0 - ADAM_B2 ** ADAM_STEP)
    delta = -ADAM_LR * (m_hat / (jnp.sqrt(v_hat) + ADAM_EPS) + ADAM_WD * w)
    return delta, m2, v2


def adamw_sharded(name, grad8, pair4, got3, w, m, v, place, after=()):
    shape = w.shape
    cdim = shape[-1]
    R = int(math.prod(shape[:-1]))
    w2, m2, v2 = (t.reshape(R, cdim) for t in (w, m, v))
    tr = _tile(R, max(8, (1 << 19) // cdim))

    def body(q_ref, own_ref, sib_ref, t_ref, w_ref, m_ref, v_ref, g_out, d_out, m_out, v_out):
        g = own_ref[0].astype(F32) + sib_ref[0].astype(F32)
        for j in range(3):
            g = g + t_ref[j].astype(F32)
        d, mn, vn = _adamw_math(g, w_ref[...], m_ref[...], v_ref[...])
        g_out[...] = g
        d_out[...] = d
        m_out[...] = mn
        v_out[...] = vn

    spec = pl.BlockSpec((tr, cdim), lambda i, qr: (i, 0))
    outs = pl.pallas_call(
        _with_after(body, 7, after), name=name,
        grid_spec=pltpu.PrefetchScalarGridSpec(
            num_scalar_prefetch=1, grid=(R // tr,),
            in_specs=[pl.BlockSpec((1, tr, cdim), lambda i, qr: (qr[2], i, 0)),
                      pl.BlockSpec((1, tr, cdim), lambda i, qr: (qr[1], i, 0)),
                      pl.BlockSpec((3, tr, cdim), lambda i, qr: (0, i, 0)), spec, spec, spec]
            + [ANY] * len(after),
            out_specs=[spec] * 4),
        out_shape=[_sds((R, cdim), F32)] * 4,
        compiler_params=_params(("parallel",)),
    )(place, grad8.reshape(NDEV, R, cdim), pair4.reshape(NCHIP, R, cdim),
      got3.reshape(3, R, cdim), w2, m2, v2, *after)
    return [o.reshape(shape) for o in outs]


def sum_small(parts, after=()):
    R = parts.shape[1]

    def body(p_ref, g_out):
        g = p_ref[0]
        for j in range(1, NDEV):
            g = g + p_ref[j]
        g_out[...] = g

    return pl.pallas_call(
        _with_after(body, 1, after), name="sum_small", grid=(1,),
        in_specs=[pl.BlockSpec((NDEV, R, LANES), lambda i: (0, 0, 0))] + [ANY] * len(after),
        out_specs=pl.BlockSpec((R, LANES), lambda i: (0, 0)), out_shape=_sds((R, LANES), F32),
        compiler_params=_params(("arbitrary",)),
    )(parts, *after)


def adamw_natural(gs, ws, ms, vs):
    n = len(ws)
    nblk = 8
    big = [w.ndim == 4 and w.shape[1] % nblk == 0 for w in ws]

    def spec(w, is_big):
        if is_big:
            return pl.BlockSpec((1, w.shape[1] // nblk) + w.shape[2:], lambda i: (0, i, 0, 0))
        return pl.BlockSpec(w.shape, functools.partial(lambda i, nd: (0,) * nd, nd=w.ndim))

    def body(*refs):
        g_refs, w_refs, m_refs, v_refs = (refs[k * n:(k + 1) * n] for k in range(4))
        d_outs, m_outs, v_outs = (refs[(4 + k) * n:(5 + k) * n] for k in range(3))

        def update(p):
            d, mn, vn = _adamw_math(g_refs[p][...], w_refs[p][...], m_refs[p][...], v_refs[p][...])
            d_outs[p][...] = d
            m_outs[p][...] = mn
            v_outs[p][...] = vn

        for p in range(n):
            if big[p]:
                update(p)

        @pl.when(pl.program_id(0) == 0)
        def _():
            for p in range(n):
                if not big[p]:
                    update(p)

    specs = [spec(w, b) for w, b in zip(ws, big)]
    outs = pl.pallas_call(
        body, name="adamw_natural", grid=(nblk,), in_specs=specs * 4, out_specs=specs * 3,
        out_shape=[_sds(w.shape, F32) for w in ws] * 3,
        compiler_params=_params(("arbitrary",)),
    )(*gs, *ws, *ms, *vs)
    return outs[:n], outs[n:2 * n], outs[2 * n:]


def adamw_ada(c_all_t, dmod_all, w, m, v, my_dev):
    D, cols = w.shape
    tr = _tile(D, 256)

    def body(k_ref, c_ref, d_ref, w_ref, m_ref, v_ref, g_out, d_out, m_out, v_out):
        cv = c_ref[...]
        act = cv * _sigmoid(cv)
        dm = d_ref[...]
        g = act[:, 0:1] * dm[0:1, :]
        for b in range(1, NDEV):
            g = g + act[:, b:b + 1] * dm[b:b + 1, :]
        d, mn, vn = _adamw_math(g, w_ref[...], m_ref[...], v_ref[...])
        g_out[...] = g
        d_out[...] = d
        m_out[...] = mn
        v_out[...] = vn

    spec = pl.BlockSpec((tr, cols), lambda i, kr: (i, 0))
    return pl.pallas_call(
        body, name="adamw_ada",
        grid_spec=pltpu.PrefetchScalarGridSpec(
            num_scalar_prefetch=1, grid=(D // tr,),
            in_specs=[pl.BlockSpec((tr, NDEV), lambda i, kr: (i, 0)),
                      pl.BlockSpec((NDEV, cols), lambda i, kr: (0, kr[0])), spec, spec, spec],
            out_specs=[spec] * 4),
        out_shape=[_sds((D, cols), F32)] * 4,
        compiler_params=_params(("parallel",)),
    )(my_dev, c_all_t, dmod_all, w, m, v)


def _small_pack(parts):
    rows = []
    for p in parts:
        flat = p.reshape(-1)
        flat = jnp.pad(flat, (0, (-flat.shape[0]) % (SUBLANES * LANES)))
        rows.append(flat.reshape(-1, LANES))
    return jnp.concatenate(rows, axis=0)


def _small_unpack(buf, shapes):
    out, r = [], 0
    for s in shapes:
        n = int(math.prod(s))
        nr = -(-n // (SUBLANES * LANES)) * SUBLANES
        out.append(buf[r:r + nr].reshape(-1)[:n].reshape(s))
        r += nr
    return out


def kernel(x, c, w_ada, b_ada, w_in, lam_re, lam_im, log_dt, ssm_b_re, ssm_b_im, ssm_c_re, ssm_c_im, ssm_d, w_glu_val, w_glu_gate, w_pool, pool_scale, w_pool_out, w_out, ln1_g, ln1_b, w_ff1, w_ff2, ln2_g, ln2_b, loss_target, m_w_ada, m_b_ada, m_w_in, m_lam_re, m_lam_im, m_log_dt, m_ssm_b_re, m_ssm_b_im, m_ssm_c_re, m_ssm_c_im, m_ssm_d, m_w_glu_val, m_w_glu_gate, m_w_pool, m_pool_scale, m_w_pool_out, m_w_out, m_ln1_g, m_ln1_b, m_w_ff1, m_w_ff2, m_ln2_g, m_ln2_b, v_w_ada, v_b_ada, v_w_in, v_lam_re, v_lam_im, v_log_dt, v_ssm_b_re, v_ssm_b_im, v_ssm_c_re, v_ssm_c_im, v_ssm_d, v_w_glu_val, v_w_glu_gate, v_w_pool, v_pool_scale, v_w_pool_out, v_w_out, v_ln1_g, v_ln1_b, v_w_ff1, v_w_ff2, v_ln2_g, v_ln2_b):
    S, D = x.shape[1], x.shape[2]
    x2d, tgt = x[0], loss_target[0]
    W = D // 2
    G = W // SSM_GROUP
    P, H, GPB = SSM_STATE, SSM_GROUP, GROUPS_PER_BLOCK
    nblk = G // GPB
    gw = W // len(POOL_WINDOWS)
    ax, ay, ac = lax.axis_index("x"), lax.axis_index("y"), lax.axis_index("c")
    my_dev = (4 * ax + 2 * ay + ac).astype(jnp.int32).reshape(1)
    place = jnp.stack([ac, 2 * ax + ay, 4 * ax + 2 * ay + ac]).astype(jnp.int32)
    ts = _tile(S, 256)

    glu = jnp.stack([w_glu_val[0], w_glu_gate[0]]).astype(BF16)
    shards = [w_in[0].astype(BF16), glu, w_pool[0].astype(BF16), w_pool_out[0].astype(BF16),
              w_out[0].astype(BF16), w_ff1[0].astype(BF16), w_ff2[0].astype(BF16)]
    wg_in, wg_pool = seq_all_gather("gather_w_in", [shards[0], shards[2]], 1)
    wg_vg, wg_po, wg_out = seq_all_gather("gather_w_mix", [shards[1], shards[3], shards[4]], 2)
    (wg_ff1,) = seq_all_gather("gather_w_ff1", shards[5:6], 3)
    (wg_ff2,) = seq_all_gather("gather_w_ff2", shards[6:7], 11)
    wg_vg = wg_vg.reshape(2 * NDEV, W, D // NDEV)
    nwin = len(POOL_WINDOWS)
    wp_full = jnp.transpose(wg_pool, (1, 0, 2, 3)).reshape(nwin, gw, gw)
    wout_full = wg_out.reshape(1, D, D)
    wff2_full = wg_ff2.reshape(1, 4 * D, D)

    small_names = [b_ada, lam_re, lam_im, log_dt, ssm_b_re, ssm_b_im, ssm_c_re, ssm_c_im, ssm_d,
                   pool_scale, ln1_g, ln1_b, ln2_g, ln2_b]
    small_m = [m_b_ada, m_lam_re, m_lam_im, m_log_dt, m_ssm_b_re, m_ssm_b_im, m_ssm_c_re, m_ssm_c_im,
               m_ssm_d, m_pool_scale, m_ln1_g, m_ln1_b, m_ln2_g, m_ln2_b]
    small_v = [v_b_ada, v_lam_re, v_lam_im, v_log_dt, v_ssm_b_re, v_ssm_b_im, v_ssm_c_re, v_ssm_c_im,
               v_ssm_d, v_pool_scale, v_ln1_g, v_ln1_b, v_ln2_g, v_ln2_b]

    mod, c_all = ada_fwd(c, w_ada[0], b_ada)
    mod = mod.reshape(6, 1, D)
    sh1, sc1, g1, sh2, sc2, g2 = (mod[i] for i in range(6))

    f2, kconst = s5_disc(lam_re[0], lam_im[0], log_dt[0].reshape(G, 1))
    kconst = kconst.reshape(NCONST, SUBLANES, G * P)
    f2r = f2.reshape(2, 1, G * P)
    bt_re = jnp.transpose(ssm_b_re[0], (2, 0, 1)).reshape(H, G * P)
    bt_im = jnp.transpose(ssm_b_im[0], (2, 0, 1)).reshape(H, G * P)
    ct_re = jnp.transpose(ssm_c_re[0], (1, 0, 2)).reshape(H, G * P)
    ct_im = jnp.transpose(ssm_c_im[0], (1, 0, 2)).reshape(H, G * P)
    s5_params = (f2r, bt_re, bt_im, ct_re, ct_im, ssm_d, kconst)

    def e1(t, b):
        xhat, _ = _ln_stats(t[0])
        return [xhat * (1.0 + b[0]) + b[1]], []
    (h1,) = _rowwise("ln_mod1", e1, S, ts, [(x2d, D, 0)], [sc1, sh1], [(D, BF16)], [])

    (proj,) = mm_nn("proj", h1, wg_in, F32, 2)
    z, xsb_all, zp = s5_fwd(proj, s5_params, nblk)
    (vt,) = mm_nn("glu", z, wg_vg, BF16, 4)
    pooled = pool_fwd(proj, W, W, gw)

    def pool_epi(vals, ex, outs):
        a = vals[0]
        outs[0][...] = a
        outs[1][...] = (a * ex[0][...]).astype(BF16)
    tmp = _tile(S, 1024)
    yp, ypool = _mm(
        "pool_mix", "nn", pooled, wp_full.astype(BF16), (S // tmp, nwin, 1),
        pl.BlockSpec((tmp, gw), lambda i, j, k: (i, j)), pl.BlockSpec((1, gw, gw), lambda i, j, k: (j, 0, 0)),
        [(_sds((S, W), F32), pl.BlockSpec((tmp, gw), lambda i, j, k: (i, j))),
         (_sds((S, W), BF16), pl.BlockSpec((tmp, gw), lambda i, j, k: (i, j)))],
        (tmp, gw), 1, gw, None, pool_epi,
        [(pool_scale, pl.BlockSpec((1, gw), lambda i, j, k: (0, j)))])
    (y_b,) = mm_nn("pool_out", ypool, wg_po, BF16, 4)

    cb = D // NDEV
    ga_cb, gb_cb = (2 * W) // cb, (2 * W + D) // cb
    mcb = 4
    wm = mcb * cb
    tsm = _tile(S, 256)

    def merge_call(name, fn, ins, n_out, after=()):
        def body(*refs):
            vals = [r[...].astype(F32) for r in refs[:len(ins)]]
            for r, v in zip(refs[len(ins):], fn(*vals)):
                r[...] = v.astype(r.dtype)
        return pl.pallas_call(
            _with_after(body, len(ins), after), name=name, grid=(S // tsm, NDEV // mcb),
            in_specs=[pl.BlockSpec((tsm, w), f) for (_, w, f) in ins] + [ANY] * len(after),
            out_specs=[pl.BlockSpec((tsm, w), lambda i, j: (i, j)) for (_, w) in n_out],
            out_shape=[_sds((S, cols), BF16) for (cols, _) in n_out],
            compiler_params=_params(("parallel", "parallel")),
        )(*[a for (a, _, _) in ins], *after)

    merge_ins = [(proj, wm, lambda i, j: (i, ga_cb // mcb + j)), (proj, wm, lambda i, j: (i, gb_cb // mcb + j)),
                 (vt, 2 * wm, lambda i, j: (i, j)), (y_b, wm, lambda i, j: (i, j))]

    def val_gate(vtv):
        return (jnp.concatenate([vtv[:, 2 * q * cb:(2 * q + 1) * cb] for q in range(mcb)], axis=1),
                jnp.concatenate([vtv[:, (2 * q + 1) * cb:(2 * q + 2) * cb] for q in range(mcb)], axis=1))

    def merge_f(ga, gb, vtv, yb):
        vv, tt = val_gate(vtv)
        return [_sigmoid(ga) * (vv * _sigmoid(tt)) + _sigmoid(gb) * yb]
    (merged,) = merge_call("merge", merge_f, merge_ins, [(D, wm)])

    (mix,) = mm_nn("mix_out", merged, wout_full, F32, 1)

    def e3(t, b):
        xv, mx = t
        g1v, l1g, l1b, sc2v, sh2v = b
        r1 = ALPHA * xv + g1v * mx
        xh1, _ = _ln_stats(r1)
        x1 = xh1 * l1g + l1b
        xh, _ = _ln_stats(x1)
        return [r1, xh * (1.0 + sc2v) + sh2v], []
    r1, h2 = _rowwise("post_mix", e3, S, ts, [(x2d, D, 0), (mix, D, 0)],
                      [g1, ln1_g, ln1_b, sc2, sh2], [(D, F32), (D, BF16)], [])

    def relu_epi(vals, ex, outs):
        outs[0][...] = jnp.maximum(vals[0], 0.0).astype(BF16)
    (rl,) = mm_nn("ff1", h2, wg_ff1, BF16, 1, epi=relu_epi)

    def square(a):
        return a * a
    (y2,) = mm_nn("ff2", rl, wff2_full, F32, 1, pro=square)

    def e4(t, b):
        r1v, y2v, tg = t
        g2v, l1g, l1b, l2g, l2b = b
        xh1, _ = _ln_stats(r1v)
        x1 = xh1 * l1g + l1b
        r2 = ALPHA * x1 + g2v * y2v
        xh2, rs2 = _ln_stats(r2)
        err = xh2 * l2g + l2b - tg
        dx2 = err * (1.0 / D)
        dr2 = _ln_bwd(dx2 * l2g, xh2, rs2)
        lsum = jnp.sum(_colsum(err * err), axis=1, keepdims=True) * (0.5 / D)
        return ([ALPHA * dr2, g2v * dr2],
                [jnp.broadcast_to(lsum, (1, LANES)), _colsum(dx2 * xh2), _colsum(dx2), _colsum(dr2 * y2v)])
    dx1a, dy2, loss_acc, g_ln2g, g_ln2b, d_g2 = _rowwise(
        "head", e4, S, ts, [(r1, D, 0), (y2, D, 0), (tgt, D, 0)], [g2, ln1_g, ln1_b, ln2_g, ln2_b],
        [(D, F32), (D, BF16)], [LANES, D, D, D])

    tn_ff = _tile(4 * D, 1024)

    def dff_epi(vals, ex, outs):
        outs[0][...] = (vals[0] * (2.0 * ex[0][...].astype(F32))).astype(BF16)
    tmf = _tile(S, 1024)
    (da1,) = mm_nt("d_ff2", dy2, wff2_full, BF16, 1, tn=tn_ff, epi=dff_epi,
                   extras=[(rl, pl.BlockSpec((tmf, tn_ff), lambda i, j, k: (i, j)))])
    gw_ff2 = mm_tn("gw_ff2", rl, dy2, BF16, NDEV, 0, pro=square)
    gw_ff1 = mm_tn("gw_ff1", h2, da1, BF16, NDEV, 1)
    tok, wait_pair_a = pair_exchange("pair_exchange_ff", [gw_ff2, gw_ff1], 4)
    (dh2,) = mm_nt("d_ff1", da1, wg_ff1, F32, 4, after=[tok])

    def e5(t, b):
        dh2v, r1v, dx1av, mx = t
        sc2v, l1g, l1b, g1v = b
        xh1, rs1 = _ln_stats(r1v)
        x1 = xh1 * l1g + l1b
        xh, rs = _ln_stats(x1)
        dx1 = dx1av + _ln_bwd(dh2v * (1.0 + sc2v), xh, rs)
        dr1 = _ln_bwd(dx1 * l1g, xh1, rs1)
        return ([ALPHA * dr1, g1v * dr1],
                [_colsum(dh2v * xh), _colsum(dh2v), _colsum(dx1 * xh1), _colsum(dx1), _colsum(dr1 * mx)])
    dxa, dmix, d_sc2, d_sh2, g_ln1g, g_ln1b, d_g1 = _rowwise(
        "post_mix_bwd", e5, S, ts, [(dh2, D, 0), (r1, D, 0), (dx1a, D, 0), (mix, D, 0)],
        [sc2, ln1_g, ln1_b, g1], [(D, F32), (D, BF16)], [D, D, D, D, D])

    (dmerged,) = mm_nt("d_mix_out", dmix, wout_full, BF16, 1)
    gw_out = mm_tn("gw_out", merged, dmix, BF16, NDEV, 0)
    grads_a, got_a = wait_pair_a(gw_out)
    parts_a = [pair_sum("pair_sum_ff%d" % i, g, t, place) for i, (g, t) in enumerate(zip(grads_a, got_a))]
    tok, wait_chip_a = chip_exchange("chip_exchange_ff", parts_a, 5)

    def merge_b(ga, gb, vtv, yb, dm):
        vv, tt = val_gate(vtv)
        sa, sb, st = _sigmoid(ga), _sigmoid(gb), _sigmoid(tt)
        dya = dm * sa
        dv, dt = dya * st, dya * vv * st * (1.0 - st)
        dvt_tile = jnp.concatenate([t[:, q * cb:(q + 1) * cb] for q in range(mcb) for t in (dv, dt)], axis=1)
        return [dm * (vv * st) * sa * (1.0 - sa), dm * yb * sb * (1.0 - sb), dvt_tile, dm * sb]
    dga, dgb_, dvt, dy_b = merge_call(
        "merge_bwd", merge_b, merge_ins + [(dmerged, wm, lambda i, j: (i, j))],
        [(D, wm), (D, wm), (2 * D, 2 * wm), (D, wm)], after=[tok])

    (dypool,) = mm_nt("d_pool_out", dy_b, wg_po, F32, NDEV)
    gw_po = mm_tn("gw_pool_out", ypool, dy_b, BF16, NDEV, 4)

    def e7(t, b):
        return [t[0] * b[0]], [_colsum(t[0] * t[1])]
    dyp, g_pscale = _rowwise("pool_scale_bwd", e7, S, ts, [(dypool, W, 0), (yp, W, 0)],
                             [pool_scale], [(W, BF16)], [W])
    (dpooled,) = _mm(
        "d_pool_mix", "nt", dyp, wp_full.astype(BF16), (S // tmp, nwin, 1),
        pl.BlockSpec((tmp, gw), lambda i, j, k: (i, j)), pl.BlockSpec((1, gw, gw), lambda i, j, k: (j, 0, 0)),
        [(_sds((S, W), F32), pl.BlockSpec((tmp, gw), lambda i, j, k: (i, j)))], (tmp, gw), 1, gw)
    tkp = _tile(S, 2048)
    gw_pool = _mm(
        "gw_pool", "tn", pooled, dyp, (nwin, 1, S // tkp),
        pl.BlockSpec((tkp, gw), lambda i, j, k: (k, i)), pl.BlockSpec((tkp, gw), lambda i, j, k: (k, i)),
        [(_sds((nwin, gw, gw), BF16), pl.BlockSpec((1, gw, gw), lambda i, j, k: (i, 0, 0)))],
        (gw, gw), 1, gw, stacked_out=True)[0]
    du_pool = pool_bwd(dpooled, gw)

    (dz,) = mm_nt("d_glu", dvt, wg_vg, BF16, 2 * NDEV)
    gw_vg = mm_tn("gw_glu", z, dvt, BF16, 2 * NDEV, 4)
    gw_pool_st = jnp.transpose(gw_pool.reshape(nwin, NDEV, gw // NDEV, gw), (1, 0, 2, 3))
    grads_b = [gw_out, gw_po, gw_pool_st, gw_vg.reshape(NDEV, 2, W, D // NDEV)]
    tok, wait_pair_b = pair_exchange("pair_exchange_mix", grads_b, 6)
    du_ssm, g_bt_re, g_bt_im, g_ct_re, g_ct_im, g_f, g_d, g_a = s5_bwd(
        proj, xsb_all, dz, zp, s5_params, nblk, after=[tok])
    grads_b, got_b = wait_pair_b(du_ssm)
    parts_b = [pair_sum("pair_sum_mix%d" % i, g, t, place) for i, (g, t) in enumerate(zip(grads_b, got_b))]
    tok, wait_chip_b = chip_exchange("chip_exchange_mix", parts_b, 7)

    dproj = jnp.concatenate([du_ssm, du_pool, dga, dgb_], axis=1)
    gw_in = mm_tn("gw_in", h1, dproj, BF16, NDEV, 1, after=[tok])
    tok, wait_pair_c = pair_exchange("pair_exchange_in", [gw_in], 8)
    (dh1,) = mm_nt("d_proj", dproj, wg_in, F32, 4, after=[tok])
    grads_c, got_c = wait_pair_c(dh1)
    parts_c = [pair_sum("pair_sum_in", grads_c[0], got_c[0], place)]
    tok, wait_chip_c = chip_exchange("chip_exchange_in", parts_c, 9)

    def e10(t, b):
        dh1v, xv, dxav = t
        xh, rs = _ln_stats(xv)
        return ([dxav + _ln_bwd(dh1v * (1.0 + b[0]), xh, rs)],
                [_colsum(dh1v * xh), _colsum(dh1v)])
    grad_x, d_sc1, d_sh1 = _rowwise("ln_mod1_bwd", e10, S, ts, [(dh1, D, 0), (x2d, D, 0), (dxa, D, 0)],
                                    [sc1], [(D, F32)], [D, D], after=[tok])

    g_b_re = jnp.transpose(g_bt_re.reshape(H, G, P), (1, 2, 0))
    g_b_im = jnp.transpose(g_bt_im.reshape(H, G, P), (1, 2, 0))
    g_c_re = jnp.transpose(g_ct_re.reshape(H, G, P), (1, 0, 2))
    g_c_im = jnp.transpose(g_ct_im.reshape(H, G, P), (1, 0, 2))
    d_ab = jnp.transpose(g_a.reshape(nblk, 2, GPB, P), (1, 0, 2, 3)).reshape(2, G, P)
    g_lr, g_li, g_ldt = s5_disc_bwd(lam_re[0], lam_im[0], log_dt[0].reshape(G, 1), d_ab,
                                    g_f.reshape(2, G, P))

    dmod = jnp.concatenate([d_sh1, d_sc1, d_g1, d_sh2, d_sc2, d_g2], axis=1)
    small_g = [dmod, g_lr, g_li, g_ldt, g_b_re, g_b_im, g_c_re, g_c_im, g_d, g_pscale,
               g_ln1g, g_ln1b, g_ln2g, g_ln2b, loss_acc]
    packed_g = _small_pack(small_g)
    (parts_all,) = seq_all_gather("gather_small", [packed_g], 10)
    glu_w = jnp.stack([w_glu_val[0], w_glu_gate[0]])
    glu_m = jnp.stack([m_w_glu_val[0], m_w_glu_gate[0]])
    glu_v = jnp.stack([v_w_glu_val[0], v_w_glu_gate[0]])
    wmv = [(w_ff2[0], m_w_ff2[0], v_w_ff2[0]), (w_ff1[0], m_w_ff1[0], v_w_ff1[0]),
           (w_out[0], m_w_out[0], v_w_out[0]), (w_pool_out[0], m_w_pool_out[0], v_w_pool_out[0]),
           (w_pool[0], m_w_pool[0], v_w_pool[0]), (glu_w, glu_m, glu_v)]
    _, got3_a = wait_chip_a(packed_g)
    upd = [adamw_sharded("adamw_%d" % i, g, p, t, w, m, v, place)
           for i, (g, p, t, (w, m, v)) in enumerate(zip(grads_a, got_a, got3_a, wmv[:2]))]
    _, got3_b = wait_chip_b(upd[-1][0])
    upd += [adamw_sharded("adamw_%d" % (2 + i), g, p, t, w, m, v, place)
            for i, (g, p, t, (w, m, v)) in enumerate(zip(grads_b, got_b, got3_b, wmv[2:]))]
    u_ff2, u_ff1, u_out, u_po, u_pool, u_glu = upd

    gsum = sum_small(parts_all, after=[upd[-1][0]])
    sg = _small_unpack(gsum, [t.shape for t in small_names] + [(1, LANES)])
    loss, sg = sg[-1][0, 0], sg[:-1]
    sd, sm, sv = adamw_natural(sg, small_names, small_m, small_v)

    nmod = 6 * D
    dmod_all = parts_all[:, :nmod // LANES, :].reshape(NDEV, nmod)
    c_all_t = jnp.transpose(c_all.reshape(NDEV, D))
    ada_out = adamw_ada(c_all_t, dmod_all, w_ada[0], m_w_ada[0], v_w_ada[0], my_dev)
    _, got3_c = wait_chip_c(ada_out[0])
    u_in = adamw_sharded("adamw_6", grads_c[0], got_c[0], got3_c[0], w_in[0], m_w_in[0], v_w_in[0], place)

    def pick(k):
        return [ada_out[k][None], sg_sd[k][0], u_in[k][None]] + [t for t in sg_sd[k][1:9]] + \
               [u_glu[k][0][None], u_glu[k][1][None], u_pool[k][None], sg_sd[k][9], u_po[k][None],
                u_out[k][None], sg_sd[k][10], sg_sd[k][11], u_ff1[k][None], u_ff2[k][None],
                sg_sd[k][12], sg_sd[k][13]]

    sg_sd = [sg, sd, sm, sv]
    return (loss, grad_x[None], *pick(0), *pick(1), *pick(2), *pick(3))
```

```python
import functools
import math

import jax
import jax.numpy as jnp
from jax import lax
from jax.experimental import pallas as pl
from jax.experimental.pallas import tpu as pltpu
from jax.experimental.pallas import tpu_sc as plsc

F32 = jnp.float32
BF16 = jnp.bfloat16
MESH = pl.DeviceIdType.MESH
NDEV = 8
NCHIP = 4

SSM_GROUP = 16
SSM_STATE = 64
GROUPS_PER_BLOCK = 8
POOL_WINDOWS = (2, 4, 8, 16)
LN_EPS = 1e-5
ALPHA = 2.0 ** 0.25
ADAM_LR, ADAM_B1, ADAM_B2, ADAM_EPS, ADAM_WD, ADAM_STEP = 0.001, 0.9, 0.999, 1e-08, 0.01, 10
SUBLANES = 8
LANES = 128
VMEM_LIMIT = 56 * 1024 * 1024


def _params(sem=None, vmem=VMEM_LIMIT):
    return pltpu.CompilerParams(dimension_semantics=sem, vmem_limit_bytes=vmem)


def _tile(n, pref):
    if n <= pref:
        return n
    t = 1 << (pref.bit_length() - 1)
    while n % t:
        t //= 2
    return t


def _cast_epi(vals, ex, outs):
    c = vals[0].shape[1]
    for s, v in enumerate(vals):
        outs[0][:, s * c:(s + 1) * c] = v.astype(outs[0].dtype)


ANY = pl.BlockSpec(memory_space=pl.ANY)


def _with_after(body, n_in, after):
    if not after:
        return body
    n_af = len(after)

    def wrapped(*refs):
        return body(*refs[:n_in], *refs[n_in + n_af:])
    return wrapped


def _mm(name, kind, a, b, grid, a_spec, b_spec, outs, acc_shape, nsub=1, c=None,
        pro=None, epi=None, extras=(), stacked_out=False, after=()):
    nk = grid[2]
    n_ex, n_out = len(extras), len(outs)

    def finish(vals, ex, out_refs):
        if epi is not None:
            epi(vals, ex, out_refs)
        elif stacked_out:
            for s, v in enumerate(vals):
                out_refs[0][s] = v.astype(out_refs[0].dtype)
        else:
            _cast_epi(vals, ex, out_refs)

    def body(*refs):
        mm_step(refs[0], refs[1], refs[2:2 + n_ex], refs[2 + n_ex:2 + n_ex + n_out], refs[-1])

    def mm_step(a_ref, b_ref, ex, out_refs, acc):
        k = pl.program_id(2)
        av = a_ref[...]
        if pro is not None:
            av = pro(av)
        if kind == "nn":
            prods = [jnp.dot(av, b_ref[s], preferred_element_type=F32) for s in range(nsub)]
        elif kind == "nt":
            t = None
            for s in range(nsub):
                d = lax.dot_general(av[:, s * c:(s + 1) * c], b_ref[s], (((1,), (1,)), ((), ())),
                                    preferred_element_type=F32)
                t = d if t is None else t + d
            prods = [t]
        else:
            t = lax.dot_general(av, b_ref[...], (((0,), (0,)), ((), ())), preferred_element_type=F32)
            prods = [t[:, s * c:(s + 1) * c] for s in range(nsub)] if stacked_out else [t]
        if nk == 1:
            finish(prods, ex, out_refs)
            return
        w = prods[0].shape[1]

        @pl.when(k == 0)
        def _():
            for s, p in enumerate(prods):
                acc[:, s * w:(s + 1) * w] = p

        @pl.when(jnp.logical_and(k > 0, k < nk - 1))
        def _():
            for s, p in enumerate(prods):
                acc[:, s * w:(s + 1) * w] += p

        @pl.when(k == nk - 1)
        def _():
            finish([acc[:, s * w:(s + 1) * w] + p for s, p in enumerate(prods)], ex, out_refs)

    return pl.pallas_call(
        _with_after(body, 2 + n_ex, after), name=name, grid=grid,
        in_specs=[a_spec, b_spec] + [e[1] for e in extras] + [ANY] * len(after),
        out_specs=[o[1] for o in outs],
        out_shape=[o[0] for o in outs],
        scratch_shapes=[pltpu.VMEM(acc_shape, F32)] if nk > 1 else [],
        compiler_params=_params(("parallel", "parallel", "arbitrary")),
    )(a, b, *[e[0] for e in extras], *after)


def _sds(shape, dtype):
    return jax.ShapeDtypeStruct(shape, dtype)


def mm_nn(name, a, b3, out_dtype, nsub, tm=1024, tk=2048, tn=None, pro=None, epi=None,
          extras=(), extra_outs=(), a_col0=0, after=()):
    M = a.shape[0]
    nb, K, cdim = b3.shape
    tm, tk = _tile(M, tm), _tile(K, tk)
    if nb == 1:
        tn = _tile(cdim, tn or 1024)
        nsub, c, nj = 1, tn, cdim // tn
        b_spec = pl.BlockSpec((1, tk, tn), lambda i, j, k: (0, k, j))
        N = cdim
    else:
        c, nj, tn = cdim, nb // nsub, nsub * cdim
        b_spec = pl.BlockSpec((nsub, tk, cdim), lambda i, j, k: (j, k, 0))
        N = nb * cdim
    kb0 = a_col0 // tk
    a_spec = pl.BlockSpec((tm, tk), lambda i, j, k: (i, kb0 + k))
    grid = (M // tm, nj, K // tk)
    o_spec = pl.BlockSpec((tm, tn), lambda i, j, k: (i, j))
    outs = [(_sds((M, N), out_dtype), o_spec)] + [(_sds((M, N), d), o_spec) for d in extra_outs]
    return _mm(name, "nn", a, b3, grid, a_spec, b_spec, outs, (tm, tn), nsub, c, pro, epi, extras,
               after=after)


def mm_nt(name, a, b3, out_dtype, nsub, tm=1024, tn=1024, epi=None, extras=(), extra_outs=(),
          after=()):
    M = a.shape[0]
    nb, N, cdim = b3.shape
    tm, tn = _tile(M, tm), _tile(N, tn)
    if nb == 1:
        tk = _tile(cdim, 2048)
        nsub, c, nk = 1, tk, cdim // tk
        b_spec = pl.BlockSpec((1, tn, tk), lambda i, j, k: (0, j, k))
    else:
        c, nk, tk = cdim, nb // nsub, nsub * cdim
        b_spec = pl.BlockSpec((nsub, tn, cdim), lambda i, j, k: (k, j, 0))
    a_spec = pl.BlockSpec((tm, tk), lambda i, j, k: (i, k))
    grid = (M // tm, N // tn, nk)
    o_spec = pl.BlockSpec((tm, tn), lambda i, j, k: (i, j))
    outs = [(_sds((M, N), out_dtype), o_spec)] + [(_sds((M, N), d), o_spec) for d in extra_outs]
    return _mm(name, "nt", a, b3, grid, a_spec, b_spec, outs, (tm, tn), nsub, c, None, epi, extras,
               after=after)


def mm_tn(name, a, b, out_dtype, nb, nsub, tma=1024, tk=2048, pro=None, a_col0=0, a_cols=None,
          after=()):
    S = a.shape[0]
    Ka = a_cols or a.shape[1]
    N = b.shape[1]
    tk = _tile(S, tk)
    if nsub == 0:
        tma, tn = _tile(Ka, tma), _tile(N, 1024)
        grid = (Ka // tma, N // tn, S // tk)
        ab0 = a_col0 // tma
        res = _mm(name, "tn", a, b, grid, pl.BlockSpec((tk, tma), lambda i, j, k: (k, ab0 + i)),
                  pl.BlockSpec((tk, tn), lambda i, j, k: (k, j)),
                  [(_sds((Ka, N), out_dtype), pl.BlockSpec((tma, tn), lambda i, j, k: (i, j)))],
                  (tma, tn), 1, tn, pro, None, (), after=after)[0]
        return res.reshape(nb, Ka // nb, N)
    else:
        c = N // nb
        tma = _tile(Ka, tma)
        grid = (Ka // tma, nb // nsub, S // tk)
        o_spec = pl.BlockSpec((nsub, tma, c), lambda i, j, k: (j, i, 0))
        out = _sds((nb, Ka, c), out_dtype)
        nsub_k = nsub
        tn = nsub * c
        b_spec = pl.BlockSpec((tk, tn), lambda i, j, k: (k, j))
    ab0 = a_col0 // tma
    a_spec = pl.BlockSpec((tk, tma), lambda i, j, k: (k, ab0 + i))
    return _mm(name, "tn", a, b, grid, a_spec, b_spec, [(out, o_spec)], (tma, tn), nsub_k, c,
               pro, None, (), stacked_out=True, after=after)[0]


def _rowwise(name, fn, S, ts, tiled, bcast, tiled_out, acc_out, after=()):
    nt, nb, no, na = len(tiled), len(bcast), len(tiled_out), len(acc_out)

    def body(*refs):
        tin = [r[...] for r in refs[:nt]]
        bin_ = [r[...] for r in refs[nt:nt + nb]]
        o_refs = refs[nt + nb:nt + nb + no]
        a_refs = refs[nt + nb + no:]
        touts, aouts = fn(tin, bin_)
        for r, v in zip(o_refs, touts):
            r[...] = v.astype(r.dtype)
        i = pl.program_id(0)

        @pl.when(i == 0)
        def _():
            for r, v in zip(a_refs, aouts):
                r[...] = v

        @pl.when(i > 0)
        def _():
            for r, v in zip(a_refs, aouts):
                r[...] += v

    in_specs = [pl.BlockSpec((ts, w), functools.partial(lambda i, cb: (i, cb), cb=cb))
                for (_, w, cb) in tiled]
    in_specs += [pl.BlockSpec(b.shape, lambda i: (0, 0)) for b in bcast]
    out_specs = [pl.BlockSpec((ts, w), lambda i: (i, 0)) for (w, _) in tiled_out]
    out_specs += [pl.BlockSpec((1, w), lambda i: (0, 0)) for w in acc_out]
    out_shape = [_sds((S, w), d) for (w, d) in tiled_out] + [_sds((1, w), F32) for w in acc_out]
    return pl.pallas_call(
        _with_after(body, nt + nb, after), name=name, grid=(S // ts,),
        in_specs=in_specs + [ANY] * len(after), out_specs=out_specs,
        out_shape=out_shape, compiler_params=_params(("arbitrary",)),
    )(*[t[0] for t in tiled], *bcast, *after)


def _ln_full(v):
    mu = jnp.mean(v, axis=-1, keepdims=True)
    vc = v - mu
    var = jnp.mean(vc * vc, axis=-1, keepdims=True)
    rstd = lax.rsqrt(var + LN_EPS)
    return vc * rstd, mu, rstd


def _ln_stats(v):
    xhat, _, rstd = _ln_full(v)
    return xhat, rstd


def _ln_bwd(dxhat, xhat, rstd):
    return rstd * (dxhat - jnp.mean(dxhat, axis=-1, keepdims=True)
                   - xhat * jnp.mean(dxhat * xhat, axis=-1, keepdims=True))


def _colsum(v):
    return jnp.sum(v, axis=0, keepdims=True)


def _sigmoid(v):
    return 1.0 / (1.0 + jnp.exp(-v))


_GELU_C = math.sqrt(2.0 / math.pi)


def _gelu(v):
    return 0.5 * v * (1.0 + jnp.tanh(_GELU_C * (v + 0.044715 * v * v * v)))


def _gelu_grad(v):
    t = jnp.tanh(_GELU_C * (v + 0.044715 * v * v * v))
    return 0.5 * (1.0 + t) + 0.5 * v * (1.0 - t * t) * _GELU_C * (1.0 + 3 * 0.044715 * v * v)


def _disc(lr, li, ldt):
    dt = jnp.exp(ldt)
    mag = jnp.exp(lr * dt)
    ang = li * dt
    ab_re = mag * jnp.cos(ang)
    ab_im = mag * jnp.sin(ang)
    num_re = ab_re - 1.0
    num_im = ab_im
    den = lr * lr + li * li
    f_re = (num_re * lr + num_im * li) / den
    f_im = (num_im * lr - num_re * li) / den
    return ab_re, ab_im, f_re, f_im


def _cmul(ar, ai, br, bi):
    return ar * br - ai * bi, ar * bi + ai * br


SCAN_FOLD = 4
NCONST = 18


def s5_disc(lam_re, lam_im, log_dt):
    G, P = lam_re.shape

    def body(lr_ref, li_ref, ldt_ref, f_ref, k_ref):
        ab_re, ab_im, f_re, f_im = _disc(lr_ref[...], li_ref[...], ldt_ref[...])
        f_ref[0] = f_re
        f_ref[1] = f_im
        fr, fi = ab_re, ab_im
        for _ in range(SCAN_FOLD - 1):
            fr, fi = _cmul(fr, fi, ab_re, ab_im)
        pr, pi = [fr], [fi]
        for _ in range(SUBLANES - 1):
            nr, ni = _cmul(pr[-1], pi[-1], fr, fi)
            pr.append(nr)
            pi.append(ni)
        zero = jnp.zeros_like(ab_re)
        for r in range(SUBLANES):
            k_ref[16, r] = ab_re
            k_ref[17, r] = ab_im
        for n, sh in enumerate((1, 2, 4)):
            for r in range(SUBLANES):
                k_ref[2 * n, r] = pr[sh - 1] if r >= sh else zero
                k_ref[2 * n + 1, r] = pi[sh - 1] if r >= sh else zero
                k_ref[8 + 2 * n, r] = pr[sh - 1] if r + sh < SUBLANES else zero
                k_ref[8 + 2 * n + 1, r] = -pi[sh - 1] if r + sh < SUBLANES else zero
        for r in range(SUBLANES):
            k_ref[6, r] = pr[r]
            k_ref[7, r] = pi[r]
            k_ref[14, r] = pr[SUBLANES - 1 - r]
            k_ref[15, r] = -pi[SUBLANES - 1 - r]

    vm = pl.BlockSpec(memory_space=pltpu.VMEM)
    return pl.pallas_call(
        body, name="s5_disc", in_specs=[vm, vm, vm], out_specs=[vm, vm],
        out_shape=[_sds((2, G, P), F32), _sds((NCONST, SUBLANES, G, P), F32)],
    )(lam_re, lam_im, log_dt)


def s5_disc_bwd(lam_re, lam_im, log_dt, d_ab, d_f):
    G, P = lam_re.shape

    def body(lr_ref, li_ref, ldt_ref, dab_ref, df_ref, glr_ref, gli_ref, gdt_ref):
        _, vjp = jax.vjp(_disc, lr_ref[...], li_ref[...], ldt_ref[...])
        glr, gli, gdt = vjp((dab_ref[0], dab_ref[1], df_ref[0], df_ref[1]))
        glr_ref[...] = glr
        gli_ref[...] = gli
        gdt_ref[...] = gdt

    vm = pl.BlockSpec(memory_space=pltpu.VMEM)
    return pl.pallas_call(
        body, name="s5_disc_bwd", in_specs=[vm] * 5, out_specs=[vm] * 3,
        out_shape=[_sds((G, P), F32), _sds((G, P), F32), _sds((G, 1), F32)],
    )(lam_re, lam_im, log_dt, d_ab, d_f)


def _group_mask(cw, nst):
    row = lax.broadcasted_iota(jnp.int32, (cw, 2 * nst), 0) // SSM_GROUP
    col = (lax.broadcasted_iota(jnp.int32, (cw, 2 * nst), 1) % nst) // SSM_STATE
    return row == col


def _spread(t, mask):
    reps = mask.shape[0] // t.shape[0]
    return jnp.where(mask, jnp.tile(t, (reps, 1)), 0.0).astype(BF16)


def _gather_groups(t, mask):
    t = jnp.where(mask, t, 0.0)
    out = t[0:SSM_GROUP]
    for g in range(1, t.shape[0] // SSM_GROUP):
        out = out + t[g * SSM_GROUP:(g + 1) * SSM_GROUP]
    return out


def _s5_operands(f_ref, br_ref, bi_ref, cr_ref, ci_ref, mask):
    fr, fi = f_ref[0], f_ref[1]
    br, bi = br_ref[...], bi_ref[...]
    bm = _spread(jnp.concatenate([fr * br - fi * bi, fr * bi + fi * br], axis=1), mask)
    cm = _spread(jnp.concatenate([cr_ref[...], -ci_ref[...]], axis=1), mask)
    return bm, cm


def _planes_put(ref, val):
    for c in range(ref.shape[0]):
        ref[c] = val[:, c * LANES:(c + 1) * LANES]


def _planes_get(ref):
    return jnp.concatenate([ref[c] for c in range(ref.shape[0])], axis=1)


def _rows_ld(ref, start, lo, hi):
    rows = pl.ds(start, SUBLANES, stride=SCAN_FOLD)
    return jnp.concatenate([ref[c, rows, :] for c in range(lo // LANES, hi // LANES)], axis=1)


def _rows_st(ref, start, lo, val):
    rows = pl.ds(start, SUBLANES, stride=SCAN_FOLD)
    for k in range(val.shape[1] // LANES):
        ref[lo // LANES + k, rows, :] = val[:, k * LANES:(k + 1) * LANES]


def _phases(ref, base, lo, hi):
    return [_rows_ld(ref, base + j, lo, hi) for j in range(SCAN_FOLD)]


def _row_bcast(v, r):
    return jnp.broadcast_to(v[r:r + 1, :], v.shape)


def _scan_fwd(xs, k_ref, nst):
    m = SCAN_FOLD
    ngroup = xs.shape[1] // (SUBLANES * m)
    row = lax.broadcasted_iota(jnp.int32, (SUBLANES, nst), 0)

    def step(t, carry):
        cr, ci = carry
        base = pl.multiple_of(t * (SUBLANES * m), SUBLANES * m)
        ar, ai = k_ref[16], k_ref[17]
        pr, pi = _phases(xs, base, 0, nst), _phases(xs, base, nst, 2 * nst)
        vr, vi = pr[0], pi[0]
        for j in range(1, m):
            vr, vi = pr[j] + ar * vr - ai * vi, pi[j] + ar * vi + ai * vr
        for n, sh in enumerate((1, 2, 4)):
            sr = pltpu.roll(vr, sh, 0)
            si = pltpu.roll(vi, sh, 0)
            mr, mi = k_ref[2 * n], k_ref[2 * n + 1]
            vr, vi = vr + mr * sr - mi * si, vi + mr * si + mi * sr
        qr, qi = k_ref[6], k_ref[7]
        vr, vi = vr + qr * cr - qi * ci, vi + qr * ci + qi * cr
        _rows_st(xs, base + m - 1, 0, vr)
        _rows_st(xs, base + m - 1, nst, vi)
        xr = jnp.where(row == 0, cr, pltpu.roll(vr, 1, 0))
        xi = jnp.where(row == 0, ci, pltpu.roll(vi, 1, 0))
        for j in range(m - 1):
            xr, xi = pr[j] + ar * xr - ai * xi, pi[j] + ar * xi + ai * xr
            _rows_st(xs, base + j, 0, xr)
            _rows_st(xs, base + j, nst, xi)
        return _row_bcast(vr, SUBLANES - 1), _row_bcast(vi, SUBLANES - 1)

    zero = jnp.zeros((SUBLANES, nst), F32)
    lax.fori_loop(0, ngroup, step, (zero, zero))


def _scan_bwd(g, xs, k_ref, nst):
    m = SCAN_FOLD
    ngroup = g.shape[1] // (SUBLANES * m)
    row = lax.broadcasted_iota(jnp.int32, (SUBLANES, nst), 0)

    def step(tt, carry):
        cr, ci, dar, dai = carry
        t = ngroup - 1 - tt
        base = pl.multiple_of(t * (SUBLANES * m), SUBLANES * m)
        ar, ai = k_ref[16], -k_ref[17]
        dr, di = _phases(g, base, 0, nst), _phases(g, base, nst, 2 * nst)
        wr, wi = dr[m - 1], di[m - 1]
        for j in range(m - 2, -1, -1):
            wr, wi = dr[j] + ar * wr - ai * wi, di[j] + ar * wi + ai * wr
        for n, sh in enumerate((1, 2, 4)):
            sr = pltpu.roll(wr, SUBLANES - sh, 0)
            si = pltpu.roll(wi, SUBLANES - sh, 0)
            mr, mi = k_ref[8 + 2 * n], k_ref[8 + 2 * n + 1]
            wr, wi = wr + mr * sr - mi * si, wi + mr * si + mi * sr
        qr, qi = k_ref[14], k_ref[15]
        wr, wi = wr + qr * cr - qi * ci, wi + qr * ci + qi * cr
        gr, gi = [None] * m, [None] * m
        gr[0], gi[0] = wr, wi
        nr = jnp.where(row == SUBLANES - 1, cr, pltpu.roll(wr, SUBLANES - 1, 0))
        ni = jnp.where(row == SUBLANES - 1, ci, pltpu.roll(wi, SUBLANES - 1, 0))
        for j in range(m - 1, 0, -1):
            nr, ni = dr[j] + ar * nr - ai * ni, di[j] + ar * ni + ai * nr
            gr[j], gi[j] = nr, ni
        for j in range(m):
            _rows_st(g, base + j, 0, gr[j])
            _rows_st(g, base + j, nst, gi[j])
        xr, xi = _phases(xs, base, 0, nst), _phases(xs, base, nst, 2 * nst)
        pbase = pl.multiple_of(jnp.maximum(t - 1, 0) * (SUBLANES * m), SUBLANES * m)
        live = (t > 0).astype(F32)
        lr = _row_bcast(_rows_ld(xs, pbase + m - 1, 0, nst), SUBLANES - 1) * live
        li = _row_bcast(_rows_ld(xs, pbase + m - 1, nst, 2 * nst), SUBLANES - 1) * live
        xmr = [jnp.where(row == 0, lr, pltpu.roll(xr[m - 1], 1, 0))] + xr[:m - 1]
        xmi = [jnp.where(row == 0, li, pltpu.roll(xi[m - 1], 1, 0))] + xi[:m - 1]
        for j in range(m):
            dar = dar + gr[j] * xmr[j] + gi[j] * xmi[j]
            dai = dai + gi[j] * xmr[j] - gr[j] * xmi[j]
        return _row_bcast(wr, 0), _row_bcast(wi, 0), dar, dai

    zero = jnp.zeros((SUBLANES, nst), F32)
    _, _, dar, dai = lax.fori_loop(0, ngroup, step, (zero, zero, zero, zero))
    return _colsum(dar), _colsum(dai)


def _s5_param_specs(cw, nst):
    hp = pl.BlockSpec((SSM_GROUP, nst), lambda b: (0, b))
    return [pl.BlockSpec((2, 1, nst), lambda b: (0, 0, b)), hp, hp, hp, hp,
            pl.BlockSpec((1, cw), lambda b: (0, b)),
            pl.BlockSpec((NCONST, SUBLANES, nst), lambda b: (0, 0, b))]


def s5_fwd(proj, params, nb):
    S = proj.shape[0]
    nst = params[1].shape[1] // nb
    cw = nst // SSM_STATE * SSM_GROUP

    def body(u_ref, f_ref, br_ref, bi_ref, cr_ref, ci_ref, d_ref, k_ref, z_ref, xsb_ref, zp_ref, xs):
        bm, cm = _s5_operands(f_ref, br_ref, bi_ref, cr_ref, ci_ref, _group_mask(cw, nst))
        u = u_ref[...]
        _planes_put(xs, jnp.dot(u.astype(BF16), bm, preferred_element_type=F32))
        _scan_fwd(xs, k_ref, nst)
        xsb = _planes_get(xs).astype(BF16)
        xsb_ref[...] = xsb
        y = lax.dot_general(xsb, cm, (((1,), (1,)), ((), ())), preferred_element_type=F32)
        y = y + d_ref[...] * u
        z_ref[...] = _gelu(y).astype(BF16)
        zp_ref[...] = _gelu_grad(y).astype(BF16)

    return pl.pallas_call(
        body, name="s5_fwd", grid=(nb,),
        in_specs=[pl.BlockSpec((S, cw), lambda b: (0, b))] + _s5_param_specs(cw, nst),
        out_specs=[pl.BlockSpec((S, cw), lambda b: (0, b)), pl.BlockSpec((S, 2 * nst), lambda b: (0, b)),
                   pl.BlockSpec((S, cw), lambda b: (0, b))],
        out_shape=[_sds((S, nb * cw), BF16), _sds((S, nb * 2 * nst), BF16), _sds((S, nb * cw), BF16)],
        scratch_shapes=[pltpu.VMEM((2 * nst // LANES, S, LANES), F32)],
        compiler_params=_params(("arbitrary",)),
    )(proj, *params)


def s5_bwd(proj, xsb_all, dz, zp, params, nb, after=()):
    S = proj.shape[0]
    nst = params[1].shape[1] // nb
    cw = nst // SSM_STATE * SSM_GROUP

    def body(u_ref, xsb_ref, dz_ref, zp_ref, f_ref, br_ref, bi_ref, cr_ref, ci_ref, d_ref, k_ref,
             du_ref, gbr_ref, gbi_ref, gcr_ref, gci_ref, gf_ref, gd_ref, ga_ref, xs, g):
        mask = _group_mask(cw, nst)
        bm, cm = _s5_operands(f_ref, br_ref, bi_ref, cr_ref, ci_ref, mask)
        u = u_ref[...]
        ub = u.astype(BF16)
        d = d_ref[...]
        xsb = xsb_ref[...]
        _planes_put(xs, xsb.astype(F32))
        dy = dz_ref[...].astype(F32) * zp_ref[...].astype(F32)
        gd_ref[...] = _colsum(dy * u)
        dyb = dy.astype(BF16)
        gc = _gather_groups(lax.dot_general(dyb, xsb, (((0,), (0,)), ((), ())),
                                            preferred_element_type=F32), mask)
        gcr_ref[...] = gc[:, :nst]
        gci_ref[...] = -gc[:, nst:]
        _planes_put(g, jnp.dot(dyb, cm, preferred_element_type=F32))
        ar, ai = _scan_bwd(g, xs, k_ref, nst)
        ga_ref[0, 0:1, :] = ar
        ga_ref[0, 1:2, :] = ai
        gb = _planes_get(g).astype(BF16)
        du = lax.dot_general(gb, bm, (((1,), (1,)), ((), ())), preferred_element_type=F32) + d * dy
        du_ref[...] = du.astype(BF16)
        gbb = _gather_groups(lax.dot_general(ub, gb, (((0,), (0,)), ((), ())),
                                             preferred_element_type=F32), mask)
        dr, di = gbb[:, :nst], gbb[:, nst:]
        fr, fi = f_ref[0], f_ref[1]
        br, bi = br_ref[...], bi_ref[...]
        gbr_ref[...] = fr * dr + fi * di
        gbi_ref[...] = fr * di - fi * dr
        gf_ref[0] = _colsum(dr * br + di * bi)
        gf_ref[1] = _colsum(di * br - dr * bi)

    hp = pl.BlockSpec((SSM_GROUP, nst), lambda b: (0, b))
    hp_sds = _sds((SSM_GROUP, nb * nst), F32)
    return pl.pallas_call(
        _with_after(body, 11, after), name="s5_bwd", grid=(nb,),
        in_specs=[pl.BlockSpec((S, cw), lambda b: (0, b)),
                  pl.BlockSpec((S, 2 * nst), lambda b: (0, b)),
                  pl.BlockSpec((S, cw), lambda b: (0, b)),
                  pl.BlockSpec((S, cw), lambda b: (0, b))] + _s5_param_specs(cw, nst)
        + [ANY] * len(after),
        out_specs=[pl.BlockSpec((S, cw), lambda b: (0, b)), hp, hp, hp, hp,
                   pl.BlockSpec((2, 1, nst), lambda b: (0, 0, b)),
                   pl.BlockSpec((1, cw), lambda b: (0, b)),
                   pl.BlockSpec((1, 2, nst), lambda b: (b, 0, 0))],
        out_shape=[_sds((S, nb * cw), BF16), hp_sds, hp_sds, hp_sds, hp_sds,
                   _sds((2, 1, nb * nst), F32), _sds((1, nb * cw), F32), _sds((nb, 2, nst), F32)],
        scratch_shapes=[pltpu.VMEM((2 * nst // LANES, S, LANES), F32)] * 2,
        compiler_params=_params(("arbitrary",)),
    )(proj, xsb_all, dz, zp, *params, *after)


def _shift_rows(v, k, row, down):
    n = v.shape[0]
    if down:
        return jnp.where(row >= k, pltpu.roll(v, k, 0), 0.0)
    return jnp.where(row < n - k, pltpu.roll(v, n - k, 0), 0.0)


def _window(v, gi, row, down):
    sums = []
    s = v
    for k in (1, 2, 4, 8):
        s = s + _shift_rows(s, k, row, down)
        sums.append(s)
    out = sums[3]
    for n in (2, 1, 0):
        out = jnp.where(gi == n, sums[n], out)
    return out


def pool_fwd(proj, col0, width, gw):
    S = proj.shape[0]
    cb0 = col0 // gw

    def body(u_ref, o_ref):
        gi = pl.program_id(0)
        u = u_ref[...]
        row = lax.broadcasted_iota(jnp.int32, u.shape, 0)
        w = jnp.left_shift(2, gi)
        count = jnp.minimum(row + 1, w).astype(F32)
        o_ref[...] = (_window(u, gi, row, True) / count - u).astype(BF16)

    return pl.pallas_call(
        body, name="pool_fwd", grid=(len(POOL_WINDOWS),),
        in_specs=[pl.BlockSpec((S, gw), lambda g: (0, cb0 + g))],
        out_specs=pl.BlockSpec((S, gw), lambda g: (0, g)),
        out_shape=_sds((S, width), BF16), compiler_params=_params(("arbitrary",)),
    )(proj)


def pool_bwd(dpooled, gw):
    S, width = dpooled.shape

    def body(d_ref, o_ref):
        gi = pl.program_id(0)
        d = d_ref[...]
        row = lax.broadcasted_iota(jnp.int32, d.shape, 0)
        w = jnp.left_shift(2, gi)
        count = jnp.minimum(row + 1, w).astype(F32)
        o_ref[...] = (_window(d / count, gi, row, False) - d).astype(BF16)

    return pl.pallas_call(
        body, name="pool_bwd", grid=(len(POOL_WINDOWS),),
        in_specs=[pl.BlockSpec((S, gw), lambda g: (0, g))],
        out_specs=pl.BlockSpec((S, gw), lambda g: (0, g)),
        out_shape=_sds((S, width), BF16), compiler_params=_params(("arbitrary",)),
    )(dpooled)


def _place():
    x, y, c = lax.axis_index("x"), lax.axis_index("y"), lax.axis_index("c")
    chips = [(1 - x, y), (x, 1 - y), (1 - x, 1 - y)]
    return x, y, c, chips


HBM = pl.BlockSpec(memory_space=pltpu.HBM)


def _routed_gather_body(n):
    def body(*refs):
        ins, outs = refs[:n], refs[n:2 * n]
        send_sems, recv_sems, local_sems = refs[2 * n:]
        x, y, c, (xn, yn, dg) = _place()
        me, sibling = (x, y, c), (x, y, 1 - c)
        barrier = pltpu.get_barrier_semaphore()
        for peer in (sibling, (*xn, c), (*yn, c)):
            pl.semaphore_signal(barrier, inc=1, device_id=peer, device_id_type=MESH)
        pl.semaphore_wait(barrier, 3)

        def piece(i, p, h):
            rows = ins[i].shape[0] // 2
            return outs[i].at[4 * p[0] + 2 * p[1] + p[2], pl.ds(h * rows, rows)]

        def copy(i, k, src, dst, to):
            return pltpu.make_async_remote_copy(src_ref=src, dst_ref=dst, send_sem=send_sems.at[i, k],
                                                recv_sem=recv_sems.at[i, k], device_id=to,
                                                device_id_type=MESH)

        started = []

        def go(cp):
            cp.start()
            started.append(cp)

        for i in range(n):
            rows = ins[i].shape[0] // 2
            for h in range(2):
                own = ins[i].at[pl.ds(h * rows, rows)]
                go(copy(i, 1 + h, own, piece(i, me, h), (*xn, c)))
                go(copy(i, 3 + h, own, piece(i, me, h), (*yn, c)))
        for i in range(n):
            go(copy(i, 0, ins[i], outs[i].at[4 * x + 2 * y + c], sibling))
        mine = [pltpu.make_async_copy(ins[i], outs[i].at[4 * x + 2 * y + c], local_sems.at[i])
                for i in range(n)]
        for cp in mine:
            cp.start()
        for i in range(n):
            for k, chip, h, onward, ksib in ((1, xn, 0, (5, yn), 7), (4, yn, 1, (6, xn), 10),
                                            (2, xn, 1, None, 8), (3, yn, 0, None, 9),
                                            (5, dg, 0, None, 11), (6, dg, 1, None, 12)):
                got = piece(i, (*chip, c), h)
                copy(i, k, got, got, me).wait_recv()
                if onward is not None:
                    go(copy(i, onward[0], got, got, (*onward[1], c)))
                go(copy(i, ksib, got, got, sibling))
        for i in range(n):
            block = outs[i].at[4 * x + 2 * y + 1 - c]
            copy(i, 0, block, block, me).wait_recv()
            for ksib, chip, h in ((7, xn, 0), (10, yn, 1), (8, xn, 1), (9, yn, 0), (11, dg, 0), (12, dg, 1)):
                got = piece(i, (*chip, 1 - c), h)
                copy(i, ksib, got, got, me).wait_recv()
        for cp in started:
            cp.wait_send()
        for cp in mine:
            cp.wait()

    return body


def _on_sequencer(name, body, arrays, out_sds, sems, collective_id):
    ins = [jax.new_ref(a, memory_space=pltpu.MemorySpace.HBM) for a in arrays]
    outs = [jax.empty_ref(s, memory_space=pltpu.MemorySpace.HBM) for s in out_sds]

    @pl.kernel(mesh=plsc.ScalarSubcoreMesh(axis_name="sequencer", num_cores=1), name=name,
               scratch_types=tuple(sems),
               compiler_params=pltpu.CompilerParams(collective_id=collective_id))
    def launch(*sem_refs):
        body(*ins, *outs, *sem_refs)

    launch()
    return [o[...] for o in outs]


def seq_all_gather(name, shards, collective_id):
    n = len(shards)
    return _on_sequencer(
        name, _routed_gather_body(n), shards, [_sds((NDEV,) + s.shape, s.dtype) for s in shards],
        [pltpu.SemaphoreType.DMA((n, 13)), pltpu.SemaphoreType.DMA((n, 13)),
         pltpu.SemaphoreType.DMA((n,))], collective_id)


def pair_exchange(name, grads, collective_id):
    def plan(srcs, lands):
        x, y, c, _ = _place()
        return ([(i, q, srcs[i].at[2 * q + 1 - c], lands[i].at[q], (x, y, 1 - c))
                 for i in range(len(srcs)) for q in range(NCHIP)], [(x, y, 1 - c)])

    return _split_exchange(name, grads, [_sds((NCHIP,) + g.shape[1:], g.dtype) for g in grads],
                           plan, NCHIP, collective_id)


SEM = pl.BlockSpec(memory_space=pltpu.SEMAPHORE)


def _split_exchange(name, srcs, land_sds, plan, ncopy, collective_id):
    n = len(srcs)
    nsem = n * ncopy
    effect = pltpu.SideEffectType.DATAFLOW_SIDE_EFFECTING

    def descriptors(src_refs, land_refs, send_sems, recv_sems):
        copies, peers = plan(src_refs, land_refs)
        return [pltpu.make_async_remote_copy(src_ref=s, dst_ref=d, send_sem=send_sems[i * ncopy + k],
                                             recv_sem=recv_sems[i * ncopy + k], device_id=to,
                                             device_id_type=MESH) for (i, k, s, d, to) in copies], peers

    def start_body(*refs):
        src_refs, land_refs = refs[:n], refs[n:2 * n]
        send_sems, recv_sems = refs[2 * n:2 * n + nsem], refs[2 * n + nsem:2 * n + 2 * nsem]
        token = refs[-1]
        cps, peers = descriptors(src_refs, land_refs, send_sems, recv_sems)
        barrier = pltpu.get_barrier_semaphore()
        for peer in peers:
            pl.semaphore_signal(barrier, inc=1, device_id=peer, device_id_type=MESH)
        pl.semaphore_wait(barrier, len(peers))
        for cp in cps:
            cp.start()
        token[...] = jnp.zeros_like(token)

    lands = [pltpu.with_memory_space_constraint(lax.empty(s.shape, s.dtype), pltpu.HBM) for s in land_sds]
    srcs = [pltpu.with_memory_space_constraint(s, pltpu.HBM) for s in srcs]
    res = pl.pallas_call(
        start_body, name=name + "_start",
        out_shape=(pltpu.SemaphoreType.DMA(()),) * (2 * nsem)
        + tuple(pltpu.HBM(s.shape, s.dtype) for s in srcs)
        + tuple(pltpu.HBM(s.shape, s.dtype) for s in land_sds) + (_sds((SUBLANES, LANES), F32),),
        in_specs=[HBM] * (2 * n),
        out_specs=(SEM,) * (2 * nsem) + (HBM,) * (2 * n) + (pl.BlockSpec(memory_space=pltpu.VMEM),),
        input_output_aliases={i: 2 * nsem + i for i in range(2 * n)},
        compiler_params=pltpu.CompilerParams(has_side_effects=effect, collective_id=collective_id),
    )(*srcs, *lands)
    sems = res[:2 * nsem]
    thru = res[2 * nsem:2 * nsem + 2 * n]
    token = res[-1]

    def wait(after):
        def wait_body(*refs):
            src_refs, land_refs = refs[:n], refs[n:2 * n]
            cps, _ = descriptors(src_refs, land_refs, refs[2 * n:2 * n + nsem],
                                 refs[2 * n + nsem:2 * n + 2 * nsem])
            for cp in cps:
                cp.wait_send()
            for cp in cps:
                cp.wait_recv()

        out = pl.pallas_call(
            wait_body, name=name + "_wait",
            out_shape=tuple(pltpu.HBM(s.shape, s.dtype) for s in srcs)
            + tuple(pltpu.HBM(s.shape, s.dtype) for s in land_sds),
            in_specs=[HBM] * (2 * n) + [SEM] * (2 * nsem) + [pl.BlockSpec(memory_space=pl.ANY)],
            out_specs=(HBM,) * (2 * n),
            input_output_aliases={i: i for i in range(2 * n)},
            compiler_params=pltpu.CompilerParams(has_side_effects=effect),
        )(*thru, *sems, after)
        return list(out[:n]), list(out[n:])

    return token, wait


def pair_sum(name, grad, got, place):
    shp = grad.shape[1:]
    r, cdim = shp[-2], shp[-1]
    lead = int(math.prod(shp[:-2])) if len(shp) > 2 else 1
    g5 = grad.reshape(NCHIP, 2, lead * r, cdim)
    t4 = got.reshape(NCHIP, lead * r, cdim)
    R = lead * r
    tr = _tile(R, max(16, (1 << 19) // cdim))

    def body(p_ref, g_ref, t_ref, o_ref):
        o_ref[...] = (g_ref[0].astype(F32) + t_ref[...].astype(F32)).astype(o_ref.dtype)

    out = pl.pallas_call(
        body, name=name,
        grid_spec=pltpu.PrefetchScalarGridSpec(
            num_scalar_prefetch=1, grid=(NCHIP - 1, R // tr),
            in_specs=[pl.BlockSpec((1, 1, tr, cdim), lambda j, i, p: (p[1] ^ (j + 1), p[0], i, 0)),
                      pl.BlockSpec((1, tr, cdim), lambda j, i, p: (p[1] ^ (j + 1), i, 0))],
            out_specs=pl.BlockSpec((1, tr, cdim), lambda j, i, p: (p[1] ^ (j + 1), i, 0))),
        out_shape=_sds((NCHIP, R, cdim), grad.dtype),
        compiler_params=_params(("parallel", "parallel")),
    )(place, g5, t4)
    return out


def chip_exchange(name, parts, collective_id):
    def plan(srcs, lands):
        x, y, c, chips = _place()
        return ([(i, j, srcs[i].at[2 * chip[0] + chip[1]], lands[i].at[j], (*chip, c))
                 for i in range(len(srcs)) for j, chip in enumerate(chips)],
                [(*chip, c) for chip in chips])

    return _split_exchange(name, parts, [_sds((3,) + p.shape[1:], p.dtype) for p in parts],
                           plan, 3, collective_id)


def ada_fwd(c_row, w_ada, b_ada):
    D, cols = w_ada.shape

    def body(c_ref, w_ref, b_ref, mod_ref, call_ref, act8, part, s1, r1, s2, r2):
        x, y, c, _ = _place()
        me = 4 * x + 2 * y + c
        call_ref[me] = c_ref[...]
        cps = []
        for k in range(1, NDEV):
            to = (x ^ (k >> 2), y ^ ((k >> 1) & 1), c ^ (k & 1))
            cps.append(pltpu.make_async_remote_copy(
                src_ref=c_ref, dst_ref=call_ref.at[me], send_sem=s1.at[k - 1],
                recv_sem=r1.at[k - 1], device_id=to, device_id_type=MESH))
            cps[-1].start()
        for cp in cps:
            cp.wait()
        for b in range(NDEV):
            act8[b:b + 1, :] = call_ref[b]
        cv = act8[...]
        act = (cv * _sigmoid(cv)).astype(BF16)
        res = jnp.dot(act, w_ref[...].astype(BF16), preferred_element_type=F32)
        for b in range(NDEV):
            part[b] = res[b:b + 1, :]
        mod_ref[me] = part[me]
        cps = []
        for k in range(1, NDEV):
            to = (x ^ (k >> 2), y ^ ((k >> 1) & 1), c ^ (k & 1))
            dst = 4 * to[0] + 2 * to[1] + to[2]
            cps.append(pltpu.make_async_remote_copy(
                src_ref=part.at[dst], dst_ref=mod_ref.at[me], send_sem=s2.at[k - 1],
                recv_sem=r2.at[k - 1], device_id=to, device_id_type=MESH))
            cps[-1].start()
        for cp in cps:
            cp.wait()
        for b in range(NDEV):
            mod_ref[b] = mod_ref[b] + b_ref[b]

    vm = pl.BlockSpec(memory_space=pltpu.VMEM)
    return pl.pallas_call(
        body, name="ada_fwd", in_specs=[vm, vm, vm], out_specs=[vm, vm],
        out_shape=[_sds((NDEV, 1, cols), F32), _sds((NDEV, 1, D), F32)],
        scratch_shapes=[pltpu.VMEM((NDEV, D), F32), pltpu.VMEM((NDEV, 1, cols), F32),
                        pltpu.SemaphoreType.DMA((NDEV - 1,)), pltpu.SemaphoreType.DMA((NDEV - 1,)),
                        pltpu.SemaphoreType.DMA((NDEV - 1,)), pltpu.SemaphoreType.DMA((NDEV - 1,))],
        compiler_params=pltpu.CompilerParams(vmem_limit_bytes=VMEM_LIMIT),
    )(c_row, w_ada, b_ada.reshape(NDEV, 1, cols))


def _adamw_math(g, w, m, v):
    m2 = ADAM_B1 * m + (1.0 - ADAM_B1) * g
    v2 = ADAM_B2 * v + (1.0 - ADAM_B2) * (g * g)
    m_hat = m2 / (1.0 - ADAM_B1 ** ADAM_STEP)
    v_hat = v2 / (1.0 - ADAM_B2 ** ADAM_STEP)
    delta = -ADAM_LR * (m_hat / (jnp.sqrt(v_hat) + ADAM_EPS) + ADAM_WD * w)
    return delta, m2, v2


def adamw_sharded(name, grad8, pair4, got3, w, m, v, place, after=()):
    shape = w.shape
    cdim = shape[-1]
    R = int(math.prod(shape[:-1]))
    w2, m2, v2 = (t.reshape(R, cdim) for t in (w, m, v))
    tr = _tile(R, max(16, (1 << 18) // cdim))

    def body(q_ref, own_ref, sib_ref, t_ref, w_ref, m_ref, v_ref, g_out, d_out, m_out, v_out):
        g = own_ref[0].astype(F32) + sib_ref[0].astype(F32)
        for j in range(3):
            g = g + t_ref[j].astype(F32)
        d, mn, vn = _adamw_math(g, w_ref[...], m_ref[...], v_ref[...])
        g_out[...] = g
        d_out[...] = d
        m_out[...] = mn
        v_out[...] = vn

    spec = pl.BlockSpec((tr, cdim), lambda i, qr: (i, 0))
    outs = pl.pallas_call(
        _with_after(body, 7, after), name=name,
        grid_spec=pltpu.PrefetchScalarGridSpec(
            num_scalar_prefetch=1, grid=(R // tr,),
            in_specs=[pl.BlockSpec((1, tr, cdim), lambda i, qr: (qr[2], i, 0)),
                      pl.BlockSpec((1, tr, cdim), lambda i, qr: (qr[1], i, 0)),
                      pl.BlockSpec((3, tr, cdim), lambda i, qr: (0, i, 0)), spec, spec, spec]
            + [ANY] * len(after),
            out_specs=[spec] * 4),
        out_shape=[_sds((R, cdim), F32)] * 4,
        compiler_params=_params(("parallel",)),
    )(place, grad8.reshape(NDEV, R, cdim), pair4.reshape(NCHIP, R, cdim),
      got3.reshape(3, R, cdim), w2, m2, v2, *after)
    return [o.reshape(shape) for o in outs]


def sum_small(parts, after=()):
    R = parts.shape[1]

    def body(p_ref, g_out):
        g = p_ref[0]
        for j in range(1, NDEV):
            g = g + p_ref[j]
        g_out[...] = g

    return pl.pallas_call(
        _with_after(body, 1, after), name="sum_small", grid=(1,),
        in_specs=[pl.BlockSpec((NDEV, R, LANES), lambda i: (0, 0, 0))] + [ANY] * len(after),
        out_specs=pl.BlockSpec((R, LANES), lambda i: (0, 0)), out_shape=_sds((R, LANES), F32),
        compiler_params=_params(("arbitrary",)),
    )(parts, *after)


def adamw_natural(gs, ws, ms, vs):
    n = len(ws)
    nblk = 8
    big = [w.ndim == 4 and w.shape[1] % nblk == 0 for w in ws]

    def spec(w, is_big):
        if is_big:
            return pl.BlockSpec((1, w.shape[1] // nblk) + w.shape[2:], lambda i: (0, i, 0, 0))
        return pl.BlockSpec(w.shape, functools.partial(lambda i, nd: (0,) * nd, nd=w.ndim))

    def body(*refs):
        g_refs, w_refs, m_refs, v_refs = (refs[k * n:(k + 1) * n] for k in range(4))
        d_outs, m_outs, v_outs = (refs[(4 + k) * n:(5 + k) * n] for k in range(3))

        def update(p):
            d, mn, vn = _adamw_math(g_refs[p][...], w_refs[p][...], m_refs[p][...], v_refs[p][...])
            d_outs[p][...] = d
            m_outs[p][...] = mn
            v_outs[p][...] = vn

        for p in range(n):
            if big[p]:
                update(p)

        @pl.when(pl.program_id(0) == 0)
        def _():
            for p in range(n):
                if not big[p]:
                    update(p)

    specs = [spec(w, b) for w, b in zip(ws, big)]
    outs = pl.pallas_call(
        body, name="adamw_natural", grid=(nblk,), in_specs=specs * 4, out_specs=specs * 3,
        out_shape=[_sds(w.shape, F32) for w in ws] * 3,
        compiler_params=_params(("arbitrary",)),
    )(*gs, *ws, *ms, *vs)
    return outs[:n], outs[n:2 * n], outs[2 * n:]


def adamw_ada(c_all_t, dmod_all, w, m, v, my_dev):
    D, cols = w.shape
    tr = _tile(D, 256)

    def body(k_ref, c_ref, d_ref, w_ref, m_ref, v_ref, g_out, d_out, m_out, v_out):
        cv = c_ref[...]
        act = cv * _sigmoid(cv)
        dm = d_ref[...]
        g = act[:, 0:1] * dm[0:1, :]
        for b in range(1, NDEV):
            g = g + act[:, b:b + 1] * dm[b:b + 1, :]
        d, mn, vn = _adamw_math(g, w_ref[...], m_ref[...], v_ref[...])
        g_out[...] = g
        d_out[...] = d
        m_out[...] = mn
        v_out[...] = vn

    spec = pl.BlockSpec((tr, cols), lambda i, kr: (i, 0))
    return pl.pallas_call(
        body, name="adamw_ada",
        grid_spec=pltpu.PrefetchScalarGridSpec(
            num_scalar_prefetch=1, grid=(D // tr,),
            in_specs=[pl.BlockSpec((tr, NDEV), lambda i, kr: (i, 0)),
                      pl.BlockSpec((NDEV, cols), lambda i, kr: (0, kr[0])), spec, spec, spec],
            out_specs=[spec] * 4),
        out_shape=[_sds((D, cols), F32)] * 4,
        compiler_params=_params(("parallel",)),
    )(my_dev, c_all_t, dmod_all, w, m, v)


def _small_pack(parts):
    rows = []
    for p in parts:
        flat = p.reshape(-1)
        flat = jnp.pad(flat, (0, (-flat.shape[0]) % (SUBLANES * LANES)))
        rows.append(flat.reshape(-1, LANES))
    return jnp.concatenate(rows, axis=0)


def _small_unpack(buf, shapes):
    out, r = [], 0
    for s in shapes:
        n = int(math.prod(s))
        nr = -(-n // (SUBLANES * LANES)) * SUBLANES
        out.append(buf[r:r + nr].reshape(-1)[:n].reshape(s))
        r += nr
    return out


def kernel(x, c, w_ada, b_ada, w_in, lam_re, lam_im, log_dt, ssm_b_re, ssm_b_im, ssm_c_re, ssm_c_im, ssm_d, w_glu_val, w_glu_gate, w_pool, pool_scale, w_pool_out, w_out, ln1_g, ln1_b, w_ff1, w_ff2, ln2_g, ln2_b, loss_target, m_w_ada, m_b_ada, m_w_in, m_lam_re, m_lam_im, m_log_dt, m_ssm_b_re, m_ssm_b_im, m_ssm_c_re, m_ssm_c_im, m_ssm_d, m_w_glu_val, m_w_glu_gate, m_w_pool, m_pool_scale, m_w_pool_out, m_w_out, m_ln1_g, m_ln1_b, m_w_ff1, m_w_ff2, m_ln2_g, m_ln2_b, v_w_ada, v_b_ada, v_w_in, v_lam_re, v_lam_im, v_log_dt, v_ssm_b_re, v_ssm_b_im, v_ssm_c_re, v_ssm_c_im, v_ssm_d, v_w_glu_val, v_w_glu_gate, v_w_pool, v_pool_scale, v_w_pool_out, v_w_out, v_ln1_g, v_ln1_b, v_w_ff1, v_w_ff2, v_ln2_g, v_ln2_b):
    S, D = x.shape[1], x.shape[2]
    x2d, tgt = x[0], loss_target[0]
    W = D // 2
    G = W // SSM_GROUP
    P, H, GPB = SSM_STATE, SSM_GROUP, GROUPS_PER_BLOCK
    nblk = G // GPB
    gw = W // len(POOL_WINDOWS)
    ax, ay, ac = lax.axis_index("x"), lax.axis_index("y"), lax.axis_index("c")
    my_dev = (4 * ax + 2 * ay + ac).astype(jnp.int32).reshape(1)
    place = jnp.stack([ac, 2 * ax + ay, 4 * ax + 2 * ay + ac]).astype(jnp.int32)
    ts = _tile(S, 256)

    glu = jnp.stack([w_glu_val[0], w_glu_gate[0]]).astype(BF16)
    shards = [w_in[0].astype(BF16), glu, w_pool[0].astype(BF16), w_pool_out[0].astype(BF16),
              w_out[0].astype(BF16), w_ff1[0].astype(BF16), w_ff2[0].astype(BF16)]
    wg_in, wg_pool = seq_all_gather("gather_w_in", [shards[0], shards[2]], 1)
    wg_vg, wg_po, wg_out = seq_all_gather("gather_w_mix", [shards[1], shards[3], shards[4]], 2)
    (wg_ff1,) = seq_all_gather("gather_w_ff1", shards[5:6], 3)
    (wg_ff2,) = seq_all_gather("gather_w_ff2", shards[6:7], 11)
    wg_vg = wg_vg.reshape(2 * NDEV, W, D // NDEV)
    nwin = len(POOL_WINDOWS)
    wp_full = jnp.transpose(wg_pool, (1, 0, 2, 3)).reshape(nwin, gw, gw)
    wout_full = wg_out.reshape(1, D, D)
    wff2_full = wg_ff2.reshape(1, 4 * D, D)

    small_names = [b_ada, lam_re, lam_im, log_dt, ssm_b_re, ssm_b_im, ssm_c_re, ssm_c_im, ssm_d,
                   pool_scale, ln1_g, ln1_b, ln2_g, ln2_b]
    small_m = [m_b_ada, m_lam_re, m_lam_im, m_log_dt, m_ssm_b_re, m_ssm_b_im, m_ssm_c_re, m_ssm_c_im,
               m_ssm_d, m_pool_scale, m_ln1_g, m_ln1_b, m_ln2_g, m_ln2_b]
    small_v = [v_b_ada, v_lam_re, v_lam_im, v_log_dt, v_ssm_b_re, v_ssm_b_im, v_ssm_c_re, v_ssm_c_im,
               v_ssm_d, v_pool_scale, v_ln1_g, v_ln1_b, v_ln2_g, v_ln2_b]

    mod, c_all = ada_fwd(c, w_ada[0], b_ada)
    mod = mod.reshape(6, 1, D)
    sh1, sc1, g1, sh2, sc2, g2 = (mod[i] for i in range(6))

    f2, kconst = s5_disc(lam_re[0], lam_im[0], log_dt[0].reshape(G, 1))
    kconst = kconst.reshape(NCONST, SUBLANES, G * P)
    f2r = f2.reshape(2, 1, G * P)
    bt_re = jnp.transpose(ssm_b_re[0], (2, 0, 1)).reshape(H, G * P)
    bt_im = jnp.transpose(ssm_b_im[0], (2, 0, 1)).reshape(H, G * P)
    ct_re = jnp.transpose(ssm_c_re[0], (1, 0, 2)).reshape(H, G * P)
    ct_im = jnp.transpose(ssm_c_im[0], (1, 0, 2)).reshape(H, G * P)
    s5_params = (f2r, bt_re, bt_im, ct_re, ct_im, ssm_d, kconst)

    stat = [(1, F32), (1, F32)]

    def e1(t, b):
        xhat, mu, rs = _ln_full(t[0])
        return [xhat * (1.0 + b[0]) + b[1], mu, rs], []
    h1, mu0, rs0 = _rowwise("ln_mod1", e1, S, ts, [(x2d, D, 0)], [sc1, sh1], [(D, BF16)] + stat, [])

    (proj,) = mm_nn("proj", h1, wg_in, F32, 2)
    z, xsb_all, zp = s5_fwd(proj, s5_params, nblk)
    (vt,) = mm_nn("glu", z, wg_vg, BF16, 4)
    pooled = pool_fwd(proj, W, W, gw)

    def pool_epi(vals, ex, outs):
        a = vals[0]
        outs[0][...] = a
        outs[1][...] = (a * ex[0][...]).astype(BF16)
    tmp = _tile(S, 1024)
    yp, ypool = _mm(
        "pool_mix", "nn", pooled, wp_full.astype(BF16), (S // tmp, nwin, 1),
        pl.BlockSpec((tmp, gw), lambda i, j, k: (i, j)), pl.BlockSpec((1, gw, gw), lambda i, j, k: (j, 0, 0)),
        [(_sds((S, W), F32), pl.BlockSpec((tmp, gw), lambda i, j, k: (i, j))),
         (_sds((S, W), BF16), pl.BlockSpec((tmp, gw), lambda i, j, k: (i, j)))],
        (tmp, gw), 1, gw, None, pool_epi,
        [(pool_scale, pl.BlockSpec((1, gw), lambda i, j, k: (0, j)))])
    (y_b,) = mm_nn("pool_out", ypool, wg_po, BF16, 4)

    cb = D // NDEV
    ga_cb, gb_cb = (2 * W) // cb, (2 * W + D) // cb
    mcb = 4
    wm = mcb * cb
    tsm = _tile(S, 256)

    def merge_call(name, fn, ins, n_out, after=()):
        def body(*refs):
            vals = [r[...].astype(F32) for r in refs[:len(ins)]]
            for r, v in zip(refs[len(ins):], fn(*vals)):
                r[...] = v.astype(r.dtype)
        return pl.pallas_call(
            _with_after(body, len(ins), after), name=name, grid=(S // tsm, NDEV // mcb),
            in_specs=[pl.BlockSpec((tsm, w), f) for (_, w, f) in ins] + [ANY] * len(after),
            out_specs=[pl.BlockSpec((tsm, w), lambda i, j: (i, j)) for (_, w) in n_out],
            out_shape=[_sds((S, cols), BF16) for (cols, _) in n_out],
            compiler_params=_params(("parallel", "parallel")),
        )(*[a for (a, _, _) in ins], *after)

    merge_ins = [(proj, wm, lambda i, j: (i, ga_cb // mcb + j)), (proj, wm, lambda i, j: (i, gb_cb // mcb + j)),
                 (vt, 2 * wm, lambda i, j: (i, j)), (y_b, wm, lambda i, j: (i, j))]

    def val_gate(vtv):
        return (jnp.concatenate([vtv[:, 2 * q * cb:(2 * q + 1) * cb] for q in range(mcb)], axis=1),
                jnp.concatenate([vtv[:, (2 * q + 1) * cb:(2 * q + 2) * cb] for q in range(mcb)], axis=1))

    def merge_f(ga, gb, vtv, yb):
        vv, tt = val_gate(vtv)
        return [_sigmoid(ga) * (vv * _sigmoid(tt)) + _sigmoid(gb) * yb]
    (merged,) = merge_call("merge", merge_f, merge_ins, [(D, wm)])

    (mix,) = mm_nn("mix_out", merged, wout_full, F32, 1)

    def e3(t, b):
        xv, mx = t
        g1v, l1g, l1b, sc2v, sh2v = b
        r1 = ALPHA * xv + g1v * mx
        xh1, mu1, rs1 = _ln_full(r1)
        x1 = xh1 * l1g + l1b
        xh, mu2, rs2 = _ln_full(x1)
        return [r1, xh * (1.0 + sc2v) + sh2v, mu1, rs1, mu2, rs2], []
    r1, h2, mu1, rs1, mu2, rs2 = _rowwise(
        "post_mix", e3, S, ts, [(x2d, D, 0), (mix, D, 0)],
        [g1, ln1_g, ln1_b, sc2, sh2], [(D, F32), (D, BF16)] + stat + stat, [])

    def relu_epi(vals, ex, outs):
        outs[0][...] = jnp.maximum(vals[0], 0.0).astype(BF16)
    (rl,) = mm_nn("ff1", h2, wg_ff1, BF16, 1, epi=relu_epi)

    def square(a):
        return a * a
    (y2,) = mm_nn("ff2", rl, wff2_full, F32, 1, pro=square)

    def e4(t, b):
        r1v, y2v, tg, m1, s1 = t
        g2v, l1g, l1b, l2g, l2b = b
        x1 = (r1v - m1) * s1 * l1g + l1b
        r2 = ALPHA * x1 + g2v * y2v
        xh2, rs2 = _ln_stats(r2)
        err = xh2 * l2g + l2b - tg
        dx2 = err * (1.0 / D)
        dr2 = _ln_bwd(dx2 * l2g, xh2, rs2)
        lsum = jnp.sum(_colsum(err * err), axis=1, keepdims=True) * (0.5 / D)
        return ([ALPHA * dr2, g2v * dr2],
                [jnp.broadcast_to(lsum, (1, LANES)), _colsum(dx2 * xh2), _colsum(dx2), _colsum(dr2 * y2v)])
    dx1a, dy2, loss_acc, g_ln2g, g_ln2b, d_g2 = _rowwise(
        "head", e4, S, ts, [(r1, D, 0), (y2, D, 0), (tgt, D, 0), (mu1, 1, 0), (rs1, 1, 0)],
        [g2, ln1_g, ln1_b, ln2_g, ln2_b],
        [(D, F32), (D, BF16)], [LANES, D, D, D])

    tn_ff = _tile(4 * D, 1024)

    def dff_epi(vals, ex, outs):
        outs[0][...] = (vals[0] * (2.0 * ex[0][...].astype(F32))).astype(BF16)
    tmf = _tile(S, 1024)
    (da1,) = mm_nt("d_ff2", dy2, wff2_full, BF16, 1, tn=tn_ff, epi=dff_epi,
                   extras=[(rl, pl.BlockSpec((tmf, tn_ff), lambda i, j, k: (i, j)))])
    gw_ff2 = mm_tn("gw_ff2", rl, dy2, BF16, NDEV, 0, pro=square)
    gw_ff1 = mm_tn("gw_ff1", h2, da1, BF16, NDEV, 1)
    tok, wait_pair_a = pair_exchange("pair_exchange_ff", [gw_ff2, gw_ff1], 4)
    (dh2,) = mm_nt("d_ff1", da1, wg_ff1, F32, 4, after=[tok])

    def e5(t, b):
        dh2v, r1v, dx1av, mx, m1, s1, m2, s2 = t
        sc2v, l1g, l1b, g1v = b
        xh1 = (r1v - m1) * s1
        xh = (xh1 * l1g + l1b - m2) * s2
        dx1 = dx1av + _ln_bwd(dh2v * (1.0 + sc2v), xh, s2)
        dr1 = _ln_bwd(dx1 * l1g, xh1, s1)
        return ([ALPHA * dr1, g1v * dr1],
                [_colsum(dh2v * xh), _colsum(dh2v), _colsum(dx1 * xh1), _colsum(dx1), _colsum(dr1 * mx)])
    dxa, dmix, d_sc2, d_sh2, g_ln1g, g_ln1b, d_g1 = _rowwise(
        "post_mix_bwd", e5, S, ts,
        [(dh2, D, 0), (r1, D, 0), (dx1a, D, 0), (mix, D, 0), (mu1, 1, 0), (rs1, 1, 0), (mu2, 1, 0), (rs2, 1, 0)],
        [sc2, ln1_g, ln1_b, g1], [(D, F32), (D, BF16)], [D, D, D, D, D])

    (dmerged,) = mm_nt("d_mix_out", dmix, wout_full, BF16, 1)
    gw_out = mm_tn("gw_out", merged, dmix, BF16, NDEV, 0)
    grads_a, got_a = wait_pair_a(gw_out)
    parts_a = [pair_sum("pair_sum_ff%d" % i, g, t, place) for i, (g, t) in enumerate(zip(grads_a, got_a))]
    tok, wait_chip_a = chip_exchange("chip_exchange_ff", parts_a, 5)

    def merge_b(ga, gb, vtv, yb, dm):
        vv, tt = val_gate(vtv)
        sa, sb, st = _sigmoid(ga), _sigmoid(gb), _sigmoid(tt)
        dya = dm * sa
        dv, dt = dya * st, dya * vv * st * (1.0 - st)
        dvt_tile = jnp.concatenate([t[:, q * cb:(q + 1) * cb] for q in range(mcb) for t in (dv, dt)], axis=1)
        return [dm * (vv * st) * sa * (1.0 - sa), dm * yb * sb * (1.0 - sb), dvt_tile, dm * sb]
    dga, dgb_, dvt, dy_b = merge_call(
        "merge_bwd", merge_b, merge_ins + [(dmerged, wm, lambda i, j: (i, j))],
        [(D, wm), (D, wm), (2 * D, 2 * wm), (D, wm)], after=[tok])

    (dypool,) = mm_nt("d_pool_out", dy_b, wg_po, F32, NDEV)
    gw_po = mm_tn("gw_pool_out", ypool, dy_b, BF16, NDEV, 4)

    def e7(t, b):
        return [t[0] * b[0]], [_colsum(t[0] * t[1])]
    dyp, g_pscale = _rowwise("pool_scale_bwd", e7, S, ts, [(dypool, W, 0), (yp, W, 0)],
                             [pool_scale], [(W, BF16)], [W])
    (dpooled,) = _mm(
        "d_pool_mix", "nt", dyp, wp_full.astype(BF16), (S // tmp, nwin, 1),
        pl.BlockSpec((tmp, gw), lambda i, j, k: (i, j)), pl.BlockSpec((1, gw, gw), lambda i, j, k: (j, 0, 0)),
        [(_sds((S, W), F32), pl.BlockSpec((tmp, gw), lambda i, j, k: (i, j)))], (tmp, gw), 1, gw)
    tkp = _tile(S, 2048)
    gw_pool = _mm(
        "gw_pool", "tn", pooled, dyp, (nwin, 1, S // tkp),
        pl.BlockSpec((tkp, gw), lambda i, j, k: (k, i)), pl.BlockSpec((tkp, gw), lambda i, j, k: (k, i)),
        [(_sds((nwin, gw, gw), BF16), pl.BlockSpec((1, gw, gw), lambda i, j, k: (i, 0, 0)))],
        (gw, gw), 1, gw, stacked_out=True)[0]
    du_pool = pool_bwd(dpooled, gw)

    (dz,) = mm_nt("d_glu", dvt, wg_vg, BF16, 2 * NDEV)
    gw_vg = mm_tn("gw_glu", z, dvt, BF16, 2 * NDEV, 4)
    gw_pool_st = jnp.transpose(gw_pool.reshape(nwin, NDEV, gw // NDEV, gw), (1, 0, 2, 3))
    grads_b = [gw_out, gw_po, gw_pool_st, gw_vg.reshape(NDEV, 2, W, D // NDEV)]
    tok, wait_pair_b = pair_exchange("pair_exchange_mix", grads_b, 6)
    du_ssm, g_bt_re, g_bt_im, g_ct_re, g_ct_im, g_f, g_d, g_a = s5_bwd(
        proj, xsb_all, dz, zp, s5_params, nblk, after=[tok])
    grads_b, got_b = wait_pair_b(du_ssm)
    parts_b = [pair_sum("pair_sum_mix%d" % i, g, t, place) for i, (g, t) in enumerate(zip(grads_b, got_b))]
    tok, wait_chip_b = chip_exchange("chip_exchange_mix", parts_b, 7)

    dproj = jnp.concatenate([du_ssm, du_pool, dga, dgb_], axis=1)
    gw_in = mm_tn("gw_in", h1, dproj, BF16, NDEV, 1, after=[tok])
    tok, wait_pair_c = pair_exchange("pair_exchange_in", [gw_in], 8)
    (dh1,) = mm_nt("d_proj", dproj, wg_in, F32, 4, after=[tok])
    grads_c, got_c = wait_pair_c(dh1)
    parts_c = [pair_sum("pair_sum_in", grads_c[0], got_c[0], place)]
    tok, wait_chip_c = chip_exchange("chip_exchange_in", parts_c, 9)

    def e10(t, b):
        dh1v, xv, dxav, m0, s0 = t
        xh = (xv - m0) * s0
        return ([dxav + _ln_bwd(dh1v * (1.0 + b[0]), xh, s0)],
                [_colsum(dh1v * xh), _colsum(dh1v)])
    grad_x, d_sc1, d_sh1 = _rowwise(
        "ln_mod1_bwd", e10, S, ts, [(dh1, D, 0), (x2d, D, 0), (dxa, D, 0), (mu0, 1, 0), (rs0, 1, 0)],
        [sc1], [(D, F32)], [D, D], after=[tok])

    g_b_re = jnp.transpose(g_bt_re.reshape(H, G, P), (1, 2, 0))
    g_b_im = jnp.transpose(g_bt_im.reshape(H, G, P), (1, 2, 0))
    g_c_re = jnp.transpose(g_ct_re.reshape(H, G, P), (1, 0, 2))
    g_c_im = jnp.transpose(g_ct_im.reshape(H, G, P), (1, 0, 2))
    d_ab = jnp.transpose(g_a.reshape(nblk, 2, GPB, P), (1, 0, 2, 3)).reshape(2, G, P)
    g_lr, g_li, g_ldt = s5_disc_bwd(lam_re[0], lam_im[0], log_dt[0].reshape(G, 1), d_ab,
                                    g_f.reshape(2, G, P))

    dmod = jnp.concatenate([d_sh1, d_sc1, d_g1, d_sh2, d_sc2, d_g2], axis=1)
    small_g = [dmod, g_lr, g_li, g_ldt, g_b_re, g_b_im, g_c_re, g_c_im, g_d, g_pscale,
               g_ln1g, g_ln1b, g_ln2g, g_ln2b, loss_acc]
    packed_g = _small_pack(small_g)
    (parts_all,) = seq_all_gather("gather_small", [packed_g], 10)
    glu_w = jnp.stack([w_glu_val[0], w_glu_gate[0]])
    glu_m = jnp.stack([m_w_glu_val[0], m_w_glu_gate[0]])
    glu_v = jnp.stack([v_w_glu_val[0], v_w_glu_gate[0]])
    wmv = [(w_ff2[0], m_w_ff2[0], v_w_ff2[0]), (w_ff1[0], m_w_ff1[0], v_w_ff1[0]),
           (w_out[0], m_w_out[0], v_w_out[0]), (w_pool_out[0], m_w_pool_out[0], v_w_pool_out[0]),
           (w_pool[0], m_w_pool[0], v_w_pool[0]), (glu_w, glu_m, glu_v)]
    _, got3_a = wait_chip_a(packed_g)
    upd = [adamw_sharded("adamw_%d" % i, g, p, t, w, m, v, place)
           for i, (g, p, t, (w, m, v)) in enumerate(zip(grads_a, got_a, got3_a, wmv[:2]))]
    _, got3_b = wait_chip_b(upd[-1][0])
    upd += [adamw_sharded("adamw_%d" % (2 + i), g, p, t, w, m, v, place)
            for i, (g, p, t, (w, m, v)) in enumerate(zip(grads_b, got_b, got3_b, wmv[2:]))]
    u_ff2, u_ff1, u_out, u_po, u_pool, u_glu = upd

    gsum = sum_small(parts_all, after=[upd[-1][0]])
    sg = _small_unpack(gsum, [t.shape for t in small_names] + [(1, LANES)])
    loss, sg = sg[-1][0, 0], sg[:-1]
    sd, sm, sv = adamw_natural(sg, small_names, small_m, small_v)

    nmod = 6 * D
    dmod_all = parts_all[:, :nmod // LANES, :].reshape(NDEV, nmod)
    c_all_t = jnp.transpose(c_all.reshape(NDEV, D))
    ada_out = adamw_ada(c_all_t, dmod_all, w_ada[0], m_w_ada[0], v_w_ada[0], my_dev)
    _, got3_c = wait_chip_c(ada_out[0])
    u_in = adamw_sharded("adamw_6", grads_c[0], got_c[0], got3_c[0], w_in[0], m_w_in[0], v_w_in[0], place)

    def pick(k):
        return [ada_out[k][None], sg_sd[k][0], u_in[k][None]] + [t for t in sg_sd[k][1:9]] + \
               [u_glu[k][0][None], u_glu[k][1][None], u_pool[k][None], sg_sd[k][9], u_po[k][None],
                u_out[k][None], sg_sd[k][10], sg_sd[k][11], u_ff1[k][None], u_ff2[k][None],
                sg_sd[k][12], sg_sd[k][13]]

    sg_sd = [sg, sd, sm, sv]
    return (loss, grad_x[None], *pick(0), *pick(1), *pick(2), *pick(3))
```

```python
import functools
import math

import jax
import jax.numpy as jnp
from jax import lax
from jax.experimental import pallas as pl
from jax.experimental.pallas import tpu as pltpu
from jax.experimental.pallas import tpu_sc as plsc

F32 = jnp.float32
BF16 = jnp.bfloat16
MESH = pl.DeviceIdType.MESH
NDEV = 8
NCHIP = 4

SSM_GROUP = 16
SSM_STATE = 64
GROUPS_PER_BLOCK = 8
POOL_WINDOWS = (2, 4, 8, 16)
LN_EPS = 1e-5
ALPHA = 2.0 ** 0.25
ADAM_LR, ADAM_B1, ADAM_B2, ADAM_EPS, ADAM_WD, ADAM_STEP = 0.001, 0.9, 0.999, 1e-08, 0.01, 10
SUBLANES = 8
LANES = 128
VMEM_LIMIT = 56 * 1024 * 1024


def _params(sem=None, vmem=VMEM_LIMIT):
    return pltpu.CompilerParams(dimension_semantics=sem, vmem_limit_bytes=vmem)


def _tile(n, pref):
    if n <= pref:
        return n
    t = 1 << (pref.bit_length() - 1)
    while n % t:
        t //= 2
    return t


def _cast_epi(vals, ex, outs):
    c = vals[0].shape[1]
    for s, v in enumerate(vals):
        outs[0][:, s * c:(s + 1) * c] = v.astype(outs[0].dtype)


ANY = pl.BlockSpec(memory_space=pl.ANY)


def _with_after(body, n_in, after):
    if not after:
        return body
    n_af = len(after)

    def wrapped(*refs):
        return body(*refs[:n_in], *refs[n_in + n_af:])
    return wrapped


def _mm(name, kind, a, b, grid, a_spec, b_spec, outs, acc_shape, nsub=1, c=None,
        pro=None, epi=None, extras=(), stacked_out=False, after=()):
    nk = grid[2]
    n_ex, n_out = len(extras), len(outs)

    def finish(vals, ex, out_refs):
        if epi is not None:
            epi(vals, ex, out_refs)
        elif stacked_out:
            for s, v in enumerate(vals):
                out_refs[0][s] = v.astype(out_refs[0].dtype)
        else:
            _cast_epi(vals, ex, out_refs)

    def body(*refs):
        mm_step(refs[0], refs[1], refs[2:2 + n_ex], refs[2 + n_ex:2 + n_ex + n_out], refs[-1])

    def mm_step(a_ref, b_ref, ex, out_refs, acc):
        k = pl.program_id(2)
        av = a_ref[...]
        if pro is not None:
            av = pro(av)
        if kind == "nn":
            prods = [jnp.dot(av, b_ref[s], preferred_element_type=F32) for s in range(nsub)]
        elif kind == "nt":
            t = None
            for s in range(nsub):
                d = lax.dot_general(av[:, s * c:(s + 1) * c], b_ref[s], (((1,), (1,)), ((), ())),
                                    preferred_element_type=F32)
                t = d if t is None else t + d
            prods = [t]
        else:
            t = lax.dot_general(av, b_ref[...], (((0,), (0,)), ((), ())), preferred_element_type=F32)
            prods = [t[:, s * c:(s + 1) * c] for s in range(nsub)] if stacked_out else [t]
        if nk == 1:
            finish(prods, ex, out_refs)
            return
        w = prods[0].shape[1]

        @pl.when(k == 0)
        def _():
            for s, p in enumerate(prods):
                acc[:, s * w:(s + 1) * w] = p

        @pl.when(jnp.logical_and(k > 0, k < nk - 1))
        def _():
            for s, p in enumerate(prods):
                acc[:, s * w:(s + 1) * w] += p

        @pl.when(k == nk - 1)
        def _():
            finish([acc[:, s * w:(s + 1) * w] + p for s, p in enumerate(prods)], ex, out_refs)

    return pl.pallas_call(
        _with_after(body, 2 + n_ex, after), name=name, grid=grid,
        in_specs=[a_spec, b_spec] + [e[1] for e in extras] + [ANY] * len(after),
        out_specs=[o[1] for o in outs],
        out_shape=[o[0] for o in outs],
        scratch_shapes=[pltpu.VMEM(acc_shape, F32)] if nk > 1 else [],
        compiler_params=_params(("parallel", "parallel", "arbitrary")),
    )(a, b, *[e[0] for e in extras], *after)


def _sds(shape, dtype):
    return jax.ShapeDtypeStruct(shape, dtype)


def mm_nn(name, a, b3, out_dtype, nsub, tm=1024, tk=2048, tn=None, pro=None, epi=None,
          extras=(), extra_outs=(), a_col0=0, after=()):
    M = a.shape[0]
    nb, K, cdim = b3.shape
    tm, tk = _tile(M, tm), _tile(K, tk)
    if nb == 1:
        tn = _tile(cdim, tn or 1024)
        nsub, c, nj = 1, tn, cdim // tn
        b_spec = pl.BlockSpec((1, tk, tn), lambda i, j, k: (0, k, j))
        N = cdim
    else:
        c, nj, tn = cdim, nb // nsub, nsub * cdim
        b_spec = pl.BlockSpec((nsub, tk, cdim), lambda i, j, k: (j, k, 0))
        N = nb * cdim
    kb0 = a_col0 // tk
    a_spec = pl.BlockSpec((tm, tk), lambda i, j, k: (i, kb0 + k))
    grid = (M // tm, nj, K // tk)
    o_spec = pl.BlockSpec((tm, tn), lambda i, j, k: (i, j))
    outs = [(_sds((M, N), out_dtype), o_spec)] + [(_sds((M, N), d), o_spec) for d in extra_outs]
    return _mm(name, "nn", a, b3, grid, a_spec, b_spec, outs, (tm, tn), nsub, c, pro, epi, extras,
               after=after)


def mm_nt(name, a, b3, out_dtype, nsub, tm=1024, tn=1024, epi=None, extras=(), extra_outs=(),
          after=()):
    M = a.shape[0]
    nb, N, cdim = b3.shape
    tm, tn = _tile(M, tm), _tile(N, tn)
    if nb == 1:
        tk = _tile(cdim, 2048)
        nsub, c, nk = 1, tk, cdim // tk
        b_spec = pl.BlockSpec((1, tn, tk), lambda i, j, k: (0, j, k))
    else:
        c, nk, tk = cdim, nb // nsub, nsub * cdim
        b_spec = pl.BlockSpec((nsub, tn, cdim), lambda i, j, k: (k, j, 0))
    a_spec = pl.BlockSpec((tm, tk), lambda i, j, k: (i, k))
    grid = (M // tm, N // tn, nk)
    o_spec = pl.BlockSpec((tm, tn), lambda i, j, k: (i, j))
    outs = [(_sds((M, N), out_dtype), o_spec)] + [(_sds((M, N), d), o_spec) for d in extra_outs]
    return _mm(name, "nt", a, b3, grid, a_spec, b_spec, outs, (tm, tn), nsub, c, None, epi, extras,
               after=after)


def mm_tn(name, a, b, out_dtype, nb, nsub, tma=1024, tk=2048, pro=None, a_col0=0, a_cols=None,
          after=()):
    S = a.shape[0]
    Ka = a_cols or a.shape[1]
    N = b.shape[1]
    tk = _tile(S, tk)
    if nsub == 0:
        tma, tn = _tile(Ka, tma), _tile(N, 1024)
        grid = (Ka // tma, N // tn, S // tk)
        ab0 = a_col0 // tma
        res = _mm(name, "tn", a, b, grid, pl.BlockSpec((tk, tma), lambda i, j, k: (k, ab0 + i)),
                  pl.BlockSpec((tk, tn), lambda i, j, k: (k, j)),
                  [(_sds((Ka, N), out_dtype), pl.BlockSpec((tma, tn), lambda i, j, k: (i, j)))],
                  (tma, tn), 1, tn, pro, None, (), after=after)[0]
        return res.reshape(nb, Ka // nb, N)
    else:
        c = N // nb
        tma = _tile(Ka, tma)
        grid = (Ka // tma, nb // nsub, S // tk)
        o_spec = pl.BlockSpec((nsub, tma, c), lambda i, j, k: (j, i, 0))
        out = _sds((nb, Ka, c), out_dtype)
        nsub_k = nsub
        tn = nsub * c
        b_spec = pl.BlockSpec((tk, tn), lambda i, j, k: (k, j))
    ab0 = a_col0 // tma
    a_spec = pl.BlockSpec((tk, tma), lambda i, j, k: (k, ab0 + i))
    return _mm(name, "tn", a, b, grid, a_spec, b_spec, [(out, o_spec)], (tma, tn), nsub_k, c,
               pro, None, (), stacked_out=True, after=after)[0]


def _rowwise(name, fn, S, ts, tiled, bcast, tiled_out, acc_out, after=()):
    nt, nb, no, na = len(tiled), len(bcast), len(tiled_out), len(acc_out)

    def body(*refs):
        tin = [r[...] for r in refs[:nt]]
        bin_ = [r[...] for r in refs[nt:nt + nb]]
        o_refs = refs[nt + nb:nt + nb + no]
        a_refs = refs[nt + nb + no:]
        touts, aouts = fn(tin, bin_)
        for r, v in zip(o_refs, touts):
            r[...] = v.astype(r.dtype)
        i = pl.program_id(0)

        @pl.when(i == 0)
        def _():
            for r, v in zip(a_refs, aouts):
                r[...] = v

        @pl.when(i > 0)
        def _():
            for r, v in zip(a_refs, aouts):
                r[...] += v

    in_specs = [pl.BlockSpec((ts, w), functools.partial(lambda i, cb: (i, cb), cb=cb))
                for (_, w, cb) in tiled]
    in_specs += [pl.BlockSpec(b.shape, lambda i: (0, 0)) for b in bcast]
    out_specs = [pl.BlockSpec((ts, w), lambda i: (i, 0)) for (w, _) in tiled_out]
    out_specs += [pl.BlockSpec((1, w), lambda i: (0, 0)) for w in acc_out]
    out_shape = [_sds((S, w), d) for (w, d) in tiled_out] + [_sds((1, w), F32) for w in acc_out]
    return pl.pallas_call(
        _with_after(body, nt + nb, after), name=name, grid=(S // ts,),
        in_specs=in_specs + [ANY] * len(after), out_specs=out_specs,
        out_shape=out_shape, compiler_params=_params(("arbitrary",)),
    )(*[t[0] for t in tiled], *bcast, *after)


def _ln_stats(v):
    mu = jnp.mean(v, axis=-1, keepdims=True)
    vc = v - mu
    var = jnp.mean(vc * vc, axis=-1, keepdims=True)
    rstd = lax.rsqrt(var + LN_EPS)
    return vc * rstd, rstd


def _ln_bwd(dxhat, xhat, rstd):
    return rstd * (dxhat - jnp.mean(dxhat, axis=-1, keepdims=True)
                   - xhat * jnp.mean(dxhat * xhat, axis=-1, keepdims=True))


def _colsum(v):
    return jnp.sum(v, axis=0, keepdims=True)


def _sigmoid(v):
    return 1.0 / (1.0 + jnp.exp(-v))


_GELU_C = math.sqrt(2.0 / math.pi)


def _gelu(v):
    return 0.5 * v * (1.0 + jnp.tanh(_GELU_C * (v + 0.044715 * v * v * v)))


def _gelu_grad(v):
    t = jnp.tanh(_GELU_C * (v + 0.044715 * v * v * v))
    return 0.5 * (1.0 + t) + 0.5 * v * (1.0 - t * t) * _GELU_C * (1.0 + 3 * 0.044715 * v * v)


def _disc(lr, li, ldt):
    dt = jnp.exp(ldt)
    mag = jnp.exp(lr * dt)
    ang = li * dt
    ab_re = mag * jnp.cos(ang)
    ab_im = mag * jnp.sin(ang)
    num_re = ab_re - 1.0
    num_im = ab_im
    den = lr * lr + li * li
    f_re = (num_re * lr + num_im * li) / den
    f_im = (num_im * lr - num_re * li) / den
    return ab_re, ab_im, f_re, f_im


def _cmul(ar, ai, br, bi):
    return ar * br - ai * bi, ar * bi + ai * br


SCAN_FOLD = 4
NCONST = 18


def s5_disc(lam_re, lam_im, log_dt):
    G, P = lam_re.shape

    def body(lr_ref, li_ref, ldt_ref, f_ref, k_ref):
        ab_re, ab_im, f_re, f_im = _disc(lr_ref[...], li_ref[...], ldt_ref[...])
        f_ref[0] = f_re
        f_ref[1] = f_im
        fr, fi = ab_re, ab_im
        for _ in range(SCAN_FOLD - 1):
            fr, fi = _cmul(fr, fi, ab_re, ab_im)
        pr, pi = [fr], [fi]
        for _ in range(SUBLANES - 1):
            nr, ni = _cmul(pr[-1], pi[-1], fr, fi)
            pr.append(nr)
            pi.append(ni)
        zero = jnp.zeros_like(ab_re)
        for r in range(SUBLANES):
            k_ref[16, r] = ab_re
            k_ref[17, r] = ab_im
        for n, sh in enumerate((1, 2, 4)):
            for r in range(SUBLANES):
                k_ref[2 * n, r] = pr[sh - 1] if r >= sh else zero
                k_ref[2 * n + 1, r] = pi[sh - 1] if r >= sh else zero
                k_ref[8 + 2 * n, r] = pr[sh - 1] if r + sh < SUBLANES else zero
                k_ref[8 + 2 * n + 1, r] = -pi[sh - 1] if r + sh < SUBLANES else zero
        for r in range(SUBLANES):
            k_ref[6, r] = pr[r]
            k_ref[7, r] = pi[r]
            k_ref[14, r] = pr[SUBLANES - 1 - r]
            k_ref[15, r] = -pi[SUBLANES - 1 - r]

    vm = pl.BlockSpec(memory_space=pltpu.VMEM)
    return pl.pallas_call(
        body, name="s5_disc", in_specs=[vm, vm, vm], out_specs=[vm, vm],
        out_shape=[_sds((2, G, P), F32), _sds((NCONST, SUBLANES, G, P), F32)],
    )(lam_re, lam_im, log_dt)


def s5_disc_bwd(lam_re, lam_im, log_dt, d_ab, d_f):
    G, P = lam_re.shape

    def body(lr_ref, li_ref, ldt_ref, dab_ref, df_ref, glr_ref, gli_ref, gdt_ref):
        _, vjp = jax.vjp(_disc, lr_ref[...], li_ref[...], ldt_ref[...])
        glr, gli, gdt = vjp((dab_ref[0], dab_ref[1], df_ref[0], df_ref[1]))
        glr_ref[...] = glr
        gli_ref[...] = gli
        gdt_ref[...] = gdt

    vm = pl.BlockSpec(memory_space=pltpu.VMEM)
    return pl.pallas_call(
        body, name="s5_disc_bwd", in_specs=[vm] * 5, out_specs=[vm] * 3,
        out_shape=[_sds((G, P), F32), _sds((G, P), F32), _sds((G, 1), F32)],
    )(lam_re, lam_im, log_dt, d_ab, d_f)


def _group_mask(cw, nst):
    row = lax.broadcasted_iota(jnp.int32, (cw, 2 * nst), 0) // SSM_GROUP
    col = (lax.broadcasted_iota(jnp.int32, (cw, 2 * nst), 1) % nst) // SSM_STATE
    return row == col


def _spread(t, mask):
    reps = mask.shape[0] // t.shape[0]
    return jnp.where(mask, jnp.tile(t, (reps, 1)), 0.0).astype(BF16)


def _gather_groups(t, mask):
    t = jnp.where(mask, t, 0.0)
    out = t[0:SSM_GROUP]
    for g in range(1, t.shape[0] // SSM_GROUP):
        out = out + t[g * SSM_GROUP:(g + 1) * SSM_GROUP]
    return out


def _s5_operands(f_ref, br_ref, bi_ref, cr_ref, ci_ref, mask):
    fr, fi = f_ref[0], f_ref[1]
    br, bi = br_ref[...], bi_ref[...]
    bm = _spread(jnp.concatenate([fr * br - fi * bi, fr * bi + fi * br], axis=1), mask)
    cm = _spread(jnp.concatenate([cr_ref[...], -ci_ref[...]], axis=1), mask)
    return bm, cm


def _planes_put(ref, val):
    for c in range(ref.shape[0]):
        ref[c] = val[:, c * LANES:(c + 1) * LANES]


def _planes_get(ref):
    return jnp.concatenate([ref[c] for c in range(ref.shape[0])], axis=1)


def _rows_ld(ref, start, lo, hi):
    rows = pl.ds(start, SUBLANES, stride=SCAN_FOLD)
    return jnp.concatenate([ref[c, rows, :] for c in range(lo // LANES, hi // LANES)], axis=1)


def _rows_st(ref, start, lo, val):
    rows = pl.ds(start, SUBLANES, stride=SCAN_FOLD)
    for k in range(val.shape[1] // LANES):
        ref[lo // LANES + k, rows, :] = val[:, k * LANES:(k + 1) * LANES]


def _phases(ref, base, lo, hi):
    return [_rows_ld(ref, base + j, lo, hi) for j in range(SCAN_FOLD)]


def _row_bcast(v, r):
    return jnp.broadcast_to(v[r:r + 1, :], v.shape)


def _scan_fwd(xs, k_ref, nst):
    m = SCAN_FOLD
    ngroup = xs.shape[1] // (SUBLANES * m)
    row = lax.broadcasted_iota(jnp.int32, (SUBLANES, nst), 0)

    def step(t, carry):
        cr, ci = carry
        base = pl.multiple_of(t * (SUBLANES * m), SUBLANES * m)
        ar, ai = k_ref[16], k_ref[17]
        pr, pi = _phases(xs, base, 0, nst), _phases(xs, base, nst, 2 * nst)
        vr, vi = pr[0], pi[0]
        for j in range(1, m):
            vr, vi = pr[j] + ar * vr - ai * vi, pi[j] + ar * vi + ai * vr
        for n, sh in enumerate((1, 2, 4)):
            sr = pltpu.roll(vr, sh, 0)
            si = pltpu.roll(vi, sh, 0)
            mr, mi = k_ref[2 * n], k_ref[2 * n + 1]
            vr, vi = vr + mr * sr - mi * si, vi + mr * si + mi * sr
        qr, qi = k_ref[6], k_ref[7]
        vr, vi = vr + qr * cr - qi * ci, vi + qr * ci + qi * cr
        _rows_st(xs, base + m - 1, 0, vr)
        _rows_st(xs, base + m - 1, nst, vi)
        xr = jnp.where(row == 0, cr, pltpu.roll(vr, 1, 0))
        xi = jnp.where(row == 0, ci, pltpu.roll(vi, 1, 0))
        for j in range(m - 1):
            xr, xi = pr[j] + ar * xr - ai * xi, pi[j] + ar * xi + ai * xr
            _rows_st(xs, base + j, 0, xr)
            _rows_st(xs, base + j, nst, xi)
        return _row_bcast(vr, SUBLANES - 1), _row_bcast(vi, SUBLANES - 1)

    zero = jnp.zeros((SUBLANES, nst), F32)
    lax.fori_loop(0, ngroup, step, (zero, zero))


def _scan_bwd(g, xs, k_ref, nst):
    m = SCAN_FOLD
    ngroup = g.shape[1] // (SUBLANES * m)
    row = lax.broadcasted_iota(jnp.int32, (SUBLANES, nst), 0)

    def step(tt, carry):
        cr, ci, dar, dai = carry
        t = ngroup - 1 - tt
        base = pl.multiple_of(t * (SUBLANES * m), SUBLANES * m)
        ar, ai = k_ref[16], -k_ref[17]
        dr, di = _phases(g, base, 0, nst), _phases(g, base, nst, 2 * nst)
        wr, wi = dr[m - 1], di[m - 1]
        for j in range(m - 2, -1, -1):
            wr, wi = dr[j] + ar * wr - ai * wi, di[j] + ar * wi + ai * wr
        for n, sh in enumerate((1, 2, 4)):
            sr = pltpu.roll(wr, SUBLANES - sh, 0)
            si = pltpu.roll(wi, SUBLANES - sh, 0)
            mr, mi = k_ref[8 + 2 * n], k_ref[8 + 2 * n + 1]
            wr, wi = wr + mr * sr - mi * si, wi + mr * si + mi * sr
        qr, qi = k_ref[14], k_ref[15]
        wr, wi = wr + qr * cr - qi * ci, wi + qr * ci + qi * cr
        gr, gi = [None] * m, [None] * m
        gr[0], gi[0] = wr, wi
        nr = jnp.where(row == SUBLANES - 1, cr, pltpu.roll(wr, SUBLANES - 1, 0))
        ni = jnp.where(row == SUBLANES - 1, ci, pltpu.roll(wi, SUBLANES - 1, 0))
        for j in range(m - 1, 0, -1):
            nr, ni = dr[j] + ar * nr - ai * ni, di[j] + ar * ni + ai * nr
            gr[j], gi[j] = nr, ni
        for j in range(m):
            _rows_st(g, base + j, 0, gr[j])
            _rows_st(g, base + j, nst, gi[j])
        xr, xi = _phases(xs, base, 0, nst), _phases(xs, base, nst, 2 * nst)
        pbase = pl.multiple_of(jnp.maximum(t - 1, 0) * (SUBLANES * m), SUBLANES * m)
        live = (t > 0).astype(F32)
        lr = _row_bcast(_rows_ld(xs, pbase + m - 1, 0, nst), SUBLANES - 1) * live
        li = _row_bcast(_rows_ld(xs, pbase + m - 1, nst, 2 * nst), SUBLANES - 1) * live
        xmr = [jnp.where(row == 0, lr, pltpu.roll(xr[m - 1], 1, 0))] + xr[:m - 1]
        xmi = [jnp.where(row == 0, li, pltpu.roll(xi[m - 1], 1, 0))] + xi[:m - 1]
        for j in range(m):
            dar = dar + gr[j] * xmr[j] + gi[j] * xmi[j]
            dai = dai + gi[j] * xmr[j] - gr[j] * xmi[j]
        return _row_bcast(wr, 0), _row_bcast(wi, 0), dar, dai

    zero = jnp.zeros((SUBLANES, nst), F32)
    _, _, dar, dai = lax.fori_loop(0, ngroup, step, (zero, zero, zero, zero))
    return _colsum(dar), _colsum(dai)


def _s5_param_specs(cw, nst):
    hp = pl.BlockSpec((SSM_GROUP, nst), lambda b: (0, b))
    return [pl.BlockSpec((2, 1, nst), lambda b: (0, 0, b)), hp, hp, hp, hp,
            pl.BlockSpec((1, cw), lambda b: (0, b)),
            pl.BlockSpec((NCONST, SUBLANES, nst), lambda b: (0, 0, b))]


def s5_fwd(proj, params, nb):
    S = proj.shape[0]
    nst = params[1].shape[1] // nb
    cw = nst // SSM_STATE * SSM_GROUP

    def body(u_ref, f_ref, br_ref, bi_ref, cr_ref, ci_ref, d_ref, k_ref, z_ref, xsb_ref, zp_ref, xs):
        bm, cm = _s5_operands(f_ref, br_ref, bi_ref, cr_ref, ci_ref, _group_mask(cw, nst))
        u = u_ref[...]
        _planes_put(xs, jnp.dot(u.astype(BF16), bm, preferred_element_type=F32))
        _scan_fwd(xs, k_ref, nst)
        xsb = _planes_get(xs).astype(BF16)
        xsb_ref[...] = xsb
        y = lax.dot_general(xsb, cm, (((1,), (1,)), ((), ())), preferred_element_type=F32)
        y = y + d_ref[...] * u
        z_ref[...] = _gelu(y).astype(BF16)
        zp_ref[...] = _gelu_grad(y).astype(BF16)

    return pl.pallas_call(
        body, name="s5_fwd", grid=(nb,),
        in_specs=[pl.BlockSpec((S, cw), lambda b: (0, b))] + _s5_param_specs(cw, nst),
        out_specs=[pl.BlockSpec((S, cw), lambda b: (0, b)), pl.BlockSpec((S, 2 * nst), lambda b: (0, b)),
                   pl.BlockSpec((S, cw), lambda b: (0, b))],
        out_shape=[_sds((S, nb * cw), BF16), _sds((S, nb * 2 * nst), BF16), _sds((S, nb * cw), BF16)],
        scratch_shapes=[pltpu.VMEM((2 * nst // LANES, S, LANES), F32)],
        compiler_params=_params(("arbitrary",)),
    )(proj, *params)


def s5_bwd(proj, xsb_all, dz, zp, params, nb, after=()):
    S = proj.shape[0]
    nst = params[1].shape[1] // nb
    cw = nst // SSM_STATE * SSM_GROUP

    def body(u_ref, xsb_ref, dz_ref, zp_ref, f_ref, br_ref, bi_ref, cr_ref, ci_ref, d_ref, k_ref,
             du_ref, gbr_ref, gbi_ref, gcr_ref, gci_ref, gf_ref, gd_ref, ga_ref, xs, g):
        mask = _group_mask(cw, nst)
        bm, cm = _s5_operands(f_ref, br_ref, bi_ref, cr_ref, ci_ref, mask)
        u = u_ref[...]
        ub = u.astype(BF16)
        d = d_ref[...]
        xsb = xsb_ref[...]
        _planes_put(xs, xsb.astype(F32))
        dy = dz_ref[...].astype(F32) * zp_ref[...].astype(F32)
        gd_ref[...] = _colsum(dy * u)
        dyb = dy.astype(BF16)
        gc = _gather_groups(lax.dot_general(dyb, xsb, (((0,), (0,)), ((), ())),
                                            preferred_element_type=F32), mask)
        gcr_ref[...] = gc[:, :nst]
        gci_ref[...] = -gc[:, nst:]
        _planes_put(g, jnp.dot(dyb, cm, preferred_element_type=F32))
        ar, ai = _scan_bwd(g, xs, k_ref, nst)
        ga_ref[0, 0:1, :] = ar
        ga_ref[0, 1:2, :] = ai
        gb = _planes_get(g).astype(BF16)
        du = lax.dot_general(gb, bm, (((1,), (1,)), ((), ())), preferred_element_type=F32) + d * dy
        du_ref[...] = du.astype(BF16)
        gbb = _gather_groups(lax.dot_general(ub, gb, (((0,), (0,)), ((), ())),
                                             preferred_element_type=F32), mask)
        dr, di = gbb[:, :nst], gbb[:, nst:]
        fr, fi = f_ref[0], f_ref[1]
        br, bi = br_ref[...], bi_ref[...]
        gbr_ref[...] = fr * dr + fi * di
        gbi_ref[...] = fr * di - fi * dr
        gf_ref[0] = _colsum(dr * br + di * bi)
        gf_ref[1] = _colsum(di * br - dr * bi)

    hp = pl.BlockSpec((SSM_GROUP, nst), lambda b: (0, b))
    hp_sds = _sds((SSM_GROUP, nb * nst), F32)
    return pl.pallas_call(
        _with_after(body, 11, after), name="s5_bwd", grid=(nb,),
        in_specs=[pl.BlockSpec((S, cw), lambda b: (0, b)),
                  pl.BlockSpec((S, 2 * nst), lambda b: (0, b)),
                  pl.BlockSpec((S, cw), lambda b: (0, b)),
                  pl.BlockSpec((S, cw), lambda b: (0, b))] + _s5_param_specs(cw, nst)
        + [ANY] * len(after),
        out_specs=[pl.BlockSpec((S, cw), lambda b: (0, b)), hp, hp, hp, hp,
                   pl.BlockSpec((2, 1, nst), lambda b: (0, 0, b)),
                   pl.BlockSpec((1, cw), lambda b: (0, b)),
                   pl.BlockSpec((1, 2, nst), lambda b: (b, 0, 0))],
        out_shape=[_sds((S, nb * cw), BF16), hp_sds, hp_sds, hp_sds, hp_sds,
                   _sds((2, 1, nb * nst), F32), _sds((1, nb * cw), F32), _sds((nb, 2, nst), F32)],
        scratch_shapes=[pltpu.VMEM((2 * nst // LANES, S, LANES), F32)] * 2,
        compiler_params=_params(("arbitrary",)),
    )(proj, xsb_all, dz, zp, *params, *after)


def _shift_rows(v, k, row, down):
    n = v.shape[0]
    if down:
        return jnp.where(row >= k, pltpu.roll(v, k, 0), 0.0)
    return jnp.where(row < n - k, pltpu.roll(v, n - k, 0), 0.0)


def _window(v, gi, row, down):
    sums = []
    s = v
    for k in (1, 2, 4, 8):
        s = s + _shift_rows(s, k, row, down)
        sums.append(s)
    out = sums[3]
    for n in (2, 1, 0):
        out = jnp.where(gi == n, sums[n], out)
    return out


def pool_fwd(proj, col0, width, gw):
    S = proj.shape[0]
    cb0 = col0 // gw

    def body(u_ref, o_ref):
        gi = pl.program_id(0)
        u = u_ref[...]
        row = lax.broadcasted_iota(jnp.int32, u.shape, 0)
        w = jnp.left_shift(2, gi)
        count = jnp.minimum(row + 1, w).astype(F32)
        o_ref[...] = (_window(u, gi, row, True) / count - u).astype(BF16)

    return pl.pallas_call(
        body, name="pool_fwd", grid=(len(POOL_WINDOWS),),
        in_specs=[pl.BlockSpec((S, gw), lambda g: (0, cb0 + g))],
        out_specs=pl.BlockSpec((S, gw), lambda g: (0, g)),
        out_shape=_sds((S, width), BF16), compiler_params=_params(("arbitrary",)),
    )(proj)


def pool_bwd(dpooled, gw):
    S, width = dpooled.shape

    def body(d_ref, o_ref):
        gi = pl.program_id(0)
        d = d_ref[...]
        row = lax.broadcasted_iota(jnp.int32, d.shape, 0)
        w = jnp.left_shift(2, gi)
        count = jnp.minimum(row + 1, w).astype(F32)
        o_ref[...] = (_window(d / count, gi, row, False) - d).astype(BF16)

    return pl.pallas_call(
        body, name="pool_bwd", grid=(len(POOL_WINDOWS),),
        in_specs=[pl.BlockSpec((S, gw), lambda g: (0, g))],
        out_specs=pl.BlockSpec((S, gw), lambda g: (0, g)),
        out_shape=_sds((S, width), BF16), compiler_params=_params(("arbitrary",)),
    )(dpooled)


def _place():
    x, y, c = lax.axis_index("x"), lax.axis_index("y"), lax.axis_index("c")
    chips = [(1 - x, y), (x, 1 - y), (1 - x, 1 - y)]
    return x, y, c, chips


HBM = pl.BlockSpec(memory_space=pltpu.HBM)


def _routed_gather_body(n):
    def body(*refs):
        ins, outs = refs[:n], refs[n:2 * n]
        send_sems, recv_sems, local_sems = refs[2 * n:]
        x, y, c, (xn, yn, dg) = _place()
        me, sibling = (x, y, c), (x, y, 1 - c)
        barrier = pltpu.get_barrier_semaphore()
        for peer in (sibling, (*xn, c), (*yn, c)):
            pl.semaphore_signal(barrier, inc=1, device_id=peer, device_id_type=MESH)
        pl.semaphore_wait(barrier, 3)

        def piece(i, p, h):
            rows = ins[i].shape[0] // 2
            return outs[i].at[4 * p[0] + 2 * p[1] + p[2], pl.ds(h * rows, rows)]

        def copy(i, k, src, dst, to):
            return pltpu.make_async_remote_copy(src_ref=src, dst_ref=dst, send_sem=send_sems.at[i, k],
                                                recv_sem=recv_sems.at[i, k], device_id=to,
                                                device_id_type=MESH)

        started = []

        def go(cp):
            cp.start()
            started.append(cp)

        for i in range(n):
            rows = ins[i].shape[0] // 2
            for h in range(2):
                own = ins[i].at[pl.ds(h * rows, rows)]
                go(copy(i, 1 + h, own, piece(i, me, h), (*xn, c)))
                go(copy(i, 3 + h, own, piece(i, me, h), (*yn, c)))
        for i in range(n):
            go(copy(i, 0, ins[i], outs[i].at[4 * x + 2 * y + c], sibling))
        mine = [pltpu.make_async_copy(ins[i], outs[i].at[4 * x + 2 * y + c], local_sems.at[i])
                for i in range(n)]
        for cp in mine:
            cp.start()
        for i in range(n):
            for k, chip, h, onward, ksib in ((1, xn, 0, (5, yn), 7), (4, yn, 1, (6, xn), 10),
                                            (2, xn, 1, None, 8), (3, yn, 0, None, 9),
                                            (5, dg, 0, None, 11), (6, dg, 1, None, 12)):
                got = piece(i, (*chip, c), h)
                copy(i, k, got, got, me).wait_recv()
                if onward is not None:
                    go(copy(i, onward[0], got, got, (*onward[1], c)))
                go(copy(i, ksib, got, got, sibling))
        for i in range(n):
            block = outs[i].at[4 * x + 2 * y + 1 - c]
            copy(i, 0, block, block, me).wait_recv()
            for ksib, chip, h in ((7, xn, 0), (10, yn, 1), (8, xn, 1), (9, yn, 0), (11, dg, 0), (12, dg, 1)):
                got = piece(i, (*chip, 1 - c), h)
                copy(i, ksib, got, got, me).wait_recv()
        for cp in started:
            cp.wait_send()
        for cp in mine:
            cp.wait()

    return body


def _on_sequencer(name, body, arrays, out_sds, sems, collective_id):
    ins = [jax.new_ref(a, memory_space=pltpu.MemorySpace.HBM) for a in arrays]
    outs = [jax.empty_ref(s, memory_space=pltpu.MemorySpace.HBM) for s in out_sds]

    @pl.kernel(mesh=plsc.ScalarSubcoreMesh(axis_name="sequencer", num_cores=1), name=name,
               scratch_types=tuple(sems),
               compiler_params=pltpu.CompilerParams(collective_id=collective_id))
    def launch(*sem_refs):
        body(*ins, *outs, *sem_refs)

    launch()
    return [o[...] for o in outs]


def seq_all_gather(name, shards, collective_id):
    n = len(shards)
    return _on_sequencer(
        name, _routed_gather_body(n), shards, [_sds((NDEV,) + s.shape, s.dtype) for s in shards],
        [pltpu.SemaphoreType.DMA((n, 13)), pltpu.SemaphoreType.DMA((n, 13)),
         pltpu.SemaphoreType.DMA((n,))], collective_id)


def pair_exchange(name, grads, collective_id):
    def plan(srcs, lands):
        x, y, c, _ = _place()
        return ([(i, q, srcs[i].at[2 * q + 1 - c], lands[i].at[q], (x, y, 1 - c))
                 for i in range(len(srcs)) for q in range(NCHIP)], [(x, y, 1 - c)])

    return _split_exchange(name, grads, [_sds((NCHIP,) + g.shape[1:], g.dtype) for g in grads],
                           plan, NCHIP, collective_id)


SEM = pl.BlockSpec(memory_space=pltpu.SEMAPHORE)


def _split_exchange(name, srcs, land_sds, plan, ncopy, collective_id):
    n = len(srcs)
    nsem = n * ncopy
    effect = pltpu.SideEffectType.DATAFLOW_SIDE_EFFECTING

    def descriptors(src_refs, land_refs, send_sems, recv_sems):
        copies, peers = plan(src_refs, land_refs)
        return [pltpu.make_async_remote_copy(src_ref=s, dst_ref=d, send_sem=send_sems[i * ncopy + k],
                                             recv_sem=recv_sems[i * ncopy + k], device_id=to,
                                             device_id_type=MESH) for (i, k, s, d, to) in copies], peers

    def start_body(*refs):
        src_refs, land_refs = refs[:n], refs[n:2 * n]
        send_sems, recv_sems = refs[2 * n:2 * n + nsem], refs[2 * n + nsem:2 * n + 2 * nsem]
        token = refs[-1]
        cps, peers = descriptors(src_refs, land_refs, send_sems, recv_sems)
        barrier = pltpu.get_barrier_semaphore()
        for peer in peers:
            pl.semaphore_signal(barrier, inc=1, device_id=peer, device_id_type=MESH)
        pl.semaphore_wait(barrier, len(peers))
        for cp in cps:
            cp.start()
        token[...] = jnp.zeros_like(token)

    lands = [pltpu.with_memory_space_constraint(lax.empty(s.shape, s.dtype), pltpu.HBM) for s in land_sds]
    srcs = [pltpu.with_memory_space_constraint(s, pltpu.HBM) for s in srcs]
    res = pl.pallas_call(
        start_body, name=name + "_start",
        out_shape=(pltpu.SemaphoreType.DMA(()),) * (2 * nsem)
        + tuple(pltpu.HBM(s.shape, s.dtype) for s in srcs)
        + tuple(pltpu.HBM(s.shape, s.dtype) for s in land_sds) + (_sds((SUBLANES, LANES), F32),),
        in_specs=[HBM] * (2 * n),
        out_specs=(SEM,) * (2 * nsem) + (HBM,) * (2 * n) + (pl.BlockSpec(memory_space=pltpu.VMEM),),
        input_output_aliases={i: 2 * nsem + i for i in range(2 * n)},
        compiler_params=pltpu.CompilerParams(has_side_effects=effect, collective_id=collective_id),
    )(*srcs, *lands)
    sems = res[:2 * nsem]
    thru = res[2 * nsem:2 * nsem + 2 * n]
    token = res[-1]

    def wait(after):
        def wait_body(*refs):
            src_refs, land_refs = refs[:n], refs[n:2 * n]
            cps, _ = descriptors(src_refs, land_refs, refs[2 * n:2 * n + nsem],
                                 refs[2 * n + nsem:2 * n + 2 * nsem])
            for cp in cps:
                cp.wait_send()
            for cp in cps:
                cp.wait_recv()

        out = pl.pallas_call(
            wait_body, name=name + "_wait",
            out_shape=tuple(pltpu.HBM(s.shape, s.dtype) for s in srcs)
            + tuple(pltpu.HBM(s.shape, s.dtype) for s in land_sds),
            in_specs=[HBM] * (2 * n) + [SEM] * (2 * nsem) + [pl.BlockSpec(memory_space=pl.ANY)],
            out_specs=(HBM,) * (2 * n),
            input_output_aliases={i: i for i in range(2 * n)},
            compiler_params=pltpu.CompilerParams(has_side_effects=effect),
        )(*thru, *sems, after)
        return list(out[:n]), list(out[n:])

    return token, wait


def pair_sum(name, grad, got, place):
    shp = grad.shape[1:]
    r, cdim = shp[-2], shp[-1]
    lead = int(math.prod(shp[:-2])) if len(shp) > 2 else 1
    g5 = grad.reshape(NCHIP, 2, lead * r, cdim)
    t4 = got.reshape(NCHIP, lead * r, cdim)
    R = lead * r
    tr = _tile(R, max(8, (1 << 20) // cdim))

    def body(p_ref, g_ref, t_ref, o_ref):
        o_ref[...] = (g_ref[0].astype(F32) + t_ref[...].astype(F32)).astype(o_ref.dtype)

    out = pl.pallas_call(
        body, name=name,
        grid_spec=pltpu.PrefetchScalarGridSpec(
            num_scalar_prefetch=1, grid=(NCHIP - 1, R // tr),
            in_specs=[pl.BlockSpec((1, 1, tr, cdim), lambda j, i, p: (p[1] ^ (j + 1), p[0], i, 0)),
                      pl.BlockSpec((1, tr, cdim), lambda j, i, p: (p[1] ^ (j + 1), i, 0))],
            out_specs=pl.BlockSpec((1, tr, cdim), lambda j, i, p: (p[1] ^ (j + 1), i, 0))),
        out_shape=_sds((NCHIP, R, cdim), grad.dtype),
        compiler_params=_params(("parallel", "parallel")),
    )(place, g5, t4)
    return out


def chip_exchange(name, parts, collective_id):
    def plan(srcs, lands):
        x, y, c, chips = _place()
        return ([(i, j, srcs[i].at[2 * chip[0] + chip[1]], lands[i].at[j], (*chip, c))
                 for i in range(len(srcs)) for j, chip in enumerate(chips)],
                [(*chip, c) for chip in chips])

    return _split_exchange(name, parts, [_sds((3,) + p.shape[1:], p.dtype) for p in parts],
                           plan, 3, collective_id)


def ada_fwd(c_row, w_ada, b_ada):
    D, cols = w_ada.shape

    def body(c_ref, w_ref, b_ref, mod_ref, call_ref, act8, part, s1, r1, s2, r2):
        x, y, c, _ = _place()
        me = 4 * x + 2 * y + c
        call_ref[me] = c_ref[...]
        cps = []
        for k in range(1, NDEV):
            to = (x ^ (k >> 2), y ^ ((k >> 1) & 1), c ^ (k & 1))
            cps.append(pltpu.make_async_remote_copy(
                src_ref=c_ref, dst_ref=call_ref.at[me], send_sem=s1.at[k - 1],
                recv_sem=r1.at[k - 1], device_id=to, device_id_type=MESH))
            cps[-1].start()
        for cp in cps:
            cp.wait()
        for b in range(NDEV):
            act8[b:b + 1, :] = call_ref[b]
        cv = act8[...]
        act = (cv * _sigmoid(cv)).astype(BF16)
        res = jnp.dot(act, w_ref[...].astype(BF16), preferred_element_type=F32)
        for b in range(NDEV):
            part[b] = res[b:b + 1, :]
        mod_ref[me] = part[me]
        cps = []
        for k in range(1, NDEV):
            to = (x ^ (k >> 2), y ^ ((k >> 1) & 1), c ^ (k & 1))
            dst = 4 * to[0] + 2 * to[1] + to[2]
            cps.append(pltpu.make_async_remote_copy(
                src_ref=part.at[dst], dst_ref=mod_ref.at[me], send_sem=s2.at[k - 1],
                recv_sem=r2.at[k - 1], device_id=to, device_id_type=MESH))
            cps[-1].start()
        for cp in cps:
            cp.wait()
        for b in range(NDEV):
            mod_ref[b] = mod_ref[b] + b_ref[b]

    vm = pl.BlockSpec(memory_space=pltpu.VMEM)
    return pl.pallas_call(
        body, name="ada_fwd", in_specs=[vm, vm, vm], out_specs=[vm, vm],
        out_shape=[_sds((NDEV, 1, cols), F32), _sds((NDEV, 1, D), F32)],
        scratch_shapes=[pltpu.VMEM((NDEV, D), F32), pltpu.VMEM((NDEV, 1, cols), F32),
                        pltpu.SemaphoreType.DMA((NDEV - 1,)), pltpu.SemaphoreType.DMA((NDEV - 1,)),
                        pltpu.SemaphoreType.DMA((NDEV - 1,)), pltpu.SemaphoreType.DMA((NDEV - 1,))],
        compiler_params=pltpu.CompilerParams(vmem_limit_bytes=VMEM_LIMIT),
    )(c_row, w_ada, b_ada.reshape(NDEV, 1, cols))


def _adamw_math(g, w, m, v):
    m2 = ADAM_B1 * m + (1.0 - ADAM_B1) * g
    v2 = ADAM_B2 * v + (1.0 - ADAM_B2) * (g * g)
    m_hat = m2 / (1.0 - ADAM_B1 ** ADAM_STEP)
    v_hat = v2 / (1.0 - ADAM_B2 ** ADAM_STEP)
    delta = -ADAM_LR * (m_hat / (jnp.sqrt(v_hat) + ADAM_EPS) + ADAM_WD * w)
    return delta, m2, v2


def adamw_sharded(name, grad8, pair4, got3, w, m, v, place, after=()):
    shape = w.shape
    cdim = shape[-1]
    R = int(math.prod(shape[:-1]))
    w2, m2, v2 = (t.reshape(R, cdim) for t in (w, m, v))
    tr = _tile(R, max(8, (1 << 19) // cdim))

    def body(q_ref, own_ref, sib_ref, t_ref, w_ref, m_ref, v_ref, g_out, d_out, m_out, v_out):
        g = own_ref[0].astype(F32) + sib_ref[0].astype(F32)
        for j in range(3):
            g = g + t_ref[j].astype(F32)
        d, mn, vn = _adamw_math(g, w_ref[...], m_ref[...], v_ref[...])
        g_out[...] = g
        d_out[...] = d
        m_out[...] = mn
        v_out[...] = vn

    spec = pl.BlockSpec((tr, cdim), lambda i, qr: (i, 0))
    outs = pl.pallas_call(
        _with_after(body, 7, after), name=name,
        grid_spec=pltpu.PrefetchScalarGridSpec(
            num_scalar_prefetch=1, grid=(R // tr,),
            in_specs=[pl.BlockSpec((1, tr, cdim), lambda i, qr: (qr[2], i, 0)),
                      pl.BlockSpec((1, tr, cdim), lambda i, qr: (qr[1], i, 0)),
                      pl.BlockSpec((3, tr, cdim), lambda i, qr: (0, i, 0)), spec, spec, spec]
            + [ANY] * len(after),
            out_specs=[spec] * 4),
        out_shape=[_sds((R, cdim), F32)] * 4,
        compiler_params=_params(("parallel",)),
    )(place, grad8.reshape(NDEV, R, cdim), pair4.reshape(NCHIP, R, cdim),
      got3.reshape(3, R, cdim), w2, m2, v2, *after)
    return [o.reshape(shape) for o in outs]


def sum_small(parts, after=()):
    R = parts.shape[1]

    def body(p_ref, g_out):
        g = p_ref[0]
        for j in range(1, NDEV):
            g = g + p_ref[j]
        g_out[...] = g

    return pl.pallas_call(
        _with_after(body, 1, after), name="sum_small", grid=(1,),
        in_specs=[pl.BlockSpec((NDEV, R, LANES), lambda i: (0, 0, 0))] + [ANY] * len(after),
        out_specs=pl.BlockSpec((R, LANES), lambda i: (0, 0)), out_shape=_sds((R, LANES), F32),
        compiler_params=_params(("arbitrary",)),
    )(parts, *after)


def adamw_natural(gs, ws, ms, vs):
    n = len(ws)
    nblk = 8
    big = [w.ndim == 4 and w.shape[1] % nblk == 0 for w in ws]

    def spec(w, is_big):
        if is_big:
            return pl.BlockSpec((1, w.shape[1] // nblk) + w.shape[2:], lambda i: (0, i, 0, 0))
        return pl.BlockSpec(w.shape, functools.partial(lambda i, nd: (0,) * nd, nd=w.ndim))

    def body(*refs):
        g_refs, w_refs, m_refs, v_refs = (refs[k * n:(k + 1) * n] for k in range(4))
        d_outs, m_outs, v_outs = (refs[(4 + k) * n:(5 + k) * n] for k in range(3))

        def update(p):
            d, mn, vn = _adamw_math(g_refs[p][...], w_refs[p][...], m_refs[p][...], v_refs[p][...])
            d_outs[p][...] = d
            m_outs[p][...] = mn
            v_outs[p][...] = vn

        for p in range(n):
            if big[p]:
                update(p)

        @pl.when(pl.program_id(0) == 0)
        def _():
            for p in range(n):
                if not big[p]:
                    update(p)

    specs = [spec(w, b) for w, b in zip(ws, big)]
    outs = pl.pallas_call(
        body, name="adamw_natural", grid=(nblk,), in_specs=specs * 4, out_specs=specs * 3,
        out_shape=[_sds(w.shape, F32) for w in ws] * 3,
        compiler_params=_params(("arbitrary",)),
    )(*gs, *ws, *ms, *vs)
    return outs[:n], outs[n:2 * n], outs[2 * n:]


def adamw_ada(c_all_t, dmod_all, w, m, v, my_dev):
    D, cols = w.shape
    tr = _tile(D, 256)

    def body(k_ref, c_ref, d_ref, w_ref, m_ref, v_ref, g_out, d_out, m_out, v_out):
        cv = c_ref[...]
        act = cv * _sigmoid(cv)
        dm = d_ref[...]
        g = act[:, 0:1] * dm[0:1, :]
        for b in range(1, NDEV):
            g = g + act[:, b:b + 1] * dm[b:b + 1, :]
        d, mn, vn = _adamw_math(g, w_ref[...], m_ref[...], v_ref[...])
        g_out[...] = g
        d_out[...] = d
        m_out[...] = mn
        v_out[...] = vn

    spec = pl.BlockSpec((tr, cols), lambda i, kr: (i, 0))
    return pl.pallas_call(
        body, name="adamw_ada",
        grid_spec=pltpu.PrefetchScalarGridSpec(
            num_scalar_prefetch=1, grid=(D // tr,),
            in_specs=[pl.BlockSpec((tr, NDEV), lambda i, kr: (i, 0)),
                      pl.BlockSpec((NDEV, cols), lambda i, kr: (0, kr[0])), spec, spec, spec],
            out_specs=[spec] * 4),
        out_shape=[_sds((D, cols), F32)] * 4,
        compiler_params=_params(("parallel",)),
    )(my_dev, c_all_t, dmod_all, w, m, v)


def _small_pack(parts):
    rows = []
    for p in parts:
        flat = p.reshape(-1)
        flat = jnp.pad(flat, (0, (-flat.shape[0]) % (SUBLANES * LANES)))
        rows.append(flat.reshape(-1, LANES))
    return jnp.concatenate(rows, axis=0)


def _small_unpack(buf, shapes):
    out, r = [], 0
    for s in shapes:
        n = int(math.prod(s))
        nr = -(-n // (SUBLANES * LANES)) * SUBLANES
        out.append(buf[r:r + nr].reshape(-1)[:n].reshape(s))
        r += nr
    return out


def kernel(x, c, w_ada, b_ada, w_in, lam_re, lam_im, log_dt, ssm_b_re, ssm_b_im, ssm_c_re, ssm_c_im, ssm_d, w_glu_val, w_glu_gate, w_pool, pool_scale, w_pool_out, w_out, ln1_g, ln1_b, w_ff1, w_ff2, ln2_g, ln2_b, loss_target, m_w_ada, m_b_ada, m_w_in, m_lam_re, m_lam_im, m_log_dt, m_ssm_b_re, m_ssm_b_im, m_ssm_c_re, m_ssm_c_im, m_ssm_d, m_w_glu_val, m_w_glu_gate, m_w_pool, m_pool_scale, m_w_pool_out, m_w_out, m_ln1_g, m_ln1_b, m_w_ff1, m_w_ff2, m_ln2_g, m_ln2_b, v_w_ada, v_b_ada, v_w_in, v_lam_re, v_lam_im, v_log_dt, v_ssm_b_re, v_ssm_b_im, v_ssm_c_re, v_ssm_c_im, v_ssm_d, v_w_glu_val, v_w_glu_gate, v_w_pool, v_pool_scale, v_w_pool_out, v_w_out, v_ln1_g, v_ln1_b, v_w_ff1, v_w_ff2, v_ln2_g, v_ln2_b):
    S, D = x.shape[1], x.shape[2]
    x2d, tgt = x[0], loss_target[0]
    W = D // 2
    G = W // SSM_GROUP
    P, H, GPB = SSM_STATE, SSM_GROUP, GROUPS_PER_BLOCK
    nblk = G // GPB
    gw = W // len(POOL_WINDOWS)
    ax, ay, ac = lax.axis_index("x"), lax.axis_index("y"), lax.axis_index("c")
    my_dev = (4 * ax + 2 * ay + ac).astype(jnp.int32).reshape(1)
    place = jnp.stack([ac, 2 * ax + ay, 4 * ax + 2 * ay + ac]).astype(jnp.int32)
    ts = _tile(S, 256)

    glu = jnp.stack([w_glu_val[0], w_glu_gate[0]]).astype(BF16)
    shards = [w_in[0].astype(BF16), glu, w_pool[0].astype(BF16), w_pool_out[0].astype(BF16),
              w_out[0].astype(BF16), w_ff1[0].astype(BF16), w_ff2[0].astype(BF16)]
    wg_in, wg_pool = seq_all_gather("gather_w_in", [shards[0], shards[2]], 1)
    wg_vg, wg_po, wg_out = seq_all_gather("gather_w_mix", [shards[1], shards[3], shards[4]], 2)
    (wg_ff1,) = seq_all_gather("gather_w_ff1", shards[5:6], 3)
    (wg_ff2,) = seq_all_gather("gather_w_ff2", shards[6:7], 11)
    wg_vg = wg_vg.reshape(2 * NDEV, W, D // NDEV)
    nwin = len(POOL_WINDOWS)
    wp_full = jnp.transpose(wg_pool, (1, 0, 2, 3)).reshape(nwin, gw, gw)
    wout_full = wg_out.reshape(1, D, D)
    wff2_full = wg_ff2.reshape(1, 4 * D, D)

    small_names = [b_ada, lam_re, lam_im, log_dt, ssm_b_re, ssm_b_im, ssm_c_re, ssm_c_im, ssm_d,
                   pool_scale, ln1_g, ln1_b, ln2_g, ln2_b]
    small_m = [m_b_ada, m_lam_re, m_lam_im, m_log_dt, m_ssm_b_re, m_ssm_b_im, m_ssm_c_re, m_ssm_c_im,
               m_ssm_d, m_pool_scale, m_ln1_g, m_ln1_b, m_ln2_g, m_ln2_b]
    small_v = [v_b_ada, v_lam_re, v_lam_im, v_log_dt, v_ssm_b_re, v_ssm_b_im, v_ssm_c_re, v_ssm_c_im,
               v_ssm_d, v_pool_scale, v_ln1_g, v_ln1_b, v_ln2_g, v_ln2_b]

    mod, c_all = ada_fwd(c, w_ada[0], b_ada)
    mod = mod.reshape(6, 1, D)
    sh1, sc1, g1, sh2, sc2, g2 = (mod[i] for i in range(6))

    f2, kconst = s5_disc(lam_re[0], lam_im[0], log_dt[0].reshape(G, 1))
    kconst = kconst.reshape(NCONST, SUBLANES, G * P)
    f2r = f2.reshape(2, 1, G * P)
    bt_re = jnp.transpose(ssm_b_re[0], (2, 0, 1)).reshape(H, G * P)
    bt_im = jnp.transpose(ssm_b_im[0], (2, 0, 1)).reshape(H, G * P)
    ct_re = jnp.transpose(ssm_c_re[0], (1, 0, 2)).reshape(H, G * P)
    ct_im = jnp.transpose(ssm_c_im[0], (1, 0, 2)).reshape(H, G * P)
    s5_params = (f2r, bt_re, bt_im, ct_re, ct_im, ssm_d, kconst)

    def e1(t, b):
        xhat, _ = _ln_stats(t[0])
        return [xhat * (1.0 + b[0]) + b[1]], []
    (h1,) = _rowwise("ln_mod1", e1, S, ts, [(x2d, D, 0)], [sc1, sh1], [(D, BF16)], [])

    (proj,) = mm_nn("proj", h1, wg_in, F32, 2)
    z, xsb_all, zp = s5_fwd(proj, s5_params, nblk)
    (vt,) = mm_nn("glu", z, wg_vg, BF16, 4)
    pooled = pool_fwd(proj, W, W, gw)

    def pool_epi(vals, ex, outs):
        a = vals[0]
        outs[0][...] = a
        outs[1][...] = (a * ex[0][...]).astype(BF16)
    tmp = _tile(S, 1024)
    yp, ypool = _mm(
        "pool_mix", "nn", pooled, wp_full.astype(BF16), (S // tmp, nwin, 1),
        pl.BlockSpec((tmp, gw), lambda i, j, k: (i, j)), pl.BlockSpec((1, gw, gw), lambda i, j, k: (j, 0, 0)),
        [(_sds((S, W), F32), pl.BlockSpec((tmp, gw), lambda i, j, k: (i, j))),
         (_sds((S, W), BF16), pl.BlockSpec((tmp, gw), lambda i, j, k: (i, j)))],
        (tmp, gw), 1, gw, None, pool_epi,
        [(pool_scale, pl.BlockSpec((1, gw), lambda i, j, k: (0, j)))])
    (y_b,) = mm_nn("pool_out", ypool, wg_po, BF16, 4)

    cb = D // NDEV
    ga_cb, gb_cb = (2 * W) // cb, (2 * W + D) // cb
    mcb = 4
    wm = mcb * cb
    tsm = _tile(S, 256)

    def merge_call(name, fn, ins, n_out, after=()):
        def body(*refs):
            vals = [r[...].astype(F32) for r in refs[:len(ins)]]
            for r, v in zip(refs[len(ins):], fn(*vals)):
                r[...] = v.astype(r.dtype)
        return pl.pallas_call(
            _with_after(body, len(ins), after), name=name, grid=(S // tsm, NDEV // mcb),
            in_specs=[pl.BlockSpec((tsm, w), f) for (_, w, f) in ins] + [ANY] * len(after),
            out_specs=[pl.BlockSpec((tsm, w), lambda i, j: (i, j)) for (_, w) in n_out],
            out_shape=[_sds((S, cols), BF16) for (cols, _) in n_out],
            compiler_params=_params(("parallel", "parallel")),
        )(*[a for (a, _, _) in ins], *after)

    merge_ins = [(proj, wm, lambda i, j: (i, ga_cb // mcb + j)), (proj, wm, lambda i, j: (i, gb_cb // mcb + j)),
                 (vt, 2 * wm, lambda i, j: (i, j)), (y_b, wm, lambda i, j: (i, j))]

    def val_gate(vtv):
        return (jnp.concatenate([vtv[:, 2 * q * cb:(2 * q + 1) * cb] for q in range(mcb)], axis=1),
                jnp.concatenate([vtv[:, (2 * q + 1) * cb:(2 * q + 2) * cb] for q in range(mcb)], axis=1))

    def merge_f(ga, gb, vtv, yb):
        vv, tt = val_gate(vtv)
        return [_sigmoid(ga) * (vv * _sigmoid(tt)) + _sigmoid(gb) * yb]
    (merged,) = merge_call("merge", merge_f, merge_ins, [(D, wm)])

    (mix,) = mm_nn("mix_out", merged, wout_full, F32, 1)

    def e3(t, b):
        xv, mx = t
        g1v, l1g, l1b, sc2v, sh2v = b
        r1 = ALPHA * xv + g1v * mx
        xh1, _ = _ln_stats(r1)
        x1 = xh1 * l1g + l1b
        xh, _ = _ln_stats(x1)
        return [r1, xh * (1.0 + sc2v) + sh2v], []
    r1, h2 = _rowwise("post_mix", e3, S, ts, [(x2d, D, 0), (mix, D, 0)],
                      [g1, ln1_g, ln1_b, sc2, sh2], [(D, F32), (D, BF16)], [])

    def relu_epi(vals, ex, outs):
        outs[0][...] = jnp.maximum(vals[0], 0.0).astype(BF16)
    (rl,) = mm_nn("ff1", h2, wg_ff1, BF16, 1, epi=relu_epi)

    def square(a):
        return a * a
    (y2,) = mm_nn("ff2", rl, wff2_full, F32, 1, pro=square)

    def e4(t, b):
        r1v, y2v, tg = t
        g2v, l1g, l1b, l2g, l2b = b
        xh1, _ = _ln_stats(r1v)
        x1 = xh1 * l1g + l1b
        r2 = ALPHA * x1 + g2v * y2v
        xh2, rs2 = _ln_stats(r2)
        err = xh2 * l2g + l2b - tg
        dx2 = err * (1.0 / D)
        dr2 = _ln_bwd(dx2 * l2g, xh2, rs2)
        lsum = jnp.sum(_colsum(err * err), axis=1, keepdims=True) * (0.5 / D)
        return ([ALPHA * dr2, g2v * dr2],
                [jnp.broadcast_to(lsum, (1, LANES)), _colsum(dx2 * xh2), _colsum(dx2), _colsum(dr2 * y2v)])
    dx1a, dy2, loss_acc, g_ln2g, g_ln2b, d_g2 = _rowwise(
        "head", e4, S, ts, [(r1, D, 0), (y2, D, 0), (tgt, D, 0)], [g2, ln1_g, ln1_b, ln2_g, ln2_b],
        [(D, F32), (D, BF16)], [LANES, D, D, D])

    tn_ff = _tile(4 * D, 1024)

    def dff_epi(vals, ex, outs):
        outs[0][...] = (vals[0] * (2.0 * ex[0][...].astype(F32))).astype(BF16)
    tmf = _tile(S, 1024)
    (da1,) = mm_nt("d_ff2", dy2, wff2_full, BF16, 1, tn=tn_ff, epi=dff_epi,
                   extras=[(rl, pl.BlockSpec((tmf, tn_ff), lambda i, j, k: (i, j)))])
    gw_ff2 = mm_tn("gw_ff2", rl, dy2, BF16, NDEV, 0, pro=square)
    gw_ff1 = mm_tn("gw_ff1", h2, da1, BF16, NDEV, 1)
    tok, wait_pair_a = pair_exchange("pair_exchange_ff", [gw_ff2, gw_ff1], 4)
    (dh2,) = mm_nt("d_ff1", da1, wg_ff1, F32, 4, after=[tok])

    def e5(t, b):
        dh2v, r1v, dx1av, mx = t
        sc2v, l1g, l1b, g1v = b
        xh1, rs1 = _ln_stats(r1v)
        x1 = xh1 * l1g + l1b
        xh, rs = _ln_stats(x1)
        dx1 = dx1av + _ln_bwd(dh2v * (1.0 + sc2v), xh, rs)
        dr1 = _ln_bwd(dx1 * l1g, xh1, rs1)
        return ([ALPHA * dr1, g1v * dr1],
                [_colsum(dh2v * xh), _colsum(dh2v), _colsum(dx1 * xh1), _colsum(dx1), _colsum(dr1 * mx)])
    dxa, dmix, d_sc2, d_sh2, g_ln1g, g_ln1b, d_g1 = _rowwise(
        "post_mix_bwd", e5, S, ts, [(dh2, D, 0), (r1, D, 0), (dx1a, D, 0), (mix, D, 0)],
        [sc2, ln1_g, ln1_b, g1], [(D, F32), (D, BF16)], [D, D, D, D, D])

    (dmerged,) = mm_nt("d_mix_out", dmix, wout_full, BF16, 1)
    gw_out = mm_tn("gw_out", merged, dmix, BF16, NDEV, 0)
    grads_a, got_a = wait_pair_a(gw_out)
    parts_a = [pair_sum("pair_sum_ff%d" % i, g, t, place) for i, (g, t) in enumerate(zip(grads_a, got_a))]
    tok, wait_chip_a = chip_exchange("chip_exchange_ff", parts_a, 5)

    def merge_b(ga, gb, vtv, yb, dm):
        vv, tt = val_gate(vtv)
        sa, sb, st = _sigmoid(ga), _sigmoid(gb), _sigmoid(tt)
        dya = dm * sa
        dv, dt = dya * st, dya * vv * st * (1.0 - st)
        dvt_tile = jnp.concatenate([t[:, q * cb:(q + 1) * cb] for q in range(mcb) for t in (dv, dt)], axis=1)
        return [dm * (vv * st) * sa * (1.0 - sa), dm * yb * sb * (1.0 - sb), dvt_tile, dm * sb]
    dga, dgb_, dvt, dy_b = merge_call(
        "merge_bwd", merge_b, merge_ins + [(dmerged, wm, lambda i, j: (i, j))],
        [(D, wm), (D, wm), (2 * D, 2 * wm), (D, wm)], after=[tok])

    (dypool,) = mm_nt("d_pool_out", dy_b, wg_po, F32, NDEV)
    gw_po = mm_tn("gw_pool_out", ypool, dy_b, BF16, NDEV, 4)

    def e7(t, b):
        return [t[0] * b[0]], [_colsum(t[0] * t[1])]
    dyp, g_pscale = _rowwise("pool_scale_bwd", e7, S, ts, [(dypool, W, 0), (yp, W, 0)],
                             [pool_scale], [(W, BF16)], [W])
    (dpooled,) = _mm(
        "d_pool_mix", "nt", dyp, wp_full.astype(BF16), (S // tmp, nwin, 1),
        pl.BlockSpec((tmp, gw), lambda i, j, k: (i, j)), pl.BlockSpec((1, gw, gw), lambda i, j, k: (j, 0, 0)),
        [(_sds((S, W), F32), pl.BlockSpec((tmp, gw), lambda i, j, k: (i, j)))], (tmp, gw), 1, gw)
    tkp = _tile(S, 2048)
    gw_pool = _mm(
        "gw_pool", "tn", pooled, dyp, (nwin, 1, S // tkp),
        pl.BlockSpec((tkp, gw), lambda i, j, k: (k, i)), pl.BlockSpec((tkp, gw), lambda i, j, k: (k, i)),
        [(_sds((nwin, gw, gw), BF16), pl.BlockSpec((1, gw, gw), lambda i, j, k: (i, 0, 0)))],
        (gw, gw), 1, gw, stacked_out=True)[0]
    du_pool = pool_bwd(dpooled, gw)

    (dz,) = mm_nt("d_glu", dvt, wg_vg, BF16, 2 * NDEV)
    gw_vg = mm_tn("gw_glu", z, dvt, BF16, 2 * NDEV, 4)
    gw_pool_st = jnp.transpose(gw_pool.reshape(nwin, NDEV, gw // NDEV, gw), (1, 0, 2, 3))
    grads_b = [gw_out, gw_po, gw_pool_st, gw_vg.reshape(NDEV, 2, W, D // NDEV)]
    tok, wait_pair_b = pair_exchange("pair_exchange_mix", grads_b, 6)
    du_ssm, g_bt_re, g_bt_im, g_ct_re, g_ct_im, g_f, g_d, g_a = s5_bwd(
        proj, xsb_all, dz, zp, s5_params, nblk, after=[tok])
    grads_b, got_b = wait_pair_b(du_ssm)
    parts_b = [pair_sum("pair_sum_mix%d" % i, g, t, place) for i, (g, t) in enumerate(zip(grads_b, got_b))]
    tok, wait_chip_b = chip_exchange("chip_exchange_mix", parts_b, 7)

    dproj = jnp.concatenate([du_ssm, du_pool, dga, dgb_], axis=1)
    gw_in = mm_tn("gw_in", h1, dproj, BF16, NDEV, 1, after=[tok])
    tok, wait_pair_c = pair_exchange("pair_exchange_in", [gw_in], 8)
    (dh1,) = mm_nt("d_proj", dproj, wg_in, F32, 4, after=[tok])
    grads_c, got_c = wait_pair_c(dh1)
    parts_c = [pair_sum("pair_sum_in", grads_c[0], got_c[0], place)]
    tok, wait_chip_c = chip_exchange("chip_exchange_in", parts_c, 9)

    def e10(t, b):
        dh1v, xv, dxav = t
        xh, rs = _ln_stats(xv)
        return ([dxav + _ln_bwd(dh1v * (1.0 + b[0]), xh, rs)],
                [_colsum(dh1v * xh), _colsum(dh1v)])
    grad_x, d_sc1, d_sh1 = _rowwise("ln_mod1_bwd", e10, S, ts, [(dh1, D, 0), (x2d, D, 0), (dxa, D, 0)],
                                    [sc1], [(D, F32)], [D, D], after=[tok])

    g_b_re = jnp.transpose(g_bt_re.reshape(H, G, P), (1, 0, 2))
    g_b_im = jnp.transpose(g_bt_im.reshape(H, G, P), (1, 0, 2))
    g_c_re = jnp.transpose(g_ct_re.reshape(H, G, P), (1, 0, 2))
    g_c_im = jnp.transpose(g_ct_im.reshape(H, G, P), (1, 0, 2))
    d_ab = jnp.transpose(g_a.reshape(nblk, 2, GPB, P), (1, 0, 2, 3)).reshape(2, G, P)
    g_lr, g_li, g_ldt = s5_disc_bwd(lam_re[0], lam_im[0], log_dt[0].reshape(G, 1), d_ab,
                                    g_f.reshape(2, G, P))

    dmod = jnp.concatenate([d_sh1, d_sc1, d_g1, d_sh2, d_sc2, d_g2], axis=1)
    small_g = [dmod, g_lr, g_li, g_ldt, g_b_re, g_b_im, g_c_re, g_c_im, g_d, g_pscale,
               g_ln1g, g_ln1b, g_ln2g, g_ln2b, loss_acc]
    packed_g = _small_pack(small_g)
    (parts_all,) = seq_all_gather("gather_small", [packed_g], 10)
    glu_w = jnp.stack([w_glu_val[0], w_glu_gate[0]])
    glu_m = jnp.stack([m_w_glu_val[0], m_w_glu_gate[0]])
    glu_v = jnp.stack([v_w_glu_val[0], v_w_glu_gate[0]])
    wmv = [(w_ff2[0], m_w_ff2[0], v_w_ff2[0]), (w_ff1[0], m_w_ff1[0], v_w_ff1[0]),
           (w_out[0], m_w_out[0], v_w_out[0]), (w_pool_out[0], m_w_pool_out[0], v_w_pool_out[0]),
           (w_pool[0], m_w_pool[0], v_w_pool[0]), (glu_w, glu_m, glu_v)]
    _, got3_a = wait_chip_a(packed_g)
    upd = [adamw_sharded("adamw_%d" % i, g, p, t, w, m, v, place)
           for i, (g, p, t, (w, m, v)) in enumerate(zip(grads_a, got_a, got3_a, wmv[:2]))]
    _, got3_b = wait_chip_b(upd[-1][0])
    upd += [adamw_sharded("adamw_%d" % (2 + i), g, p, t, w, m, v, place)
            for i, (g, p, t, (w, m, v)) in enumerate(zip(grads_b, got_b, got3_b, wmv[2:]))]
    u_ff2, u_ff1, u_out, u_po, u_pool, u_glu = upd

    gsum = sum_small(parts_all, after=[upd[-1][0]])
    def swap_b(ts_):
        return [jnp.swapaxes(t, 2, 3) if i in (4, 5) else t for i, t in enumerate(ts_)]

    sg = _small_unpack(gsum, [t.shape for t in swap_b(small_names)] + [(1, LANES)])
    loss, sg = sg[-1][0, 0], sg[:-1]
    sd, sm, sv = adamw_natural(sg, swap_b(small_names), swap_b(small_m), swap_b(small_v))
    sg, sd, sm, sv = swap_b(sg), swap_b(sd), swap_b(sm), swap_b(sv)

    nmod = 6 * D
    dmod_all = parts_all[:, :nmod // LANES, :].reshape(NDEV, nmod)
    c_all_t = jnp.transpose(c_all.reshape(NDEV, D))
    ada_out = adamw_ada(c_all_t, dmod_all, w_ada[0], m_w_ada[0], v_w_ada[0], my_dev)
    _, got3_c = wait_chip_c(ada_out[0])
    u_in = adamw_sharded("adamw_6", grads_c[0], got_c[0], got3_c[0], w_in[0], m_w_in[0], v_w_in[0], place)

    def pick(k):
        return [ada_out[k][None], sg_sd[k][0], u_in[k][None]] + [t for t in sg_sd[k][1:9]] + \
               [u_glu[k][0][None], u_glu[k][1][None], u_pool[k][None], sg_sd[k][9], u_po[k][None],
                u_out[k][None], sg_sd[k][10], sg_sd[k][11], u_ff1[k][None], u_ff2[k][None],
                sg_sd[k][12], sg_sd[k][13]]

    sg_sd = [sg, sd, sm, sv]
    return (loss, grad_x[None], *pick(0), *pick(1), *pick(2), *pick(3))
```

```python
import functools
import math

import jax
import jax.numpy as jnp
from jax import lax
from jax.experimental import pallas as pl
from jax.experimental.pallas import tpu as pltpu
from jax.experimental.pallas import tpu_sc as plsc

F32 = jnp.float32
BF16 = jnp.bfloat16
MESH = pl.DeviceIdType.MESH
NDEV = 8
NCHIP = 4

SSM_GROUP = 16
SSM_STATE = 64
GROUPS_PER_BLOCK = 8
POOL_WINDOWS = (2, 4, 8, 16)
LN_EPS = 1e-5
ALPHA = 2.0 ** 0.25
ADAM_LR, ADAM_B1, ADAM_B2, ADAM_EPS, ADAM_WD, ADAM_STEP = 0.001, 0.9, 0.999, 1e-08, 0.01, 10
SUBLANES = 8
LANES = 128
VMEM_LIMIT = 56 * 1024 * 1024


def _params(sem=None, vmem=VMEM_LIMIT):
    return pltpu.CompilerParams(dimension_semantics=sem, vmem_limit_bytes=vmem)


def _tile(n, pref):
    if n <= pref:
        return n
    t = 1 << (pref.bit_length() - 1)
    while n % t:
        t //= 2
    return t


def _cast_epi(vals, ex, outs):
    c = vals[0].shape[1]
    for s, v in enumerate(vals):
        outs[0][:, s * c:(s + 1) * c] = v.astype(outs[0].dtype)


ANY = pl.BlockSpec(memory_space=pl.ANY)


def _with_after(body, n_in, after):
    if not after:
        return body
    n_af = len(after)

    def wrapped(*refs):
        return body(*refs[:n_in], *refs[n_in + n_af:])
    return wrapped


def _mm(name, kind, a, b, grid, a_spec, b_spec, outs, acc_shape, nsub=1, c=None,
        pro=None, epi=None, extras=(), stacked_out=False, after=()):
    nk = grid[2]
    n_ex, n_out = len(extras), len(outs)

    def finish(vals, ex, out_refs):
        if epi is not None:
            epi(vals, ex, out_refs)
        elif stacked_out:
            for s, v in enumerate(vals):
                out_refs[0][s] = v.astype(out_refs[0].dtype)
        else:
            _cast_epi(vals, ex, out_refs)

    def body(*refs):
        mm_step(refs[0], refs[1], refs[2:2 + n_ex], refs[2 + n_ex:2 + n_ex + n_out], refs[-1])

    def mm_step(a_ref, b_ref, ex, out_refs, acc):
        k = pl.program_id(2)
        av = a_ref[...]
        if pro is not None:
            av = pro(av)
        if kind == "nn":
            prods = [jnp.dot(av, b_ref[s], preferred_element_type=F32) for s in range(nsub)]
        elif kind == "nt":
            t = None
            for s in range(nsub):
                d = lax.dot_general(av[:, s * c:(s + 1) * c], b_ref[s], (((1,), (1,)), ((), ())),
                                    preferred_element_type=F32)
                t = d if t is None else t + d
            prods = [t]
        else:
            t = lax.dot_general(av, b_ref[...], (((0,), (0,)), ((), ())), preferred_element_type=F32)
            prods = [t[:, s * c:(s + 1) * c] for s in range(nsub)] if stacked_out else [t]
        if nk == 1:
            finish(prods, ex, out_refs)
            return
        w = prods[0].shape[1]

        @pl.when(k == 0)
        def _():
            for s, p in enumerate(prods):
                acc[:, s * w:(s + 1) * w] = p

        @pl.when(jnp.logical_and(k > 0, k < nk - 1))
        def _():
            for s, p in enumerate(prods):
                acc[:, s * w:(s + 1) * w] += p

        @pl.when(k == nk - 1)
        def _():
            finish([acc[:, s * w:(s + 1) * w] + p for s, p in enumerate(prods)], ex, out_refs)

    return pl.pallas_call(
        _with_after(body, 2 + n_ex, after), name=name, grid=grid,
        in_specs=[a_spec, b_spec] + [e[1] for e in extras] + [ANY] * len(after),
        out_specs=[o[1] for o in outs],
        out_shape=[o[0] for o in outs],
        scratch_shapes=[pltpu.VMEM(acc_shape, F32)] if nk > 1 else [],
        compiler_params=_params(("parallel", "parallel", "arbitrary")),
    )(a, b, *[e[0] for e in extras], *after)


def _sds(shape, dtype):
    return jax.ShapeDtypeStruct(shape, dtype)


def mm_nn(name, a, b3, out_dtype, nsub, tm=1024, tk=2048, tn=None, pro=None, epi=None,
          extras=(), after=()):
    M = a.shape[0]
    nb, K, cdim = b3.shape
    tm, tk = _tile(M, tm), _tile(K, tk)
    if nb == 1:
        tn = _tile(cdim, tn or 1024)
        nsub, c, nj = 1, tn, cdim // tn
        b_spec = pl.BlockSpec((1, tk, tn), lambda i, j, k: (0, k, j))
        N = cdim
    else:
        c, nj, tn = cdim, nb // nsub, nsub * cdim
        b_spec = pl.BlockSpec((nsub, tk, cdim), lambda i, j, k: (j, k, 0))
        N = nb * cdim
    a_spec = pl.BlockSpec((tm, tk), lambda i, j, k: (i, k))
    grid = (M // tm, nj, K // tk)
    outs = [(_sds((M, N), out_dtype), pl.BlockSpec((tm, tn), lambda i, j, k: (i, j)))]
    return _mm(name, "nn", a, b3, grid, a_spec, b_spec, outs, (tm, tn), nsub, c, pro, epi, extras,
               after=after)


def mm_nt(name, a, b3, out_dtype, nsub, tm=1024, tn=1024, epi=None, extras=(), after=()):
    M = a.shape[0]
    nb, N, cdim = b3.shape
    tm, tn = _tile(M, tm), _tile(N, tn)
    if nb == 1:
        tk = _tile(cdim, 2048)
        nsub, c, nk = 1, tk, cdim // tk
        b_spec = pl.BlockSpec((1, tn, tk), lambda i, j, k: (0, j, k))
    else:
        c, nk, tk = cdim, nb // nsub, nsub * cdim
        b_spec = pl.BlockSpec((nsub, tn, cdim), lambda i, j, k: (k, j, 0))
    a_spec = pl.BlockSpec((tm, tk), lambda i, j, k: (i, k))
    grid = (M // tm, N // tn, nk)
    outs = [(_sds((M, N), out_dtype), pl.BlockSpec((tm, tn), lambda i, j, k: (i, j)))]
    return _mm(name, "nt", a, b3, grid, a_spec, b_spec, outs, (tm, tn), nsub, c, None, epi, extras,
               after=after)


def mm_tn(name, a, b, out_dtype, nb, nsub, tma=1024, tk=2048, pro=None, after=()):
    S, Ka = a.shape
    N = b.shape[1]
    tk, tma = _tile(S, tk), _tile(Ka, tma)
    a_spec = pl.BlockSpec((tk, tma), lambda i, j, k: (k, i))
    if nsub == 0:
        tn = _tile(N, 1024)
        res = _mm(name, "tn", a, b, (Ka // tma, N // tn, S // tk), a_spec,
                  pl.BlockSpec((tk, tn), lambda i, j, k: (k, j)),
                  [(_sds((Ka, N), out_dtype), pl.BlockSpec((tma, tn), lambda i, j, k: (i, j)))],
                  (tma, tn), 1, tn, pro, None, (), after=after)[0]
        return res.reshape(nb, Ka // nb, N)
    c = N // nb
    tn = nsub * c
    outs = [(_sds((nb, Ka, c), out_dtype), pl.BlockSpec((nsub, tma, c), lambda i, j, k: (j, i, 0)))]
    return _mm(name, "tn", a, b, (Ka // tma, nb // nsub, S // tk), a_spec,
               pl.BlockSpec((tk, tn), lambda i, j, k: (k, j)), outs, (tma, tn), nsub, c,
               pro, None, (), stacked_out=True, after=after)[0]


def _rowwise(name, fn, S, ts, tiled, bcast, tiled_out, acc_out, after=()):
    nt, nb, no, na = len(tiled), len(bcast), len(tiled_out), len(acc_out)

    def body(*refs):
        tin = [r[...] for r in refs[:nt]]
        bin_ = [r[...] for r in refs[nt:nt + nb]]
        o_refs = refs[nt + nb:nt + nb + no]
        a_refs = refs[nt + nb + no:]
        touts, aouts = fn(tin, bin_)
        for r, v in zip(o_refs, touts):
            r[...] = v.astype(r.dtype)
        i = pl.program_id(0)

        @pl.when(i == 0)
        def _():
            for r, v in zip(a_refs, aouts):
                r[...] = v

        @pl.when(i > 0)
        def _():
            for r, v in zip(a_refs, aouts):
                r[...] += v

    in_specs = [pl.BlockSpec((ts, w), functools.partial(lambda i, cb: (i, cb), cb=cb))
                for (_, w, cb) in tiled]
    in_specs += [pl.BlockSpec(b.shape, lambda i: (0, 0)) for b in bcast]
    out_specs = [pl.BlockSpec((ts, w), lambda i: (i, 0)) for (w, _) in tiled_out]
    out_specs += [pl.BlockSpec((1, w), lambda i: (0, 0)) for w in acc_out]
    out_shape = [_sds((S, w), d) for (w, d) in tiled_out] + [_sds((1, w), F32) for w in acc_out]
    return pl.pallas_call(
        _with_after(body, nt + nb, after), name=name, grid=(S // ts,),
        in_specs=in_specs + [ANY] * len(after), out_specs=out_specs,
        out_shape=out_shape, compiler_params=_params(("arbitrary",)),
    )(*[t[0] for t in tiled], *bcast, *after)


def _ln_stats(v):
    mu = jnp.mean(v, axis=-1, keepdims=True)
    vc = v - mu
    var = jnp.mean(vc * vc, axis=-1, keepdims=True)
    rstd = lax.rsqrt(var + LN_EPS)
    return vc * rstd, rstd


def _ln_bwd(dxhat, xhat, rstd):
    return rstd * (dxhat - jnp.mean(dxhat, axis=-1, keepdims=True)
                   - xhat * jnp.mean(dxhat * xhat, axis=-1, keepdims=True))


def _colsum(v):
    return jnp.sum(v, axis=0, keepdims=True)


def _sigmoid(v):
    return 1.0 / (1.0 + jnp.exp(-v))


_GELU_C = math.sqrt(2.0 / math.pi)


def _gelu(v):
    return 0.5 * v * (1.0 + jnp.tanh(_GELU_C * (v + 0.044715 * v * v * v)))


def _gelu_grad(v):
    t = jnp.tanh(_GELU_C * (v + 0.044715 * v * v * v))
    return 0.5 * (1.0 + t) + 0.5 * v * (1.0 - t * t) * _GELU_C * (1.0 + 3 * 0.044715 * v * v)


def _disc(lr, li, ldt):
    dt = jnp.exp(ldt)
    mag = jnp.exp(lr * dt)
    ang = li * dt
    ab_re = mag * jnp.cos(ang)
    ab_im = mag * jnp.sin(ang)
    num_re = ab_re - 1.0
    num_im = ab_im
    den = lr * lr + li * li
    f_re = (num_re * lr + num_im * li) / den
    f_im = (num_im * lr - num_re * li) / den
    return ab_re, ab_im, f_re, f_im


def _cmul(ar, ai, br, bi):
    return ar * br - ai * bi, ar * bi + ai * br


SCAN_FOLD = 4
NCONST = 18


def s5_disc(lam_re, lam_im, log_dt):
    G, P = lam_re.shape

    def body(lr_ref, li_ref, ldt_ref, f_ref, k_ref):
        ab_re, ab_im, f_re, f_im = _disc(lr_ref[...], li_ref[...], ldt_ref[...])
        f_ref[0] = f_re
        f_ref[1] = f_im
        fr, fi = ab_re, ab_im
        for _ in range(SCAN_FOLD - 1):
            fr, fi = _cmul(fr, fi, ab_re, ab_im)
        pr, pi = [fr], [fi]
        for _ in range(SUBLANES - 1):
            nr, ni = _cmul(pr[-1], pi[-1], fr, fi)
            pr.append(nr)
            pi.append(ni)
        zero = jnp.zeros_like(ab_re)
        for r in range(SUBLANES):
            k_ref[16, r] = ab_re
            k_ref[17, r] = ab_im
        for n, sh in enumerate((1, 2, 4)):
            for r in range(SUBLANES):
                k_ref[2 * n, r] = pr[sh - 1] if r >= sh else zero
                k_ref[2 * n + 1, r] = pi[sh - 1] if r >= sh else zero
                k_ref[8 + 2 * n, r] = pr[sh - 1] if r + sh < SUBLANES else zero
                k_ref[8 + 2 * n + 1, r] = -pi[sh - 1] if r + sh < SUBLANES else zero
        for r in range(SUBLANES):
            k_ref[6, r] = pr[r]
            k_ref[7, r] = pi[r]
            k_ref[14, r] = pr[SUBLANES - 1 - r]
            k_ref[15, r] = -pi[SUBLANES - 1 - r]

    vm = pl.BlockSpec(memory_space=pltpu.VMEM)
    return pl.pallas_call(
        body, name="s5_disc", in_specs=[vm, vm, vm], out_specs=[vm, vm],
        out_shape=[_sds((2, G, P), F32), _sds((NCONST, SUBLANES, G, P), F32)],
    )(lam_re, lam_im, log_dt)


def s5_disc_bwd(lam_re, lam_im, log_dt, d_ab, d_f):
    G, P = lam_re.shape

    def body(lr_ref, li_ref, ldt_ref, dab_ref, df_ref, glr_ref, gli_ref, gdt_ref):
        _, vjp = jax.vjp(_disc, lr_ref[...], li_ref[...], ldt_ref[...])
        glr, gli, gdt = vjp((dab_ref[0], dab_ref[1], df_ref[0], df_ref[1]))
        glr_ref[...] = glr
        gli_ref[...] = gli
        gdt_ref[...] = gdt

    vm = pl.BlockSpec(memory_space=pltpu.VMEM)
    return pl.pallas_call(
        body, name="s5_disc_bwd", in_specs=[vm] * 5, out_specs=[vm] * 3,
        out_shape=[_sds((G, P), F32), _sds((G, P), F32), _sds((G, 1), F32)],
    )(lam_re, lam_im, log_dt, d_ab, d_f)


def _group_mask(cw, nst):
    row = lax.broadcasted_iota(jnp.int32, (cw, 2 * nst), 0) // SSM_GROUP
    col = (lax.broadcasted_iota(jnp.int32, (cw, 2 * nst), 1) % nst) // SSM_STATE
    return row == col


def _spread(t, mask):
    reps = mask.shape[0] // t.shape[0]
    return jnp.where(mask, jnp.tile(t, (reps, 1)), 0.0).astype(BF16)


def _gather_groups(t, mask):
    t = jnp.where(mask, t, 0.0)
    out = t[0:SSM_GROUP]
    for g in range(1, t.shape[0] // SSM_GROUP):
        out = out + t[g * SSM_GROUP:(g + 1) * SSM_GROUP]
    return out


def _s5_operands(f_ref, br_ref, bi_ref, cr_ref, ci_ref, mask):
    fr, fi = f_ref[0], f_ref[1]
    br, bi = br_ref[...], bi_ref[...]
    bm = _spread(jnp.concatenate([fr * br - fi * bi, fr * bi + fi * br], axis=1), mask)
    cm = _spread(jnp.concatenate([cr_ref[...], -ci_ref[...]], axis=1), mask)
    return bm, cm


def _planes_put(ref, val):
    for c in range(ref.shape[0]):
        ref[c] = val[:, c * LANES:(c + 1) * LANES]


def _planes_get(ref):
    return jnp.concatenate([ref[c] for c in range(ref.shape[0])], axis=1)


def _rows_ld(ref, start, lo, hi):
    rows = pl.ds(start, SUBLANES, stride=SCAN_FOLD)
    return jnp.concatenate([ref[c, rows, :] for c in range(lo // LANES, hi // LANES)], axis=1)


def _rows_st(ref, start, lo, val):
    rows = pl.ds(start, SUBLANES, stride=SCAN_FOLD)
    for k in range(val.shape[1] // LANES):
        ref[lo // LANES + k, rows, :] = val[:, k * LANES:(k + 1) * LANES]


def _phases(ref, base, lo, hi):
    return [_rows_ld(ref, base + j, lo, hi) for j in range(SCAN_FOLD)]


def _row_bcast(v, r):
    return jnp.broadcast_to(v[r:r + 1, :], v.shape)


def _scan_fwd(xs, k_ref, nst):
    m = SCAN_FOLD
    ngroup = xs.shape[1] // (SUBLANES * m)
    row = lax.broadcasted_iota(jnp.int32, (SUBLANES, nst), 0)

    def step(t, carry):
        cr, ci = carry
        base = pl.multiple_of(t * (SUBLANES * m), SUBLANES * m)
        ar, ai = k_ref[16], k_ref[17]
        pr, pi = _phases(xs, base, 0, nst), _phases(xs, base, nst, 2 * nst)
        vr, vi = pr[0], pi[0]
        for j in range(1, m):
            vr, vi = pr[j] + ar * vr - ai * vi, pi[j] + ar * vi + ai * vr
        for n, sh in enumerate((1, 2, 4)):
            sr = pltpu.roll(vr, sh, 0)
            si = pltpu.roll(vi, sh, 0)
            mr, mi = k_ref[2 * n], k_ref[2 * n + 1]
            vr, vi = vr + mr * sr - mi * si, vi + mr * si + mi * sr
        qr, qi = k_ref[6], k_ref[7]
        vr, vi = vr + qr * cr - qi * ci, vi + qr * ci + qi * cr
        _rows_st(xs, base + m - 1, 0, vr)
        _rows_st(xs, base + m - 1, nst, vi)
        xr = jnp.where(row == 0, cr, pltpu.roll(vr, 1, 0))
        xi = jnp.where(row == 0, ci, pltpu.roll(vi, 1, 0))
        for j in range(m - 1):
            xr, xi = pr[j] + ar * xr - ai * xi, pi[j] + ar * xi + ai * xr
            _rows_st(xs, base + j, 0, xr)
            _rows_st(xs, base + j, nst, xi)
        return _row_bcast(vr, SUBLANES - 1), _row_bcast(vi, SUBLANES - 1)

    zero = jnp.zeros((SUBLANES, nst), F32)
    lax.fori_loop(0, ngroup, step, (zero, zero))


def _scan_bwd(g, xs, k_ref, nst):
    m = SCAN_FOLD
    ngroup = g.shape[1] // (SUBLANES * m)
    row = lax.broadcasted_iota(jnp.int32, (SUBLANES, nst), 0)

    def step(tt, carry):
        cr, ci, dar, dai = carry
        t = ngroup - 1 - tt
        base = pl.multiple_of(t * (SUBLANES * m), SUBLANES * m)
        ar, ai = k_ref[16], -k_ref[17]
        dr, di = _phases(g, base, 0, nst), _phases(g, base, nst, 2 * nst)
        wr, wi = dr[m - 1], di[m - 1]
        for j in range(m - 2, -1, -1):
            wr, wi = dr[j] + ar * wr - ai * wi, di[j] + ar * wi + ai * wr
        for n, sh in enumerate((1, 2, 4)):
            sr = pltpu.roll(wr, SUBLANES - sh, 0)
            si = pltpu.roll(wi, SUBLANES - sh, 0)
            mr, mi = k_ref[8 + 2 * n], k_ref[8 + 2 * n + 1]
            wr, wi = wr + mr * sr - mi * si, wi + mr * si + mi * sr
        qr, qi = k_ref[14], k_ref[15]
        wr, wi = wr + qr * cr - qi * ci, wi + qr * ci + qi * cr
        gr, gi = [None] * m, [None] * m
        gr[0], gi[0] = wr, wi
        nr = jnp.where(row == SUBLANES - 1, cr, pltpu.roll(wr, SUBLANES - 1, 0))
        ni = jnp.where(row == SUBLANES - 1, ci, pltpu.roll(wi, SUBLANES - 1, 0))
        for j in range(m - 1, 0, -1):
            nr, ni = dr[j] + ar * nr - ai * ni, di[j] + ar * ni + ai * nr
            gr[j], gi[j] = nr, ni
        for j in range(m):
            _rows_st(g, base + j, 0, gr[j])
            _rows_st(g, base + j, nst, gi[j])
        xr, xi = _phases(xs, base, 0, nst), _phases(xs, base, nst, 2 * nst)
        pbase = pl.multiple_of(jnp.maximum(t - 1, 0) * (SUBLANES * m), SUBLANES * m)
        live = (t > 0).astype(F32)
        lr = _row_bcast(_rows_ld(xs, pbase + m - 1, 0, nst), SUBLANES - 1) * live
        li = _row_bcast(_rows_ld(xs, pbase + m - 1, nst, 2 * nst), SUBLANES - 1) * live
        xmr = [jnp.where(row == 0, lr, pltpu.roll(xr[m - 1], 1, 0))] + xr[:m - 1]
        xmi = [jnp.where(row == 0, li, pltpu.roll(xi[m - 1], 1, 0))] + xi[:m - 1]
        for j in range(m):
            dar = dar + gr[j] * xmr[j] + gi[j] * xmi[j]
            dai = dai + gi[j] * xmr[j] - gr[j] * xmi[j]
        return _row_bcast(wr, 0), _row_bcast(wi, 0), dar, dai

    zero = jnp.zeros((SUBLANES, nst), F32)
    _, _, dar, dai = lax.fori_loop(0, ngroup, step, (zero, zero, zero, zero))
    return _colsum(dar), _colsum(dai)


def _s5_param_specs(cw, nst):
    hp = pl.BlockSpec((SSM_GROUP, nst), lambda b: (0, b))
    return [pl.BlockSpec((2, 1, nst), lambda b: (0, 0, b)), hp, hp, hp, hp,
            pl.BlockSpec((1, cw), lambda b: (0, b)),
            pl.BlockSpec((NCONST, SUBLANES, nst), lambda b: (0, 0, b))]


def s5_fwd(proj, params, nb):
    S = proj.shape[0]
    nst = params[1].shape[1] // nb
    cw = nst // SSM_STATE * SSM_GROUP

    def body(u_ref, f_ref, br_ref, bi_ref, cr_ref, ci_ref, d_ref, k_ref, z_ref, xsb_ref, zp_ref, xs):
        bm, cm = _s5_operands(f_ref, br_ref, bi_ref, cr_ref, ci_ref, _group_mask(cw, nst))
        u = u_ref[...]
        _planes_put(xs, jnp.dot(u.astype(BF16), bm, preferred_element_type=F32))
        _scan_fwd(xs, k_ref, nst)
        xsb = _planes_get(xs).astype(BF16)
        xsb_ref[...] = xsb
        y = lax.dot_general(xsb, cm, (((1,), (1,)), ((), ())), preferred_element_type=F32)
        y = y + d_ref[...] * u
        z_ref[...] = _gelu(y).astype(BF16)
        zp_ref[...] = _gelu_grad(y).astype(BF16)

    return pl.pallas_call(
        body, name="s5_fwd", grid=(nb,),
        in_specs=[pl.BlockSpec((S, cw), lambda b: (0, b))] + _s5_param_specs(cw, nst),
        out_specs=[pl.BlockSpec((S, cw), lambda b: (0, b)), pl.BlockSpec((S, 2 * nst), lambda b: (0, b)),
                   pl.BlockSpec((S, cw), lambda b: (0, b))],
        out_shape=[_sds((S, nb * cw), BF16), _sds((S, nb * 2 * nst), BF16), _sds((S, nb * cw), BF16)],
        scratch_shapes=[pltpu.VMEM((2 * nst // LANES, S, LANES), F32)],
        compiler_params=_params(("arbitrary",)),
    )(proj, *params)


def s5_bwd(proj, xsb_all, dz, zp, params, nb, after=()):
    S = proj.shape[0]
    nst = params[1].shape[1] // nb
    cw = nst // SSM_STATE * SSM_GROUP

    def body(u_ref, xsb_ref, dz_ref, zp_ref, f_ref, br_ref, bi_ref, cr_ref, ci_ref, d_ref, k_ref,
             du_ref, gbr_ref, gbi_ref, gcr_ref, gci_ref, gf_ref, gd_ref, ga_ref, xs, g):
        mask = _group_mask(cw, nst)
        bm, cm = _s5_operands(f_ref, br_ref, bi_ref, cr_ref, ci_ref, mask)
        u = u_ref[...]
        ub = u.astype(BF16)
        d = d_ref[...]
        xsb = xsb_ref[...]
        _planes_put(xs, xsb.astype(F32))
        dy = dz_ref[...].astype(F32) * zp_ref[...].astype(F32)
        gd_ref[...] = _colsum(dy * u)
        dyb = dy.astype(BF16)
        gc = _gather_groups(lax.dot_general(dyb, xsb, (((0,), (0,)), ((), ())),
                                            preferred_element_type=F32), mask)
        gcr_ref[...] = gc[:, :nst]
        gci_ref[...] = -gc[:, nst:]
        _planes_put(g, jnp.dot(dyb, cm, preferred_element_type=F32))
        ar, ai = _scan_bwd(g, xs, k_ref, nst)
        ga_ref[0, 0:1, :] = ar
        ga_ref[0, 1:2, :] = ai
        gb = _planes_get(g).astype(BF16)
        du = lax.dot_general(gb, bm, (((1,), (1,)), ((), ())), preferred_element_type=F32) + d * dy
        du_ref[...] = du.astype(BF16)
        gbb = _gather_groups(lax.dot_general(ub, gb, (((0,), (0,)), ((), ())),
                                             preferred_element_type=F32), mask)
        dr, di = gbb[:, :nst], gbb[:, nst:]
        fr, fi = f_ref[0], f_ref[1]
        br, bi = br_ref[...], bi_ref[...]
        gbr_ref[...] = fr * dr + fi * di
        gbi_ref[...] = fr * di - fi * dr
        gf_ref[0] = _colsum(dr * br + di * bi)
        gf_ref[1] = _colsum(di * br - dr * bi)

    hp = pl.BlockSpec((SSM_GROUP, nst), lambda b: (0, b))
    hp_sds = _sds((SSM_GROUP, nb * nst), F32)
    return pl.pallas_call(
        _with_after(body, 11, after), name="s5_bwd", grid=(nb,),
        in_specs=[pl.BlockSpec((S, cw), lambda b: (0, b)),
                  pl.BlockSpec((S, 2 * nst), lambda b: (0, b)),
                  pl.BlockSpec((S, cw), lambda b: (0, b)),
                  pl.BlockSpec((S, cw), lambda b: (0, b))] + _s5_param_specs(cw, nst)
        + [ANY] * len(after),
        out_specs=[pl.BlockSpec((S, cw), lambda b: (0, b)), hp, hp, hp, hp,
                   pl.BlockSpec((2, 1, nst), lambda b: (0, 0, b)),
                   pl.BlockSpec((1, cw), lambda b: (0, b)),
                   pl.BlockSpec((1, 2, nst), lambda b: (b, 0, 0))],
        out_shape=[_sds((S, nb * cw), BF16), hp_sds, hp_sds, hp_sds, hp_sds,
                   _sds((2, 1, nb * nst), F32), _sds((1, nb * cw), F32), _sds((nb, 2, nst), F32)],
        scratch_shapes=[pltpu.VMEM((2 * nst // LANES, S, LANES), F32)] * 2,
        compiler_params=_params(("arbitrary",)),
    )(proj, xsb_all, dz, zp, *params, *after)


def _shift_rows(v, k, row, down):
    n = v.shape[0]
    if down:
        return jnp.where(row >= k, pltpu.roll(v, k, 0), 0.0)
    return jnp.where(row < n - k, pltpu.roll(v, n - k, 0), 0.0)


def _window(v, gi, row, down):
    sums = []
    s = v
    for k in (1, 2, 4, 8):
        s = s + _shift_rows(s, k, row, down)
        sums.append(s)
    out = sums[3]
    for n in (2, 1, 0):
        out = jnp.where(gi == n, sums[n], out)
    return out


def pool_fwd(proj, col0, width, gw):
    S = proj.shape[0]
    cb0 = col0 // gw

    def body(u_ref, o_ref):
        gi = pl.program_id(0)
        u = u_ref[...]
        row = lax.broadcasted_iota(jnp.int32, u.shape, 0)
        w = jnp.left_shift(2, gi)
        count = jnp.minimum(row + 1, w).astype(F32)
        o_ref[...] = (_window(u, gi, row, True) / count - u).astype(BF16)

    return pl.pallas_call(
        body, name="pool_fwd", grid=(len(POOL_WINDOWS),),
        in_specs=[pl.BlockSpec((S, gw), lambda g: (0, cb0 + g))],
        out_specs=pl.BlockSpec((S, gw), lambda g: (0, g)),
        out_shape=_sds((S, width), BF16), compiler_params=_params(("arbitrary",)),
    )(proj)


def pool_bwd(dpooled, gw):
    S, width = dpooled.shape

    def body(d_ref, o_ref):
        gi = pl.program_id(0)
        d = d_ref[...]
        row = lax.broadcasted_iota(jnp.int32, d.shape, 0)
        w = jnp.left_shift(2, gi)
        count = jnp.minimum(row + 1, w).astype(F32)
        o_ref[...] = (_window(d / count, gi, row, False) - d).astype(BF16)

    return pl.pallas_call(
        body, name="pool_bwd", grid=(len(POOL_WINDOWS),),
        in_specs=[pl.BlockSpec((S, gw), lambda g: (0, g))],
        out_specs=pl.BlockSpec((S, gw), lambda g: (0, g)),
        out_shape=_sds((S, width), BF16), compiler_params=_params(("arbitrary",)),
    )(dpooled)


def _place():
    x, y, c = lax.axis_index("x"), lax.axis_index("y"), lax.axis_index("c")
    chips = [(1 - x, y), (x, 1 - y), (1 - x, 1 - y)]
    return x, y, c, chips


HBM = pl.BlockSpec(memory_space=pltpu.HBM)


def _routed_gather_body(n):
    def body(*refs):
        ins, outs = refs[:n], refs[n:2 * n]
        send_sems, recv_sems, local_sems = refs[2 * n:]
        x, y, c, (xn, yn, dg) = _place()
        me, sibling = (x, y, c), (x, y, 1 - c)
        barrier = pltpu.get_barrier_semaphore()
        for peer in (sibling, (*xn, c), (*yn, c)):
            pl.semaphore_signal(barrier, inc=1, device_id=peer, device_id_type=MESH)
        pl.semaphore_wait(barrier, 3)

        def piece(i, p, h):
            rows = ins[i].shape[0] // 2
            return outs[i].at[4 * p[0] + 2 * p[1] + p[2], pl.ds(h * rows, rows)]

        def copy(i, k, src, dst, to):
            return pltpu.make_async_remote_copy(src_ref=src, dst_ref=dst, send_sem=send_sems.at[i, k],
                                                recv_sem=recv_sems.at[i, k], device_id=to,
                                                device_id_type=MESH)

        started = []

        def go(cp):
            cp.start()
            started.append(cp)

        for i in range(n):
            rows = ins[i].shape[0] // 2
            for h in range(2):
                own = ins[i].at[pl.ds(h * rows, rows)]
                go(copy(i, 1 + h, own, piece(i, me, h), (*xn, c)))
                go(copy(i, 3 + h, own, piece(i, me, h), (*yn, c)))
        for i in range(n):
            go(copy(i, 0, ins[i], outs[i].at[4 * x + 2 * y + c], sibling))
        mine = [pltpu.make_async_copy(ins[i], outs[i].at[4 * x + 2 * y + c], local_sems.at[i])
                for i in range(n)]
        for cp in mine:
            cp.start()
        for i in range(n):
            for k, chip, h, onward, ksib in ((1, xn, 0, (5, yn), 7), (4, yn, 1, (6, xn), 10),
                                            (2, xn, 1, None, 8), (3, yn, 0, None, 9),
                                            (5, dg, 0, None, 11), (6, dg, 1, None, 12)):
                got = piece(i, (*chip, c), h)
                copy(i, k, got, got, me).wait_recv()
                if onward is not None:
                    go(copy(i, onward[0], got, got, (*onward[1], c)))
                go(copy(i, ksib, got, got, sibling))
        for i in range(n):
            block = outs[i].at[4 * x + 2 * y + 1 - c]
            copy(i, 0, block, block, me).wait_recv()
            for ksib, chip, h in ((7, xn, 0), (10, yn, 1), (8, xn, 1), (9, yn, 0), (11, dg, 0), (12, dg, 1)):
                got = piece(i, (*chip, 1 - c), h)
                copy(i, ksib, got, got, me).wait_recv()
        for cp in started:
            cp.wait_send()
        for cp in mine:
            cp.wait()

    return body


def _on_sequencer(name, body, arrays, out_sds, sems, collective_id):
    ins = [jax.new_ref(a, memory_space=pltpu.MemorySpace.HBM) for a in arrays]
    outs = [jax.empty_ref(s, memory_space=pltpu.MemorySpace.HBM) for s in out_sds]

    @pl.kernel(mesh=plsc.ScalarSubcoreMesh(axis_name="sequencer", num_cores=1), name=name,
               scratch_types=tuple(sems),
               compiler_params=pltpu.CompilerParams(collective_id=collective_id))
    def launch(*sem_refs):
        body(*ins, *outs, *sem_refs)

    launch()
    return [o[...] for o in outs]


def seq_all_gather(name, shards, collective_id):
    n = len(shards)
    return _on_sequencer(
        name, _routed_gather_body(n), shards, [_sds((NDEV,) + s.shape, s.dtype) for s in shards],
        [pltpu.SemaphoreType.DMA((n, 13)), pltpu.SemaphoreType.DMA((n, 13)),
         pltpu.SemaphoreType.DMA((n,))], collective_id)


def pair_exchange(name, grads, collective_id):
    def plan(srcs, lands):
        x, y, c, _ = _place()
        return ([(i, q, srcs[i].at[2 * q + 1 - c], lands[i].at[q], (x, y, 1 - c))
                 for i in range(len(srcs)) for q in range(NCHIP)], [(x, y, 1 - c)])

    return _split_exchange(name, grads, [_sds((NCHIP,) + g.shape[1:], g.dtype) for g in grads],
                           plan, NCHIP, collective_id)


SEM = pl.BlockSpec(memory_space=pltpu.SEMAPHORE)


def _split_exchange(name, srcs, land_sds, plan, ncopy, collective_id):
    n = len(srcs)
    nsem = n * ncopy
    effect = pltpu.SideEffectType.DATAFLOW_SIDE_EFFECTING

    def descriptors(src_refs, land_refs, send_sems, recv_sems):
        copies, peers = plan(src_refs, land_refs)
        return [pltpu.make_async_remote_copy(src_ref=s, dst_ref=d, send_sem=send_sems[i * ncopy + k],
                                             recv_sem=recv_sems[i * ncopy + k], device_id=to,
                                             device_id_type=MESH) for (i, k, s, d, to) in copies], peers

    def start_body(*refs):
        src_refs, land_refs = refs[:n], refs[n:2 * n]
        send_sems, recv_sems = refs[2 * n:2 * n + nsem], refs[2 * n + nsem:2 * n + 2 * nsem]
        token = refs[-1]
        cps, peers = descriptors(src_refs, land_refs, send_sems, recv_sems)
        barrier = pltpu.get_barrier_semaphore()
        for peer in peers:
            pl.semaphore_signal(barrier, inc=1, device_id=peer, device_id_type=MESH)
        pl.semaphore_wait(barrier, len(peers))
        for cp in cps:
            cp.start()
        token[...] = jnp.zeros_like(token)

    lands = [pltpu.with_memory_space_constraint(lax.empty(s.shape, s.dtype), pltpu.HBM) for s in land_sds]
    srcs = [pltpu.with_memory_space_constraint(s, pltpu.HBM) for s in srcs]
    res = pl.pallas_call(
        start_body, name=name + "_start",
        out_shape=(pltpu.SemaphoreType.DMA(()),) * (2 * nsem)
        + tuple(pltpu.HBM(s.shape, s.dtype) for s in srcs)
        + tuple(pltpu.HBM(s.shape, s.dtype) for s in land_sds) + (_sds((SUBLANES, LANES), F32),),
        in_specs=[HBM] * (2 * n),
        out_specs=(SEM,) * (2 * nsem) + (HBM,) * (2 * n) + (pl.BlockSpec(memory_space=pltpu.VMEM),),
        input_output_aliases={i: 2 * nsem + i for i in range(2 * n)},
        compiler_params=pltpu.CompilerParams(has_side_effects=effect, collective_id=collective_id),
    )(*srcs, *lands)
    sems = res[:2 * nsem]
    thru = res[2 * nsem:2 * nsem + 2 * n]
    token = res[-1]

    def wait(after):
        def wait_body(*refs):
            src_refs, land_refs = refs[:n], refs[n:2 * n]
            cps, _ = descriptors(src_refs, land_refs, refs[2 * n:2 * n + nsem],
                                 refs[2 * n + nsem:2 * n + 2 * nsem])
            for cp in cps:
                cp.wait_send()
            for cp in cps:
                cp.wait_recv()

        out = pl.pallas_call(
            wait_body, name=name + "_wait",
            out_shape=tuple(pltpu.HBM(s.shape, s.dtype) for s in srcs)
            + tuple(pltpu.HBM(s.shape, s.dtype) for s in land_sds),
            in_specs=[HBM] * (2 * n) + [SEM] * (2 * nsem) + [pl.BlockSpec(memory_space=pl.ANY)],
            out_specs=(HBM,) * (2 * n),
            input_output_aliases={i: i for i in range(2 * n)},
            compiler_params=pltpu.CompilerParams(has_side_effects=effect),
        )(*thru, *sems, after)
        return list(out[:n]), list(out[n:])

    return token, wait


def pair_sum(name, grad, got, place):
    shp = grad.shape[1:]
    r, cdim = shp[-2], shp[-1]
    lead = int(math.prod(shp[:-2])) if len(shp) > 2 else 1
    g5 = grad.reshape(NCHIP, 2, lead * r, cdim)
    t4 = got.reshape(NCHIP, lead * r, cdim)
    R = lead * r
    tr = _tile(R, max(8, (1 << 20) // cdim))

    def body(p_ref, g_ref, t_ref, o_ref):
        o_ref[...] = (g_ref[0].astype(F32) + t_ref[...].astype(F32)).astype(o_ref.dtype)

    out = pl.pallas_call(
        body, name=name,
        grid_spec=pltpu.PrefetchScalarGridSpec(
            num_scalar_prefetch=1, grid=(NCHIP - 1, R // tr),
            in_specs=[pl.BlockSpec((1, 1, tr, cdim), lambda j, i, p: (p[1] ^ (j + 1), p[0], i, 0)),
                      pl.BlockSpec((1, tr, cdim), lambda j, i, p: (p[1] ^ (j + 1), i, 0))],
            out_specs=pl.BlockSpec((1, tr, cdim), lambda j, i, p: (p[1] ^ (j + 1), i, 0))),
        out_shape=_sds((NCHIP, R, cdim), grad.dtype),
        compiler_params=_params(("parallel", "parallel")),
    )(place, g5, t4)
    return out


def chip_exchange(name, parts, collective_id):
    def plan(srcs, lands):
        x, y, c, chips = _place()
        return ([(i, j, srcs[i].at[2 * chip[0] + chip[1]], lands[i].at[j], (*chip, c))
                 for i in range(len(srcs)) for j, chip in enumerate(chips)],
                [(*chip, c) for chip in chips])

    return _split_exchange(name, parts, [_sds((3,) + p.shape[1:], p.dtype) for p in parts],
                           plan, 3, collective_id)


def ada_fwd(c_row, w_ada, b_ada):
    D, cols = w_ada.shape

    def body(c_ref, w_ref, b_ref, mod_ref, call_ref, act8, part, s1, r1, s2, r2):
        x, y, c, _ = _place()
        me = 4 * x + 2 * y + c
        call_ref[me] = c_ref[...]
        cps = []
        for k in range(1, NDEV):
            to = (x ^ (k >> 2), y ^ ((k >> 1) & 1), c ^ (k & 1))
            cps.append(pltpu.make_async_remote_copy(
                src_ref=c_ref, dst_ref=call_ref.at[me], send_sem=s1.at[k - 1],
                recv_sem=r1.at[k - 1], device_id=to, device_id_type=MESH))
            cps[-1].start()
        for cp in cps:
            cp.wait()
        for b in range(NDEV):
            act8[b:b + 1, :] = call_ref[b]
        cv = act8[...]
        act = (cv * _sigmoid(cv)).astype(BF16)
        res = jnp.dot(act, w_ref[...].astype(BF16), preferred_element_type=F32)
        for b in range(NDEV):
            part[b] = res[b:b + 1, :]
        mod_ref[me] = part[me]
        cps = []
        for k in range(1, NDEV):
            to = (x ^ (k >> 2), y ^ ((k >> 1) & 1), c ^ (k & 1))
            dst = 4 * to[0] + 2 * to[1] + to[2]
            cps.append(pltpu.make_async_remote_copy(
                src_ref=part.at[dst], dst_ref=mod_ref.at[me], send_sem=s2.at[k - 1],
                recv_sem=r2.at[k - 1], device_id=to, device_id_type=MESH))
            cps[-1].start()
        for cp in cps:
            cp.wait()
        for b in range(NDEV):
            mod_ref[b] = mod_ref[b] + b_ref[b]

    vm = pl.BlockSpec(memory_space=pltpu.VMEM)
    return pl.pallas_call(
        body, name="ada_fwd", in_specs=[vm, vm, vm], out_specs=[vm, vm],
        out_shape=[_sds((NDEV, 1, cols), F32), _sds((NDEV, 1, D), F32)],
        scratch_shapes=[pltpu.VMEM((NDEV, D), F32), pltpu.VMEM((NDEV, 1, cols), F32),
                        pltpu.SemaphoreType.DMA((NDEV - 1,)), pltpu.SemaphoreType.DMA((NDEV - 1,)),
                        pltpu.SemaphoreType.DMA((NDEV - 1,)), pltpu.SemaphoreType.DMA((NDEV - 1,))],
        compiler_params=pltpu.CompilerParams(vmem_limit_bytes=VMEM_LIMIT),
    )(c_row, w_ada, b_ada.reshape(NDEV, 1, cols))


def _adamw_math(g, w, m, v):
    m2 = ADAM_B1 * m + (1.0 - ADAM_B1) * g
    v2 = ADAM_B2 * v + (1.0 - ADAM_B2) * (g * g)
    m_hat = m2 / (1.0 - ADAM_B1 ** ADAM_STEP)
    v_hat = v2 / (1.0 - ADAM_B2 ** ADAM_STEP)
    delta = -ADAM_LR * (m_hat / (jnp.sqrt(v_hat) + ADAM_EPS) + ADAM_WD * w)
    return delta, m2, v2


def adamw_sharded(name, grad8, pair4, got3, w, m, v, place, after=()):
    shape = w.shape
    cdim = shape[-1]
    R = int(math.prod(shape[:-1]))
    w2, m2, v2 = (t.reshape(R, cdim) for t in (w, m, v))
    tr = _tile(R, max(8, (1 << 19) // cdim))

    def body(q_ref, own_ref, sib_ref, t_ref, w_ref, m_ref, v_ref, g_out, d_out, m_out, v_out):
        g = own_ref[0].astype(F32) + sib_ref[0].astype(F32)
        for j in range(3):
            g = g + t_ref[j].astype(F32)
        d, mn, vn = _adamw_math(g, w_ref[...], m_ref[...], v_ref[...])
        g_out[...] = g
        d_out[...] = d
        m_out[...] = mn
        v_out[...] = vn

    spec = pl.BlockSpec((tr, cdim), lambda i, qr: (i, 0))
    outs = pl.pallas_call(
        _with_after(body, 7, after), name=name,
        grid_spec=pltpu.PrefetchScalarGridSpec(
            num_scalar_prefetch=1, grid=(R // tr,),
            in_specs=[pl.BlockSpec((1, tr, cdim), lambda i, qr: (qr[2], i, 0)),
                      pl.BlockSpec((1, tr, cdim), lambda i, qr: (qr[1], i, 0)),
                      pl.BlockSpec((3, tr, cdim), lambda i, qr: (0, i, 0)), spec, spec, spec]
            + [ANY] * len(after),
            out_specs=[spec] * 4),
        out_shape=[_sds((R, cdim), F32)] * 4,
        compiler_params=_params(("parallel",)),
    )(place, grad8.reshape(NDEV, R, cdim), pair4.reshape(NCHIP, R, cdim),
      got3.reshape(3, R, cdim), w2, m2, v2, *after)
    return [o.reshape(shape) for o in outs]


def sum_small(parts, after=()):
    R = parts.shape[1]

    def body(p_ref, g_out):
        g = p_ref[0]
        for j in range(1, NDEV):
            g = g + p_ref[j]
        g_out[...] = g

    return pl.pallas_call(
        _with_after(body, 1, after), name="sum_small", grid=(1,),
        in_specs=[pl.BlockSpec((NDEV, R, LANES), lambda i: (0, 0, 0))] + [ANY] * len(after),
        out_specs=pl.BlockSpec((R, LANES), lambda i: (0, 0)), out_shape=_sds((R, LANES), F32),
        compiler_params=_params(("arbitrary",)),
    )(parts, *after)


def adamw_natural(gs, ws, ms, vs):
    n = len(ws)
    nblk = 8
    big = [w.ndim == 4 and w.shape[1] % nblk == 0 for w in ws]

    def spec(w, is_big):
        if is_big:
            return pl.BlockSpec((1, w.shape[1] // nblk) + w.shape[2:], lambda i: (0, i, 0, 0))
        return pl.BlockSpec(w.shape, functools.partial(lambda i, nd: (0,) * nd, nd=w.ndim))

    def body(*refs):
        g_refs, w_refs, m_refs, v_refs = (refs[k * n:(k + 1) * n] for k in range(4))
        d_outs, m_outs, v_outs = (refs[(4 + k) * n:(5 + k) * n] for k in range(3))

        def update(p):
            d, mn, vn = _adamw_math(g_refs[p][...], w_refs[p][...], m_refs[p][...], v_refs[p][...])
            d_outs[p][...] = d
            m_outs[p][...] = mn
            v_outs[p][...] = vn

        for p in range(n):
            if big[p]:
                update(p)

        @pl.when(pl.program_id(0) == 0)
        def _():
            for p in range(n):
                if not big[p]:
                    update(p)

    specs = [spec(w, b) for w, b in zip(ws, big)]
    outs = pl.pallas_call(
        body, name="adamw_natural", grid=(nblk,), in_specs=specs * 4, out_specs=specs * 3,
        out_shape=[_sds(w.shape, F32) for w in ws] * 3,
        compiler_params=_params(("arbitrary",)),
    )(*gs, *ws, *ms, *vs)
    return outs[:n], outs[n:2 * n], outs[2 * n:]


def adamw_ada(c_all_t, dmod_all, w, m, v, my_dev):
    D, cols = w.shape
    tr = _tile(D, 256)

    def body(k_ref, c_ref, d_ref, w_ref, m_ref, v_ref, g_out, d_out, m_out, v_out):
        cv = c_ref[...]
        act = cv * _sigmoid(cv)
        dm = d_ref[...]
        g = act[:, 0:1] * dm[0:1, :]
        for b in range(1, NDEV):
            g = g + act[:, b:b + 1] * dm[b:b + 1, :]
        d, mn, vn = _adamw_math(g, w_ref[...], m_ref[...], v_ref[...])
        g_out[...] = g
        d_out[...] = d
        m_out[...] = mn
        v_out[...] = vn

    spec = pl.BlockSpec((tr, cols), lambda i, kr: (i, 0))
    return pl.pallas_call(
        body, name="adamw_ada",
        grid_spec=pltpu.PrefetchScalarGridSpec(
            num_scalar_prefetch=1, grid=(D // tr,),
            in_specs=[pl.BlockSpec((tr, NDEV), lambda i, kr: (i, 0)),
                      pl.BlockSpec((NDEV, cols), lambda i, kr: (0, kr[0])), spec, spec, spec],
            out_specs=[spec] * 4),
        out_shape=[_sds((D, cols), F32)] * 4,
        compiler_params=_params(("parallel",)),
    )(my_dev, c_all_t, dmod_all, w, m, v)


def _small_pack(parts):
    rows = []
    for p in parts:
        flat = p.reshape(-1)
        flat = jnp.pad(flat, (0, (-flat.shape[0]) % (SUBLANES * LANES)))
        rows.append(flat.reshape(-1, LANES))
    return jnp.concatenate(rows, axis=0)


def _small_unpack(buf, shapes):
    out, r = [], 0
    for s in shapes:
        n = int(math.prod(s))
        nr = -(-n // (SUBLANES * LANES)) * SUBLANES
        out.append(buf[r:r + nr].reshape(-1)[:n].reshape(s))
        r += nr
    return out


def kernel(x, c, w_ada, b_ada, w_in, lam_re, lam_im, log_dt, ssm_b_re, ssm_b_im, ssm_c_re, ssm_c_im, ssm_d, w_glu_val, w_glu_gate, w_pool, pool_scale, w_pool_out, w_out, ln1_g, ln1_b, w_ff1, w_ff2, ln2_g, ln2_b, loss_target, m_w_ada, m_b_ada, m_w_in, m_lam_re, m_lam_im, m_log_dt, m_ssm_b_re, m_ssm_b_im, m_ssm_c_re, m_ssm_c_im, m_ssm_d, m_w_glu_val, m_w_glu_gate, m_w_pool, m_pool_scale, m_w_pool_out, m_w_out, m_ln1_g, m_ln1_b, m_w_ff1, m_w_ff2, m_ln2_g, m_ln2_b, v_w_ada, v_b_ada, v_w_in, v_lam_re, v_lam_im, v_log_dt, v_ssm_b_re, v_ssm_b_im, v_ssm_c_re, v_ssm_c_im, v_ssm_d, v_w_glu_val, v_w_glu_gate, v_w_pool, v_pool_scale, v_w_pool_out, v_w_out, v_ln1_g, v_ln1_b, v_w_ff1, v_w_ff2, v_ln2_g, v_ln2_b):
    S, D = x.shape[1], x.shape[2]
    x2d, tgt = x[0], loss_target[0]
    W = D // 2
    G = W // SSM_GROUP
    P, H, GPB = SSM_STATE, SSM_GROUP, GROUPS_PER_BLOCK
    nblk = G // GPB
    gw = W // len(POOL_WINDOWS)
    ax, ay, ac = lax.axis_index("x"), lax.axis_index("y"), lax.axis_index("c")
    my_dev = (4 * ax + 2 * ay + ac).astype(jnp.int32).reshape(1)
    place = jnp.stack([ac, 2 * ax + ay, 4 * ax + 2 * ay + ac]).astype(jnp.int32)
    ts = _tile(S, 256)

    glu = jnp.stack([w_glu_val[0], w_glu_gate[0]]).astype(BF16)
    shards = [w_in[0].astype(BF16), glu, w_pool[0].astype(BF16), w_pool_out[0].astype(BF16),
              w_out[0].astype(BF16), w_ff1[0].astype(BF16), w_ff2[0].astype(BF16)]
    wg_in, wg_pool = seq_all_gather("gather_w_in", [shards[0], shards[2]], 1)
    wg_vg, wg_po, wg_out = seq_all_gather("gather_w_mix", [shards[1], shards[3], shards[4]], 2)
    (wg_ff1,) = seq_all_gather("gather_w_ff1", shards[5:6], 3)
    (wg_ff2,) = seq_all_gather("gather_w_ff2", shards[6:7], 11)
    wg_vg = wg_vg.reshape(2 * NDEV, W, D // NDEV)
    nwin = len(POOL_WINDOWS)
    wp_full = jnp.transpose(wg_pool, (1, 0, 2, 3)).reshape(nwin, gw, gw)
    wout_full = wg_out.reshape(1, D, D)
    wff2_full = wg_ff2.reshape(1, 4 * D, D)

    small_names = [b_ada, lam_re, lam_im, log_dt, ssm_b_re, ssm_b_im, ssm_c_re, ssm_c_im, ssm_d,
                   pool_scale, ln1_g, ln1_b, ln2_g, ln2_b]
    small_m = [m_b_ada, m_lam_re, m_lam_im, m_log_dt, m_ssm_b_re, m_ssm_b_im, m_ssm_c_re, m_ssm_c_im,
               m_ssm_d, m_pool_scale, m_ln1_g, m_ln1_b, m_ln2_g, m_ln2_b]
    small_v = [v_b_ada, v_lam_re, v_lam_im, v_log_dt, v_ssm_b_re, v_ssm_b_im, v_ssm_c_re, v_ssm_c_im,
               v_ssm_d, v_pool_scale, v_ln1_g, v_ln1_b, v_ln2_g, v_ln2_b]

    mod, c_all = ada_fwd(c, w_ada[0], b_ada)
    mod = mod.reshape(6, 1, D)
    sh1, sc1, g1, sh2, sc2, g2 = (mod[i] for i in range(6))

    f2, kconst = s5_disc(lam_re[0], lam_im[0], log_dt[0].reshape(G, 1))
    kconst = kconst.reshape(NCONST, SUBLANES, G * P)
    f2r = f2.reshape(2, 1, G * P)
    bt_re = jnp.transpose(ssm_b_re[0], (2, 0, 1)).reshape(H, G * P)
    bt_im = jnp.transpose(ssm_b_im[0], (2, 0, 1)).reshape(H, G * P)
    ct_re = jnp.transpose(ssm_c_re[0], (1, 0, 2)).reshape(H, G * P)
    ct_im = jnp.transpose(ssm_c_im[0], (1, 0, 2)).reshape(H, G * P)
    s5_params = (f2r, bt_re, bt_im, ct_re, ct_im, ssm_d, kconst)

    def e1(t, b):
        xhat, _ = _ln_stats(t[0])
        return [xhat * (1.0 + b[0]) + b[1]], []
    (h1,) = _rowwise("ln_mod1", e1, S, ts, [(x2d, D, 0)], [sc1, sh1], [(D, BF16)], [])

    (proj,) = mm_nn("proj", h1, wg_in, F32, 2)
    z, xsb_all, zp = s5_fwd(proj, s5_params, nblk)
    (vt,) = mm_nn("glu", z, wg_vg, BF16, 4)
    pooled = pool_fwd(proj, W, W, gw)

    def pool_epi(vals, ex, outs):
        a = vals[0]
        outs[0][...] = a
        outs[1][...] = (a * ex[0][...]).astype(BF16)
    tmp = _tile(S, 1024)
    yp, ypool = _mm(
        "pool_mix", "nn", pooled, wp_full.astype(BF16), (S // tmp, nwin, 1),
        pl.BlockSpec((tmp, gw), lambda i, j, k: (i, j)), pl.BlockSpec((1, gw, gw), lambda i, j, k: (j, 0, 0)),
        [(_sds((S, W), F32), pl.BlockSpec((tmp, gw), lambda i, j, k: (i, j))),
         (_sds((S, W), BF16), pl.BlockSpec((tmp, gw), lambda i, j, k: (i, j)))],
        (tmp, gw), 1, gw, None, pool_epi,
        [(pool_scale, pl.BlockSpec((1, gw), lambda i, j, k: (0, j)))])
    (y_b,) = mm_nn("pool_out", ypool, wg_po, BF16, 4)

    cb = D // NDEV
    ga_cb, gb_cb = (2 * W) // cb, (2 * W + D) // cb
    mcb = NDEV
    wm = mcb * cb
    tsm = _tile(S, 128)

    def merge_call(name, fn, ins, n_out, after=()):
        def body(*refs):
            vals = [r[...].astype(F32) for r in refs[:len(ins)]]
            for r, v in zip(refs[len(ins):], fn(*vals)):
                r[...] = v.astype(r.dtype)
        return pl.pallas_call(
            _with_after(body, len(ins), after), name=name, grid=(S // tsm, NDEV // mcb),
            in_specs=[pl.BlockSpec((tsm, w), f) for (_, w, f) in ins] + [ANY] * len(after),
            out_specs=[pl.BlockSpec((tsm, w), lambda i, j: (i, j)) for (_, w) in n_out],
            out_shape=[_sds((S, cols), BF16) for (cols, _) in n_out],
            compiler_params=_params(("parallel", "parallel")),
        )(*[a for (a, _, _) in ins], *after)

    merge_ins = [(proj, wm, lambda i, j: (i, ga_cb // mcb + j)), (proj, wm, lambda i, j: (i, gb_cb // mcb + j)),
                 (vt, 2 * wm, lambda i, j: (i, j)), (y_b, wm, lambda i, j: (i, j))]

    def val_gate(vtv):
        return (jnp.concatenate([vtv[:, 2 * q * cb:(2 * q + 1) * cb] for q in range(mcb)], axis=1),
                jnp.concatenate([vtv[:, (2 * q + 1) * cb:(2 * q + 2) * cb] for q in range(mcb)], axis=1))

    def merge_f(ga, gb, vtv, yb):
        vv, tt = val_gate(vtv)
        return [_sigmoid(ga) * (vv * _sigmoid(tt)) + _sigmoid(gb) * yb]
    (merged,) = merge_call("merge", merge_f, merge_ins, [(D, wm)])

    (mix,) = mm_nn("mix_out", merged, wout_full, F32, 1)

    def e3(t, b):
        xv, mx = t
        g1v, l1g, l1b, sc2v, sh2v = b
        r1 = ALPHA * xv + g1v * mx
        xh1, _ = _ln_stats(r1)
        x1 = xh1 * l1g + l1b
        xh, _ = _ln_stats(x1)
        return [r1, xh * (1.0 + sc2v) + sh2v], []
    r1, h2 = _rowwise("post_mix", e3, S, ts, [(x2d, D, 0), (mix, D, 0)],
                      [g1, ln1_g, ln1_b, sc2, sh2], [(D, F32), (D, BF16)], [])

    def relu_epi(vals, ex, outs):
        outs[0][...] = jnp.maximum(vals[0], 0.0).astype(BF16)
    (rl,) = mm_nn("ff1", h2, wg_ff1, BF16, 1, epi=relu_epi)

    def square(a):
        return a * a
    (y2,) = mm_nn("ff2", rl, wff2_full, F32, 1, pro=square)

    def e4(t, b):
        r1v, y2v, tg = t
        g2v, l1g, l1b, l2g, l2b = b
        xh1, _ = _ln_stats(r1v)
        x1 = xh1 * l1g + l1b
        r2 = ALPHA * x1 + g2v * y2v
        xh2, rs2 = _ln_stats(r2)
        err = xh2 * l2g + l2b - tg
        dx2 = err * (1.0 / D)
        dr2 = _ln_bwd(dx2 * l2g, xh2, rs2)
        lsum = jnp.sum(_colsum(err * err), axis=1, keepdims=True) * (0.5 / D)
        return ([ALPHA * dr2, g2v * dr2],
                [jnp.broadcast_to(lsum, (1, LANES)), _colsum(dx2 * xh2), _colsum(dx2), _colsum(dr2 * y2v)])
    dx1a, dy2, loss_acc, g_ln2g, g_ln2b, d_g2 = _rowwise(
        "head", e4, S, ts, [(r1, D, 0), (y2, D, 0), (tgt, D, 0)], [g2, ln1_g, ln1_b, ln2_g, ln2_b],
        [(D, F32), (D, BF16)], [LANES, D, D, D])

    tn_ff = _tile(4 * D, 1024)

    def dff_epi(vals, ex, outs):
        outs[0][...] = (vals[0] * (2.0 * ex[0][...].astype(F32))).astype(BF16)
    tmf = _tile(S, 1024)
    (da1,) = mm_nt("d_ff2", dy2, wff2_full, BF16, 1, tn=tn_ff, epi=dff_epi,
                   extras=[(rl, pl.BlockSpec((tmf, tn_ff), lambda i, j, k: (i, j)))])
    gw_ff2 = mm_tn("gw_ff2", rl, dy2, BF16, NDEV, 0, pro=square)
    gw_ff1 = mm_tn("gw_ff1", h2, da1, BF16, NDEV, 1)
    tok, wait_pair_a = pair_exchange("pair_exchange_ff", [gw_ff2, gw_ff1], 4)
    (dh2,) = mm_nt("d_ff1", da1, wg_ff1, F32, 4, after=[tok])

    def e5(t, b):
        dh2v, r1v, dx1av, mx = t
        sc2v, l1g, l1b, g1v = b
        xh1, rs1 = _ln_stats(r1v)
        x1 = xh1 * l1g + l1b
        xh, rs = _ln_stats(x1)
        dx1 = dx1av + _ln_bwd(dh2v * (1.0 + sc2v), xh, rs)
        dr1 = _ln_bwd(dx1 * l1g, xh1, rs1)
        return ([ALPHA * dr1, g1v * dr1],
                [_colsum(dh2v * xh), _colsum(dh2v), _colsum(dx1 * xh1), _colsum(dx1), _colsum(dr1 * mx)])
    dxa, dmix, d_sc2, d_sh2, g_ln1g, g_ln1b, d_g1 = _rowwise(
        "post_mix_bwd", e5, S, ts, [(dh2, D, 0), (r1, D, 0), (dx1a, D, 0), (mix, D, 0)],
        [sc2, ln1_g, ln1_b, g1], [(D, F32), (D, BF16)], [D, D, D, D, D])

    (dmerged,) = mm_nt("d_mix_out", dmix, wout_full, BF16, 1)
    gw_out = mm_tn("gw_out", merged, dmix, BF16, NDEV, 0)
    grads_a, got_a = wait_pair_a(gw_out)
    parts_a = [pair_sum("pair_sum_ff%d" % i, g, t, place) for i, (g, t) in enumerate(zip(grads_a, got_a))]
    tok, wait_chip_a = chip_exchange("chip_exchange_ff", parts_a, 5)

    def merge_b(ga, gb, vtv, yb, dm):
        vv, tt = val_gate(vtv)
        sa, sb, st = _sigmoid(ga), _sigmoid(gb), _sigmoid(tt)
        dya = dm * sa
        dv, dt = dya * st, dya * vv * st * (1.0 - st)
        dvt_tile = jnp.concatenate([t[:, q * cb:(q + 1) * cb] for q in range(mcb) for t in (dv, dt)], axis=1)
        return [dm * (vv * st) * sa * (1.0 - sa), dm * yb * sb * (1.0 - sb), dvt_tile, dm * sb]
    dga, dgb_, dvt, dy_b = merge_call(
        "merge_bwd", merge_b, merge_ins + [(dmerged, wm, lambda i, j: (i, j))],
        [(D, wm), (D, wm), (2 * D, 2 * wm), (D, wm)], after=[tok])

    (dypool,) = mm_nt("d_pool_out", dy_b, wg_po, F32, NDEV)
    gw_po = mm_tn("gw_pool_out", ypool, dy_b, BF16, NDEV, 4)

    def e7(t, b):
        return [t[0] * b[0]], [_colsum(t[0] * t[1])]
    dyp, g_pscale = _rowwise("pool_scale_bwd", e7, S, ts, [(dypool, W, 0), (yp, W, 0)],
                             [pool_scale], [(W, BF16)], [W])
    (dpooled,) = _mm(
        "d_pool_mix", "nt", dyp, wp_full.astype(BF16), (S // tmp, nwin, 1),
        pl.BlockSpec((tmp, gw), lambda i, j, k: (i, j)), pl.BlockSpec((1, gw, gw), lambda i, j, k: (j, 0, 0)),
        [(_sds((S, W), F32), pl.BlockSpec((tmp, gw), lambda i, j, k: (i, j)))], (tmp, gw), 1, gw)
    tkp = _tile(S, 2048)
    gw_pool = _mm(
        "gw_pool", "tn", pooled, dyp, (nwin, 1, S // tkp),
        pl.BlockSpec((tkp, gw), lambda i, j, k: (k, i)), pl.BlockSpec((tkp, gw), lambda i, j, k: (k, i)),
        [(_sds((nwin, gw, gw), BF16), pl.BlockSpec((1, gw, gw), lambda i, j, k: (i, 0, 0)))],
        (gw, gw), 1, gw, stacked_out=True)[0]
    du_pool = pool_bwd(dpooled, gw)

    (dz,) = mm_nt("d_glu", dvt, wg_vg, BF16, 2 * NDEV)
    gw_vg = mm_tn("gw_glu", z, dvt, BF16, 2 * NDEV, 4)
    gw_pool_st = jnp.transpose(gw_pool.reshape(nwin, NDEV, gw // NDEV, gw), (1, 0, 2, 3))
    grads_b = [gw_out, gw_po, gw_pool_st, gw_vg.reshape(NDEV, 2, W, D // NDEV)]
    tok, wait_pair_b = pair_exchange("pair_exchange_mix", grads_b, 6)
    du_ssm, g_bt_re, g_bt_im, g_ct_re, g_ct_im, g_f, g_d, g_a = s5_bwd(
        proj, xsb_all, dz, zp, s5_params, nblk, after=[tok])
    grads_b, got_b = wait_pair_b(du_ssm)
    parts_b = [pair_sum("pair_sum_mix%d" % i, g, t, place) for i, (g, t) in enumerate(zip(grads_b, got_b))]
    tok, wait_chip_b = chip_exchange("chip_exchange_mix", parts_b, 7)

    dproj = jnp.concatenate([du_ssm, du_pool, dga, dgb_], axis=1)
    gw_in = mm_tn("gw_in", h1, dproj, BF16, NDEV, 1, after=[tok])
    tok, wait_pair_c = pair_exchange("pair_exchange_in", [gw_in], 8)
    (dh1,) = mm_nt("d_proj", dproj, wg_in, F32, 4, after=[tok])
    grads_c, got_c = wait_pair_c(dh1)
    parts_c = [pair_sum("pair_sum_in", grads_c[0], got_c[0], place)]
    tok, wait_chip_c = chip_exchange("chip_exchange_in", parts_c, 9)

    def e10(t, b):
        dh1v, xv, dxav = t
        xh, rs = _ln_stats(xv)
        return ([dxav + _ln_bwd(dh1v * (1.0 + b[0]), xh, rs)],
                [_colsum(dh1v * xh), _colsum(dh1v)])
    grad_x, d_sc1, d_sh1 = _rowwise("ln_mod1_bwd", e10, S, ts, [(dh1, D, 0), (x2d, D, 0), (dxa, D, 0)],
                                    [sc1], [(D, F32)], [D, D], after=[tok])

    g_b_re = jnp.transpose(g_bt_re.reshape(H, G, P), (1, 0, 2))
    g_b_im = jnp.transpose(g_bt_im.reshape(H, G, P), (1, 0, 2))
    g_c_re = jnp.transpose(g_ct_re.reshape(H, G, P), (1, 0, 2))
    g_c_im = jnp.transpose(g_ct_im.reshape(H, G, P), (1, 0, 2))
    d_ab = jnp.transpose(g_a.reshape(nblk, 2, GPB, P), (1, 0, 2, 3)).reshape(2, G, P)
    g_lr, g_li, g_ldt = s5_disc_bwd(lam_re[0], lam_im[0], log_dt[0].reshape(G, 1), d_ab,
                                    g_f.reshape(2, G, P))

    dmod = jnp.concatenate([d_sh1, d_sc1, d_g1, d_sh2, d_sc2, d_g2], axis=1)
    small_g = [dmod, g_lr, g_li, g_ldt, g_b_re, g_b_im, g_c_re, g_c_im, g_d, g_pscale,
               g_ln1g, g_ln1b, g_ln2g, g_ln2b, loss_acc]
    packed_g = _small_pack(small_g)
    (parts_all,) = seq_all_gather("gather_small", [packed_g], 10)
    glu_w = jnp.stack([w_glu_val[0], w_glu_gate[0]])
    glu_m = jnp.stack([m_w_glu_val[0], m_w_glu_gate[0]])
    glu_v = jnp.stack([v_w_glu_val[0], v_w_glu_gate[0]])
    wmv = [(w_ff2[0], m_w_ff2[0], v_w_ff2[0]), (w_ff1[0], m_w_ff1[0], v_w_ff1[0]),
           (w_out[0], m_w_out[0], v_w_out[0]), (w_pool_out[0], m_w_pool_out[0], v_w_pool_out[0]),
           (w_pool[0], m_w_pool[0], v_w_pool[0]), (glu_w, glu_m, glu_v)]
    _, got3_a = wait_chip_a(packed_g)
    upd = [adamw_sharded("adamw_%d" % i, g, p, t, w, m, v, place)
           for i, (g, p, t, (w, m, v)) in enumerate(zip(grads_a, got_a, got3_a, wmv[:2]))]
    _, got3_b = wait_chip_b(upd[-1][0])
    upd += [adamw_sharded("adamw_%d" % (2 + i), g, p, t, w, m, v, place)
            for i, (g, p, t, (w, m, v)) in enumerate(zip(grads_b, got_b, got3_b, wmv[2:]))]
    u_ff2, u_ff1, u_out, u_po, u_pool, u_glu = upd

    gsum = sum_small(parts_all, after=[upd[-1][0]])
    def swap_b(ts_):
        return [jnp.swapaxes(t, 2, 3) if i in (4, 5) else t for i, t in enumerate(ts_)]

    sg = _small_unpack(gsum, [t.shape for t in swap_b(small_names)] + [(1, LANES)])
    loss, sg = sg[-1][0, 0], sg[:-1]
    sd, sm, sv = adamw_natural(sg, swap_b(small_names), swap_b(small_m), swap_b(small_v))
    sg, sd, sm, sv = swap_b(sg), swap_b(sd), swap_b(sm), swap_b(sv)

    nmod = 6 * D
    dmod_all = parts_all[:, :nmod // LANES, :].reshape(NDEV, nmod)
    c_all_t = jnp.transpose(c_all.reshape(NDEV, D))
    ada_out = adamw_ada(c_all_t, dmod_all, w_ada[0], m_w_ada[0], v_w_ada[0], my_dev)
    _, got3_c = wait_chip_c(ada_out[0])
    u_in = adamw_sharded("adamw_6", grads_c[0], got_c[0], got3_c[0], w_in[0], m_w_in[0], v_w_in[0], place)

    def pick(k):
        return [ada_out[k][None], sg_sd[k][0], u_in[k][None]] + [t for t in sg_sd[k][1:9]] + \
               [u_glu[k][0][None], u_glu[k][1][None], u_pool[k][None], sg_sd[k][9], u_po[k][None],
                u_out[k][None], sg_sd[k][10], sg_sd[k][11], u_ff1[k][None], u_ff2[k][None],
                sg_sd[k][12], sg_sd[k][13]]

    sg_sd = [sg, sd, sm, sv]
    return (loss, grad_x[None], *pick(0), *pick(1), *pick(2), *pick(3))
```

```python
import functools
import math

import jax
import jax.numpy as jnp
from jax import lax
from jax.experimental import pallas as pl
from jax.experimental.pallas import tpu as pltpu
from jax.experimental.pallas import tpu_sc as plsc

F32 = jnp.float32
BF16 = jnp.bfloat16
MESH = pl.DeviceIdType.MESH
NDEV = 8
NCHIP = 4

SSM_GROUP = 16
SSM_STATE = 64
GROUPS_PER_BLOCK = 8
POOL_WINDOWS = (2, 4, 8, 16)
LN_EPS = 1e-5
ALPHA = 2.0 ** 0.25
ADAM_LR, ADAM_B1, ADAM_B2, ADAM_EPS, ADAM_WD, ADAM_STEP = 0.001, 0.9, 0.999, 1e-08, 0.01, 10
SUBLANES = 8
LANES = 128
VMEM_LIMIT = 56 * 1024 * 1024


def _params(sem=None, vmem=VMEM_LIMIT):
    return pltpu.CompilerParams(dimension_semantics=sem, vmem_limit_bytes=vmem)


def _tile(n, pref):
    if n <= pref:
        return n
    t = 1 << (pref.bit_length() - 1)
    while n % t:
        t //= 2
    return t


def _cast_epi(vals, ex, outs):
    c = vals[0].shape[1]
    for s, v in enumerate(vals):
        outs[0][:, s * c:(s + 1) * c] = v.astype(outs[0].dtype)


ANY = pl.BlockSpec(memory_space=pl.ANY)


def _with_after(body, n_in, after):
    if not after:
        return body
    n_af = len(after)

    def wrapped(*refs):
        return body(*refs[:n_in], *refs[n_in + n_af:])
    return wrapped


def _mm(name, kind, a, b, grid, a_spec, b_spec, outs, acc_shape, nsub=1, c=None,
        pro=None, epi=None, extras=(), stacked_out=False, after=()):
    nk = grid[2]
    n_ex, n_out = len(extras), len(outs)

    def finish(vals, ex, out_refs):
        if epi is not None:
            epi(vals, ex, out_refs)
        elif stacked_out:
            for s, v in enumerate(vals):
                out_refs[0][s] = v.astype(out_refs[0].dtype)
        else:
            _cast_epi(vals, ex, out_refs)

    def body(*refs):
        mm_step(refs[0], refs[1], refs[2:2 + n_ex], refs[2 + n_ex:2 + n_ex + n_out], refs[-1])

    def mm_step(a_ref, b_ref, ex, out_refs, acc):
        k = pl.program_id(2)
        av = a_ref[...]
        if pro is not None:
            av = pro(av)
        if kind == "nn":
            prods = [jnp.dot(av, b_ref[s], preferred_element_type=F32) for s in range(nsub)]
        elif kind == "nt":
            t = None
            for s in range(nsub):
                d = lax.dot_general(av[:, s * c:(s + 1) * c], b_ref[s], (((1,), (1,)), ((), ())),
                                    preferred_element_type=F32)
                t = d if t is None else t + d
            prods = [t]
        else:
            t = lax.dot_general(av, b_ref[...], (((0,), (0,)), ((), ())), preferred_element_type=F32)
            prods = [t[:, s * c:(s + 1) * c] for s in range(nsub)] if stacked_out else [t]
        if nk == 1:
            finish(prods, ex, out_refs)
            return
        w = prods[0].shape[1]

        @pl.when(k == 0)
        def _():
            for s, p in enumerate(prods):
                acc[:, s * w:(s + 1) * w] = p

        @pl.when(jnp.logical_and(k > 0, k < nk - 1))
        def _():
            for s, p in enumerate(prods):
                acc[:, s * w:(s + 1) * w] += p

        @pl.when(k == nk - 1)
        def _():
            finish([acc[:, s * w:(s + 1) * w] + p for s, p in enumerate(prods)], ex, out_refs)

    return pl.pallas_call(
        _with_after(body, 2 + n_ex, after), name=name, grid=grid,
        in_specs=[a_spec, b_spec] + [e[1] for e in extras] + [ANY] * len(after),
        out_specs=[o[1] for o in outs],
        out_shape=[o[0] for o in outs],
        scratch_shapes=[pltpu.VMEM(acc_shape, F32)] if nk > 1 else [],
        compiler_params=_params(("parallel", "parallel", "arbitrary")),
    )(a, b, *[e[0] for e in extras], *after)


def _sds(shape, dtype):
    return jax.ShapeDtypeStruct(shape, dtype)


def mm_nn(name, a, b3, out_dtype, nsub, tm=1024, tk=2048, tn=None, pro=None, epi=None,
          extras=(), after=()):
    M = a.shape[0]
    nb, K, cdim = b3.shape
    tm, tk = _tile(M, tm), _tile(K, tk)
    if nb == 1:
        tn = _tile(cdim, tn or 1024)
        nsub, c, nj = 1, tn, cdim // tn
        b_spec = pl.BlockSpec((1, tk, tn), lambda i, j, k: (0, k, j))
        N = cdim
    else:
        c, nj, tn = cdim, nb // nsub, nsub * cdim
        b_spec = pl.BlockSpec((nsub, tk, cdim), lambda i, j, k: (j, k, 0))
        N = nb * cdim
    a_spec = pl.BlockSpec((tm, tk), lambda i, j, k: (i, k))
    grid = (M // tm, nj, K // tk)
    outs = [(_sds((M, N), out_dtype), pl.BlockSpec((tm, tn), lambda i, j, k: (i, j)))]
    return _mm(name, "nn", a, b3, grid, a_spec, b_spec, outs, (tm, tn), nsub, c, pro, epi, extras,
               after=after)


def mm_nt(name, a, b3, out_dtype, nsub, tm=1024, tn=1024, epi=None, extras=(), after=()):
    M = a.shape[0]
    nb, N, cdim = b3.shape
    tm, tn = _tile(M, tm), _tile(N, tn)
    if nb == 1:
        tk = _tile(cdim, 2048)
        nsub, c, nk = 1, tk, cdim // tk
        b_spec = pl.BlockSpec((1, tn, tk), lambda i, j, k: (0, j, k))
    else:
        c, nk, tk = cdim, nb // nsub, nsub * cdim
        b_spec = pl.BlockSpec((nsub, tn, cdim), lambda i, j, k: (k, j, 0))
    a_spec = pl.BlockSpec((tm, tk), lambda i, j, k: (i, k))
    grid = (M // tm, N // tn, nk)
    outs = [(_sds((M, N), out_dtype), pl.BlockSpec((tm, tn), lambda i, j, k: (i, j)))]
    return _mm(name, "nt", a, b3, grid, a_spec, b_spec, outs, (tm, tn), nsub, c, None, epi, extras,
               after=after)


def mm_tn(name, a, b, out_dtype, nb, nsub, tma=1024, tk=2048, pro=None, after=()):
    S, Ka = a.shape
    N = b.shape[1]
    tk, tma = _tile(S, tk), _tile(Ka, tma)
    a_spec = pl.BlockSpec((tk, tma), lambda i, j, k: (k, i))
    if nsub == 0:
        tn = _tile(N, 1024)
        res = _mm(name, "tn", a, b, (Ka // tma, N // tn, S // tk), a_spec,
                  pl.BlockSpec((tk, tn), lambda i, j, k: (k, j)),
                  [(_sds((Ka, N), out_dtype), pl.BlockSpec((tma, tn), lambda i, j, k: (i, j)))],
                  (tma, tn), 1, tn, pro, None, (), after=after)[0]
        return res.reshape(nb, Ka // nb, N)
    c = N // nb
    tn = nsub * c
    outs = [(_sds((nb, Ka, c), out_dtype), pl.BlockSpec((nsub, tma, c), lambda i, j, k: (j, i, 0)))]
    return _mm(name, "tn", a, b, (Ka // tma, nb // nsub, S // tk), a_spec,
               pl.BlockSpec((tk, tn), lambda i, j, k: (k, j)), outs, (tma, tn), nsub, c,
               pro, None, (), stacked_out=True, after=after)[0]


def _rowwise(name, fn, S, ts, tiled, bcast, tiled_out, acc_out, after=()):
    nt, nb, no, na = len(tiled), len(bcast), len(tiled_out), len(acc_out)

    def body(*refs):
        tin = [r[...] for r in refs[:nt]]
        bin_ = [r[...] for r in refs[nt:nt + nb]]
        o_refs = refs[nt + nb:nt + nb + no]
        a_refs = refs[nt + nb + no:]
        touts, aouts = fn(tin, bin_)
        for r, v in zip(o_refs, touts):
            r[...] = v.astype(r.dtype)
        i = pl.program_id(0)

        @pl.when(i == 0)
        def _():
            for r, v in zip(a_refs, aouts):
                r[...] = v

        @pl.when(i > 0)
        def _():
            for r, v in zip(a_refs, aouts):
                r[...] += v

    in_specs = [pl.BlockSpec((ts, w), functools.partial(lambda i, cb: (i, cb), cb=cb))
                for (_, w, cb) in tiled]
    in_specs += [pl.BlockSpec(b.shape, lambda i: (0, 0)) for b in bcast]
    out_specs = [pl.BlockSpec((ts, w), lambda i: (i, 0)) for (w, _) in tiled_out]
    out_specs += [pl.BlockSpec((1, w), lambda i: (0, 0)) for w in acc_out]
    out_shape = [_sds((S, w), d) for (w, d) in tiled_out] + [_sds((1, w), F32) for w in acc_out]
    return pl.pallas_call(
        _with_after(body, nt + nb, after), name=name, grid=(S // ts,),
        in_specs=in_specs + [ANY] * len(after), out_specs=out_specs,
        out_shape=out_shape, compiler_params=_params(("arbitrary",)),
    )(*[t[0] for t in tiled], *bcast, *after)


def _ln_stats(v):
    mu = jnp.mean(v, axis=-1, keepdims=True)
    vc = v - mu
    var = jnp.mean(vc * vc, axis=-1, keepdims=True)
    rstd = lax.rsqrt(var + LN_EPS)
    return vc * rstd, rstd


def _ln_bwd(dxhat, xhat, rstd):
    return rstd * (dxhat - jnp.mean(dxhat, axis=-1, keepdims=True)
                   - xhat * jnp.mean(dxhat * xhat, axis=-1, keepdims=True))


def _colsum(v):
    return jnp.sum(v, axis=0, keepdims=True)


def _sigmoid(v):
    return 1.0 / (1.0 + jnp.exp(-v))


_GELU_C = math.sqrt(2.0 / math.pi)


def _gelu(v):
    return 0.5 * v * (1.0 + jnp.tanh(_GELU_C * (v + 0.044715 * v * v * v)))


def _gelu_grad(v):
    t = jnp.tanh(_GELU_C * (v + 0.044715 * v * v * v))
    return 0.5 * (1.0 + t) + 0.5 * v * (1.0 - t * t) * _GELU_C * (1.0 + 3 * 0.044715 * v * v)


def _disc(lr, li, ldt):
    dt = jnp.exp(ldt)
    mag = jnp.exp(lr * dt)
    ang = li * dt
    ab_re = mag * jnp.cos(ang)
    ab_im = mag * jnp.sin(ang)
    num_re = ab_re - 1.0
    num_im = ab_im
    den = lr * lr + li * li
    f_re = (num_re * lr + num_im * li) / den
    f_im = (num_im * lr - num_re * li) / den
    return ab_re, ab_im, f_re, f_im


def _cmul(ar, ai, br, bi):
    return ar * br - ai * bi, ar * bi + ai * br


SCAN_FOLD = 4
NCONST = 18


def s5_disc(lam_re, lam_im, log_dt):
    G, P = lam_re.shape

    def body(lr_ref, li_ref, ldt_ref, f_ref, k_ref):
        ab_re, ab_im, f_re, f_im = _disc(lr_ref[...], li_ref[...], ldt_ref[...])
        f_ref[0] = f_re
        f_ref[1] = f_im
        fr, fi = ab_re, ab_im
        for _ in range(SCAN_FOLD - 1):
            fr, fi = _cmul(fr, fi, ab_re, ab_im)
        pr, pi = [fr], [fi]
        for _ in range(SUBLANES - 1):
            nr, ni = _cmul(pr[-1], pi[-1], fr, fi)
            pr.append(nr)
            pi.append(ni)
        zero = jnp.zeros_like(ab_re)
        for r in range(SUBLANES):
            k_ref[16, r] = ab_re
            k_ref[17, r] = ab_im
        for n, sh in enumerate((1, 2, 4)):
            for r in range(SUBLANES):
                k_ref[2 * n, r] = pr[sh - 1] if r >= sh else zero
                k_ref[2 * n + 1, r] = pi[sh - 1] if r >= sh else zero
                k_ref[8 + 2 * n, r] = pr[sh - 1] if r + sh < SUBLANES else zero
                k_ref[8 + 2 * n + 1, r] = -pi[sh - 1] if r + sh < SUBLANES else zero
        for r in range(SUBLANES):
            k_ref[6, r] = pr[r]
            k_ref[7, r] = pi[r]
            k_ref[14, r] = pr[SUBLANES - 1 - r]
            k_ref[15, r] = -pi[SUBLANES - 1 - r]

    vm = pl.BlockSpec(memory_space=pltpu.VMEM)
    return pl.pallas_call(
        body, name="s5_disc", in_specs=[vm, vm, vm], out_specs=[vm, vm],
        out_shape=[_sds((2, G, P), F32), _sds((NCONST, SUBLANES, G, P), F32)],
    )(lam_re, lam_im, log_dt)


def s5_disc_bwd(lam_re, lam_im, log_dt, d_ab, d_f):
    G, P = lam_re.shape

    def body(lr_ref, li_ref, ldt_ref, dab_ref, df_ref, glr_ref, gli_ref, gdt_ref):
        _, vjp = jax.vjp(_disc, lr_ref[...], li_ref[...], ldt_ref[...])
        glr, gli, gdt = vjp((dab_ref[0], dab_ref[1], df_ref[0], df_ref[1]))
        glr_ref[...] = glr
        gli_ref[...] = gli
        gdt_ref[...] = gdt

    vm = pl.BlockSpec(memory_space=pltpu.VMEM)
    return pl.pallas_call(
        body, name="s5_disc_bwd", in_specs=[vm] * 5, out_specs=[vm] * 3,
        out_shape=[_sds((G, P), F32), _sds((G, P), F32), _sds((G, 1), F32)],
    )(lam_re, lam_im, log_dt, d_ab, d_f)


def _group_mask(cw, nst):
    row = lax.broadcasted_iota(jnp.int32, (cw, 2 * nst), 0) // SSM_GROUP
    col = (lax.broadcasted_iota(jnp.int32, (cw, 2 * nst), 1) % nst) // SSM_STATE
    return row == col


def _spread(t, mask):
    reps = mask.shape[0] // t.shape[0]
    return jnp.where(mask, jnp.tile(t, (reps, 1)), 0.0).astype(BF16)


def _gather_groups(t, mask):
    t = jnp.where(mask, t, 0.0)
    out = t[0:SSM_GROUP]
    for g in range(1, t.shape[0] // SSM_GROUP):
        out = out + t[g * SSM_GROUP:(g + 1) * SSM_GROUP]
    return out


def _s5_operands(f_ref, br_ref, bi_ref, cr_ref, ci_ref, mask):
    fr, fi = f_ref[0], f_ref[1]
    br, bi = br_ref[...], bi_ref[...]
    bm = _spread(jnp.concatenate([fr * br - fi * bi, fr * bi + fi * br], axis=1), mask)
    cm = _spread(jnp.concatenate([cr_ref[...], -ci_ref[...]], axis=1), mask)
    return bm, cm


def _planes_put(ref, val):
    for c in range(ref.shape[0]):
        ref[c] = val[:, c * LANES:(c + 1) * LANES]


def _planes_get(ref):
    return jnp.concatenate([ref[c] for c in range(ref.shape[0])], axis=1)


def _rows_ld(ref, start, lo, hi):
    rows = pl.ds(start, SUBLANES, stride=SCAN_FOLD)
    return jnp.concatenate([ref[c, rows, :] for c in range(lo // LANES, hi // LANES)], axis=1)


def _rows_st(ref, start, lo, val):
    rows = pl.ds(start, SUBLANES, stride=SCAN_FOLD)
    for k in range(val.shape[1] // LANES):
        ref[lo // LANES + k, rows, :] = val[:, k * LANES:(k + 1) * LANES]


def _phases(ref, base, lo, hi):
    return [_rows_ld(ref, base + j, lo, hi) for j in range(SCAN_FOLD)]


def _row_bcast(v, r):
    return jnp.broadcast_to(v[r:r + 1, :], v.shape)


def _scan_fwd(xs, k_ref, nst):
    m = SCAN_FOLD
    ngroup = xs.shape[1] // (SUBLANES * m)
    row = lax.broadcasted_iota(jnp.int32, (SUBLANES, nst), 0)

    def step(t, carry):
        cr, ci = carry
        base = pl.multiple_of(t * (SUBLANES * m), SUBLANES * m)
        ar, ai = k_ref[16], k_ref[17]
        pr, pi = _phases(xs, base, 0, nst), _phases(xs, base, nst, 2 * nst)
        vr, vi = pr[0], pi[0]
        for j in range(1, m):
            vr, vi = pr[j] + ar * vr - ai * vi, pi[j] + ar * vi + ai * vr
        for n, sh in enumerate((1, 2, 4)):
            sr = pltpu.roll(vr, sh, 0)
            si = pltpu.roll(vi, sh, 0)
            mr, mi = k_ref[2 * n], k_ref[2 * n + 1]
            vr, vi = vr + mr * sr - mi * si, vi + mr * si + mi * sr
        qr, qi = k_ref[6], k_ref[7]
        vr, vi = vr + qr * cr - qi * ci, vi + qr * ci + qi * cr
        _rows_st(xs, base + m - 1, 0, vr)
        _rows_st(xs, base + m - 1, nst, vi)
        xr = jnp.where(row == 0, cr, pltpu.roll(vr, 1, 0))
        xi = jnp.where(row == 0, ci, pltpu.roll(vi, 1, 0))
        for j in range(m - 1):
            xr, xi = pr[j] + ar * xr - ai * xi, pi[j] + ar * xi + ai * xr
            _rows_st(xs, base + j, 0, xr)
            _rows_st(xs, base + j, nst, xi)
        return _row_bcast(vr, SUBLANES - 1), _row_bcast(vi, SUBLANES - 1)

    zero = jnp.zeros((SUBLANES, nst), F32)
    lax.fori_loop(0, ngroup, step, (zero, zero))


def _scan_bwd(g, xs, k_ref, nst):
    m = SCAN_FOLD
    ngroup = g.shape[1] // (SUBLANES * m)
    row = lax.broadcasted_iota(jnp.int32, (SUBLANES, nst), 0)

    def step(tt, carry):
        cr, ci, dar, dai = carry
        t = ngroup - 1 - tt
        base = pl.multiple_of(t * (SUBLANES * m), SUBLANES * m)
        ar, ai = k_ref[16], -k_ref[17]
        dr, di = _phases(g, base, 0, nst), _phases(g, base, nst, 2 * nst)
        wr, wi = dr[m - 1], di[m - 1]
        for j in range(m - 2, -1, -1):
            wr, wi = dr[j] + ar * wr - ai * wi, di[j] + ar * wi + ai * wr
        for n, sh in enumerate((1, 2, 4)):
            sr = pltpu.roll(wr, SUBLANES - sh, 0)
            si = pltpu.roll(wi, SUBLANES - sh, 0)
            mr, mi = k_ref[8 + 2 * n], k_ref[8 + 2 * n + 1]
            wr, wi = wr + mr * sr - mi * si, wi + mr * si + mi * sr
        qr, qi = k_ref[14], k_ref[15]
        wr, wi = wr + qr * cr - qi * ci, wi + qr * ci + qi * cr
        gr, gi = [None] * m, [None] * m
        gr[0], gi[0] = wr, wi
        nr = jnp.where(row == SUBLANES - 1, cr, pltpu.roll(wr, SUBLANES - 1, 0))
        ni = jnp.where(row == SUBLANES - 1, ci, pltpu.roll(wi, SUBLANES - 1, 0))
        for j in range(m - 1, 0, -1):
            nr, ni = dr[j] + ar * nr - ai * ni, di[j] + ar * ni + ai * nr
            gr[j], gi[j] = nr, ni
        for j in range(m):
            _rows_st(g, base + j, 0, gr[j])
            _rows_st(g, base + j, nst, gi[j])
        xr, xi = _phases(xs, base, 0, nst), _phases(xs, base, nst, 2 * nst)
        pbase = pl.multiple_of(jnp.maximum(t - 1, 0) * (SUBLANES * m), SUBLANES * m)
        live = (t > 0).astype(F32)
        lr = _row_bcast(_rows_ld(xs, pbase + m - 1, 0, nst), SUBLANES - 1) * live
        li = _row_bcast(_rows_ld(xs, pbase + m - 1, nst, 2 * nst), SUBLANES - 1) * live
        xmr = [jnp.where(row == 0, lr, pltpu.roll(xr[m - 1], 1, 0))] + xr[:m - 1]
        xmi = [jnp.where(row == 0, li, pltpu.roll(xi[m - 1], 1, 0))] + xi[:m - 1]
        for j in range(m):
            dar = dar + gr[j] * xmr[j] + gi[j] * xmi[j]
            dai = dai + gi[j] * xmr[j] - gr[j] * xmi[j]
        return _row_bcast(wr, 0), _row_bcast(wi, 0), dar, dai

    zero = jnp.zeros((SUBLANES, nst), F32)
    _, _, dar, dai = lax.fori_loop(0, ngroup, step, (zero, zero, zero, zero))
    return _colsum(dar), _colsum(dai)


def _s5_param_specs(cw, nst):
    hp = pl.BlockSpec((SSM_GROUP, nst), lambda b: (0, b))
    return [pl.BlockSpec((2, 1, nst), lambda b: (0, 0, b)), hp, hp, hp, hp,
            pl.BlockSpec((1, cw), lambda b: (0, b)),
            pl.BlockSpec((NCONST, SUBLANES, nst), lambda b: (0, 0, b))]


def s5_fwd(proj, params, nb):
    S = proj.shape[0]
    nst = params[1].shape[1] // nb
    cw = nst // SSM_STATE * SSM_GROUP

    def body(u_ref, f_ref, br_ref, bi_ref, cr_ref, ci_ref, d_ref, k_ref, z_ref, xsb_ref, zp_ref, xs):
        bm, cm = _s5_operands(f_ref, br_ref, bi_ref, cr_ref, ci_ref, _group_mask(cw, nst))
        u = u_ref[...]
        _planes_put(xs, jnp.dot(u.astype(BF16), bm, preferred_element_type=F32))
        _scan_fwd(xs, k_ref, nst)
        xsb = _planes_get(xs).astype(BF16)
        xsb_ref[...] = xsb
        y = lax.dot_general(xsb, cm, (((1,), (1,)), ((), ())), preferred_element_type=F32)
        y = y + d_ref[...] * u
        z_ref[...] = _gelu(y).astype(BF16)
        zp_ref[...] = _gelu_grad(y).astype(BF16)

    return pl.pallas_call(
        body, name="s5_fwd", grid=(nb,),
        in_specs=[pl.BlockSpec((S, cw), lambda b: (0, b))] + _s5_param_specs(cw, nst),
        out_specs=[pl.BlockSpec((S, cw), lambda b: (0, b)), pl.BlockSpec((S, 2 * nst), lambda b: (0, b)),
                   pl.BlockSpec((S, cw), lambda b: (0, b))],
        out_shape=[_sds((S, nb * cw), BF16), _sds((S, nb * 2 * nst), BF16), _sds((S, nb * cw), BF16)],
        scratch_shapes=[pltpu.VMEM((2 * nst // LANES, S, LANES), F32)],
        compiler_params=_params(("arbitrary",)),
    )(proj, *params)


def s5_bwd(proj, xsb_all, dz, zp, params, nb, after=()):
    S = proj.shape[0]
    nst = params[1].shape[1] // nb
    cw = nst // SSM_STATE * SSM_GROUP

    def body(u_ref, xsb_ref, dz_ref, zp_ref, f_ref, br_ref, bi_ref, cr_ref, ci_ref, d_ref, k_ref,
             du_ref, gbr_ref, gbi_ref, gcr_ref, gci_ref, gf_ref, gd_ref, ga_ref, xs, g):
        mask = _group_mask(cw, nst)
        bm, cm = _s5_operands(f_ref, br_ref, bi_ref, cr_ref, ci_ref, mask)
        u = u_ref[...]
        ub = u.astype(BF16)
        d = d_ref[...]
        xsb = xsb_ref[...]
        _planes_put(xs, xsb.astype(F32))
        dy = dz_ref[...].astype(F32) * zp_ref[...].astype(F32)
        gd_ref[...] = _colsum(dy * u)
        dyb = dy.astype(BF16)
        gc = _gather_groups(lax.dot_general(dyb, xsb, (((0,), (0,)), ((), ())),
                                            preferred_element_type=F32), mask)
        gcr_ref[...] = gc[:, :nst]
        gci_ref[...] = -gc[:, nst:]
        _planes_put(g, jnp.dot(dyb, cm, preferred_element_type=F32))
        ar, ai = _scan_bwd(g, xs, k_ref, nst)
        ga_ref[0, 0:1, :] = ar
        ga_ref[0, 1:2, :] = ai
        gb = _planes_get(g).astype(BF16)
        du = lax.dot_general(gb, bm, (((1,), (1,)), ((), ())), preferred_element_type=F32) + d * dy
        du_ref[...] = du.astype(BF16)
        gbb = _gather_groups(lax.dot_general(ub, gb, (((0,), (0,)), ((), ())),
                                             preferred_element_type=F32), mask)
        dr, di = gbb[:, :nst], gbb[:, nst:]
        fr, fi = f_ref[0], f_ref[1]
        br, bi = br_ref[...], bi_ref[...]
        gbr_ref[...] = fr * dr + fi * di
        gbi_ref[...] = fr * di - fi * dr
        gf_ref[0] = _colsum(dr * br + di * bi)
        gf_ref[1] = _colsum(di * br - dr * bi)

    hp = pl.BlockSpec((SSM_GROUP, nst), lambda b: (0, b))
    hp_sds = _sds((SSM_GROUP, nb * nst), F32)
    return pl.pallas_call(
        _with_after(body, 11, after), name="s5_bwd", grid=(nb,),
        in_specs=[pl.BlockSpec((S, cw), lambda b: (0, b)),
                  pl.BlockSpec((S, 2 * nst), lambda b: (0, b)),
                  pl.BlockSpec((S, cw), lambda b: (0, b)),
                  pl.BlockSpec((S, cw), lambda b: (0, b))] + _s5_param_specs(cw, nst)
        + [ANY] * len(after),
        out_specs=[pl.BlockSpec((S, cw), lambda b: (0, b)), hp, hp, hp, hp,
                   pl.BlockSpec((2, 1, nst), lambda b: (0, 0, b)),
                   pl.BlockSpec((1, cw), lambda b: (0, b)),
                   pl.BlockSpec((1, 2, nst), lambda b: (b, 0, 0))],
        out_shape=[_sds((S, nb * cw), BF16), hp_sds, hp_sds, hp_sds, hp_sds,
                   _sds((2, 1, nb * nst), F32), _sds((1, nb * cw), F32), _sds((nb, 2, nst), F32)],
        scratch_shapes=[pltpu.VMEM((2 * nst // LANES, S, LANES), F32)] * 2,
        compiler_params=_params(("arbitrary",)),
    )(proj, xsb_all, dz, zp, *params, *after)


def _shift_rows(v, k, row, down):
    n = v.shape[0]
    if down:
        return jnp.where(row >= k, pltpu.roll(v, k, 0), 0.0)
    return jnp.where(row < n - k, pltpu.roll(v, n - k, 0), 0.0)


def _window(v, gi, row, down):
    sums = []
    s = v
    for k in (1, 2, 4, 8):
        s = s + _shift_rows(s, k, row, down)
        sums.append(s)
    out = sums[3]
    for n in (2, 1, 0):
        out = jnp.where(gi == n, sums[n], out)
    return out


def pool_fwd(proj, col0, width, gw):
    S = proj.shape[0]
    cb0 = col0 // gw

    def body(u_ref, o_ref):
        gi = pl.program_id(0)
        u = u_ref[...]
        row = lax.broadcasted_iota(jnp.int32, u.shape, 0)
        w = jnp.left_shift(2, gi)
        count = jnp.minimum(row + 1, w).astype(F32)
        o_ref[...] = (_window(u, gi, row, True) / count - u).astype(BF16)

    return pl.pallas_call(
        body, name="pool_fwd", grid=(len(POOL_WINDOWS),),
        in_specs=[pl.BlockSpec((S, gw), lambda g: (0, cb0 + g))],
        out_specs=pl.BlockSpec((S, gw), lambda g: (0, g)),
        out_shape=_sds((S, width), BF16), compiler_params=_params(("arbitrary",)),
    )(proj)


def pool_bwd(dpooled, gw):
    S, width = dpooled.shape

    def body(d_ref, o_ref):
        gi = pl.program_id(0)
        d = d_ref[...]
        row = lax.broadcasted_iota(jnp.int32, d.shape, 0)
        w = jnp.left_shift(2, gi)
        count = jnp.minimum(row + 1, w).astype(F32)
        o_ref[...] = (_window(d / count, gi, row, False) - d).astype(BF16)

    return pl.pallas_call(
        body, name="pool_bwd", grid=(len(POOL_WINDOWS),),
        in_specs=[pl.BlockSpec((S, gw), lambda g: (0, g))],
        out_specs=pl.BlockSpec((S, gw), lambda g: (0, g)),
        out_shape=_sds((S, width), BF16), compiler_params=_params(("arbitrary",)),
    )(dpooled)


def _place():
    x, y, c = lax.axis_index("x"), lax.axis_index("y"), lax.axis_index("c")
    chips = [(1 - x, y), (x, 1 - y), (1 - x, 1 - y)]
    return x, y, c, chips


HBM = pl.BlockSpec(memory_space=pltpu.HBM)


def _routed_gather_body(n):
    def body(*refs):
        ins, outs = refs[:n], refs[n:2 * n]
        send_sems, recv_sems, local_sems = refs[2 * n:]
        x, y, c, (xn, yn, dg) = _place()
        me, sibling = (x, y, c), (x, y, 1 - c)
        barrier = pltpu.get_barrier_semaphore()
        for peer in (sibling, (*xn, c), (*yn, c)):
            pl.semaphore_signal(barrier, inc=1, device_id=peer, device_id_type=MESH)
        pl.semaphore_wait(barrier, 3)

        def piece(i, p, h):
            rows = ins[i].shape[0] // 2
            return outs[i].at[4 * p[0] + 2 * p[1] + p[2], pl.ds(h * rows, rows)]

        def copy(i, k, src, dst, to):
            return pltpu.make_async_remote_copy(src_ref=src, dst_ref=dst, send_sem=send_sems.at[i, k],
                                                recv_sem=recv_sems.at[i, k], device_id=to,
                                                device_id_type=MESH)

        started = []

        def go(cp):
            cp.start()
            started.append(cp)

        for i in range(n):
            rows = ins[i].shape[0] // 2
            for h in range(2):
                own = ins[i].at[pl.ds(h * rows, rows)]
                go(copy(i, 1 + h, own, piece(i, me, h), (*xn, c)))
                go(copy(i, 3 + h, own, piece(i, me, h), (*yn, c)))
        for i in range(n):
            go(copy(i, 0, ins[i], outs[i].at[4 * x + 2 * y + c], sibling))
        mine = [pltpu.make_async_copy(ins[i], outs[i].at[4 * x + 2 * y + c], local_sems.at[i])
                for i in range(n)]
        for cp in mine:
            cp.start()
        for i in range(n):
            for k, chip, h, onward, ksib in ((1, xn, 0, (5, yn), 7), (4, yn, 1, (6, xn), 10),
                                            (2, xn, 1, None, 8), (3, yn, 0, None, 9),
                                            (5, dg, 0, None, 11), (6, dg, 1, None, 12)):
                got = piece(i, (*chip, c), h)
                copy(i, k, got, got, me).wait_recv()
                if onward is not None:
                    go(copy(i, onward[0], got, got, (*onward[1], c)))
                go(copy(i, ksib, got, got, sibling))
        for i in range(n):
            block = outs[i].at[4 * x + 2 * y + 1 - c]
            copy(i, 0, block, block, me).wait_recv()
            for ksib, chip, h in ((7, xn, 0), (10, yn, 1), (8, xn, 1), (9, yn, 0), (11, dg, 0), (12, dg, 1)):
                got = piece(i, (*chip, 1 - c), h)
                copy(i, ksib, got, got, me).wait_recv()
        for cp in started:
            cp.wait_send()
        for cp in mine:
            cp.wait()

    return body


def _on_sequencer(name, body, arrays, out_sds, sems, collective_id):
    ins = [jax.new_ref(a, memory_space=pltpu.MemorySpace.HBM) for a in arrays]
    outs = [jax.empty_ref(s, memory_space=pltpu.MemorySpace.HBM) for s in out_sds]

    @pl.kernel(mesh=plsc.ScalarSubcoreMesh(axis_name="sequencer", num_cores=1), name=name,
               scratch_types=tuple(sems),
               compiler_params=pltpu.CompilerParams(collective_id=collective_id))
    def launch(*sem_refs):
        body(*ins, *outs, *sem_refs)

    launch()
    return [o[...] for o in outs]


def seq_all_gather(name, shards, collective_id):
    n = len(shards)
    return _on_sequencer(
        name, _routed_gather_body(n), shards, [_sds((NDEV,) + s.shape, s.dtype) for s in shards],
        [pltpu.SemaphoreType.DMA((n, 13)), pltpu.SemaphoreType.DMA((n, 13)),
         pltpu.SemaphoreType.DMA((n,))], collective_id)


def pair_exchange(name, grads, collective_id):
    def plan(srcs, lands):
        x, y, c, _ = _place()
        return ([(i, q, srcs[i].at[2 * q + 1 - c], lands[i].at[q], (x, y, 1 - c))
                 for i in range(len(srcs)) for q in range(NCHIP)], [(x, y, 1 - c)])

    return _split_exchange(name, grads, [_sds((NCHIP,) + g.shape[1:], g.dtype) for g in grads],
                           plan, NCHIP, collective_id)


SEM = pl.BlockSpec(memory_space=pltpu.SEMAPHORE)


def _split_exchange(name, srcs, land_sds, plan, ncopy, collective_id):
    n = len(srcs)
    nsem = n * ncopy
    effect = pltpu.SideEffectType.DATAFLOW_SIDE_EFFECTING

    def descriptors(src_refs, land_refs, send_sems, recv_sems):
        copies, peers = plan(src_refs, land_refs)
        return [pltpu.make_async_remote_copy(src_ref=s, dst_ref=d, send_sem=send_sems[i * ncopy + k],
                                             recv_sem=recv_sems[i * ncopy + k], device_id=to,
                                             device_id_type=MESH) for (i, k, s, d, to) in copies], peers

    def start_body(*refs):
        src_refs, land_refs = refs[:n], refs[n:2 * n]
        send_sems, recv_sems = refs[2 * n:2 * n + nsem], refs[2 * n + nsem:2 * n + 2 * nsem]
        token = refs[-1]
        cps, peers = descriptors(src_refs, land_refs, send_sems, recv_sems)
        barrier = pltpu.get_barrier_semaphore()
        for peer in peers:
            pl.semaphore_signal(barrier, inc=1, device_id=peer, device_id_type=MESH)
        pl.semaphore_wait(barrier, len(peers))
        for cp in cps:
            cp.start()
        token[...] = jnp.zeros_like(token)

    lands = [pltpu.with_memory_space_constraint(lax.empty(s.shape, s.dtype), pltpu.HBM) for s in land_sds]
    srcs = [pltpu.with_memory_space_constraint(s, pltpu.HBM) for s in srcs]
    res = pl.pallas_call(
        start_body, name=name + "_start",
        out_shape=(pltpu.SemaphoreType.DMA(()),) * (2 * nsem)
        + tuple(pltpu.HBM(s.shape, s.dtype) for s in srcs)
        + tuple(pltpu.HBM(s.shape, s.dtype) for s in land_sds) + (_sds((SUBLANES, LANES), F32),),
        in_specs=[HBM] * (2 * n),
        out_specs=(SEM,) * (2 * nsem) + (HBM,) * (2 * n) + (pl.BlockSpec(memory_space=pltpu.VMEM),),
        input_output_aliases={i: 2 * nsem + i for i in range(2 * n)},
        compiler_params=pltpu.CompilerParams(has_side_effects=effect, collective_id=collective_id),
    )(*srcs, *lands)
    sems = res[:2 * nsem]
    thru = res[2 * nsem:2 * nsem + 2 * n]
    token = res[-1]

    def wait(after):
        def wait_body(*refs):
            src_refs, land_refs = refs[:n], refs[n:2 * n]
            cps, _ = descriptors(src_refs, land_refs, refs[2 * n:2 * n + nsem],
                                 refs[2 * n + nsem:2 * n + 2 * nsem])
            for cp in cps:
                cp.wait_send()
            for cp in cps:
                cp.wait_recv()

        out = pl.pallas_call(
            wait_body, name=name + "_wait",
            out_shape=tuple(pltpu.HBM(s.shape, s.dtype) for s in srcs)
            + tuple(pltpu.HBM(s.shape, s.dtype) for s in land_sds),
            in_specs=[HBM] * (2 * n) + [SEM] * (2 * nsem) + [pl.BlockSpec(memory_space=pl.ANY)],
            out_specs=(HBM,) * (2 * n),
            input_output_aliases={i: i for i in range(2 * n)},
            compiler_params=pltpu.CompilerParams(has_side_effects=effect),
        )(*thru, *sems, after)
        return list(out[:n]), list(out[n:])

    return token, wait


def pair_sum(name, grad, got, place):
    shp = grad.shape[1:]
    r, cdim = shp[-2], shp[-1]
    lead = int(math.prod(shp[:-2])) if len(shp) > 2 else 1
    g5 = grad.reshape(NCHIP, 2, lead * r, cdim)
    t4 = got.reshape(NCHIP, lead * r, cdim)
    R = lead * r
    tr = _tile(R, max(8, (1 << 20) // cdim))

    def body(p_ref, g_ref, t_ref, o_ref):
        o_ref[...] = (g_ref[0].astype(F32) + t_ref[...].astype(F32)).astype(o_ref.dtype)

    out = pl.pallas_call(
        body, name=name,
        grid_spec=pltpu.PrefetchScalarGridSpec(
            num_scalar_prefetch=1, grid=(NCHIP - 1, R // tr),
            in_specs=[pl.BlockSpec((1, 1, tr, cdim), lambda j, i, p: (p[1] ^ (j + 1), p[0], i, 0)),
                      pl.BlockSpec((1, tr, cdim), lambda j, i, p: (p[1] ^ (j + 1), i, 0))],
            out_specs=pl.BlockSpec((1, tr, cdim), lambda j, i, p: (p[1] ^ (j + 1), i, 0))),
        out_shape=_sds((NCHIP, R, cdim), grad.dtype),
        compiler_params=_params(("parallel", "parallel")),
    )(place, g5, t4)
    return out


def chip_exchange(name, parts, collective_id):
    def plan(srcs, lands):
        x, y, c, chips = _place()
        return ([(i, j, srcs[i].at[2 * chip[0] + chip[1]], lands[i].at[j], (*chip, c))
                 for i in range(len(srcs)) for j, chip in enumerate(chips)],
                [(*chip, c) for chip in chips])

    return _split_exchange(name, parts, [_sds((3,) + p.shape[1:], p.dtype) for p in parts],
                           plan, 3, collective_id)


def ada_fwd(c_row, w_ada, b_ada):
    D, cols = w_ada.shape

    def body(c_ref, w_ref, b_ref, mod_ref, call_ref, act8, part, s1, r1, s2, r2):
        x, y, c, _ = _place()
        me = 4 * x + 2 * y + c
        call_ref[me] = c_ref[...]
        cps = []
        for k in range(1, NDEV):
            to = (x ^ (k >> 2), y ^ ((k >> 1) & 1), c ^ (k & 1))
            cps.append(pltpu.make_async_remote_copy(
                src_ref=c_ref, dst_ref=call_ref.at[me], send_sem=s1.at[k - 1],
                recv_sem=r1.at[k - 1], device_id=to, device_id_type=MESH))
            cps[-1].start()
        for cp in cps:
            cp.wait()
        for b in range(NDEV):
            act8[b:b + 1, :] = call_ref[b]
        cv = act8[...]
        act = (cv * _sigmoid(cv)).astype(BF16)
        res = jnp.dot(act, w_ref[...].astype(BF16), preferred_element_type=F32)
        for b in range(NDEV):
            part[b] = res[b:b + 1, :]
        mod_ref[me] = part[me]
        cps = []
        for k in range(1, NDEV):
            to = (x ^ (k >> 2), y ^ ((k >> 1) & 1), c ^ (k & 1))
            dst = 4 * to[0] + 2 * to[1] + to[2]
            cps.append(pltpu.make_async_remote_copy(
                src_ref=part.at[dst], dst_ref=mod_ref.at[me], send_sem=s2.at[k - 1],
                recv_sem=r2.at[k - 1], device_id=to, device_id_type=MESH))
            cps[-1].start()
        for cp in cps:
            cp.wait()
        for b in range(NDEV):
            mod_ref[b] = mod_ref[b] + b_ref[b]

    vm = pl.BlockSpec(memory_space=pltpu.VMEM)
    return pl.pallas_call(
        body, name="ada_fwd", in_specs=[vm, vm, vm], out_specs=[vm, vm],
        out_shape=[_sds((NDEV, 1, cols), F32), _sds((NDEV, 1, D), F32)],
        scratch_shapes=[pltpu.VMEM((NDEV, D), F32), pltpu.VMEM((NDEV, 1, cols), F32),
                        pltpu.SemaphoreType.DMA((NDEV - 1,)), pltpu.SemaphoreType.DMA((NDEV - 1,)),
                        pltpu.SemaphoreType.DMA((NDEV - 1,)), pltpu.SemaphoreType.DMA((NDEV - 1,))],
        compiler_params=pltpu.CompilerParams(vmem_limit_bytes=VMEM_LIMIT),
    )(c_row, w_ada, b_ada.reshape(NDEV, 1, cols))


def _adamw_math(g, w, m, v):
    m2 = ADAM_B1 * m + (1.0 - ADAM_B1) * g
    v2 = ADAM_B2 * v + (1.0 - ADAM_B2) * (g * g)
    m_hat = m2 / (1.0 - ADAM_B1 ** ADAM_STEP)
    v_hat = v2 / (1.0 - ADAM_B2 ** ADAM_STEP)
    delta = -ADAM_LR * (m_hat / (jnp.sqrt(v_hat) + ADAM_EPS) + ADAM_WD * w)
    return delta, m2, v2


def adamw_sharded(name, grad8, pair4, got3, w, m, v, place, after=()):
    shape = w.shape
    cdim = shape[-1]
    R = int(math.prod(shape[:-1]))
    w2, m2, v2 = (t.reshape(R, cdim) for t in (w, m, v))
    tr = _tile(R, max(8, (1 << 19) // cdim))

    def body(q_ref, own_ref, sib_ref, t_ref, w_ref, m_ref, v_ref, g_out, d_out, m_out, v_out):
        g = own_ref[0].astype(F32) + sib_ref[0].astype(F32)
        for j in range(3):
            g = g + t_ref[j].astype(F32)
        d, mn, vn = _adamw_math(g, w_ref[...], m_ref[...], v_ref[...])
        g_out[...] = g
        d_out[...] = d
        m_out[...] = mn
        v_out[...] = vn

    spec = pl.BlockSpec((tr, cdim), lambda i, qr: (i, 0))
    outs = pl.pallas_call(
        _with_after(body, 7, after), name=name,
        grid_spec=pltpu.PrefetchScalarGridSpec(
            num_scalar_prefetch=1, grid=(R // tr,),
            in_specs=[pl.BlockSpec((1, tr, cdim), lambda i, qr: (qr[2], i, 0)),
                      pl.BlockSpec((1, tr, cdim), lambda i, qr: (qr[1], i, 0)),
                      pl.BlockSpec((3, tr, cdim), lambda i, qr: (0, i, 0)), spec, spec, spec]
            + [ANY] * len(after),
            out_specs=[spec] * 4),
        out_shape=[_sds((R, cdim), F32)] * 4,
        compiler_params=_params(("parallel",)),
    )(place, grad8.reshape(NDEV, R, cdim), pair4.reshape(NCHIP, R, cdim),
      got3.reshape(3, R, cdim), w2, m2, v2, *after)
    return [o.reshape(shape) for o in outs]


def sum_small(parts, after=()):
    R = parts.shape[1]

    def body(p_ref, g_out):
        g = p_ref[0]
        for j in range(1, NDEV):
            g = g + p_ref[j]
        g_out[...] = g

    return pl.pallas_call(
        _with_after(body, 1, after), name="sum_small", grid=(1,),
        in_specs=[pl.BlockSpec((NDEV, R, LANES), lambda i: (0, 0, 0))] + [ANY] * len(after),
        out_specs=pl.BlockSpec((R, LANES), lambda i: (0, 0)), out_shape=_sds((R, LANES), F32),
        compiler_params=_params(("arbitrary",)),
    )(parts, *after)


def adamw_natural(gs, ws, ms, vs):
    n = len(ws)
    nblk = 8
    big = [w.ndim == 4 and w.shape[1] % nblk == 0 for w in ws]

    def spec(w, is_big):
        if is_big:
            return pl.BlockSpec((1, w.shape[1] // nblk) + w.shape[2:], lambda i: (0, i, 0, 0))
        return pl.BlockSpec(w.shape, functools.partial(lambda i, nd: (0,) * nd, nd=w.ndim))

    def body(*refs):
        g_refs, w_refs, m_refs, v_refs = (refs[k * n:(k + 1) * n] for k in range(4))
        d_outs, m_outs, v_outs = (refs[(4 + k) * n:(5 + k) * n] for k in range(3))

        def update(p):
            d, mn, vn = _adamw_math(g_refs[p][...], w_refs[p][...], m_refs[p][...], v_refs[p][...])
            d_outs[p][...] = d
            m_outs[p][...] = mn
            v_outs[p][...] = vn

        for p in range(n):
            if big[p]:
                update(p)

        @pl.when(pl.program_id(0) == 0)
        def _():
            for p in range(n):
                if not big[p]:
                    update(p)

    specs = [spec(w, b) for w, b in zip(ws, big)]
    outs = pl.pallas_call(
        body, name="adamw_natural", grid=(nblk,), in_specs=specs * 4, out_specs=specs * 3,
        out_shape=[_sds(w.shape, F32) for w in ws] * 3,
        compiler_params=_params(("arbitrary",)),
    )(*gs, *ws, *ms, *vs)
    return outs[:n], outs[n:2 * n], outs[2 * n:]


def adamw_ada(c_all_t, dmod_all, w, m, v, my_dev):
    D, cols = w.shape
    tr = _tile(D, 256)

    def body(k_ref, c_ref, d_ref, w_ref, m_ref, v_ref, g_out, d_out, m_out, v_out):
        cv = c_ref[...]
        act = cv * _sigmoid(cv)
        dm = d_ref[...]
        g = act[:, 0:1] * dm[0:1, :]
        for b in range(1, NDEV):
            g = g + act[:, b:b + 1] * dm[b:b + 1, :]
        d, mn, vn = _adamw_math(g, w_ref[...], m_ref[...], v_ref[...])
        g_out[...] = g
        d_out[...] = d
        m_out[...] = mn
        v_out[...] = vn

    spec = pl.BlockSpec((tr, cols), lambda i, kr: (i, 0))
    return pl.pallas_call(
        body, name="adamw_ada",
        grid_spec=pltpu.PrefetchScalarGridSpec(
            num_scalar_prefetch=1, grid=(D // tr,),
            in_specs=[pl.BlockSpec((tr, NDEV), lambda i, kr: (i, 0)),
                      pl.BlockSpec((NDEV, cols), lambda i, kr: (0, kr[0])), spec, spec, spec],
            out_specs=[spec] * 4),
        out_shape=[_sds((D, cols), F32)] * 4,
        compiler_params=_params(("parallel",)),
    )(my_dev, c_all_t, dmod_all, w, m, v)


def _small_pack(parts):
    rows = []
    for p in parts:
        flat = p.reshape(-1)
        flat = jnp.pad(flat, (0, (-flat.shape[0]) % (SUBLANES * LANES)))
        rows.append(flat.reshape(-1, LANES))
    return jnp.concatenate(rows, axis=0)


def _small_unpack(buf, shapes):
    out, r = [], 0
    for s in shapes:
        n = int(math.prod(s))
        nr = -(-n // (SUBLANES * LANES)) * SUBLANES
        out.append(buf[r:r + nr].reshape(-1)[:n].reshape(s))
        r += nr
    return out


def kernel(x, c, w_ada, b_ada, w_in, lam_re, lam_im, log_dt, ssm_b_re, ssm_b_im, ssm_c_re, ssm_c_im, ssm_d, w_glu_val, w_glu_gate, w_pool, pool_scale, w_pool_out, w_out, ln1_g, ln1_b, w_ff1, w_ff2, ln2_g, ln2_b, loss_target, m_w_ada, m_b_ada, m_w_in, m_lam_re, m_lam_im, m_log_dt, m_ssm_b_re, m_ssm_b_im, m_ssm_c_re, m_ssm_c_im, m_ssm_d, m_w_glu_val, m_w_glu_gate, m_w_pool, m_pool_scale, m_w_pool_out, m_w_out, m_ln1_g, m_ln1_b, m_w_ff1, m_w_ff2, m_ln2_g, m_ln2_b, v_w_ada, v_b_ada, v_w_in, v_lam_re, v_lam_im, v_log_dt, v_ssm_b_re, v_ssm_b_im, v_ssm_c_re, v_ssm_c_im, v_ssm_d, v_w_glu_val, v_w_glu_gate, v_w_pool, v_pool_scale, v_w_pool_out, v_w_out, v_ln1_g, v_ln1_b, v_w_ff1, v_w_ff2, v_ln2_g, v_ln2_b):
    S, D = x.shape[1], x.shape[2]
    x2d, tgt = x[0], loss_target[0]
    W = D // 2
    G = W // SSM_GROUP
    P, H, GPB = SSM_STATE, SSM_GROUP, GROUPS_PER_BLOCK
    nblk = G // GPB
    gw = W // len(POOL_WINDOWS)
    ax, ay, ac = lax.axis_index("x"), lax.axis_index("y"), lax.axis_index("c")
    my_dev = (4 * ax + 2 * ay + ac).astype(jnp.int32).reshape(1)
    place = jnp.stack([ac, 2 * ax + ay, 4 * ax + 2 * ay + ac]).astype(jnp.int32)
    ts = _tile(S, 256)

    glu = jnp.stack([w_glu_val[0], w_glu_gate[0]]).astype(BF16)
    shards = [w_in[0].astype(BF16), glu, w_pool[0].astype(BF16), w_pool_out[0].astype(BF16),
              w_out[0].astype(BF16), w_ff1[0].astype(BF16), w_ff2[0].astype(BF16)]
    wg_in, wg_pool = seq_all_gather("gather_w_in", [shards[0], shards[2]], 1)
    wg_vg, wg_po, wg_out = seq_all_gather("gather_w_mix", [shards[1], shards[3], shards[4]], 2)
    (wg_ff1,) = seq_all_gather("gather_w_ff1", shards[5:6], 3)
    (wg_ff2,) = seq_all_gather("gather_w_ff2", shards[6:7], 11)
    wg_vg = wg_vg.reshape(2 * NDEV, W, D // NDEV)
    nwin = len(POOL_WINDOWS)
    wp_full = jnp.transpose(wg_pool, (1, 0, 2, 3)).reshape(nwin, gw, gw)
    wout_full = wg_out.reshape(1, D, D)
    wff2_full = wg_ff2.reshape(1, 4 * D, D)

    small_names = [b_ada, lam_re, lam_im, log_dt, ssm_b_re, ssm_b_im, ssm_c_re, ssm_c_im, ssm_d,
                   pool_scale, ln1_g, ln1_b, ln2_g, ln2_b]
    small_m = [m_b_ada, m_lam_re, m_lam_im, m_log_dt, m_ssm_b_re, m_ssm_b_im, m_ssm_c_re, m_ssm_c_im,
               m_ssm_d, m_pool_scale, m_ln1_g, m_ln1_b, m_ln2_g, m_ln2_b]
    small_v = [v_b_ada, v_lam_re, v_lam_im, v_log_dt, v_ssm_b_re, v_ssm_b_im, v_ssm_c_re, v_ssm_c_im,
               v_ssm_d, v_pool_scale, v_ln1_g, v_ln1_b, v_ln2_g, v_ln2_b]

    mod, c_all = ada_fwd(c, w_ada[0], b_ada)
    mod = mod.reshape(6, 1, D)
    sh1, sc1, g1, sh2, sc2, g2 = (mod[i] for i in range(6))

    f2, kconst = s5_disc(lam_re[0], lam_im[0], log_dt[0].reshape(G, 1))
    kconst = kconst.reshape(NCONST, SUBLANES, G * P)
    f2r = f2.reshape(2, 1, G * P)
    bt_re = jnp.transpose(ssm_b_re[0], (2, 0, 1)).reshape(H, G * P)
    bt_im = jnp.transpose(ssm_b_im[0], (2, 0, 1)).reshape(H, G * P)
    ct_re = jnp.transpose(ssm_c_re[0], (1, 0, 2)).reshape(H, G * P)
    ct_im = jnp.transpose(ssm_c_im[0], (1, 0, 2)).reshape(H, G * P)
    s5_params = (f2r, bt_re, bt_im, ct_re, ct_im, ssm_d, kconst)

    def e1(t, b):
        xhat, _ = _ln_stats(t[0])
        return [xhat * (1.0 + b[0]) + b[1]], []
    (h1,) = _rowwise("ln_mod1", e1, S, ts, [(x2d, D, 0)], [sc1, sh1], [(D, BF16)], [])

    (proj,) = mm_nn("proj", h1, wg_in, F32, 2)
    z, xsb_all, zp = s5_fwd(proj, s5_params, nblk)
    (vt,) = mm_nn("glu", z, wg_vg, BF16, 4)
    pooled = pool_fwd(proj, W, W, gw)

    def pool_epi(vals, ex, outs):
        a = vals[0]
        outs[0][...] = a
        outs[1][...] = (a * ex[0][...]).astype(BF16)
    tmp = _tile(S, 1024)
    yp, ypool = _mm(
        "pool_mix", "nn", pooled, wp_full.astype(BF16), (S // tmp, nwin, 1),
        pl.BlockSpec((tmp, gw), lambda i, j, k: (i, j)), pl.BlockSpec((1, gw, gw), lambda i, j, k: (j, 0, 0)),
        [(_sds((S, W), F32), pl.BlockSpec((tmp, gw), lambda i, j, k: (i, j))),
         (_sds((S, W), BF16), pl.BlockSpec((tmp, gw), lambda i, j, k: (i, j)))],
        (tmp, gw), 1, gw, None, pool_epi,
        [(pool_scale, pl.BlockSpec((1, gw), lambda i, j, k: (0, j)))])
    (y_b,) = mm_nn("pool_out", ypool, wg_po, BF16, 4)

    cb = D // NDEV
    ga_cb, gb_cb = (2 * W) // cb, (2 * W + D) // cb
    mcb = 4
    wm = mcb * cb
    tsm = _tile(S, 256)

    def merge_call(name, fn, ins, n_out, after=()):
        def body(*refs):
            vals = [r[...].astype(F32) for r in refs[:len(ins)]]
            for r, v in zip(refs[len(ins):], fn(*vals)):
                r[...] = v.astype(r.dtype)
        return pl.pallas_call(
            _with_after(body, len(ins), after), name=name, grid=(S // tsm, NDEV // mcb),
            in_specs=[pl.BlockSpec((tsm, w), f) for (_, w, f) in ins] + [ANY] * len(after),
            out_specs=[pl.BlockSpec((tsm, w), lambda i, j: (i, j)) for (_, w) in n_out],
            out_shape=[_sds((S, cols), BF16) for (cols, _) in n_out],
            compiler_params=_params(("parallel", "parallel")),
        )(*[a for (a, _, _) in ins], *after)

    merge_ins = [(proj, wm, lambda i, j: (i, ga_cb // mcb + j)), (proj, wm, lambda i, j: (i, gb_cb // mcb + j)),
                 (vt, 2 * wm, lambda i, j: (i, j)), (y_b, wm, lambda i, j: (i, j))]

    def val_gate(vtv):
        return (jnp.concatenate([vtv[:, 2 * q * cb:(2 * q + 1) * cb] for q in range(mcb)], axis=1),
                jnp.concatenate([vtv[:, (2 * q + 1) * cb:(2 * q + 2) * cb] for q in range(mcb)], axis=1))

    def merge_f(ga, gb, vtv, yb):
        vv, tt = val_gate(vtv)
        return [_sigmoid(ga) * (vv * _sigmoid(tt)) + _sigmoid(gb) * yb]
    (merged,) = merge_call("merge", merge_f, merge_ins, [(D, wm)])

    (mix,) = mm_nn("mix_out", merged, wout_full, F32, 1)

    def e3(t, b):
        xv, mx = t
        g1v, l1g, l1b, sc2v, sh2v = b
        r1 = ALPHA * xv + g1v * mx
        xh1, _ = _ln_stats(r1)
        x1 = xh1 * l1g + l1b
        xh, _ = _ln_stats(x1)
        return [r1, xh * (1.0 + sc2v) + sh2v], []
    r1, h2 = _rowwise("post_mix", e3, S, ts, [(x2d, D, 0), (mix, D, 0)],
                      [g1, ln1_g, ln1_b, sc2, sh2], [(D, F32), (D, BF16)], [])

    def relu_epi(vals, ex, outs):
        outs[0][...] = jnp.maximum(vals[0], 0.0).astype(BF16)
    (rl,) = mm_nn("ff1", h2, wg_ff1, BF16, 1, epi=relu_epi)

    def square(a):
        return a * a
    (y2,) = mm_nn("ff2", rl, wff2_full, F32, 1, pro=square)

    def e4(t, b):
        r1v, y2v, tg = t
        g2v, l1g, l1b, l2g, l2b = b
        xh1, _ = _ln_stats(r1v)
        x1 = xh1 * l1g + l1b
        r2 = ALPHA * x1 + g2v * y2v
        xh2, rs2 = _ln_stats(r2)
        err = xh2 * l2g + l2b - tg
        dx2 = err * (1.0 / D)
        dr2 = _ln_bwd(dx2 * l2g, xh2, rs2)
        lsum = jnp.sum(_colsum(err * err), axis=1, keepdims=True) * (0.5 / D)
        return ([ALPHA * dr2, g2v * dr2],
                [jnp.broadcast_to(lsum, (1, LANES)), _colsum(dx2 * xh2), _colsum(dx2), _colsum(dr2 * y2v)])
    dx1a, dy2, loss_acc, g_ln2g, g_ln2b, d_g2 = _rowwise(
        "head", e4, S, ts, [(r1, D, 0), (y2, D, 0), (tgt, D, 0)], [g2, ln1_g, ln1_b, ln2_g, ln2_b],
        [(D, F32), (D, BF16)], [LANES, D, D, D])

    tn_ff = _tile(4 * D, 1024)

    def dff_epi(vals, ex, outs):
        outs[0][...] = (vals[0] * (2.0 * ex[0][...].astype(F32))).astype(BF16)
    tmf = _tile(S, 1024)
    (da1,) = mm_nt("d_ff2", dy2, wff2_full, BF16, 1, tn=tn_ff, epi=dff_epi,
                   extras=[(rl, pl.BlockSpec((tmf, tn_ff), lambda i, j, k: (i, j)))])
    gw_ff2 = mm_tn("gw_ff2", rl, dy2, BF16, NDEV, 0, pro=square)
    gw_ff1 = mm_tn("gw_ff1", h2, da1, BF16, NDEV, 1)
    tok, wait_pair_a = pair_exchange("pair_exchange_ff", [gw_ff2, gw_ff1], 4)
    (dh2,) = mm_nt("d_ff1", da1, wg_ff1, F32, 4, after=[tok])

    def e5(t, b):
        dh2v, r1v, dx1av, mx = t
        sc2v, l1g, l1b, g1v = b
        xh1, rs1 = _ln_stats(r1v)
        x1 = xh1 * l1g + l1b
        xh, rs = _ln_stats(x1)
        dx1 = dx1av + _ln_bwd(dh2v * (1.0 + sc2v), xh, rs)
        dr1 = _ln_bwd(dx1 * l1g, xh1, rs1)
        return ([ALPHA * dr1, g1v * dr1],
                [_colsum(dh2v * xh), _colsum(dh2v), _colsum(dx1 * xh1), _colsum(dx1), _colsum(dr1 * mx)])
    dxa, dmix, d_sc2, d_sh2, g_ln1g, g_ln1b, d_g1 = _rowwise(
        "post_mix_bwd", e5, S, ts, [(dh2, D, 0), (r1, D, 0), (dx1a, D, 0), (mix, D, 0)],
        [sc2, ln1_g, ln1_b, g1], [(D, F32), (D, BF16)], [D, D, D, D, D])

    (dmerged,) = mm_nt("d_mix_out", dmix, wout_full, BF16, 1)
    gw_out = mm_tn("gw_out", merged, dmix, BF16, NDEV, 0)
    grads_a, got_a = wait_pair_a(gw_out)
    parts_a = [pair_sum("pair_sum_ff%d" % i, g, t, place) for i, (g, t) in enumerate(zip(grads_a, got_a))]
    tok, wait_chip_a = chip_exchange("chip_exchange_ff", parts_a, 5)

    def merge_b(ga, gb, vtv, yb, dm):
        vv, tt = val_gate(vtv)
        sa, sb, st = _sigmoid(ga), _sigmoid(gb), _sigmoid(tt)
        dya = dm * sa
        dv, dt = dya * st, dya * vv * st * (1.0 - st)
        dvt_tile = jnp.concatenate([t[:, q * cb:(q + 1) * cb] for q in range(mcb) for t in (dv, dt)], axis=1)
        return [dm * (vv * st) * sa * (1.0 - sa), dm * yb * sb * (1.0 - sb), dvt_tile, dm * sb]
    dga, dgb_, dvt, dy_b = merge_call(
        "merge_bwd", merge_b, merge_ins + [(dmerged, wm, lambda i, j: (i, j))],
        [(D, wm), (D, wm), (2 * D, 2 * wm), (D, wm)], after=[tok])

    (dypool,) = mm_nt("d_pool_out", dy_b, wg_po, F32, NDEV)
    gw_po = mm_tn("gw_pool_out", ypool, dy_b, BF16, NDEV, 4)

    def e7(t, b):
        return [t[0] * b[0]], [_colsum(t[0] * t[1])]
    dyp, g_pscale = _rowwise("pool_scale_bwd", e7, S, ts, [(dypool, W, 0), (yp, W, 0)],
                             [pool_scale], [(W, BF16)], [W])
    (dpooled,) = _mm(
        "d_pool_mix", "nt", dyp, wp_full.astype(BF16), (S // tmp, nwin, 1),
        pl.BlockSpec((tmp, gw), lambda i, j, k: (i, j)), pl.BlockSpec((1, gw, gw), lambda i, j, k: (j, 0, 0)),
        [(_sds((S, W), F32), pl.BlockSpec((tmp, gw), lambda i, j, k: (i, j)))], (tmp, gw), 1, gw)
    tkp = _tile(S, 2048)
    gw_pool = _mm(
        "gw_pool", "tn", pooled, dyp, (nwin, 1, S // tkp),
        pl.BlockSpec((tkp, gw), lambda i, j, k: (k, i)), pl.BlockSpec((tkp, gw), lambda i, j, k: (k, i)),
        [(_sds((nwin, gw, gw), BF16), pl.BlockSpec((1, gw, gw), lambda i, j, k: (i, 0, 0)))],
        (gw, gw), 1, gw, stacked_out=True)[0]
    du_pool = pool_bwd(dpooled, gw)

    (dz,) = mm_nt("d_glu", dvt, wg_vg, BF16, 2 * NDEV)
    gw_vg = mm_tn("gw_glu", z, dvt, BF16, 2 * NDEV, 4)
    gw_pool_st = jnp.transpose(gw_pool.reshape(nwin, NDEV, gw // NDEV, gw), (1, 0, 2, 3))
    grads_b = [gw_out, gw_po, gw_pool_st, gw_vg.reshape(NDEV, 2, W, D // NDEV)]
    tok, wait_pair_b = pair_exchange("pair_exchange_mix", grads_b, 6)
    du_ssm, g_bt_re, g_bt_im, g_ct_re, g_ct_im, g_f, g_d, g_a = s5_bwd(
        proj, xsb_all, dz, zp, s5_params, nblk, after=[tok])
    grads_b, got_b = wait_pair_b(du_ssm)
    parts_b = [pair_sum("pair_sum_mix%d" % i, g, t, place) for i, (g, t) in enumerate(zip(grads_b, got_b))]
    tok, wait_chip_b = chip_exchange("chip_exchange_mix", parts_b, 7)

    dproj = jnp.concatenate([du_ssm, du_pool, dga, dgb_], axis=1)
    gw_in = mm_tn("gw_in", h1, dproj, BF16, NDEV, 1, after=[tok])
    tok, wait_pair_c = pair_exchange("pair_exchange_in", [gw_in], 8)
    (dh1,) = mm_nt("d_proj", dproj, wg_in, F32, 4, after=[tok])
    grads_c, got_c = wait_pair_c(dh1)
    parts_c = [pair_sum("pair_sum_in", grads_c[0], got_c[0], place)]
    tok, wait_chip_c = chip_exchange("chip_exchange_in", parts_c, 9)

    def e10(t, b):
        dh1v, xv, dxav = t
        xh, rs = _ln_stats(xv)
        return ([dxav + _ln_bwd(dh1v * (1.0 + b[0]), xh, rs)],
                [_colsum(dh1v * xh), _colsum(dh1v)])
    grad_x, d_sc1, d_sh1 = _rowwise("ln_mod1_bwd", e10, S, ts, [(dh1, D, 0), (x2d, D, 0), (dxa, D, 0)],
                                    [sc1], [(D, F32)], [D, D], after=[tok])

    g_b_re = jnp.transpose(g_bt_re.reshape(H, G, P), (1, 0, 2))
    g_b_im = jnp.transpose(g_bt_im.reshape(H, G, P), (1, 0, 2))
    g_c_re = jnp.transpose(g_ct_re.reshape(H, G, P), (1, 0, 2))
    g_c_im = jnp.transpose(g_ct_im.reshape(H, G, P), (1, 0, 2))
    d_ab = jnp.transpose(g_a.reshape(nblk, 2, GPB, P), (1, 0, 2, 3)).reshape(2, G, P)
    g_lr, g_li, g_ldt = s5_disc_bwd(lam_re[0], lam_im[0], log_dt[0].reshape(G, 1), d_ab,
                                    g_f.reshape(2, G, P))

    dmod = jnp.concatenate([d_sh1, d_sc1, d_g1, d_sh2, d_sc2, d_g2], axis=1)
    small_g = [dmod, g_lr, g_li, g_ldt, g_b_re, g_b_im, g_c_re, g_c_im, g_d, g_pscale,
               g_ln1g, g_ln1b, g_ln2g, g_ln2b, loss_acc]
    packed_g = _small_pack(small_g)
    (parts_all,) = seq_all_gather("gather_small", [packed_g], 10)
    glu_w = jnp.stack([w_glu_val[0], w_glu_gate[0]])
    glu_m = jnp.stack([m_w_glu_val[0], m_w_glu_gate[0]])
    glu_v = jnp.stack([v_w_glu_val[0], v_w_glu_gate[0]])
    wmv = [(w_ff2[0], m_w_ff2[0], v_w_ff2[0]), (w_ff1[0], m_w_ff1[0], v_w_ff1[0]),
           (w_out[0], m_w_out[0], v_w_out[0]), (w_pool_out[0], m_w_pool_out[0], v_w_pool_out[0]),
           (w_pool[0], m_w_pool[0], v_w_pool[0]), (glu_w, glu_m, glu_v)]
    _, got3_a = wait_chip_a(packed_g)
    upd = [adamw_sharded("adamw_%d" % i, g, p, t, w, m, v, place)
           for i, (g, p, t, (w, m, v)) in enumerate(zip(grads_a, got_a, got3_a, wmv[:2]))]
    _, got3_b = wait_chip_b(upd[-1][0])
    upd += [adamw_sharded("adamw_%d" % (2 + i), g, p, t, w, m, v, place)
            for i, (g, p, t, (w, m, v)) in enumerate(zip(grads_b, got_b, got3_b, wmv[2:]))]
    u_ff2, u_ff1, u_out, u_po, u_pool, u_glu = upd

    gsum = sum_small(parts_all, after=[upd[-1][0]])
    def swap_b(ts_):
        return [jnp.swapaxes(t, 2, 3) if i in (4, 5) else t for i, t in enumerate(ts_)]

    sg = _small_unpack(gsum, [t.shape for t in swap_b(small_names)] + [(1, LANES)])
    loss, sg = sg[-1][0, 0], sg[:-1]
    sd, sm, sv = adamw_natural(sg, swap_b(small_names), swap_b(small_m), swap_b(small_v))
    sg, sd, sm, sv = swap_b(sg), swap_b(sd), swap_b(sm), swap_b(sv)

    nmod = 6 * D
    dmod_all = parts_all[:, :nmod // LANES, :].reshape(NDEV, nmod)
    c_all_t = jnp.transpose(c_all.reshape(NDEV, D))
    ada_out = adamw_ada(c_all_t, dmod_all, w_ada[0], m_w_ada[0], v_w_ada[0], my_dev)
    _, got3_c = wait_chip_c(ada_out[0])
    u_in = adamw_sharded("adamw_6", grads_c[0], got_c[0], got3_c[0], w_in[0], m_w_in[0], v_w_in[0], place)

    def pick(k):
        return [ada_out[k][None], sg_sd[k][0], u_in[k][None]] + [t for t in sg_sd[k][1:9]] + \
               [u_glu[k][0][None], u_glu[k][1][None], u_pool[k][None], sg_sd[k][9], u_po[k][None],
                u_out[k][None], sg_sd[k][10], sg_sd[k][11], u_ff1[k][None], u_ff2[k][None],
                sg_sd[k][12], sg_sd[k][13]]

    sg_sd = [sg, sd, sm, sv]
    return (loss, grad_x[None], *pick(0), *pick(1), *pick(2), *pick(3))
```

```python
import functools
import math

import jax
import jax.numpy as jnp
from jax import lax
from jax.experimental import pallas as pl
from jax.experimental.pallas import tpu as pltpu
from jax.experimental.pallas import tpu_sc as plsc

F32 = jnp.float32
BF16 = jnp.bfloat16
MESH = pl.DeviceIdType.MESH
NDEV = 8
NCHIP = 4

SSM_GROUP = 16
SSM_STATE = 64
GROUPS_PER_BLOCK = 8
POOL_WINDOWS = (2, 4, 8, 16)
LN_EPS = 1e-5
ALPHA = 2.0 ** 0.25
ADAM_LR, ADAM_B1, ADAM_B2, ADAM_EPS, ADAM_WD, ADAM_STEP = 0.001, 0.9, 0.999, 1e-08, 0.01, 10
SUBLANES = 8
LANES = 128
VMEM_LIMIT = 56 * 1024 * 1024


def _params(sem=None, vmem=VMEM_LIMIT):
    return pltpu.CompilerParams(dimension_semantics=sem, vmem_limit_bytes=vmem)


def _tile(n, pref):
    if n <= pref:
        return n
    t = 1 << (pref.bit_length() - 1)
    while n % t:
        t //= 2
    return t


def _cast_epi(vals, ex, outs):
    c = vals[0].shape[1]
    for s, v in enumerate(vals):
        outs[0][:, s * c:(s + 1) * c] = v.astype(outs[0].dtype)


ANY = pl.BlockSpec(memory_space=pl.ANY)


def _with_after(body, n_in, after):
    if not after:
        return body
    n_af = len(after)

    def wrapped(*refs):
        return body(*refs[:n_in], *refs[n_in + n_af:])
    return wrapped


def _mm(name, kind, a, b, grid, a_spec, b_spec, outs, acc_shape, nsub=1, c=None,
        pro=None, epi=None, extras=(), stacked_out=False, after=()):
    nk = grid[2]
    n_ex, n_out = len(extras), len(outs)

    def finish(vals, ex, out_refs):
        if epi is not None:
            epi(vals, ex, out_refs)
        elif stacked_out:
            for s, v in enumerate(vals):
                out_refs[0][s] = v.astype(out_refs[0].dtype)
        else:
            _cast_epi(vals, ex, out_refs)

    def body(*refs):
        mm_step(refs[0], refs[1], refs[2:2 + n_ex], refs[2 + n_ex:2 + n_ex + n_out], refs[-1])

    def mm_step(a_ref, b_ref, ex, out_refs, acc):
        k = pl.program_id(2)
        av = a_ref[...]
        if pro is not None:
            av = pro(av)
        if kind == "nn":
            prods = [jnp.dot(av, b_ref[s], preferred_element_type=F32) for s in range(nsub)]
        elif kind == "nt":
            t = None
            for s in range(nsub):
                d = lax.dot_general(av[:, s * c:(s + 1) * c], b_ref[s], (((1,), (1,)), ((), ())),
                                    preferred_element_type=F32)
                t = d if t is None else t + d
            prods = [t]
        else:
            t = lax.dot_general(av, b_ref[...], (((0,), (0,)), ((), ())), preferred_element_type=F32)
            prods = [t[:, s * c:(s + 1) * c] for s in range(nsub)] if stacked_out else [t]
        if nk == 1:
            finish(prods, ex, out_refs)
            return
        w = prods[0].shape[1]

        @pl.when(k == 0)
        def _():
            for s, p in enumerate(prods):
                acc[:, s * w:(s + 1) * w] = p

        @pl.when(jnp.logical_and(k > 0, k < nk - 1))
        def _():
            for s, p in enumerate(prods):
                acc[:, s * w:(s + 1) * w] += p

        @pl.when(k == nk - 1)
        def _():
            finish([acc[:, s * w:(s + 1) * w] + p for s, p in enumerate(prods)], ex, out_refs)

    return pl.pallas_call(
        _with_after(body, 2 + n_ex, after), name=name, grid=grid,
        in_specs=[a_spec, b_spec] + [e[1] for e in extras] + [ANY] * len(after),
        out_specs=[o[1] for o in outs],
        out_shape=[o[0] for o in outs],
        scratch_shapes=[pltpu.VMEM(acc_shape, F32)] if nk > 1 else [],
        compiler_params=_params(("parallel", "parallel", "arbitrary")),
    )(a, b, *[e[0] for e in extras], *after)


def _sds(shape, dtype):
    return jax.ShapeDtypeStruct(shape, dtype)


def mm_nn(name, a, b3, out_dtype, nsub, tm=1024, tk=2048, tn=None, pro=None, epi=None,
          extras=(), after=()):
    M = a.shape[0]
    nb, K, cdim = b3.shape
    tm, tk = _tile(M, tm), _tile(K, tk)
    if nb == 1:
        tn = _tile(cdim, tn or 1024)
        nsub, c, nj = 1, tn, cdim // tn
        b_spec = pl.BlockSpec((1, tk, tn), lambda i, j, k: (0, k, j))
        N = cdim
    else:
        c, nj, tn = cdim, nb // nsub, nsub * cdim
        b_spec = pl.BlockSpec((nsub, tk, cdim), lambda i, j, k: (j, k, 0))
        N = nb * cdim
    a_spec = pl.BlockSpec((tm, tk), lambda i, j, k: (i, k))
    grid = (M // tm, nj, K // tk)
    outs = [(_sds((M, N), out_dtype), pl.BlockSpec((tm, tn), lambda i, j, k: (i, j)))]
    return _mm(name, "nn", a, b3, grid, a_spec, b_spec, outs, (tm, tn), nsub, c, pro, epi, extras,
               after=after)


def mm_nt(name, a, b3, out_dtype, nsub, tm=1024, tn=1024, epi=None, extras=(), after=()):
    M = a.shape[0]
    nb, N, cdim = b3.shape
    tm, tn = _tile(M, tm), _tile(N, tn)
    if nb == 1:
        tk = _tile(cdim, 2048)
        nsub, c, nk = 1, tk, cdim // tk
        b_spec = pl.BlockSpec((1, tn, tk), lambda i, j, k: (0, j, k))
    else:
        c, nk, tk = cdim, nb // nsub, nsub * cdim
        b_spec = pl.BlockSpec((nsub, tn, cdim), lambda i, j, k: (k, j, 0))
    a_spec = pl.BlockSpec((tm, tk), lambda i, j, k: (i, k))
    grid = (M // tm, N // tn, nk)
    outs = [(_sds((M, N), out_dtype), pl.BlockSpec((tm, tn), lambda i, j, k: (i, j)))]
    return _mm(name, "nt", a, b3, grid, a_spec, b_spec, outs, (tm, tn), nsub, c, None, epi, extras,
               after=after)


def mm_tn(name, a, b, out_dtype, nb, nsub, tma=1024, tk=2048, pro=None, after=()):
    S, Ka = a.shape
    N = b.shape[1]
    tk, tma = _tile(S, tk), _tile(Ka, tma)
    a_spec = pl.BlockSpec((tk, tma), lambda i, j, k: (k, i))
    if nsub == 0:
        tn = _tile(N, 1024)
        res = _mm(name, "tn", a, b, (Ka // tma, N // tn, S // tk), a_spec,
                  pl.BlockSpec((tk, tn), lambda i, j, k: (k, j)),
                  [(_sds((Ka, N), out_dtype), pl.BlockSpec((tma, tn), lambda i, j, k: (i, j)))],
                  (tma, tn), 1, tn, pro, None, (), after=after)[0]
        return res.reshape(nb, Ka // nb, N)
    c = N // nb
    tn = nsub * c
    outs = [(_sds((nb, Ka, c), out_dtype), pl.BlockSpec((nsub, tma, c), lambda i, j, k: (j, i, 0)))]
    return _mm(name, "tn", a, b, (Ka // tma, nb // nsub, S // tk), a_spec,
               pl.BlockSpec((tk, tn), lambda i, j, k: (k, j)), outs, (tma, tn), nsub, c,
               pro, None, (), stacked_out=True, after=after)[0]


def _rowwise(name, fn, S, ts, tiled, bcast, tiled_out, acc_out, after=()):
    nt, nb, no, na = len(tiled), len(bcast), len(tiled_out), len(acc_out)

    def body(*refs):
        tin = [r[...] for r in refs[:nt]]
        bin_ = [r[...] for r in refs[nt:nt + nb]]
        o_refs = refs[nt + nb:nt + nb + no]
        a_refs = refs[nt + nb + no:]
        touts, aouts = fn(tin, bin_)
        for r, v in zip(o_refs, touts):
            r[...] = v.astype(r.dtype)
        i = pl.program_id(0)

        @pl.when(i == 0)
        def _():
            for r, v in zip(a_refs, aouts):
                r[...] = v

        @pl.when(i > 0)
        def _():
            for r, v in zip(a_refs, aouts):
                r[...] += v

    in_specs = [pl.BlockSpec((ts, w), functools.partial(lambda i, cb: (i, cb), cb=cb))
                for (_, w, cb) in tiled]
    in_specs += [pl.BlockSpec(b.shape, lambda i: (0, 0)) for b in bcast]
    out_specs = [pl.BlockSpec((ts, w), lambda i: (i, 0)) for (w, _) in tiled_out]
    out_specs += [pl.BlockSpec((1, w), lambda i: (0, 0)) for w in acc_out]
    out_shape = [_sds((S, w), d) for (w, d) in tiled_out] + [_sds((1, w), F32) for w in acc_out]
    return pl.pallas_call(
        _with_after(body, nt + nb, after), name=name, grid=(S // ts,),
        in_specs=in_specs + [ANY] * len(after), out_specs=out_specs,
        out_shape=out_shape, compiler_params=_params(("arbitrary",)),
    )(*[t[0] for t in tiled], *bcast, *after)


def _ln_stats(v):
    mu = jnp.mean(v, axis=-1, keepdims=True)
    vc = v - mu
    var = jnp.mean(vc * vc, axis=-1, keepdims=True)
    rstd = lax.rsqrt(var + LN_EPS)
    return vc * rstd, rstd


def _ln_bwd(dxhat, xhat, rstd):
    return rstd * (dxhat - jnp.mean(dxhat, axis=-1, keepdims=True)
                   - xhat * jnp.mean(dxhat * xhat, axis=-1, keepdims=True))


def _colsum(v):
    return jnp.sum(v, axis=0, keepdims=True)


def _sigmoid(v):
    return 1.0 / (1.0 + jnp.exp(-v))


_GELU_C = math.sqrt(2.0 / math.pi)


def _gelu(v):
    return 0.5 * v * (1.0 + jnp.tanh(_GELU_C * (v + 0.044715 * v * v * v)))


def _gelu_grad(v):
    t = jnp.tanh(_GELU_C * (v + 0.044715 * v * v * v))
    return 0.5 * (1.0 + t) + 0.5 * v * (1.0 - t * t) * _GELU_C * (1.0 + 3 * 0.044715 * v * v)


def _disc(lr, li, ldt):
    dt = jnp.exp(ldt)
    mag = jnp.exp(lr * dt)
    ang = li * dt
    ab_re = mag * jnp.cos(ang)
    ab_im = mag * jnp.sin(ang)
    num_re = ab_re - 1.0
    num_im = ab_im
    den = lr * lr + li * li
    f_re = (num_re * lr + num_im * li) / den
    f_im = (num_im * lr - num_re * li) / den
    return ab_re, ab_im, f_re, f_im


def _cmul(ar, ai, br, bi):
    return ar * br - ai * bi, ar * bi + ai * br


SCAN_FOLD = 4
NCONST = 18


def s5_disc(lam_re, lam_im, log_dt):
    G, P = lam_re.shape

    def body(lr_ref, li_ref, ldt_ref, f_ref, k_ref):
        ab_re, ab_im, f_re, f_im = _disc(lr_ref[...], li_ref[...], ldt_ref[...])
        f_ref[0] = f_re
        f_ref[1] = f_im
        fr, fi = ab_re, ab_im
        for _ in range(SCAN_FOLD - 1):
            fr, fi = _cmul(fr, fi, ab_re, ab_im)
        pr, pi = [fr], [fi]
        for _ in range(SUBLANES - 1):
            nr, ni = _cmul(pr[-1], pi[-1], fr, fi)
            pr.append(nr)
            pi.append(ni)
        zero = jnp.zeros_like(ab_re)
        for r in range(SUBLANES):
            k_ref[16, r] = ab_re
            k_ref[17, r] = ab_im
        for n, sh in enumerate((1, 2, 4)):
            for r in range(SUBLANES):
                k_ref[2 * n, r] = pr[sh - 1] if r >= sh else zero
                k_ref[2 * n + 1, r] = pi[sh - 1] if r >= sh else zero
                k_ref[8 + 2 * n, r] = pr[sh - 1] if r + sh < SUBLANES else zero
                k_ref[8 + 2 * n + 1, r] = -pi[sh - 1] if r + sh < SUBLANES else zero
        for r in range(SUBLANES):
            k_ref[6, r] = pr[r]
            k_ref[7, r] = pi[r]
            k_ref[14, r] = pr[SUBLANES - 1 - r]
            k_ref[15, r] = -pi[SUBLANES - 1 - r]

    vm = pl.BlockSpec(memory_space=pltpu.VMEM)
    return pl.pallas_call(
        body, name="s5_disc", in_specs=[vm, vm, vm], out_specs=[vm, vm],
        out_shape=[_sds((2, G, P), F32), _sds((NCONST, SUBLANES, G, P), F32)],
    )(lam_re, lam_im, log_dt)


def s5_disc_bwd(lam_re, lam_im, log_dt, d_ab, d_f):
    G, P = lam_re.shape

    def body(lr_ref, li_ref, ldt_ref, dab_ref, df_ref, glr_ref, gli_ref, gdt_ref):
        _, vjp = jax.vjp(_disc, lr_ref[...], li_ref[...], ldt_ref[...])
        glr, gli, gdt = vjp((dab_ref[0], dab_ref[1], df_ref[0], df_ref[1]))
        glr_ref[...] = glr
        gli_ref[...] = gli
        gdt_ref[...] = gdt

    vm = pl.BlockSpec(memory_space=pltpu.VMEM)
    return pl.pallas_call(
        body, name="s5_disc_bwd", in_specs=[vm] * 5, out_specs=[vm] * 3,
        out_shape=[_sds((G, P), F32), _sds((G, P), F32), _sds((G, 1), F32)],
    )(lam_re, lam_im, log_dt, d_ab, d_f)


def _group_mask(cw, nst):
    row = lax.broadcasted_iota(jnp.int32, (cw, 2 * nst), 0) // SSM_GROUP
    col = (lax.broadcasted_iota(jnp.int32, (cw, 2 * nst), 1) % nst) // SSM_STATE
    return row == col


def _spread(t, mask):
    reps = mask.shape[0] // t.shape[0]
    return jnp.where(mask, jnp.tile(t, (reps, 1)), 0.0).astype(BF16)


def _gather_groups(t, mask):
    t = jnp.where(mask, t, 0.0)
    out = t[0:SSM_GROUP]
    for g in range(1, t.shape[0] // SSM_GROUP):
        out = out + t[g * SSM_GROUP:(g + 1) * SSM_GROUP]
    return out


def _s5_operands(f_ref, br_ref, bi_ref, cr_ref, ci_ref, mask):
    fr, fi = f_ref[0], f_ref[1]
    br, bi = br_ref[...], bi_ref[...]
    bm = _spread(jnp.concatenate([fr * br - fi * bi, fr * bi + fi * br], axis=1), mask)
    cm = _spread(jnp.concatenate([cr_ref[...], -ci_ref[...]], axis=1), mask)
    return bm, cm


def _planes_put(ref, val):
    for c in range(ref.shape[0]):
        ref[c] = val[:, c * LANES:(c + 1) * LANES]


def _planes_get(ref):
    return jnp.concatenate([ref[c] for c in range(ref.shape[0])], axis=1)


def _rows_ld(ref, start, lo, hi):
    rows = pl.ds(start, SUBLANES, stride=SCAN_FOLD)
    return jnp.concatenate([ref[c, rows, :] for c in range(lo // LANES, hi // LANES)], axis=1)


def _rows_st(ref, start, lo, val):
    rows = pl.ds(start, SUBLANES, stride=SCAN_FOLD)
    for k in range(val.shape[1] // LANES):
        ref[lo // LANES + k, rows, :] = val[:, k * LANES:(k + 1) * LANES]


def _phases(ref, base, lo, hi):
    return [_rows_ld(ref, base + j, lo, hi) for j in range(SCAN_FOLD)]


def _row_bcast(v, r):
    return jnp.broadcast_to(v[r:r + 1, :], v.shape)


def _scan_fwd(xs, k_ref, nst):
    m = SCAN_FOLD
    ngroup = xs.shape[1] // (SUBLANES * m)
    row = lax.broadcasted_iota(jnp.int32, (SUBLANES, nst), 0)

    def step(t, carry):
        cr, ci = carry
        base = pl.multiple_of(t * (SUBLANES * m), SUBLANES * m)
        ar, ai = k_ref[16], k_ref[17]
        pr, pi = _phases(xs, base, 0, nst), _phases(xs, base, nst, 2 * nst)
        vr, vi = pr[0], pi[0]
        for j in range(1, m):
            vr, vi = pr[j] + ar * vr - ai * vi, pi[j] + ar * vi + ai * vr
        for n, sh in enumerate((1, 2, 4)):
            sr = pltpu.roll(vr, sh, 0)
            si = pltpu.roll(vi, sh, 0)
            mr, mi = k_ref[2 * n], k_ref[2 * n + 1]
            vr, vi = vr + mr * sr - mi * si, vi + mr * si + mi * sr
        qr, qi = k_ref[6], k_ref[7]
        vr, vi = vr + qr * cr - qi * ci, vi + qr * ci + qi * cr
        _rows_st(xs, base + m - 1, 0, vr)
        _rows_st(xs, base + m - 1, nst, vi)
        xr = jnp.where(row == 0, cr, pltpu.roll(vr, 1, 0))
        xi = jnp.where(row == 0, ci, pltpu.roll(vi, 1, 0))
        for j in range(m - 1):
            xr, xi = pr[j] + ar * xr - ai * xi, pi[j] + ar * xi + ai * xr
            _rows_st(xs, base + j, 0, xr)
            _rows_st(xs, base + j, nst, xi)
        return _row_bcast(vr, SUBLANES - 1), _row_bcast(vi, SUBLANES - 1)

    zero = jnp.zeros((SUBLANES, nst), F32)
    lax.fori_loop(0, ngroup, step, (zero, zero))


def _scan_bwd(g, xs, k_ref, nst):
    m = SCAN_FOLD
    ngroup = g.shape[1] // (SUBLANES * m)
    row = lax.broadcasted_iota(jnp.int32, (SUBLANES, nst), 0)

    def step(tt, carry):
        cr, ci, dar, dai = carry
        t = ngroup - 1 - tt
        base = pl.multiple_of(t * (SUBLANES * m), SUBLANES * m)
        ar, ai = k_ref[16], -k_ref[17]
        dr, di = _phases(g, base, 0, nst), _phases(g, base, nst, 2 * nst)
        wr, wi = dr[m - 1], di[m - 1]
        for j in range(m - 2, -1, -1):
            wr, wi = dr[j] + ar * wr - ai * wi, di[j] + ar * wi + ai * wr
        for n, sh in enumerate((1, 2, 4)):
            sr = pltpu.roll(wr, SUBLANES - sh, 0)
            si = pltpu.roll(wi, SUBLANES - sh, 0)
            mr, mi = k_ref[8 + 2 * n], k_ref[8 + 2 * n + 1]
            wr, wi = wr + mr * sr - mi * si, wi + mr * si + mi * sr
        qr, qi = k_ref[14], k_ref[15]
        wr, wi = wr + qr * cr - qi * ci, wi + qr * ci + qi * cr
        gr, gi = [None] * m, [None] * m
        gr[0], gi[0] = wr, wi
        nr = jnp.where(row == SUBLANES - 1, cr, pltpu.roll(wr, SUBLANES - 1, 0))
        ni = jnp.where(row == SUBLANES - 1, ci, pltpu.roll(wi, SUBLANES - 1, 0))
        for j in range(m - 1, 0, -1):
            nr, ni = dr[j] + ar * nr - ai * ni, di[j] + ar * ni + ai * nr
            gr[j], gi[j] = nr, ni
        for j in range(m):
            _rows_st(g, base + j, 0, gr[j])
            _rows_st(g, base + j, nst, gi[j])
        xr, xi = _phases(xs, base, 0, nst), _phases(xs, base, nst, 2 * nst)
        pbase = pl.multiple_of(jnp.maximum(t - 1, 0) * (SUBLANES * m), SUBLANES * m)
        live = (t > 0).astype(F32)
        lr = _row_bcast(_rows_ld(xs, pbase + m - 1, 0, nst), SUBLANES - 1) * live
        li = _row_bcast(_rows_ld(xs, pbase + m - 1, nst, 2 * nst), SUBLANES - 1) * live
        xmr = [jnp.where(row == 0, lr, pltpu.roll(xr[m - 1], 1, 0))] + xr[:m - 1]
        xmi = [jnp.where(row == 0, li, pltpu.roll(xi[m - 1], 1, 0))] + xi[:m - 1]
        for j in range(m):
            dar = dar + gr[j] * xmr[j] + gi[j] * xmi[j]
            dai = dai + gi[j] * xmr[j] - gr[j] * xmi[j]
        return _row_bcast(wr, 0), _row_bcast(wi, 0), dar, dai

    zero = jnp.zeros((SUBLANES, nst), F32)
    _, _, dar, dai = lax.fori_loop(0, ngroup, step, (zero, zero, zero, zero))
    return _colsum(dar), _colsum(dai)


def _s5_param_specs(cw, nst):
    hp = pl.BlockSpec((SSM_GROUP, nst), lambda b: (0, b))
    return [pl.BlockSpec((2, 1, nst), lambda b: (0, 0, b)), hp, hp, hp, hp,
            pl.BlockSpec((1, cw), lambda b: (0, b)),
            pl.BlockSpec((NCONST, SUBLANES, nst), lambda b: (0, 0, b))]


def s5_fwd(proj, params, nb):
    S = proj.shape[0]
    nst = params[1].shape[1] // nb
    cw = nst // SSM_STATE * SSM_GROUP

    def body(u_ref, f_ref, br_ref, bi_ref, cr_ref, ci_ref, d_ref, k_ref, z_ref, xsb_ref, zp_ref, xs):
        bm, cm = _s5_operands(f_ref, br_ref, bi_ref, cr_ref, ci_ref, _group_mask(cw, nst))
        u = u_ref[...]
        _planes_put(xs, jnp.dot(u.astype(BF16), bm, preferred_element_type=F32))
        _scan_fwd(xs, k_ref, nst)
        xsb = _planes_get(xs).astype(BF16)
        xsb_ref[...] = xsb
        y = lax.dot_general(xsb, cm, (((1,), (1,)), ((), ())), preferred_element_type=F32)
        y = y + d_ref[...] * u
        z_ref[...] = _gelu(y).astype(BF16)
        zp_ref[...] = _gelu_grad(y).astype(BF16)

    return pl.pallas_call(
        body, name="s5_fwd", grid=(nb,),
        in_specs=[pl.BlockSpec((S, cw), lambda b: (0, b))] + _s5_param_specs(cw, nst),
        out_specs=[pl.BlockSpec((S, cw), lambda b: (0, b)), pl.BlockSpec((S, 2 * nst), lambda b: (0, b)),
                   pl.BlockSpec((S, cw), lambda b: (0, b))],
        out_shape=[_sds((S, nb * cw), BF16), _sds((S, nb * 2 * nst), BF16), _sds((S, nb * cw), BF16)],
        scratch_shapes=[pltpu.VMEM((2 * nst // LANES, S, LANES), F32)],
        compiler_params=_params(("arbitrary",)),
    )(proj, *params)


def s5_bwd(proj, xsb_all, dz, zp, params, nb, after=()):
    S = proj.shape[0]
    nst = params[1].shape[1] // nb
    cw = nst // SSM_STATE * SSM_GROUP

    def body(u_ref, xsb_ref, dz_ref, zp_ref, f_ref, br_ref, bi_ref, cr_ref, ci_ref, d_ref, k_ref,
             du_ref, gbr_ref, gbi_ref, gcr_ref, gci_ref, gf_ref, gd_ref, ga_ref, xs, g):
        mask = _group_mask(cw, nst)
        bm, cm = _s5_operands(f_ref, br_ref, bi_ref, cr_ref, ci_ref, mask)
        u = u_ref[...]
        ub = u.astype(BF16)
        d = d_ref[...]
        xsb = xsb_ref[...]
        _planes_put(xs, xsb.astype(F32))
        dy = dz_ref[...].astype(F32) * zp_ref[...].astype(F32)
        gd_ref[...] = _colsum(dy * u)
        dyb = dy.astype(BF16)
        gc = _gather_groups(lax.dot_general(dyb, xsb, (((0,), (0,)), ((), ())),
                                            preferred_element_type=F32), mask)
        gcr_ref[...] = gc[:, :nst]
        gci_ref[...] = -gc[:, nst:]
        _planes_put(g, jnp.dot(dyb, cm, preferred_element_type=F32))
        ar, ai = _scan_bwd(g, xs, k_ref, nst)
        ga_ref[0, 0:1, :] = ar
        ga_ref[0, 1:2, :] = ai
        gb = _planes_get(g).astype(BF16)
        du = lax.dot_general(gb, bm, (((1,), (1,)), ((), ())), preferred_element_type=F32) + d * dy
        du_ref[...] = du.astype(BF16)
        gbb = _gather_groups(lax.dot_general(ub, gb, (((0,), (0,)), ((), ())),
                                             preferred_element_type=F32), mask)
        dr, di = gbb[:, :nst], gbb[:, nst:]
        fr, fi = f_ref[0], f_ref[1]
        br, bi = br_ref[...], bi_ref[...]
        gbr_ref[...] = fr * dr + fi * di
        gbi_ref[...] = fr * di - fi * dr
        gf_ref[0] = _colsum(dr * br + di * bi)
        gf_ref[1] = _colsum(di * br - dr * bi)

    hp = pl.BlockSpec((SSM_GROUP, nst), lambda b: (0, b))
    hp_sds = _sds((SSM_GROUP, nb * nst), F32)
    return pl.pallas_call(
        _with_after(body, 11, after), name="s5_bwd", grid=(nb,),
        in_specs=[pl.BlockSpec((S, cw), lambda b: (0, b)),
                  pl.BlockSpec((S, 2 * nst), lambda b: (0, b)),
                  pl.BlockSpec((S, cw), lambda b: (0, b)),
                  pl.BlockSpec((S, cw), lambda b: (0, b))] + _s5_param_specs(cw, nst)
        + [ANY] * len(after),
        out_specs=[pl.BlockSpec((S, cw), lambda b: (0, b)), hp, hp, hp, hp,
                   pl.BlockSpec((2, 1, nst), lambda b: (0, 0, b)),
                   pl.BlockSpec((1, cw), lambda b: (0, b)),
                   pl.BlockSpec((1, 2, nst), lambda b: (b, 0, 0))],
        out_shape=[_sds((S, nb * cw), BF16), hp_sds, hp_sds, hp_sds, hp_sds,
                   _sds((2, 1, nb * nst), F32), _sds((1, nb * cw), F32), _sds((nb, 2, nst), F32)],
        scratch_shapes=[pltpu.VMEM((2 * nst // LANES, S, LANES), F32)] * 2,
        compiler_params=_params(("arbitrary",)),
    )(proj, xsb_all, dz, zp, *params, *after)


def _shift_rows(v, k, row, down):
    n = v.shape[0]
    if down:
        return jnp.where(row >= k, pltpu.roll(v, k, 0), 0.0)
    return jnp.where(row < n - k, pltpu.roll(v, n - k, 0), 0.0)


def _window(v, gi, row, down):
    sums = []
    s = v
    for k in (1, 2, 4, 8):
        s = s + _shift_rows(s, k, row, down)
        sums.append(s)
    out = sums[3]
    for n in (2, 1, 0):
        out = jnp.where(gi == n, sums[n], out)
    return out


def pool_fwd(proj, col0, width, gw):
    S = proj.shape[0]
    cb0 = col0 // gw

    def body(u_ref, o_ref):
        gi = pl.program_id(0)
        u = u_ref[...]
        row = lax.broadcasted_iota(jnp.int32, u.shape, 0)
        w = jnp.left_shift(2, gi)
        count = jnp.minimum(row + 1, w).astype(F32)
        o_ref[...] = (_window(u, gi, row, True) / count - u).astype(BF16)

    return pl.pallas_call(
        body, name="pool_fwd", grid=(len(POOL_WINDOWS),),
        in_specs=[pl.BlockSpec((S, gw), lambda g: (0, cb0 + g))],
        out_specs=pl.BlockSpec((S, gw), lambda g: (0, g)),
        out_shape=_sds((S, width), BF16), compiler_params=_params(("arbitrary",)),
    )(proj)


def pool_bwd(dpooled, gw):
    S, width = dpooled.shape

    def body(d_ref, o_ref):
        gi = pl.program_id(0)
        d = d_ref[...]
        row = lax.broadcasted_iota(jnp.int32, d.shape, 0)
        w = jnp.left_shift(2, gi)
        count = jnp.minimum(row + 1, w).astype(F32)
        o_ref[...] = (_window(d / count, gi, row, False) - d).astype(BF16)

    return pl.pallas_call(
        body, name="pool_bwd", grid=(len(POOL_WINDOWS),),
        in_specs=[pl.BlockSpec((S, gw), lambda g: (0, g))],
        out_specs=pl.BlockSpec((S, gw), lambda g: (0, g)),
        out_shape=_sds((S, width), BF16), compiler_params=_params(("arbitrary",)),
    )(dpooled)


def _place():
    x, y, c = lax.axis_index("x"), lax.axis_index("y"), lax.axis_index("c")
    chips = [(1 - x, y), (x, 1 - y), (1 - x, 1 - y)]
    return x, y, c, chips


HBM = pl.BlockSpec(memory_space=pltpu.HBM)


GATHER_PIECES = 2
GATHER_SEMS = 1 + 12 * GATHER_PIECES


def _routed_gather_body(n):
    npc = GATHER_PIECES
    k_x, k_y = 1, 1 + 2 * npc
    k_xy, k_yx, k_sib = 1 + 4 * npc, 1 + 5 * npc, 1 + 6 * npc

    def body(*refs):
        ins, outs = refs[:n], refs[n:2 * n]
        send_sems, recv_sems, local_sems = refs[2 * n:]
        x, y, c, (xn, yn, dg) = _place()
        me, sibling = (x, y, c), (x, y, 1 - c)
        barrier = pltpu.get_barrier_semaphore()
        for peer in (sibling, (*xn, c), (*yn, c)):
            pl.semaphore_signal(barrier, inc=1, device_id=peer, device_id_type=MESH)
        pl.semaphore_wait(barrier, 3)

        def piece(i, dev, p):
            rows = ins[i].shape[0] // (2 * npc)
            return outs[i].at[4 * dev[0] + 2 * dev[1] + dev[2], pl.ds(p * rows, rows)]

        def copy(i, k, src, dst, to):
            return pltpu.make_async_remote_copy(src_ref=src, dst_ref=dst, send_sem=send_sems.at[i, k],
                                                recv_sem=recv_sems.at[i, k], device_id=to,
                                                device_id_type=MESH)

        started = []

        def go(cp):
            cp.start()
            started.append(cp)

        for i in range(n):
            rows = ins[i].shape[0] // (2 * npc)
            for q in range(2 * npc):
                py = (q + npc) % (2 * npc)
                go(copy(i, k_x + q, ins[i].at[pl.ds(q * rows, rows)], piece(i, me, q), (*xn, c)))
                go(copy(i, k_y + py, ins[i].at[pl.ds(py * rows, rows)], piece(i, me, py), (*yn, c)))
        for i in range(n):
            go(copy(i, 0, ins[i], outs[i].at[4 * x + 2 * y + c], sibling))
        mine = [pltpu.make_async_copy(ins[i], outs[i].at[4 * x + 2 * y + c], local_sems.at[i])
                for i in range(n)]
        for cp in mine:
            cp.start()

        def arrived(i, k, chip, p, onward, r):
            got = piece(i, (*chip, c), p)
            copy(i, k, got, got, me).wait_recv()
            if onward is not None:
                go(copy(i, onward[0], got, got, (*onward[1], c)))
            go(copy(i, k_sib + r, got, got, sibling))

        for i in range(n):
            for q in range(npc):
                arrived(i, k_x + q, xn, q, (k_xy + q, yn), q)
                arrived(i, k_y + npc + q, yn, npc + q, (k_yx + q, xn), 2 * npc + npc + q)
            for q in range(npc):
                arrived(i, k_x + npc + q, xn, npc + q, None, npc + q)
                arrived(i, k_y + q, yn, q, None, 2 * npc + q)
            for q in range(npc):
                arrived(i, k_xy + q, dg, q, None, 4 * npc + q)
                arrived(i, k_yx + q, dg, npc + q, None, 4 * npc + npc + q)
        for i in range(n):
            block = outs[i].at[4 * x + 2 * y + 1 - c]
            copy(i, 0, block, block, me).wait_recv()
            for j, chip in enumerate((xn, yn, dg)):
                for p in range(2 * npc):
                    got = piece(i, (*chip, 1 - c), p)
                    copy(i, k_sib + 2 * npc * j + p, got, got, me).wait_recv()
        for cp in started:
            cp.wait_send()
        for cp in mine:
            cp.wait()

    return body


def _on_sequencer(name, body, arrays, out_sds, sems, collective_id):
    ins = [jax.new_ref(a, memory_space=pltpu.MemorySpace.HBM) for a in arrays]
    outs = [jax.empty_ref(s, memory_space=pltpu.MemorySpace.HBM) for s in out_sds]

    @pl.kernel(mesh=plsc.ScalarSubcoreMesh(axis_name="sequencer", num_cores=1), name=name,
               scratch_types=tuple(sems),
               compiler_params=pltpu.CompilerParams(collective_id=collective_id))
    def launch(*sem_refs):
        body(*ins, *outs, *sem_refs)

    launch()
    return [o[...] for o in outs]


def seq_all_gather(name, shards, collective_id):
    n = len(shards)
    return _on_sequencer(
        name, _routed_gather_body(n), shards, [_sds((NDEV,) + s.shape, s.dtype) for s in shards],
        [pltpu.SemaphoreType.DMA((n, GATHER_SEMS)), pltpu.SemaphoreType.DMA((n, GATHER_SEMS)),
         pltpu.SemaphoreType.DMA((n,))], collective_id)


def pair_exchange(name, grads, collective_id):
    def plan(srcs, lands):
        x, y, c, _ = _place()
        return ([(i, q, srcs[i].at[2 * q + 1 - c], lands[i].at[q], (x, y, 1 - c))
                 for i in range(len(srcs)) for q in range(NCHIP)], [(x, y, 1 - c)])

    return _split_exchange(name, grads, [_sds((NCHIP,) + g.shape[1:], g.dtype) for g in grads],
                           plan, NCHIP, collective_id)


SEM = pl.BlockSpec(memory_space=pltpu.SEMAPHORE)


def _split_exchange(name, srcs, land_sds, plan, ncopy, collective_id):
    n = len(srcs)
    nsem = n * ncopy
    effect = pltpu.SideEffectType.DATAFLOW_SIDE_EFFECTING

    def descriptors(src_refs, land_refs, send_sems, recv_sems):
        copies, peers = plan(src_refs, land_refs)
        return [pltpu.make_async_remote_copy(src_ref=s, dst_ref=d, send_sem=send_sems[i * ncopy + k],
                                             recv_sem=recv_sems[i * ncopy + k], device_id=to,
                                             device_id_type=MESH) for (i, k, s, d, to) in copies], peers

    def start_body(*refs):
        src_refs, land_refs = refs[:n], refs[n:2 * n]
        send_sems, recv_sems = refs[2 * n:2 * n + nsem], refs[2 * n + nsem:2 * n + 2 * nsem]
        token = refs[-1]
        cps, peers = descriptors(src_refs, land_refs, send_sems, recv_sems)
        barrier = pltpu.get_barrier_semaphore()
        for peer in peers:
            pl.semaphore_signal(barrier, inc=1, device_id=peer, device_id_type=MESH)
        pl.semaphore_wait(barrier, len(peers))
        for cp in cps:
            cp.start()
        token[...] = jnp.zeros_like(token)

    lands = [pltpu.with_memory_space_constraint(lax.empty(s.shape, s.dtype), pltpu.HBM) for s in land_sds]
    srcs = [pltpu.with_memory_space_constraint(s, pltpu.HBM) for s in srcs]
    res = pl.pallas_call(
        start_body, name=name + "_start",
        out_shape=(pltpu.SemaphoreType.DMA(()),) * (2 * nsem)
        + tuple(pltpu.HBM(s.shape, s.dtype) for s in srcs)
        + tuple(pltpu.HBM(s.shape, s.dtype) for s in land_sds) + (_sds((SUBLANES, LANES), F32),),
        in_specs=[HBM] * (2 * n),
        out_specs=(SEM,) * (2 * nsem) + (HBM,) * (2 * n) + (pl.BlockSpec(memory_space=pltpu.VMEM),),
        input_output_aliases={i: 2 * nsem + i for i in range(2 * n)},
        compiler_params=pltpu.CompilerParams(has_side_effects=effect, collective_id=collective_id),
    )(*srcs, *lands)
    sems = res[:2 * nsem]
    thru = res[2 * nsem:2 * nsem + 2 * n]
    token = res[-1]

    def wait(after):
        def wait_body(*refs):
            src_refs, land_refs = refs[:n], refs[n:2 * n]
            cps, _ = descriptors(src_refs, land_refs, refs[2 * n:2 * n + nsem],
                                 refs[2 * n + nsem:2 * n + 2 * nsem])
            for cp in cps:
                cp.wait_send()
            for cp in cps:
                cp.wait_recv()

        out = pl.pallas_call(
            wait_body, name=name + "_wait",
            out_shape=tuple(pltpu.HBM(s.shape, s.dtype) for s in srcs)
            + tuple(pltpu.HBM(s.shape, s.dtype) for s in land_sds),
            in_specs=[HBM] * (2 * n) + [SEM] * (2 * nsem) + [pl.BlockSpec(memory_space=pl.ANY)],
            out_specs=(HBM,) * (2 * n),
            input_output_aliases={i: i for i in range(2 * n)},
            compiler_params=pltpu.CompilerParams(has_side_effects=effect),
        )(*thru, *sems, after)
        return list(out[:n]), list(out[n:])

    return token, wait


def pair_sum(name, grad, got, place):
    shp = grad.shape[1:]
    r, cdim = shp[-2], shp[-1]
    lead = int(math.prod(shp[:-2])) if len(shp) > 2 else 1
    g5 = grad.reshape(NCHIP, 2, lead * r, cdim)
    t4 = got.reshape(NCHIP, lead * r, cdim)
    R = lead * r
    tr = _tile(R, max(8, (1 << 20) // cdim))

    def body(p_ref, g_ref, t_ref, o_ref):
        o_ref[...] = (g_ref[0].astype(F32) + t_ref[...].astype(F32)).astype(o_ref.dtype)

    out = pl.pallas_call(
        body, name=name,
        grid_spec=pltpu.PrefetchScalarGridSpec(
            num_scalar_prefetch=1, grid=(NCHIP - 1, R // tr),
            in_specs=[pl.BlockSpec((1, 1, tr, cdim), lambda j, i, p: (p[1] ^ (j + 1), p[0], i, 0)),
                      pl.BlockSpec((1, tr, cdim), lambda j, i, p: (p[1] ^ (j + 1), i, 0))],
            out_specs=pl.BlockSpec((1, tr, cdim), lambda j, i, p: (p[1] ^ (j + 1), i, 0))),
        out_shape=_sds((NCHIP, R, cdim), grad.dtype),
        compiler_params=_params(("parallel", "parallel")),
    )(place, g5, t4)
    return out


def chip_exchange(name, parts, collective_id):
    def plan(srcs, lands):
        x, y, c, chips = _place()
        return ([(i, j, srcs[i].at[2 * chip[0] + chip[1]], lands[i].at[j], (*chip, c))
                 for i in range(len(srcs)) for j, chip in enumerate(chips)],
                [(*chip, c) for chip in chips])

    return _split_exchange(name, parts, [_sds((3,) + p.shape[1:], p.dtype) for p in parts],
                           plan, 3, collective_id)


def ada_fwd(c_row, w_ada, b_ada):
    D, cols = w_ada.shape

    def body(c_ref, w_ref, b_ref, mod_ref, call_ref, act8, part, s1, r1, s2, r2):
        x, y, c, _ = _place()
        me = 4 * x + 2 * y + c
        call_ref[me] = c_ref[...]
        cps = []
        for k in range(1, NDEV):
            to = (x ^ (k >> 2), y ^ ((k >> 1) & 1), c ^ (k & 1))
            cps.append(pltpu.make_async_remote_copy(
                src_ref=c_ref, dst_ref=call_ref.at[me], send_sem=s1.at[k - 1],
                recv_sem=r1.at[k - 1], device_id=to, device_id_type=MESH))
            cps[-1].start()
        for cp in cps:
            cp.wait()
        for b in range(NDEV):
            act8[b:b + 1, :] = call_ref[b]
        cv = act8[...]
        act = (cv * _sigmoid(cv)).astype(BF16)
        res = jnp.dot(act, w_ref[...].astype(BF16), preferred_element_type=F32)
        for b in range(NDEV):
            part[b] = res[b:b + 1, :]
        mod_ref[me] = part[me]
        cps = []
        for k in range(1, NDEV):
            to = (x ^ (k >> 2), y ^ ((k >> 1) & 1), c ^ (k & 1))
            dst = 4 * to[0] + 2 * to[1] + to[2]
            cps.append(pltpu.make_async_remote_copy(
                src_ref=part.at[dst], dst_ref=mod_ref.at[me], send_sem=s2.at[k - 1],
                recv_sem=r2.at[k - 1], device_id=to, device_id_type=MESH))
            cps[-1].start()
        for cp in cps:
            cp.wait()
        for b in range(NDEV):
            mod_ref[b] = mod_ref[b] + b_ref[b]

    vm = pl.BlockSpec(memory_space=pltpu.VMEM)
    return pl.pallas_call(
        body, name="ada_fwd", in_specs=[vm, vm, vm], out_specs=[vm, vm],
        out_shape=[_sds((NDEV, 1, cols), F32), _sds((NDEV, 1, D), F32)],
        scratch_shapes=[pltpu.VMEM((NDEV, D), F32), pltpu.VMEM((NDEV, 1, cols), F32),
                        pltpu.SemaphoreType.DMA((NDEV - 1,)), pltpu.SemaphoreType.DMA((NDEV - 1,)),
                        pltpu.SemaphoreType.DMA((NDEV - 1,)), pltpu.SemaphoreType.DMA((NDEV - 1,))],
        compiler_params=pltpu.CompilerParams(vmem_limit_bytes=VMEM_LIMIT),
    )(c_row, w_ada, b_ada.reshape(NDEV, 1, cols))


def _adamw_math(g, w, m, v):
    m2 = ADAM_B1 * m + (1.0 - ADAM_B1) * g
    v2 = ADAM_B2 * v + (1.0 - ADAM_B2) * (g * g)
    m_hat = m2 / (1.0 - ADAM_B1 ** ADAM_STEP)
    v_hat = v2 / (1.0 - ADAM_B2 ** ADAM_STEP)
    delta = -ADAM_LR * (m_hat / (jnp.sqrt(v_hat) + ADAM_EPS) + ADAM_WD * w)
    return delta, m2, v2


def adamw_sharded(name, grad8, pair4, got3, w, m, v, place, after=()):
    shape = w.shape
    cdim = shape[-1]
    R = int(math.prod(shape[:-1]))
    w2, m2, v2 = (t.reshape(R, cdim) for t in (w, m, v))
    tr = _tile(R, max(8, (1 << 19) // cdim))

    def body(q_ref, own_ref, sib_ref, t_ref, w_ref, m_ref, v_ref, g_out, d_out, m_out, v_out):
        g = own_ref[0].astype(F32) + sib_ref[0].astype(F32)
        for j in range(3):
            g = g + t_ref[j].astype(F32)
        d, mn, vn = _adamw_math(g, w_ref[...], m_ref[...], v_ref[...])
        g_out[...] = g
        d_out[...] = d
        m_out[...] = mn
        v_out[...] = vn

    spec = pl.BlockSpec((tr, cdim), lambda i, qr: (i, 0))
    outs = pl.pallas_call(
        _with_after(body, 7, after), name=name,
        grid_spec=pltpu.PrefetchScalarGridSpec(
            num_scalar_prefetch=1, grid=(R // tr,),
            in_specs=[pl.BlockSpec((1, tr, cdim), lambda i, qr: (qr[2], i, 0)),
                      pl.BlockSpec((1, tr, cdim), lambda i, qr: (qr[1], i, 0)),
                      pl.BlockSpec((3, tr, cdim), lambda i, qr: (0, i, 0)), spec, spec, spec]
            + [ANY] * len(after),
            out_specs=[spec] * 4),
        out_shape=[_sds((R, cdim), F32)] * 4,
        compiler_params=_params(("parallel",)),
    )(place, grad8.reshape(NDEV, R, cdim), pair4.reshape(NCHIP, R, cdim),
      got3.reshape(3, R, cdim), w2, m2, v2, *after)
    return [o.reshape(shape) for o in outs]


def sum_small(parts, after=()):
    R = parts.shape[1]

    def body(p_ref, g_out):
        g = p_ref[0]
        for j in range(1, NDEV):
            g = g + p_ref[j]
        g_out[...] = g

    return pl.pallas_call(
        _with_after(body, 1, after), name="sum_small", grid=(1,),
        in_specs=[pl.BlockSpec((NDEV, R, LANES), lambda i: (0, 0, 0))] + [ANY] * len(after),
        out_specs=pl.BlockSpec((R, LANES), lambda i: (0, 0)), out_shape=_sds((R, LANES), F32),
        compiler_params=_params(("arbitrary",)),
    )(parts, *after)


def adamw_natural(gs, ws, ms, vs):
    n = len(ws)
    nblk = 8
    big = [w.ndim == 4 and w.shape[1] % nblk == 0 for w in ws]

    def spec(w, is_big):
        if is_big:
            return pl.BlockSpec((1, w.shape[1] // nblk) + w.shape[2:], lambda i: (0, i, 0, 0))
        return pl.BlockSpec(w.shape, functools.partial(lambda i, nd: (0,) * nd, nd=w.ndim))

    def body(*refs):
        g_refs, w_refs, m_refs, v_refs = (refs[k * n:(k + 1) * n] for k in range(4))
        d_outs, m_outs, v_outs = (refs[(4 + k) * n:(5 + k) * n] for k in range(3))

        def update(p):
            d, mn, vn = _adamw_math(g_refs[p][...], w_refs[p][...], m_refs[p][...], v_refs[p][...])
            d_outs[p][...] = d
            m_outs[p][...] = mn
            v_outs[p][...] = vn

        for p in range(n):
            if big[p]:
                update(p)

        @pl.when(pl.program_id(0) == 0)
        def _():
            for p in range(n):
                if not big[p]:
                    update(p)

    specs = [spec(w, b) for w, b in zip(ws, big)]
    outs = pl.pallas_call(
        body, name="adamw_natural", grid=(nblk,), in_specs=specs * 4, out_specs=specs * 3,
        out_shape=[_sds(w.shape, F32) for w in ws] * 3,
        compiler_params=_params(("arbitrary",)),
    )(*gs, *ws, *ms, *vs)
    return outs[:n], outs[n:2 * n], outs[2 * n:]


def adamw_ada(c_all_t, dmod_all, w, m, v, my_dev):
    D, cols = w.shape
    tr = _tile(D, 256)

    def body(k_ref, c_ref, d_ref, w_ref, m_ref, v_ref, g_out, d_out, m_out, v_out):
        cv = c_ref[...]
        act = cv * _sigmoid(cv)
        dm = d_ref[...]
        g = act[:, 0:1] * dm[0:1, :]
        for b in range(1, NDEV):
            g = g + act[:, b:b + 1] * dm[b:b + 1, :]
        d, mn, vn = _adamw_math(g, w_ref[...], m_ref[...], v_ref[...])
        g_out[...] = g
        d_out[...] = d
        m_out[...] = mn
        v_out[...] = vn

    spec = pl.BlockSpec((tr, cols), lambda i, kr: (i, 0))
    return pl.pallas_call(
        body, name="adamw_ada",
        grid_spec=pltpu.PrefetchScalarGridSpec(
            num_scalar_prefetch=1, grid=(D // tr,),
            in_specs=[pl.BlockSpec((tr, NDEV), lambda i, kr: (i, 0)),
                      pl.BlockSpec((NDEV, cols), lambda i, kr: (0, kr[0])), spec, spec, spec],
            out_specs=[spec] * 4),
        out_shape=[_sds((D, cols), F32)] * 4,
        compiler_params=_params(("parallel",)),
    )(my_dev, c_all_t, dmod_all, w, m, v)


def _small_pack(parts):
    rows = []
    for p in parts:
        flat = p.reshape(-1)
        flat = jnp.pad(flat, (0, (-flat.shape[0]) % (SUBLANES * LANES)))
        rows.append(flat.reshape(-1, LANES))
    return jnp.concatenate(rows, axis=0)


def _small_unpack(buf, shapes):
    out, r = [], 0
    for s in shapes:
        n = int(math.prod(s))
        nr = -(-n // (SUBLANES * LANES)) * SUBLANES
        out.append(buf[r:r + nr].reshape(-1)[:n].reshape(s))
        r += nr
    return out


def kernel(x, c, w_ada, b_ada, w_in, lam_re, lam_im, log_dt, ssm_b_re, ssm_b_im, ssm_c_re, ssm_c_im, ssm_d, w_glu_val, w_glu_gate, w_pool, pool_scale, w_pool_out, w_out, ln1_g, ln1_b, w_ff1, w_ff2, ln2_g, ln2_b, loss_target, m_w_ada, m_b_ada, m_w_in, m_lam_re, m_lam_im, m_log_dt, m_ssm_b_re, m_ssm_b_im, m_ssm_c_re, m_ssm_c_im, m_ssm_d, m_w_glu_val, m_w_glu_gate, m_w_pool, m_pool_scale, m_w_pool_out, m_w_out, m_ln1_g, m_ln1_b, m_w_ff1, m_w_ff2, m_ln2_g, m_ln2_b, v_w_ada, v_b_ada, v_w_in, v_lam_re, v_lam_im, v_log_dt, v_ssm_b_re, v_ssm_b_im, v_ssm_c_re, v_ssm_c_im, v_ssm_d, v_w_glu_val, v_w_glu_gate, v_w_pool, v_pool_scale, v_w_pool_out, v_w_out, v_ln1_g, v_ln1_b, v_w_ff1, v_w_ff2, v_ln2_g, v_ln2_b):
    S, D = x.shape[1], x.shape[2]
    x2d, tgt = x[0], loss_target[0]
    W = D // 2
    G = W // SSM_GROUP
    P, H, GPB = SSM_STATE, SSM_GROUP, GROUPS_PER_BLOCK
    nblk = G // GPB
    gw = W // len(POOL_WINDOWS)
    ax, ay, ac = lax.axis_index("x"), lax.axis_index("y"), lax.axis_index("c")
    my_dev = (4 * ax + 2 * ay + ac).astype(jnp.int32).reshape(1)
    place = jnp.stack([ac, 2 * ax + ay, 4 * ax + 2 * ay + ac]).astype(jnp.int32)
    ts = _tile(S, 256)

    glu = jnp.concatenate([w_glu_val[0], w_glu_gate[0]]).astype(BF16)
    shards = [w_in[0].astype(BF16), glu, w_pool[0].astype(BF16), w_pool_out[0].astype(BF16),
              w_out[0].astype(BF16), w_ff1[0].astype(BF16), w_ff2[0].astype(BF16)]
    wg_in, wg_pool = seq_all_gather("gather_w_in", [shards[0], shards[2]], 1)
    wg_vg, wg_po, wg_out = seq_all_gather("gather_w_mix", [shards[1], shards[3], shards[4]], 2)
    (wg_ff1,) = seq_all_gather("gather_w_ff1", shards[5:6], 3)
    (wg_ff2,) = seq_all_gather("gather_w_ff2", shards[6:7], 11)
    wg_vg = wg_vg.reshape(2 * NDEV, W, D // NDEV)
    nwin = len(POOL_WINDOWS)
    wp_full = jnp.transpose(wg_pool, (1, 0, 2, 3)).reshape(nwin, gw, gw)
    wout_full = wg_out.reshape(1, D, D)
    wff2_full = wg_ff2.reshape(1, 4 * D, D)

    small_names = [b_ada, lam_re, lam_im, log_dt, ssm_b_re, ssm_b_im, ssm_c_re, ssm_c_im, ssm_d,
                   pool_scale, ln1_g, ln1_b, ln2_g, ln2_b]
    small_m = [m_b_ada, m_lam_re, m_lam_im, m_log_dt, m_ssm_b_re, m_ssm_b_im, m_ssm_c_re, m_ssm_c_im,
               m_ssm_d, m_pool_scale, m_ln1_g, m_ln1_b, m_ln2_g, m_ln2_b]
    small_v = [v_b_ada, v_lam_re, v_lam_im, v_log_dt, v_ssm_b_re, v_ssm_b_im, v_ssm_c_re, v_ssm_c_im,
               v_ssm_d, v_pool_scale, v_ln1_g, v_ln1_b, v_ln2_g, v_ln2_b]

    mod, c_all = ada_fwd(c, w_ada[0], b_ada)
    mod = mod.reshape(6, 1, D)
    sh1, sc1, g1, sh2, sc2, g2 = (mod[i] for i in range(6))

    f2, kconst = s5_disc(lam_re[0], lam_im[0], log_dt[0].reshape(G, 1))
    kconst = kconst.reshape(NCONST, SUBLANES, G * P)
    f2r = f2.reshape(2, 1, G * P)
    bt_re = jnp.transpose(ssm_b_re[0], (2, 0, 1)).reshape(H, G * P)
    bt_im = jnp.transpose(ssm_b_im[0], (2, 0, 1)).reshape(H, G * P)
    ct_re = jnp.transpose(ssm_c_re[0], (1, 0, 2)).reshape(H, G * P)
    ct_im = jnp.transpose(ssm_c_im[0], (1, 0, 2)).reshape(H, G * P)
    s5_params = (f2r, bt_re, bt_im, ct_re, ct_im, ssm_d, kconst)

    def e1(t, b):
        xhat, _ = _ln_stats(t[0])
        return [xhat * (1.0 + b[0]) + b[1]], []
    (h1,) = _rowwise("ln_mod1", e1, S, ts, [(x2d, D, 0)], [sc1, sh1], [(D, BF16)], [])

    (proj,) = mm_nn("proj", h1, wg_in, F32, 2)
    z, xsb_all, zp = s5_fwd(proj, s5_params, nblk)
    (vt,) = mm_nn("glu", z, wg_vg, BF16, 4)
    pooled = pool_fwd(proj, W, W, gw)

    def pool_epi(vals, ex, outs):
        a = vals[0]
        outs[0][...] = a
        outs[1][...] = (a * ex[0][...]).astype(BF16)
    tmp = _tile(S, 1024)
    yp, ypool = _mm(
        "pool_mix", "nn", pooled, wp_full.astype(BF16), (S // tmp, nwin, 1),
        pl.BlockSpec((tmp, gw), lambda i, j, k: (i, j)), pl.BlockSpec((1, gw, gw), lambda i, j, k: (j, 0, 0)),
        [(_sds((S, W), F32), pl.BlockSpec((tmp, gw), lambda i, j, k: (i, j))),
         (_sds((S, W), BF16), pl.BlockSpec((tmp, gw), lambda i, j, k: (i, j)))],
        (tmp, gw), 1, gw, None, pool_epi,
        [(pool_scale, pl.BlockSpec((1, gw), lambda i, j, k: (0, j)))])
    (y_b,) = mm_nn("pool_out", ypool, wg_po, BF16, 4)

    cb = D // NDEV
    ga_cb, gb_cb = (2 * W) // cb, (2 * W + D) // cb
    mcb = 4
    wm = mcb * cb
    tsm = _tile(S, 256)

    def merge_call(name, fn, ins, n_out, after=()):
        def body(*refs):
            vals = [r[...].astype(F32) for r in refs[:len(ins)]]
            for r, v in zip(refs[len(ins):], fn(*vals)):
                r[...] = v.astype(r.dtype)
        return pl.pallas_call(
            _with_after(body, len(ins), after), name=name, grid=(S // tsm, NDEV // mcb),
            in_specs=[pl.BlockSpec((tsm, w), f) for (_, w, f) in ins] + [ANY] * len(after),
            out_specs=[pl.BlockSpec((tsm, w), lambda i, j: (i, j)) for (_, w) in n_out],
            out_shape=[_sds((S, cols), BF16) for (cols, _) in n_out],
            compiler_params=_params(("parallel", "parallel")),
        )(*[a for (a, _, _) in ins], *after)

    merge_ins = [(proj, wm, lambda i, j: (i, ga_cb // mcb + j)), (proj, wm, lambda i, j: (i, gb_cb // mcb + j)),
                 (vt, 2 * wm, lambda i, j: (i, j)), (y_b, wm, lambda i, j: (i, j))]

    def val_gate(vtv):
        return (jnp.concatenate([vtv[:, 2 * q * cb:(2 * q + 1) * cb] for q in range(mcb)], axis=1),
                jnp.concatenate([vtv[:, (2 * q + 1) * cb:(2 * q + 2) * cb] for q in range(mcb)], axis=1))

    def merge_f(ga, gb, vtv, yb):
        vv, tt = val_gate(vtv)
        return [_sigmoid(ga) * (vv * _sigmoid(tt)) + _sigmoid(gb) * yb]
    (merged,) = merge_call("merge", merge_f, merge_ins, [(D, wm)])

    (mix,) = mm_nn("mix_out", merged, wout_full, F32, 1)

    def e3(t, b):
        xv, mx = t
        g1v, l1g, l1b, sc2v, sh2v = b
        r1 = ALPHA * xv + g1v * mx
        xh1, _ = _ln_stats(r1)
        x1 = xh1 * l1g + l1b
        xh, _ = _ln_stats(x1)
        return [r1, xh * (1.0 + sc2v) + sh2v], []
    r1, h2 = _rowwise("post_mix", e3, S, ts, [(x2d, D, 0), (mix, D, 0)],
                      [g1, ln1_g, ln1_b, sc2, sh2], [(D, F32), (D, BF16)], [])

    def relu_epi(vals, ex, outs):
        outs[0][...] = jnp.maximum(vals[0], 0.0).astype(BF16)
    (rl,) = mm_nn("ff1", h2, wg_ff1, BF16, 1, epi=relu_epi)

    def square(a):
        return a * a
    (y2,) = mm_nn("ff2", rl, wff2_full, F32, 1, pro=square)

    def e4(t, b):
        r1v, y2v, tg = t
        g2v, l1g, l1b, l2g, l2b = b
        xh1, _ = _ln_stats(r1v)
        x1 = xh1 * l1g + l1b
        r2 = ALPHA * x1 + g2v * y2v
        xh2, rs2 = _ln_stats(r2)
        err = xh2 * l2g + l2b - tg
        dx2 = err * (1.0 / D)
        dr2 = _ln_bwd(dx2 * l2g, xh2, rs2)
        lsum = jnp.sum(_colsum(err * err), axis=1, keepdims=True) * (0.5 / D)
        return ([ALPHA * dr2, g2v * dr2],
                [jnp.broadcast_to(lsum, (1, LANES)), _colsum(dx2 * xh2), _colsum(dx2), _colsum(dr2 * y2v)])
    dx1a, dy2, loss_acc, g_ln2g, g_ln2b, d_g2 = _rowwise(
        "head", e4, S, ts, [(r1, D, 0), (y2, D, 0), (tgt, D, 0)], [g2, ln1_g, ln1_b, ln2_g, ln2_b],
        [(D, F32), (D, BF16)], [LANES, D, D, D])

    tn_ff = _tile(4 * D, 1024)

    def dff_epi(vals, ex, outs):
        outs[0][...] = (vals[0] * (2.0 * ex[0][...].astype(F32))).astype(BF16)
    tmf = _tile(S, 1024)
    (da1,) = mm_nt("d_ff2", dy2, wff2_full, BF16, 1, tn=tn_ff, epi=dff_epi,
                   extras=[(rl, pl.BlockSpec((tmf, tn_ff), lambda i, j, k: (i, j)))])
    gw_ff2 = mm_tn("gw_ff2", rl, dy2, BF16, NDEV, 0, pro=square)
    gw_ff1 = mm_tn("gw_ff1", h2, da1, BF16, NDEV, 1)
    tok, wait_pair_a = pair_exchange("pair_exchange_ff", [gw_ff2, gw_ff1], 4)
    (dh2,) = mm_nt("d_ff1", da1, wg_ff1, F32, 4, after=[tok])

    def e5(t, b):
        dh2v, r1v, dx1av, mx = t
        sc2v, l1g, l1b, g1v = b
        xh1, rs1 = _ln_stats(r1v)
        x1 = xh1 * l1g + l1b
        xh, rs = _ln_stats(x1)
        dx1 = dx1av + _ln_bwd(dh2v * (1.0 + sc2v), xh, rs)
        dr1 = _ln_bwd(dx1 * l1g, xh1, rs1)
        return ([ALPHA * dr1, g1v * dr1],
                [_colsum(dh2v * xh), _colsum(dh2v), _colsum(dx1 * xh1), _colsum(dx1), _colsum(dr1 * mx)])
    dxa, dmix, d_sc2, d_sh2, g_ln1g, g_ln1b, d_g1 = _rowwise(
        "post_mix_bwd", e5, S, ts, [(dh2, D, 0), (r1, D, 0), (dx1a, D, 0), (mix, D, 0)],
        [sc2, ln1_g, ln1_b, g1], [(D, F32), (D, BF16)], [D, D, D, D, D])

    (dmerged,) = mm_nt("d_mix_out", dmix, wout_full, BF16, 1)
    gw_out = mm_tn("gw_out", merged, dmix, BF16, NDEV, 0)
    grads_a, got_a = wait_pair_a(gw_out)
    parts_a = [pair_sum("pair_sum_ff%d" % i, g, t, place) for i, (g, t) in enumerate(zip(grads_a, got_a))]
    tok, wait_chip_a = chip_exchange("chip_exchange_ff", parts_a, 5)

    def merge_b(ga, gb, vtv, yb, dm):
        vv, tt = val_gate(vtv)
        sa, sb, st = _sigmoid(ga), _sigmoid(gb), _sigmoid(tt)
        dya = dm * sa
        dv, dt = dya * st, dya * vv * st * (1.0 - st)
        dvt_tile = jnp.concatenate([t[:, q * cb:(q + 1) * cb] for q in range(mcb) for t in (dv, dt)], axis=1)
        return [dm * (vv * st) * sa * (1.0 - sa), dm * yb * sb * (1.0 - sb), dvt_tile, dm * sb]
    dga, dgb_, dvt, dy_b = merge_call(
        "merge_bwd", merge_b, merge_ins + [(dmerged, wm, lambda i, j: (i, j))],
        [(D, wm), (D, wm), (2 * D, 2 * wm), (D, wm)], after=[tok])

    (dypool,) = mm_nt("d_pool_out", dy_b, wg_po, F32, NDEV)
    gw_po = mm_tn("gw_pool_out", ypool, dy_b, BF16, NDEV, 4)

    def e7(t, b):
        return [t[0] * b[0]], [_colsum(t[0] * t[1])]
    dyp, g_pscale = _rowwise("pool_scale_bwd", e7, S, ts, [(dypool, W, 0), (yp, W, 0)],
                             [pool_scale], [(W, BF16)], [W])
    (dpooled,) = _mm(
        "d_pool_mix", "nt", dyp, wp_full.astype(BF16), (S // tmp, nwin, 1),
        pl.BlockSpec((tmp, gw), lambda i, j, k: (i, j)), pl.BlockSpec((1, gw, gw), lambda i, j, k: (j, 0, 0)),
        [(_sds((S, W), F32), pl.BlockSpec((tmp, gw), lambda i, j, k: (i, j)))], (tmp, gw), 1, gw)
    tkp = _tile(S, 2048)
    gw_pool = _mm(
        "gw_pool", "tn", pooled, dyp, (nwin, 1, S // tkp),
        pl.BlockSpec((tkp, gw), lambda i, j, k: (k, i)), pl.BlockSpec((tkp, gw), lambda i, j, k: (k, i)),
        [(_sds((nwin, gw, gw), BF16), pl.BlockSpec((1, gw, gw), lambda i, j, k: (i, 0, 0)))],
        (gw, gw), 1, gw, stacked_out=True)[0]
    du_pool = pool_bwd(dpooled, gw)

    (dz,) = mm_nt("d_glu", dvt, wg_vg, BF16, 2 * NDEV)
    gw_vg = mm_tn("gw_glu", z, dvt, BF16, 2 * NDEV, 4)
    gw_pool_st = jnp.transpose(gw_pool.reshape(nwin, NDEV, gw // NDEV, gw), (1, 0, 2, 3))
    grads_b = [gw_out, gw_po, gw_pool_st, gw_vg.reshape(NDEV, 2, W, D // NDEV)]
    tok, wait_pair_b = pair_exchange("pair_exchange_mix", grads_b, 6)
    du_ssm, g_bt_re, g_bt_im, g_ct_re, g_ct_im, g_f, g_d, g_a = s5_bwd(
        proj, xsb_all, dz, zp, s5_params, nblk, after=[tok])
    grads_b, got_b = wait_pair_b(du_ssm)
    parts_b = [pair_sum("pair_sum_mix%d" % i, g, t, place) for i, (g, t) in enumerate(zip(grads_b, got_b))]
    tok, wait_chip_b = chip_exchange("chip_exchange_mix", parts_b, 7)

    dproj = jnp.concatenate([du_ssm, du_pool, dga, dgb_], axis=1)
    gw_in = mm_tn("gw_in", h1, dproj, BF16, NDEV, 1, after=[tok])
    tok, wait_pair_c = pair_exchange("pair_exchange_in", [gw_in], 8)
    (dh1,) = mm_nt("d_proj", dproj, wg_in, F32, 4, after=[tok])
    grads_c, got_c = wait_pair_c(dh1)
    parts_c = [pair_sum("pair_sum_in", grads_c[0], got_c[0], place)]
    tok, wait_chip_c = chip_exchange("chip_exchange_in", parts_c, 9)

    def e10(t, b):
        dh1v, xv, dxav = t
        xh, rs = _ln_stats(xv)
        return ([dxav + _ln_bwd(dh1v * (1.0 + b[0]), xh, rs)],
                [_colsum(dh1v * xh), _colsum(dh1v)])
    grad_x, d_sc1, d_sh1 = _rowwise("ln_mod1_bwd", e10, S, ts, [(dh1, D, 0), (x2d, D, 0), (dxa, D, 0)],
                                    [sc1], [(D, F32)], [D, D], after=[tok])

    g_b_re = jnp.transpose(g_bt_re.reshape(H, G, P), (1, 0, 2))
    g_b_im = jnp.transpose(g_bt_im.reshape(H, G, P), (1, 0, 2))
    g_c_re = jnp.transpose(g_ct_re.reshape(H, G, P), (1, 0, 2))
    g_c_im = jnp.transpose(g_ct_im.reshape(H, G, P), (1, 0, 2))
    d_ab = jnp.transpose(g_a.reshape(nblk, 2, GPB, P), (1, 0, 2, 3)).reshape(2, G, P)
    g_lr, g_li, g_ldt = s5_disc_bwd(lam_re[0], lam_im[0], log_dt[0].reshape(G, 1), d_ab,
                                    g_f.reshape(2, G, P))

    dmod = jnp.concatenate([d_sh1, d_sc1, d_g1, d_sh2, d_sc2, d_g2], axis=1)
    small_g = [dmod, g_lr, g_li, g_ldt, g_b_re, g_b_im, g_c_re, g_c_im, g_d, g_pscale,
               g_ln1g, g_ln1b, g_ln2g, g_ln2b, loss_acc]
    packed_g = _small_pack(small_g)
    (parts_all,) = seq_all_gather("gather_small", [packed_g], 10)
    glu_w = jnp.stack([w_glu_val[0], w_glu_gate[0]])
    glu_m = jnp.stack([m_w_glu_val[0], m_w_glu_gate[0]])
    glu_v = jnp.stack([v_w_glu_val[0], v_w_glu_gate[0]])
    wmv = [(w_ff2[0], m_w_ff2[0], v_w_ff2[0]), (w_ff1[0], m_w_ff1[0], v_w_ff1[0]),
           (w_out[0], m_w_out[0], v_w_out[0]), (w_pool_out[0], m_w_pool_out[0], v_w_pool_out[0]),
           (w_pool[0], m_w_pool[0], v_w_pool[0]), (glu_w, glu_m, glu_v)]
    _, got3_a = wait_chip_a(packed_g)
    upd = [adamw_sharded("adamw_%d" % i, g, p, t, w, m, v, place)
           for i, (g, p, t, (w, m, v)) in enumerate(zip(grads_a, got_a, got3_a, wmv[:2]))]
    _, got3_b = wait_chip_b(upd[-1][0])
    upd += [adamw_sharded("adamw_%d" % (2 + i), g, p, t, w, m, v, place)
            for i, (g, p, t, (w, m, v)) in enumerate(zip(grads_b, got_b, got3_b, wmv[2:]))]
    u_ff2, u_ff1, u_out, u_po, u_pool, u_glu = upd

    gsum = sum_small(parts_all, after=[upd[-1][0]])
    def swap_b(ts_):
        return [jnp.swapaxes(t, 2, 3) if i in (4, 5) else t for i, t in enumerate(ts_)]

    sg = _small_unpack(gsum, [t.shape for t in swap_b(small_names)] + [(1, LANES)])
    loss, sg = sg[-1][0, 0], sg[:-1]
    sd, sm, sv = adamw_natural(sg, swap_b(small_names), swap_b(small_m), swap_b(small_v))
    sg, sd, sm, sv = swap_b(sg), swap_b(sd), swap_b(sm), swap_b(sv)

    nmod = 6 * D
    dmod_all = parts_all[:, :nmod // LANES, :].reshape(NDEV, nmod)
    c_all_t = jnp.transpose(c_all.reshape(NDEV, D))
    ada_out = adamw_ada(c_all_t, dmod_all, w_ada[0], m_w_ada[0], v_w_ada[0], my_dev)
    _, got3_c = wait_chip_c(ada_out[0])
    u_in = adamw_sharded("adamw_6", grads_c[0], got_c[0], got3_c[0], w_in[0], m_w_in[0], v_w_in[0], place)

    def pick(k):
        return [ada_out[k][None], sg_sd[k][0], u_in[k][None]] + [t for t in sg_sd[k][1:9]] + \
               [u_glu[k][0][None], u_glu[k][1][None], u_pool[k][None], sg_sd[k][9], u_po[k][None],
                u_out[k][None], sg_sd[k][10], sg_sd[k][11], u_ff1[k][None], u_ff2[k][None],
                sg_sd[k][12], sg_sd[k][13]]

    sg_sd = [sg, sd, sm, sv]
    return (loss, grad_x[None], *pick(0), *pick(1), *pick(2), *pick(3))
```

```python
import functools
import math

import jax
import jax.numpy as jnp
from jax import lax
from jax.experimental import pallas as pl
from jax.experimental.pallas import tpu as pltpu
from jax.experimental.pallas import tpu_sc as plsc

F32 = jnp.float32
BF16 = jnp.bfloat16
MESH = pl.DeviceIdType.MESH
NDEV = 8
NCHIP = 4

SSM_GROUP = 16
SSM_STATE = 64
GROUPS_PER_BLOCK = 8
POOL_WINDOWS = (2, 4, 8, 16)
LN_EPS = 1e-5
ALPHA = 2.0 ** 0.25
ADAM_LR, ADAM_B1, ADAM_B2, ADAM_EPS, ADAM_WD, ADAM_STEP = 0.001, 0.9, 0.999, 1e-08, 0.01, 10
SUBLANES = 8
LANES = 128
VMEM_LIMIT = 56 * 1024 * 1024


def _params(sem=None, vmem=VMEM_LIMIT):
    return pltpu.CompilerParams(dimension_semantics=sem, vmem_limit_bytes=vmem)


def _tile(n, pref):
    if n <= pref:
        return n
    t = 1 << (pref.bit_length() - 1)
    while n % t:
        t //= 2
    return t


def _cast_epi(vals, ex, outs):
    c = vals[0].shape[1]
    for s, v in enumerate(vals):
        outs[0][:, s * c:(s + 1) * c] = v.astype(outs[0].dtype)


ANY = pl.BlockSpec(memory_space=pl.ANY)


def _with_after(body, n_in, after):
    if not after:
        return body
    n_af = len(after)

    def wrapped(*refs):
        return body(*refs[:n_in], *refs[n_in + n_af:])
    return wrapped


def _mm(name, kind, a, b, grid, a_spec, b_spec, outs, acc_shape, nsub=1, c=None,
        pro=None, epi=None, extras=(), stacked_out=False, after=()):
    nk = grid[2]
    n_ex, n_out = len(extras), len(outs)

    def finish(vals, ex, out_refs):
        if epi is not None:
            epi(vals, ex, out_refs)
        elif stacked_out:
            for s, v in enumerate(vals):
                out_refs[0][s] = v.astype(out_refs[0].dtype)
        else:
            _cast_epi(vals, ex, out_refs)

    def body(*refs):
        mm_step(refs[0], refs[1], refs[2:2 + n_ex], refs[2 + n_ex:2 + n_ex + n_out], refs[-1])

    def mm_step(a_ref, b_ref, ex, out_refs, acc):
        k = pl.program_id(2)
        av = a_ref[...]
        if pro is not None:
            av = pro(av)
        if kind == "nn":
            prods = [jnp.dot(av, b_ref[s], preferred_element_type=F32) for s in range(nsub)]
        elif kind == "nt":
            t = None
            for s in range(nsub):
                d = lax.dot_general(av[:, s * c:(s + 1) * c], b_ref[s], (((1,), (1,)), ((), ())),
                                    preferred_element_type=F32)
                t = d if t is None else t + d
            prods = [t]
        else:
            t = lax.dot_general(av, b_ref[...], (((0,), (0,)), ((), ())), preferred_element_type=F32)
            prods = [t[:, s * c:(s + 1) * c] for s in range(nsub)] if stacked_out else [t]
        if nk == 1:
            finish(prods, ex, out_refs)
            return
        w = prods[0].shape[1]

        @pl.when(k == 0)
        def _():
            for s, p in enumerate(prods):
                acc[:, s * w:(s + 1) * w] = p

        @pl.when(jnp.logical_and(k > 0, k < nk - 1))
        def _():
            for s, p in enumerate(prods):
                acc[:, s * w:(s + 1) * w] += p

        @pl.when(k == nk - 1)
        def _():
            finish([acc[:, s * w:(s + 1) * w] + p for s, p in enumerate(prods)], ex, out_refs)

    return pl.pallas_call(
        _with_after(body, 2 + n_ex, after), name=name, grid=grid,
        in_specs=[a_spec, b_spec] + [e[1] for e in extras] + [ANY] * len(after),
        out_specs=[o[1] for o in outs],
        out_shape=[o[0] for o in outs],
        scratch_shapes=[pltpu.VMEM(acc_shape, F32)] if nk > 1 else [],
        compiler_params=_params(("parallel", "parallel", "arbitrary")),
    )(a, b, *[e[0] for e in extras], *after)


def _sds(shape, dtype):
    return jax.ShapeDtypeStruct(shape, dtype)


def mm_nn(name, a, b3, out_dtype, nsub, tm=1024, tk=2048, tn=None, pro=None, epi=None,
          extras=(), after=()):
    M = a.shape[0]
    nb, K, cdim = b3.shape
    tm, tk = _tile(M, tm), _tile(K, tk)
    if nb == 1:
        tn = _tile(cdim, tn or 1024)
        nsub, c, nj = 1, tn, cdim // tn
        b_spec = pl.BlockSpec((1, tk, tn), lambda i, j, k: (0, k, j))
        N = cdim
    else:
        c, nj, tn = cdim, nb // nsub, nsub * cdim
        b_spec = pl.BlockSpec((nsub, tk, cdim), lambda i, j, k: (j, k, 0))
        N = nb * cdim
    a_spec = pl.BlockSpec((tm, tk), lambda i, j, k: (i, k))
    grid = (M // tm, nj, K // tk)
    outs = [(_sds((M, N), out_dtype), pl.BlockSpec((tm, tn), lambda i, j, k: (i, j)))]
    return _mm(name, "nn", a, b3, grid, a_spec, b_spec, outs, (tm, tn), nsub, c, pro, epi, extras,
               after=after)


def mm_nt(name, a, b3, out_dtype, nsub, tm=1024, tn=1024, epi=None, extras=(), after=()):
    M = a.shape[0]
    nb, N, cdim = b3.shape
    tm, tn = _tile(M, tm), _tile(N, tn)
    if nb == 1:
        tk = _tile(cdim, 2048)
        nsub, c, nk = 1, tk, cdim // tk
        b_spec = pl.BlockSpec((1, tn, tk), lambda i, j, k: (0, j, k))
    else:
        c, nk, tk = cdim, nb // nsub, nsub * cdim
        b_spec = pl.BlockSpec((nsub, tn, cdim), lambda i, j, k: (k, j, 0))
    a_spec = pl.BlockSpec((tm, tk), lambda i, j, k: (i, k))
    grid = (M // tm, N // tn, nk)
    outs = [(_sds((M, N), out_dtype), pl.BlockSpec((tm, tn), lambda i, j, k: (i, j)))]
    return _mm(name, "nt", a, b3, grid, a_spec, b_spec, outs, (tm, tn), nsub, c, None, epi, extras,
               after=after)


def mm_tn(name, a, b, out_dtype, nb, nsub, tma=1024, tk=2048, pro=None, after=()):
    S, Ka = a.shape
    N = b.shape[1]
    tk, tma = _tile(S, tk), _tile(Ka, tma)
    a_spec = pl.BlockSpec((tk, tma), lambda i, j, k: (k, i))
    if nsub == 0:
        tn = _tile(N, 1024)
        res = _mm(name, "tn", a, b, (Ka // tma, N // tn, S // tk), a_spec,
                  pl.BlockSpec((tk, tn), lambda i, j, k: (k, j)),
                  [(_sds((Ka, N), out_dtype), pl.BlockSpec((tma, tn), lambda i, j, k: (i, j)))],
                  (tma, tn), 1, tn, pro, None, (), after=after)[0]
        return res.reshape(nb, Ka // nb, N)
    c = N // nb
    tn = nsub * c
    outs = [(_sds((nb, Ka, c), out_dtype), pl.BlockSpec((nsub, tma, c), lambda i, j, k: (j, i, 0)))]
    return _mm(name, "tn", a, b, (Ka // tma, nb // nsub, S // tk), a_spec,
               pl.BlockSpec((tk, tn), lambda i, j, k: (k, j)), outs, (tma, tn), nsub, c,
               pro, None, (), stacked_out=True, after=after)[0]


def _rowwise(name, fn, S, ts, tiled, bcast, tiled_out, acc_out, after=()):
    nt, nb, no, na = len(tiled), len(bcast), len(tiled_out), len(acc_out)

    def body(*refs):
        tin = [r[...] for r in refs[:nt]]
        bin_ = [r[...] for r in refs[nt:nt + nb]]
        o_refs = refs[nt + nb:nt + nb + no]
        a_refs = refs[nt + nb + no:]
        touts, aouts = fn(tin, bin_)
        for r, v in zip(o_refs, touts):
            r[...] = v.astype(r.dtype)
        i = pl.program_id(0)

        @pl.when(i == 0)
        def _():
            for r, v in zip(a_refs, aouts):
                r[...] = v

        @pl.when(i > 0)
        def _():
            for r, v in zip(a_refs, aouts):
                r[...] += v

    in_specs = [pl.BlockSpec((ts, w), functools.partial(lambda i, cb: (i, cb), cb=cb))
                for (_, w, cb) in tiled]
    in_specs += [pl.BlockSpec(b.shape, lambda i: (0, 0)) for b in bcast]
    out_specs = [pl.BlockSpec((ts, w), lambda i: (i, 0)) for (w, _) in tiled_out]
    out_specs += [pl.BlockSpec((1, w), lambda i: (0, 0)) for w in acc_out]
    out_shape = [_sds((S, w), d) for (w, d) in tiled_out] + [_sds((1, w), F32) for w in acc_out]
    return pl.pallas_call(
        _with_after(body, nt + nb, after), name=name, grid=(S // ts,),
        in_specs=in_specs + [ANY] * len(after), out_specs=out_specs,
        out_shape=out_shape, compiler_params=_params(("arbitrary",)),
    )(*[t[0] for t in tiled], *bcast, *after)


def _ln_stats(v):
    mu = jnp.mean(v, axis=-1, keepdims=True)
    vc = v - mu
    var = jnp.mean(vc * vc, axis=-1, keepdims=True)
    rstd = lax.rsqrt(var + LN_EPS)
    return vc * rstd, rstd


def _ln_bwd(dxhat, xhat, rstd):
    return rstd * (dxhat - jnp.mean(dxhat, axis=-1, keepdims=True)
                   - xhat * jnp.mean(dxhat * xhat, axis=-1, keepdims=True))


def _colsum(v):
    return jnp.sum(v, axis=0, keepdims=True)


def _sigmoid(v):
    return 1.0 / (1.0 + jnp.exp(-v))


_GELU_C = math.sqrt(2.0 / math.pi)


def _gelu(v):
    return 0.5 * v * (1.0 + jnp.tanh(_GELU_C * (v + 0.044715 * v * v * v)))


def _gelu_grad(v):
    t = jnp.tanh(_GELU_C * (v + 0.044715 * v * v * v))
    return 0.5 * (1.0 + t) + 0.5 * v * (1.0 - t * t) * _GELU_C * (1.0 + 3 * 0.044715 * v * v)


def _disc(lr, li, ldt):
    dt = jnp.exp(ldt)
    mag = jnp.exp(lr * dt)
    ang = li * dt
    ab_re = mag * jnp.cos(ang)
    ab_im = mag * jnp.sin(ang)
    num_re = ab_re - 1.0
    num_im = ab_im
    den = lr * lr + li * li
    f_re = (num_re * lr + num_im * li) / den
    f_im = (num_im * lr - num_re * li) / den
    return ab_re, ab_im, f_re, f_im


def _cmul(ar, ai, br, bi):
    return ar * br - ai * bi, ar * bi + ai * br


SCAN_FOLD = 4
NCONST = 18


def s5_disc(lam_re, lam_im, log_dt):
    G, P = lam_re.shape

    def body(lr_ref, li_ref, ldt_ref, f_ref, k_ref):
        ab_re, ab_im, f_re, f_im = _disc(lr_ref[...], li_ref[...], ldt_ref[...])
        f_ref[0] = f_re
        f_ref[1] = f_im
        fr, fi = ab_re, ab_im
        for _ in range(SCAN_FOLD - 1):
            fr, fi = _cmul(fr, fi, ab_re, ab_im)
        pr, pi = [fr], [fi]
        for _ in range(SUBLANES - 1):
            nr, ni = _cmul(pr[-1], pi[-1], fr, fi)
            pr.append(nr)
            pi.append(ni)
        zero = jnp.zeros_like(ab_re)
        for r in range(SUBLANES):
            k_ref[16, r] = ab_re
            k_ref[17, r] = ab_im
        for n, sh in enumerate((1, 2, 4)):
            for r in range(SUBLANES):
                k_ref[2 * n, r] = pr[sh - 1] if r >= sh else zero
                k_ref[2 * n + 1, r] = pi[sh - 1] if r >= sh else zero
                k_ref[8 + 2 * n, r] = pr[sh - 1] if r + sh < SUBLANES else zero
                k_ref[8 + 2 * n + 1, r] = -pi[sh - 1] if r + sh < SUBLANES else zero
        for r in range(SUBLANES):
            k_ref[6, r] = pr[r]
            k_ref[7, r] = pi[r]
            k_ref[14, r] = pr[SUBLANES - 1 - r]
            k_ref[15, r] = -pi[SUBLANES - 1 - r]

    vm = pl.BlockSpec(memory_space=pltpu.VMEM)
    return pl.pallas_call(
        body, name="s5_disc", in_specs=[vm, vm, vm], out_specs=[vm, vm],
        out_shape=[_sds((2, G, P), F32), _sds((NCONST, SUBLANES, G, P), F32)],
    )(lam_re, lam_im, log_dt)


def s5_disc_bwd(lam_re, lam_im, log_dt, d_ab, d_f):
    G, P = lam_re.shape

    def body(lr_ref, li_ref, ldt_ref, dab_ref, df_ref, glr_ref, gli_ref, gdt_ref):
        _, vjp = jax.vjp(_disc, lr_ref[...], li_ref[...], ldt_ref[...])
        glr, gli, gdt = vjp((dab_ref[0], dab_ref[1], df_ref[0], df_ref[1]))
        glr_ref[...] = glr
        gli_ref[...] = gli
        gdt_ref[...] = gdt

    vm = pl.BlockSpec(memory_space=pltpu.VMEM)
    return pl.pallas_call(
        body, name="s5_disc_bwd", in_specs=[vm] * 5, out_specs=[vm] * 3,
        out_shape=[_sds((G, P), F32), _sds((G, P), F32), _sds((G, 1), F32)],
    )(lam_re, lam_im, log_dt, d_ab, d_f)


def _group_mask(cw, nst):
    row = lax.broadcasted_iota(jnp.int32, (cw, 2 * nst), 0) // SSM_GROUP
    col = (lax.broadcasted_iota(jnp.int32, (cw, 2 * nst), 1) % nst) // SSM_STATE
    return row == col


def _spread(t, mask):
    reps = mask.shape[0] // t.shape[0]
    return jnp.where(mask, jnp.tile(t, (reps, 1)), 0.0).astype(BF16)


def _gather_groups(t, mask):
    t = jnp.where(mask, t, 0.0)
    out = t[0:SSM_GROUP]
    for g in range(1, t.shape[0] // SSM_GROUP):
        out = out + t[g * SSM_GROUP:(g + 1) * SSM_GROUP]
    return out


def _s5_operands(f_ref, br_ref, bi_ref, cr_ref, ci_ref, mask):
    fr, fi = f_ref[0], f_ref[1]
    br, bi = br_ref[...], bi_ref[...]
    bm = _spread(jnp.concatenate([fr * br - fi * bi, fr * bi + fi * br], axis=1), mask)
    cm = _spread(jnp.concatenate([cr_ref[...], -ci_ref[...]], axis=1), mask)
    return bm, cm


def _planes_put(ref, val):
    for c in range(ref.shape[0]):
        ref[c] = val[:, c * LANES:(c + 1) * LANES]


def _planes_get(ref):
    return jnp.concatenate([ref[c] for c in range(ref.shape[0])], axis=1)


def _rows_ld(ref, start, lo, hi):
    rows = pl.ds(start, SUBLANES, stride=SCAN_FOLD)
    return jnp.concatenate([ref[c, rows, :] for c in range(lo // LANES, hi // LANES)], axis=1)


def _rows_st(ref, start, lo, val):
    rows = pl.ds(start, SUBLANES, stride=SCAN_FOLD)
    for k in range(val.shape[1] // LANES):
        ref[lo // LANES + k, rows, :] = val[:, k * LANES:(k + 1) * LANES]


def _phases(ref, base, lo, hi):
    return [_rows_ld(ref, base + j, lo, hi) for j in range(SCAN_FOLD)]


def _row_bcast(v, r):
    return jnp.broadcast_to(v[r:r + 1, :], v.shape)


def _scan_fwd(xs, k_ref, nst):
    m = SCAN_FOLD
    ngroup = xs.shape[1] // (SUBLANES * m)
    row = lax.broadcasted_iota(jnp.int32, (SUBLANES, nst), 0)

    def step(t, carry):
        cr, ci = carry
        base = pl.multiple_of(t * (SUBLANES * m), SUBLANES * m)
        ar, ai = k_ref[16], k_ref[17]
        pr, pi = _phases(xs, base, 0, nst), _phases(xs, base, nst, 2 * nst)
        vr, vi = pr[0], pi[0]
        for j in range(1, m):
            vr, vi = pr[j] + ar * vr - ai * vi, pi[j] + ar * vi + ai * vr
        for n, sh in enumerate((1, 2, 4)):
            sr = pltpu.roll(vr, sh, 0)
            si = pltpu.roll(vi, sh, 0)
            mr, mi = k_ref[2 * n], k_ref[2 * n + 1]
            vr, vi = vr + mr * sr - mi * si, vi + mr * si + mi * sr
        qr, qi = k_ref[6], k_ref[7]
        vr, vi = vr + qr * cr - qi * ci, vi + qr * ci + qi * cr
        _rows_st(xs, base + m - 1, 0, vr)
        _rows_st(xs, base + m - 1, nst, vi)
        xr = jnp.where(row == 0, cr, pltpu.roll(vr, 1, 0))
        xi = jnp.where(row == 0, ci, pltpu.roll(vi, 1, 0))
        for j in range(m - 1):
            xr, xi = pr[j] + ar * xr - ai * xi, pi[j] + ar * xi + ai * xr
            _rows_st(xs, base + j, 0, xr)
            _rows_st(xs, base + j, nst, xi)
        return _row_bcast(vr, SUBLANES - 1), _row_bcast(vi, SUBLANES - 1)

    zero = jnp.zeros((SUBLANES, nst), F32)
    lax.fori_loop(0, ngroup, step, (zero, zero))


def _scan_bwd(g, xs, k_ref, nst):
    m = SCAN_FOLD
    ngroup = g.shape[1] // (SUBLANES * m)
    row = lax.broadcasted_iota(jnp.int32, (SUBLANES, nst), 0)

    def step(tt, carry):
        cr, ci, dar, dai = carry
        t = ngroup - 1 - tt
        base = pl.multiple_of(t * (SUBLANES * m), SUBLANES * m)
        ar, ai = k_ref[16], -k_ref[17]
        dr, di = _phases(g, base, 0, nst), _phases(g, base, nst, 2 * nst)
        wr, wi = dr[m - 1], di[m - 1]
        for j in range(m - 2, -1, -1):
            wr, wi = dr[j] + ar * wr - ai * wi, di[j] + ar * wi + ai * wr
        for n, sh in enumerate((1, 2, 4)):
            sr = pltpu.roll(wr, SUBLANES - sh, 0)
            si = pltpu.roll(wi, SUBLANES - sh, 0)
            mr, mi = k_ref[8 + 2 * n], k_ref[8 + 2 * n + 1]
            wr, wi = wr + mr * sr - mi * si, wi + mr * si + mi * sr
        qr, qi = k_ref[14], k_ref[15]
        wr, wi = wr + qr * cr - qi * ci, wi + qr * ci + qi * cr
        gr, gi = [None] * m, [None] * m
        gr[0], gi[0] = wr, wi
        nr = jnp.where(row == SUBLANES - 1, cr, pltpu.roll(wr, SUBLANES - 1, 0))
        ni = jnp.where(row == SUBLANES - 1, ci, pltpu.roll(wi, SUBLANES - 1, 0))
        for j in range(m - 1, 0, -1):
            nr, ni = dr[j] + ar * nr - ai * ni, di[j] + ar * ni + ai * nr
            gr[j], gi[j] = nr, ni
        for j in range(m):
            _rows_st(g, base + j, 0, gr[j])
            _rows_st(g, base + j, nst, gi[j])
        xr, xi = _phases(xs, base, 0, nst), _phases(xs, base, nst, 2 * nst)
        pbase = pl.multiple_of(jnp.maximum(t - 1, 0) * (SUBLANES * m), SUBLANES * m)
        live = (t > 0).astype(F32)
        lr = _row_bcast(_rows_ld(xs, pbase + m - 1, 0, nst), SUBLANES - 1) * live
        li = _row_bcast(_rows_ld(xs, pbase + m - 1, nst, 2 * nst), SUBLANES - 1) * live
        xmr = [jnp.where(row == 0, lr, pltpu.roll(xr[m - 1], 1, 0))] + xr[:m - 1]
        xmi = [jnp.where(row == 0, li, pltpu.roll(xi[m - 1], 1, 0))] + xi[:m - 1]
        for j in range(m):
            dar = dar + gr[j] * xmr[j] + gi[j] * xmi[j]
            dai = dai + gi[j] * xmr[j] - gr[j] * xmi[j]
        return _row_bcast(wr, 0), _row_bcast(wi, 0), dar, dai

    zero = jnp.zeros((SUBLANES, nst), F32)
    _, _, dar, dai = lax.fori_loop(0, ngroup, step, (zero, zero, zero, zero))
    return _colsum(dar), _colsum(dai)


def _s5_param_specs(cw, nst):
    hp = pl.BlockSpec((SSM_GROUP, nst), lambda b: (0, b))
    return [pl.BlockSpec((2, 1, nst), lambda b: (0, 0, b)), hp, hp, hp, hp,
            pl.BlockSpec((1, cw), lambda b: (0, b)),
            pl.BlockSpec((NCONST, SUBLANES, nst), lambda b: (0, 0, b))]


def s5_fwd(proj, params, nb):
    S = proj.shape[0]
    nst = params[1].shape[1] // nb
    cw = nst // SSM_STATE * SSM_GROUP

    def body(u_ref, f_ref, br_ref, bi_ref, cr_ref, ci_ref, d_ref, k_ref, z_ref, xsb_ref, zp_ref, xs):
        bm, cm = _s5_operands(f_ref, br_ref, bi_ref, cr_ref, ci_ref, _group_mask(cw, nst))
        u = u_ref[...]
        _planes_put(xs, jnp.dot(u.astype(BF16), bm, preferred_element_type=F32))
        _scan_fwd(xs, k_ref, nst)
        xsb = _planes_get(xs).astype(BF16)
        xsb_ref[...] = xsb
        y = lax.dot_general(xsb, cm, (((1,), (1,)), ((), ())), preferred_element_type=F32)
        y = y + d_ref[...] * u
        z_ref[...] = _gelu(y).astype(BF16)
        zp_ref[...] = _gelu_grad(y).astype(BF16)

    return pl.pallas_call(
        body, name="s5_fwd", grid=(nb,),
        in_specs=[pl.BlockSpec((S, cw), lambda b: (0, b))] + _s5_param_specs(cw, nst),
        out_specs=[pl.BlockSpec((S, cw), lambda b: (0, b)), pl.BlockSpec((S, 2 * nst), lambda b: (0, b)),
                   pl.BlockSpec((S, cw), lambda b: (0, b))],
        out_shape=[_sds((S, nb * cw), BF16), _sds((S, nb * 2 * nst), BF16), _sds((S, nb * cw), BF16)],
        scratch_shapes=[pltpu.VMEM((2 * nst // LANES, S, LANES), F32)],
        compiler_params=_params(("arbitrary",)),
    )(proj, *params)


def s5_bwd(proj, xsb_all, dz, zp, params, nb, after=()):
    S = proj.shape[0]
    nst = params[1].shape[1] // nb
    cw = nst // SSM_STATE * SSM_GROUP

    def body(u_ref, xsb_ref, dz_ref, zp_ref, f_ref, br_ref, bi_ref, cr_ref, ci_ref, d_ref, k_ref,
             du_ref, gbr_ref, gbi_ref, gcr_ref, gci_ref, gf_ref, gd_ref, ga_ref, xs, g):
        mask = _group_mask(cw, nst)
        bm, cm = _s5_operands(f_ref, br_ref, bi_ref, cr_ref, ci_ref, mask)
        u = u_ref[...]
        ub = u.astype(BF16)
        d = d_ref[...]
        xsb = xsb_ref[...]
        _planes_put(xs, xsb.astype(F32))
        dy = dz_ref[...].astype(F32) * zp_ref[...].astype(F32)
        gd_ref[...] = _colsum(dy * u)
        dyb = dy.astype(BF16)
        gc = _gather_groups(lax.dot_general(dyb, xsb, (((0,), (0,)), ((), ())),
                                            preferred_element_type=F32), mask)
        gcr_ref[...] = gc[:, :nst]
        gci_ref[...] = -gc[:, nst:]
        _planes_put(g, jnp.dot(dyb, cm, preferred_element_type=F32))
        ar, ai = _scan_bwd(g, xs, k_ref, nst)
        ga_ref[0, 0:1, :] = ar
        ga_ref[0, 1:2, :] = ai
        gb = _planes_get(g).astype(BF16)
        du = lax.dot_general(gb, bm, (((1,), (1,)), ((), ())), preferred_element_type=F32) + d * dy
        du_ref[...] = du.astype(BF16)
        gbb = _gather_groups(lax.dot_general(ub, gb, (((0,), (0,)), ((), ())),
                                             preferred_element_type=F32), mask)
        dr, di = gbb[:, :nst], gbb[:, nst:]
        fr, fi = f_ref[0], f_ref[1]
        br, bi = br_ref[...], bi_ref[...]
        gbr_ref[...] = fr * dr + fi * di
        gbi_ref[...] = fr * di - fi * dr
        gf_ref[0] = _colsum(dr * br + di * bi)
        gf_ref[1] = _colsum(di * br - dr * bi)

    hp = pl.BlockSpec((SSM_GROUP, nst), lambda b: (0, b))
    hp_sds = _sds((SSM_GROUP, nb * nst), F32)
    return pl.pallas_call(
        _with_after(body, 11, after), name="s5_bwd", grid=(nb,),
        in_specs=[pl.BlockSpec((S, cw), lambda b: (0, b)),
                  pl.BlockSpec((S, 2 * nst), lambda b: (0, b)),
                  pl.BlockSpec((S, cw), lambda b: (0, b)),
                  pl.BlockSpec((S, cw), lambda b: (0, b))] + _s5_param_specs(cw, nst)
        + [ANY] * len(after),
        out_specs=[pl.BlockSpec((S, cw), lambda b: (0, b)), hp, hp, hp, hp,
                   pl.BlockSpec((2, 1, nst), lambda b: (0, 0, b)),
                   pl.BlockSpec((1, cw), lambda b: (0, b)),
                   pl.BlockSpec((1, 2, nst), lambda b: (b, 0, 0))],
        out_shape=[_sds((S, nb * cw), BF16), hp_sds, hp_sds, hp_sds, hp_sds,
                   _sds((2, 1, nb * nst), F32), _sds((1, nb * cw), F32), _sds((nb, 2, nst), F32)],
        scratch_shapes=[pltpu.VMEM((2 * nst // LANES, S, LANES), F32)] * 2,
        compiler_params=_params(("arbitrary",)),
    )(proj, xsb_all, dz, zp, *params, *after)


def _shift_rows(v, k, row, down):
    n = v.shape[0]
    if down:
        return jnp.where(row >= k, pltpu.roll(v, k, 0), 0.0)
    return jnp.where(row < n - k, pltpu.roll(v, n - k, 0), 0.0)


def _window(v, gi, row, down):
    sums = []
    s = v
    for k in (1, 2, 4, 8):
        s = s + _shift_rows(s, k, row, down)
        sums.append(s)
    out = sums[3]
    for n in (2, 1, 0):
        out = jnp.where(gi == n, sums[n], out)
    return out


def pool_fwd(proj, col0, width, gw):
    S = proj.shape[0]
    cb0 = col0 // gw

    def body(u_ref, o_ref):
        gi = pl.program_id(0)
        u = u_ref[...]
        row = lax.broadcasted_iota(jnp.int32, u.shape, 0)
        w = jnp.left_shift(2, gi)
        count = jnp.minimum(row + 1, w).astype(F32)
        o_ref[...] = (_window(u, gi, row, True) / count - u).astype(BF16)

    return pl.pallas_call(
        body, name="pool_fwd", grid=(len(POOL_WINDOWS),),
        in_specs=[pl.BlockSpec((S, gw), lambda g: (0, cb0 + g))],
        out_specs=pl.BlockSpec((S, gw), lambda g: (0, g)),
        out_shape=_sds((S, width), BF16), compiler_params=_params(("arbitrary",)),
    )(proj)


def pool_bwd(dpooled, gw):
    S, width = dpooled.shape

    def body(d_ref, o_ref):
        gi = pl.program_id(0)
        d = d_ref[...]
        row = lax.broadcasted_iota(jnp.int32, d.shape, 0)
        w = jnp.left_shift(2, gi)
        count = jnp.minimum(row + 1, w).astype(F32)
        o_ref[...] = (_window(d / count, gi, row, False) - d).astype(BF16)

    return pl.pallas_call(
        body, name="pool_bwd", grid=(len(POOL_WINDOWS),),
        in_specs=[pl.BlockSpec((S, gw), lambda g: (0, g))],
        out_specs=pl.BlockSpec((S, gw), lambda g: (0, g)),
        out_shape=_sds((S, width), BF16), compiler_params=_params(("arbitrary",)),
    )(dpooled)


def _place():
    x, y, c = lax.axis_index("x"), lax.axis_index("y"), lax.axis_index("c")
    chips = [(1 - x, y), (x, 1 - y), (1 - x, 1 - y)]
    return x, y, c, chips


HBM = pl.BlockSpec(memory_space=pltpu.HBM)


GATHER_PIECES = 4
GATHER_SEMS = 1 + 12 * GATHER_PIECES


def _routed_gather_body(n):
    npc = GATHER_PIECES
    k_x, k_y = 1, 1 + 2 * npc
    k_xy, k_yx, k_sib = 1 + 4 * npc, 1 + 5 * npc, 1 + 6 * npc

    def body(*refs):
        ins, outs = refs[:n], refs[n:2 * n]
        send_sems, recv_sems, local_sems = refs[2 * n:]
        x, y, c, (xn, yn, dg) = _place()
        me, sibling = (x, y, c), (x, y, 1 - c)
        barrier = pltpu.get_barrier_semaphore()
        for peer in (sibling, (*xn, c), (*yn, c)):
            pl.semaphore_signal(barrier, inc=1, device_id=peer, device_id_type=MESH)
        pl.semaphore_wait(barrier, 3)

        def piece(i, dev, p):
            rows = ins[i].shape[0] // (2 * npc)
            return outs[i].at[4 * dev[0] + 2 * dev[1] + dev[2], pl.ds(p * rows, rows)]

        def copy(i, k, src, dst, to):
            return pltpu.make_async_remote_copy(src_ref=src, dst_ref=dst, send_sem=send_sems.at[i, k],
                                                recv_sem=recv_sems.at[i, k], device_id=to,
                                                device_id_type=MESH)

        started = []

        def go(cp):
            cp.start()
            started.append(cp)

        for i in range(n):
            rows = ins[i].shape[0] // (2 * npc)
            for q in range(2 * npc):
                py = (q + npc) % (2 * npc)
                go(copy(i, k_x + q, ins[i].at[pl.ds(q * rows, rows)], piece(i, me, q), (*xn, c)))
                go(copy(i, k_y + py, ins[i].at[pl.ds(py * rows, rows)], piece(i, me, py), (*yn, c)))
        for i in range(n):
            go(copy(i, 0, ins[i], outs[i].at[4 * x + 2 * y + c], sibling))
        mine = [pltpu.make_async_copy(ins[i], outs[i].at[4 * x + 2 * y + c], local_sems.at[i])
                for i in range(n)]
        for cp in mine:
            cp.start()

        def arrived(i, k, chip, p, onward, r):
            got = piece(i, (*chip, c), p)
            copy(i, k, got, got, me).wait_recv()
            if onward is not None:
                go(copy(i, onward[0], got, got, (*onward[1], c)))
            go(copy(i, k_sib + r, got, got, sibling))

        for i in range(n):
            for q in range(npc):
                arrived(i, k_x + q, xn, q, (k_xy + q, yn), q)
                arrived(i, k_y + npc + q, yn, npc + q, (k_yx + q, xn), 2 * npc + npc + q)
            for q in range(npc):
                arrived(i, k_x + npc + q, xn, npc + q, None, npc + q)
                arrived(i, k_y + q, yn, q, None, 2 * npc + q)
            for q in range(npc):
                arrived(i, k_xy + q, dg, q, None, 4 * npc + q)
                arrived(i, k_yx + q, dg, npc + q, None, 4 * npc + npc + q)
        for i in range(n):
            block = outs[i].at[4 * x + 2 * y + 1 - c]
            copy(i, 0, block, block, me).wait_recv()
            for j, chip in enumerate((xn, yn, dg)):
                for p in range(2 * npc):
                    got = piece(i, (*chip, 1 - c), p)
                    copy(i, k_sib + 2 * npc * j + p, got, got, me).wait_recv()
        for cp in started:
            cp.wait_send()
        for cp in mine:
            cp.wait()

    return body


def _on_sequencer(name, body, arrays, out_sds, sems, collective_id):
    ins = [jax.new_ref(a, memory_space=pltpu.MemorySpace.HBM) for a in arrays]
    outs = [jax.empty_ref(s, memory_space=pltpu.MemorySpace.HBM) for s in out_sds]

    @pl.kernel(mesh=plsc.ScalarSubcoreMesh(axis_name="sequencer", num_cores=1), name=name,
               scratch_types=tuple(sems),
               compiler_params=pltpu.CompilerParams(collective_id=collective_id))
    def launch(*sem_refs):
        body(*ins, *outs, *sem_refs)

    launch()
    return [o[...] for o in outs]


def seq_all_gather(name, shards, collective_id):
    n = len(shards)
    return _on_sequencer(
        name, _routed_gather_body(n), shards, [_sds((NDEV,) + s.shape, s.dtype) for s in shards],
        [pltpu.SemaphoreType.DMA((n, GATHER_SEMS)), pltpu.SemaphoreType.DMA((n, GATHER_SEMS)),
         pltpu.SemaphoreType.DMA((n,))], collective_id)


def pair_exchange(name, grads, collective_id):
    def plan(srcs, lands):
        x, y, c, _ = _place()
        return ([(i, q, srcs[i].at[2 * q + 1 - c], lands[i].at[q], (x, y, 1 - c))
                 for i in range(len(srcs)) for q in range(NCHIP)], [(x, y, 1 - c)])

    return _split_exchange(name, grads, [_sds((NCHIP,) + g.shape[1:], g.dtype) for g in grads],
                           plan, NCHIP, collective_id)


SEM = pl.BlockSpec(memory_space=pltpu.SEMAPHORE)


def _split_exchange(name, srcs, land_sds, plan, ncopy, collective_id):
    n = len(srcs)
    nsem = n * ncopy
    effect = pltpu.SideEffectType.DATAFLOW_SIDE_EFFECTING

    def descriptors(src_refs, land_refs, send_sems, recv_sems):
        copies, peers = plan(src_refs, land_refs)
        return [pltpu.make_async_remote_copy(src_ref=s, dst_ref=d, send_sem=send_sems[i * ncopy + k],
                                             recv_sem=recv_sems[i * ncopy + k], device_id=to,
                                             device_id_type=MESH) for (i, k, s, d, to) in copies], peers

    def start_body(*refs):
        src_refs, land_refs = refs[:n], refs[n:2 * n]
        send_sems, recv_sems = refs[2 * n:2 * n + nsem], refs[2 * n + nsem:2 * n + 2 * nsem]
        token = refs[-1]
        cps, peers = descriptors(src_refs, land_refs, send_sems, recv_sems)
        barrier = pltpu.get_barrier_semaphore()
        for peer in peers:
            pl.semaphore_signal(barrier, inc=1, device_id=peer, device_id_type=MESH)
        pl.semaphore_wait(barrier, len(peers))
        for cp in cps:
            cp.start()
        token[...] = jnp.zeros_like(token)

    lands = [pltpu.with_memory_space_constraint(lax.empty(s.shape, s.dtype), pltpu.HBM) for s in land_sds]
    srcs = [pltpu.with_memory_space_constraint(s, pltpu.HBM) for s in srcs]
    res = pl.pallas_call(
        start_body, name=name + "_start",
        out_shape=(pltpu.SemaphoreType.DMA(()),) * (2 * nsem)
        + tuple(pltpu.HBM(s.shape, s.dtype) for s in srcs)
        + tuple(pltpu.HBM(s.shape, s.dtype) for s in land_sds) + (_sds((SUBLANES, LANES), F32),),
        in_specs=[HBM] * (2 * n),
        out_specs=(SEM,) * (2 * nsem) + (HBM,) * (2 * n) + (pl.BlockSpec(memory_space=pltpu.VMEM),),
        input_output_aliases={i: 2 * nsem + i for i in range(2 * n)},
        compiler_params=pltpu.CompilerParams(has_side_effects=effect, collective_id=collective_id),
    )(*srcs, *lands)
    sems = res[:2 * nsem]
    thru = res[2 * nsem:2 * nsem + 2 * n]
    token = res[-1]

    def wait(after):
        def wait_body(*refs):
            src_refs, land_refs = refs[:n], refs[n:2 * n]
            cps, _ = descriptors(src_refs, land_refs, refs[2 * n:2 * n + nsem],
                                 refs[2 * n + nsem:2 * n + 2 * nsem])
            for cp in cps:
                cp.wait_send()
            for cp in cps:
                cp.wait_recv()

        out = pl.pallas_call(
            wait_body, name=name + "_wait",
            out_shape=tuple(pltpu.HBM(s.shape, s.dtype) for s in srcs)
            + tuple(pltpu.HBM(s.shape, s.dtype) for s in land_sds),
            in_specs=[HBM] * (2 * n) + [SEM] * (2 * nsem) + [pl.BlockSpec(memory_space=pl.ANY)],
            out_specs=(HBM,) * (2 * n),
            input_output_aliases={i: i for i in range(2 * n)},
            compiler_params=pltpu.CompilerParams(has_side_effects=effect),
        )(*thru, *sems, after)
        return list(out[:n]), list(out[n:])

    return token, wait


def pair_sum(name, grad, got, place):
    shp = grad.shape[1:]
    r, cdim = shp[-2], shp[-1]
    lead = int(math.prod(shp[:-2])) if len(shp) > 2 else 1
    g5 = grad.reshape(NCHIP, 2, lead * r, cdim)
    t4 = got.reshape(NCHIP, lead * r, cdim)
    R = lead * r
    tr = _tile(R, max(8, (1 << 20) // cdim))

    def body(p_ref, g_ref, t_ref, o_ref):
        o_ref[...] = (g_ref[0].astype(F32) + t_ref[...].astype(F32)).astype(o_ref.dtype)

    out = pl.pallas_call(
        body, name=name,
        grid_spec=pltpu.PrefetchScalarGridSpec(
            num_scalar_prefetch=1, grid=(NCHIP - 1, R // tr),
            in_specs=[pl.BlockSpec((1, 1, tr, cdim), lambda j, i, p: (p[1] ^ (j + 1), p[0], i, 0)),
                      pl.BlockSpec((1, tr, cdim), lambda j, i, p: (p[1] ^ (j + 1), i, 0))],
            out_specs=pl.BlockSpec((1, tr, cdim), lambda j, i, p: (p[1] ^ (j + 1), i, 0))),
        out_shape=_sds((NCHIP, R, cdim), grad.dtype),
        compiler_params=_params(("parallel", "parallel")),
    )(place, g5, t4)
    return out


def chip_exchange(name, parts, collective_id):
    def plan(srcs, lands):
        x, y, c, chips = _place()
        return ([(i, j, srcs[i].at[2 * chip[0] + chip[1]], lands[i].at[j], (*chip, c))
                 for i in range(len(srcs)) for j, chip in enumerate(chips)],
                [(*chip, c) for chip in chips])

    return _split_exchange(name, parts, [_sds((3,) + p.shape[1:], p.dtype) for p in parts],
                           plan, 3, collective_id)


def ada_fwd(c_row, w_ada, b_ada):
    D, cols = w_ada.shape

    def body(c_ref, w_ref, b_ref, mod_ref, call_ref, act8, part, s1, r1, s2, r2):
        x, y, c, _ = _place()
        me = 4 * x + 2 * y + c
        call_ref[me] = c_ref[...]
        cps = []
        for k in range(1, NDEV):
            to = (x ^ (k >> 2), y ^ ((k >> 1) & 1), c ^ (k & 1))
            cps.append(pltpu.make_async_remote_copy(
                src_ref=c_ref, dst_ref=call_ref.at[me], send_sem=s1.at[k - 1],
                recv_sem=r1.at[k - 1], device_id=to, device_id_type=MESH))
            cps[-1].start()
        for cp in cps:
            cp.wait()
        for b in range(NDEV):
            act8[b:b + 1, :] = call_ref[b]
        cv = act8[...]
        act = (cv * _sigmoid(cv)).astype(BF16)
        res = jnp.dot(act, w_ref[...].astype(BF16), preferred_element_type=F32)
        for b in range(NDEV):
            part[b] = res[b:b + 1, :]
        mod_ref[me] = part[me]
        cps = []
        for k in range(1, NDEV):
            to = (x ^ (k >> 2), y ^ ((k >> 1) & 1), c ^ (k & 1))
            dst = 4 * to[0] + 2 * to[1] + to[2]
            cps.append(pltpu.make_async_remote_copy(
                src_ref=part.at[dst], dst_ref=mod_ref.at[me], send_sem=s2.at[k - 1],
                recv_sem=r2.at[k - 1], device_id=to, device_id_type=MESH))
            cps[-1].start()
        for cp in cps:
            cp.wait()
        for b in range(NDEV):
            mod_ref[b] = mod_ref[b] + b_ref[b]

    vm = pl.BlockSpec(memory_space=pltpu.VMEM)
    return pl.pallas_call(
        body, name="ada_fwd", in_specs=[vm, vm, vm], out_specs=[vm, vm],
        out_shape=[_sds((NDEV, 1, cols), F32), _sds((NDEV, 1, D), F32)],
        scratch_shapes=[pltpu.VMEM((NDEV, D), F32), pltpu.VMEM((NDEV, 1, cols), F32),
                        pltpu.SemaphoreType.DMA((NDEV - 1,)), pltpu.SemaphoreType.DMA((NDEV - 1,)),
                        pltpu.SemaphoreType.DMA((NDEV - 1,)), pltpu.SemaphoreType.DMA((NDEV - 1,))],
        compiler_params=pltpu.CompilerParams(vmem_limit_bytes=VMEM_LIMIT),
    )(c_row, w_ada, b_ada.reshape(NDEV, 1, cols))


def _adamw_math(g, w, m, v):
    m2 = ADAM_B1 * m + (1.0 - ADAM_B1) * g
    v2 = ADAM_B2 * v + (1.0 - ADAM_B2) * (g * g)
    m_hat = m2 / (1.0 - ADAM_B1 ** ADAM_STEP)
    v_hat = v2 / (1.0 - ADAM_B2 ** ADAM_STEP)
    delta = -ADAM_LR * (m_hat / (jnp.sqrt(v_hat) + ADAM_EPS) + ADAM_WD * w)
    return delta, m2, v2


def adamw_sharded(name, grad8, pair4, got3, w, m, v, place, after=()):
    shape = w.shape
    cdim = shape[-1]
    R = int(math.prod(shape[:-1]))
    w2, m2, v2 = (t.reshape(R, cdim) for t in (w, m, v))
    tr = _tile(R, max(8, (1 << 19) // cdim))

    def body(q_ref, own_ref, sib_ref, t_ref, w_ref, m_ref, v_ref, g_out, d_out, m_out, v_out):
        g = own_ref[0].astype(F32) + sib_ref[0].astype(F32)
        for j in range(3):
            g = g + t_ref[j].astype(F32)
        d, mn, vn = _adamw_math(g, w_ref[...], m_ref[...], v_ref[...])
        g_out[...] = g
        d_out[...] = d
        m_out[...] = mn
        v_out[...] = vn

    spec = pl.BlockSpec((tr, cdim), lambda i, qr: (i, 0))
    outs = pl.pallas_call(
        _with_after(body, 7, after), name=name,
        grid_spec=pltpu.PrefetchScalarGridSpec(
            num_scalar_prefetch=1, grid=(R // tr,),
            in_specs=[pl.BlockSpec((1, tr, cdim), lambda i, qr: (qr[2], i, 0)),
                      pl.BlockSpec((1, tr, cdim), lambda i, qr: (qr[1], i, 0)),
                      pl.BlockSpec((3, tr, cdim), lambda i, qr: (0, i, 0)), spec, spec, spec]
            + [ANY] * len(after),
            out_specs=[spec] * 4),
        out_shape=[_sds((R, cdim), F32)] * 4,
        compiler_params=_params(("parallel",)),
    )(place, grad8.reshape(NDEV, R, cdim), pair4.reshape(NCHIP, R, cdim),
      got3.reshape(3, R, cdim), w2, m2, v2, *after)
    return [o.reshape(shape) for o in outs]


def sum_small(parts, after=()):
    R = parts.shape[1]

    def body(p_ref, g_out):
        g = p_ref[0]
        for j in range(1, NDEV):
            g = g + p_ref[j]
        g_out[...] = g

    return pl.pallas_call(
        _with_after(body, 1, after), name="sum_small", grid=(1,),
        in_specs=[pl.BlockSpec((NDEV, R, LANES), lambda i: (0, 0, 0))] + [ANY] * len(after),
        out_specs=pl.BlockSpec((R, LANES), lambda i: (0, 0)), out_shape=_sds((R, LANES), F32),
        compiler_params=_params(("arbitrary",)),
    )(parts, *after)


def adamw_natural(gs, ws, ms, vs):
    n = len(ws)
    nblk = 8
    big = [w.ndim == 4 and w.shape[1] % nblk == 0 for w in ws]

    def spec(w, is_big):
        if is_big:
            return pl.BlockSpec((1, w.shape[1] // nblk) + w.shape[2:], lambda i: (0, i, 0, 0))
        return pl.BlockSpec(w.shape, functools.partial(lambda i, nd: (0,) * nd, nd=w.ndim))

    def body(*refs):
        g_refs, w_refs, m_refs, v_refs = (refs[k * n:(k + 1) * n] for k in range(4))
        d_outs, m_outs, v_outs = (refs[(4 + k) * n:(5 + k) * n] for k in range(3))

        def update(p):
            d, mn, vn = _adamw_math(g_refs[p][...], w_refs[p][...], m_refs[p][...], v_refs[p][...])
            d_outs[p][...] = d
            m_outs[p][...] = mn
            v_outs[p][...] = vn

        for p in range(n):
            if big[p]:
                update(p)

        @pl.when(pl.program_id(0) == 0)
        def _():
            for p in range(n):
                if not big[p]:
                    update(p)

    specs = [spec(w, b) for w, b in zip(ws, big)]
    outs = pl.pallas_call(
        body, name="adamw_natural", grid=(nblk,), in_specs=specs * 4, out_specs=specs * 3,
        out_shape=[_sds(w.shape, F32) for w in ws] * 3,
        compiler_params=_params(("arbitrary",)),
    )(*gs, *ws, *ms, *vs)
    return outs[:n], outs[n:2 * n], outs[2 * n:]


def adamw_ada(c_all_t, dmod_all, w, m, v, my_dev):
    D, cols = w.shape
    tr = _tile(D, 256)

    def body(k_ref, c_ref, d_ref, w_ref, m_ref, v_ref, g_out, d_out, m_out, v_out):
        cv = c_ref[...]
        act = cv * _sigmoid(cv)
        dm = d_ref[...]
        g = act[:, 0:1] * dm[0:1, :]
        for b in range(1, NDEV):
            g = g + act[:, b:b + 1] * dm[b:b + 1, :]
        d, mn, vn = _adamw_math(g, w_ref[...], m_ref[...], v_ref[...])
        g_out[...] = g
        d_out[...] = d
        m_out[...] = mn
        v_out[...] = vn

    spec = pl.BlockSpec((tr, cols), lambda i, kr: (i, 0))
    return pl.pallas_call(
        body, name="adamw_ada",
        grid_spec=pltpu.PrefetchScalarGridSpec(
            num_scalar_prefetch=1, grid=(D // tr,),
            in_specs=[pl.BlockSpec((tr, NDEV), lambda i, kr: (i, 0)),
                      pl.BlockSpec((NDEV, cols), lambda i, kr: (0, kr[0])), spec, spec, spec],
            out_specs=[spec] * 4),
        out_shape=[_sds((D, cols), F32)] * 4,
        compiler_params=_params(("parallel",)),
    )(my_dev, c_all_t, dmod_all, w, m, v)


def _small_pack(parts):
    rows = []
    for p in parts:
        flat = p.reshape(-1)
        flat = jnp.pad(flat, (0, (-flat.shape[0]) % (SUBLANES * LANES)))
        rows.append(flat.reshape(-1, LANES))
    return jnp.concatenate(rows, axis=0)


def _small_unpack(buf, shapes):
    out, r = [], 0
    for s in shapes:
        n = int(math.prod(s))
        nr = -(-n // (SUBLANES * LANES)) * SUBLANES
        out.append(buf[r:r + nr].reshape(-1)[:n].reshape(s))
        r += nr
    return out


def kernel(x, c, w_ada, b_ada, w_in, lam_re, lam_im, log_dt, ssm_b_re, ssm_b_im, ssm_c_re, ssm_c_im, ssm_d, w_glu_val, w_glu_gate, w_pool, pool_scale, w_pool_out, w_out, ln1_g, ln1_b, w_ff1, w_ff2, ln2_g, ln2_b, loss_target, m_w_ada, m_b_ada, m_w_in, m_lam_re, m_lam_im, m_log_dt, m_ssm_b_re, m_ssm_b_im, m_ssm_c_re, m_ssm_c_im, m_ssm_d, m_w_glu_val, m_w_glu_gate, m_w_pool, m_pool_scale, m_w_pool_out, m_w_out, m_ln1_g, m_ln1_b, m_w_ff1, m_w_ff2, m_ln2_g, m_ln2_b, v_w_ada, v_b_ada, v_w_in, v_lam_re, v_lam_im, v_log_dt, v_ssm_b_re, v_ssm_b_im, v_ssm_c_re, v_ssm_c_im, v_ssm_d, v_w_glu_val, v_w_glu_gate, v_w_pool, v_pool_scale, v_w_pool_out, v_w_out, v_ln1_g, v_ln1_b, v_w_ff1, v_w_ff2, v_ln2_g, v_ln2_b):
    S, D = x.shape[1], x.shape[2]
    x2d, tgt = x[0], loss_target[0]
    W = D // 2
    G = W // SSM_GROUP
    P, H, GPB = SSM_STATE, SSM_GROUP, GROUPS_PER_BLOCK
    nblk = G // GPB
    gw = W // len(POOL_WINDOWS)
    ax, ay, ac = lax.axis_index("x"), lax.axis_index("y"), lax.axis_index("c")
    my_dev = (4 * ax + 2 * ay + ac).astype(jnp.int32).reshape(1)
    place = jnp.stack([ac, 2 * ax + ay, 4 * ax + 2 * ay + ac]).astype(jnp.int32)
    ts = _tile(S, 256)

    glu = jnp.concatenate([w_glu_val[0], w_glu_gate[0]]).astype(BF16)
    nwin = len(POOL_WINDOWS)
    shards = [w_in[0].astype(BF16), glu, w_pool[0].astype(BF16).reshape(nwin * gw // NDEV, gw),
              w_pool_out[0].astype(BF16),
              w_out[0].astype(BF16), w_ff1[0].astype(BF16), w_ff2[0].astype(BF16)]
    wg_in, wg_pool = seq_all_gather("gather_w_in", [shards[0], shards[2]], 1)
    wg_vg, wg_po, wg_out = seq_all_gather("gather_w_mix", [shards[1], shards[3], shards[4]], 2)
    (wg_ff1,) = seq_all_gather("gather_w_ff1", shards[5:6], 3)
    (wg_ff2,) = seq_all_gather("gather_w_ff2", shards[6:7], 11)
    wg_vg = wg_vg.reshape(2 * NDEV, W, D // NDEV)
    wg_pool = wg_pool.reshape(NDEV, nwin, gw // NDEV, gw)
    wp_full = jnp.transpose(wg_pool, (1, 0, 2, 3)).reshape(nwin, gw, gw)
    wout_full = wg_out.reshape(1, D, D)
    wff2_full = wg_ff2.reshape(1, 4 * D, D)

    small_names = [b_ada, lam_re, lam_im, log_dt, ssm_b_re, ssm_b_im, ssm_c_re, ssm_c_im, ssm_d,
                   pool_scale, ln1_g, ln1_b, ln2_g, ln2_b]
    small_m = [m_b_ada, m_lam_re, m_lam_im, m_log_dt, m_ssm_b_re, m_ssm_b_im, m_ssm_c_re, m_ssm_c_im,
               m_ssm_d, m_pool_scale, m_ln1_g, m_ln1_b, m_ln2_g, m_ln2_b]
    small_v = [v_b_ada, v_lam_re, v_lam_im, v_log_dt, v_ssm_b_re, v_ssm_b_im, v_ssm_c_re, v_ssm_c_im,
               v_ssm_d, v_pool_scale, v_ln1_g, v_ln1_b, v_ln2_g, v_ln2_b]

    mod, c_all = ada_fwd(c, w_ada[0], b_ada)
    mod = mod.reshape(6, 1, D)
    sh1, sc1, g1, sh2, sc2, g2 = (mod[i] for i in range(6))

    f2, kconst = s5_disc(lam_re[0], lam_im[0], log_dt[0].reshape(G, 1))
    kconst = kconst.reshape(NCONST, SUBLANES, G * P)
    f2r = f2.reshape(2, 1, G * P)
    bt_re = jnp.transpose(ssm_b_re[0], (2, 0, 1)).reshape(H, G * P)
    bt_im = jnp.transpose(ssm_b_im[0], (2, 0, 1)).reshape(H, G * P)
    ct_re = jnp.transpose(ssm_c_re[0], (1, 0, 2)).reshape(H, G * P)
    ct_im = jnp.transpose(ssm_c_im[0], (1, 0, 2)).reshape(H, G * P)
    s5_params = (f2r, bt_re, bt_im, ct_re, ct_im, ssm_d, kconst)

    def e1(t, b):
        xhat, _ = _ln_stats(t[0])
        return [xhat * (1.0 + b[0]) + b[1]], []
    (h1,) = _rowwise("ln_mod1", e1, S, ts, [(x2d, D, 0)], [sc1, sh1], [(D, BF16)], [])

    (proj,) = mm_nn("proj", h1, wg_in, F32, 2)
    z, xsb_all, zp = s5_fwd(proj, s5_params, nblk)
    (vt,) = mm_nn("glu", z, wg_vg, BF16, 4)
    pooled = pool_fwd(proj, W, W, gw)

    def pool_epi(vals, ex, outs):
        a = vals[0]
        outs[0][...] = a
        outs[1][...] = (a * ex[0][...]).astype(BF16)
    tmp = _tile(S, 1024)
    yp, ypool = _mm(
        "pool_mix", "nn", pooled, wp_full.astype(BF16), (S // tmp, nwin, 1),
        pl.BlockSpec((tmp, gw), lambda i, j, k: (i, j)), pl.BlockSpec((1, gw, gw), lambda i, j, k: (j, 0, 0)),
        [(_sds((S, W), F32), pl.BlockSpec((tmp, gw), lambda i, j, k: (i, j))),
         (_sds((S, W), BF16), pl.BlockSpec((tmp, gw), lambda i, j, k: (i, j)))],
        (tmp, gw), 1, gw, None, pool_epi,
        [(pool_scale, pl.BlockSpec((1, gw), lambda i, j, k: (0, j)))])
    (y_b,) = mm_nn("pool_out", ypool, wg_po, BF16, 4)

    cb = D // NDEV
    ga_cb, gb_cb = (2 * W) // cb, (2 * W + D) // cb
    mcb = 4
    wm = mcb * cb
    tsm = _tile(S, 256)

    def merge_call(name, fn, ins, n_out, after=()):
        def body(*refs):
            vals = [r[...].astype(F32) for r in refs[:len(ins)]]
            for r, v in zip(refs[len(ins):], fn(*vals)):
                r[...] = v.astype(r.dtype)
        return pl.pallas_call(
            _with_after(body, len(ins), after), name=name, grid=(S // tsm, NDEV // mcb),
            in_specs=[pl.BlockSpec((tsm, w), f) for (_, w, f) in ins] + [ANY] * len(after),
            out_specs=[pl.BlockSpec((tsm, w), lambda i, j: (i, j)) for (_, w) in n_out],
            out_shape=[_sds((S, cols), BF16) for (cols, _) in n_out],
            compiler_params=_params(("parallel", "parallel")),
        )(*[a for (a, _, _) in ins], *after)

    merge_ins = [(proj, wm, lambda i, j: (i, ga_cb // mcb + j)), (proj, wm, lambda i, j: (i, gb_cb // mcb + j)),
                 (vt, 2 * wm, lambda i, j: (i, j)), (y_b, wm, lambda i, j: (i, j))]

    def val_gate(vtv):
        return (jnp.concatenate([vtv[:, 2 * q * cb:(2 * q + 1) * cb] for q in range(mcb)], axis=1),
                jnp.concatenate([vtv[:, (2 * q + 1) * cb:(2 * q + 2) * cb] for q in range(mcb)], axis=1))

    def merge_f(ga, gb, vtv, yb):
        vv, tt = val_gate(vtv)
        return [_sigmoid(ga) * (vv * _sigmoid(tt)) + _sigmoid(gb) * yb]
    (merged,) = merge_call("merge", merge_f, merge_ins, [(D, wm)])

    (mix,) = mm_nn("mix_out", merged, wout_full, F32, 1)

    def e3(t, b):
        xv, mx = t
        g1v, l1g, l1b, sc2v, sh2v = b
        r1 = ALPHA * xv + g1v * mx
        xh1, _ = _ln_stats(r1)
        x1 = xh1 * l1g + l1b
        xh, _ = _ln_stats(x1)
        return [r1, xh * (1.0 + sc2v) + sh2v], []
    r1, h2 = _rowwise("post_mix", e3, S, ts, [(x2d, D, 0), (mix, D, 0)],
                      [g1, ln1_g, ln1_b, sc2, sh2], [(D, F32), (D, BF16)], [])

    def relu_epi(vals, ex, outs):
        outs[0][...] = jnp.maximum(vals[0], 0.0).astype(BF16)
    (rl,) = mm_nn("ff1", h2, wg_ff1, BF16, 1, epi=relu_epi)

    def square(a):
        return a * a
    (y2,) = mm_nn("ff2", rl, wff2_full, F32, 1, pro=square)

    def e4(t, b):
        r1v, y2v, tg = t
        g2v, l1g, l1b, l2g, l2b = b
        xh1, _ = _ln_stats(r1v)
        x1 = xh1 * l1g + l1b
        r2 = ALPHA * x1 + g2v * y2v
        xh2, rs2 = _ln_stats(r2)
        err = xh2 * l2g + l2b - tg
        dx2 = err * (1.0 / D)
        dr2 = _ln_bwd(dx2 * l2g, xh2, rs2)
        lsum = jnp.sum(_colsum(err * err), axis=1, keepdims=True) * (0.5 / D)
        return ([ALPHA * dr2, g2v * dr2],
                [jnp.broadcast_to(lsum, (1, LANES)), _colsum(dx2 * xh2), _colsum(dx2), _colsum(dr2 * y2v)])
    dx1a, dy2, loss_acc, g_ln2g, g_ln2b, d_g2 = _rowwise(
        "head", e4, S, ts, [(r1, D, 0), (y2, D, 0), (tgt, D, 0)], [g2, ln1_g, ln1_b, ln2_g, ln2_b],
        [(D, F32), (D, BF16)], [LANES, D, D, D])

    tn_ff = _tile(4 * D, 1024)

    def dff_epi(vals, ex, outs):
        outs[0][...] = (vals[0] * (2.0 * ex[0][...].astype(F32))).astype(BF16)
    tmf = _tile(S, 1024)
    (da1,) = mm_nt("d_ff2", dy2, wff2_full, BF16, 1, tn=tn_ff, epi=dff_epi,
                   extras=[(rl, pl.BlockSpec((tmf, tn_ff), lambda i, j, k: (i, j)))])
    gw_ff2 = mm_tn("gw_ff2", rl, dy2, BF16, NDEV, 0, pro=square)
    gw_ff1 = mm_tn("gw_ff1", h2, da1, BF16, NDEV, 1)
    tok, wait_pair_a = pair_exchange("pair_exchange_ff", [gw_ff2, gw_ff1], 4)
    (dh2,) = mm_nt("d_ff1", da1, wg_ff1, F32, 4, after=[tok])

    def e5(t, b):
        dh2v, r1v, dx1av, mx = t
        sc2v, l1g, l1b, g1v = b
        xh1, rs1 = _ln_stats(r1v)
        x1 = xh1 * l1g + l1b
        xh, rs = _ln_stats(x1)
        dx1 = dx1av + _ln_bwd(dh2v * (1.0 + sc2v), xh, rs)
        dr1 = _ln_bwd(dx1 * l1g, xh1, rs1)
        return ([ALPHA * dr1, g1v * dr1],
                [_colsum(dh2v * xh), _colsum(dh2v), _colsum(dx1 * xh1), _colsum(dx1), _colsum(dr1 * mx)])
    dxa, dmix, d_sc2, d_sh2, g_ln1g, g_ln1b, d_g1 = _rowwise(
        "post_mix_bwd", e5, S, ts, [(dh2, D, 0), (r1, D, 0), (dx1a, D, 0), (mix, D, 0)],
        [sc2, ln1_g, ln1_b, g1], [(D, F32), (D, BF16)], [D, D, D, D, D])

    (dmerged,) = mm_nt("d_mix_out", dmix, wout_full, BF16, 1)
    gw_out = mm_tn("gw_out", merged, dmix, BF16, NDEV, 0)
    grads_a, got_a = wait_pair_a(gw_out)
    parts_a = [pair_sum("pair_sum_ff%d" % i, g, t, place) for i, (g, t) in enumerate(zip(grads_a, got_a))]
    tok, wait_chip_a = chip_exchange("chip_exchange_ff", parts_a, 5)

    def merge_b(ga, gb, vtv, yb, dm):
        vv, tt = val_gate(vtv)
        sa, sb, st = _sigmoid(ga), _sigmoid(gb), _sigmoid(tt)
        dya = dm * sa
        dv, dt = dya * st, dya * vv * st * (1.0 - st)
        dvt_tile = jnp.concatenate([t[:, q * cb:(q + 1) * cb] for q in range(mcb) for t in (dv, dt)], axis=1)
        return [dm * (vv * st) * sa * (1.0 - sa), dm * yb * sb * (1.0 - sb), dvt_tile, dm * sb]
    dga, dgb_, dvt, dy_b = merge_call(
        "merge_bwd", merge_b, merge_ins + [(dmerged, wm, lambda i, j: (i, j))],
        [(D, wm), (D, wm), (2 * D, 2 * wm), (D, wm)], after=[tok])

    (dypool,) = mm_nt("d_pool_out", dy_b, wg_po, F32, NDEV)
    gw_po = mm_tn("gw_pool_out", ypool, dy_b, BF16, NDEV, 4)

    def e7(t, b):
        return [t[0] * b[0]], [_colsum(t[0] * t[1])]
    dyp, g_pscale = _rowwise("pool_scale_bwd", e7, S, ts, [(dypool, W, 0), (yp, W, 0)],
                             [pool_scale], [(W, BF16)], [W])
    (dpooled,) = _mm(
        "d_pool_mix", "nt", dyp, wp_full.astype(BF16), (S // tmp, nwin, 1),
        pl.BlockSpec((tmp, gw), lambda i, j, k: (i, j)), pl.BlockSpec((1, gw, gw), lambda i, j, k: (j, 0, 0)),
        [(_sds((S, W), F32), pl.BlockSpec((tmp, gw), lambda i, j, k: (i, j)))], (tmp, gw), 1, gw)
    tkp = _tile(S, 2048)
    gw_pool = _mm(
        "gw_pool", "tn", pooled, dyp, (nwin, 1, S // tkp),
        pl.BlockSpec((tkp, gw), lambda i, j, k: (k, i)), pl.BlockSpec((tkp, gw), lambda i, j, k: (k, i)),
        [(_sds((nwin, gw, gw), BF16), pl.BlockSpec((1, gw, gw), lambda i, j, k: (i, 0, 0)))],
        (gw, gw), 1, gw, stacked_out=True)[0]
    du_pool = pool_bwd(dpooled, gw)

    (dz,) = mm_nt("d_glu", dvt, wg_vg, BF16, 2 * NDEV)
    gw_vg = mm_tn("gw_glu", z, dvt, BF16, 2 * NDEV, 4)
    gw_pool_st = jnp.transpose(gw_pool.reshape(nwin, NDEV, gw // NDEV, gw), (1, 0, 2, 3))
    grads_b = [gw_out, gw_po, gw_pool_st, gw_vg.reshape(NDEV, 2, W, D // NDEV)]
    tok, wait_pair_b = pair_exchange("pair_exchange_mix", grads_b, 6)
    du_ssm, g_bt_re, g_bt_im, g_ct_re, g_ct_im, g_f, g_d, g_a = s5_bwd(
        proj, xsb_all, dz, zp, s5_params, nblk, after=[tok])
    grads_b, got_b = wait_pair_b(du_ssm)
    parts_b = [pair_sum("pair_sum_mix%d" % i, g, t, place) for i, (g, t) in enumerate(zip(grads_b, got_b))]
    tok, wait_chip_b = chip_exchange("chip_exchange_mix", parts_b, 7)

    dproj = jnp.concatenate([du_ssm, du_pool, dga, dgb_], axis=1)
    gw_in = mm_tn("gw_in", h1, dproj, BF16, NDEV, 1, after=[tok])
    tok, wait_pair_c = pair_exchange("pair_exchange_in", [gw_in], 8)
    (dh1,) = mm_nt("d_proj", dproj, wg_in, F32, 4, after=[tok])
    grads_c, got_c = wait_pair_c(dh1)
    parts_c = [pair_sum("pair_sum_in", grads_c[0], got_c[0], place)]
    tok, wait_chip_c = chip_exchange("chip_exchange_in", parts_c, 9)

    def e10(t, b):
        dh1v, xv, dxav = t
        xh, rs = _ln_stats(xv)
        return ([dxav + _ln_bwd(dh1v * (1.0 + b[0]), xh, rs)],
                [_colsum(dh1v * xh), _colsum(dh1v)])
    grad_x, d_sc1, d_sh1 = _rowwise("ln_mod1_bwd", e10, S, ts, [(dh1, D, 0), (x2d, D, 0), (dxa, D, 0)],
                                    [sc1], [(D, F32)], [D, D], after=[tok])

    g_b_re = jnp.transpose(g_bt_re.reshape(H, G, P), (1, 0, 2))
    g_b_im = jnp.transpose(g_bt_im.reshape(H, G, P), (1, 0, 2))
    g_c_re = jnp.transpose(g_ct_re.reshape(H, G, P), (1, 0, 2))
    g_c_im = jnp.transpose(g_ct_im.reshape(H, G, P), (1, 0, 2))
    d_ab = jnp.transpose(g_a.reshape(nblk, 2, GPB, P), (1, 0, 2, 3)).reshape(2, G, P)
    g_lr, g_li, g_ldt = s5_disc_bwd(lam_re[0], lam_im[0], log_dt[0].reshape(G, 1), d_ab,
                                    g_f.reshape(2, G, P))

    dmod = jnp.concatenate([d_sh1, d_sc1, d_g1, d_sh2, d_sc2, d_g2], axis=1)
    small_g = [dmod, g_lr, g_li, g_ldt, g_b_re, g_b_im, g_c_re, g_c_im, g_d, g_pscale,
               g_ln1g, g_ln1b, g_ln2g, g_ln2b, loss_acc]
    packed_g = _small_pack(small_g)
    (parts_all,) = seq_all_gather("gather_small", [packed_g], 10)
    glu_w = jnp.stack([w_glu_val[0], w_glu_gate[0]])
    glu_m = jnp.stack([m_w_glu_val[0], m_w_glu_gate[0]])
    glu_v = jnp.stack([v_w_glu_val[0], v_w_glu_gate[0]])
    wmv = [(w_ff2[0], m_w_ff2[0], v_w_ff2[0]), (w_ff1[0], m_w_ff1[0], v_w_ff1[0]),
           (w_out[0], m_w_out[0], v_w_out[0]), (w_pool_out[0], m_w_pool_out[0], v_w_pool_out[0]),
           (w_pool[0], m_w_pool[0], v_w_pool[0]), (glu_w, glu_m, glu_v)]
    _, got3_a = wait_chip_a(packed_g)
    upd = [adamw_sharded("adamw_%d" % i, g, p, t, w, m, v, place)
           for i, (g, p, t, (w, m, v)) in enumerate(zip(grads_a, got_a, got3_a, wmv[:2]))]
    _, got3_b = wait_chip_b(upd[-1][0])
    upd += [adamw_sharded("adamw_%d" % (2 + i), g, p, t, w, m, v, place)
            for i, (g, p, t, (w, m, v)) in enumerate(zip(grads_b, got_b, got3_b, wmv[2:]))]
    u_ff2, u_ff1, u_out, u_po, u_pool, u_glu = upd

    gsum = sum_small(parts_all, after=[upd[-1][0]])
    def swap_b(ts_):
        return [jnp.swapaxes(t, 2, 3) if i in (4, 5) else t for i, t in enumerate(ts_)]

    sg = _small_unpack(gsum, [t.shape for t in swap_b(small_names)] + [(1, LANES)])
    loss, sg = sg[-1][0, 0], sg[:-1]
    sd, sm, sv = adamw_natural(sg, swap_b(small_names), swap_b(small_m), swap_b(small_v))
    sg, sd, sm, sv = swap_b(sg), swap_b(sd), swap_b(sm), swap_b(sv)

    nmod = 6 * D
    dmod_all = parts_all[:, :nmod // LANES, :].reshape(NDEV, nmod)
    c_all_t = jnp.transpose(c_all.reshape(NDEV, D))
    ada_out = adamw_ada(c_all_t, dmod_all, w_ada[0], m_w_ada[0], v_w_ada[0], my_dev)
    _, got3_c = wait_chip_c(ada_out[0])
    u_in = adamw_sharded("adamw_6", grads_c[0], got_c[0], got3_c[0], w_in[0], m_w_in[0], v_w_in[0], place)

    def pick(k):
        return [ada_out[k][None], sg_sd[k][0], u_in[k][None]] + [t for t in sg_sd[k][1:9]] + \
               [u_glu[k][0][None], u_glu[k][1][None], u_pool[k][None], sg_sd[k][9], u_po[k][None],
                u_out[k][None], sg_sd[k][10], sg_sd[k][11], u_ff1[k][None], u_ff2[k][None],
                sg_sd[k][12], sg_sd[k][13]]

    sg_sd = [sg, sd, sm, sv]
    return (loss, grad_x[None], *pick(0), *pick(1), *pick(2), *pick(3))
```

```python
import functools
import math

import jax
import jax.numpy as jnp
from jax import lax
from jax.experimental import pallas as pl
from jax.experimental.pallas import tpu as pltpu
from jax.experimental.pallas import tpu_sc as plsc

F32 = jnp.float32
BF16 = jnp.bfloat16
MESH = pl.DeviceIdType.MESH
NDEV = 8
NCHIP = 4

SSM_GROUP = 16
SSM_STATE = 64
GROUPS_PER_BLOCK = 8
POOL_WINDOWS = (2, 4, 8, 16)
LN_EPS = 1e-5
ALPHA = 2.0 ** 0.25
ADAM_LR, ADAM_B1, ADAM_B2, ADAM_EPS, ADAM_WD, ADAM_STEP = 0.001, 0.9, 0.999, 1e-08, 0.01, 10
SUBLANES = 8
LANES = 128
VMEM_LIMIT = 56 * 1024 * 1024


def _params(sem=None, vmem=VMEM_LIMIT):
    return pltpu.CompilerParams(dimension_semantics=sem, vmem_limit_bytes=vmem)


def _tile(n, pref):
    if n <= pref:
        return n
    t = 1 << (pref.bit_length() - 1)
    while n % t:
        t //= 2
    return t


def _cast_epi(vals, ex, outs):
    c = vals[0].shape[1]
    for s, v in enumerate(vals):
        outs[0][:, s * c:(s + 1) * c] = v.astype(outs[0].dtype)


ANY = pl.BlockSpec(memory_space=pl.ANY)


def _with_after(body, n_in, after):
    if not after:
        return body
    n_af = len(after)

    def wrapped(*refs):
        return body(*refs[:n_in], *refs[n_in + n_af:])
    return wrapped


def _mm(name, kind, a, b, grid, a_spec, b_spec, outs, acc_shape, nsub=1, c=None,
        pro=None, epi=None, extras=(), stacked_out=False, after=()):
    nk = grid[2]
    n_ex, n_out = len(extras), len(outs)

    def finish(vals, ex, out_refs):
        if epi is not None:
            epi(vals, ex, out_refs)
        elif stacked_out:
            for s, v in enumerate(vals):
                out_refs[0][s] = v.astype(out_refs[0].dtype)
        else:
            _cast_epi(vals, ex, out_refs)

    def body(*refs):
        mm_step(refs[0], refs[1], refs[2:2 + n_ex], refs[2 + n_ex:2 + n_ex + n_out], refs[-1])

    def mm_step(a_ref, b_ref, ex, out_refs, acc):
        k = pl.program_id(2)
        av = a_ref[...]
        if pro is not None:
            av = pro(av)
        if kind == "nn":
            prods = [jnp.dot(av, b_ref[s], preferred_element_type=F32) for s in range(nsub)]
        elif kind == "nt":
            t = None
            for s in range(nsub):
                d = lax.dot_general(av[:, s * c:(s + 1) * c], b_ref[s], (((1,), (1,)), ((), ())),
                                    preferred_element_type=F32)
                t = d if t is None else t + d
            prods = [t]
        else:
            t = lax.dot_general(av, b_ref[...], (((0,), (0,)), ((), ())), preferred_element_type=F32)
            prods = [t[:, s * c:(s + 1) * c] for s in range(nsub)] if stacked_out else [t]
        if nk == 1:
            finish(prods, ex, out_refs)
            return
        w = prods[0].shape[1]

        @pl.when(k == 0)
        def _():
            for s, p in enumerate(prods):
                acc[:, s * w:(s + 1) * w] = p

        @pl.when(jnp.logical_and(k > 0, k < nk - 1))
        def _():
            for s, p in enumerate(prods):
                acc[:, s * w:(s + 1) * w] += p

        @pl.when(k == nk - 1)
        def _():
            finish([acc[:, s * w:(s + 1) * w] + p for s, p in enumerate(prods)], ex, out_refs)

    return pl.pallas_call(
        _with_after(body, 2 + n_ex, after), name=name, grid=grid,
        in_specs=[a_spec, b_spec] + [e[1] for e in extras] + [ANY] * len(after),
        out_specs=[o[1] for o in outs],
        out_shape=[o[0] for o in outs],
        scratch_shapes=[pltpu.VMEM(acc_shape, F32)] if nk > 1 else [],
        compiler_params=_params(("parallel", "parallel", "arbitrary")),
    )(a, b, *[e[0] for e in extras], *after)


def _sds(shape, dtype):
    return jax.ShapeDtypeStruct(shape, dtype)


def mm_nn(name, a, b3, out_dtype, nsub, tm=1024, tk=2048, tn=None, pro=None, epi=None,
          extras=(), after=()):
    M = a.shape[0]
    nb, K, cdim = b3.shape
    tm, tk = _tile(M, tm), _tile(K, tk)
    if nb == 1:
        tn = _tile(cdim, tn or 1024)
        nsub, c, nj = 1, tn, cdim // tn
        b_spec = pl.BlockSpec((1, tk, tn), lambda i, j, k: (0, k, j))
        N = cdim
    else:
        c, nj, tn = cdim, nb // nsub, nsub * cdim
        b_spec = pl.BlockSpec((nsub, tk, cdim), lambda i, j, k: (j, k, 0))
        N = nb * cdim
    a_spec = pl.BlockSpec((tm, tk), lambda i, j, k: (i, k))
    grid = (M // tm, nj, K // tk)
    outs = [(_sds((M, N), out_dtype), pl.BlockSpec((tm, tn), lambda i, j, k: (i, j)))]
    return _mm(name, "nn", a, b3, grid, a_spec, b_spec, outs, (tm, tn), nsub, c, pro, epi, extras,
               after=after)


def mm_nt(name, a, b3, out_dtype, nsub, tm=1024, tn=1024, epi=None, extras=(), after=()):
    M = a.shape[0]
    nb, N, cdim = b3.shape
    tm, tn = _tile(M, tm), _tile(N, tn)
    if nb == 1:
        tk = _tile(cdim, 2048)
        nsub, c, nk = 1, tk, cdim // tk
        b_spec = pl.BlockSpec((1, tn, tk), lambda i, j, k: (0, j, k))
    else:
        c, nk, tk = cdim, nb // nsub, nsub * cdim
        b_spec = pl.BlockSpec((nsub, tn, cdim), lambda i, j, k: (k, j, 0))
    a_spec = pl.BlockSpec((tm, tk), lambda i, j, k: (i, k))
    grid = (M // tm, N // tn, nk)
    outs = [(_sds((M, N), out_dtype), pl.BlockSpec((tm, tn), lambda i, j, k: (i, j)))]
    return _mm(name, "nt", a, b3, grid, a_spec, b_spec, outs, (tm, tn), nsub, c, None, epi, extras,
               after=after)


def mm_tn(name, a, b, out_dtype, nb, nsub, tma=1024, tk=2048, pro=None, after=()):
    S, Ka = a.shape
    N = b.shape[1]
    tk, tma = _tile(S, tk), _tile(Ka, tma)
    a_spec = pl.BlockSpec((tk, tma), lambda i, j, k: (k, i))
    if nsub == 0:
        tn = _tile(N, 1024)
        res = _mm(name, "tn", a, b, (Ka // tma, N // tn, S // tk), a_spec,
                  pl.BlockSpec((tk, tn), lambda i, j, k: (k, j)),
                  [(_sds((Ka, N), out_dtype), pl.BlockSpec((tma, tn), lambda i, j, k: (i, j)))],
                  (tma, tn), 1, tn, pro, None, (), after=after)[0]
        return res.reshape(nb, Ka // nb, N)
    c = N // nb
    tn = nsub * c
    outs = [(_sds((nb, Ka, c), out_dtype), pl.BlockSpec((nsub, tma, c), lambda i, j, k: (j, i, 0)))]
    return _mm(name, "tn", a, b, (Ka // tma, nb // nsub, S // tk), a_spec,
               pl.BlockSpec((tk, tn), lambda i, j, k: (k, j)), outs, (tma, tn), nsub, c,
               pro, None, (), stacked_out=True, after=after)[0]


def _rowwise(name, fn, S, ts, tiled, bcast, tiled_out, acc_out, after=()):
    nt, nb, no, na = len(tiled), len(bcast), len(tiled_out), len(acc_out)

    def body(*refs):
        tin = [r[...] for r in refs[:nt]]
        bin_ = [r[...] for r in refs[nt:nt + nb]]
        o_refs = refs[nt + nb:nt + nb + no]
        a_refs = refs[nt + nb + no:]
        touts, aouts = fn(tin, bin_)
        for r, v in zip(o_refs, touts):
            r[...] = v.astype(r.dtype)
        i = pl.program_id(0)

        @pl.when(i == 0)
        def _():
            for r, v in zip(a_refs, aouts):
                r[...] = v

        @pl.when(i > 0)
        def _():
            for r, v in zip(a_refs, aouts):
                r[...] += v

    in_specs = [pl.BlockSpec((ts, w), functools.partial(lambda i, cb: (i, cb), cb=cb))
                for (_, w, cb) in tiled]
    in_specs += [pl.BlockSpec(b.shape, lambda i: (0, 0)) for b in bcast]
    out_specs = [pl.BlockSpec((ts, w), lambda i: (i, 0)) for (w, _) in tiled_out]
    out_specs += [pl.BlockSpec((1, w), lambda i: (0, 0)) for w in acc_out]
    out_shape = [_sds((S, w), d) for (w, d) in tiled_out] + [_sds((1, w), F32) for w in acc_out]
    return pl.pallas_call(
        _with_after(body, nt + nb, after), name=name, grid=(S // ts,),
        in_specs=in_specs + [ANY] * len(after), out_specs=out_specs,
        out_shape=out_shape, compiler_params=_params(("arbitrary",)),
    )(*[t[0] for t in tiled], *bcast, *after)


def _ln_stats(v):
    mu = jnp.mean(v, axis=-1, keepdims=True)
    vc = v - mu
    var = jnp.mean(vc * vc, axis=-1, keepdims=True)
    rstd = lax.rsqrt(var + LN_EPS)
    return vc * rstd, rstd


def _ln_bwd(dxhat, xhat, rstd):
    return rstd * (dxhat - jnp.mean(dxhat, axis=-1, keepdims=True)
                   - xhat * jnp.mean(dxhat * xhat, axis=-1, keepdims=True))


def _colsum(v):
    return jnp.sum(v, axis=0, keepdims=True)


def _sigmoid(v):
    return 1.0 / (1.0 + jnp.exp(-v))


_GELU_C = math.sqrt(2.0 / math.pi)


def _gelu(v):
    return 0.5 * v * (1.0 + jnp.tanh(_GELU_C * (v + 0.044715 * v * v * v)))


def _gelu_grad(v):
    t = jnp.tanh(_GELU_C * (v + 0.044715 * v * v * v))
    return 0.5 * (1.0 + t) + 0.5 * v * (1.0 - t * t) * _GELU_C * (1.0 + 3 * 0.044715 * v * v)


def _disc(lr, li, ldt):
    dt = jnp.exp(ldt)
    mag = jnp.exp(lr * dt)
    ang = li * dt
    ab_re = mag * jnp.cos(ang)
    ab_im = mag * jnp.sin(ang)
    num_re = ab_re - 1.0
    num_im = ab_im
    den = lr * lr + li * li
    f_re = (num_re * lr + num_im * li) / den
    f_im = (num_im * lr - num_re * li) / den
    return ab_re, ab_im, f_re, f_im


def _cmul(ar, ai, br, bi):
    return ar * br - ai * bi, ar * bi + ai * br


SCAN_FOLD = 4
NCONST = 18


def s5_disc(lam_re, lam_im, log_dt):
    G, P = lam_re.shape

    def body(lr_ref, li_ref, ldt_ref, f_ref, k_ref):
        ab_re, ab_im, f_re, f_im = _disc(lr_ref[...], li_ref[...], ldt_ref[...])
        f_ref[0] = f_re
        f_ref[1] = f_im
        fr, fi = ab_re, ab_im
        for _ in range(SCAN_FOLD - 1):
            fr, fi = _cmul(fr, fi, ab_re, ab_im)
        pr, pi = [fr], [fi]
        for _ in range(SUBLANES - 1):
            nr, ni = _cmul(pr[-1], pi[-1], fr, fi)
            pr.append(nr)
            pi.append(ni)
        zero = jnp.zeros_like(ab_re)
        for r in range(SUBLANES):
            k_ref[16, r] = ab_re
            k_ref[17, r] = ab_im
        for n, sh in enumerate((1, 2, 4)):
            for r in range(SUBLANES):
                k_ref[2 * n, r] = pr[sh - 1] if r >= sh else zero
                k_ref[2 * n + 1, r] = pi[sh - 1] if r >= sh else zero
                k_ref[8 + 2 * n, r] = pr[sh - 1] if r + sh < SUBLANES else zero
                k_ref[8 + 2 * n + 1, r] = -pi[sh - 1] if r + sh < SUBLANES else zero
        for r in range(SUBLANES):
            k_ref[6, r] = pr[r]
            k_ref[7, r] = pi[r]
            k_ref[14, r] = pr[SUBLANES - 1 - r]
            k_ref[15, r] = -pi[SUBLANES - 1 - r]

    vm = pl.BlockSpec(memory_space=pltpu.VMEM)
    return pl.pallas_call(
        body, name="s5_disc", in_specs=[vm, vm, vm], out_specs=[vm, vm],
        out_shape=[_sds((2, G, P), F32), _sds((NCONST, SUBLANES, G, P), F32)],
    )(lam_re, lam_im, log_dt)


def s5_disc_bwd(lam_re, lam_im, log_dt, d_ab, d_f):
    G, P = lam_re.shape

    def body(lr_ref, li_ref, ldt_ref, dab_ref, df_ref, glr_ref, gli_ref, gdt_ref):
        _, vjp = jax.vjp(_disc, lr_ref[...], li_ref[...], ldt_ref[...])
        glr, gli, gdt = vjp((dab_ref[0], dab_ref[1], df_ref[0], df_ref[1]))
        glr_ref[...] = glr
        gli_ref[...] = gli
        gdt_ref[...] = gdt

    vm = pl.BlockSpec(memory_space=pltpu.VMEM)
    return pl.pallas_call(
        body, name="s5_disc_bwd", in_specs=[vm] * 5, out_specs=[vm] * 3,
        out_shape=[_sds((G, P), F32), _sds((G, P), F32), _sds((G, 1), F32)],
    )(lam_re, lam_im, log_dt, d_ab, d_f)


def _group_mask(cw, nst):
    row = lax.broadcasted_iota(jnp.int32, (cw, 2 * nst), 0) // SSM_GROUP
    col = (lax.broadcasted_iota(jnp.int32, (cw, 2 * nst), 1) % nst) // SSM_STATE
    return row == col


def _spread(t, mask):
    reps = mask.shape[0] // t.shape[0]
    return jnp.where(mask, jnp.tile(t, (reps, 1)), 0.0).astype(BF16)


def _gather_groups(t, mask):
    t = jnp.where(mask, t, 0.0)
    out = t[0:SSM_GROUP]
    for g in range(1, t.shape[0] // SSM_GROUP):
        out = out + t[g * SSM_GROUP:(g + 1) * SSM_GROUP]
    return out


def _s5_operands(f_ref, br_ref, bi_ref, cr_ref, ci_ref, mask):
    fr, fi = f_ref[0], f_ref[1]
    br, bi = br_ref[...], bi_ref[...]
    bm = _spread(jnp.concatenate([fr * br - fi * bi, fr * bi + fi * br], axis=1), mask)
    cm = _spread(jnp.concatenate([cr_ref[...], -ci_ref[...]], axis=1), mask)
    return bm, cm


def _planes_put(ref, val):
    for c in range(ref.shape[0]):
        ref[c] = val[:, c * LANES:(c + 1) * LANES]


def _planes_get(ref):
    return jnp.concatenate([ref[c] for c in range(ref.shape[0])], axis=1)


def _rows_ld(ref, start, lo, hi):
    rows = pl.ds(start, SUBLANES, stride=SCAN_FOLD)
    return jnp.concatenate([ref[c, rows, :] for c in range(lo // LANES, hi // LANES)], axis=1)


def _rows_st(ref, start, lo, val):
    rows = pl.ds(start, SUBLANES, stride=SCAN_FOLD)
    for k in range(val.shape[1] // LANES):
        ref[lo // LANES + k, rows, :] = val[:, k * LANES:(k + 1) * LANES]


def _phases(ref, base, lo, hi):
    return [_rows_ld(ref, base + j, lo, hi) for j in range(SCAN_FOLD)]


def _row_bcast(v, r):
    return jnp.broadcast_to(v[r:r + 1, :], v.shape)


def _scan_fwd(xs, k_ref, nst):
    m = SCAN_FOLD
    ngroup = xs.shape[1] // (SUBLANES * m)
    row = lax.broadcasted_iota(jnp.int32, (SUBLANES, nst), 0)

    def step(t, carry):
        cr, ci = carry
        base = pl.multiple_of(t * (SUBLANES * m), SUBLANES * m)
        ar, ai = k_ref[16], k_ref[17]
        pr, pi = _phases(xs, base, 0, nst), _phases(xs, base, nst, 2 * nst)
        vr, vi = pr[0], pi[0]
        for j in range(1, m):
            vr, vi = pr[j] + ar * vr - ai * vi, pi[j] + ar * vi + ai * vr
        for n, sh in enumerate((1, 2, 4)):
            sr = pltpu.roll(vr, sh, 0)
            si = pltpu.roll(vi, sh, 0)
            mr, mi = k_ref[2 * n], k_ref[2 * n + 1]
            vr, vi = vr + mr * sr - mi * si, vi + mr * si + mi * sr
        qr, qi = k_ref[6], k_ref[7]
        vr, vi = vr + qr * cr - qi * ci, vi + qr * ci + qi * cr
        _rows_st(xs, base + m - 1, 0, vr)
        _rows_st(xs, base + m - 1, nst, vi)
        xr = jnp.where(row == 0, cr, pltpu.roll(vr, 1, 0))
        xi = jnp.where(row == 0, ci, pltpu.roll(vi, 1, 0))
        for j in range(m - 1):
            xr, xi = pr[j] + ar * xr - ai * xi, pi[j] + ar * xi + ai * xr
            _rows_st(xs, base + j, 0, xr)
            _rows_st(xs, base + j, nst, xi)
        return _row_bcast(vr, SUBLANES - 1), _row_bcast(vi, SUBLANES - 1)

    zero = jnp.zeros((SUBLANES, nst), F32)
    lax.fori_loop(0, ngroup, step, (zero, zero))


def _scan_bwd(g, xs, k_ref, nst):
    m = SCAN_FOLD
    ngroup = g.shape[1] // (SUBLANES * m)
    row = lax.broadcasted_iota(jnp.int32, (SUBLANES, nst), 0)

    def step(tt, carry):
        cr, ci, dar, dai = carry
        t = ngroup - 1 - tt
        base = pl.multiple_of(t * (SUBLANES * m), SUBLANES * m)
        ar, ai = k_ref[16], -k_ref[17]
        dr, di = _phases(g, base, 0, nst), _phases(g, base, nst, 2 * nst)
        wr, wi = dr[m - 1], di[m - 1]
        for j in range(m - 2, -1, -1):
            wr, wi = dr[j] + ar * wr - ai * wi, di[j] + ar * wi + ai * wr
        for n, sh in enumerate((1, 2, 4)):
            sr = pltpu.roll(wr, SUBLANES - sh, 0)
            si = pltpu.roll(wi, SUBLANES - sh, 0)
            mr, mi = k_ref[8 + 2 * n], k_ref[8 + 2 * n + 1]
            wr, wi = wr + mr * sr - mi * si, wi + mr * si + mi * sr
        qr, qi = k_ref[14], k_ref[15]
        wr, wi = wr + qr * cr - qi * ci, wi + qr * ci + qi * cr
        gr, gi = [None] * m, [None] * m
        gr[0], gi[0] = wr, wi
        nr = jnp.where(row == SUBLANES - 1, cr, pltpu.roll(wr, SUBLANES - 1, 0))
        ni = jnp.where(row == SUBLANES - 1, ci, pltpu.roll(wi, SUBLANES - 1, 0))
        for j in range(m - 1, 0, -1):
            nr, ni = dr[j] + ar * nr - ai * ni, di[j] + ar * ni + ai * nr
            gr[j], gi[j] = nr, ni
        for j in range(m):
            _rows_st(g, base + j, 0, gr[j])
            _rows_st(g, base + j, nst, gi[j])
        xr, xi = _phases(xs, base, 0, nst), _phases(xs, base, nst, 2 * nst)
        pbase = pl.multiple_of(jnp.maximum(t - 1, 0) * (SUBLANES * m), SUBLANES * m)
        live = (t > 0).astype(F32)
        lr = _row_bcast(_rows_ld(xs, pbase + m - 1, 0, nst), SUBLANES - 1) * live
        li = _row_bcast(_rows_ld(xs, pbase + m - 1, nst, 2 * nst), SUBLANES - 1) * live
        xmr = [jnp.where(row == 0, lr, pltpu.roll(xr[m - 1], 1, 0))] + xr[:m - 1]
        xmi = [jnp.where(row == 0, li, pltpu.roll(xi[m - 1], 1, 0))] + xi[:m - 1]
        for j in range(m):
            dar = dar + gr[j] * xmr[j] + gi[j] * xmi[j]
            dai = dai + gi[j] * xmr[j] - gr[j] * xmi[j]
        return _row_bcast(wr, 0), _row_bcast(wi, 0), dar, dai

    zero = jnp.zeros((SUBLANES, nst), F32)
    _, _, dar, dai = lax.fori_loop(0, ngroup, step, (zero, zero, zero, zero))
    return _colsum(dar), _colsum(dai)


def _s5_param_specs(cw, nst):
    hp = pl.BlockSpec((SSM_GROUP, nst), lambda b: (0, b))
    return [pl.BlockSpec((2, 1, nst), lambda b: (0, 0, b)), hp, hp, hp, hp,
            pl.BlockSpec((1, cw), lambda b: (0, b)),
            pl.BlockSpec((NCONST, SUBLANES, nst), lambda b: (0, 0, b))]


def s5_fwd(proj, params, nb):
    S = proj.shape[0]
    nst = params[1].shape[1] // nb
    cw = nst // SSM_STATE * SSM_GROUP

    def body(u_ref, f_ref, br_ref, bi_ref, cr_ref, ci_ref, d_ref, k_ref, z_ref, xsb_ref, zp_ref, xs):
        bm, cm = _s5_operands(f_ref, br_ref, bi_ref, cr_ref, ci_ref, _group_mask(cw, nst))
        u = u_ref[...]
        _planes_put(xs, jnp.dot(u.astype(BF16), bm, preferred_element_type=F32))
        _scan_fwd(xs, k_ref, nst)
        xsb = _planes_get(xs).astype(BF16)
        xsb_ref[...] = xsb
        y = lax.dot_general(xsb, cm, (((1,), (1,)), ((), ())), preferred_element_type=F32)
        y = y + d_ref[...] * u
        z_ref[...] = _gelu(y).astype(BF16)
        zp_ref[...] = _gelu_grad(y).astype(BF16)

    return pl.pallas_call(
        body, name="s5_fwd", grid=(nb,),
        in_specs=[pl.BlockSpec((S, cw), lambda b: (0, b))] + _s5_param_specs(cw, nst),
        out_specs=[pl.BlockSpec((S, cw), lambda b: (0, b)), pl.BlockSpec((S, 2 * nst), lambda b: (0, b)),
                   pl.BlockSpec((S, cw), lambda b: (0, b))],
        out_shape=[_sds((S, nb * cw), BF16), _sds((S, nb * 2 * nst), BF16), _sds((S, nb * cw), BF16)],
        scratch_shapes=[pltpu.VMEM((2 * nst // LANES, S, LANES), F32)],
        compiler_params=_params(("arbitrary",)),
    )(proj, *params)


def s5_bwd(proj, xsb_all, dz, zp, params, nb, after=()):
    S = proj.shape[0]
    nst = params[1].shape[1] // nb
    cw = nst // SSM_STATE * SSM_GROUP

    def body(u_ref, xsb_ref, dz_ref, zp_ref, f_ref, br_ref, bi_ref, cr_ref, ci_ref, d_ref, k_ref,
             du_ref, gbr_ref, gbi_ref, gcr_ref, gci_ref, gf_ref, gd_ref, ga_ref, xs, g):
        mask = _group_mask(cw, nst)
        bm, cm = _s5_operands(f_ref, br_ref, bi_ref, cr_ref, ci_ref, mask)
        u = u_ref[...]
        ub = u.astype(BF16)
        d = d_ref[...]
        xsb = xsb_ref[...]
        _planes_put(xs, xsb.astype(F32))
        dy = dz_ref[...].astype(F32) * zp_ref[...].astype(F32)
        gd_ref[...] = _colsum(dy * u)
        dyb = dy.astype(BF16)
        gc = _gather_groups(lax.dot_general(dyb, xsb, (((0,), (0,)), ((), ())),
                                            preferred_element_type=F32), mask)
        gcr_ref[...] = gc[:, :nst]
        gci_ref[...] = -gc[:, nst:]
        _planes_put(g, jnp.dot(dyb, cm, preferred_element_type=F32))
        ar, ai = _scan_bwd(g, xs, k_ref, nst)
        ga_ref[0, 0:1, :] = ar
        ga_ref[0, 1:2, :] = ai
        gb = _planes_get(g).astype(BF16)
        du = lax.dot_general(gb, bm, (((1,), (1,)), ((), ())), preferred_element_type=F32) + d * dy
        du_ref[...] = du.astype(BF16)
        gbb = _gather_groups(lax.dot_general(ub, gb, (((0,), (0,)), ((), ())),
                                             preferred_element_type=F32), mask)
        dr, di = gbb[:, :nst], gbb[:, nst:]
        fr, fi = f_ref[0], f_ref[1]
        br, bi = br_ref[...], bi_ref[...]
        gbr_ref[...] = fr * dr + fi * di
        gbi_ref[...] = fr * di - fi * dr
        gf_ref[0] = _colsum(dr * br + di * bi)
        gf_ref[1] = _colsum(di * br - dr * bi)

    hp = pl.BlockSpec((SSM_GROUP, nst), lambda b: (0, b))
    hp_sds = _sds((SSM_GROUP, nb * nst), F32)
    return pl.pallas_call(
        _with_after(body, 11, after), name="s5_bwd", grid=(nb,),
        in_specs=[pl.BlockSpec((S, cw), lambda b: (0, b)),
                  pl.BlockSpec((S, 2 * nst), lambda b: (0, b)),
                  pl.BlockSpec((S, cw), lambda b: (0, b)),
                  pl.BlockSpec((S, cw), lambda b: (0, b))] + _s5_param_specs(cw, nst)
        + [ANY] * len(after),
        out_specs=[pl.BlockSpec((S, cw), lambda b: (0, b)), hp, hp, hp, hp,
                   pl.BlockSpec((2, 1, nst), lambda b: (0, 0, b)),
                   pl.BlockSpec((1, cw), lambda b: (0, b)),
                   pl.BlockSpec((1, 2, nst), lambda b: (b, 0, 0))],
        out_shape=[_sds((S, nb * cw), BF16), hp_sds, hp_sds, hp_sds, hp_sds,
                   _sds((2, 1, nb * nst), F32), _sds((1, nb * cw), F32), _sds((nb, 2, nst), F32)],
        scratch_shapes=[pltpu.VMEM((2 * nst // LANES, S, LANES), F32)] * 2,
        compiler_params=_params(("arbitrary",)),
    )(proj, xsb_all, dz, zp, *params, *after)


def _shift_rows(v, k, row, down):
    n = v.shape[0]
    if down:
        return jnp.where(row >= k, pltpu.roll(v, k, 0), 0.0)
    return jnp.where(row < n - k, pltpu.roll(v, n - k, 0), 0.0)


def _window(v, gi, row, down):
    sums = []
    s = v
    for k in (1, 2, 4, 8):
        s = s + _shift_rows(s, k, row, down)
        sums.append(s)
    out = sums[3]
    for n in (2, 1, 0):
        out = jnp.where(gi == n, sums[n], out)
    return out


def pool_fwd(proj, col0, width, gw):
    S = proj.shape[0]
    cb0 = col0 // gw

    def body(u_ref, o_ref):
        gi = pl.program_id(0)
        u = u_ref[...]
        row = lax.broadcasted_iota(jnp.int32, u.shape, 0)
        w = jnp.left_shift(2, gi)
        count = jnp.minimum(row + 1, w).astype(F32)
        o_ref[...] = (_window(u, gi, row, True) / count - u).astype(BF16)

    return pl.pallas_call(
        body, name="pool_fwd", grid=(len(POOL_WINDOWS),),
        in_specs=[pl.BlockSpec((S, gw), lambda g: (0, cb0 + g))],
        out_specs=pl.BlockSpec((S, gw), lambda g: (0, g)),
        out_shape=_sds((S, width), BF16), compiler_params=_params(("arbitrary",)),
    )(proj)


def pool_bwd(dpooled, gw):
    S, width = dpooled.shape

    def body(d_ref, o_ref):
        gi = pl.program_id(0)
        d = d_ref[...]
        row = lax.broadcasted_iota(jnp.int32, d.shape, 0)
        w = jnp.left_shift(2, gi)
        count = jnp.minimum(row + 1, w).astype(F32)
        o_ref[...] = (_window(d / count, gi, row, False) - d).astype(BF16)

    return pl.pallas_call(
        body, name="pool_bwd", grid=(len(POOL_WINDOWS),),
        in_specs=[pl.BlockSpec((S, gw), lambda g: (0, g))],
        out_specs=pl.BlockSpec((S, gw), lambda g: (0, g)),
        out_shape=_sds((S, width), BF16), compiler_params=_params(("arbitrary",)),
    )(dpooled)


def _place():
    x, y, c = lax.axis_index("x"), lax.axis_index("y"), lax.axis_index("c")
    chips = [(1 - x, y), (x, 1 - y), (1 - x, 1 - y)]
    return x, y, c, chips


HBM = pl.BlockSpec(memory_space=pltpu.HBM)


GATHER_PIECES = 2
GATHER_SEMS = 1 + 12 * GATHER_PIECES


def _routed_gather_body(n):
    npc = GATHER_PIECES
    k_x, k_y = 1, 1 + 2 * npc
    k_xy, k_yx, k_sib = 1 + 4 * npc, 1 + 5 * npc, 1 + 6 * npc

    def body(*refs):
        ins, outs = refs[:n], refs[n:2 * n]
        send_sems, recv_sems, local_sems = refs[2 * n:]
        x, y, c, (xn, yn, dg) = _place()
        me, sibling = (x, y, c), (x, y, 1 - c)
        barrier = pltpu.get_barrier_semaphore()
        for peer in (sibling, (*xn, c), (*yn, c)):
            pl.semaphore_signal(barrier, inc=1, device_id=peer, device_id_type=MESH)
        pl.semaphore_wait(barrier, 3)

        def piece(i, dev, p):
            rows = ins[i].shape[0] // (2 * npc)
            return outs[i].at[4 * dev[0] + 2 * dev[1] + dev[2], pl.ds(p * rows, rows)]

        def copy(i, k, src, dst, to):
            return pltpu.make_async_remote_copy(src_ref=src, dst_ref=dst, send_sem=send_sems.at[i, k],
                                                recv_sem=recv_sems.at[i, k], device_id=to,
                                                device_id_type=MESH)

        started = []

        def go(cp):
            cp.start()
            started.append(cp)

        for i in range(n):
            rows = ins[i].shape[0] // (2 * npc)
            for q in range(2 * npc):
                py = (q + npc) % (2 * npc)
                go(copy(i, k_x + q, ins[i].at[pl.ds(q * rows, rows)], piece(i, me, q), (*xn, c)))
                go(copy(i, k_y + py, ins[i].at[pl.ds(py * rows, rows)], piece(i, me, py), (*yn, c)))
        for i in range(n):
            go(copy(i, 0, ins[i], outs[i].at[4 * x + 2 * y + c], sibling))
        mine = [pltpu.make_async_copy(ins[i], outs[i].at[4 * x + 2 * y + c], local_sems.at[i])
                for i in range(n)]
        for cp in mine:
            cp.start()

        def arrived(i, k, chip, p, onward, r):
            got = piece(i, (*chip, c), p)
            copy(i, k, got, got, me).wait_recv()
            if onward is not None:
                go(copy(i, onward[0], got, got, (*onward[1], c)))
            go(copy(i, k_sib + r, got, got, sibling))

        for i in range(n):
            for q in range(npc):
                arrived(i, k_x + q, xn, q, (k_xy + q, yn), q)
                arrived(i, k_y + npc + q, yn, npc + q, (k_yx + q, xn), 2 * npc + npc + q)
            for q in range(npc):
                arrived(i, k_x + npc + q, xn, npc + q, None, npc + q)
                arrived(i, k_y + q, yn, q, None, 2 * npc + q)
            for q in range(npc):
                arrived(i, k_xy + q, dg, q, None, 4 * npc + q)
                arrived(i, k_yx + q, dg, npc + q, None, 4 * npc + npc + q)
        for i in range(n):
            block = outs[i].at[4 * x + 2 * y + 1 - c]
            copy(i, 0, block, block, me).wait_recv()
            for j, chip in enumerate((xn, yn, dg)):
                for p in range(2 * npc):
                    got = piece(i, (*chip, 1 - c), p)
                    copy(i, k_sib + 2 * npc * j + p, got, got, me).wait_recv()
        for cp in started:
            cp.wait_send()
        for cp in mine:
            cp.wait()

    return body


def _on_sequencer(name, body, arrays, out_sds, sems, collective_id):
    ins = [jax.new_ref(a, memory_space=pltpu.MemorySpace.HBM) for a in arrays]
    outs = [jax.empty_ref(s, memory_space=pltpu.MemorySpace.HBM) for s in out_sds]

    @pl.kernel(mesh=plsc.ScalarSubcoreMesh(axis_name="sequencer", num_cores=1), name=name,
               scratch_types=tuple(sems),
               compiler_params=pltpu.CompilerParams(collective_id=collective_id))
    def launch(*sem_refs):
        body(*ins, *outs, *sem_refs)

    launch()
    return [o[...] for o in outs]


def seq_all_gather(name, shards, collective_id):
    n = len(shards)
    return _on_sequencer(
        name, _routed_gather_body(n), shards, [_sds((NDEV,) + s.shape, s.dtype) for s in shards],
        [pltpu.SemaphoreType.DMA((n, GATHER_SEMS)), pltpu.SemaphoreType.DMA((n, GATHER_SEMS)),
         pltpu.SemaphoreType.DMA((n,))], collective_id)


def pair_exchange(name, grads, collective_id):
    def plan(srcs, lands):
        x, y, c, _ = _place()
        return ([(i, q, srcs[i].at[2 * q + 1 - c], lands[i].at[q], (x, y, 1 - c))
                 for i in range(len(srcs)) for q in range(NCHIP)], [(x, y, 1 - c)])

    return _split_exchange(name, grads, [_sds((NCHIP,) + g.shape[1:], g.dtype) for g in grads],
                           plan, NCHIP, collective_id)


SEM = pl.BlockSpec(memory_space=pltpu.SEMAPHORE)


def _split_exchange(name, srcs, land_sds, plan, ncopy, collective_id):
    n = len(srcs)
    nsem = n * ncopy
    effect = pltpu.SideEffectType.DATAFLOW_SIDE_EFFECTING

    def descriptors(src_refs, land_refs, send_sems, recv_sems):
        copies, peers = plan(src_refs, land_refs)
        return [pltpu.make_async_remote_copy(src_ref=s, dst_ref=d, send_sem=send_sems[i * ncopy + k],
                                             recv_sem=recv_sems[i * ncopy + k], device_id=to,
                                             device_id_type=MESH) for (i, k, s, d, to) in copies], peers

    def start_body(*refs):
        src_refs, land_refs = refs[:n], refs[n:2 * n]
        send_sems, recv_sems = refs[2 * n:2 * n + nsem], refs[2 * n + nsem:2 * n + 2 * nsem]
        token = refs[-1]
        cps, peers = descriptors(src_refs, land_refs, send_sems, recv_sems)
        barrier = pltpu.get_barrier_semaphore()
        for peer in peers:
            pl.semaphore_signal(barrier, inc=1, device_id=peer, device_id_type=MESH)
        pl.semaphore_wait(barrier, len(peers))
        for cp in cps:
            cp.start()
        token[...] = jnp.zeros_like(token)

    lands = [pltpu.with_memory_space_constraint(lax.empty(s.shape, s.dtype), pltpu.HBM) for s in land_sds]
    srcs = [pltpu.with_memory_space_constraint(s, pltpu.HBM) for s in srcs]
    res = pl.pallas_call(
        start_body, name=name + "_start",
        out_shape=(pltpu.SemaphoreType.DMA(()),) * (2 * nsem)
        + tuple(pltpu.HBM(s.shape, s.dtype) for s in srcs)
        + tuple(pltpu.HBM(s.shape, s.dtype) for s in land_sds) + (_sds((SUBLANES, LANES), F32),),
        in_specs=[HBM] * (2 * n),
        out_specs=(SEM,) * (2 * nsem) + (HBM,) * (2 * n) + (pl.BlockSpec(memory_space=pltpu.VMEM),),
        input_output_aliases={i: 2 * nsem + i for i in range(2 * n)},
        compiler_params=pltpu.CompilerParams(has_side_effects=effect, collective_id=collective_id),
    )(*srcs, *lands)
    sems = res[:2 * nsem]
    thru = res[2 * nsem:2 * nsem + 2 * n]
    token = res[-1]

    def wait(after):
        def wait_body(*refs):
            src_refs, land_refs = refs[:n], refs[n:2 * n]
            cps, _ = descriptors(src_refs, land_refs, refs[2 * n:2 * n + nsem],
                                 refs[2 * n + nsem:2 * n + 2 * nsem])
            for cp in cps:
                cp.wait_send()
            for cp in cps:
                cp.wait_recv()

        out = pl.pallas_call(
            wait_body, name=name + "_wait",
            out_shape=tuple(pltpu.HBM(s.shape, s.dtype) for s in srcs)
            + tuple(pltpu.HBM(s.shape, s.dtype) for s in land_sds),
            in_specs=[HBM] * (2 * n) + [SEM] * (2 * nsem) + [pl.BlockSpec(memory_space=pl.ANY)],
            out_specs=(HBM,) * (2 * n),
            input_output_aliases={i: i for i in range(2 * n)},
            compiler_params=pltpu.CompilerParams(has_side_effects=effect),
        )(*thru, *sems, after)
        return list(out[:n]), list(out[n:])

    return token, wait


def pair_sum(name, grad, got, place):
    shp = grad.shape[1:]
    r, cdim = shp[-2], shp[-1]
    lead = int(math.prod(shp[:-2])) if len(shp) > 2 else 1
    g5 = grad.reshape(NCHIP, 2, lead * r, cdim)
    t4 = got.reshape(NCHIP, lead * r, cdim)
    R = lead * r
    tr = _tile(R, max(8, (1 << 21) // cdim))

    def body(p_ref, g_ref, t_ref, o_ref):
        o_ref[...] = (g_ref[0].astype(F32) + t_ref[...].astype(F32)).astype(o_ref.dtype)

    out = pl.pallas_call(
        body, name=name,
        grid_spec=pltpu.PrefetchScalarGridSpec(
            num_scalar_prefetch=1, grid=(NCHIP - 1, R // tr),
            in_specs=[pl.BlockSpec((1, 1, tr, cdim), lambda j, i, p: (p[1] ^ (j + 1), p[0], i, 0)),
                      pl.BlockSpec((1, tr, cdim), lambda j, i, p: (p[1] ^ (j + 1), i, 0))],
            out_specs=pl.BlockSpec((1, tr, cdim), lambda j, i, p: (p[1] ^ (j + 1), i, 0))),
        out_shape=_sds((NCHIP, R, cdim), grad.dtype),
        compiler_params=_params(("parallel", "parallel")),
    )(place, g5, t4)
    return out


def chip_exchange(name, parts, collective_id):
    def plan(srcs, lands):
        x, y, c, chips = _place()
        return ([(i, j, srcs[i].at[2 * chip[0] + chip[1]], lands[i].at[j], (*chip, c))
                 for i in range(len(srcs)) for j, chip in enumerate(chips)],
                [(*chip, c) for chip in chips])

    return _split_exchange(name, parts, [_sds((3,) + p.shape[1:], p.dtype) for p in parts],
                           plan, 3, collective_id)


def ada_fwd(c_row, w_ada, b_ada):
    D, cols = w_ada.shape

    def body(c_ref, w_ref, b_ref, mod_ref, call_ref, act8, part, s1, r1, s2, r2):
        x, y, c, _ = _place()
        me = 4 * x + 2 * y + c
        call_ref[me] = c_ref[...]
        cps = []
        for k in range(1, NDEV):
            to = (x ^ (k >> 2), y ^ ((k >> 1) & 1), c ^ (k & 1))
            cps.append(pltpu.make_async_remote_copy(
                src_ref=c_ref, dst_ref=call_ref.at[me], send_sem=s1.at[k - 1],
                recv_sem=r1.at[k - 1], device_id=to, device_id_type=MESH))
            cps[-1].start()
        for cp in cps:
            cp.wait()
        for b in range(NDEV):
            act8[b:b + 1, :] = call_ref[b]
        cv = act8[...]
        act = (cv * _sigmoid(cv)).astype(BF16)
        res = jnp.dot(act, w_ref[...].astype(BF16), preferred_element_type=F32)
        for b in range(NDEV):
            part[b] = res[b:b + 1, :]
        mod_ref[me] = part[me]
        cps = []
        for k in range(1, NDEV):
            to = (x ^ (k >> 2), y ^ ((k >> 1) & 1), c ^ (k & 1))
            dst = 4 * to[0] + 2 * to[1] + to[2]
            cps.append(pltpu.make_async_remote_copy(
                src_ref=part.at[dst], dst_ref=mod_ref.at[me], send_sem=s2.at[k - 1],
                recv_sem=r2.at[k - 1], device_id=to, device_id_type=MESH))
            cps[-1].start()
        for cp in cps:
            cp.wait()
        for b in range(NDEV):
            mod_ref[b] = mod_ref[b] + b_ref[b]

    vm = pl.BlockSpec(memory_space=pltpu.VMEM)
    return pl.pallas_call(
        body, name="ada_fwd", in_specs=[vm, vm, vm], out_specs=[vm, vm],
        out_shape=[_sds((NDEV, 1, cols), F32), _sds((NDEV, 1, D), F32)],
        scratch_shapes=[pltpu.VMEM((NDEV, D), F32), pltpu.VMEM((NDEV, 1, cols), F32),
                        pltpu.SemaphoreType.DMA((NDEV - 1,)), pltpu.SemaphoreType.DMA((NDEV - 1,)),
                        pltpu.SemaphoreType.DMA((NDEV - 1,)), pltpu.SemaphoreType.DMA((NDEV - 1,))],
        compiler_params=pltpu.CompilerParams(vmem_limit_bytes=VMEM_LIMIT),
    )(c_row, w_ada, b_ada.reshape(NDEV, 1, cols))


def _adamw_math(g, w, m, v):
    m2 = ADAM_B1 * m + (1.0 - ADAM_B1) * g
    v2 = ADAM_B2 * v + (1.0 - ADAM_B2) * (g * g)
    m_hat = m2 / (1.0 - ADAM_B1 ** ADAM_STEP)
    v_hat = v2 / (1.0 - ADAM_B2 ** ADAM_STEP)
    delta = -ADAM_LR * (m_hat / (jnp.sqrt(v_hat) + ADAM_EPS) + ADAM_WD * w)
    return delta, m2, v2


def adamw_sharded(name, grad8, pair4, got3, w, m, v, place, after=()):
    shape = w.shape
    cdim = shape[-1]
    R = int(math.prod(shape[:-1]))
    w2, m2, v2 = (t.reshape(R, cdim) for t in (w, m, v))
    tr = _tile(R, max(8, (1 << 19) // cdim))

    def body(q_ref, own_ref, sib_ref, t_ref, w_ref, m_ref, v_ref, g_out, d_out, m_out, v_out):
        g = own_ref[0].astype(F32) + sib_ref[0].astype(F32)
        for j in range(3):
            g = g + t_ref[j].astype(F32)
        d, mn, vn = _adamw_math(g, w_ref[...], m_ref[...], v_ref[...])
        g_out[...] = g
        d_out[...] = d
        m_out[...] = mn
        v_out[...] = vn

    spec = pl.BlockSpec((tr, cdim), lambda i, qr: (i, 0))
    outs = pl.pallas_call(
        _with_after(body, 7, after), name=name,
        grid_spec=pltpu.PrefetchScalarGridSpec(
            num_scalar_prefetch=1, grid=(R // tr,),
            in_specs=[pl.BlockSpec((1, tr, cdim), lambda i, qr: (qr[2], i, 0)),
                      pl.BlockSpec((1, tr, cdim), lambda i, qr: (qr[1], i, 0)),
                      pl.BlockSpec((3, tr, cdim), lambda i, qr: (0, i, 0)), spec, spec, spec]
            + [ANY] * len(after),
            out_specs=[spec] * 4),
        out_shape=[_sds((R, cdim), F32)] * 4,
        compiler_params=_params(("parallel",)),
    )(place, grad8.reshape(NDEV, R, cdim), pair4.reshape(NCHIP, R, cdim),
      got3.reshape(3, R, cdim), w2, m2, v2, *after)
    return [o.reshape(shape) for o in outs]


def sum_small(parts, after=()):
    R = parts.shape[1]

    def body(p_ref, g_out):
        g = p_ref[0]
        for j in range(1, NDEV):
            g = g + p_ref[j]
        g_out[...] = g

    return pl.pallas_call(
        _with_after(body, 1, after), name="sum_small", grid=(1,),
        in_specs=[pl.BlockSpec((NDEV, R, LANES), lambda i: (0, 0, 0))] + [ANY] * len(after),
        out_specs=pl.BlockSpec((R, LANES), lambda i: (0, 0)), out_shape=_sds((R, LANES), F32),
        compiler_params=_params(("arbitrary",)),
    )(parts, *after)


def adamw_natural(gs, ws, ms, vs):
    n = len(ws)
    nblk = 8
    big = [w.ndim == 4 and w.shape[1] % nblk == 0 for w in ws]

    def spec(w, is_big):
        if is_big:
            return pl.BlockSpec((1, w.shape[1] // nblk) + w.shape[2:], lambda i: (0, i, 0, 0))
        return pl.BlockSpec(w.shape, functools.partial(lambda i, nd: (0,) * nd, nd=w.ndim))

    def body(*refs):
        g_refs, w_refs, m_refs, v_refs = (refs[k * n:(k + 1) * n] for k in range(4))
        d_outs, m_outs, v_outs = (refs[(4 + k) * n:(5 + k) * n] for k in range(3))

        def update(p):
            d, mn, vn = _adamw_math(g_refs[p][...], w_refs[p][...], m_refs[p][...], v_refs[p][...])
            d_outs[p][...] = d
            m_outs[p][...] = mn
            v_outs[p][...] = vn

        for p in range(n):
            if big[p]:
                update(p)

        @pl.when(pl.program_id(0) == 0)
        def _():
            for p in range(n):
                if not big[p]:
                    update(p)

    specs = [spec(w, b) for w, b in zip(ws, big)]
    outs = pl.pallas_call(
        body, name="adamw_natural", grid=(nblk,), in_specs=specs * 4, out_specs=specs * 3,
        out_shape=[_sds(w.shape, F32) for w in ws] * 3,
        compiler_params=_params(("arbitrary",)),
    )(*gs, *ws, *ms, *vs)
    return outs[:n], outs[n:2 * n], outs[2 * n:]


def adamw_ada(c_all_t, dmod_all, w, m, v, my_dev):
    D, cols = w.shape
    tr = _tile(D, 256)

    def body(k_ref, c_ref, d_ref, w_ref, m_ref, v_ref, g_out, d_out, m_out, v_out):
        cv = c_ref[...]
        act = cv * _sigmoid(cv)
        dm = d_ref[...]
        g = act[:, 0:1] * dm[0:1, :]
        for b in range(1, NDEV):
            g = g + act[:, b:b + 1] * dm[b:b + 1, :]
        d, mn, vn = _adamw_math(g, w_ref[...], m_ref[...], v_ref[...])
        g_out[...] = g
        d_out[...] = d
        m_out[...] = mn
        v_out[...] = vn

    spec = pl.BlockSpec((tr, cols), lambda i, kr: (i, 0))
    return pl.pallas_call(
        body, name="adamw_ada",
        grid_spec=pltpu.PrefetchScalarGridSpec(
            num_scalar_prefetch=1, grid=(D // tr,),
            in_specs=[pl.BlockSpec((tr, NDEV), lambda i, kr: (i, 0)),
                      pl.BlockSpec((NDEV, cols), lambda i, kr: (0, kr[0])), spec, spec, spec],
            out_specs=[spec] * 4),
        out_shape=[_sds((D, cols), F32)] * 4,
        compiler_params=_params(("parallel",)),
    )(my_dev, c_all_t, dmod_all, w, m, v)


def _small_pack(parts):
    rows = []
    for p in parts:
        flat = p.reshape(-1)
        flat = jnp.pad(flat, (0, (-flat.shape[0]) % (SUBLANES * LANES)))
        rows.append(flat.reshape(-1, LANES))
    return jnp.concatenate(rows, axis=0)


def _small_unpack(buf, shapes):
    out, r = [], 0
    for s in shapes:
        n = int(math.prod(s))
        nr = -(-n // (SUBLANES * LANES)) * SUBLANES
        out.append(buf[r:r + nr].reshape(-1)[:n].reshape(s))
        r += nr
    return out


def kernel(x, c, w_ada, b_ada, w_in, lam_re, lam_im, log_dt, ssm_b_re, ssm_b_im, ssm_c_re, ssm_c_im, ssm_d, w_glu_val, w_glu_gate, w_pool, pool_scale, w_pool_out, w_out, ln1_g, ln1_b, w_ff1, w_ff2, ln2_g, ln2_b, loss_target, m_w_ada, m_b_ada, m_w_in, m_lam_re, m_lam_im, m_log_dt, m_ssm_b_re, m_ssm_b_im, m_ssm_c_re, m_ssm_c_im, m_ssm_d, m_w_glu_val, m_w_glu_gate, m_w_pool, m_pool_scale, m_w_pool_out, m_w_out, m_ln1_g, m_ln1_b, m_w_ff1, m_w_ff2, m_ln2_g, m_ln2_b, v_w_ada, v_b_ada, v_w_in, v_lam_re, v_lam_im, v_log_dt, v_ssm_b_re, v_ssm_b_im, v_ssm_c_re, v_ssm_c_im, v_ssm_d, v_w_glu_val, v_w_glu_gate, v_w_pool, v_pool_scale, v_w_pool_out, v_w_out, v_ln1_g, v_ln1_b, v_w_ff1, v_w_ff2, v_ln2_g, v_ln2_b):
    S, D = x.shape[1], x.shape[2]
    x2d, tgt = x[0], loss_target[0]
    W = D // 2
    G = W // SSM_GROUP
    P, H, GPB = SSM_STATE, SSM_GROUP, GROUPS_PER_BLOCK
    nblk = G // GPB
    gw = W // len(POOL_WINDOWS)
    ax, ay, ac = lax.axis_index("x"), lax.axis_index("y"), lax.axis_index("c")
    my_dev = (4 * ax + 2 * ay + ac).astype(jnp.int32).reshape(1)
    place = jnp.stack([ac, 2 * ax + ay, 4 * ax + 2 * ay + ac]).astype(jnp.int32)
    ts = _tile(S, 256)

    glu = jnp.concatenate([w_glu_val[0], w_glu_gate[0]]).astype(BF16)
    shards = [w_in[0].astype(BF16), glu, w_pool[0].astype(BF16), w_pool_out[0].astype(BF16),
              w_out[0].astype(BF16), w_ff1[0].astype(BF16), w_ff2[0].astype(BF16)]
    wg_in, wg_pool = seq_all_gather("gather_w_in", [shards[0], shards[2]], 1)
    wg_vg, wg_po, wg_out = seq_all_gather("gather_w_mix", [shards[1], shards[3], shards[4]], 2)
    (wg_ff1,) = seq_all_gather("gather_w_ff1", shards[5:6], 3)
    (wg_ff2,) = seq_all_gather("gather_w_ff2", shards[6:7], 11)
    wg_vg = wg_vg.reshape(2 * NDEV, W, D // NDEV)
    nwin = len(POOL_WINDOWS)
    wp_full = jnp.transpose(wg_pool, (1, 0, 2, 3)).reshape(nwin, gw, gw)
    wout_full = wg_out.reshape(1, D, D)
    wff2_full = wg_ff2.reshape(1, 4 * D, D)

    small_names = [b_ada, lam_re, lam_im, log_dt, ssm_b_re, ssm_b_im, ssm_c_re, ssm_c_im, ssm_d,
                   pool_scale, ln1_g, ln1_b, ln2_g, ln2_b]
    small_m = [m_b_ada, m_lam_re, m_lam_im, m_log_dt, m_ssm_b_re, m_ssm_b_im, m_ssm_c_re, m_ssm_c_im,
               m_ssm_d, m_pool_scale, m_ln1_g, m_ln1_b, m_ln2_g, m_ln2_b]
    small_v = [v_b_ada, v_lam_re, v_lam_im, v_log_dt, v_ssm_b_re, v_ssm_b_im, v_ssm_c_re, v_ssm_c_im,
               v_ssm_d, v_pool_scale, v_ln1_g, v_ln1_b, v_ln2_g, v_ln2_b]

    mod, c_all = ada_fwd(c, w_ada[0], b_ada)
    mod = mod.reshape(6, 1, D)
    sh1, sc1, g1, sh2, sc2, g2 = (mod[i] for i in range(6))

    f2, kconst = s5_disc(lam_re[0], lam_im[0], log_dt[0].reshape(G, 1))
    kconst = kconst.reshape(NCONST, SUBLANES, G * P)
    f2r = f2.reshape(2, 1, G * P)
    bt_re = jnp.transpose(ssm_b_re[0], (2, 0, 1)).reshape(H, G * P)
    bt_im = jnp.transpose(ssm_b_im[0], (2, 0, 1)).reshape(H, G * P)
    ct_re = jnp.transpose(ssm_c_re[0], (1, 0, 2)).reshape(H, G * P)
    ct_im = jnp.transpose(ssm_c_im[0], (1, 0, 2)).reshape(H, G * P)
    s5_params = (f2r, bt_re, bt_im, ct_re, ct_im, ssm_d, kconst)

    def e1(t, b):
        xhat, _ = _ln_stats(t[0])
        return [xhat * (1.0 + b[0]) + b[1]], []
    (h1,) = _rowwise("ln_mod1", e1, S, ts, [(x2d, D, 0)], [sc1, sh1], [(D, BF16)], [])

    (proj,) = mm_nn("proj", h1, wg_in, F32, 2)
    z, xsb_all, zp = s5_fwd(proj, s5_params, nblk)
    (vt,) = mm_nn("glu", z, wg_vg, BF16, 4)
    pooled = pool_fwd(proj, W, W, gw)

    def pool_epi(vals, ex, outs):
        a = vals[0]
        outs[0][...] = a
        outs[1][...] = (a * ex[0][...]).astype(BF16)
    tmp = _tile(S, 1024)
    yp, ypool = _mm(
        "pool_mix", "nn", pooled, wp_full.astype(BF16), (S // tmp, nwin, 1),
        pl.BlockSpec((tmp, gw), lambda i, j, k: (i, j)), pl.BlockSpec((1, gw, gw), lambda i, j, k: (j, 0, 0)),
        [(_sds((S, W), F32), pl.BlockSpec((tmp, gw), lambda i, j, k: (i, j))),
         (_sds((S, W), BF16), pl.BlockSpec((tmp, gw), lambda i, j, k: (i, j)))],
        (tmp, gw), 1, gw, None, pool_epi,
        [(pool_scale, pl.BlockSpec((1, gw), lambda i, j, k: (0, j)))])
    (y_b,) = mm_nn("pool_out", ypool, wg_po, BF16, 4)

    cb = D // NDEV
    ga_cb, gb_cb = (2 * W) // cb, (2 * W + D) // cb
    mcb = 4
    wm = mcb * cb
    tsm = _tile(S, 256)

    def merge_call(name, fn, ins, n_out, after=()):
        def body(*refs):
            vals = [r[...].astype(F32) for r in refs[:len(ins)]]
            for r, v in zip(refs[len(ins):], fn(*vals)):
                r[...] = v.astype(r.dtype)
        return pl.pallas_call(
            _with_after(body, len(ins), after), name=name, grid=(S // tsm, NDEV // mcb),
            in_specs=[pl.BlockSpec((tsm, w), f) for (_, w, f) in ins] + [ANY] * len(after),
            out_specs=[pl.BlockSpec((tsm, w), lambda i, j: (i, j)) for (_, w) in n_out],
            out_shape=[_sds((S, cols), BF16) for (cols, _) in n_out],
            compiler_params=_params(("parallel", "parallel")),
        )(*[a for (a, _, _) in ins], *after)

    merge_ins = [(proj, wm, lambda i, j: (i, ga_cb // mcb + j)), (proj, wm, lambda i, j: (i, gb_cb // mcb + j)),
                 (vt, 2 * wm, lambda i, j: (i, j)), (y_b, wm, lambda i, j: (i, j))]

    def val_gate(vtv):
        return (jnp.concatenate([vtv[:, 2 * q * cb:(2 * q + 1) * cb] for q in range(mcb)], axis=1),
                jnp.concatenate([vtv[:, (2 * q + 1) * cb:(2 * q + 2) * cb] for q in range(mcb)], axis=1))

    def merge_f(ga, gb, vtv, yb):
        vv, tt = val_gate(vtv)
        return [_sigmoid(ga) * (vv * _sigmoid(tt)) + _sigmoid(gb) * yb]
    (merged,) = merge_call("merge", merge_f, merge_ins, [(D, wm)])

    (mix,) = mm_nn("mix_out", merged, wout_full, F32, 1)

    def e3(t, b):
        xv, mx = t
        g1v, l1g, l1b, sc2v, sh2v = b
        r1 = ALPHA * xv + g1v * mx
        xh1, _ = _ln_stats(r1)
        x1 = xh1 * l1g + l1b
        xh, _ = _ln_stats(x1)
        return [r1, xh * (1.0 + sc2v) + sh2v], []
    r1, h2 = _rowwise("post_mix", e3, S, ts, [(x2d, D, 0), (mix, D, 0)],
                      [g1, ln1_g, ln1_b, sc2, sh2], [(D, F32), (D, BF16)], [])

    def relu_epi(vals, ex, outs):
        outs[0][...] = jnp.maximum(vals[0], 0.0).astype(BF16)
    (rl,) = mm_nn("ff1", h2, wg_ff1, BF16, 1, epi=relu_epi)

    def square(a):
        return a * a
    (y2,) = mm_nn("ff2", rl, wff2_full, F32, 1, pro=square)

    def e4(t, b):
        r1v, y2v, tg = t
        g2v, l1g, l1b, l2g, l2b = b
        xh1, _ = _ln_stats(r1v)
        x1 = xh1 * l1g + l1b
        r2 = ALPHA * x1 + g2v * y2v
        xh2, rs2 = _ln_stats(r2)
        err = xh2 * l2g + l2b - tg
        dx2 = err * (1.0 / D)
        dr2 = _ln_bwd(dx2 * l2g, xh2, rs2)
        lsum = jnp.sum(_colsum(err * err), axis=1, keepdims=True) * (0.5 / D)
        return ([ALPHA * dr2, g2v * dr2],
                [jnp.broadcast_to(lsum, (1, LANES)), _colsum(dx2 * xh2), _colsum(dx2), _colsum(dr2 * y2v)])
    dx1a, dy2, loss_acc, g_ln2g, g_ln2b, d_g2 = _rowwise(
        "head", e4, S, ts, [(r1, D, 0), (y2, D, 0), (tgt, D, 0)], [g2, ln1_g, ln1_b, ln2_g, ln2_b],
        [(D, F32), (D, BF16)], [LANES, D, D, D])

    tn_ff = _tile(4 * D, 1024)

    def dff_epi(vals, ex, outs):
        outs[0][...] = (vals[0] * (2.0 * ex[0][...].astype(F32))).astype(BF16)
    tmf = _tile(S, 1024)
    (da1,) = mm_nt("d_ff2", dy2, wff2_full, BF16, 1, tn=tn_ff, epi=dff_epi,
                   extras=[(rl, pl.BlockSpec((tmf, tn_ff), lambda i, j, k: (i, j)))])
    gw_ff2 = mm_tn("gw_ff2", rl, dy2, BF16, NDEV, 0, pro=square)
    gw_ff1 = mm_tn("gw_ff1", h2, da1, BF16, NDEV, 1)
    tok, wait_pair_a = pair_exchange("pair_exchange_ff", [gw_ff2, gw_ff1], 4)
    (dh2,) = mm_nt("d_ff1", da1, wg_ff1, F32, 4, after=[tok])

    def e5(t, b):
        dh2v, r1v, dx1av, mx = t
        sc2v, l1g, l1b, g1v = b
        xh1, rs1 = _ln_stats(r1v)
        x1 = xh1 * l1g + l1b
        xh, rs = _ln_stats(x1)
        dx1 = dx1av + _ln_bwd(dh2v * (1.0 + sc2v), xh, rs)
        dr1 = _ln_bwd(dx1 * l1g, xh1, rs1)
        return ([ALPHA * dr1, g1v * dr1],
                [_colsum(dh2v * xh), _colsum(dh2v), _colsum(dx1 * xh1), _colsum(dx1), _colsum(dr1 * mx)])
    dxa, dmix, d_sc2, d_sh2, g_ln1g, g_ln1b, d_g1 = _rowwise(
        "post_mix_bwd", e5, S, ts, [(dh2, D, 0), (r1, D, 0), (dx1a, D, 0), (mix, D, 0)],
        [sc2, ln1_g, ln1_b, g1], [(D, F32), (D, BF16)], [D, D, D, D, D])

    (dmerged,) = mm_nt("d_mix_out", dmix, wout_full, BF16, 1)
    gw_out = mm_tn("gw_out", merged, dmix, BF16, NDEV, 0)
    grads_a, got_a = wait_pair_a(gw_out)
    parts_a = [pair_sum("pair_sum_ff%d" % i, g, t, place) for i, (g, t) in enumerate(zip(grads_a, got_a))]
    tok, wait_chip_a = chip_exchange("chip_exchange_ff", parts_a, 5)

    def merge_b(ga, gb, vtv, yb, dm):
        vv, tt = val_gate(vtv)
        sa, sb, st = _sigmoid(ga), _sigmoid(gb), _sigmoid(tt)
        dya = dm * sa
        dv, dt = dya * st, dya * vv * st * (1.0 - st)
        dvt_tile = jnp.concatenate([t[:, q * cb:(q + 1) * cb] for q in range(mcb) for t in (dv, dt)], axis=1)
        return [dm * (vv * st) * sa * (1.0 - sa), dm * yb * sb * (1.0 - sb), dvt_tile, dm * sb]
    dga, dgb_, dvt, dy_b = merge_call(
        "merge_bwd", merge_b, merge_ins + [(dmerged, wm, lambda i, j: (i, j))],
        [(D, wm), (D, wm), (2 * D, 2 * wm), (D, wm)], after=[tok])

    (dypool,) = mm_nt("d_pool_out", dy_b, wg_po, F32, NDEV)
    gw_po = mm_tn("gw_pool_out", ypool, dy_b, BF16, NDEV, 4)

    def e7(t, b):
        return [t[0] * b[0]], [_colsum(t[0] * t[1])]
    dyp, g_pscale = _rowwise("pool_scale_bwd", e7, S, ts, [(dypool, W, 0), (yp, W, 0)],
                             [pool_scale], [(W, BF16)], [W])
    (dpooled,) = _mm(
        "d_pool_mix", "nt", dyp, wp_full.astype(BF16), (S // tmp, nwin, 1),
        pl.BlockSpec((tmp, gw), lambda i, j, k: (i, j)), pl.BlockSpec((1, gw, gw), lambda i, j, k: (j, 0, 0)),
        [(_sds((S, W), F32), pl.BlockSpec((tmp, gw), lambda i, j, k: (i, j)))], (tmp, gw), 1, gw)
    tkp = _tile(S, 2048)
    gw_pool = _mm(
        "gw_pool", "tn", pooled, dyp, (nwin, 1, S // tkp),
        pl.BlockSpec((tkp, gw), lambda i, j, k: (k, i)), pl.BlockSpec((tkp, gw), lambda i, j, k: (k, i)),
        [(_sds((nwin, gw, gw), BF16), pl.BlockSpec((1, gw, gw), lambda i, j, k: (i, 0, 0)))],
        (gw, gw), 1, gw, stacked_out=True)[0]
    du_pool = pool_bwd(dpooled, gw)

    (dz,) = mm_nt("d_glu", dvt, wg_vg, BF16, 2 * NDEV)
    gw_vg = mm_tn("gw_glu", z, dvt, BF16, 2 * NDEV, 4)
    gw_pool_st = jnp.transpose(gw_pool.reshape(nwin, NDEV, gw // NDEV, gw), (1, 0, 2, 3))
    grads_b = [gw_out, gw_po, gw_pool_st, gw_vg.reshape(NDEV, 2, W, D // NDEV)]
    tok, wait_pair_b = pair_exchange("pair_exchange_mix", grads_b, 6)
    du_ssm, g_bt_re, g_bt_im, g_ct_re, g_ct_im, g_f, g_d, g_a = s5_bwd(
        proj, xsb_all, dz, zp, s5_params, nblk, after=[tok])
    grads_b, got_b = wait_pair_b(du_ssm)
    parts_b = [pair_sum("pair_sum_mix%d" % i, g, t, place) for i, (g, t) in enumerate(zip(grads_b, got_b))]
    tok, wait_chip_b = chip_exchange("chip_exchange_mix", parts_b, 7)

    dproj = jnp.concatenate([du_ssm, du_pool, dga, dgb_], axis=1)
    gw_in = mm_tn("gw_in", h1, dproj, BF16, NDEV, 1, after=[tok])
    tok, wait_pair_c = pair_exchange("pair_exchange_in", [gw_in], 8)
    (dh1,) = mm_nt("d_proj", dproj, wg_in, F32, 4, after=[tok])
    grads_c, got_c = wait_pair_c(dh1)
    parts_c = [pair_sum("pair_sum_in", grads_c[0], got_c[0], place)]
    tok, wait_chip_c = chip_exchange("chip_exchange_in", parts_c, 9)

    def e10(t, b):
        dh1v, xv, dxav = t
        xh, rs = _ln_stats(xv)
        return ([dxav + _ln_bwd(dh1v * (1.0 + b[0]), xh, rs)],
                [_colsum(dh1v * xh), _colsum(dh1v)])
    grad_x, d_sc1, d_sh1 = _rowwise("ln_mod1_bwd", e10, S, ts, [(dh1, D, 0), (x2d, D, 0), (dxa, D, 0)],
                                    [sc1], [(D, F32)], [D, D], after=[tok])

    g_b_re = jnp.transpose(g_bt_re.reshape(H, G, P), (1, 0, 2))
    g_b_im = jnp.transpose(g_bt_im.reshape(H, G, P), (1, 0, 2))
    g_c_re = jnp.transpose(g_ct_re.reshape(H, G, P), (1, 0, 2))
    g_c_im = jnp.transpose(g_ct_im.reshape(H, G, P), (1, 0, 2))
    d_ab = jnp.transpose(g_a.reshape(nblk, 2, GPB, P), (1, 0, 2, 3)).reshape(2, G, P)
    g_lr, g_li, g_ldt = s5_disc_bwd(lam_re[0], lam_im[0], log_dt[0].reshape(G, 1), d_ab,
                                    g_f.reshape(2, G, P))

    dmod = jnp.concatenate([d_sh1, d_sc1, d_g1, d_sh2, d_sc2, d_g2], axis=1)
    small_g = [dmod, g_lr, g_li, g_ldt, g_b_re, g_b_im, g_c_re, g_c_im, g_d, g_pscale,
               g_ln1g, g_ln1b, g_ln2g, g_ln2b, loss_acc]
    packed_g = _small_pack(small_g)
    (parts_all,) = seq_all_gather("gather_small", [packed_g], 10)
    glu_w = jnp.stack([w_glu_val[0], w_glu_gate[0]])
    glu_m = jnp.stack([m_w_glu_val[0], m_w_glu_gate[0]])
    glu_v = jnp.stack([v_w_glu_val[0], v_w_glu_gate[0]])
    wmv = [(w_ff2[0], m_w_ff2[0], v_w_ff2[0]), (w_ff1[0], m_w_ff1[0], v_w_ff1[0]),
           (w_out[0], m_w_out[0], v_w_out[0]), (w_pool_out[0], m_w_pool_out[0], v_w_pool_out[0]),
           (w_pool[0], m_w_pool[0], v_w_pool[0]), (glu_w, glu_m, glu_v)]
    _, got3_a = wait_chip_a(packed_g)
    upd = [adamw_sharded("adamw_%d" % i, g, p, t, w, m, v, place)
           for i, (g, p, t, (w, m, v)) in enumerate(zip(grads_a, got_a, got3_a, wmv[:2]))]
    _, got3_b = wait_chip_b(upd[-1][0])
    upd += [adamw_sharded("adamw_%d" % (2 + i), g, p, t, w, m, v, place)
            for i, (g, p, t, (w, m, v)) in enumerate(zip(grads_b, got_b, got3_b, wmv[2:]))]
    u_ff2, u_ff1, u_out, u_po, u_pool, u_glu = upd

    gsum = sum_small(parts_all, after=[upd[-1][0]])
    def swap_b(ts_):
        return [jnp.swapaxes(t, 2, 3) if i in (4, 5) else t for i, t in enumerate(ts_)]

    sg = _small_unpack(gsum, [t.shape for t in swap_b(small_names)] + [(1, LANES)])
    loss, sg = sg[-1][0, 0], sg[:-1]
    sd, sm, sv = adamw_natural(sg, swap_b(small_names), swap_b(small_m), swap_b(small_v))
    sg, sd, sm, sv = swap_b(sg), swap_b(sd), swap_b(sm), swap_b(sv)

    nmod = 6 * D
    dmod_all = parts_all[:, :nmod // LANES, :].reshape(NDEV, nmod)
    c_all_t = jnp.transpose(c_all.reshape(NDEV, D))
    ada_out = adamw_ada(c_all_t, dmod_all, w_ada[0], m_w_ada[0], v_w_ada[0], my_dev)
    _, got3_c = wait_chip_c(ada_out[0])
    u_in = adamw_sharded("adamw_6", grads_c[0], got_c[0], got3_c[0], w_in[0], m_w_in[0], v_w_in[0], place)

    def pick(k):
        return [ada_out[k][None], sg_sd[k][0], u_in[k][None]] + [t for t in sg_sd[k][1:9]] + \
               [u_glu[k][0][None], u_glu[k][1][None], u_pool[k][None], sg_sd[k][9], u_po[k][None],
                u_out[k][None], sg_sd[k][10], sg_sd[k][11], u_ff1[k][None], u_ff2[k][None],
                sg_sd[k][12], sg_sd[k][13]]

    sg_sd = [sg, sd, sm, sv]
    return (loss, grad_x[None], *pick(0), *pick(1), *pick(2), *pick(3))
```

```python
import functools
import math

import jax
import jax.numpy as jnp
from jax import lax
from jax.experimental import pallas as pl
from jax.experimental.pallas import tpu as pltpu
from jax.experimental.pallas import tpu_sc as plsc

F32 = jnp.float32
BF16 = jnp.bfloat16
MESH = pl.DeviceIdType.MESH
NDEV = 8
NCHIP = 4

SSM_GROUP = 16
SSM_STATE = 64
GROUPS_PER_BLOCK = 8
POOL_WINDOWS = (2, 4, 8, 16)
LN_EPS = 1e-5
ALPHA = 2.0 ** 0.25
ADAM_LR, ADAM_B1, ADAM_B2, ADAM_EPS, ADAM_WD, ADAM_STEP = 0.001, 0.9, 0.999, 1e-08, 0.01, 10
SUBLANES = 8
LANES = 128
VMEM_LIMIT = 56 * 1024 * 1024


def _params(sem=None, vmem=VMEM_LIMIT):
    return pltpu.CompilerParams(dimension_semantics=sem, vmem_limit_bytes=vmem)


def _tile(n, pref):
    if n <= pref:
        return n
    t = 1 << (pref.bit_length() - 1)
    while n % t:
        t //= 2
    return t


def _cast_epi(vals, ex, outs):
    c = vals[0].shape[1]
    for s, v in enumerate(vals):
        outs[0][:, s * c:(s + 1) * c] = v.astype(outs[0].dtype)


ANY = pl.BlockSpec(memory_space=pl.ANY)


def _with_after(body, n_in, after):
    if not after:
        return body
    n_af = len(after)

    def wrapped(*refs):
        return body(*refs[:n_in], *refs[n_in + n_af:])
    return wrapped


def _mm(name, kind, a, b, grid, a_spec, b_spec, outs, acc_shape, nsub=1, c=None,
        pro=None, epi=None, extras=(), stacked_out=False, after=()):
    nk = grid[2]
    n_ex, n_out = len(extras), len(outs)

    def finish(vals, ex, out_refs):
        if epi is not None:
            epi(vals, ex, out_refs)
        elif stacked_out:
            for s, v in enumerate(vals):
                out_refs[0][s] = v.astype(out_refs[0].dtype)
        else:
            _cast_epi(vals, ex, out_refs)

    def body(*refs):
        mm_step(refs[0], refs[1], refs[2:2 + n_ex], refs[2 + n_ex:2 + n_ex + n_out], refs[-1])

    def mm_step(a_ref, b_ref, ex, out_refs, acc):
        k = pl.program_id(2)
        av = a_ref[...]
        if pro is not None:
            av = pro(av)
        if kind == "nn":
            prods = [jnp.dot(av, b_ref[s], preferred_element_type=F32) for s in range(nsub)]
        elif kind == "nt":
            t = None
            for s in range(nsub):
                d = lax.dot_general(av[:, s * c:(s + 1) * c], b_ref[s], (((1,), (1,)), ((), ())),
                                    preferred_element_type=F32)
                t = d if t is None else t + d
            prods = [t]
        else:
            t = lax.dot_general(av, b_ref[...], (((0,), (0,)), ((), ())), preferred_element_type=F32)
            prods = [t[:, s * c:(s + 1) * c] for s in range(nsub)] if stacked_out else [t]
        if nk == 1:
            finish(prods, ex, out_refs)
            return
        w = prods[0].shape[1]

        @pl.when(k == 0)
        def _():
            for s, p in enumerate(prods):
                acc[:, s * w:(s + 1) * w] = p

        @pl.when(jnp.logical_and(k > 0, k < nk - 1))
        def _():
            for s, p in enumerate(prods):
                acc[:, s * w:(s + 1) * w] += p

        @pl.when(k == nk - 1)
        def _():
            finish([acc[:, s * w:(s + 1) * w] + p for s, p in enumerate(prods)], ex, out_refs)

    return pl.pallas_call(
        _with_after(body, 2 + n_ex, after), name=name, grid=grid,
        in_specs=[a_spec, b_spec] + [e[1] for e in extras] + [ANY] * len(after),
        out_specs=[o[1] for o in outs],
        out_shape=[o[0] for o in outs],
        scratch_shapes=[pltpu.VMEM(acc_shape, F32)] if nk > 1 else [],
        compiler_params=_params(("parallel", "parallel", "arbitrary")),
    )(a, b, *[e[0] for e in extras], *after)


def _sds(shape, dtype):
    return jax.ShapeDtypeStruct(shape, dtype)


def mm_nn(name, a, b3, out_dtype, nsub, tm=1024, tk=2048, tn=None, pro=None, epi=None,
          extras=(), after=()):
    M = a.shape[0]
    nb, K, cdim = b3.shape
    tm, tk = _tile(M, tm), _tile(K, tk)
    if nb == 1:
        tn = _tile(cdim, tn or 1024)
        nsub, c, nj = 1, tn, cdim // tn
        b_spec = pl.BlockSpec((1, tk, tn), lambda i, j, k: (0, k, j))
        N = cdim
    else:
        c, nj, tn = cdim, nb // nsub, nsub * cdim
        b_spec = pl.BlockSpec((nsub, tk, cdim), lambda i, j, k: (j, k, 0))
        N = nb * cdim
    a_spec = pl.BlockSpec((tm, tk), lambda i, j, k: (i, k))
    grid = (M // tm, nj, K // tk)
    outs = [(_sds((M, N), out_dtype), pl.BlockSpec((tm, tn), lambda i, j, k: (i, j)))]
    return _mm(name, "nn", a, b3, grid, a_spec, b_spec, outs, (tm, tn), nsub, c, pro, epi, extras,
               after=after)


def mm_nt(name, a, b3, out_dtype, nsub, tm=1024, tn=1024, epi=None, extras=(), after=()):
    M = a.shape[0]
    nb, N, cdim = b3.shape
    tm, tn = _tile(M, tm), _tile(N, tn)
    if nb == 1:
        tk = _tile(cdim, 2048)
        nsub, c, nk = 1, tk, cdim // tk
        b_spec = pl.BlockSpec((1, tn, tk), lambda i, j, k: (0, j, k))
    else:
        c, nk, tk = cdim, nb // nsub, nsub * cdim
        b_spec = pl.BlockSpec((nsub, tn, cdim), lambda i, j, k: (k, j, 0))
    a_spec = pl.BlockSpec((tm, tk), lambda i, j, k: (i, k))
    grid = (M // tm, N // tn, nk)
    outs = [(_sds((M, N), out_dtype), pl.BlockSpec((tm, tn), lambda i, j, k: (i, j)))]
    return _mm(name, "nt", a, b3, grid, a_spec, b_spec, outs, (tm, tn), nsub, c, None, epi, extras,
               after=after)


def mm_tn(name, a, b, out_dtype, nb, nsub, tma=1024, tk=2048, pro=None, after=()):
    S, Ka = a.shape
    N = b.shape[1]
    tk, tma = _tile(S, tk), _tile(Ka, tma)
    a_spec = pl.BlockSpec((tk, tma), lambda i, j, k: (k, i))
    if nsub == 0:
        tn = _tile(N, 1024)
        res = _mm(name, "tn", a, b, (Ka // tma, N // tn, S // tk), a_spec,
                  pl.BlockSpec((tk, tn), lambda i, j, k: (k, j)),
                  [(_sds((Ka, N), out_dtype), pl.BlockSpec((tma, tn), lambda i, j, k: (i, j)))],
                  (tma, tn), 1, tn, pro, None, (), after=after)[0]
        return res.reshape(nb, Ka // nb, N)
    c = N // nb
    tn = nsub * c
    outs = [(_sds((nb, Ka, c), out_dtype), pl.BlockSpec((nsub, tma, c), lambda i, j, k: (j, i, 0)))]
    return _mm(name, "tn", a, b, (Ka // tma, nb // nsub, S // tk), a_spec,
               pl.BlockSpec((tk, tn), lambda i, j, k: (k, j)), outs, (tma, tn), nsub, c,
               pro, None, (), stacked_out=True, after=after)[0]


def _rowwise(name, fn, S, ts, tiled, bcast, tiled_out, acc_out, after=()):
    nt, nb, no, na = len(tiled), len(bcast), len(tiled_out), len(acc_out)

    def body(*refs):
        tin = [r[...] for r in refs[:nt]]
        bin_ = [r[...] for r in refs[nt:nt + nb]]
        o_refs = refs[nt + nb:nt + nb + no]
        a_refs = refs[nt + nb + no:]
        touts, aouts = fn(tin, bin_)
        for r, v in zip(o_refs, touts):
            r[...] = v.astype(r.dtype)
        i = pl.program_id(0)

        @pl.when(i == 0)
        def _():
            for r, v in zip(a_refs, aouts):
                r[...] = v

        @pl.when(i > 0)
        def _():
            for r, v in zip(a_refs, aouts):
                r[...] += v

    in_specs = [pl.BlockSpec((ts, w), functools.partial(lambda i, cb: (i, cb), cb=cb))
                for (_, w, cb) in tiled]
    in_specs += [pl.BlockSpec(b.shape, lambda i: (0, 0)) for b in bcast]
    out_specs = [pl.BlockSpec((ts, w), lambda i: (i, 0)) for (w, _) in tiled_out]
    out_specs += [pl.BlockSpec((1, w), lambda i: (0, 0)) for w in acc_out]
    out_shape = [_sds((S, w), d) for (w, d) in tiled_out] + [_sds((1, w), F32) for w in acc_out]
    return pl.pallas_call(
        _with_after(body, nt + nb, after), name=name, grid=(S // ts,),
        in_specs=in_specs + [ANY] * len(after), out_specs=out_specs,
        out_shape=out_shape, compiler_params=_params(("arbitrary",)),
    )(*[t[0] for t in tiled], *bcast, *after)


def _ln_stats(v):
    mu = jnp.mean(v, axis=-1, keepdims=True)
    vc = v - mu
    var = jnp.mean(vc * vc, axis=-1, keepdims=True)
    rstd = lax.rsqrt(var + LN_EPS)
    return vc * rstd, rstd


def _ln_bwd(dxhat, xhat, rstd):
    return rstd * (dxhat - jnp.mean(dxhat, axis=-1, keepdims=True)
                   - xhat * jnp.mean(dxhat * xhat, axis=-1, keepdims=True))


def _colsum(v):
    return jnp.sum(v, axis=0, keepdims=True)


def _sigmoid(v):
    return 1.0 / (1.0 + jnp.exp(-v))


_GELU_C = math.sqrt(2.0 / math.pi)


def _gelu(v):
    return 0.5 * v * (1.0 + jnp.tanh(_GELU_C * (v + 0.044715 * v * v * v)))


def _gelu_grad(v):
    t = jnp.tanh(_GELU_C * (v + 0.044715 * v * v * v))
    return 0.5 * (1.0 + t) + 0.5 * v * (1.0 - t * t) * _GELU_C * (1.0 + 3 * 0.044715 * v * v)


def _disc(lr, li, ldt):
    dt = jnp.exp(ldt)
    mag = jnp.exp(lr * dt)
    ang = li * dt
    ab_re = mag * jnp.cos(ang)
    ab_im = mag * jnp.sin(ang)
    num_re = ab_re - 1.0
    num_im = ab_im
    den = lr * lr + li * li
    f_re = (num_re * lr + num_im * li) / den
    f_im = (num_im * lr - num_re * li) / den
    return ab_re, ab_im, f_re, f_im


def _cmul(ar, ai, br, bi):
    return ar * br - ai * bi, ar * bi + ai * br


SCAN_FOLD = 4
NCONST = 18


def s5_disc(lam_re, lam_im, log_dt):
    G, P = lam_re.shape

    def body(lr_ref, li_ref, ldt_ref, f_ref, k_ref):
        ab_re, ab_im, f_re, f_im = _disc(lr_ref[...], li_ref[...], ldt_ref[...])
        f_ref[0] = f_re
        f_ref[1] = f_im
        fr, fi = ab_re, ab_im
        for _ in range(SCAN_FOLD - 1):
            fr, fi = _cmul(fr, fi, ab_re, ab_im)
        pr, pi = [fr], [fi]
        for _ in range(SUBLANES - 1):
            nr, ni = _cmul(pr[-1], pi[-1], fr, fi)
            pr.append(nr)
            pi.append(ni)
        zero = jnp.zeros_like(ab_re)
        for r in range(SUBLANES):
            k_ref[16, r] = ab_re
            k_ref[17, r] = ab_im
        for n, sh in enumerate((1, 2, 4)):
            for r in range(SUBLANES):
                k_ref[2 * n, r] = pr[sh - 1] if r >= sh else zero
                k_ref[2 * n + 1, r] = pi[sh - 1] if r >= sh else zero
                k_ref[8 + 2 * n, r] = pr[sh - 1] if r + sh < SUBLANES else zero
                k_ref[8 + 2 * n + 1, r] = -pi[sh - 1] if r + sh < SUBLANES else zero
        for r in range(SUBLANES):
            k_ref[6, r] = pr[r]
            k_ref[7, r] = pi[r]
            k_ref[14, r] = pr[SUBLANES - 1 - r]
            k_ref[15, r] = -pi[SUBLANES - 1 - r]

    vm = pl.BlockSpec(memory_space=pltpu.VMEM)
    return pl.pallas_call(
        body, name="s5_disc", in_specs=[vm, vm, vm], out_specs=[vm, vm],
        out_shape=[_sds((2, G, P), F32), _sds((NCONST, SUBLANES, G, P), F32)],
    )(lam_re, lam_im, log_dt)


def s5_disc_bwd(lam_re, lam_im, log_dt, d_ab, d_f):
    G, P = lam_re.shape

    def body(lr_ref, li_ref, ldt_ref, dab_ref, df_ref, glr_ref, gli_ref, gdt_ref):
        _, vjp = jax.vjp(_disc, lr_ref[...], li_ref[...], ldt_ref[...])
        glr, gli, gdt = vjp((dab_ref[0], dab_ref[1], df_ref[0], df_ref[1]))
        glr_ref[...] = glr
        gli_ref[...] = gli
        gdt_ref[...] = gdt

    vm = pl.BlockSpec(memory_space=pltpu.VMEM)
    return pl.pallas_call(
        body, name="s5_disc_bwd", in_specs=[vm] * 5, out_specs=[vm] * 3,
        out_shape=[_sds((G, P), F32), _sds((G, P), F32), _sds((G, 1), F32)],
    )(lam_re, lam_im, log_dt, d_ab, d_f)


def _group_mask(cw, nst):
    row = lax.broadcasted_iota(jnp.int32, (cw, 2 * nst), 0) // SSM_GROUP
    col = (lax.broadcasted_iota(jnp.int32, (cw, 2 * nst), 1) % nst) // SSM_STATE
    return row == col


def _spread(t, mask):
    reps = mask.shape[0] // t.shape[0]
    return jnp.where(mask, jnp.tile(t, (reps, 1)), 0.0).astype(BF16)


def _gather_groups(t, mask):
    t = jnp.where(mask, t, 0.0)
    out = t[0:SSM_GROUP]
    for g in range(1, t.shape[0] // SSM_GROUP):
        out = out + t[g * SSM_GROUP:(g + 1) * SSM_GROUP]
    return out


def _s5_operands(f_ref, br_ref, bi_ref, cr_ref, ci_ref, mask):
    fr, fi = f_ref[0], f_ref[1]
    br, bi = br_ref[...], bi_ref[...]
    bm = _spread(jnp.concatenate([fr * br - fi * bi, fr * bi + fi * br], axis=1), mask)
    cm = _spread(jnp.concatenate([cr_ref[...], -ci_ref[...]], axis=1), mask)
    return bm, cm


def _planes_put(ref, val):
    for c in range(ref.shape[0]):
        ref[c] = val[:, c * LANES:(c + 1) * LANES]


def _planes_get(ref):
    return jnp.concatenate([ref[c] for c in range(ref.shape[0])], axis=1)


def _rows_ld(ref, start, lo, hi):
    rows = pl.ds(start, SUBLANES, stride=SCAN_FOLD)
    return jnp.concatenate([ref[c, rows, :] for c in range(lo // LANES, hi // LANES)], axis=1)


def _rows_st(ref, start, lo, val):
    rows = pl.ds(start, SUBLANES, stride=SCAN_FOLD)
    for k in range(val.shape[1] // LANES):
        ref[lo // LANES + k, rows, :] = val[:, k * LANES:(k + 1) * LANES]


def _phases(ref, base, lo, hi):
    return [_rows_ld(ref, base + j, lo, hi) for j in range(SCAN_FOLD)]


def _row_bcast(v, r):
    return jnp.broadcast_to(v[r:r + 1, :], v.shape)


def _scan_fwd(xs, k_ref, nst):
    m = SCAN_FOLD
    ngroup = xs.shape[1] // (SUBLANES * m)
    row = lax.broadcasted_iota(jnp.int32, (SUBLANES, nst), 0)

    def step(t, carry):
        cr, ci = carry
        base = pl.multiple_of(t * (SUBLANES * m), SUBLANES * m)
        ar, ai = k_ref[16], k_ref[17]
        pr, pi = _phases(xs, base, 0, nst), _phases(xs, base, nst, 2 * nst)
        vr, vi = pr[0], pi[0]
        for j in range(1, m):
            vr, vi = pr[j] + ar * vr - ai * vi, pi[j] + ar * vi + ai * vr
        for n, sh in enumerate((1, 2, 4)):
            sr = pltpu.roll(vr, sh, 0)
            si = pltpu.roll(vi, sh, 0)
            mr, mi = k_ref[2 * n], k_ref[2 * n + 1]
            vr, vi = vr + mr * sr - mi * si, vi + mr * si + mi * sr
        qr, qi = k_ref[6], k_ref[7]
        vr, vi = vr + qr * cr - qi * ci, vi + qr * ci + qi * cr
        _rows_st(xs, base + m - 1, 0, vr)
        _rows_st(xs, base + m - 1, nst, vi)
        xr = jnp.where(row == 0, cr, pltpu.roll(vr, 1, 0))
        xi = jnp.where(row == 0, ci, pltpu.roll(vi, 1, 0))
        for j in range(m - 1):
            xr, xi = pr[j] + ar * xr - ai * xi, pi[j] + ar * xi + ai * xr
            _rows_st(xs, base + j, 0, xr)
            _rows_st(xs, base + j, nst, xi)
        return _row_bcast(vr, SUBLANES - 1), _row_bcast(vi, SUBLANES - 1)

    zero = jnp.zeros((SUBLANES, nst), F32)
    lax.fori_loop(0, ngroup, step, (zero, zero))


def _scan_bwd(g, xs, k_ref, nst):
    m = SCAN_FOLD
    ngroup = g.shape[1] // (SUBLANES * m)
    row = lax.broadcasted_iota(jnp.int32, (SUBLANES, nst), 0)

    def step(tt, carry):
        cr, ci, dar, dai = carry
        t = ngroup - 1 - tt
        base = pl.multiple_of(t * (SUBLANES * m), SUBLANES * m)
        ar, ai = k_ref[16], -k_ref[17]
        dr, di = _phases(g, base, 0, nst), _phases(g, base, nst, 2 * nst)
        wr, wi = dr[m - 1], di[m - 1]
        for j in range(m - 2, -1, -1):
            wr, wi = dr[j] + ar * wr - ai * wi, di[j] + ar * wi + ai * wr
        for n, sh in enumerate((1, 2, 4)):
            sr = pltpu.roll(wr, SUBLANES - sh, 0)
            si = pltpu.roll(wi, SUBLANES - sh, 0)
            mr, mi = k_ref[8 + 2 * n], k_ref[8 + 2 * n + 1]
            wr, wi = wr + mr * sr - mi * si, wi + mr * si + mi * sr
        qr, qi = k_ref[14], k_ref[15]
        wr, wi = wr + qr * cr - qi * ci, wi + qr * ci + qi * cr
        gr, gi = [None] * m, [None] * m
        gr[0], gi[0] = wr, wi
        nr = jnp.where(row == SUBLANES - 1, cr, pltpu.roll(wr, SUBLANES - 1, 0))
        ni = jnp.where(row == SUBLANES - 1, ci, pltpu.roll(wi, SUBLANES - 1, 0))
        for j in range(m - 1, 0, -1):
            nr, ni = dr[j] + ar * nr - ai * ni, di[j] + ar * ni + ai * nr
            gr[j], gi[j] = nr, ni
        for j in range(m):
            _rows_st(g, base + j, 0, gr[j])
            _rows_st(g, base + j, nst, gi[j])
        xr, xi = _phases(xs, base, 0, nst), _phases(xs, base, nst, 2 * nst)
        pbase = pl.multiple_of(jnp.maximum(t - 1, 0) * (SUBLANES * m), SUBLANES * m)
        live = (t > 0).astype(F32)
        lr = _row_bcast(_rows_ld(xs, pbase + m - 1, 0, nst), SUBLANES - 1) * live
        li = _row_bcast(_rows_ld(xs, pbase + m - 1, nst, 2 * nst), SUBLANES - 1) * live
        xmr = [jnp.where(row == 0, lr, pltpu.roll(xr[m - 1], 1, 0))] + xr[:m - 1]
        xmi = [jnp.where(row == 0, li, pltpu.roll(xi[m - 1], 1, 0))] + xi[:m - 1]
        for j in range(m):
            dar = dar + gr[j] * xmr[j] + gi[j] * xmi[j]
            dai = dai + gi[j] * xmr[j] - gr[j] * xmi[j]
        return _row_bcast(wr, 0), _row_bcast(wi, 0), dar, dai

    zero = jnp.zeros((SUBLANES, nst), F32)
    _, _, dar, dai = lax.fori_loop(0, ngroup, step, (zero, zero, zero, zero))
    return _colsum(dar), _colsum(dai)


def _s5_param_specs(cw, nst):
    hp = pl.BlockSpec((SSM_GROUP, nst), lambda b: (0, b))
    return [pl.BlockSpec((2, 1, nst), lambda b: (0, 0, b)), hp, hp, hp, hp,
            pl.BlockSpec((1, cw), lambda b: (0, b)),
            pl.BlockSpec((NCONST, SUBLANES, nst), lambda b: (0, 0, b))]


def s5_fwd(proj, params, nb):
    S = proj.shape[0]
    nst = params[1].shape[1] // nb
    cw = nst // SSM_STATE * SSM_GROUP

    def body(u_ref, f_ref, br_ref, bi_ref, cr_ref, ci_ref, d_ref, k_ref, z_ref, xsb_ref, zp_ref, xs):
        bm, cm = _s5_operands(f_ref, br_ref, bi_ref, cr_ref, ci_ref, _group_mask(cw, nst))
        u = u_ref[...]
        _planes_put(xs, jnp.dot(u.astype(BF16), bm, preferred_element_type=F32))
        _scan_fwd(xs, k_ref, nst)
        xsb = _planes_get(xs).astype(BF16)
        xsb_ref[...] = xsb
        y = lax.dot_general(xsb, cm, (((1,), (1,)), ((), ())), preferred_element_type=F32)
        y = y + d_ref[...] * u
        z_ref[...] = _gelu(y).astype(BF16)
        zp_ref[...] = _gelu_grad(y).astype(BF16)

    return pl.pallas_call(
        body, name="s5_fwd", grid=(nb,),
        in_specs=[pl.BlockSpec((S, cw), lambda b: (0, b))] + _s5_param_specs(cw, nst),
        out_specs=[pl.BlockSpec((S, cw), lambda b: (0, b)), pl.BlockSpec((S, 2 * nst), lambda b: (0, b)),
                   pl.BlockSpec((S, cw), lambda b: (0, b))],
        out_shape=[_sds((S, nb * cw), BF16), _sds((S, nb * 2 * nst), BF16), _sds((S, nb * cw), BF16)],
        scratch_shapes=[pltpu.VMEM((2 * nst // LANES, S, LANES), F32)],
        compiler_params=_params(("arbitrary",)),
    )(proj, *params)


def s5_bwd(proj, xsb_all, dz, zp, params, nb, after=()):
    S = proj.shape[0]
    nst = params[1].shape[1] // nb
    cw = nst // SSM_STATE * SSM_GROUP

    def body(u_ref, xsb_ref, dz_ref, zp_ref, f_ref, br_ref, bi_ref, cr_ref, ci_ref, d_ref, k_ref,
             du_ref, gbr_ref, gbi_ref, gcr_ref, gci_ref, gf_ref, gd_ref, ga_ref, xs, g):
        mask = _group_mask(cw, nst)
        bm, cm = _s5_operands(f_ref, br_ref, bi_ref, cr_ref, ci_ref, mask)
        u = u_ref[...]
        ub = u.astype(BF16)
        d = d_ref[...]
        xsb = xsb_ref[...]
        _planes_put(xs, xsb.astype(F32))
        dy = dz_ref[...].astype(F32) * zp_ref[...].astype(F32)
        gd_ref[...] = _colsum(dy * u)
        dyb = dy.astype(BF16)
        gc = _gather_groups(lax.dot_general(dyb, xsb, (((0,), (0,)), ((), ())),
                                            preferred_element_type=F32), mask)
        gcr_ref[...] = gc[:, :nst]
        gci_ref[...] = -gc[:, nst:]
        _planes_put(g, jnp.dot(dyb, cm, preferred_element_type=F32))
        ar, ai = _scan_bwd(g, xs, k_ref, nst)
        ga_ref[0, 0:1, :] = ar
        ga_ref[0, 1:2, :] = ai
        gb = _planes_get(g).astype(BF16)
        du = lax.dot_general(gb, bm, (((1,), (1,)), ((), ())), preferred_element_type=F32) + d * dy
        du_ref[...] = du.astype(BF16)
        gbb = _gather_groups(lax.dot_general(ub, gb, (((0,), (0,)), ((), ())),
                                             preferred_element_type=F32), mask)
        dr, di = gbb[:, :nst], gbb[:, nst:]
        fr, fi = f_ref[0], f_ref[1]
        br, bi = br_ref[...], bi_ref[...]
        gbr_ref[...] = fr * dr + fi * di
        gbi_ref[...] = fr * di - fi * dr
        gf_ref[0] = _colsum(dr * br + di * bi)
        gf_ref[1] = _colsum(di * br - dr * bi)

    hp = pl.BlockSpec((SSM_GROUP, nst), lambda b: (0, b))
    hp_sds = _sds((SSM_GROUP, nb * nst), F32)
    return pl.pallas_call(
        _with_after(body, 11, after), name="s5_bwd", grid=(nb,),
        in_specs=[pl.BlockSpec((S, cw), lambda b: (0, b)),
                  pl.BlockSpec((S, 2 * nst), lambda b: (0, b)),
                  pl.BlockSpec((S, cw), lambda b: (0, b)),
                  pl.BlockSpec((S, cw), lambda b: (0, b))] + _s5_param_specs(cw, nst)
        + [ANY] * len(after),
        out_specs=[pl.BlockSpec((S, cw), lambda b: (0, b)), hp, hp, hp, hp,
                   pl.BlockSpec((2, 1, nst), lambda b: (0, 0, b)),
                   pl.BlockSpec((1, cw), lambda b: (0, b)),
                   pl.BlockSpec((1, 2, nst), lambda b: (b, 0, 0))],
        out_shape=[_sds((S, nb * cw), BF16), hp_sds, hp_sds, hp_sds, hp_sds,
                   _sds((2, 1, nb * nst), F32), _sds((1, nb * cw), F32), _sds((nb, 2, nst), F32)],
        scratch_shapes=[pltpu.VMEM((2 * nst // LANES, S, LANES), F32)] * 2,
        compiler_params=_params(("arbitrary",)),
    )(proj, xsb_all, dz, zp, *params, *after)


def _shift_rows(v, k, row, down):
    n = v.shape[0]
    if down:
        return jnp.where(row >= k, pltpu.roll(v, k, 0), 0.0)
    return jnp.where(row < n - k, pltpu.roll(v, n - k, 0), 0.0)


def _window(v, gi, row, down):
    sums = []
    s = v
    for k in (1, 2, 4, 8):
        s = s + _shift_rows(s, k, row, down)
        sums.append(s)
    out = sums[3]
    for n in (2, 1, 0):
        out = jnp.where(gi == n, sums[n], out)
    return out


def pool_fwd(proj, col0, width, gw):
    S = proj.shape[0]
    cb0 = col0 // gw

    def body(u_ref, o_ref):
        gi = pl.program_id(0)
        u = u_ref[...]
        row = lax.broadcasted_iota(jnp.int32, u.shape, 0)
        w = jnp.left_shift(2, gi)
        count = jnp.minimum(row + 1, w).astype(F32)
        o_ref[...] = (_window(u, gi, row, True) / count - u).astype(BF16)

    return pl.pallas_call(
        body, name="pool_fwd", grid=(len(POOL_WINDOWS),),
        in_specs=[pl.BlockSpec((S, gw), lambda g: (0, cb0 + g))],
        out_specs=pl.BlockSpec((S, gw), lambda g: (0, g)),
        out_shape=_sds((S, width), BF16), compiler_params=_params(("arbitrary",)),
    )(proj)


def pool_bwd(dpooled, gw):
    S, width = dpooled.shape

    def body(d_ref, o_ref):
        gi = pl.program_id(0)
        d = d_ref[...]
        row = lax.broadcasted_iota(jnp.int32, d.shape, 0)
        w = jnp.left_shift(2, gi)
        count = jnp.minimum(row + 1, w).astype(F32)
        o_ref[...] = (_window(d / count, gi, row, False) - d).astype(BF16)

    return pl.pallas_call(
        body, name="pool_bwd", grid=(len(POOL_WINDOWS),),
        in_specs=[pl.BlockSpec((S, gw), lambda g: (0, g))],
        out_specs=pl.BlockSpec((S, gw), lambda g: (0, g)),
        out_shape=_sds((S, width), BF16), compiler_params=_params(("arbitrary",)),
    )(dpooled)


def _place():
    x, y, c = lax.axis_index("x"), lax.axis_index("y"), lax.axis_index("c")
    chips = [(1 - x, y), (x, 1 - y), (1 - x, 1 - y)]
    return x, y, c, chips


HBM = pl.BlockSpec(memory_space=pltpu.HBM)


GATHER_PIECES = 2
GATHER_SEMS = 1 + 12 * GATHER_PIECES


def _routed_gather_body(n):
    npc = GATHER_PIECES
    k_x, k_y = 1, 1 + 2 * npc
    k_xy, k_yx, k_sib = 1 + 4 * npc, 1 + 5 * npc, 1 + 6 * npc

    def body(*refs):
        ins, outs = refs[:n], refs[n:2 * n]
        send_sems, recv_sems, local_sems = refs[2 * n:]
        x, y, c, (xn, yn, dg) = _place()
        me, sibling = (x, y, c), (x, y, 1 - c)
        barrier = pltpu.get_barrier_semaphore()
        for peer in (sibling, (*xn, c), (*yn, c)):
            pl.semaphore_signal(barrier, inc=1, device_id=peer, device_id_type=MESH)
        pl.semaphore_wait(barrier, 3)

        def piece(i, dev, p):
            rows = ins[i].shape[0] // (2 * npc)
            return outs[i].at[4 * dev[0] + 2 * dev[1] + dev[2], pl.ds(p * rows, rows)]

        def copy(i, k, src, dst, to):
            return pltpu.make_async_remote_copy(src_ref=src, dst_ref=dst, send_sem=send_sems.at[i, k],
                                                recv_sem=recv_sems.at[i, k], device_id=to,
                                                device_id_type=MESH)

        started = []

        def go(cp):
            cp.start()
            started.append(cp)

        for i in range(n):
            rows = ins[i].shape[0] // (2 * npc)
            for q in range(2 * npc):
                py = (q + npc) % (2 * npc)
                go(copy(i, k_x + q, ins[i].at[pl.ds(q * rows, rows)], piece(i, me, q), (*xn, c)))
                go(copy(i, k_y + py, ins[i].at[pl.ds(py * rows, rows)], piece(i, me, py), (*yn, c)))
        for i in range(n):
            go(copy(i, 0, ins[i], outs[i].at[4 * x + 2 * y + c], sibling))
        mine = [pltpu.make_async_copy(ins[i], outs[i].at[4 * x + 2 * y + c], local_sems.at[i])
                for i in range(n)]
        for cp in mine:
            cp.start()

        def arrived(i, k, chip, p, onward, r):
            got = piece(i, (*chip, c), p)
            copy(i, k, got, got, me).wait_recv()
            if onward is not None:
                go(copy(i, onward[0], got, got, (*onward[1], c)))
            go(copy(i, k_sib + r, got, got, sibling))

        for i in range(n):
            for q in range(npc):
                arrived(i, k_x + q, xn, q, (k_xy + q, yn), q)
                arrived(i, k_y + npc + q, yn, npc + q, (k_yx + q, xn), 2 * npc + npc + q)
            for q in range(npc):
                arrived(i, k_x + npc + q, xn, npc + q, None, npc + q)
                arrived(i, k_y + q, yn, q, None, 2 * npc + q)
            for q in range(npc):
                arrived(i, k_xy + q, dg, q, None, 4 * npc + q)
                arrived(i, k_yx + q, dg, npc + q, None, 4 * npc + npc + q)
        for i in range(n):
            block = outs[i].at[4 * x + 2 * y + 1 - c]
            copy(i, 0, block, block, me).wait_recv()
            for j, chip in enumerate((xn, yn, dg)):
                for p in range(2 * npc):
                    got = piece(i, (*chip, 1 - c), p)
                    copy(i, k_sib + 2 * npc * j + p, got, got, me).wait_recv()
        for cp in started:
            cp.wait_send()
        for cp in mine:
            cp.wait()

    return body


def _on_sequencer(name, body, arrays, out_sds, sems, collective_id):
    ins = [jax.new_ref(a, memory_space=pltpu.MemorySpace.HBM) for a in arrays]
    outs = [jax.empty_ref(s, memory_space=pltpu.MemorySpace.HBM) for s in out_sds]

    @pl.kernel(mesh=plsc.ScalarSubcoreMesh(axis_name="sequencer", num_cores=1), name=name,
               scratch_types=tuple(sems),
               compiler_params=pltpu.CompilerParams(collective_id=collective_id))
    def launch(*sem_refs):
        body(*ins, *outs, *sem_refs)

    launch()
    return [o[...] for o in outs]


def seq_all_gather(name, shards, collective_id):
    n = len(shards)
    return _on_sequencer(
        name, _routed_gather_body(n), shards, [_sds((NDEV,) + s.shape, s.dtype) for s in shards],
        [pltpu.SemaphoreType.DMA((n, GATHER_SEMS)), pltpu.SemaphoreType.DMA((n, GATHER_SEMS)),
         pltpu.SemaphoreType.DMA((n,))], collective_id)


def pair_exchange(name, grads, collective_id):
    def plan(srcs, lands):
        x, y, c, _ = _place()
        return ([(i, q, srcs[i].at[2 * q + 1 - c], lands[i].at[q], (x, y, 1 - c))
                 for i in range(len(srcs)) for q in range(NCHIP)], [(x, y, 1 - c)])

    return _split_exchange(name, grads, [_sds((NCHIP,) + g.shape[1:], g.dtype) for g in grads],
                           plan, NCHIP, collective_id)


SEM = pl.BlockSpec(memory_space=pltpu.SEMAPHORE)


def _split_exchange(name, srcs, land_sds, plan, ncopy, collective_id):
    n = len(srcs)
    nsem = n * ncopy
    effect = pltpu.SideEffectType.DATAFLOW_SIDE_EFFECTING

    def descriptors(src_refs, land_refs, send_sems, recv_sems):
        copies, peers = plan(src_refs, land_refs)
        return [pltpu.make_async_remote_copy(src_ref=s, dst_ref=d, send_sem=send_sems[i * ncopy + k],
                                             recv_sem=recv_sems[i * ncopy + k], device_id=to,
                                             device_id_type=MESH) for (i, k, s, d, to) in copies], peers

    def start_body(*refs):
        src_refs, land_refs = refs[:n], refs[n:2 * n]
        send_sems, recv_sems = refs[2 * n:2 * n + nsem], refs[2 * n + nsem:2 * n + 2 * nsem]
        token = refs[-1]
        cps, peers = descriptors(src_refs, land_refs, send_sems, recv_sems)
        barrier = pltpu.get_barrier_semaphore()
        for peer in peers:
            pl.semaphore_signal(barrier, inc=1, device_id=peer, device_id_type=MESH)
        pl.semaphore_wait(barrier, len(peers))
        for cp in cps:
            cp.start()
        token[...] = jnp.zeros_like(token)

    lands = [pltpu.with_memory_space_constraint(lax.empty(s.shape, s.dtype), pltpu.HBM) for s in land_sds]
    srcs = [pltpu.with_memory_space_constraint(s, pltpu.HBM) for s in srcs]
    res = pl.pallas_call(
        start_body, name=name + "_start",
        out_shape=(pltpu.SemaphoreType.DMA(()),) * (2 * nsem)
        + tuple(pltpu.HBM(s.shape, s.dtype) for s in srcs)
        + tuple(pltpu.HBM(s.shape, s.dtype) for s in land_sds) + (_sds((SUBLANES, LANES), F32),),
        in_specs=[HBM] * (2 * n),
        out_specs=(SEM,) * (2 * nsem) + (HBM,) * (2 * n) + (pl.BlockSpec(memory_space=pltpu.VMEM),),
        input_output_aliases={i: 2 * nsem + i for i in range(2 * n)},
        compiler_params=pltpu.CompilerParams(has_side_effects=effect, collective_id=collective_id),
    )(*srcs, *lands)
    sems = res[:2 * nsem]
    thru = res[2 * nsem:2 * nsem + 2 * n]
    token = res[-1]

    def wait(after):
        def wait_body(*refs):
            src_refs, land_refs = refs[:n], refs[n:2 * n]
            cps, _ = descriptors(src_refs, land_refs, refs[2 * n:2 * n + nsem],
                                 refs[2 * n + nsem:2 * n + 2 * nsem])
            for cp in cps:
                cp.wait_send()
            for cp in cps:
                cp.wait_recv()

        out = pl.pallas_call(
            wait_body, name=name + "_wait",
            out_shape=tuple(pltpu.HBM(s.shape, s.dtype) for s in srcs)
            + tuple(pltpu.HBM(s.shape, s.dtype) for s in land_sds),
            in_specs=[HBM] * (2 * n) + [SEM] * (2 * nsem) + [pl.BlockSpec(memory_space=pl.ANY)],
            out_specs=(HBM,) * (2 * n),
            input_output_aliases={i: i for i in range(2 * n)},
            compiler_params=pltpu.CompilerParams(has_side_effects=effect),
        )(*thru, *sems, after)
        return list(out[:n]), list(out[n:])

    return token, wait


def pair_sum(name, grad, got, place):
    shp = grad.shape[1:]
    r, cdim = shp[-2], shp[-1]
    lead = int(math.prod(shp[:-2])) if len(shp) > 2 else 1
    g5 = grad.reshape(NCHIP, 2, lead * r, cdim)
    t4 = got.reshape(NCHIP, lead * r, cdim)
    R = lead * r
    tr = _tile(R, max(16, (1 << 19) // cdim))
    per = R // tr
    nchunk = (NCHIP - 1) * per
    nbuf = 3

    def body(p_ref, g_hbm, t_hbm, o_hbm, gbuf, tbuf, obuf, in_sems, out_sems):
        c, chip = p_ref[0], p_ref[1]

        def where(i):
            i = jnp.asarray(i, jnp.int32)
            return chip ^ (i // per + 1), pl.ds(pl.multiple_of((i % per) * tr, tr), tr)

        def loads(i):
            q, rows = where(i)
            s = i % nbuf
            return (pltpu.make_async_copy(g_hbm.at[q, c, rows], gbuf.at[s], in_sems.at[s, 0]),
                    pltpu.make_async_copy(t_hbm.at[q, rows], tbuf.at[s], in_sems.at[s, 1]))

        def store(i):
            q, rows = where(i)
            return pltpu.make_async_copy(obuf.at[i % 2], o_hbm.at[q, rows], out_sems.at[i % 2])

        for i in range(nbuf - 1):
            for cp in loads(i):
                cp.start()

        def step(i, carry):
            @pl.when(i + nbuf - 1 < nchunk)
            def _():
                for cp in loads(i + nbuf - 1):
                    cp.start()
            for cp in loads(i):
                cp.wait()

            @pl.when(i >= 2)
            def _():
                store(i - 2).wait()
            s = i % nbuf
            obuf[i % 2] = (gbuf[s].astype(F32) + tbuf[s].astype(F32)).astype(obuf.dtype)
            store(i).start()
            return carry

        lax.fori_loop(0, nchunk, step, 0)
        for i in range(nchunk - 2, nchunk):
            store(i).wait()

    buf = pltpu.VMEM((nbuf, tr, cdim), grad.dtype)
    out = pl.pallas_call(
        body, name=name,
        grid_spec=pltpu.PrefetchScalarGridSpec(
            num_scalar_prefetch=1, grid=(1,), in_specs=[ANY, ANY], out_specs=ANY,
            scratch_shapes=[buf, buf, pltpu.VMEM((2, tr, cdim), grad.dtype),
                            pltpu.SemaphoreType.DMA((nbuf, 2)), pltpu.SemaphoreType.DMA((2,))]),
        out_shape=_sds((NCHIP, R, cdim), grad.dtype),
        compiler_params=_params(("arbitrary",)),
    )(place, g5, t4)
    return out


def chip_exchange(name, parts, collective_id):
    def plan(srcs, lands):
        x, y, c, chips = _place()
        return ([(i, j, srcs[i].at[2 * chip[0] + chip[1]], lands[i].at[j], (*chip, c))
                 for i in range(len(srcs)) for j, chip in enumerate(chips)],
                [(*chip, c) for chip in chips])

    return _split_exchange(name, parts, [_sds((3,) + p.shape[1:], p.dtype) for p in parts],
                           plan, 3, collective_id)


def ada_fwd(c_row, w_ada, b_ada):
    D, cols = w_ada.shape

    def body(c_ref, w_ref, b_ref, mod_ref, call_ref, act8, part, s1, r1, s2, r2):
        x, y, c, _ = _place()
        me = 4 * x + 2 * y + c
        call_ref[me] = c_ref[...]
        cps = []
        for k in range(1, NDEV):
            to = (x ^ (k >> 2), y ^ ((k >> 1) & 1), c ^ (k & 1))
            cps.append(pltpu.make_async_remote_copy(
                src_ref=c_ref, dst_ref=call_ref.at[me], send_sem=s1.at[k - 1],
                recv_sem=r1.at[k - 1], device_id=to, device_id_type=MESH))
            cps[-1].start()
        for cp in cps:
            cp.wait()
        for b in range(NDEV):
            act8[b:b + 1, :] = call_ref[b]
        cv = act8[...]
        act = (cv * _sigmoid(cv)).astype(BF16)
        res = jnp.dot(act, w_ref[...].astype(BF16), preferred_element_type=F32)
        for b in range(NDEV):
            part[b] = res[b:b + 1, :]
        mod_ref[me] = part[me]
        cps = []
        for k in range(1, NDEV):
            to = (x ^ (k >> 2), y ^ ((k >> 1) & 1), c ^ (k & 1))
            dst = 4 * to[0] + 2 * to[1] + to[2]
            cps.append(pltpu.make_async_remote_copy(
                src_ref=part.at[dst], dst_ref=mod_ref.at[me], send_sem=s2.at[k - 1],
                recv_sem=r2.at[k - 1], device_id=to, device_id_type=MESH))
            cps[-1].start()
        for cp in cps:
            cp.wait()
        for b in range(NDEV):
            mod_ref[b] = mod_ref[b] + b_ref[b]

    vm = pl.BlockSpec(memory_space=pltpu.VMEM)
    return pl.pallas_call(
        body, name="ada_fwd", in_specs=[vm, vm, vm], out_specs=[vm, vm],
        out_shape=[_sds((NDEV, 1, cols), F32), _sds((NDEV, 1, D), F32)],
        scratch_shapes=[pltpu.VMEM((NDEV, D), F32), pltpu.VMEM((NDEV, 1, cols), F32),
                        pltpu.SemaphoreType.DMA((NDEV - 1,)), pltpu.SemaphoreType.DMA((NDEV - 1,)),
                        pltpu.SemaphoreType.DMA((NDEV - 1,)), pltpu.SemaphoreType.DMA((NDEV - 1,))],
        compiler_params=pltpu.CompilerParams(vmem_limit_bytes=VMEM_LIMIT),
    )(c_row, w_ada, b_ada.reshape(NDEV, 1, cols))


def _adamw_math(g, w, m, v):
    m2 = ADAM_B1 * m + (1.0 - ADAM_B1) * g
    v2 = ADAM_B2 * v + (1.0 - ADAM_B2) * (g * g)
    m_hat = m2 / (1.0 - ADAM_B1 ** ADAM_STEP)
    v_hat = v2 / (1.0 - ADAM_B2 ** ADAM_STEP)
    delta = -ADAM_LR * (m_hat / (jnp.sqrt(v_hat) + ADAM_EPS) + ADAM_WD * w)
    return delta, m2, v2


def adamw_sharded(name, grad8, pair4, got3, w, m, v, place, after=()):
    shape = w.shape
    cdim = shape[-1]
    R = int(math.prod(shape[:-1]))
    w2, m2, v2 = (t.reshape(R, cdim) for t in (w, m, v))
    tr = _tile(R, max(8, (1 << 19) // cdim))

    def body(q_ref, own_ref, sib_ref, t_ref, w_ref, m_ref, v_ref, g_out, d_out, m_out, v_out):
        g = own_ref[0].astype(F32) + sib_ref[0].astype(F32)
        for j in range(3):
            g = g + t_ref[j].astype(F32)
        d, mn, vn = _adamw_math(g, w_ref[...], m_ref[...], v_ref[...])
        g_out[...] = g
        d_out[...] = d
        m_out[...] = mn
        v_out[...] = vn

    spec = pl.BlockSpec((tr, cdim), lambda i, qr: (i, 0))
    outs = pl.pallas_call(
        _with_after(body, 7, after), name=name,
        grid_spec=pltpu.PrefetchScalarGridSpec(
            num_scalar_prefetch=1, grid=(R // tr,),
            in_specs=[pl.BlockSpec((1, tr, cdim), lambda i, qr: (qr[2], i, 0)),
                      pl.BlockSpec((1, tr, cdim), lambda i, qr: (qr[1], i, 0)),
                      pl.BlockSpec((3, tr, cdim), lambda i, qr: (0, i, 0)), spec, spec, spec]
            + [ANY] * len(after),
            out_specs=[spec] * 4),
        out_shape=[_sds((R, cdim), F32)] * 4,
        compiler_params=_params(("parallel",)),
    )(place, grad8.reshape(NDEV, R, cdim), pair4.reshape(NCHIP, R, cdim),
      got3.reshape(3, R, cdim), w2, m2, v2, *after)
    return [o.reshape(shape) for o in outs]


def sum_small(parts, after=()):
    R = parts.shape[1]

    def body(p_ref, g_out):
        g = p_ref[0]
        for j in range(1, NDEV):
            g = g + p_ref[j]
        g_out[...] = g

    return pl.pallas_call(
        _with_after(body, 1, after), name="sum_small", grid=(1,),
        in_specs=[pl.BlockSpec((NDEV, R, LANES), lambda i: (0, 0, 0))] + [ANY] * len(after),
        out_specs=pl.BlockSpec((R, LANES), lambda i: (0, 0)), out_shape=_sds((R, LANES), F32),
        compiler_params=_params(("arbitrary",)),
    )(parts, *after)


def adamw_natural(gs, ws, ms, vs):
    n = len(ws)
    nblk = 8
    big = [w.ndim == 4 and w.shape[1] % nblk == 0 for w in ws]

    def spec(w, is_big):
        if is_big:
            return pl.BlockSpec((1, w.shape[1] // nblk) + w.shape[2:], lambda i: (0, i, 0, 0))
        return pl.BlockSpec(w.shape, functools.partial(lambda i, nd: (0,) * nd, nd=w.ndim))

    def body(*refs):
        g_refs, w_refs, m_refs, v_refs = (refs[k * n:(k + 1) * n] for k in range(4))
        d_outs, m_outs, v_outs = (refs[(4 + k) * n:(5 + k) * n] for k in range(3))

        def update(p):
            d, mn, vn = _adamw_math(g_refs[p][...], w_refs[p][...], m_refs[p][...], v_refs[p][...])
            d_outs[p][...] = d
            m_outs[p][...] = mn
            v_outs[p][...] = vn

        for p in range(n):
            if big[p]:
                update(p)

        @pl.when(pl.program_id(0) == 0)
        def _():
            for p in range(n):
                if not big[p]:
                    update(p)

    specs = [spec(w, b) for w, b in zip(ws, big)]
    outs = pl.pallas_call(
        body, name="adamw_natural", grid=(nblk,), in_specs=specs * 4, out_specs=specs * 3,
        out_shape=[_sds(w.shape, F32) for w in ws] * 3,
        compiler_params=_params(("arbitrary",)),
    )(*gs, *ws, *ms, *vs)
    return outs[:n], outs[n:2 * n], outs[2 * n:]


def adamw_ada(c_all_t, dmod_all, w, m, v, my_dev):
    D, cols = w.shape
    tr = _tile(D, 256)

    def body(k_ref, c_ref, d_ref, w_ref, m_ref, v_ref, g_out, d_out, m_out, v_out):
        cv = c_ref[...]
        act = cv * _sigmoid(cv)
        dm = d_ref[...]
        g = act[:, 0:1] * dm[0:1, :]
        for b in range(1, NDEV):
            g = g + act[:, b:b + 1] * dm[b:b + 1, :]
        d, mn, vn = _adamw_math(g, w_ref[...], m_ref[...], v_ref[...])
        g_out[...] = g
        d_out[...] = d
        m_out[...] = mn
        v_out[...] = vn

    spec = pl.BlockSpec((tr, cols), lambda i, kr: (i, 0))
    return pl.pallas_call(
        body, name="adamw_ada",
        grid_spec=pltpu.PrefetchScalarGridSpec(
            num_scalar_prefetch=1, grid=(D // tr,),
            in_specs=[pl.BlockSpec((tr, NDEV), lambda i, kr: (i, 0)),
                      pl.BlockSpec((NDEV, cols), lambda i, kr: (0, kr[0])), spec, spec, spec],
            out_specs=[spec] * 4),
        out_shape=[_sds((D, cols), F32)] * 4,
        compiler_params=_params(("parallel",)),
    )(my_dev, c_all_t, dmod_all, w, m, v)


def _small_pack(parts):
    rows = []
    for p in parts:
        flat = p.reshape(-1)
        flat = jnp.pad(flat, (0, (-flat.shape[0]) % (SUBLANES * LANES)))
        rows.append(flat.reshape(-1, LANES))
    return jnp.concatenate(rows, axis=0)


def _small_unpack(buf, shapes):
    out, r = [], 0
    for s in shapes:
        n = int(math.prod(s))
        nr = -(-n // (SUBLANES * LANES)) * SUBLANES
        out.append(buf[r:r + nr].reshape(-1)[:n].reshape(s))
        r += nr
    return out


def kernel(x, c, w_ada, b_ada, w_in, lam_re, lam_im, log_dt, ssm_b_re, ssm_b_im, ssm_c_re, ssm_c_im, ssm_d, w_glu_val, w_glu_gate, w_pool, pool_scale, w_pool_out, w_out, ln1_g, ln1_b, w_ff1, w_ff2, ln2_g, ln2_b, loss_target, m_w_ada, m_b_ada, m_w_in, m_lam_re, m_lam_im, m_log_dt, m_ssm_b_re, m_ssm_b_im, m_ssm_c_re, m_ssm_c_im, m_ssm_d, m_w_glu_val, m_w_glu_gate, m_w_pool, m_pool_scale, m_w_pool_out, m_w_out, m_ln1_g, m_ln1_b, m_w_ff1, m_w_ff2, m_ln2_g, m_ln2_b, v_w_ada, v_b_ada, v_w_in, v_lam_re, v_lam_im, v_log_dt, v_ssm_b_re, v_ssm_b_im, v_ssm_c_re, v_ssm_c_im, v_ssm_d, v_w_glu_val, v_w_glu_gate, v_w_pool, v_pool_scale, v_w_pool_out, v_w_out, v_ln1_g, v_ln1_b, v_w_ff1, v_w_ff2, v_ln2_g, v_ln2_b):
    S, D = x.shape[1], x.shape[2]
    x2d, tgt = x[0], loss_target[0]
    W = D // 2
    G = W // SSM_GROUP
    P, H, GPB = SSM_STATE, SSM_GROUP, GROUPS_PER_BLOCK
    nblk = G // GPB
    gw = W // len(POOL_WINDOWS)
    ax, ay, ac = lax.axis_index("x"), lax.axis_index("y"), lax.axis_index("c")
    my_dev = (4 * ax + 2 * ay + ac).astype(jnp.int32).reshape(1)
    place = jnp.stack([ac, 2 * ax + ay, 4 * ax + 2 * ay + ac]).astype(jnp.int32)
    ts = _tile(S, 256)

    glu = jnp.concatenate([w_glu_val[0], w_glu_gate[0]]).astype(BF16)
    shards = [w_in[0].astype(BF16), glu, w_pool[0].astype(BF16), w_pool_out[0].astype(BF16),
              w_out[0].astype(BF16), w_ff1[0].astype(BF16), w_ff2[0].astype(BF16)]
    wg_in, wg_pool = seq_all_gather("gather_w_in", [shards[0], shards[2]], 1)
    wg_vg, wg_po, wg_out = seq_all_gather("gather_w_mix", [shards[1], shards[3], shards[4]], 2)
    (wg_ff1,) = seq_all_gather("gather_w_ff1", shards[5:6], 3)
    (wg_ff2,) = seq_all_gather("gather_w_ff2", shards[6:7], 11)
    wg_vg = wg_vg.reshape(2 * NDEV, W, D // NDEV)
    nwin = len(POOL_WINDOWS)
    wp_full = jnp.transpose(wg_pool, (1, 0, 2, 3)).reshape(nwin, gw, gw)
    wout_full = wg_out.reshape(1, D, D)
    wff2_full = wg_ff2.reshape(1, 4 * D, D)

    small_names = [b_ada, lam_re, lam_im, log_dt, ssm_b_re, ssm_b_im, ssm_c_re, ssm_c_im, ssm_d,
                   pool_scale, ln1_g, ln1_b, ln2_g, ln2_b]
    small_m = [m_b_ada, m_lam_re, m_lam_im, m_log_dt, m_ssm_b_re, m_ssm_b_im, m_ssm_c_re, m_ssm_c_im,
               m_ssm_d, m_pool_scale, m_ln1_g, m_ln1_b, m_ln2_g, m_ln2_b]
    small_v = [v_b_ada, v_lam_re, v_lam_im, v_log_dt, v_ssm_b_re, v_ssm_b_im, v_ssm_c_re, v_ssm_c_im,
               v_ssm_d, v_pool_scale, v_ln1_g, v_ln1_b, v_ln2_g, v_ln2_b]

    mod, c_all = ada_fwd(c, w_ada[0], b_ada)
    mod = mod.reshape(6, 1, D)
    sh1, sc1, g1, sh2, sc2, g2 = (mod[i] for i in range(6))

    f2, kconst = s5_disc(lam_re[0], lam_im[0], log_dt[0].reshape(G, 1))
    kconst = kconst.reshape(NCONST, SUBLANES, G * P)
    f2r = f2.reshape(2, 1, G * P)
    bt_re = jnp.transpose(ssm_b_re[0], (2, 0, 1)).reshape(H, G * P)
    bt_im = jnp.transpose(ssm_b_im[0], (2, 0, 1)).reshape(H, G * P)
    ct_re = jnp.transpose(ssm_c_re[0], (1, 0, 2)).reshape(H, G * P)
    ct_im = jnp.transpose(ssm_c_im[0], (1, 0, 2)).reshape(H, G * P)
    s5_params = (f2r, bt_re, bt_im, ct_re, ct_im, ssm_d, kconst)

    def e1(t, b):
        xhat, _ = _ln_stats(t[0])
        return [xhat * (1.0 + b[0]) + b[1]], []
    (h1,) = _rowwise("ln_mod1", e1, S, ts, [(x2d, D, 0)], [sc1, sh1], [(D, BF16)], [])

    (proj,) = mm_nn("proj", h1, wg_in, F32, 2)
    z, xsb_all, zp = s5_fwd(proj, s5_params, nblk)
    (vt,) = mm_nn("glu", z, wg_vg, BF16, 4)
    pooled = pool_fwd(proj, W, W, gw)

    def pool_epi(vals, ex, outs):
        a = vals[0]
        outs[0][...] = a
        outs[1][...] = (a * ex[0][...]).astype(BF16)
    tmp = _tile(S, 1024)
    yp, ypool = _mm(
        "pool_mix", "nn", pooled, wp_full.astype(BF16), (S // tmp, nwin, 1),
        pl.BlockSpec((tmp, gw), lambda i, j, k: (i, j)), pl.BlockSpec((1, gw, gw), lambda i, j, k: (j, 0, 0)),
        [(_sds((S, W), F32), pl.BlockSpec((tmp, gw), lambda i, j, k: (i, j))),
         (_sds((S, W), BF16), pl.BlockSpec((tmp, gw), lambda i, j, k: (i, j)))],
        (tmp, gw), 1, gw, None, pool_epi,
        [(pool_scale, pl.BlockSpec((1, gw), lambda i, j, k: (0, j)))])
    (y_b,) = mm_nn("pool_out", ypool, wg_po, BF16, 4)

    cb = D // NDEV
    ga_cb, gb_cb = (2 * W) // cb, (2 * W + D) // cb
    mcb = 4
    wm = mcb * cb
    tsm = _tile(S, 256)

    def merge_call(name, fn, ins, n_out, after=()):
        def body(*refs):
            vals = [r[...].astype(F32) for r in refs[:len(ins)]]
            for r, v in zip(refs[len(ins):], fn(*vals)):
                r[...] = v.astype(r.dtype)
        return pl.pallas_call(
            _with_after(body, len(ins), after), name=name, grid=(S // tsm, NDEV // mcb),
            in_specs=[pl.BlockSpec((tsm, w), f) for (_, w, f) in ins] + [ANY] * len(after),
            out_specs=[pl.BlockSpec((tsm, w), lambda i, j: (i, j)) for (_, w) in n_out],
            out_shape=[_sds((S, cols), BF16) for (cols, _) in n_out],
            compiler_params=_params(("parallel", "parallel")),
        )(*[a for (a, _, _) in ins], *after)

    merge_ins = [(proj, wm, lambda i, j: (i, ga_cb // mcb + j)), (proj, wm, lambda i, j: (i, gb_cb // mcb + j)),
                 (vt, 2 * wm, lambda i, j: (i, j)), (y_b, wm, lambda i, j: (i, j))]

    def val_gate(vtv):
        return (jnp.concatenate([vtv[:, 2 * q * cb:(2 * q + 1) * cb] for q in range(mcb)], axis=1),
                jnp.concatenate([vtv[:, (2 * q + 1) * cb:(2 * q + 2) * cb] for q in range(mcb)], axis=1))

    def merge_f(ga, gb, vtv, yb):
        vv, tt = val_gate(vtv)
        return [_sigmoid(ga) * (vv * _sigmoid(tt)) + _sigmoid(gb) * yb]
    (merged,) = merge_call("merge", merge_f, merge_ins, [(D, wm)])

    (mix,) = mm_nn("mix_out", merged, wout_full, F32, 1)

    def e3(t, b):
        xv, mx = t
        g1v, l1g, l1b, sc2v, sh2v = b
        r1 = ALPHA * xv + g1v * mx
        xh1, _ = _ln_stats(r1)
        x1 = xh1 * l1g + l1b
        xh, _ = _ln_stats(x1)
        return [r1, xh * (1.0 + sc2v) + sh2v], []
    r1, h2 = _rowwise("post_mix", e3, S, ts, [(x2d, D, 0), (mix, D, 0)],
                      [g1, ln1_g, ln1_b, sc2, sh2], [(D, F32), (D, BF16)], [])

    def relu_epi(vals, ex, outs):
        outs[0][...] = jnp.maximum(vals[0], 0.0).astype(BF16)
    (rl,) = mm_nn("ff1", h2, wg_ff1, BF16, 1, epi=relu_epi)

    def square(a):
        return a * a
    (y2,) = mm_nn("ff2", rl, wff2_full, F32, 1, pro=square)

    def e4(t, b):
        r1v, y2v, tg = t
        g2v, l1g, l1b, l2g, l2b = b
        xh1, _ = _ln_stats(r1v)
        x1 = xh1 * l1g + l1b
        r2 = ALPHA * x1 + g2v * y2v
        xh2, rs2 = _ln_stats(r2)
        err = xh2 * l2g + l2b - tg
        dx2 = err * (1.0 / D)
        dr2 = _ln_bwd(dx2 * l2g, xh2, rs2)
        lsum = jnp.sum(_colsum(err * err), axis=1, keepdims=True) * (0.5 / D)
        return ([ALPHA * dr2, g2v * dr2],
                [jnp.broadcast_to(lsum, (1, LANES)), _colsum(dx2 * xh2), _colsum(dx2), _colsum(dr2 * y2v)])
    dx1a, dy2, loss_acc, g_ln2g, g_ln2b, d_g2 = _rowwise(
        "head", e4, S, ts, [(r1, D, 0), (y2, D, 0), (tgt, D, 0)], [g2, ln1_g, ln1_b, ln2_g, ln2_b],
        [(D, F32), (D, BF16)], [LANES, D, D, D])

    tn_ff = _tile(4 * D, 1024)

    def dff_epi(vals, ex, outs):
        outs[0][...] = (vals[0] * (2.0 * ex[0][...].astype(F32))).astype(BF16)
    tmf = _tile(S, 1024)
    (da1,) = mm_nt("d_ff2", dy2, wff2_full, BF16, 1, tn=tn_ff, epi=dff_epi,
                   extras=[(rl, pl.BlockSpec((tmf, tn_ff), lambda i, j, k: (i, j)))])
    gw_ff2 = mm_tn("gw_ff2", rl, dy2, BF16, NDEV, 0, pro=square)
    gw_ff1 = mm_tn("gw_ff1", h2, da1, BF16, NDEV, 1)
    tok, wait_pair_a = pair_exchange("pair_exchange_ff", [gw_ff2, gw_ff1], 4)
    (dh2,) = mm_nt("d_ff1", da1, wg_ff1, F32, 4, after=[tok])

    def e5(t, b):
        dh2v, r1v, dx1av, mx = t
        sc2v, l1g, l1b, g1v = b
        xh1, rs1 = _ln_stats(r1v)
        x1 = xh1 * l1g + l1b
        xh, rs = _ln_stats(x1)
        dx1 = dx1av + _ln_bwd(dh2v * (1.0 + sc2v), xh, rs)
        dr1 = _ln_bwd(dx1 * l1g, xh1, rs1)
        return ([ALPHA * dr1, g1v * dr1],
                [_colsum(dh2v * xh), _colsum(dh2v), _colsum(dx1 * xh1), _colsum(dx1), _colsum(dr1 * mx)])
    dxa, dmix, d_sc2, d_sh2, g_ln1g, g_ln1b, d_g1 = _rowwise(
        "post_mix_bwd", e5, S, ts, [(dh2, D, 0), (r1, D, 0), (dx1a, D, 0), (mix, D, 0)],
        [sc2, ln1_g, ln1_b, g1], [(D, F32), (D, BF16)], [D, D, D, D, D])

    (dmerged,) = mm_nt("d_mix_out", dmix, wout_full, BF16, 1)
    gw_out = mm_tn("gw_out", merged, dmix, BF16, NDEV, 0)
    grads_a, got_a = wait_pair_a(gw_out)
    parts_a = [pair_sum("pair_sum_ff%d" % i, g, t, place) for i, (g, t) in enumerate(zip(grads_a, got_a))]
    tok, wait_chip_a = chip_exchange("chip_exchange_ff", parts_a, 5)

    def merge_b(ga, gb, vtv, yb, dm):
        vv, tt = val_gate(vtv)
        sa, sb, st = _sigmoid(ga), _sigmoid(gb), _sigmoid(tt)
        dya = dm * sa
        dv, dt = dya * st, dya * vv * st * (1.0 - st)
        dvt_tile = jnp.concatenate([t[:, q * cb:(q + 1) * cb] for q in range(mcb) for t in (dv, dt)], axis=1)
        return [dm * (vv * st) * sa * (1.0 - sa), dm * yb * sb * (1.0 - sb), dvt_tile, dm * sb]
    dga, dgb_, dvt, dy_b = merge_call(
        "merge_bwd", merge_b, merge_ins + [(dmerged, wm, lambda i, j: (i, j))],
        [(D, wm), (D, wm), (2 * D, 2 * wm), (D, wm)], after=[tok])

    (dypool,) = mm_nt("d_pool_out", dy_b, wg_po, F32, NDEV)
    gw_po = mm_tn("gw_pool_out", ypool, dy_b, BF16, NDEV, 4)

    def e7(t, b):
        return [t[0] * b[0]], [_colsum(t[0] * t[1])]
    dyp, g_pscale = _rowwise("pool_scale_bwd", e7, S, ts, [(dypool, W, 0), (yp, W, 0)],
                             [pool_scale], [(W, BF16)], [W])
    (dpooled,) = _mm(
        "d_pool_mix", "nt", dyp, wp_full.astype(BF16), (S // tmp, nwin, 1),
        pl.BlockSpec((tmp, gw), lambda i, j, k: (i, j)), pl.BlockSpec((1, gw, gw), lambda i, j, k: (j, 0, 0)),
        [(_sds((S, W), F32), pl.BlockSpec((tmp, gw), lambda i, j, k: (i, j)))], (tmp, gw), 1, gw)
    tkp = _tile(S, 2048)
    gw_pool = _mm(
        "gw_pool", "tn", pooled, dyp, (nwin, 1, S // tkp),
        pl.BlockSpec((tkp, gw), lambda i, j, k: (k, i)), pl.BlockSpec((tkp, gw), lambda i, j, k: (k, i)),
        [(_sds((nwin, gw, gw), BF16), pl.BlockSpec((1, gw, gw), lambda i, j, k: (i, 0, 0)))],
        (gw, gw), 1, gw, stacked_out=True)[0]
    du_pool = pool_bwd(dpooled, gw)

    (dz,) = mm_nt("d_glu", dvt, wg_vg, BF16, 2 * NDEV)
    gw_vg = mm_tn("gw_glu", z, dvt, BF16, 2 * NDEV, 4)
    gw_pool_st = jnp.transpose(gw_pool.reshape(nwin, NDEV, gw // NDEV, gw), (1, 0, 2, 3))
    grads_b = [gw_out, gw_po, gw_pool_st, gw_vg.reshape(NDEV, 2, W, D // NDEV)]
    tok, wait_pair_b = pair_exchange("pair_exchange_mix", grads_b, 6)
    du_ssm, g_bt_re, g_bt_im, g_ct_re, g_ct_im, g_f, g_d, g_a = s5_bwd(
        proj, xsb_all, dz, zp, s5_params, nblk, after=[tok])
    grads_b, got_b = wait_pair_b(du_ssm)
    parts_b = [pair_sum("pair_sum_mix%d" % i, g, t, place) for i, (g, t) in enumerate(zip(grads_b, got_b))]
    tok, wait_chip_b = chip_exchange("chip_exchange_mix", parts_b, 7)

    dproj = jnp.concatenate([du_ssm, du_pool, dga, dgb_], axis=1)
    gw_in = mm_tn("gw_in", h1, dproj, BF16, NDEV, 1, after=[tok])
    tok, wait_pair_c = pair_exchange("pair_exchange_in", [gw_in], 8)
    (dh1,) = mm_nt("d_proj", dproj, wg_in, F32, 4, after=[tok])
    grads_c, got_c = wait_pair_c(dh1)
    parts_c = [pair_sum("pair_sum_in", grads_c[0], got_c[0], place)]
    tok, wait_chip_c = chip_exchange("chip_exchange_in", parts_c, 9)

    def e10(t, b):
        dh1v, xv, dxav = t
        xh, rs = _ln_stats(xv)
        return ([dxav + _ln_bwd(dh1v * (1.0 + b[0]), xh, rs)],
                [_colsum(dh1v * xh), _colsum(dh1v)])
    grad_x, d_sc1, d_sh1 = _rowwise("ln_mod1_bwd", e10, S, ts, [(dh1, D, 0), (x2d, D, 0), (dxa, D, 0)],
                                    [sc1], [(D, F32)], [D, D], after=[tok])

    g_b_re = jnp.transpose(g_bt_re.reshape(H, G, P), (1, 0, 2))
    g_b_im = jnp.transpose(g_bt_im.reshape(H, G, P), (1, 0, 2))
    g_c_re = jnp.transpose(g_ct_re.reshape(H, G, P), (1, 0, 2))
    g_c_im = jnp.transpose(g_ct_im.reshape(H, G, P), (1, 0, 2))
    d_ab = jnp.transpose(g_a.reshape(nblk, 2, GPB, P), (1, 0, 2, 3)).reshape(2, G, P)
    g_lr, g_li, g_ldt = s5_disc_bwd(lam_re[0], lam_im[0], log_dt[0].reshape(G, 1), d_ab,
                                    g_f.reshape(2, G, P))

    dmod = jnp.concatenate([d_sh1, d_sc1, d_g1, d_sh2, d_sc2, d_g2], axis=1)
    small_g = [dmod, g_lr, g_li, g_ldt, g_b_re, g_b_im, g_c_re, g_c_im, g_d, g_pscale,
               g_ln1g, g_ln1b, g_ln2g, g_ln2b, loss_acc]
    packed_g = _small_pack(small_g)
    (parts_all,) = seq_all_gather("gather_small", [packed_g], 10)
    glu_w = jnp.stack([w_glu_val[0], w_glu_gate[0]])
    glu_m = jnp.stack([m_w_glu_val[0], m_w_glu_gate[0]])
    glu_v = jnp.stack([v_w_glu_val[0], v_w_glu_gate[0]])
    wmv = [(w_ff2[0], m_w_ff2[0], v_w_ff2[0]), (w_ff1[0], m_w_ff1[0], v_w_ff1[0]),
           (w_out[0], m_w_out[0], v_w_out[0]), (w_pool_out[0], m_w_pool_out[0], v_w_pool_out[0]),
           (w_pool[0], m_w_pool[0], v_w_pool[0]), (glu_w, glu_m, glu_v)]
    _, got3_a = wait_chip_a(packed_g)
    upd = [adamw_sharded("adamw_%d" % i, g, p, t, w, m, v, place)
           for i, (g, p, t, (w, m, v)) in enumerate(zip(grads_a, got_a, got3_a, wmv[:2]))]
    _, got3_b = wait_chip_b(upd[-1][0])
    upd += [adamw_sharded("adamw_%d" % (2 + i), g, p, t, w, m, v, place)
            for i, (g, p, t, (w, m, v)) in enumerate(zip(grads_b, got_b, got3_b, wmv[2:]))]
    u_ff2, u_ff1, u_out, u_po, u_pool, u_glu = upd

    gsum = sum_small(parts_all, after=[upd[-1][0]])
    def swap_b(ts_):
        return [jnp.swapaxes(t, 2, 3) if i in (4, 5) else t for i, t in enumerate(ts_)]

    sg = _small_unpack(gsum, [t.shape for t in swap_b(small_names)] + [(1, LANES)])
    loss, sg = sg[-1][0, 0], sg[:-1]
    sd, sm, sv = adamw_natural(sg, swap_b(small_names), swap_b(small_m), swap_b(small_v))
    sg, sd, sm, sv = swap_b(sg), swap_b(sd), swap_b(sm), swap_b(sv)

    nmod = 6 * D
    dmod_all = parts_all[:, :nmod // LANES, :].reshape(NDEV, nmod)
    c_all_t = jnp.transpose(c_all.reshape(NDEV, D))
    ada_out = adamw_ada(c_all_t, dmod_all, w_ada[0], m_w_ada[0], v_w_ada[0], my_dev)
    _, got3_c = wait_chip_c(ada_out[0])
    u_in = adamw_sharded("adamw_6", grads_c[0], got_c[0], got3_c[0], w_in[0], m_w_in[0], v_w_in[0], place)

    def pick(k):
        return [ada_out[k][None], sg_sd[k][0], u_in[k][None]] + [t for t in sg_sd[k][1:9]] + \
               [u_glu[k][0][None], u_glu[k][1][None], u_pool[k][None], sg_sd[k][9], u_po[k][None],
                u_out[k][None], sg_sd[k][10], sg_sd[k][11], u_ff1[k][None], u_ff2[k][None],
                sg_sd[k][12], sg_sd[k][13]]

    sg_sd = [sg, sd, sm, sv]
    return (loss, grad_x[None], *pick(0), *pick(1), *pick(2), *pick(3))
```

```python
import functools
import math

import jax
import jax.numpy as jnp
from jax import lax
from jax.experimental import pallas as pl
from jax.experimental.pallas import tpu as pltpu
from jax.experimental.pallas import tpu_sc as plsc

F32 = jnp.float32
BF16 = jnp.bfloat16
MESH = pl.DeviceIdType.MESH
NDEV = 8
NCHIP = 4

SSM_GROUP = 16
SSM_STATE = 64
GROUPS_PER_BLOCK = 8
POOL_WINDOWS = (2, 4, 8, 16)
LN_EPS = 1e-5
ALPHA = 2.0 ** 0.25
ADAM_LR, ADAM_B1, ADAM_B2, ADAM_EPS, ADAM_WD, ADAM_STEP = 0.001, 0.9, 0.999, 1e-08, 0.01, 10
SUBLANES = 8
LANES = 128
VMEM_LIMIT = 56 * 1024 * 1024


def _params(sem=None, vmem=VMEM_LIMIT):
    return pltpu.CompilerParams(dimension_semantics=sem, vmem_limit_bytes=vmem)


def _tile(n, pref):
    if n <= pref:
        return n
    t = 1 << (pref.bit_length() - 1)
    while n % t:
        t //= 2
    return t


def _cast_epi(vals, ex, outs):
    c = vals[0].shape[1]
    for s, v in enumerate(vals):
        outs[0][:, s * c:(s + 1) * c] = v.astype(outs[0].dtype)


ANY = pl.BlockSpec(memory_space=pl.ANY)


def _with_after(body, n_in, after):
    if not after:
        return body
    n_af = len(after)

    def wrapped(*refs):
        return body(*refs[:n_in], *refs[n_in + n_af:])
    return wrapped


def _mm(name, kind, a, b, grid, a_spec, b_spec, outs, acc_shape, nsub=1, c=None,
        pro=None, epi=None, extras=(), stacked_out=False, after=()):
    nk = grid[2]
    n_ex, n_out = len(extras), len(outs)

    def finish(vals, ex, out_refs):
        if epi is not None:
            epi(vals, ex, out_refs)
        elif stacked_out:
            for s, v in enumerate(vals):
                out_refs[0][s] = v.astype(out_refs[0].dtype)
        else:
            _cast_epi(vals, ex, out_refs)

    def body(*refs):
        mm_step(refs[0], refs[1], refs[2:2 + n_ex], refs[2 + n_ex:2 + n_ex + n_out], refs[-1])

    def mm_step(a_ref, b_ref, ex, out_refs, acc):
        k = pl.program_id(2)
        av = a_ref[...]
        if pro is not None:
            av = pro(av)
        if kind == "nn":
            prods = [jnp.dot(av, b_ref[s], preferred_element_type=F32) for s in range(nsub)]
        elif kind == "nt":
            t = None
            for s in range(nsub):
                d = lax.dot_general(av[:, s * c:(s + 1) * c], b_ref[s], (((1,), (1,)), ((), ())),
                                    preferred_element_type=F32)
                t = d if t is None else t + d
            prods = [t]
        else:
            t = lax.dot_general(av, b_ref[...], (((0,), (0,)), ((), ())), preferred_element_type=F32)
            prods = [t[:, s * c:(s + 1) * c] for s in range(nsub)] if stacked_out else [t]
        if nk == 1:
            finish(prods, ex, out_refs)
            return
        w = prods[0].shape[1]

        @pl.when(k == 0)
        def _():
            for s, p in enumerate(prods):
                acc[:, s * w:(s + 1) * w] = p

        @pl.when(jnp.logical_and(k > 0, k < nk - 1))
        def _():
            for s, p in enumerate(prods):
                acc[:, s * w:(s + 1) * w] += p

        @pl.when(k == nk - 1)
        def _():
            finish([acc[:, s * w:(s + 1) * w] + p for s, p in enumerate(prods)], ex, out_refs)

    return pl.pallas_call(
        _with_after(body, 2 + n_ex, after), name=name, grid=grid,
        in_specs=[a_spec, b_spec] + [e[1] for e in extras] + [ANY] * len(after),
        out_specs=[o[1] for o in outs],
        out_shape=[o[0] for o in outs],
        scratch_shapes=[pltpu.VMEM(acc_shape, F32)] if nk > 1 else [],
        compiler_params=_params(("parallel", "parallel", "arbitrary")),
    )(a, b, *[e[0] for e in extras], *after)


def _sds(shape, dtype):
    return jax.ShapeDtypeStruct(shape, dtype)


def mm_nn(name, a, b3, out_dtype, nsub, tm=1024, tk=2048, tn=None, pro=None, epi=None,
          extras=(), after=()):
    M = a.shape[0]
    nb, K, cdim = b3.shape
    tm, tk = _tile(M, tm), _tile(K, tk)
    if nb == 1:
        tn = _tile(cdim, tn or 1024)
        nsub, c, nj = 1, tn, cdim // tn
        b_spec = pl.BlockSpec((1, tk, tn), lambda i, j, k: (0, k, j))
        N = cdim
    else:
        c, nj, tn = cdim, nb // nsub, nsub * cdim
        b_spec = pl.BlockSpec((nsub, tk, cdim), lambda i, j, k: (j, k, 0))
        N = nb * cdim
    a_spec = pl.BlockSpec((tm, tk), lambda i, j, k: (i, k))
    grid = (M // tm, nj, K // tk)
    outs = [(_sds((M, N), out_dtype), pl.BlockSpec((tm, tn), lambda i, j, k: (i, j)))]
    return _mm(name, "nn", a, b3, grid, a_spec, b_spec, outs, (tm, tn), nsub, c, pro, epi, extras,
               after=after)


def mm_nt(name, a, b3, out_dtype, nsub, tm=1024, tn=1024, epi=None, extras=(), after=()):
    M = a.shape[0]
    nb, N, cdim = b3.shape
    tm, tn = _tile(M, tm), _tile(N, tn)
    if nb == 1:
        tk = _tile(cdim, 2048)
        nsub, c, nk = 1, tk, cdim // tk
        b_spec = pl.BlockSpec((1, tn, tk), lambda i, j, k: (0, j, k))
    else:
        c, nk, tk = cdim, nb // nsub, nsub * cdim
        b_spec = pl.BlockSpec((nsub, tn, cdim), lambda i, j, k: (k, j, 0))
    a_spec = pl.BlockSpec((tm, tk), lambda i, j, k: (i, k))
    grid = (M // tm, N // tn, nk)
    outs = [(_sds((M, N), out_dtype), pl.BlockSpec((tm, tn), lambda i, j, k: (i, j)))]
    return _mm(name, "nt", a, b3, grid, a_spec, b_spec, outs, (tm, tn), nsub, c, None, epi, extras,
               after=after)


def mm_tn(name, a, b, out_dtype, nb, nsub, tma=1024, tk=2048, pro=None, after=()):
    S, Ka = a.shape
    N = b.shape[1]
    tk, tma = _tile(S, tk), _tile(Ka, tma)
    a_spec = pl.BlockSpec((tk, tma), lambda i, j, k: (k, i))
    if nsub == 0:
        tn = _tile(N, 1024)
        res = _mm(name, "tn", a, b, (Ka // tma, N // tn, S // tk), a_spec,
                  pl.BlockSpec((tk, tn), lambda i, j, k: (k, j)),
                  [(_sds((Ka, N), out_dtype), pl.BlockSpec((tma, tn), lambda i, j, k: (i, j)))],
                  (tma, tn), 1, tn, pro, None, (), after=after)[0]
        return res.reshape(nb, Ka // nb, N)
    c = N // nb
    tn = nsub * c
    outs = [(_sds((nb, Ka, c), out_dtype), pl.BlockSpec((nsub, tma, c), lambda i, j, k: (j, i, 0)))]
    return _mm(name, "tn", a, b, (Ka // tma, nb // nsub, S // tk), a_spec,
               pl.BlockSpec((tk, tn), lambda i, j, k: (k, j)), outs, (tma, tn), nsub, c,
               pro, None, (), stacked_out=True, after=after)[0]


def _rowwise(name, fn, S, ts, tiled, bcast, tiled_out, acc_out, after=()):
    nt, nb, no, na = len(tiled), len(bcast), len(tiled_out), len(acc_out)

    def body(*refs):
        tin = [r[...] for r in refs[:nt]]
        bin_ = [r[...] for r in refs[nt:nt + nb]]
        o_refs = refs[nt + nb:nt + nb + no]
        a_refs = refs[nt + nb + no:]
        touts, aouts = fn(tin, bin_)
        for r, v in zip(o_refs, touts):
            r[...] = v.astype(r.dtype)
        i = pl.program_id(0)

        @pl.when(i == 0)
        def _():
            for r, v in zip(a_refs, aouts):
                r[...] = v

        @pl.when(i > 0)
        def _():
            for r, v in zip(a_refs, aouts):
                r[...] += v

    in_specs = [pl.BlockSpec((ts, w), functools.partial(lambda i, cb: (i, cb), cb=cb))
                for (_, w, cb) in tiled]
    in_specs += [pl.BlockSpec(b.shape, lambda i: (0, 0)) for b in bcast]
    out_specs = [pl.BlockSpec((ts, w), lambda i: (i, 0)) for (w, _) in tiled_out]
    out_specs += [pl.BlockSpec((1, w), lambda i: (0, 0)) for w in acc_out]
    out_shape = [_sds((S, w), d) for (w, d) in tiled_out] + [_sds((1, w), F32) for w in acc_out]
    return pl.pallas_call(
        _with_after(body, nt + nb, after), name=name, grid=(S // ts,),
        in_specs=in_specs + [ANY] * len(after), out_specs=out_specs,
        out_shape=out_shape, compiler_params=_params(("arbitrary",)),
    )(*[t[0] for t in tiled], *bcast, *after)


def _ln_stats(v):
    mu = jnp.mean(v, axis=-1, keepdims=True)
    vc = v - mu
    var = jnp.mean(vc * vc, axis=-1, keepdims=True)
    rstd = lax.rsqrt(var + LN_EPS)
    return vc * rstd, rstd


def _ln_bwd(dxhat, xhat, rstd):
    return rstd * (dxhat - jnp.mean(dxhat, axis=-1, keepdims=True)
                   - xhat * jnp.mean(dxhat * xhat, axis=-1, keepdims=True))


def _colsum(v):
    return jnp.sum(v, axis=0, keepdims=True)


def _sigmoid(v):
    return 1.0 / (1.0 + jnp.exp(-v))


_GELU_C = math.sqrt(2.0 / math.pi)


def _gelu(v):
    return 0.5 * v * (1.0 + jnp.tanh(_GELU_C * (v + 0.044715 * v * v * v)))


def _gelu_grad(v):
    t = jnp.tanh(_GELU_C * (v + 0.044715 * v * v * v))
    return 0.5 * (1.0 + t) + 0.5 * v * (1.0 - t * t) * _GELU_C * (1.0 + 3 * 0.044715 * v * v)


def _disc(lr, li, ldt):
    dt = jnp.exp(ldt)
    mag = jnp.exp(lr * dt)
    ang = li * dt
    ab_re = mag * jnp.cos(ang)
    ab_im = mag * jnp.sin(ang)
    num_re = ab_re - 1.0
    num_im = ab_im
    den = lr * lr + li * li
    f_re = (num_re * lr + num_im * li) / den
    f_im = (num_im * lr - num_re * li) / den
    return ab_re, ab_im, f_re, f_im


def _cmul(ar, ai, br, bi):
    return ar * br - ai * bi, ar * bi + ai * br


SCAN_FOLD = 4
NCONST = 18


def s5_disc(lam_re, lam_im, log_dt):
    G, P = lam_re.shape

    def body(lr_ref, li_ref, ldt_ref, f_ref, k_ref):
        ab_re, ab_im, f_re, f_im = _disc(lr_ref[...], li_ref[...], ldt_ref[...])
        f_ref[0] = f_re
        f_ref[1] = f_im
        fr, fi = ab_re, ab_im
        for _ in range(SCAN_FOLD - 1):
            fr, fi = _cmul(fr, fi, ab_re, ab_im)
        pr, pi = [fr], [fi]
        for _ in range(SUBLANES - 1):
            nr, ni = _cmul(pr[-1], pi[-1], fr, fi)
            pr.append(nr)
            pi.append(ni)
        zero = jnp.zeros_like(ab_re)
        for r in range(SUBLANES):
            k_ref[16, r] = ab_re
            k_ref[17, r] = ab_im
        for n, sh in enumerate((1, 2, 4)):
            for r in range(SUBLANES):
                k_ref[2 * n, r] = pr[sh - 1] if r >= sh else zero
                k_ref[2 * n + 1, r] = pi[sh - 1] if r >= sh else zero
                k_ref[8 + 2 * n, r] = pr[sh - 1] if r + sh < SUBLANES else zero
                k_ref[8 + 2 * n + 1, r] = -pi[sh - 1] if r + sh < SUBLANES else zero
        for r in range(SUBLANES):
            k_ref[6, r] = pr[r]
            k_ref[7, r] = pi[r]
            k_ref[14, r] = pr[SUBLANES - 1 - r]
            k_ref[15, r] = -pi[SUBLANES - 1 - r]

    vm = pl.BlockSpec(memory_space=pltpu.VMEM)
    return pl.pallas_call(
        body, name="s5_disc", in_specs=[vm, vm, vm], out_specs=[vm, vm],
        out_shape=[_sds((2, G, P), F32), _sds((NCONST, SUBLANES, G, P), F32)],
    )(lam_re, lam_im, log_dt)


def s5_disc_bwd(lam_re, lam_im, log_dt, d_ab, d_f):
    G, P = lam_re.shape

    def body(lr_ref, li_ref, ldt_ref, dab_ref, df_ref, glr_ref, gli_ref, gdt_ref):
        _, vjp = jax.vjp(_disc, lr_ref[...], li_ref[...], ldt_ref[...])
        glr, gli, gdt = vjp((dab_ref[0], dab_ref[1], df_ref[0], df_ref[1]))
        glr_ref[...] = glr
        gli_ref[...] = gli
        gdt_ref[...] = gdt

    vm = pl.BlockSpec(memory_space=pltpu.VMEM)
    return pl.pallas_call(
        body, name="s5_disc_bwd", in_specs=[vm] * 5, out_specs=[vm] * 3,
        out_shape=[_sds((G, P), F32), _sds((G, P), F32), _sds((G, 1), F32)],
    )(lam_re, lam_im, log_dt, d_ab, d_f)


def _group_mask(cw, nst):
    row = lax.broadcasted_iota(jnp.int32, (cw, 2 * nst), 0) // SSM_GROUP
    col = (lax.broadcasted_iota(jnp.int32, (cw, 2 * nst), 1) % nst) // SSM_STATE
    return row == col


def _spread(t, mask):
    reps = mask.shape[0] // t.shape[0]
    return jnp.where(mask, jnp.tile(t, (reps, 1)), 0.0).astype(BF16)


def _gather_groups(t, mask):
    t = jnp.where(mask, t, 0.0)
    out = t[0:SSM_GROUP]
    for g in range(1, t.shape[0] // SSM_GROUP):
        out = out + t[g * SSM_GROUP:(g + 1) * SSM_GROUP]
    return out


def _s5_operands(f_ref, br_ref, bi_ref, cr_ref, ci_ref, mask):
    fr, fi = f_ref[0], f_ref[1]
    br, bi = br_ref[...], bi_ref[...]
    bm = _spread(jnp.concatenate([fr * br - fi * bi, fr * bi + fi * br], axis=1), mask)
    cm = _spread(jnp.concatenate([cr_ref[...], -ci_ref[...]], axis=1), mask)
    return bm, cm


def _planes_put(ref, val):
    for c in range(ref.shape[0]):
        ref[c] = val[:, c * LANES:(c + 1) * LANES]


def _planes_get(ref):
    return jnp.concatenate([ref[c] for c in range(ref.shape[0])], axis=1)


def _rows_ld(ref, start, lo, hi):
    rows = pl.ds(start, SUBLANES, stride=SCAN_FOLD)
    return jnp.concatenate([ref[c, rows, :] for c in range(lo // LANES, hi // LANES)], axis=1)


def _rows_st(ref, start, lo, val):
    rows = pl.ds(start, SUBLANES, stride=SCAN_FOLD)
    for k in range(val.shape[1] // LANES):
        ref[lo // LANES + k, rows, :] = val[:, k * LANES:(k + 1) * LANES]


def _phases(ref, base, lo, hi):
    return [_rows_ld(ref, base + j, lo, hi) for j in range(SCAN_FOLD)]


def _row_bcast(v, r):
    return jnp.broadcast_to(v[r:r + 1, :], v.shape)


def _scan_fwd(xs, k_ref, nst):
    m = SCAN_FOLD
    ngroup = xs.shape[1] // (SUBLANES * m)
    row = lax.broadcasted_iota(jnp.int32, (SUBLANES, nst), 0)

    def step(t, carry):
        cr, ci = carry
        base = pl.multiple_of(t * (SUBLANES * m), SUBLANES * m)
        ar, ai = k_ref[16], k_ref[17]
        pr, pi = _phases(xs, base, 0, nst), _phases(xs, base, nst, 2 * nst)
        vr, vi = pr[0], pi[0]
        for j in range(1, m):
            vr, vi = pr[j] + ar * vr - ai * vi, pi[j] + ar * vi + ai * vr
        for n, sh in enumerate((1, 2, 4)):
            sr = pltpu.roll(vr, sh, 0)
            si = pltpu.roll(vi, sh, 0)
            mr, mi = k_ref[2 * n], k_ref[2 * n + 1]
            vr, vi = vr + mr * sr - mi * si, vi + mr * si + mi * sr
        qr, qi = k_ref[6], k_ref[7]
        vr, vi = vr + qr * cr - qi * ci, vi + qr * ci + qi * cr
        _rows_st(xs, base + m - 1, 0, vr)
        _rows_st(xs, base + m - 1, nst, vi)
        xr = jnp.where(row == 0, cr, pltpu.roll(vr, 1, 0))
        xi = jnp.where(row == 0, ci, pltpu.roll(vi, 1, 0))
        for j in range(m - 1):
            xr, xi = pr[j] + ar * xr - ai * xi, pi[j] + ar * xi + ai * xr
            _rows_st(xs, base + j, 0, xr)
            _rows_st(xs, base + j, nst, xi)
        return _row_bcast(vr, SUBLANES - 1), _row_bcast(vi, SUBLANES - 1)

    zero = jnp.zeros((SUBLANES, nst), F32)
    lax.fori_loop(0, ngroup, step, (zero, zero))


def _scan_bwd(g, xs, k_ref, nst):
    m = SCAN_FOLD
    ngroup = g.shape[1] // (SUBLANES * m)
    row = lax.broadcasted_iota(jnp.int32, (SUBLANES, nst), 0)

    def step(tt, carry):
        cr, ci, dar, dai = carry
        t = ngroup - 1 - tt
        base = pl.multiple_of(t * (SUBLANES * m), SUBLANES * m)
        ar, ai = k_ref[16], -k_ref[17]
        dr, di = _phases(g, base, 0, nst), _phases(g, base, nst, 2 * nst)
        wr, wi = dr[m - 1], di[m - 1]
        for j in range(m - 2, -1, -1):
            wr, wi = dr[j] + ar * wr - ai * wi, di[j] + ar * wi + ai * wr
        for n, sh in enumerate((1, 2, 4)):
            sr = pltpu.roll(wr, SUBLANES - sh, 0)
            si = pltpu.roll(wi, SUBLANES - sh, 0)
            mr, mi = k_ref[8 + 2 * n], k_ref[8 + 2 * n + 1]
            wr, wi = wr + mr * sr - mi * si, wi + mr * si + mi * sr
        qr, qi = k_ref[14], k_ref[15]
        wr, wi = wr + qr * cr - qi * ci, wi + qr * ci + qi * cr
        gr, gi = [None] * m, [None] * m
        gr[0], gi[0] = wr, wi
        nr = jnp.where(row == SUBLANES - 1, cr, pltpu.roll(wr, SUBLANES - 1, 0))
        ni = jnp.where(row == SUBLANES - 1, ci, pltpu.roll(wi, SUBLANES - 1, 0))
        for j in range(m - 1, 0, -1):
            nr, ni = dr[j] + ar * nr - ai * ni, di[j] + ar * ni + ai * nr
            gr[j], gi[j] = nr, ni
        for j in range(m):
            _rows_st(g, base + j, 0, gr[j])
            _rows_st(g, base + j, nst, gi[j])
        xr, xi = _phases(xs, base, 0, nst), _phases(xs, base, nst, 2 * nst)
        pbase = pl.multiple_of(jnp.maximum(t - 1, 0) * (SUBLANES * m), SUBLANES * m)
        live = (t > 0).astype(F32)
        lr = _row_bcast(_rows_ld(xs, pbase + m - 1, 0, nst), SUBLANES - 1) * live
        li = _row_bcast(_rows_ld(xs, pbase + m - 1, nst, 2 * nst), SUBLANES - 1) * live
        xmr = [jnp.where(row == 0, lr, pltpu.roll(xr[m - 1], 1, 0))] + xr[:m - 1]
        xmi = [jnp.where(row == 0, li, pltpu.roll(xi[m - 1], 1, 0))] + xi[:m - 1]
        for j in range(m):
            dar = dar + gr[j] * xmr[j] + gi[j] * xmi[j]
            dai = dai + gi[j] * xmr[j] - gr[j] * xmi[j]
        return _row_bcast(wr, 0), _row_bcast(wi, 0), dar, dai

    zero = jnp.zeros((SUBLANES, nst), F32)
    _, _, dar, dai = lax.fori_loop(0, ngroup, step, (zero, zero, zero, zero))
    return _colsum(dar), _colsum(dai)


def _s5_param_specs(cw, nst):
    hp = pl.BlockSpec((SSM_GROUP, nst), lambda b: (0, b))
    return [pl.BlockSpec((2, 1, nst), lambda b: (0, 0, b)), hp, hp, hp, hp,
            pl.BlockSpec((1, cw), lambda b: (0, b)),
            pl.BlockSpec((NCONST, SUBLANES, nst), lambda b: (0, 0, b))]


def s5_fwd(proj, params, nb):
    S = proj.shape[0]
    nst = params[1].shape[1] // nb
    cw = nst // SSM_STATE * SSM_GROUP

    def body(u_ref, f_ref, br_ref, bi_ref, cr_ref, ci_ref, d_ref, k_ref, z_ref, xsb_ref, zp_ref, xs):
        bm, cm = _s5_operands(f_ref, br_ref, bi_ref, cr_ref, ci_ref, _group_mask(cw, nst))
        u = u_ref[...]
        _planes_put(xs, jnp.dot(u.astype(BF16), bm, preferred_element_type=F32))
        _scan_fwd(xs, k_ref, nst)
        xsb = _planes_get(xs).astype(BF16)
        xsb_ref[...] = xsb
        y = lax.dot_general(xsb, cm, (((1,), (1,)), ((), ())), preferred_element_type=F32)
        y = y + d_ref[...] * u
        z_ref[...] = _gelu(y).astype(BF16)
        zp_ref[...] = _gelu_grad(y).astype(BF16)

    return pl.pallas_call(
        body, name="s5_fwd", grid=(nb,),
        in_specs=[pl.BlockSpec((S, cw), lambda b: (0, b))] + _s5_param_specs(cw, nst),
        out_specs=[pl.BlockSpec((S, cw), lambda b: (0, b)), pl.BlockSpec((S, 2 * nst), lambda b: (0, b)),
                   pl.BlockSpec((S, cw), lambda b: (0, b))],
        out_shape=[_sds((S, nb * cw), BF16), _sds((S, nb * 2 * nst), BF16), _sds((S, nb * cw), BF16)],
        scratch_shapes=[pltpu.VMEM((2 * nst // LANES, S, LANES), F32)],
        compiler_params=_params(("arbitrary",)),
    )(proj, *params)


def s5_bwd(proj, xsb_all, dz, zp, params, nb, after=()):
    S = proj.shape[0]
    nst = params[1].shape[1] // nb
    cw = nst // SSM_STATE * SSM_GROUP

    def body(u_ref, xsb_ref, dz_ref, zp_ref, f_ref, br_ref, bi_ref, cr_ref, ci_ref, d_ref, k_ref,
             du_ref, gbr_ref, gbi_ref, gcr_ref, gci_ref, gf_ref, gd_ref, ga_ref, xs, g):
        mask = _group_mask(cw, nst)
        bm, cm = _s5_operands(f_ref, br_ref, bi_ref, cr_ref, ci_ref, mask)
        u = u_ref[...]
        ub = u.astype(BF16)
        d = d_ref[...]
        xsb = xsb_ref[...]
        _planes_put(xs, xsb.astype(F32))
        dy = dz_ref[...].astype(F32) * zp_ref[...].astype(F32)
        gd_ref[...] = _colsum(dy * u)
        dyb = dy.astype(BF16)
        gc = _gather_groups(lax.dot_general(dyb, xsb, (((0,), (0,)), ((), ())),
                                            preferred_element_type=F32), mask)
        gcr_ref[...] = gc[:, :nst]
        gci_ref[...] = -gc[:, nst:]
        _planes_put(g, jnp.dot(dyb, cm, preferred_element_type=F32))
        ar, ai = _scan_bwd(g, xs, k_ref, nst)
        ga_ref[0, 0:1, :] = ar
        ga_ref[0, 1:2, :] = ai
        gb = _planes_get(g).astype(BF16)
        du = lax.dot_general(gb, bm, (((1,), (1,)), ((), ())), preferred_element_type=F32) + d * dy
        du_ref[...] = du.astype(BF16)
        gbb = _gather_groups(lax.dot_general(ub, gb, (((0,), (0,)), ((), ())),
                                             preferred_element_type=F32), mask)
        dr, di = gbb[:, :nst], gbb[:, nst:]
        fr, fi = f_ref[0], f_ref[1]
        br, bi = br_ref[...], bi_ref[...]
        gbr_ref[...] = fr * dr + fi * di
        gbi_ref[...] = fr * di - fi * dr
        gf_ref[0] = _colsum(dr * br + di * bi)
        gf_ref[1] = _colsum(di * br - dr * bi)

    hp = pl.BlockSpec((SSM_GROUP, nst), lambda b: (0, b))
    hp_sds = _sds((SSM_GROUP, nb * nst), F32)
    return pl.pallas_call(
        _with_after(body, 11, after), name="s5_bwd", grid=(nb,),
        in_specs=[pl.BlockSpec((S, cw), lambda b: (0, b)),
                  pl.BlockSpec((S, 2 * nst), lambda b: (0, b)),
                  pl.BlockSpec((S, cw), lambda b: (0, b)),
                  pl.BlockSpec((S, cw), lambda b: (0, b))] + _s5_param_specs(cw, nst)
        + [ANY] * len(after),
        out_specs=[pl.BlockSpec((S, cw), lambda b: (0, b)), hp, hp, hp, hp,
                   pl.BlockSpec((2, 1, nst), lambda b: (0, 0, b)),
                   pl.BlockSpec((1, cw), lambda b: (0, b)),
                   pl.BlockSpec((1, 2, nst), lambda b: (b, 0, 0))],
        out_shape=[_sds((S, nb * cw), BF16), hp_sds, hp_sds, hp_sds, hp_sds,
                   _sds((2, 1, nb * nst), F32), _sds((1, nb * cw), F32), _sds((nb, 2, nst), F32)],
        scratch_shapes=[pltpu.VMEM((2 * nst // LANES, S, LANES), F32)] * 2,
        compiler_params=_params(("arbitrary",)),
    )(proj, xsb_all, dz, zp, *params, *after)


def _shift_rows(v, k, row, down):
    n = v.shape[0]
    if down:
        return jnp.where(row >= k, pltpu.roll(v, k, 0), 0.0)
    return jnp.where(row < n - k, pltpu.roll(v, n - k, 0), 0.0)


def _window(v, gi, row, down):
    sums = []
    s = v
    for k in (1, 2, 4, 8):
        s = s + _shift_rows(s, k, row, down)
        sums.append(s)
    out = sums[3]
    for n in (2, 1, 0):
        out = jnp.where(gi == n, sums[n], out)
    return out


def pool_fwd(proj, col0, width, gw):
    S = proj.shape[0]
    cb0 = col0 // gw

    def body(u_ref, o_ref):
        gi = pl.program_id(0)
        u = u_ref[...]
        row = lax.broadcasted_iota(jnp.int32, u.shape, 0)
        w = jnp.left_shift(2, gi)
        count = jnp.minimum(row + 1, w).astype(F32)
        o_ref[...] = (_window(u, gi, row, True) / count - u).astype(BF16)

    return pl.pallas_call(
        body, name="pool_fwd", grid=(len(POOL_WINDOWS),),
        in_specs=[pl.BlockSpec((S, gw), lambda g: (0, cb0 + g))],
        out_specs=pl.BlockSpec((S, gw), lambda g: (0, g)),
        out_shape=_sds((S, width), BF16), compiler_params=_params(("arbitrary",)),
    )(proj)


def pool_bwd(dpooled, gw):
    S, width = dpooled.shape

    def body(d_ref, o_ref):
        gi = pl.program_id(0)
        d = d_ref[...]
        row = lax.broadcasted_iota(jnp.int32, d.shape, 0)
        w = jnp.left_shift(2, gi)
        count = jnp.minimum(row + 1, w).astype(F32)
        o_ref[...] = (_window(d / count, gi, row, False) - d).astype(BF16)

    return pl.pallas_call(
        body, name="pool_bwd", grid=(len(POOL_WINDOWS),),
        in_specs=[pl.BlockSpec((S, gw), lambda g: (0, g))],
        out_specs=pl.BlockSpec((S, gw), lambda g: (0, g)),
        out_shape=_sds((S, width), BF16), compiler_params=_params(("arbitrary",)),
    )(dpooled)


def _place():
    x, y, c = lax.axis_index("x"), lax.axis_index("y"), lax.axis_index("c")
    chips = [(1 - x, y), (x, 1 - y), (1 - x, 1 - y)]
    return x, y, c, chips


HBM = pl.BlockSpec(memory_space=pltpu.HBM)


GATHER_PIECES = 2
GATHER_SEMS = 1 + 12 * GATHER_PIECES


def _routed_gather_body(n):
    npc = GATHER_PIECES
    k_x, k_y = 1, 1 + 2 * npc
    k_xy, k_yx, k_sib = 1 + 4 * npc, 1 + 5 * npc, 1 + 6 * npc

    def body(*refs):
        ins, outs = refs[:n], refs[n:2 * n]
        send_sems, recv_sems, local_sems = refs[2 * n:]
        x, y, c, (xn, yn, dg) = _place()
        me, sibling = (x, y, c), (x, y, 1 - c)
        barrier = pltpu.get_barrier_semaphore()
        for peer in (sibling, (*xn, c), (*yn, c)):
            pl.semaphore_signal(barrier, inc=1, device_id=peer, device_id_type=MESH)
        pl.semaphore_wait(barrier, 3)

        def piece(i, dev, p):
            rows = ins[i].shape[0] // (2 * npc)
            return outs[i].at[4 * dev[0] + 2 * dev[1] + dev[2], pl.ds(p * rows, rows)]

        def copy(i, k, src, dst, to):
            return pltpu.make_async_remote_copy(src_ref=src, dst_ref=dst, send_sem=send_sems.at[i, k],
                                                recv_sem=recv_sems.at[i, k], device_id=to,
                                                device_id_type=MESH)

        started = []

        def go(cp):
            cp.start()
            started.append(cp)

        for i in range(n):
            rows = ins[i].shape[0] // (2 * npc)
            for q in range(2 * npc):
                py = (q + npc) % (2 * npc)
                go(copy(i, k_x + q, ins[i].at[pl.ds(q * rows, rows)], piece(i, me, q), (*xn, c)))
                go(copy(i, k_y + py, ins[i].at[pl.ds(py * rows, rows)], piece(i, me, py), (*yn, c)))
        for i in range(n):
            go(copy(i, 0, ins[i], outs[i].at[4 * x + 2 * y + c], sibling))
        mine = [pltpu.make_async_copy(ins[i], outs[i].at[4 * x + 2 * y + c], local_sems.at[i])
                for i in range(n)]
        for cp in mine:
            cp.start()

        def arrived(i, k, chip, p, onward, r):
            got = piece(i, (*chip, c), p)
            copy(i, k, got, got, me).wait_recv()
            if onward is not None:
                go(copy(i, onward[0], got, got, (*onward[1], c)))
            go(copy(i, k_sib + r, got, got, sibling))

        for i in range(n):
            for q in range(npc):
                arrived(i, k_x + q, xn, q, (k_xy + q, yn), q)
                arrived(i, k_y + npc + q, yn, npc + q, (k_yx + q, xn), 2 * npc + npc + q)
            for q in range(npc):
                arrived(i, k_x + npc + q, xn, npc + q, None, npc + q)
                arrived(i, k_y + q, yn, q, None, 2 * npc + q)
            for q in range(npc):
                arrived(i, k_xy + q, dg, q, None, 4 * npc + q)
                arrived(i, k_yx + q, dg, npc + q, None, 4 * npc + npc + q)
        for i in range(n):
            block = outs[i].at[4 * x + 2 * y + 1 - c]
            copy(i, 0, block, block, me).wait_recv()
            for j, chip in enumerate((xn, yn, dg)):
                for p in range(2 * npc):
                    got = piece(i, (*chip, 1 - c), p)
                    copy(i, k_sib + 2 * npc * j + p, got, got, me).wait_recv()
        for cp in started:
            cp.wait_send()
        for cp in mine:
            cp.wait()

    return body


def _on_sequencer(name, body, arrays, out_sds, sems, collective_id):
    ins = [jax.new_ref(a, memory_space=pltpu.MemorySpace.HBM) for a in arrays]
    outs = [jax.empty_ref(s, memory_space=pltpu.MemorySpace.HBM) for s in out_sds]

    @pl.kernel(mesh=plsc.ScalarSubcoreMesh(axis_name="sequencer", num_cores=1), name=name,
               scratch_types=tuple(sems),
               compiler_params=pltpu.CompilerParams(collective_id=collective_id))
    def launch(*sem_refs):
        body(*ins, *outs, *sem_refs)

    launch()
    return [o[...] for o in outs]


def seq_all_gather(name, shards, collective_id):
    n = len(shards)
    return _on_sequencer(
        name, _routed_gather_body(n), shards, [_sds((NDEV,) + s.shape, s.dtype) for s in shards],
        [pltpu.SemaphoreType.DMA((n, GATHER_SEMS)), pltpu.SemaphoreType.DMA((n, GATHER_SEMS)),
         pltpu.SemaphoreType.DMA((n,))], collective_id)


def pair_exchange(name, grads, collective_id):
    def plan(srcs, lands):
        x, y, c, _ = _place()
        return ([(i, q, srcs[i].at[2 * q + 1 - c], lands[i].at[q], (x, y, 1 - c))
                 for i in range(len(srcs)) for q in range(NCHIP)], [(x, y, 1 - c)])

    return _split_exchange(name, grads, [_sds((NCHIP,) + g.shape[1:], g.dtype) for g in grads],
                           plan, NCHIP, collective_id)


SEM = pl.BlockSpec(memory_space=pltpu.SEMAPHORE)


def _split_exchange(name, srcs, land_sds, plan, ncopy, collective_id):
    n = len(srcs)
    nsem = n * ncopy
    effect = pltpu.SideEffectType.DATAFLOW_SIDE_EFFECTING

    def descriptors(src_refs, land_refs, send_sems, recv_sems):
        copies, peers = plan(src_refs, land_refs)
        return [pltpu.make_async_remote_copy(src_ref=s, dst_ref=d, send_sem=send_sems[i * ncopy + k],
                                             recv_sem=recv_sems[i * ncopy + k], device_id=to,
                                             device_id_type=MESH) for (i, k, s, d, to) in copies], peers

    def start_body(*refs):
        src_refs, land_refs = refs[:n], refs[n:2 * n]
        send_sems, recv_sems = refs[2 * n:2 * n + nsem], refs[2 * n + nsem:2 * n + 2 * nsem]
        token = refs[-1]
        cps, peers = descriptors(src_refs, land_refs, send_sems, recv_sems)
        barrier = pltpu.get_barrier_semaphore()
        for peer in peers:
            pl.semaphore_signal(barrier, inc=1, device_id=peer, device_id_type=MESH)
        pl.semaphore_wait(barrier, len(peers))
        for cp in cps:
            cp.start()
        token[...] = jnp.zeros_like(token)

    lands = [pltpu.with_memory_space_constraint(lax.empty(s.shape, s.dtype), pltpu.HBM) for s in land_sds]
    srcs = [pltpu.with_memory_space_constraint(s, pltpu.HBM) for s in srcs]
    res = pl.pallas_call(
        start_body, name=name + "_start",
        out_shape=(pltpu.SemaphoreType.DMA(()),) * (2 * nsem)
        + tuple(pltpu.HBM(s.shape, s.dtype) for s in srcs)
        + tuple(pltpu.HBM(s.shape, s.dtype) for s in land_sds) + (_sds((SUBLANES, LANES), F32),),
        in_specs=[HBM] * (2 * n),
        out_specs=(SEM,) * (2 * nsem) + (HBM,) * (2 * n) + (pl.BlockSpec(memory_space=pltpu.VMEM),),
        input_output_aliases={i: 2 * nsem + i for i in range(2 * n)},
        compiler_params=pltpu.CompilerParams(has_side_effects=effect, collective_id=collective_id),
    )(*srcs, *lands)
    sems = res[:2 * nsem]
    thru = res[2 * nsem:2 * nsem + 2 * n]
    token = res[-1]

    def wait(after):
        def wait_body(*refs):
            src_refs, land_refs = refs[:n], refs[n:2 * n]
            cps, _ = descriptors(src_refs, land_refs, refs[2 * n:2 * n + nsem],
                                 refs[2 * n + nsem:2 * n + 2 * nsem])
            for cp in cps:
                cp.wait_send()
            for cp in cps:
                cp.wait_recv()

        out = pl.pallas_call(
            wait_body, name=name + "_wait",
            out_shape=tuple(pltpu.HBM(s.shape, s.dtype) for s in srcs)
            + tuple(pltpu.HBM(s.shape, s.dtype) for s in land_sds),
            in_specs=[HBM] * (2 * n) + [SEM] * (2 * nsem) + [pl.BlockSpec(memory_space=pl.ANY)],
            out_specs=(HBM,) * (2 * n),
            input_output_aliases={i: i for i in range(2 * n)},
            compiler_params=pltpu.CompilerParams(has_side_effects=effect),
        )(*thru, *sems, after)
        return list(out[:n]), list(out[n:])

    return token, wait


def pair_sum(name, grad, got, place):
    shp = grad.shape[1:]
    r, cdim = shp[-2], shp[-1]
    lead = int(math.prod(shp[:-2])) if len(shp) > 2 else 1
    g5 = grad.reshape(NCHIP, 2, lead * r, cdim)
    t4 = got.reshape(NCHIP, lead * r, cdim)
    R = lead * r
    tr = _tile(R, max(8, (1 << 20) // cdim))

    def body(p_ref, g_ref, t_ref, o_ref):
        o_ref[...] = (g_ref[0].astype(F32) + t_ref[...].astype(F32)).astype(o_ref.dtype)

    out = pl.pallas_call(
        body, name=name,
        grid_spec=pltpu.PrefetchScalarGridSpec(
            num_scalar_prefetch=1, grid=(NCHIP - 1, R // tr),
            in_specs=[pl.BlockSpec((1, 1, tr, cdim), lambda j, i, p: (p[1] ^ (j + 1), p[0], i, 0)),
                      pl.BlockSpec((1, tr, cdim), lambda j, i, p: (p[1] ^ (j + 1), i, 0))],
            out_specs=pl.BlockSpec((1, tr, cdim), lambda j, i, p: (p[1] ^ (j + 1), i, 0))),
        out_shape=_sds((NCHIP, R, cdim), grad.dtype),
        compiler_params=_params(("parallel", "parallel")),
    )(place, g5, t4)
    return out


def chip_exchange(name, parts, collective_id):
    def plan(srcs, lands):
        x, y, c, chips = _place()
        return ([(i, j, srcs[i].at[2 * chip[0] + chip[1]], lands[i].at[j], (*chip, c))
                 for i in range(len(srcs)) for j, chip in enumerate(chips)],
                [(*chip, c) for chip in chips])

    return _split_exchange(name, parts, [_sds((3,) + p.shape[1:], p.dtype) for p in parts],
                           plan, 3, collective_id)


def ada_fwd(c_row, w_ada, b_ada):
    D, cols = w_ada.shape

    def body(c_ref, w_ref, b_ref, mod_ref, call_ref, act8, part, s1, r1, s2, r2):
        x, y, c, _ = _place()
        me = 4 * x + 2 * y + c
        call_ref[me] = c_ref[...]
        cps = []
        for k in range(1, NDEV):
            to = (x ^ (k >> 2), y ^ ((k >> 1) & 1), c ^ (k & 1))
            cps.append(pltpu.make_async_remote_copy(
                src_ref=c_ref, dst_ref=call_ref.at[me], send_sem=s1.at[k - 1],
                recv_sem=r1.at[k - 1], device_id=to, device_id_type=MESH))
            cps[-1].start()
        for cp in cps:
            cp.wait()
        for b in range(NDEV):
            act8[b:b + 1, :] = call_ref[b]
        cv = act8[...]
        act = (cv * _sigmoid(cv)).astype(BF16)
        res = jnp.dot(act, w_ref[...].astype(BF16), preferred_element_type=F32)
        for b in range(NDEV):
            part[b] = res[b:b + 1, :]
        mod_ref[me] = part[me]
        cps = []
        for k in range(1, NDEV):
            to = (x ^ (k >> 2), y ^ ((k >> 1) & 1), c ^ (k & 1))
            dst = 4 * to[0] + 2 * to[1] + to[2]
            cps.append(pltpu.make_async_remote_copy(
                src_ref=part.at[dst], dst_ref=mod_ref.at[me], send_sem=s2.at[k - 1],
                recv_sem=r2.at[k - 1], device_id=to, device_id_type=MESH))
            cps[-1].start()
        for cp in cps:
            cp.wait()
        for b in range(NDEV):
            mod_ref[b] = mod_ref[b] + b_ref[b]

    vm = pl.BlockSpec(memory_space=pltpu.VMEM)
    return pl.pallas_call(
        body, name="ada_fwd", in_specs=[vm, vm, vm], out_specs=[vm, vm],
        out_shape=[_sds((NDEV, 1, cols), F32), _sds((NDEV, 1, D), F32)],
        scratch_shapes=[pltpu.VMEM((NDEV, D), F32), pltpu.VMEM((NDEV, 1, cols), F32),
                        pltpu.SemaphoreType.DMA((NDEV - 1,)), pltpu.SemaphoreType.DMA((NDEV - 1,)),
                        pltpu.SemaphoreType.DMA((NDEV - 1,)), pltpu.SemaphoreType.DMA((NDEV - 1,))],
        compiler_params=pltpu.CompilerParams(vmem_limit_bytes=VMEM_LIMIT),
    )(c_row, w_ada, b_ada.reshape(NDEV, 1, cols))


def _adamw_math(g, w, m, v):
    m2 = ADAM_B1 * m + (1.0 - ADAM_B1) * g
    v2 = ADAM_B2 * v + (1.0 - ADAM_B2) * (g * g)
    m_hat = m2 / (1.0 - ADAM_B1 ** ADAM_STEP)
    v_hat = v2 / (1.0 - ADAM_B2 ** ADAM_STEP)
    delta = -ADAM_LR * (m_hat / (jnp.sqrt(v_hat) + ADAM_EPS) + ADAM_WD * w)
    return delta, m2, v2


def adamw_sharded(name, grad8, pair4, got3, w, m, v, place, after=()):
    shape = w.shape
    cdim = shape[-1]
    R = int(math.prod(shape[:-1]))
    w2, m2, v2 = (t.reshape(R, cdim) for t in (w, m, v))
    tr = _tile(R, max(8, (1 << 19) // cdim))

    def body(q_ref, own_ref, sib_ref, t_ref, w_ref, m_ref, v_ref, g_out, d_out, m_out, v_out):
        g = own_ref[0].astype(F32) + sib_ref[0].astype(F32)
        for j in range(3):
            g = g + t_ref[j].astype(F32)
        d, mn, vn = _adamw_math(g, w_ref[...], m_ref[...], v_ref[...])
        g_out[...] = g
        d_out[...] = d
        m_out[...] = mn
        v_out[...] = vn

    spec = pl.BlockSpec((tr, cdim), lambda i, qr: (i, 0))
    outs = pl.pallas_call(
        _with_after(body, 7, after), name=name,
        grid_spec=pltpu.PrefetchScalarGridSpec(
            num_scalar_prefetch=1, grid=(R // tr,),
            in_specs=[pl.BlockSpec((1, tr, cdim), lambda i, qr: (qr[2], i, 0)),
                      pl.BlockSpec((1, tr, cdim), lambda i, qr: (qr[1], i, 0)),
                      pl.BlockSpec((3, tr, cdim), lambda i, qr: (0, i, 0)), spec, spec, spec]
            + [ANY] * len(after),
            out_specs=[spec] * 4),
        out_shape=[_sds((R, cdim), F32)] * 4,
        compiler_params=_params(("parallel",)),
    )(place, grad8.reshape(NDEV, R, cdim), pair4.reshape(NCHIP, R, cdim),
      got3.reshape(3, R, cdim), w2, m2, v2, *after)
    return [o.reshape(shape) for o in outs]


def sum_small(parts, after=()):
    R = parts.shape[1]

    def body(p_ref, g_out):
        g = p_ref[0]
        for j in range(1, NDEV):
            g = g + p_ref[j]
        g_out[...] = g

    return pl.pallas_call(
        _with_after(body, 1, after), name="sum_small", grid=(1,),
        in_specs=[pl.BlockSpec((NDEV, R, LANES), lambda i: (0, 0, 0))] + [ANY] * len(after),
        out_specs=pl.BlockSpec((R, LANES), lambda i: (0, 0)), out_shape=_sds((R, LANES), F32),
        compiler_params=_params(("arbitrary",)),
    )(parts, *after)


def adamw_natural(gs, ws, ms, vs):
    n = len(ws)
    nblk = 8
    big = [w.ndim == 4 and w.shape[1] % nblk == 0 for w in ws]

    def spec(w, is_big):
        if is_big:
            return pl.BlockSpec((1, w.shape[1] // nblk) + w.shape[2:], lambda i: (0, i, 0, 0))
        return pl.BlockSpec(w.shape, functools.partial(lambda i, nd: (0,) * nd, nd=w.ndim))

    def body(*refs):
        g_refs, w_refs, m_refs, v_refs = (refs[k * n:(k + 1) * n] for k in range(4))
        d_outs, m_outs, v_outs = (refs[(4 + k) * n:(5 + k) * n] for k in range(3))

        def update(p):
            d, mn, vn = _adamw_math(g_refs[p][...], w_refs[p][...], m_refs[p][...], v_refs[p][...])
            d_outs[p][...] = d
            m_outs[p][...] = mn
            v_outs[p][...] = vn

        for p in range(n):
            if big[p]:
                update(p)

        @pl.when(pl.program_id(0) == 0)
        def _():
            for p in range(n):
                if not big[p]:
                    update(p)

    specs = [spec(w, b) for w, b in zip(ws, big)]
    outs = pl.pallas_call(
        body, name="adamw_natural", grid=(nblk,), in_specs=specs * 4, out_specs=specs * 3,
        out_shape=[_sds(w.shape, F32) for w in ws] * 3,
        compiler_params=_params(("arbitrary",)),
    )(*gs, *ws, *ms, *vs)
    return outs[:n], outs[n:2 * n], outs[2 * n:]


def adamw_ada(c_all_t, dmod_all, w, m, v, my_dev):
    D, cols = w.shape
    tr = _tile(D, 256)

    def body(k_ref, c_ref, d_ref, w_ref, m_ref, v_ref, g_out, d_out, m_out, v_out):
        cv = c_ref[...]
        act = cv * _sigmoid(cv)
        g = jnp.dot(act.astype(BF16), d_ref[...].astype(BF16), preferred_element_type=F32)
        d, mn, vn = _adamw_math(g, w_ref[...], m_ref[...], v_ref[...])
        g_out[...] = g
        d_out[...] = d
        m_out[...] = mn
        v_out[...] = vn

    spec = pl.BlockSpec((tr, cols), lambda i, kr: (i, 0))
    return pl.pallas_call(
        body, name="adamw_ada",
        grid_spec=pltpu.PrefetchScalarGridSpec(
            num_scalar_prefetch=1, grid=(D // tr,),
            in_specs=[pl.BlockSpec((tr, NDEV), lambda i, kr: (i, 0)),
                      pl.BlockSpec((NDEV, cols), lambda i, kr: (0, kr[0])), spec, spec, spec],
            out_specs=[spec] * 4),
        out_shape=[_sds((D, cols), F32)] * 4,
        compiler_params=_params(("parallel",)),
    )(my_dev, c_all_t, dmod_all, w, m, v)


def _small_pack(parts):
    rows = []
    for p in parts:
        flat = p.reshape(-1)
        flat = jnp.pad(flat, (0, (-flat.shape[0]) % (SUBLANES * LANES)))
        rows.append(flat.reshape(-1, LANES))
    return jnp.concatenate(rows, axis=0)


def _small_unpack(buf, shapes):
    out, r = [], 0
    for s in shapes:
        n = int(math.prod(s))
        nr = -(-n // (SUBLANES * LANES)) * SUBLANES
        out.append(buf[r:r + nr].reshape(-1)[:n].reshape(s))
        r += nr
    return out


def kernel(x, c, w_ada, b_ada, w_in, lam_re, lam_im, log_dt, ssm_b_re, ssm_b_im, ssm_c_re, ssm_c_im, ssm_d, w_glu_val, w_glu_gate, w_pool, pool_scale, w_pool_out, w_out, ln1_g, ln1_b, w_ff1, w_ff2, ln2_g, ln2_b, loss_target, m_w_ada, m_b_ada, m_w_in, m_lam_re, m_lam_im, m_log_dt, m_ssm_b_re, m_ssm_b_im, m_ssm_c_re, m_ssm_c_im, m_ssm_d, m_w_glu_val, m_w_glu_gate, m_w_pool, m_pool_scale, m_w_pool_out, m_w_out, m_ln1_g, m_ln1_b, m_w_ff1, m_w_ff2, m_ln2_g, m_ln2_b, v_w_ada, v_b_ada, v_w_in, v_lam_re, v_lam_im, v_log_dt, v_ssm_b_re, v_ssm_b_im, v_ssm_c_re, v_ssm_c_im, v_ssm_d, v_w_glu_val, v_w_glu_gate, v_w_pool, v_pool_scale, v_w_pool_out, v_w_out, v_ln1_g, v_ln1_b, v_w_ff1, v_w_ff2, v_ln2_g, v_ln2_b):
    S, D = x.shape[1], x.shape[2]
    x2d, tgt = x[0], loss_target[0]
    W = D // 2
    G = W // SSM_GROUP
    P, H, GPB = SSM_STATE, SSM_GROUP, GROUPS_PER_BLOCK
    nblk = G // GPB
    gw = W // len(POOL_WINDOWS)
    ax, ay, ac = lax.axis_index("x"), lax.axis_index("y"), lax.axis_index("c")
    my_dev = (4 * ax + 2 * ay + ac).astype(jnp.int32).reshape(1)
    place = jnp.stack([ac, 2 * ax + ay, 4 * ax + 2 * ay + ac]).astype(jnp.int32)
    ts = _tile(S, 256)

    glu = jnp.concatenate([w_glu_val[0], w_glu_gate[0]]).astype(BF16)
    shards = [w_in[0].astype(BF16), glu, w_pool[0].astype(BF16), w_pool_out[0].astype(BF16),
              w_out[0].astype(BF16), w_ff1[0].astype(BF16), w_ff2[0].astype(BF16)]
    wg_in, wg_pool = seq_all_gather("gather_w_in", [shards[0], shards[2]], 1)
    wg_vg, wg_po, wg_out = seq_all_gather("gather_w_mix", [shards[1], shards[3], shards[4]], 2)
    (wg_ff1,) = seq_all_gather("gather_w_ff1", shards[5:6], 3)
    (wg_ff2,) = seq_all_gather("gather_w_ff2", shards[6:7], 11)
    wg_vg = wg_vg.reshape(2 * NDEV, W, D // NDEV)
    nwin = len(POOL_WINDOWS)
    wp_full = jnp.transpose(wg_pool, (1, 0, 2, 3)).reshape(nwin, gw, gw)
    wout_full = wg_out.reshape(1, D, D)
    wff2_full = wg_ff2.reshape(1, 4 * D, D)

    small_names = [b_ada, lam_re, lam_im, log_dt, ssm_b_re, ssm_b_im, ssm_c_re, ssm_c_im, ssm_d,
                   pool_scale, ln1_g, ln1_b, ln2_g, ln2_b]
    small_m = [m_b_ada, m_lam_re, m_lam_im, m_log_dt, m_ssm_b_re, m_ssm_b_im, m_ssm_c_re, m_ssm_c_im,
               m_ssm_d, m_pool_scale, m_ln1_g, m_ln1_b, m_ln2_g, m_ln2_b]
    small_v = [v_b_ada, v_lam_re, v_lam_im, v_log_dt, v_ssm_b_re, v_ssm_b_im, v_ssm_c_re, v_ssm_c_im,
               v_ssm_d, v_pool_scale, v_ln1_g, v_ln1_b, v_ln2_g, v_ln2_b]

    mod, c_all = ada_fwd(c, w_ada[0], b_ada)
    mod = mod.reshape(6, 1, D)
    sh1, sc1, g1, sh2, sc2, g2 = (mod[i] for i in range(6))

    f2, kconst = s5_disc(lam_re[0], lam_im[0], log_dt[0].reshape(G, 1))
    kconst = kconst.reshape(NCONST, SUBLANES, G * P)
    f2r = f2.reshape(2, 1, G * P)
    bt_re = jnp.transpose(ssm_b_re[0], (2, 0, 1)).reshape(H, G * P)
    bt_im = jnp.transpose(ssm_b_im[0], (2, 0, 1)).reshape(H, G * P)
    ct_re = jnp.transpose(ssm_c_re[0], (1, 0, 2)).reshape(H, G * P)
    ct_im = jnp.transpose(ssm_c_im[0], (1, 0, 2)).reshape(H, G * P)
    s5_params = (f2r, bt_re, bt_im, ct_re, ct_im, ssm_d, kconst)

    def e1(t, b):
        xhat, _ = _ln_stats(t[0])
        return [xhat * (1.0 + b[0]) + b[1]], []
    (h1,) = _rowwise("ln_mod1", e1, S, ts, [(x2d, D, 0)], [sc1, sh1], [(D, BF16)], [])

    (proj,) = mm_nn("proj", h1, wg_in, F32, 2)
    z, xsb_all, zp = s5_fwd(proj, s5_params, nblk)
    (vt,) = mm_nn("glu", z, wg_vg, BF16, 4)
    pooled = pool_fwd(proj, W, W, gw)

    def pool_epi(vals, ex, outs):
        a = vals[0]
        outs[0][...] = a
        outs[1][...] = (a * ex[0][...]).astype(BF16)
    tmp = _tile(S, 1024)
    yp, ypool = _mm(
        "pool_mix", "nn", pooled, wp_full.astype(BF16), (S // tmp, nwin, 1),
        pl.BlockSpec((tmp, gw), lambda i, j, k: (i, j)), pl.BlockSpec((1, gw, gw), lambda i, j, k: (j, 0, 0)),
        [(_sds((S, W), F32), pl.BlockSpec((tmp, gw), lambda i, j, k: (i, j))),
         (_sds((S, W), BF16), pl.BlockSpec((tmp, gw), lambda i, j, k: (i, j)))],
        (tmp, gw), 1, gw, None, pool_epi,
        [(pool_scale, pl.BlockSpec((1, gw), lambda i, j, k: (0, j)))])
    (y_b,) = mm_nn("pool_out", ypool, wg_po, BF16, 4)

    cb = D // NDEV
    ga_cb, gb_cb = (2 * W) // cb, (2 * W + D) // cb
    mcb = 4
    wm = mcb * cb
    tsm = _tile(S, 256)

    def merge_call(name, fn, ins, n_out, after=()):
        def body(*refs):
            vals = [r[...].astype(F32) for r in refs[:len(ins)]]
            for r, v in zip(refs[len(ins):], fn(*vals)):
                r[...] = v.astype(r.dtype)
        return pl.pallas_call(
            _with_after(body, len(ins), after), name=name, grid=(S // tsm, NDEV // mcb),
            in_specs=[pl.BlockSpec((tsm, w), f) for (_, w, f) in ins] + [ANY] * len(after),
            out_specs=[pl.BlockSpec((tsm, w), lambda i, j: (i, j)) for (_, w) in n_out],
            out_shape=[_sds((S, cols), BF16) for (cols, _) in n_out],
            compiler_params=_params(("parallel", "parallel")),
        )(*[a for (a, _, _) in ins], *after)

    merge_ins = [(proj, wm, lambda i, j: (i, ga_cb // mcb + j)), (proj, wm, lambda i, j: (i, gb_cb // mcb + j)),
                 (vt, 2 * wm, lambda i, j: (i, j)), (y_b, wm, lambda i, j: (i, j))]

    def val_gate(vtv):
        return (jnp.concatenate([vtv[:, 2 * q * cb:(2 * q + 1) * cb] for q in range(mcb)], axis=1),
                jnp.concatenate([vtv[:, (2 * q + 1) * cb:(2 * q + 2) * cb] for q in range(mcb)], axis=1))

    def merge_f(ga, gb, vtv, yb):
        vv, tt = val_gate(vtv)
        return [_sigmoid(ga) * (vv * _sigmoid(tt)) + _sigmoid(gb) * yb]
    (merged,) = merge_call("merge", merge_f, merge_ins, [(D, wm)])

    (mix,) = mm_nn("mix_out", merged, wout_full, F32, 1)

    def e3(t, b):
        xv, mx = t
        g1v, l1g, l1b, sc2v, sh2v = b
        r1 = ALPHA * xv + g1v * mx
        xh1, _ = _ln_stats(r1)
        x1 = xh1 * l1g + l1b
        xh, _ = _ln_stats(x1)
        return [r1, xh * (1.0 + sc2v) + sh2v], []
    r1, h2 = _rowwise("post_mix", e3, S, ts, [(x2d, D, 0), (mix, D, 0)],
                      [g1, ln1_g, ln1_b, sc2, sh2], [(D, F32), (D, BF16)], [])

    def relu_epi(vals, ex, outs):
        outs[0][...] = jnp.maximum(vals[0], 0.0).astype(BF16)
    (rl,) = mm_nn("ff1", h2, wg_ff1, BF16, 1, epi=relu_epi)

    def square(a):
        return a * a
    (y2,) = mm_nn("ff2", rl, wff2_full, F32, 1, pro=square)

    def e4(t, b):
        r1v, y2v, tg = t
        g2v, l1g, l1b, l2g, l2b = b
        xh1, _ = _ln_stats(r1v)
        x1 = xh1 * l1g + l1b
        r2 = ALPHA * x1 + g2v * y2v
        xh2, rs2 = _ln_stats(r2)
        err = xh2 * l2g + l2b - tg
        dx2 = err * (1.0 / D)
        dr2 = _ln_bwd(dx2 * l2g, xh2, rs2)
        lsum = jnp.sum(_colsum(err * err), axis=1, keepdims=True) * (0.5 / D)
        return ([ALPHA * dr2, g2v * dr2],
                [jnp.broadcast_to(lsum, (1, LANES)), _colsum(dx2 * xh2), _colsum(dx2), _colsum(dr2 * y2v)])
    dx1a, dy2, loss_acc, g_ln2g, g_ln2b, d_g2 = _rowwise(
        "head", e4, S, ts, [(r1, D, 0), (y2, D, 0), (tgt, D, 0)], [g2, ln1_g, ln1_b, ln2_g, ln2_b],
        [(D, F32), (D, BF16)], [LANES, D, D, D])

    tn_ff = _tile(4 * D, 1024)

    def dff_epi(vals, ex, outs):
        outs[0][...] = (vals[0] * (2.0 * ex[0][...].astype(F32))).astype(BF16)
    tmf = _tile(S, 1024)
    (da1,) = mm_nt("d_ff2", dy2, wff2_full, BF16, 1, tn=tn_ff, epi=dff_epi,
                   extras=[(rl, pl.BlockSpec((tmf, tn_ff), lambda i, j, k: (i, j)))])
    gw_ff2 = mm_tn("gw_ff2", rl, dy2, BF16, NDEV, 0, pro=square)
    gw_ff1 = mm_tn("gw_ff1", h2, da1, BF16, NDEV, 1)
    tok, wait_pair_a = pair_exchange("pair_exchange_ff", [gw_ff2, gw_ff1], 4)
    (dh2,) = mm_nt("d_ff1", da1, wg_ff1, F32, 4, after=[tok])

    def e5(t, b):
        dh2v, r1v, dx1av, mx = t
        sc2v, l1g, l1b, g1v = b
        xh1, rs1 = _ln_stats(r1v)
        x1 = xh1 * l1g + l1b
        xh, rs = _ln_stats(x1)
        dx1 = dx1av + _ln_bwd(dh2v * (1.0 + sc2v), xh, rs)
        dr1 = _ln_bwd(dx1 * l1g, xh1, rs1)
        return ([ALPHA * dr1, g1v * dr1],
                [_colsum(dh2v * xh), _colsum(dh2v), _colsum(dx1 * xh1), _colsum(dx1), _colsum(dr1 * mx)])
    dxa, dmix, d_sc2, d_sh2, g_ln1g, g_ln1b, d_g1 = _rowwise(
        "post_mix_bwd", e5, S, ts, [(dh2, D, 0), (r1, D, 0), (dx1a, D, 0), (mix, D, 0)],
        [sc2, ln1_g, ln1_b, g1], [(D, F32), (D, BF16)], [D, D, D, D, D])

    (dmerged,) = mm_nt("d_mix_out", dmix, wout_full, BF16, 1)
    gw_out = mm_tn("gw_out", merged, dmix, BF16, NDEV, 0)
    grads_a, got_a = wait_pair_a(gw_out)
    parts_a = [pair_sum("pair_sum_ff%d" % i, g, t, place) for i, (g, t) in enumerate(zip(grads_a, got_a))]
    tok, wait_chip_a = chip_exchange("chip_exchange_ff", parts_a, 5)

    def merge_b(ga, gb, vtv, yb, dm):
        vv, tt = val_gate(vtv)
        sa, sb, st = _sigmoid(ga), _sigmoid(gb), _sigmoid(tt)
        dya = dm * sa
        dv, dt = dya * st, dya * vv * st * (1.0 - st)
        dvt_tile = jnp.concatenate([t[:, q * cb:(q + 1) * cb] for q in range(mcb) for t in (dv, dt)], axis=1)
        return [dm * (vv * st) * sa * (1.0 - sa), dm * yb * sb * (1.0 - sb), dvt_tile, dm * sb]
    dga, dgb_, dvt, dy_b = merge_call(
        "merge_bwd", merge_b, merge_ins + [(dmerged, wm, lambda i, j: (i, j))],
        [(D, wm), (D, wm), (2 * D, 2 * wm), (D, wm)], after=[tok])

    (dypool,) = mm_nt("d_pool_out", dy_b, wg_po, F32, NDEV)
    gw_po = mm_tn("gw_pool_out", ypool, dy_b, BF16, NDEV, 4)

    def e7(t, b):
        return [t[0] * b[0]], [_colsum(t[0] * t[1])]
    dyp, g_pscale = _rowwise("pool_scale_bwd", e7, S, ts, [(dypool, W, 0), (yp, W, 0)],
                             [pool_scale], [(W, BF16)], [W])
    (dpooled,) = _mm(
        "d_pool_mix", "nt", dyp, wp_full.astype(BF16), (S // tmp, nwin, 1),
        pl.BlockSpec((tmp, gw), lambda i, j, k: (i, j)), pl.BlockSpec((1, gw, gw), lambda i, j, k: (j, 0, 0)),
        [(_sds((S, W), F32), pl.BlockSpec((tmp, gw), lambda i, j, k: (i, j)))], (tmp, gw), 1, gw)
    tkp = _tile(S, 2048)
    gw_pool = _mm(
        "gw_pool", "tn", pooled, dyp, (nwin, 1, S // tkp),
        pl.BlockSpec((tkp, gw), lambda i, j, k: (k, i)), pl.BlockSpec((tkp, gw), lambda i, j, k: (k, i)),
        [(_sds((nwin, gw, gw), BF16), pl.BlockSpec((1, gw, gw), lambda i, j, k: (i, 0, 0)))],
        (gw, gw), 1, gw, stacked_out=True)[0]
    du_pool = pool_bwd(dpooled, gw)

    (dz,) = mm_nt("d_glu", dvt, wg_vg, BF16, 2 * NDEV)
    gw_vg = mm_tn("gw_glu", z, dvt, BF16, 2 * NDEV, 4)
    gw_pool_st = jnp.transpose(gw_pool.reshape(nwin, NDEV, gw // NDEV, gw), (1, 0, 2, 3))
    grads_b = [gw_out, gw_po, gw_pool_st, gw_vg.reshape(NDEV, 2, W, D // NDEV)]
    tok, wait_pair_b = pair_exchange("pair_exchange_mix", grads_b, 6)
    du_ssm, g_bt_re, g_bt_im, g_ct_re, g_ct_im, g_f, g_d, g_a = s5_bwd(
        proj, xsb_all, dz, zp, s5_params, nblk, after=[tok])
    grads_b, got_b = wait_pair_b(du_ssm)
    parts_b = [pair_sum("pair_sum_mix%d" % i, g, t, place) for i, (g, t) in enumerate(zip(grads_b, got_b))]
    tok, wait_chip_b = chip_exchange("chip_exchange_mix", parts_b, 7)

    dproj = jnp.concatenate([du_ssm, du_pool, dga, dgb_], axis=1)
    gw_in = mm_tn("gw_in", h1, dproj, BF16, NDEV, 1, after=[tok])
    tok, wait_pair_c = pair_exchange("pair_exchange_in", [gw_in], 8)
    (dh1,) = mm_nt("d_proj", dproj, wg_in, F32, 4, after=[tok])
    grads_c, got_c = wait_pair_c(dh1)
    parts_c = [pair_sum("pair_sum_in", grads_c[0], got_c[0], place)]
    tok, wait_chip_c = chip_exchange("chip_exchange_in", parts_c, 9)

    def e10(t, b):
        dh1v, xv, dxav = t
        xh, rs = _ln_stats(xv)
        return ([dxav + _ln_bwd(dh1v * (1.0 + b[0]), xh, rs)],
                [_colsum(dh1v * xh), _colsum(dh1v)])
    grad_x, d_sc1, d_sh1 = _rowwise("ln_mod1_bwd", e10, S, ts, [(dh1, D, 0), (x2d, D, 0), (dxa, D, 0)],
                                    [sc1], [(D, F32)], [D, D], after=[tok])

    g_b_re = jnp.transpose(g_bt_re.reshape(H, G, P), (1, 0, 2))
    g_b_im = jnp.transpose(g_bt_im.reshape(H, G, P), (1, 0, 2))
    g_c_re = jnp.transpose(g_ct_re.reshape(H, G, P), (1, 0, 2))
    g_c_im = jnp.transpose(g_ct_im.reshape(H, G, P), (1, 0, 2))
    d_ab = jnp.transpose(g_a.reshape(nblk, 2, GPB, P), (1, 0, 2, 3)).reshape(2, G, P)
    g_lr, g_li, g_ldt = s5_disc_bwd(lam_re[0], lam_im[0], log_dt[0].reshape(G, 1), d_ab,
                                    g_f.reshape(2, G, P))

    dmod = jnp.concatenate([d_sh1, d_sc1, d_g1, d_sh2, d_sc2, d_g2], axis=1)
    small_g = [dmod, g_lr, g_li, g_ldt, g_b_re, g_b_im, g_c_re, g_c_im, g_d, g_pscale,
               g_ln1g, g_ln1b, g_ln2g, g_ln2b, loss_acc]
    packed_g = _small_pack(small_g)
    (parts_all,) = seq_all_gather("gather_small", [packed_g], 10)
    glu_w = jnp.stack([w_glu_val[0], w_glu_gate[0]])
    glu_m = jnp.stack([m_w_glu_val[0], m_w_glu_gate[0]])
    glu_v = jnp.stack([v_w_glu_val[0], v_w_glu_gate[0]])
    wmv = [(w_ff2[0], m_w_ff2[0], v_w_ff2[0]), (w_ff1[0], m_w_ff1[0], v_w_ff1[0]),
           (w_out[0], m_w_out[0], v_w_out[0]), (w_pool_out[0], m_w_pool_out[0], v_w_pool_out[0]),
           (w_pool[0], m_w_pool[0], v_w_pool[0]), (glu_w, glu_m, glu_v)]
    _, got3_a = wait_chip_a(packed_g)
    upd = [adamw_sharded("adamw_%d" % i, g, p, t, w, m, v, place)
           for i, (g, p, t, (w, m, v)) in enumerate(zip(grads_a, got_a, got3_a, wmv[:2]))]
    _, got3_b = wait_chip_b(upd[-1][0])
    upd += [adamw_sharded("adamw_%d" % (2 + i), g, p, t, w, m, v, place)
            for i, (g, p, t, (w, m, v)) in enumerate(zip(grads_b, got_b, got3_b, wmv[2:]))]
    u_ff2, u_ff1, u_out, u_po, u_pool, u_glu = upd

    gsum = sum_small(parts_all, after=[upd[-1][0]])
    def swap_b(ts_):
        return [jnp.swapaxes(t, 2, 3) if i in (4, 5) else t for i, t in enumerate(ts_)]

    sg = _small_unpack(gsum, [t.shape for t in swap_b(small_names)] + [(1, LANES)])
    loss, sg = sg[-1][0, 0], sg[:-1]
    sd, sm, sv = adamw_natural(sg, swap_b(small_names), swap_b(small_m), swap_b(small_v))
    sg, sd, sm, sv = swap_b(sg), swap_b(sd), swap_b(sm), swap_b(sv)

    nmod = 6 * D
    dmod_all = parts_all[:, :nmod // LANES, :].reshape(NDEV, nmod)
    c_all_t = jnp.transpose(c_all.reshape(NDEV, D))
    ada_out = adamw_ada(c_all_t, dmod_all, w_ada[0], m_w_ada[0], v_w_ada[0], my_dev)
    _, got3_c = wait_chip_c(ada_out[0])
    u_in = adamw_sharded("adamw_6", grads_c[0], got_c[0], got3_c[0], w_in[0], m_w_in[0], v_w_in[0], place)

    def pick(k):
        return [ada_out[k][None], sg_sd[k][0], u_in[k][None]] + [t for t in sg_sd[k][1:9]] + \
               [u_glu[k][0][None], u_glu[k][1][None], u_pool[k][None], sg_sd[k][9], u_po[k][None],
                u_out[k][None], sg_sd[k][10], sg_sd[k][11], u_ff1[k][None], u_ff2[k][None],
                sg_sd[k][12], sg_sd[k][13]]

    sg_sd = [sg, sd, sm, sv]
    return (loss, grad_x[None], *pick(0), *pick(1), *pick(2), *pick(3))
```
